```python
import math
import jax, jax.numpy as jnp
from jax import lax
import numpy as np

D_MODEL = 1024
BATCH = 4
SEQ = 8192
DEPTH = 1

HEAD_DIM = 64
N_HEADS = D_MODEL // HEAD_DIM
D_MIX = N_HEADS * HEAD_DIM
DIL_HEADS = 6
NSA_HEADS = N_HEADS - DIL_HEADS
DIL_PATTERNS = ((128, 1), (512, 4), (2048, 16))
NSA_KV_HEADS = 2
NSA_GROUP = NSA_HEADS // NSA_KV_HEADS
CMP_BLOCK = 32
CMP_STRIDE = 16
CMP_HIDDEN = 256
SEL_BLOCK = 64
SEL_TOPK = 16
WIN = 512
FORCE_SCORE = 1.0e4
N_BUCKETS = 32
MAX_DISTANCE = 2048
N_GROUPS = 4
EXPERTS_PER_GROUP = 4
N_EXPERTS = N_GROUPS * EXPERTS_PER_GROUP
D_EXPERT = 512
TOPK_IN_GROUP = 2
Q_BLOCK = 128
EPS = 1e-6
N_IN = 3 * DIL_HEADS * HEAD_DIM + NSA_HEADS * HEAD_DIM + 6 * NSA_KV_HEADS * HEAD_DIM + 3 * NSA_HEADS

kernel_name = "hybrid_dilated_nsa_hiermoe"


def rmsnorm(x, g):
    xf = x.astype(jnp.float32)
    y = xf * lax.rsqrt(jnp.mean(xf * xf, axis=-1, keepdims=True) + EPS)
    return (y * g.astype(jnp.float32)).astype(x.dtype)


def t5_bucket(dist):
    dist = jnp.maximum(dist, 0)
    max_exact = N_BUCKETS // 2
    large = max_exact + (jnp.log(jnp.maximum(dist, 1).astype(jnp.float32) / max_exact)
                         / math.log(MAX_DISTANCE / max_exact) * (N_BUCKETS - max_exact)).astype(jnp.int32)
    large = jnp.minimum(large, N_BUCKETS - 1)
    return jnp.where(dist < max_exact, dist, large)


def masked_softmax(s, mask, axis):
    s = jnp.where(mask, s, -jnp.inf)
    m = jnp.max(s, axis=axis, keepdims=True)
    m = jnp.where(jnp.isfinite(m), m, 0.0)
    e = jnp.where(mask, jnp.exp(s - m), 0.0)
    den = jnp.sum(e, axis=axis, keepdims=True)
    p = e / jnp.maximum(den, 1e-30)
    return p, m + jnp.log(den)


def compress(tok, pos_emb, w1, w2, n_cmp):
    B = tok.shape[0]
    idx = jnp.arange(n_cmp)[:, None] * CMP_STRIDE + jnp.arange(CMP_BLOCK)[None, :]
    blocks = jnp.take(tok, idx, axis=1) + pos_emb[None, None, :, None, :]
    flat = blocks.transpose(0, 1, 3, 2, 4).reshape(B, n_cmp, NSA_KV_HEADS, CMP_BLOCK * HEAD_DIM)
    return jax.nn.gelu(flat @ w1) @ w2


def hybrid_mixer(xn, w_in, w_out, cmp_pos_k, cmp_pos_v, cmp_k_w1, cmp_k_w2, cmp_v_w1, cmp_v_w2, rel_bias):
    B, T, _ = xn.shape
    scale = 1.0 / math.sqrt(HEAD_DIM)
    proj = xn @ w_in
    sizes = [DIL_HEADS * HEAD_DIM] * 3 + [NSA_HEADS * HEAD_DIM] + [NSA_KV_HEADS * HEAD_DIM] * 6 + [3 * NSA_HEADS]
    parts = jnp.split(proj, np.cumsum(sizes)[:-1].tolist(), axis=-1)
    q_a, k_a, v_a = [p.reshape(B, T, DIL_HEADS, HEAD_DIM) for p in parts[:3]]
    q_b = parts[3].reshape(B, T, NSA_KV_HEADS, NSA_GROUP, HEAD_DIM)
    k_ct, v_ct, k_s, v_s, k_w, v_w = [p.reshape(B, T, NSA_KV_HEADS, HEAD_DIM) for p in parts[4:10]]
    gates = jax.nn.sigmoid(parts[10].astype(jnp.float32)).reshape(B, T, NSA_KV_HEADS, NSA_GROUP, 3)

    bias_a = rel_bias[:, :DIL_HEADS].T
    bias_b = rel_bias[:, DIL_HEADS:].T.reshape(NSA_KV_HEADS, NSA_GROUP, N_BUCKETS)

    n_cmp = (T - CMP_BLOCK) // CMP_STRIDE + 1
    k_cmp = compress(k_ct, cmp_pos_k, cmp_k_w1, cmp_k_w2, n_cmp)
    v_cmp = compress(v_ct, cmp_pos_v, cmp_v_w1, cmp_v_w2, n_cmp)
    cmp_start = jnp.arange(n_cmp) * CMP_STRIDE
    cmp_end = cmp_start + CMP_BLOCK - 1
    n_sel = T // SEL_BLOCK
    n_top = min(SEL_TOPK, n_sel)
    sel_start = jnp.arange(n_sel) * SEL_BLOCK
    overlap = jnp.clip(jnp.minimum(cmp_start[:, None] + CMP_BLOCK, sel_start[None, :] + SEL_BLOCK)
                       - jnp.maximum(cmp_start[:, None], sel_start[None, :]), 0).astype(jnp.float32) / CMP_BLOCK
    k_sel_blocks = k_s.reshape(B, n_sel, SEL_BLOCK, NSA_KV_HEADS, HEAD_DIM).transpose(0, 3, 1, 2, 4)
    v_sel_blocks = v_s.reshape(B, n_sel, SEL_BLOCK, NSA_KV_HEADS, HEAD_DIM).transpose(0, 3, 1, 2, 4)
    pad = ((0, 0), (WIN, 0), (0, 0), (0, 0))
    k_w_pad = jnp.pad(k_w, pad)
    v_w_pad = jnp.pad(v_w, pad)

    bi = jnp.arange(B)[:, None, None]
    hi = jnp.arange(NSA_KV_HEADS)[None, :, None]
    hi6 = jnp.arange(NSA_KV_HEADS)[None, :, None, None, None, None]
    gi6 = jnp.arange(NSA_GROUP)[None, None, :, None, None, None]

    def block_fn(b):
        t0 = b * Q_BLOCK
        t = t0 + jnp.arange(Q_BLOCK)
        qa = lax.dynamic_slice_in_dim(q_a, t0, Q_BLOCK, axis=1)
        outs, lses = [], []
        for (w, d) in DIL_PATTERNS:
            dist = d * jnp.arange(w // d + 1)
            idx = t[:, None] - dist[None, :]
            valid = idx >= 0
            idx = jnp.maximum(idx, 0)
            kg = jnp.take(k_a, idx, axis=1)
            vg = jnp.take(v_a, idx, axis=1)
            s = jnp.einsum('bqhd,bqkhd->bhqk', qa, kg, preferred_element_type=jnp.float32) * scale
            s = s + bias_a[:, t5_bucket(dist)][None, :, None, :].astype(jnp.float32)
            p, lse = masked_softmax(s, valid[None, None], -1)
            outs.append(jnp.einsum('bhqk,bqkhd->bqhd', p, vg))
            lses.append(lse)
        wts = jax.nn.softmax(jnp.stack(lses, axis=0), axis=0)
        o_a = sum(jnp.swapaxes(wts[i], 1, 2) * outs[i] for i in range(len(DIL_PATTERNS)))

        qb = lax.dynamic_slice_in_dim(q_b, t0, Q_BLOCK, axis=1)
        dist_c = t[:, None] - cmp_end[None, :]
        s_c = jnp.einsum('bqkgd,bnkd->bkgqn', qb, k_cmp, preferred_element_type=jnp.float32) * scale
        s_c = s_c + bias_b[:, :, t5_bucket(dist_c)][None].astype(jnp.float32)
        p_c, _ = masked_softmax(s_c, (dist_c >= 0)[None, None, None], -1)
        o_c = jnp.einsum('bkgqn,bnkd->bqkgd', p_c, v_cmp)
        imp = jnp.einsum('bkgqn,ns->bkqs', p_c, overlap)
        blk = jnp.arange(n_sel)[None, :]
        cur = (t // SEL_BLOCK)[:, None]
        forced = (blk == cur) | (blk == cur - 1) | (blk == 0)
        imp = jnp.where(forced, FORCE_SCORE, jnp.where(blk <= cur, imp, -1.0))
        _, sel = lax.top_k(imp, n_top)
        sel_flat = sel.reshape(B, NSA_KV_HEADS, Q_BLOCK * n_top)
        kg = k_sel_blocks[bi, hi, sel_flat].reshape(B, NSA_KV_HEADS, Q_BLOCK, n_top, SEL_BLOCK, HEAD_DIM)
        vg = v_sel_blocks[bi, hi, sel_flat].reshape(B, NSA_KV_HEADS, Q_BLOCK, n_top, SEL_BLOCK, HEAD_DIM)
        pos = sel[..., None] * SEL_BLOCK + jnp.arange(SEL_BLOCK)
        dist_s = t[None, None, :, None, None] - pos
        s_s = jnp.einsum('bqkgd,bkqnld->bkgqnl', qb, kg, preferred_element_type=jnp.float32) * scale
        s_s = s_s + bias_b[hi6, gi6, t5_bucket(dist_s)[:, :, None]].astype(jnp.float32)
        flat_shape = (B, NSA_KV_HEADS, NSA_GROUP, Q_BLOCK, n_top * SEL_BLOCK)
        mask_s = jnp.broadcast_to((dist_s >= 0)[:, :, None], s_s.shape).reshape(flat_shape)
        p_s, _ = masked_softmax(s_s.reshape(flat_shape), mask_s, -1)
        p_s = p_s.reshape(s_s.shape)
        o_s = jnp.einsum('bkgqnl,bkqnld->bqkgd', p_s, vg)
        kw = lax.dynamic_slice_in_dim(k_w_pad, t0, Q_BLOCK + WIN, axis=1)
        vw = lax.dynamic_slice_in_dim(v_w_pad, t0, Q_BLOCK + WIN, axis=1)
        s_pos = t0 - WIN + jnp.arange(Q_BLOCK + WIN)
        dist_w = t[:, None] - s_pos[None, :]
        mask_w = (dist_w >= 0) & (dist_w < WIN) & (s_pos[None, :] >= 0)
        s_w = jnp.einsum('bqkgd,bskd->bkgqs', qb, kw, preferred_element_type=jnp.float32) * scale
        s_w = s_w + bias_b[:, :, t5_bucket(dist_w)][None].astype(jnp.float32)
        p_w, _ = masked_softmax(s_w, mask_w[None, None, None], -1)
        o_w = jnp.einsum('bkgqs,bskd->bqkgd', p_w, vw)
        g = lax.dynamic_slice_in_dim(gates, t0, Q_BLOCK, axis=1)
        o_b = g[..., 0:1] * o_c + g[..., 1:2] * o_s + g[..., 2:3] * o_w
        return jnp.concatenate([o_a.reshape(B, Q_BLOCK, DIL_HEADS * HEAD_DIM),
                                o_b.reshape(B, Q_BLOCK, NSA_HEADS * HEAD_DIM)], axis=-1)

    ys = lax.map(block_fn, jnp.arange(T // Q_BLOCK))
    y = ys.transpose(1, 0, 2, 3).reshape(B, T, D_MIX).astype(xn.dtype)
    return (y @ w_out).astype(xn.dtype)


def hier_moe(h, w_router_group, b_router_group, w_router_expert, b_router_expert, w_gate, w_up, w_down):
    B, T, D = h.shape
    xf = h.reshape(B * T, D)
    g_logits = (xf @ w_router_group + b_router_group).astype(jnp.float32)
    g_idx = jnp.argmax(g_logits, axis=-1)
    g_prob = jnp.take_along_axis(jax.nn.softmax(g_logits, axis=-1), g_idx[:, None], axis=1)[:, 0]
    e_logits = (jnp.einsum('nd,dge->nge', xf, w_router_expert) + b_router_expert).astype(jnp.float32)
    e_logits = jnp.take_along_axis(e_logits, g_idx[:, None, None], axis=1)[:, 0]
    top_v, top_i = lax.top_k(e_logits, TOPK_IN_GROUP)
    e_prob = jax.nn.softmax(top_v, axis=-1)
    expert_id = g_idx[:, None] * EXPERTS_PER_GROUP + top_i
    combine = g_prob[:, None] * jnp.sum(e_prob[..., None] * jax.nn.one_hot(expert_id, N_EXPERTS, dtype=jnp.float32), axis=1)
    y = jnp.zeros(xf.shape, jnp.float32)
    for e in range(N_EXPERTS):
        he = jax.nn.silu(xf @ w_gate[e]) * (xf @ w_up[e])
        y = y + combine[:, e:e + 1] * (he @ w_down[e])
    return y.reshape(B, T, D).astype(h.dtype)


def setup_inputs(seed: int = 0) -> dict:
    key = jax.random.key(seed)
    ks = jax.random.split(key, 20)
    f32 = jnp.float32
    nrm = lambda k, shape, s: s * jax.random.normal(k, shape, f32)
    L = DEPTH
    return {
        "x": jax.random.normal(ks[0], (BATCH, SEQ, D_MODEL), f32),
        "rel_bias": nrm(ks[1], (N_BUCKETS, N_HEADS), 0.1),
        "norm_mix": 1.0 + nrm(ks[2], (L, D_MODEL), 0.02),
        "w_in": nrm(ks[3], (L, D_MODEL, N_IN), D_MODEL ** -0.5),
        "w_out": nrm(ks[4], (L, D_MIX, D_MODEL), D_MIX ** -0.5),
        "cmp_pos_k": nrm(ks[5], (L, CMP_BLOCK, HEAD_DIM), 0.1),
        "cmp_pos_v": nrm(ks[6], (L, CMP_BLOCK, HEAD_DIM), 0.1),
        "cmp_k_w1": nrm(ks[7], (L, CMP_BLOCK * HEAD_DIM, CMP_HIDDEN), (CMP_BLOCK * HEAD_DIM) ** -0.5),
        "cmp_k_w2": nrm(ks[8], (L, CMP_HIDDEN, HEAD_DIM), CMP_HIDDEN ** -0.5),
        "cmp_v_w1": nrm(ks[9], (L, CMP_BLOCK * HEAD_DIM, CMP_HIDDEN), (CMP_BLOCK * HEAD_DIM) ** -0.5),
        "cmp_v_w2": nrm(ks[10], (L, CMP_HIDDEN, HEAD_DIM), CMP_HIDDEN ** -0.5),
        "norm_ffn": 1.0 + nrm(ks[11], (L, D_MODEL), 0.02),
        "w_router_group": nrm(ks[12], (L, D_MODEL, N_GROUPS), D_MODEL ** -0.5),
        "b_router_group": nrm(ks[13], (L, N_GROUPS), 0.01),
        "w_router_expert": nrm(ks[14], (L, D_MODEL, N_GROUPS, EXPERTS_PER_GROUP), D_MODEL ** -0.5),
        "b_router_expert": nrm(ks[15], (L, N_GROUPS, EXPERTS_PER_GROUP), 0.01),
        "w_gate": nrm(ks[16], (L, N_EXPERTS, D_MODEL, D_EXPERT), D_MODEL ** -0.5),
        "w_up": nrm(ks[17], (L, N_EXPERTS, D_MODEL, D_EXPERT), D_MODEL ** -0.5),
        "w_down": nrm(ks[18], (L, N_EXPERTS, D_EXPERT, D_MODEL), D_EXPERT ** -0.5),
        "norm_final": 1.0 + nrm(ks[19], (D_MODEL,), 0.02),
    }


def reference(x, rel_bias, norm_mix, w_in, w_out, cmp_pos_k, cmp_pos_v, cmp_k_w1, cmp_k_w2,
              cmp_v_w1, cmp_v_w2, norm_ffn, w_router_group, b_router_group, w_router_expert,
              b_router_expert, w_gate, w_up, w_down, norm_final):
    h = x
    for l in range(DEPTH):
        h = h + hybrid_mixer(rmsnorm(h, norm_mix[l]), w_in[l], w_out[l], cmp_pos_k[l], cmp_pos_v[l],
                             cmp_k_w1[l], cmp_k_w2[l], cmp_v_w1[l], cmp_v_w2[l], rel_bias)
        h = h + hier_moe(rmsnorm(h, norm_ffn[l]), w_router_group[l], b_router_group[l],
                         w_router_expert[l], b_router_expert[l], w_gate[l], w_up[l], w_down[l])
    return rmsnorm(h, norm_final)
```

```python
import functools
import math

import jax
import jax.numpy as jnp
import numpy as np
from jax import lax
from jax.experimental import pallas as pl
from jax.experimental.pallas import tpu as pltpu

HEAD_DIM = 64
DIL_HEADS = 6
NSA_KV_HEADS = 2
NSA_GROUP = 5
NSA_HEADS = NSA_KV_HEADS * NSA_GROUP
N_HEADS = DIL_HEADS + NSA_HEADS
DIL_PATTERNS = ((128, 1), (512, 4), (2048, 16))
CMP_BLOCK = 32
CMP_STRIDE = 16
CMP_HIDDEN = 256
SEL_BLOCK = 64
SEL_TOPK = 16
WIN = 512
FORCE_SCORE = 1.0e4
N_BUCKETS = 32
MAX_DISTANCE = 2048
N_GROUPS = 4
EXPERTS_PER_GROUP = 4
N_EXPERTS = N_GROUPS * EXPERTS_PER_GROUP
D_EXPERT = 512
EPS = 1e-6

LANES = 128
QB = 128
NEG = -1.0e30
DA = DIL_HEADS * HEAD_DIM
DB = NSA_HEADS * HEAD_DIM
N_ROWGROUPS = NSA_HEADS
SEL_KT = 256
VMEM_LIMIT = 56 * 1024 * 1024

F32 = jnp.float32
BF16 = jnp.bfloat16
NT_DIMS = (((1,), (1,)), ((), ()))


def _nt_dot(a, b):
    return lax.dot_general(a, b, NT_DIMS, preferred_element_type=F32)


def _dot(a, b):
    return jnp.dot(a, b, preferred_element_type=F32)


def _bucket_np(dist):
    dist = np.maximum(np.asarray(dist, np.int64), 0)
    max_exact = N_BUCKETS // 2
    x = np.maximum(dist, 1).astype(np.float32) / np.float32(max_exact)
    large = max_exact + (np.log(x) / np.float32(math.log(MAX_DISTANCE / max_exact))
                         * np.float32(N_BUCKETS - max_exact)).astype(np.int32)
    large = np.minimum(large, N_BUCKETS - 1)
    return np.where(dist < max_exact, dist, large).astype(np.int32)


def _bias_table(rel_bias_heads, dist, valid):
    idx = _bucket_np(dist)
    tbl = jnp.take(rel_bias_heads.T.astype(F32), jnp.asarray(idx.reshape(-1)), axis=1)
    tbl = tbl.reshape((rel_bias_heads.shape[1],) + idx.shape)
    return jnp.where(jnp.asarray(valid)[None], tbl, NEG)


IN_TM = 512
C_QA, C_KA, C_VA = 0, DA, 2 * DA
C_QB = 3 * DA
C_KC = C_QB + DB
C_VC, C_KS, C_VS, C_KW, C_VW, C_GT = (C_KC + LANES * i for i in range(1, 7))
N_COLS = C_GT + LANES


def _permute_w_in(w_in):
    scale = 1.0 / math.sqrt(HEAD_DIM)
    sizes = [DA] * 3 + [DB] + [NSA_KV_HEADS * HEAD_DIM] * 6 + [3 * NSA_HEADS]
    offs = np.concatenate([[0], np.cumsum(sizes)])
    part = lambda i: w_in[:, offs[i]:offs[i + 1]]
    d = w_in.shape[0]
    qb = part(3).reshape(d, NSA_KV_HEADS, NSA_GROUP, HEAD_DIM).transpose(0, 2, 1, 3).reshape(d, DB)
    gt = part(10).reshape(d, NSA_KV_HEADS, NSA_GROUP, 3).transpose(0, 3, 2, 1).reshape(d, 3 * NSA_HEADS)
    gt = jnp.pad(gt, ((0, 0), (0, LANES - 3 * NSA_HEADS)))
    cols = [part(0) * scale, part(1), part(2), qb * scale] + [part(i) for i in range(4, 10)] + [gt]
    return jnp.concatenate(cols, axis=1).astype(BF16)


def _inproj_kernel(seq_len, x_ref, g_ref, w_ref, qa_ref, ka_ref, va_ref, qb_ref, kc_ref, vc_ref,
                   ksaug_ref, vsaug_ref, kw_ref, vwaug_ref, gates_ref):
    x = x_ref[...]
    xn = (x * lax.rsqrt(jnp.mean(x * x, axis=-1, keepdims=True) + EPS) * g_ref[...]).astype(BF16)
    seg = lambda a, n: _dot(xn, w_ref[:, a:a + n])
    qa_ref[...] = seg(C_QA, DA).astype(BF16)
    ka_ref[...] = seg(C_KA, DA).astype(BF16)
    va_ref[...] = seg(C_VA, DA).astype(BF16)
    qb_ref[...] = seg(C_QB, DB).astype(BF16)
    kc_ref[...] = seg(C_KC, LANES).astype(BF16)
    vc_ref[...] = seg(C_VC, LANES).astype(BF16)
    kw_ref[...] = seg(C_KW, LANES).astype(BF16)
    tm = x.shape[0]
    tok = (pl.program_id(0) * tm) % seq_len + lax.broadcasted_iota(jnp.int32, (tm, LANES), 0)
    lane = lax.broadcasted_iota(jnp.int32, (tm, LANES), 1)
    onehot = jnp.where(lane == tok // SEL_BLOCK, 1.0, 0.0).astype(BF16)
    ones = jnp.ones((tm, LANES), BF16)
    ksaug_ref[:, 0:LANES] = seg(C_KS, LANES).astype(BF16)
    ksaug_ref[:, LANES:2 * LANES] = onehot
    vsaug_ref[:, 0:LANES] = seg(C_VS, LANES).astype(BF16)
    vsaug_ref[:, LANES:2 * LANES] = ones
    vwaug_ref[:, 0:LANES] = seg(C_VW, LANES).astype(BF16)
    vwaug_ref[:, LANES:2 * LANES] = ones
    gates_ref[...] = jax.nn.sigmoid(seg(C_GT, LANES))


def _inproj(x2d, gain, w_perm, seq_len):
    n, d = x2d.shape
    row = lambda w: pl.BlockSpec((IN_TM, w), lambda i: (i, 0))
    widths = [DA, DA, DA, DB, LANES, LANES, 2 * LANES, 2 * LANES, LANES, 2 * LANES]
    out_shape = [jax.ShapeDtypeStruct((n, w), BF16) for w in widths] + [jax.ShapeDtypeStruct((n, LANES), F32)]
    return pl.pallas_call(
        functools.partial(_inproj_kernel, seq_len),
        grid=(n // IN_TM,),
        in_specs=[row(d), pl.BlockSpec((1, d), lambda i: (0, 0)), pl.BlockSpec((d, N_COLS), lambda i: (0, 0))],
        out_specs=[row(w) for w in widths] + [row(LANES)],
        out_shape=out_shape,
        compiler_params=pltpu.CompilerParams(dimension_semantics=("parallel",), vmem_limit_bytes=VMEM_LIMIT),
        name="inproj",
    )(x2d, gain.reshape(1, d), w_perm)


def _embed_pair(w, n_tok):
    c = w.shape[1]
    w4 = w.reshape(n_tok, 1, HEAD_DIM, 1, c) * jnp.eye(NSA_KV_HEADS, dtype=w.dtype).reshape(1, 2, 1, 2, 1)
    return w4.reshape(n_tok * 2 * HEAD_DIM, 2 * c)


def _gelu_tanh(x):
    return 0.5 * x * (1.0 + jnp.tanh(math.sqrt(2.0 / math.pi) * (x + 0.044715 * (x * x * x))))


def _compress_kernel(ck_ref, cv_ref, posk_ref, posv_ref, wk1a, wk1b, wk2, wv1a, wv1b, wv2,
                     kout_ref, vout_ref, shift_ref):
    ncp = ck_ref.shape[1]
    for c_ref, pos_ref, w1a, w1b, w2, out_ref in ((ck_ref, posk_ref, wk1a, wk1b, wk2, kout_ref),
                                                  (cv_ref, posv_ref, wv1a, wv1b, wv2, vout_ref)):
        c = c_ref[0].astype(F32)
        first = _dot((c + pos_ref[0:1, :]).astype(BF16), w1a[...])
        second = _dot((c + pos_ref[1:2, :]).astype(BF16), w1b[...])
        shift_ref[0:ncp, :] = second
        shift_ref[ncp:ncp + 8, :] = jnp.zeros((8, second.shape[1]), F32)
        hidden = _gelu_tanh(first + shift_ref[1:ncp + 1, :])
        out_ref[0] = _dot(hidden.astype(BF16), w2[...]).astype(BF16)


def _compress(kc, vc, pos_k, pos_v, k_w1, k_w2, v_w1, v_w2):
    b, t, _ = kc.shape
    ncp = t // CMP_STRIDE
    half = CMP_STRIDE * HEAD_DIM
    wide = CMP_STRIDE * LANES

    def prep(w1, w2, pos):
        pos_pair = jnp.broadcast_to(pos.reshape(2, CMP_STRIDE, 1, HEAD_DIM), (2, CMP_STRIDE, 2, HEAD_DIM))
        return (_embed_pair(w1[:half], CMP_STRIDE).astype(BF16), _embed_pair(w1[half:], CMP_STRIDE).astype(BF16),
                jnp.kron(jnp.eye(NSA_KV_HEADS, dtype=w2.dtype), w2).astype(BF16),
                pos_pair.reshape(2, wide).astype(F32))

    wk1a, wk1b, wk2, posk = prep(k_w1, k_w2, pos_k)
    wv1a, wv1b, wv2, posv = prep(v_w1, v_w2, pos_v)
    full = lambda a: pl.BlockSpec(a.shape, lambda i: (0,) * a.ndim)
    tok = pl.BlockSpec((1, ncp, wide), lambda i: (i, 0, 0))
    out = pl.BlockSpec((1, ncp, LANES), lambda i: (i, 0, 0))
    return pl.pallas_call(
        _compress_kernel,
        grid=(b,),
        in_specs=[tok, tok, full(posk), full(posv), full(wk1a), full(wk1b), full(wk2), full(wv1a), full(wv1b), full(wv2)],
        out_specs=[out, out],
        out_shape=[jax.ShapeDtypeStruct((b, ncp, LANES), BF16)] * 2,
        scratch_shapes=[pltpu.VMEM((ncp + 8, 2 * CMP_HIDDEN), F32)],
        compiler_params=pltpu.CompilerParams(dimension_semantics=("parallel",), vmem_limit_bytes=VMEM_LIMIT),
        name="compress",
    )(kc.reshape(b, ncp, wide), vc.reshape(b, ncp, wide), posk, posv, wk1a, wk1b, wk2, wv1a, wv1b, wv2)


def _pair_masks(rows):
    lane = lax.broadcasted_iota(jnp.int32, (rows, LANES), 1)
    return lane < HEAD_DIM


def _dil_kernel(q_ref, kp_ref, kc_ref, vp_ref, vc_ref, bias_ref, o_ref, lse_ref):
    first = pl.program_id(2) == 0
    low = _pair_masks(QB)
    lane2 = lax.broadcasted_iota(jnp.int32, (QB, 2 * QB), 1)
    prev_mask = jnp.where(jnp.logical_and(lane2 < QB, first), NEG, 0.0)
    zero = jnp.zeros((QB, LANES), BF16)
    for p in range(DIL_HEADS // 2):
        cs = slice(p * LANES, (p + 1) * LANES)
        q = q_ref[0, :, cs]
        lhs = jnp.concatenate([jnp.where(low, q, zero), jnp.where(low, zero, q)], axis=0)
        keys = jnp.concatenate([kp_ref[0, :, cs], kc_ref[0, :, cs]], axis=0)
        vals = jnp.concatenate([vp_ref[0, :, cs], vc_ref[0, :, cs]], axis=0)
        s = _nt_dot(lhs, keys)
        bias = jnp.concatenate([bias_ref[2 * p] + prev_mask, bias_ref[2 * p + 1] + prev_mask], axis=0)
        s = s + bias
        m = jnp.max(s, axis=1, keepdims=True)
        e = jnp.exp(s - m)
        l = jnp.sum(e, axis=1, keepdims=True)
        pv = _dot(e.astype(BF16), vals) * (1.0 / l)
        lse = m + jnp.log(l)
        o_ref[0, :, cs] = jnp.where(low, pv[:QB], pv[QB:])
        lse_ref[0, :, cs] = jnp.where(low, jnp.broadcast_to(lse[:QB], (QB, LANES)),
                                      jnp.broadcast_to(lse[QB:], (QB, LANES)))


def _dilated(qa, ka, va, bias, dil):
    b, t, _ = qa.shape
    tr = t // dil
    view = lambda a: a.reshape(b, tr, dil * DA)
    cur = pl.BlockSpec((1, QB, DA), lambda bi, r, i: (bi, i, r))
    prev = pl.BlockSpec((1, QB, DA), lambda bi, r, i: (bi, jnp.maximum(i - 1, 0), r))
    o, lse = pl.pallas_call(
        _dil_kernel,
        grid=(b, dil, tr // QB),
        in_specs=[cur, prev, cur, prev, cur, pl.BlockSpec(bias.shape, lambda bi, r, i: (0, 0, 0))],
        out_specs=[cur, cur],
        out_shape=[jax.ShapeDtypeStruct((b, tr, dil * DA), F32)] * 2,
        compiler_params=pltpu.CompilerParams(dimension_semantics=("parallel", "parallel", "parallel"),
                                             vmem_limit_bytes=VMEM_LIMIT),
        name=f"dilated_{dil}",
    )(view(qa), view(ka), view(ka), view(va), view(va), bias)
    return o.reshape(b, t, DA), lse.reshape(b, t, DA)


def _dil_bias(rel_bias, dil):
    qi = np.arange(QB)[:, None]
    ci = np.arange(2 * QB)[None, :]
    j = qi + QB - ci
    valid = (j >= 0) & (j <= DIL_PATTERNS[0][0])
    return _bias_table(rel_bias[:, :DIL_HEADS], dil * np.maximum(j, 0), valid)


CMP_TILE_KEYS = LANES
CMP_TILE_SPAN = CMP_TILE_KEYS * CMP_STRIDE // QB
CMP_CONST_DELTA = 28


def _cmp_bias(rel_bias):
    delta = np.arange(-1, CMP_CONST_DELTA + 1)[:, None, None]
    qi = np.arange(QB)[None, :, None]
    ni = np.arange(CMP_TILE_KEYS)[None, None, :]
    dist = QB * delta + qi - CMP_STRIDE * ni - (CMP_BLOCK - 1)
    valid = (dist >= 0) & (delta >= 0)
    return _bias_table(rel_bias[:, DIL_HEADS:], np.maximum(dist, 0), valid).transpose(1, 0, 2, 3)


def _overlap_matrix(ncp, n_sel_pad):
    n = np.arange(ncp)[:, None] * CMP_STRIDE
    s = np.arange(n_sel_pad)[None, :] * SEL_BLOCK
    ov = np.clip(np.minimum(n + CMP_BLOCK, s + SEL_BLOCK) - np.maximum(n, s), 0, None) / CMP_BLOCK
    return jnp.asarray(ov, BF16)


def _gate_tile(gates_ref, branch, g):
    c = branch * NSA_HEADS + g * 2
    low = _pair_masks(QB)
    return jnp.where(low, jnp.broadcast_to(gates_ref[0, :, c:c + 1], (QB, LANES)),
                     jnp.broadcast_to(gates_ref[0, :, c + 1:c + 2], (QB, LANES)))


def _masked_q(qb_ref, g, kv):
    q = qb_ref[0, :, g * LANES:(g + 1) * LANES]
    low = _pair_masks(QB)
    keep = low if kv == 0 else jnp.logical_not(low)
    return jnp.where(keep, q, jnp.zeros_like(q))


def _cmp_kernel(n_tiles, qb_ref, kcmp_ref, vcmp_ref, gates_ref, ov_ref, *rest):
    tbl_refs, (oc_ref, sel_ref) = rest[:n_tiles], rest[n_tiles:]
    t0 = pl.program_id(1) * QB
    low = _pair_masks(QB)
    blk = lax.broadcasted_iota(jnp.int32, (QB, LANES), 1)
    cur = (t0 + lax.broadcasted_iota(jnp.int32, (QB, LANES), 0)) // SEL_BLOCK
    blk_f = blk.astype(F32)
    forced = (blk == cur) | (blk == cur - 1) | (blk == 0)
    causal = blk <= cur
    keys, vals, ov = kcmp_ref[0], vcmp_ref[0], ov_ref[...]
    outs = []
    for kv in range(NSA_KV_HEADS):
        psum = jnp.zeros((QB, keys.shape[0]), F32)
        outs.append([])
        for g in range(NSA_GROUP):
            r = kv * NSA_GROUP + g
            s = _nt_dot(_masked_q(qb_ref, g, kv), keys)
            s = s + jnp.concatenate([tbl_refs[c][0, r] for c in range(n_tiles)], axis=1)
            m = jnp.max(s, axis=1, keepdims=True)
            e = jnp.exp(s - m)
            den = jnp.sum(e, axis=1, keepdims=True)
            p = jnp.where(m > 0.5 * NEG, e * (1.0 / den), 0.0)
            psum = psum + p
            outs[kv].append(_dot(p.astype(BF16), vals))
        hi = psum.astype(BF16)
        lo = (psum - hi.astype(F32)).astype(BF16)
        imp = _dot(hi, ov) + _dot(lo, ov)
        val = jnp.where(forced, FORCE_SCORE, jnp.where(causal, imp, -1.0))

        def pick(_, carry):
            val, sel = carry
            mx = jnp.max(val, axis=1, keepdims=True)
            idx = jnp.min(jnp.where(val == mx, blk_f, float(LANES)), axis=1, keepdims=True)
            hit = blk_f == idx
            return jnp.where(hit, -jnp.inf, val), jnp.where(hit, 1.0, sel)

        _, sel = lax.fori_loop(0, SEL_TOPK, pick, (val, jnp.zeros((QB, LANES), F32)))
        sel_ref[0, kv] = jnp.where(causal, sel, 0.0).astype(BF16)
    for g in range(NSA_GROUP):
        oc_ref[0, :, g * LANES:(g + 1) * LANES] = jnp.where(low, outs[0][g], outs[1][g]) * _gate_tile(gates_ref, 0, g)


def _compressed_branch(qb, kcmp, vcmp, gates, rel_bias):
    b, t, _ = qb.shape
    ncp = kcmp.shape[1]
    n_tiles = ncp // CMP_TILE_KEYS
    tbl = _cmp_bias(rel_bias)
    ov = _overlap_matrix(ncp, LANES)

    def tbl_spec(c):
        return pl.BlockSpec((1, NSA_HEADS, QB, CMP_TILE_KEYS),
                            lambda bi, i: (jnp.clip(i - CMP_TILE_SPAN * c, -1, CMP_CONST_DELTA) + 1, 0, 0, 0))

    blockq = lambda w: pl.BlockSpec((1, QB, w), lambda bi, i: (bi, i, 0))
    batch = lambda a: pl.BlockSpec((1,) + a.shape[1:], lambda bi, i: (bi, 0, 0))
    return pl.pallas_call(
        functools.partial(_cmp_kernel, n_tiles),
        grid=(b, t // QB),
        in_specs=[blockq(DB), batch(kcmp), batch(vcmp), blockq(LANES), pl.BlockSpec(ov.shape, lambda bi, i: (0, 0))]
                 + [tbl_spec(c) for c in range(n_tiles)],
        out_specs=[blockq(DB), pl.BlockSpec((1, NSA_KV_HEADS, QB, LANES), lambda bi, i: (bi, 0, i, 0))],
        out_shape=[jax.ShapeDtypeStruct((b, t, DB), F32), jax.ShapeDtypeStruct((b, NSA_KV_HEADS, t, LANES), BF16)],
        compiler_params=pltpu.CompilerParams(dimension_semantics=("parallel", "parallel"), vmem_limit_bytes=VMEM_LIMIT),
        name="nsa_compressed",
    )(qb, kcmp, vcmp, gates, ov, *([tbl] * n_tiles))


SEL_NEAR = 13


def _sel_bias(rel_bias):
    off = np.arange(-1, SEL_NEAR + 1)[:, None, None]
    dist = QB * off + np.arange(QB)[None, :, None] - np.arange(QB)[None, None, :]
    heads = rel_bias[:, DIL_HEADS:]
    tbl = _bias_table(heads, np.maximum(dist, 0), (dist >= 0) | (off < 0)) - heads[N_BUCKETS - 1][:, None, None, None]
    tbl = jnp.where(jnp.asarray(off < 0)[None], 0.0, tbl)
    return jnp.maximum(tbl, NEG).transpose(1, 0, 2, 3)


def _sel_kernel(qb_ref, sel_ref, gates_ref, cfar_ref, ks_ref, vs_ref, tbl_ref, out_ref,
                qaug_ref, s_ref, acc_ref, m_ref):
    qblk = pl.program_id(1)
    low = _pair_masks(QB)
    for kv in range(NSA_KV_HEADS):
        unchosen = jnp.where(sel_ref[0, kv].astype(F32) > 0.0, 0.0, NEG)
        for g in range(NSA_GROUP):
            r = kv * NSA_GROUP + g
            rows = slice(r * QB, (r + 1) * QB)
            qaug_ref[rows, 0:LANES] = _masked_q(qb_ref, g, kv)
            qaug_ref[rows, LANES:2 * LANES] = (unchosen + cfar_ref[r:r + 1, :]).astype(BF16)
    acc_ref[...] = jnp.zeros_like(acc_ref)
    m_ref[...] = jnp.full_like(m_ref, NEG)

    def tile(j, near):
        keys = ks_ref[0, pl.ds(pl.multiple_of(j * SEL_KT, SEL_KT), SEL_KT), :]
        vals = vs_ref[0, pl.ds(pl.multiple_of(j * SEL_KT, SEL_KT), SEL_KT), :]
        s_ref[...] = _nt_dot(qaug_ref[...], keys)
        for r in range(N_ROWGROUPS):
            rows = slice(r * QB, (r + 1) * QB)
            s = s_ref[rows, :]
            if near:
                e1 = qblk - 2 * j + 1
                s = s + jnp.concatenate([tbl_ref[e1, r], tbl_ref[e1 - 1, r]], axis=1)
            m_old = m_ref[rows, :]
            m_new = jnp.maximum(m_old, jnp.max(s, axis=1, keepdims=True))
            alpha = jnp.exp(m_old - m_new)
            p = jnp.exp(s - jnp.concatenate([m_new, m_new], axis=1))
            acc_ref[rows, :] = jnp.concatenate([alpha, alpha], axis=1) * acc_ref[rows, :] + _dot(p.astype(BF16), vals)
            m_ref[rows, :] = m_new

    n_tiles = (qblk + 2) // 2
    n_far = jnp.maximum((qblk - (SEL_NEAR - 1)) // 2, 0)
    lax.fori_loop(0, n_far, lambda j, c: (tile(j, False), c)[1], 0)
    lax.fori_loop(n_far, n_tiles, lambda j, c: (tile(j, True), c)[1], 0)
    for g in range(NSA_GROUP):
        o = []
        for kv in range(NSA_KV_HEADS):
            a = acc_ref[(kv * NSA_GROUP + g) * QB:(kv * NSA_GROUP + g + 1) * QB, :]
            o.append(a[:, 0:LANES] * (1.0 / a[:, LANES:2 * LANES]))
        out_ref[0, :, g * LANES:(g + 1) * LANES] = jnp.where(low, o[0], o[1]) * _gate_tile(gates_ref, 1, g)


def _selected_branch(qb, sel, gates, ksaug, vsaug, rel_bias):
    b, t, _ = qb.shape
    tbl = _sel_bias(rel_bias)
    cfar = jnp.broadcast_to(rel_bias[N_BUCKETS - 1, DIL_HEADS:, None].astype(F32), (NSA_HEADS, LANES))
    cfar = jnp.pad(cfar, ((0, 16 - NSA_HEADS), (0, 0)))
    blockq = lambda w: pl.BlockSpec((1, QB, w), lambda bi, i: (bi, i, 0))
    batch = lambda a: pl.BlockSpec((1,) + a.shape[1:], lambda bi, i: (bi, 0, 0))
    rows = N_ROWGROUPS * QB
    return pl.pallas_call(
        _sel_kernel,
        grid=(b, t // QB),
        in_specs=[blockq(DB), pl.BlockSpec((1, NSA_KV_HEADS, QB, LANES), lambda bi, i: (bi, 0, i, 0)), blockq(LANES),
                  pl.BlockSpec(cfar.shape, lambda bi, i: (0, 0)), batch(ksaug), batch(vsaug),
                  pl.BlockSpec(tbl.shape, lambda bi, i: (0, 0, 0, 0))],
        out_specs=blockq(DB),
        out_shape=jax.ShapeDtypeStruct((b, t, DB), F32),
        scratch_shapes=[pltpu.VMEM((rows, 2 * LANES), BF16), pltpu.VMEM((rows, SEL_KT), F32),
                        pltpu.VMEM((rows, 2 * LANES), F32), pltpu.VMEM((rows, LANES), F32)],
        compiler_params=pltpu.CompilerParams(dimension_semantics=("parallel", "parallel"), vmem_limit_bytes=VMEM_LIMIT),
        name="nsa_selected",
    )(qb, sel, gates, cfar, ksaug, vsaug, tbl)


WIN_KEYS = WIN + QB


def _win_bias(rel_bias):
    dist = np.arange(QB)[:, None] - np.arange(WIN_KEYS)[None, :] + WIN
    return _bias_table(rel_bias[:, DIL_HEADS:], np.maximum(dist, 0), (dist >= 0) & (dist < WIN))


def _win_kernel(qb_ref, gates_ref, kw_ref, vw_ref, tbl_ref, out_ref, q_ref, s_ref):
    qblk = pl.program_id(1)
    low = _pair_masks(QB)
    for kv in range(NSA_KV_HEADS):
        for g in range(NSA_GROUP):
            r = kv * NSA_GROUP + g
            q_ref[r * QB:(r + 1) * QB, :] = _masked_q(qb_ref, g, kv)
    start = pl.multiple_of(qblk * QB, QB)
    keys = kw_ref[0, pl.ds(start, WIN_KEYS), :]
    vals = vw_ref[0, pl.ds(start, WIN_KEYS), :]
    s_ref[...] = _nt_dot(q_ref[...], keys)
    col = lax.broadcasted_iota(jnp.int32, (1, WIN_KEYS), 1)
    pad_mask = jnp.where(col + qblk * QB >= WIN, 0.0, NEG)
    outs = []
    for r in range(N_ROWGROUPS):
        s = s_ref[r * QB:(r + 1) * QB, :] + tbl_ref[r] + pad_mask
        e = jnp.exp(s - jnp.max(s, axis=1, keepdims=True))
        pv = _dot(e.astype(BF16), vals)
        outs.append(pv[:, 0:LANES] * (1.0 / pv[:, LANES:2 * LANES]))
    for g in range(NSA_GROUP):
        out_ref[0, :, g * LANES:(g + 1) * LANES] = (jnp.where(low, outs[g], outs[NSA_GROUP + g])
                                                    * _gate_tile(gates_ref, 2, g))


def _window_branch(qb, gates, kw, vwaug, rel_bias):
    b, t, _ = qb.shape
    tbl = _win_bias(rel_bias)
    kw_pad = jnp.pad(kw, ((0, 0), (WIN, 0), (0, 0)))
    vw_pad = jnp.pad(vwaug, ((0, 0), (WIN, 0), (0, 0)))
    blockq = lambda w: pl.BlockSpec((1, QB, w), lambda bi, i: (bi, i, 0))
    batch = lambda a: pl.BlockSpec((1,) + a.shape[1:], lambda bi, i: (bi, 0, 0))
    rows = N_ROWGROUPS * QB
    return pl.pallas_call(
        _win_kernel,
        grid=(b, t // QB),
        in_specs=[blockq(DB), blockq(LANES), batch(kw_pad), batch(vw_pad), pl.BlockSpec(tbl.shape, lambda bi, i: (0, 0, 0))],
        out_specs=blockq(DB),
        out_shape=jax.ShapeDtypeStruct((b, t, DB), F32),
        scratch_shapes=[pltpu.VMEM((rows, LANES), BF16), pltpu.VMEM((rows, WIN_KEYS), F32)],
        compiler_params=pltpu.CompilerParams(dimension_semantics=("parallel", "parallel"), vmem_limit_bytes=VMEM_LIMIT),
        name="nsa_window",
    )(qb, gates, kw_pad, vw_pad, tbl)


OUT_TM = 256
C_GROUP = N_EXPERTS


def _outproj_kernel(x_ref, o1_ref, l1_ref, o4_ref, l4_ref, o16_ref, l16_ref, oc_ref, os_ref, ow_ref,
                    wout_ref, g_ref, wr_ref, br_ref, h_ref, hn_ref, comb_ref):
    l1, l4, l16 = l1_ref[...], l4_ref[...], l16_ref[...]
    mx = jnp.maximum(jnp.maximum(l1, l4), l16)
    w1, w4, w16 = jnp.exp(l1 - mx), jnp.exp(l4 - mx), jnp.exp(l16 - mx)
    oa = (w1 * o1_ref[...] + w4 * o4_ref[...] + w16 * o16_ref[...]) * (1.0 / (w1 + w4 + w16))
    ob = oc_ref[...] + os_ref[...] + ow_ref[...]
    y = _dot(oa.astype(BF16), wout_ref[0:DA, :]) + _dot(ob.astype(BF16), wout_ref[DA:DA + DB, :])
    h = x_ref[...] + y
    h_ref[...] = h
    hn = h * lax.rsqrt(jnp.mean(h * h, axis=-1, keepdims=True) + EPS) * g_ref[...]
    hn_ref[...] = hn.astype(BF16)
    logits = jnp.dot(hn, wr_ref[...], precision=lax.Precision.HIGHEST, preferred_element_type=F32) + br_ref[...]
    lane = lax.broadcasted_iota(jnp.int32, logits.shape, 1)
    lane_f = lane.astype(F32)
    big = float(LANES)
    gl = jnp.where((lane >= C_GROUP) & (lane < C_GROUP + N_GROUPS), logits, -jnp.inf)
    gmax = jnp.max(gl, axis=1, keepdims=True)
    gidx = jnp.min(jnp.where(gl == gmax, lane_f, big), axis=1, keepdims=True) - C_GROUP
    gprob = 1.0 / jnp.sum(jnp.exp(gl - gmax), axis=1, keepdims=True)
    grp_of_lane = (lane // EXPERTS_PER_GROUP).astype(F32)
    el = jnp.where((lane < N_EXPERTS) & (grp_of_lane == gidx), logits, -jnp.inf)
    v1 = jnp.max(el, axis=1, keepdims=True)
    i1 = jnp.min(jnp.where(el == v1, lane_f, big), axis=1, keepdims=True)
    el2 = jnp.where(lane_f == i1, -jnp.inf, el)
    v2 = jnp.max(el2, axis=1, keepdims=True)
    i2 = jnp.min(jnp.where(el2 == v2, lane_f, big), axis=1, keepdims=True)
    e2 = jnp.exp(v2 - v1)
    p1 = 1.0 / (1.0 + e2)
    comb_ref[...] = gprob * (jnp.where(lane_f == i1, p1, 0.0) + jnp.where(lane_f == i2, e2 * p1, 0.0))


def _outproj(x2d, a_parts, b_parts, w_out_perm, gain, w_router, b_router):
    n, d = x2d.shape
    row = lambda w: pl.BlockSpec((OUT_TM, w), lambda i: (i, 0))
    full = lambda a: pl.BlockSpec(a.shape, lambda i: (0, 0))
    return pl.pallas_call(
        _outproj_kernel,
        grid=(n // OUT_TM,),
        in_specs=[row(d)] + [row(DA)] * 6 + [row(DB)] * 3 + [full(w_out_perm), pl.BlockSpec((1, d), lambda i: (0, 0)),
                                                            full(w_router), full(b_router)],
        out_specs=[row(d), row(d), row(LANES)],
        out_shape=[jax.ShapeDtypeStruct((n, d), F32), jax.ShapeDtypeStruct((n, d), BF16),
                   jax.ShapeDtypeStruct((n, LANES), F32)],
        compiler_params=pltpu.CompilerParams(dimension_semantics=("parallel",), vmem_limit_bytes=VMEM_LIMIT),
        name="outproj_router",
    )(x2d, *a_parts, *b_parts, w_out_perm, gain.reshape(1, d), w_router, b_router)


MOE_TM = 512


def _moe_kernel(h_ref, hn_ref, comb_ref, wg_ref, wu_ref, wd_ref, g_ref, out_ref, acc_ref):
    e = pl.program_id(1)

    @pl.when(e == 0)
    def _():
        acc_ref[...] = jnp.zeros_like(acc_ref)

    hn = hn_ref[...]
    gate = _dot(hn, wg_ref[0])
    up = _dot(hn, wu_ref[0])
    lane = lax.broadcasted_iota(jnp.int32, comb_ref.shape, 1)
    weight = jnp.sum(jnp.where(lane == e, comb_ref[...], 0.0), axis=1, keepdims=True)
    hidden = (gate * jax.nn.sigmoid(gate) * up * weight).astype(BF16)
    acc_ref[...] += _dot(hidden, wd_ref[0])

    @pl.when(e == pl.num_programs(1) - 1)
    def _():
        y = h_ref[...] + acc_ref[...]
        out_ref[...] = y * lax.rsqrt(jnp.mean(y * y, axis=-1, keepdims=True) + EPS) * g_ref[...]


def _moe(h, hn, comb, w_gate, w_up, w_down, gain):
    n, d = h.shape
    tm = min(MOE_TM, n)
    row = lambda w: pl.BlockSpec((tm, w), lambda i, e: (i, 0))
    return pl.pallas_call(
        _moe_kernel,
        grid=(n // tm, N_EXPERTS),
        in_specs=[row(d), row(d), row(LANES),
                  pl.BlockSpec((1, d, D_EXPERT), lambda i, e: (e, 0, 0)),
                  pl.BlockSpec((1, d, D_EXPERT), lambda i, e: (e, 0, 0)),
                  pl.BlockSpec((1, D_EXPERT, d), lambda i, e: (e, 0, 0)),
                  pl.BlockSpec((1, d), lambda i, e: (0, 0))],
        out_specs=row(d),
        out_shape=jax.ShapeDtypeStruct((n, d), F32),
        scratch_shapes=[pltpu.VMEM((tm, d), F32)],
        compiler_params=pltpu.CompilerParams(dimension_semantics=("parallel", "arbitrary"), vmem_limit_bytes=VMEM_LIMIT),
        name="moe_experts",
    )(h, hn, comb, w_gate, w_up, w_down, gain.reshape(1, d))


def _permute_w_out(w_out):
    d = w_out.shape[1]
    wb = w_out[DA:].reshape(NSA_KV_HEADS, NSA_GROUP, HEAD_DIM, d).transpose(1, 0, 2, 3).reshape(DB, d)
    return jnp.concatenate([w_out[:DA], wb], axis=0).astype(BF16)


def _router_weights(w_group, b_group, w_expert, b_expert):
    d = w_group.shape[0]
    w = jnp.concatenate([w_expert.reshape(d, N_EXPERTS), w_group], axis=1)
    b = jnp.concatenate([b_expert.reshape(N_EXPERTS), b_group])
    pad = LANES - w.shape[1]
    return jnp.pad(w, ((0, 0), (0, pad))).astype(F32), jnp.pad(b, (0, pad)).reshape(1, LANES).astype(F32)


def _layer(h, rel_bias, norm_mix, w_in, w_out, cmp_pos_k, cmp_pos_v, cmp_k_w1, cmp_k_w2, cmp_v_w1, cmp_v_w2,
           norm_ffn, w_rg, b_rg, w_re, b_re, w_gate, w_up, w_down, out_gain):
    b, t, d = h.shape
    n = b * t
    assert t % (QB * DIL_PATTERNS[-1][1]) == 0 and t // SEL_BLOCK <= LANES and n % MOE_TM == 0
    x2d = h.reshape(n, d)
    qa, ka, va, qb, kc, vc, ksaug, vsaug, kw, vwaug, gates = _inproj(x2d, norm_mix, _permute_w_in(w_in), t)
    seq = lambda a: a.reshape(b, t, a.shape[-1])
    qa, ka, va, qb, kc, vc, ksaug, vsaug, kw, vwaug, gates = map(seq, (qa, ka, va, qb, kc, vc, ksaug, vsaug, kw, vwaug, gates))
    a_parts = []
    for _, dil in DIL_PATTERNS:
        o, lse = _dilated(qa, ka, va, _dil_bias(rel_bias, dil), dil)
        a_parts += [o.reshape(n, DA), lse.reshape(n, DA)]
    kcmp, vcmp = _compress(kc, vc, cmp_pos_k, cmp_pos_v, cmp_k_w1, cmp_k_w2, cmp_v_w1, cmp_v_w2)
    o_cmp, sel = _compressed_branch(qb, kcmp, vcmp, gates, rel_bias)
    o_sel = _selected_branch(qb, sel, gates, ksaug, vsaug, rel_bias)
    o_win = _window_branch(qb, gates, kw, vwaug, rel_bias)
    b_parts = [o.reshape(n, DB) for o in (o_cmp, o_sel, o_win)]
    w_router, b_router = _router_weights(w_rg, b_rg, w_re, b_re)
    h2, hn, comb = _outproj(x2d, a_parts, b_parts, _permute_w_out(w_out), norm_ffn, w_router, b_router)
    return _moe(h2, hn, comb, w_gate.astype(BF16), w_up.astype(BF16), w_down.astype(BF16), out_gain)


def kernel(x, rel_bias, norm_mix, w_in, w_out, cmp_pos_k, cmp_pos_v, cmp_k_w1, cmp_k_w2, cmp_v_w1, cmp_v_w2,
           norm_ffn, w_router_group, b_router_group, w_router_expert, b_router_expert, w_gate, w_up, w_down,
           norm_final):
    depth = norm_mix.shape[0]
    assert depth == 1, "the final RMSNorm is fused into the last layer's expert kernel"
    out = _layer(x, rel_bias, norm_mix[0], w_in[0], w_out[0], cmp_pos_k[0], cmp_pos_v[0], cmp_k_w1[0], cmp_k_w2[0],
                 cmp_v_w1[0], cmp_v_w2[0], norm_ffn[0], w_router_group[0], b_router_group[0], w_router_expert[0],
                 b_router_expert[0], w_gate[0], w_up[0], w_down[0], norm_final)
    return out.reshape(x.shape)
```

```python
import functools
import math

import jax
import jax.numpy as jnp
import numpy as np
from jax import lax
from jax.experimental import pallas as pl
from jax.experimental.pallas import tpu as pltpu

HEAD_DIM = 64
DIL_HEADS = 6
NSA_KV_HEADS = 2
NSA_GROUP = 5
NSA_HEADS = NSA_KV_HEADS * NSA_GROUP
N_HEADS = DIL_HEADS + NSA_HEADS
DIL_PATTERNS = ((128, 1), (512, 4), (2048, 16))
CMP_BLOCK = 32
CMP_STRIDE = 16
CMP_HIDDEN = 256
SEL_BLOCK = 64
SEL_TOPK = 16
WIN = 512
FORCE_SCORE = 1.0e4
N_BUCKETS = 32
MAX_DISTANCE = 2048
N_GROUPS = 4
EXPERTS_PER_GROUP = 4
N_EXPERTS = N_GROUPS * EXPERTS_PER_GROUP
D_EXPERT = 512
EPS = 1e-6

LANES = 128
QB = 128
NEG = -1.0e30
DA = DIL_HEADS * HEAD_DIM
DB = NSA_HEADS * HEAD_DIM
N_ROWGROUPS = NSA_HEADS
SEL_KT = 256
VMEM_LIMIT = 56 * 1024 * 1024

F32 = jnp.float32
BF16 = jnp.bfloat16
NT_DIMS = (((1,), (1,)), ((), ()))


def _nt_dot(a, b):
    return lax.dot_general(a, b, NT_DIMS, preferred_element_type=F32)


def _dot(a, b):
    return jnp.dot(a, b, preferred_element_type=F32)


def _bucket_np(dist):
    dist = np.maximum(np.asarray(dist, np.int64), 0)
    max_exact = N_BUCKETS // 2
    x = np.maximum(dist, 1).astype(np.float32) / np.float32(max_exact)
    large = max_exact + (np.log(x) / np.float32(math.log(MAX_DISTANCE / max_exact))
                         * np.float32(N_BUCKETS - max_exact)).astype(np.int32)
    large = np.minimum(large, N_BUCKETS - 1)
    return np.where(dist < max_exact, dist, large).astype(np.int32)


BIAS_LEN = 4096


def _bias_1d(rel_bias_heads):
    onehot = (_bucket_np(np.arange(BIAS_LEN))[None, :] == np.arange(N_BUCKETS)[:, None]).astype(np.float32)
    return jnp.dot(rel_bias_heads.T.astype(F32), jnp.asarray(onehot), precision=lax.Precision.HIGHEST)


def _extend(f, lo, hi):
    assert hi <= f.shape[-1]
    if lo >= 0:
        return f[..., lo:hi]
    pad = jnp.full(f.shape[:-1] + (-lo,), NEG, f.dtype)
    return jnp.concatenate([pad, f[..., :hi]], axis=-1)


def _toeplitz(w, q, c):
    n = q + c - 1
    assert w.shape[-1] == n
    lead = w.shape[:-1]
    wp = jnp.concatenate([w, jnp.zeros(lead + (1,), w.dtype)], axis=-1)
    flat = jnp.broadcast_to(wp[..., None, :], lead + (q, n + 1)).reshape(lead + (q * (n + 1),))
    return flat[..., :q * n].reshape(lead + (q, n))[..., q - 1:q - 1 + c]


def _toeplitz_of(fn_vals, lo, q, c):
    return _toeplitz(jnp.flip(fn_vals, axis=-1), q, c)


IN_TM = 512
C_QA, C_KA, C_VA = 0, DA, 2 * DA
C_QB = 3 * DA
C_KC = C_QB + DB
C_VC, C_KS, C_VS, C_KW, C_VW, C_GT = (C_KC + LANES * i for i in range(1, 7))
N_COLS = C_GT + LANES


def _permute_w_in(w_in):
    scale = 1.0 / math.sqrt(HEAD_DIM)
    sizes = [DA] * 3 + [DB] + [NSA_KV_HEADS * HEAD_DIM] * 6 + [3 * NSA_HEADS]
    offs = np.concatenate([[0], np.cumsum(sizes)])
    part = lambda i: w_in[:, offs[i]:offs[i + 1]]
    d = w_in.shape[0]
    qb = part(3).reshape(d, NSA_KV_HEADS, NSA_GROUP, HEAD_DIM).transpose(0, 2, 1, 3).reshape(d, DB)
    gt = part(10).reshape(d, NSA_KV_HEADS, NSA_GROUP, 3).transpose(0, 3, 2, 1).reshape(d, 3 * NSA_HEADS)
    gt = jnp.pad(gt, ((0, 0), (0, LANES - 3 * NSA_HEADS)))
    cols = [part(0) * scale, part(1), part(2), qb * scale] + [part(i) for i in range(4, 10)] + [gt]
    return jnp.concatenate(cols, axis=1).astype(BF16)


def _inproj_kernel(seq_len, x_ref, g_ref, w_ref, qa_ref, ka_ref, va_ref, qb_ref, kc_ref, vc_ref,
                   ksaug_ref, vsaug_ref, kw_ref, vwaug_ref, gates_ref):
    x = x_ref[...]
    xn = (x * lax.rsqrt(jnp.mean(x * x, axis=-1, keepdims=True) + EPS) * g_ref[...]).astype(BF16)
    seg = lambda a, n: _dot(xn, w_ref[:, a:a + n])
    qa_ref[...] = seg(C_QA, DA).astype(BF16)
    ka_ref[...] = seg(C_KA, DA).astype(BF16)
    va_ref[...] = seg(C_VA, DA).astype(BF16)
    qb_ref[...] = seg(C_QB, DB).astype(BF16)
    kc_ref[...] = seg(C_KC, LANES).astype(BF16)
    vc_ref[...] = seg(C_VC, LANES).astype(BF16)
    kw_ref[...] = seg(C_KW, LANES).astype(BF16)
    tm = x.shape[0]
    tok = (pl.program_id(0) * tm) % seq_len + lax.broadcasted_iota(jnp.int32, (tm, LANES), 0)
    lane = lax.broadcasted_iota(jnp.int32, (tm, LANES), 1)
    onehot = jnp.where(lane == tok // SEL_BLOCK, 1.0, 0.0).astype(BF16)
    ones = jnp.ones((tm, LANES), BF16)
    ksaug_ref[:, 0:LANES] = seg(C_KS, LANES).astype(BF16)
    ksaug_ref[:, LANES:2 * LANES] = onehot
    vsaug_ref[:, 0:LANES] = seg(C_VS, LANES).astype(BF16)
    vsaug_ref[:, LANES:2 * LANES] = ones
    vwaug_ref[:, 0:LANES] = seg(C_VW, LANES).astype(BF16)
    vwaug_ref[:, LANES:2 * LANES] = ones
    gates_ref[...] = jax.nn.sigmoid(seg(C_GT, LANES))


def _inproj(x2d, gain, w_perm, seq_len):
    n, d = x2d.shape
    row = lambda w: pl.BlockSpec((IN_TM, w), lambda i: (i, 0))
    widths = [DA, DA, DA, DB, LANES, LANES, 2 * LANES, 2 * LANES, LANES, 2 * LANES]
    out_shape = [jax.ShapeDtypeStruct((n, w), BF16) for w in widths] + [jax.ShapeDtypeStruct((n, LANES), F32)]
    return pl.pallas_call(
        functools.partial(_inproj_kernel, seq_len),
        grid=(n // IN_TM,),
        in_specs=[row(d), pl.BlockSpec((1, d), lambda i: (0, 0)), pl.BlockSpec((d, N_COLS), lambda i: (0, 0))],
        out_specs=[row(w) for w in widths] + [row(LANES)],
        out_shape=out_shape,
        compiler_params=pltpu.CompilerParams(dimension_semantics=("parallel",), vmem_limit_bytes=VMEM_LIMIT),
        name="inproj",
    )(x2d, gain.reshape(1, d), w_perm)


def _embed_pair(w, n_tok):
    c = w.shape[1]
    w4 = w.reshape(n_tok, 1, HEAD_DIM, 1, c) * jnp.eye(NSA_KV_HEADS, dtype=w.dtype).reshape(1, 2, 1, 2, 1)
    return w4.reshape(n_tok * 2 * HEAD_DIM, 2 * c)


def _gelu_tanh(x):
    return 0.5 * x * (1.0 + jnp.tanh(math.sqrt(2.0 / math.pi) * (x + 0.044715 * (x * x * x))))


def _compress_kernel(ck_ref, cv_ref, posk_ref, posv_ref, wk1a, wk1b, wk2, wv1a, wv1b, wv2,
                     kout_ref, vout_ref, shift_ref):
    ncp = ck_ref.shape[1]
    for c_ref, pos_ref, w1a, w1b, w2, out_ref in ((ck_ref, posk_ref, wk1a, wk1b, wk2, kout_ref),
                                                  (cv_ref, posv_ref, wv1a, wv1b, wv2, vout_ref)):
        c = c_ref[0].astype(F32)
        first = _dot((c + pos_ref[0:1, :]).astype(BF16), w1a[...])
        second = _dot((c + pos_ref[1:2, :]).astype(BF16), w1b[...])
        shift_ref[0:ncp, :] = second
        shift_ref[ncp:ncp + 8, :] = jnp.zeros((8, second.shape[1]), F32)
        hidden = _gelu_tanh(first + shift_ref[1:ncp + 1, :])
        out_ref[0] = _dot(hidden.astype(BF16), w2[...]).astype(BF16)


def _compress(kc, vc, pos_k, pos_v, k_w1, k_w2, v_w1, v_w2):
    b, t, _ = kc.shape
    ncp = t // CMP_STRIDE
    half = CMP_STRIDE * HEAD_DIM
    wide = CMP_STRIDE * LANES

    def prep(w1, w2, pos):
        pos_pair = jnp.broadcast_to(pos.reshape(2, CMP_STRIDE, 1, HEAD_DIM), (2, CMP_STRIDE, 2, HEAD_DIM))
        return (_embed_pair(w1[:half], CMP_STRIDE).astype(BF16), _embed_pair(w1[half:], CMP_STRIDE).astype(BF16),
                jnp.kron(jnp.eye(NSA_KV_HEADS, dtype=w2.dtype), w2).astype(BF16),
                pos_pair.reshape(2, wide).astype(F32))

    wk1a, wk1b, wk2, posk = prep(k_w1, k_w2, pos_k)
    wv1a, wv1b, wv2, posv = prep(v_w1, v_w2, pos_v)
    full = lambda a: pl.BlockSpec(a.shape, lambda i: (0,) * a.ndim)
    tok = pl.BlockSpec((1, ncp, wide), lambda i: (i, 0, 0))
    out = pl.BlockSpec((1, ncp, LANES), lambda i: (i, 0, 0))
    return pl.pallas_call(
        _compress_kernel,
        grid=(b,),
        in_specs=[tok, tok, full(posk), full(posv), full(wk1a), full(wk1b), full(wk2), full(wv1a), full(wv1b), full(wv2)],
        out_specs=[out, out],
        out_shape=[jax.ShapeDtypeStruct((b, ncp, LANES), BF16)] * 2,
        scratch_shapes=[pltpu.VMEM((ncp + 8, 2 * CMP_HIDDEN), F32)],
        compiler_params=pltpu.CompilerParams(dimension_semantics=("parallel",), vmem_limit_bytes=VMEM_LIMIT),
        name="compress",
    )(kc.reshape(b, ncp, wide), vc.reshape(b, ncp, wide), posk, posv, wk1a, wk1b, wk2, wv1a, wv1b, wv2)


def _pair_masks(rows):
    lane = lax.broadcasted_iota(jnp.int32, (rows, LANES), 1)
    return lane < HEAD_DIM


def _dil_kernel(q_ref, kp_ref, kc_ref, vp_ref, vc_ref, bias_ref, o_ref, lse_ref):
    first = pl.program_id(2) == 0
    low = _pair_masks(QB)
    lane2 = lax.broadcasted_iota(jnp.int32, (QB, 2 * QB), 1)
    prev_mask = jnp.where(jnp.logical_and(lane2 < QB, first), NEG, 0.0)
    zero = jnp.zeros((QB, LANES), BF16)
    for p in range(DIL_HEADS // 2):
        cs = slice(p * LANES, (p + 1) * LANES)
        q = q_ref[0, :, cs]
        lhs = jnp.concatenate([jnp.where(low, q, zero), jnp.where(low, zero, q)], axis=0)
        keys = jnp.concatenate([kp_ref[0, :, cs], kc_ref[0, :, cs]], axis=0)
        vals = jnp.concatenate([vp_ref[0, :, cs], vc_ref[0, :, cs]], axis=0)
        s = _nt_dot(lhs, keys)
        bias = jnp.concatenate([bias_ref[2 * p] + prev_mask, bias_ref[2 * p + 1] + prev_mask], axis=0)
        s = s + bias
        m = jnp.max(s, axis=1, keepdims=True)
        e = jnp.exp(s - m)
        l = jnp.sum(e, axis=1, keepdims=True)
        pv = _dot(e.astype(BF16), vals) * (1.0 / l)
        lse = m + jnp.log(l)
        o_ref[0, :, cs] = jnp.where(low, pv[:QB], pv[QB:])
        lse_ref[0, :, cs] = jnp.where(low, jnp.broadcast_to(lse[:QB], (QB, LANES)),
                                      jnp.broadcast_to(lse[QB:], (QB, LANES)))


def _dilated(qa, ka, va, bias, dil):
    b, t, _ = qa.shape
    tr = t // dil
    view = lambda a: a.reshape(b, tr, dil * DA)
    cur = pl.BlockSpec((1, QB, DA), lambda bi, r, i: (bi, i, r))
    prev = pl.BlockSpec((1, QB, DA), lambda bi, r, i: (bi, jnp.maximum(i - 1, 0), r))
    o, lse = pl.pallas_call(
        _dil_kernel,
        grid=(b, dil, tr // QB),
        in_specs=[cur, prev, cur, prev, cur, pl.BlockSpec(bias.shape, lambda bi, r, i: (0, 0, 0))],
        out_specs=[cur, cur],
        out_shape=[jax.ShapeDtypeStruct((b, tr, dil * DA), F32)] * 2,
        compiler_params=pltpu.CompilerParams(dimension_semantics=("parallel", "parallel", "parallel"),
                                             vmem_limit_bytes=VMEM_LIMIT),
        name=f"dilated_{dil}",
    )(view(qa), view(ka), view(ka), view(va), view(va), bias)
    return o.reshape(b, t, DA), lse.reshape(b, t, DA)


def _dil_bias(f_a, dil):
    steps = DIL_PATTERNS[0][0]
    g = f_a[:, 0:dil * steps + 1:dil]
    lo, hi = QB - (2 * QB - 1), QB + QB
    vals = jnp.concatenate([jnp.full((DIL_HEADS, -lo), NEG, F32), g, jnp.full((DIL_HEADS, hi - steps - 1), NEG, F32)], axis=1)
    return _toeplitz_of(vals, lo, QB, 2 * QB)


CMP_TILE_KEYS = LANES
CMP_TILE_SPAN = CMP_TILE_KEYS * CMP_STRIDE // QB
CMP_CONST_DELTA = 28


def _cmp_bias(f_b):
    per = QB // CMP_STRIDE
    n_rows = per * (CMP_CONST_DELTA + 1)
    m_lo, m_hi = -(CMP_TILE_KEYS - 1), n_rows
    base = _extend(f_b, CMP_STRIDE * m_lo - (CMP_BLOCK - 1), CMP_STRIDE * m_hi - (CMP_BLOCK - 1))
    g = base.reshape(NSA_HEADS, m_hi - m_lo, CMP_STRIDE).transpose(0, 2, 1)
    t = _toeplitz_of(g, m_lo, n_rows, CMP_TILE_KEYS)
    t = t.reshape(NSA_HEADS, CMP_STRIDE, CMP_CONST_DELTA + 1, per, CMP_TILE_KEYS).transpose(2, 0, 3, 1, 4)
    t = t.reshape(CMP_CONST_DELTA + 1, NSA_HEADS, QB, CMP_TILE_KEYS)
    return jnp.concatenate([jnp.full((1,) + t.shape[1:], NEG, F32), t], axis=0)


def _overlap_matrix_t(ncp, n_sel_pad):
    n = np.arange(ncp)[None, :] * CMP_STRIDE
    s = np.arange(n_sel_pad)[:, None] * SEL_BLOCK
    ov = np.clip(np.minimum(n + CMP_BLOCK, s + SEL_BLOCK) - np.maximum(n, s), 0, None) / CMP_BLOCK
    return jnp.asarray(ov, BF16)


def _gate_tile(gates_ref, branch, g):
    c = branch * NSA_HEADS + g * 2
    low = _pair_masks(QB)
    return jnp.where(low, jnp.broadcast_to(gates_ref[0, :, c:c + 1], (QB, LANES)),
                     jnp.broadcast_to(gates_ref[0, :, c + 1:c + 2], (QB, LANES)))


def _masked_q(qb_ref, g, kv):
    q = qb_ref[0, :, g * LANES:(g + 1) * LANES]
    low = _pair_masks(QB)
    keep = low if kv == 0 else jnp.logical_not(low)
    return jnp.where(keep, q, jnp.zeros_like(q))


def _cmp_kernel(n_tiles, qb_ref, kcmp_ref, vcmp_ref, gates_ref, ov_ref, *rest):
    tbl_refs, (oc_ref, sel_ref) = rest[:n_tiles], rest[n_tiles:]
    t0 = pl.program_id(1) * QB
    low = _pair_masks(QB)
    blk = lax.broadcasted_iota(jnp.int32, (LANES, QB), 0)
    cur = (t0 + lax.broadcasted_iota(jnp.int32, (LANES, QB), 1)) // SEL_BLOCK
    blk_f = blk.astype(F32)
    forced = (blk == cur) | (blk == cur - 1) | (blk == 0)
    causal = blk <= cur
    keys, vals, ov_t = kcmp_ref[0], vcmp_ref[0], ov_ref[...]
    outs, scores = [], []
    for kv in range(NSA_KV_HEADS):
        psum = jnp.zeros((QB, keys.shape[0]), F32)
        outs.append([])
        for g in range(NSA_GROUP):
            r = kv * NSA_GROUP + g
            s = _nt_dot(_masked_q(qb_ref, g, kv), keys)
            s = s + jnp.concatenate([tbl_refs[c][0, r] for c in range(n_tiles)], axis=1)
            m = jnp.max(s, axis=1, keepdims=True)
            e = jnp.exp(s - m)
            den = jnp.sum(e, axis=1, keepdims=True)
            p = jnp.where(m > 0.5 * NEG, e * (1.0 / den), 0.0)
            psum = psum + p
            outs[kv].append(_dot(p.astype(BF16), vals))
        hi = psum.astype(BF16)
        lo = (psum - hi.astype(F32)).astype(BF16)
        imp_t = _nt_dot(ov_t, hi) + _nt_dot(ov_t, lo)
        scores.append(jnp.where(forced, FORCE_SCORE, jnp.where(causal, imp_t, -1.0)))

    def pick(_, carry):
        new = []
        for val, sel in carry:
            mx = jnp.max(val, axis=0, keepdims=True)
            idx = jnp.min(jnp.where(val == mx, blk_f, float(LANES)), axis=0, keepdims=True)
            hit = blk_f == idx
            new.append((jnp.where(hit, -jnp.inf, val), jnp.where(hit, 1.0, sel)))
        return tuple(new)

    picked = lax.fori_loop(0, SEL_TOPK, pick, tuple((v, jnp.zeros((LANES, QB), F32)) for v in scores))
    eye = jnp.where(lax.broadcasted_iota(jnp.int32, (QB, QB), 0) == lax.broadcasted_iota(jnp.int32, (QB, QB), 1),
                    1.0, 0.0).astype(BF16)
    for kv in range(NSA_KV_HEADS):
        sel_t = jnp.where(causal, picked[kv][1], 0.0).astype(BF16)
        sel_ref[0, kv] = _nt_dot(eye, sel_t).astype(BF16)
    for g in range(NSA_GROUP):
        oc_ref[0, :, g * LANES:(g + 1) * LANES] = jnp.where(low, outs[0][g], outs[1][g]) * _gate_tile(gates_ref, 0, g)


def _compressed_branch(qb, kcmp, vcmp, gates, f_b):
    b, t, _ = qb.shape
    ncp = kcmp.shape[1]
    n_tiles = ncp // CMP_TILE_KEYS
    tbl = _cmp_bias(f_b)
    ov = _overlap_matrix_t(ncp, LANES)

    def tbl_spec(c):
        return pl.BlockSpec((1, NSA_HEADS, QB, CMP_TILE_KEYS),
                            lambda bi, i: (jnp.clip(i - CMP_TILE_SPAN * c, -1, CMP_CONST_DELTA) + 1, 0, 0, 0))

    blockq = lambda w: pl.BlockSpec((1, QB, w), lambda bi, i: (bi, i, 0))
    batch = lambda a: pl.BlockSpec((1,) + a.shape[1:], lambda bi, i: (bi, 0, 0))
    return pl.pallas_call(
        functools.partial(_cmp_kernel, n_tiles),
        grid=(b, t // QB),
        in_specs=[blockq(DB), batch(kcmp), batch(vcmp), blockq(LANES), pl.BlockSpec(ov.shape, lambda bi, i: (0, 0))]
                 + [tbl_spec(c) for c in range(n_tiles)],
        out_specs=[blockq(DB), pl.BlockSpec((1, NSA_KV_HEADS, QB, LANES), lambda bi, i: (bi, 0, i, 0))],
        out_shape=[jax.ShapeDtypeStruct((b, t, DB), F32), jax.ShapeDtypeStruct((b, NSA_KV_HEADS, t, LANES), BF16)],
        compiler_params=pltpu.CompilerParams(dimension_semantics=("parallel", "parallel"), vmem_limit_bytes=VMEM_LIMIT),
        name="nsa_compressed",
    )(qb, kcmp, vcmp, gates, ov, *([tbl] * n_tiles))


SEL_NEAR = 13


def _sel_bias(f_b):
    n_off = SEL_NEAR + 1
    cols = QB * n_off
    rel = f_b - f_b[:, BIAS_LEN - 1:]
    lo = -(QB - 1)
    big = _toeplitz_of(_extend(rel, lo, lo + QB + cols - 1), lo, QB, cols)
    tiles = jnp.flip(big.reshape(NSA_HEADS, QB, n_off, QB).transpose(2, 0, 1, 3), axis=0)
    return jnp.concatenate([jnp.zeros((1,) + tiles.shape[1:], F32), tiles], axis=0)


def _sel_kernel(qb_ref, sel_ref, gates_ref, cfar_ref, ks_ref, vs_ref, tbl_ref, out_ref,
                qaug_ref, s_ref, acc_ref, m_ref):
    qblk = pl.program_id(1)
    low = _pair_masks(QB)
    for kv in range(NSA_KV_HEADS):
        unchosen = jnp.where(sel_ref[0, kv].astype(F32) > 0.0, 0.0, NEG)
        for g in range(NSA_GROUP):
            r = kv * NSA_GROUP + g
            rows = slice(r * QB, (r + 1) * QB)
            qaug_ref[rows, 0:LANES] = _masked_q(qb_ref, g, kv)
            qaug_ref[rows, LANES:2 * LANES] = (unchosen + cfar_ref[r:r + 1, :]).astype(BF16)
    acc_ref[...] = jnp.zeros_like(acc_ref)
    m_ref[...] = jnp.full_like(m_ref, NEG)

    def tile(j, near):
        keys = ks_ref[0, pl.ds(pl.multiple_of(j * SEL_KT, SEL_KT), SEL_KT), :]
        vals = vs_ref[0, pl.ds(pl.multiple_of(j * SEL_KT, SEL_KT), SEL_KT), :]
        s_ref[...] = _nt_dot(qaug_ref[...], keys)
        for r in range(N_ROWGROUPS):
            rows = slice(r * QB, (r + 1) * QB)
            s = s_ref[rows, :]
            if near:
                e1 = qblk - 2 * j + 1
                s = s + jnp.concatenate([tbl_ref[e1, r], tbl_ref[e1 - 1, r]], axis=1)
            m_old = m_ref[rows, :]
            m_new = jnp.maximum(m_old, jnp.max(s, axis=1, keepdims=True))
            alpha = jnp.exp(m_old - m_new)
            p = jnp.exp(s - jnp.concatenate([m_new, m_new], axis=1))
            acc_ref[rows, :] = jnp.concatenate([alpha, alpha], axis=1) * acc_ref[rows, :] + _dot(p.astype(BF16), vals)
            m_ref[rows, :] = m_new

    n_tiles = (qblk + 2) // 2
    n_far = jnp.maximum((qblk - (SEL_NEAR - 1)) // 2, 0)
    lax.fori_loop(0, n_far, lambda j, c: (tile(j, False), c)[1], 0)
    lax.fori_loop(n_far, n_tiles, lambda j, c: (tile(j, True), c)[1], 0)
    for g in range(NSA_GROUP):
        o = []
        for kv in range(NSA_KV_HEADS):
            a = acc_ref[(kv * NSA_GROUP + g) * QB:(kv * NSA_GROUP + g + 1) * QB, :]
            o.append(a[:, 0:LANES] * (1.0 / a[:, LANES:2 * LANES]))
        out_ref[0, :, g * LANES:(g + 1) * LANES] = jnp.where(low, o[0], o[1]) * _gate_tile(gates_ref, 1, g)


def _selected_branch(qb, sel, gates, ksaug, vsaug, f_b):
    b, t, _ = qb.shape
    tbl = _sel_bias(f_b)
    cfar = jnp.broadcast_to(f_b[:, BIAS_LEN - 1:], (NSA_HEADS, LANES))
    cfar = jnp.pad(cfar, ((0, 16 - NSA_HEADS), (0, 0)))
    blockq = lambda w: pl.BlockSpec((1, QB, w), lambda bi, i: (bi, i, 0))
    batch = lambda a: pl.BlockSpec((1,) + a.shape[1:], lambda bi, i: (bi, 0, 0))
    rows = N_ROWGROUPS * QB
    return pl.pallas_call(
        _sel_kernel,
        grid=(b, t // QB),
        in_specs=[blockq(DB), pl.BlockSpec((1, NSA_KV_HEADS, QB, LANES), lambda bi, i: (bi, 0, i, 0)), blockq(LANES),
                  pl.BlockSpec(cfar.shape, lambda bi, i: (0, 0)), batch(ksaug), batch(vsaug),
                  pl.BlockSpec(tbl.shape, lambda bi, i: (0, 0, 0, 0))],
        out_specs=blockq(DB),
        out_shape=jax.ShapeDtypeStruct((b, t, DB), F32),
        scratch_shapes=[pltpu.VMEM((rows, 2 * LANES), BF16), pltpu.VMEM((rows, SEL_KT), F32),
                        pltpu.VMEM((rows, 2 * LANES), F32), pltpu.VMEM((rows, LANES), F32)],
        compiler_params=pltpu.CompilerParams(dimension_semantics=("parallel", "parallel"), vmem_limit_bytes=VMEM_LIMIT),
        name="nsa_selected",
    )(qb, sel, gates, cfar, ksaug, vsaug, tbl)


WIN_KEYS = WIN + QB


def _win_bias(f_b):
    lo = WIN - (WIN_KEYS - 1)
    vals = _extend(f_b[:, :WIN], lo, WIN)
    vals = jnp.concatenate([vals, jnp.full((NSA_HEADS, lo + QB + WIN_KEYS - 1 - WIN), NEG, F32)], axis=1)
    return _toeplitz_of(vals, lo, QB, WIN_KEYS)


def _win_kernel(qb_ref, gates_ref, kw_ref, vw_ref, tbl_ref, out_ref, q_ref, s_ref):
    qblk = pl.program_id(1)
    low = _pair_masks(QB)
    for kv in range(NSA_KV_HEADS):
        for g in range(NSA_GROUP):
            r = kv * NSA_GROUP + g
            q_ref[r * QB:(r + 1) * QB, :] = _masked_q(qb_ref, g, kv)
    start = pl.multiple_of(qblk * QB, QB)
    keys = kw_ref[0, pl.ds(start, WIN_KEYS), :]
    vals = vw_ref[0, pl.ds(start, WIN_KEYS), :]
    s_ref[...] = _nt_dot(q_ref[...], keys)
    col = lax.broadcasted_iota(jnp.int32, (1, WIN_KEYS), 1)
    pad_mask = jnp.where(col + qblk * QB >= WIN, 0.0, NEG)
    outs = []
    for r in range(N_ROWGROUPS):
        s = s_ref[r * QB:(r + 1) * QB, :] + tbl_ref[r] + pad_mask
        e = jnp.exp(s - jnp.max(s, axis=1, keepdims=True))
        pv = _dot(e.astype(BF16), vals)
        outs.append(pv[:, 0:LANES] * (1.0 / pv[:, LANES:2 * LANES]))
    for g in range(NSA_GROUP):
        out_ref[0, :, g * LANES:(g + 1) * LANES] = (jnp.where(low, outs[g], outs[NSA_GROUP + g])
                                                    * _gate_tile(gates_ref, 2, g))


def _window_branch(qb, gates, kw, vwaug, f_b):
    b, t, _ = qb.shape
    tbl = _win_bias(f_b)
    kw_pad = jnp.pad(kw, ((0, 0), (WIN, 0), (0, 0)))
    vw_pad = jnp.pad(vwaug, ((0, 0), (WIN, 0), (0, 0)))
    blockq = lambda w: pl.BlockSpec((1, QB, w), lambda bi, i: (bi, i, 0))
    batch = lambda a: pl.BlockSpec((1,) + a.shape[1:], lambda bi, i: (bi, 0, 0))
    rows = N_ROWGROUPS * QB
    return pl.pallas_call(
        _win_kernel,
        grid=(b, t // QB),
        in_specs=[blockq(DB), blockq(LANES), batch(kw_pad), batch(vw_pad), pl.BlockSpec(tbl.shape, lambda bi, i: (0, 0, 0))],
        out_specs=blockq(DB),
        out_shape=jax.ShapeDtypeStruct((b, t, DB), F32),
        scratch_shapes=[pltpu.VMEM((rows, LANES), BF16), pltpu.VMEM((rows, WIN_KEYS), F32)],
        compiler_params=pltpu.CompilerParams(dimension_semantics=("parallel", "parallel"), vmem_limit_bytes=VMEM_LIMIT),
        name="nsa_window",
    )(qb, gates, kw_pad, vw_pad, tbl)


OUT_TM = 256
C_GROUP = N_EXPERTS


def _outproj_kernel(x_ref, o1_ref, l1_ref, o4_ref, l4_ref, o16_ref, l16_ref, oc_ref, os_ref, ow_ref,
                    wout_ref, g_ref, wr_ref, br_ref, h_ref, hn_ref, comb_ref):
    l1, l4, l16 = l1_ref[...], l4_ref[...], l16_ref[...]
    mx = jnp.maximum(jnp.maximum(l1, l4), l16)
    w1, w4, w16 = jnp.exp(l1 - mx), jnp.exp(l4 - mx), jnp.exp(l16 - mx)
    oa = (w1 * o1_ref[...] + w4 * o4_ref[...] + w16 * o16_ref[...]) * (1.0 / (w1 + w4 + w16))
    ob = oc_ref[...] + os_ref[...] + ow_ref[...]
    y = _dot(oa.astype(BF16), wout_ref[0:DA, :]) + _dot(ob.astype(BF16), wout_ref[DA:DA + DB, :])
    h = x_ref[...] + y
    h_ref[...] = h
    hn = h * lax.rsqrt(jnp.mean(h * h, axis=-1, keepdims=True) + EPS) * g_ref[...]
    hn_ref[...] = hn.astype(BF16)
    logits = jnp.dot(hn, wr_ref[...], precision=lax.Precision.HIGHEST, preferred_element_type=F32) + br_ref[...]
    lane = lax.broadcasted_iota(jnp.int32, logits.shape, 1)
    lane_f = lane.astype(F32)
    big = float(LANES)
    gl = jnp.where((lane >= C_GROUP) & (lane < C_GROUP + N_GROUPS), logits, -jnp.inf)
    gmax = jnp.max(gl, axis=1, keepdims=True)
    gidx = jnp.min(jnp.where(gl == gmax, lane_f, big), axis=1, keepdims=True) - C_GROUP
    gprob = 1.0 / jnp.sum(jnp.exp(gl - gmax), axis=1, keepdims=True)
    grp_of_lane = (lane // EXPERTS_PER_GROUP).astype(F32)
    el = jnp.where((lane < N_EXPERTS) & (grp_of_lane == gidx), logits, -jnp.inf)
    v1 = jnp.max(el, axis=1, keepdims=True)
    i1 = jnp.min(jnp.where(el == v1, lane_f, big), axis=1, keepdims=True)
    el2 = jnp.where(lane_f == i1, -jnp.inf, el)
    v2 = jnp.max(el2, axis=1, keepdims=True)
    i2 = jnp.min(jnp.where(el2 == v2, lane_f, big), axis=1, keepdims=True)
    e2 = jnp.exp(v2 - v1)
    p1 = 1.0 / (1.0 + e2)
    comb_ref[...] = gprob * (jnp.where(lane_f == i1, p1, 0.0) + jnp.where(lane_f == i2, e2 * p1, 0.0))


def _outproj(x2d, a_parts, b_parts, w_out_perm, gain, w_router, b_router):
    n, d = x2d.shape
    row = lambda w: pl.BlockSpec((OUT_TM, w), lambda i: (i, 0))
    full = lambda a: pl.BlockSpec(a.shape, lambda i: (0, 0))
    return pl.pallas_call(
        _outproj_kernel,
        grid=(n // OUT_TM,),
        in_specs=[row(d)] + [row(DA)] * 6 + [row(DB)] * 3 + [full(w_out_perm), pl.BlockSpec((1, d), lambda i: (0, 0)),
                                                            full(w_router), full(b_router)],
        out_specs=[row(d), row(d), row(LANES)],
        out_shape=[jax.ShapeDtypeStruct((n, d), F32), jax.ShapeDtypeStruct((n, d), BF16),
                   jax.ShapeDtypeStruct((n, LANES), F32)],
        compiler_params=pltpu.CompilerParams(dimension_semantics=("parallel",), vmem_limit_bytes=VMEM_LIMIT),
        name="outproj_router",
    )(x2d, *a_parts, *b_parts, w_out_perm, gain.reshape(1, d), w_router, b_router)


MOE_TM = 512


def _moe_kernel(h_ref, hn_ref, comb_ref, wg_ref, wu_ref, wd_ref, g_ref, out_ref, acc_ref):
    e = pl.program_id(1)

    @pl.when(e == 0)
    def _():
        acc_ref[...] = jnp.zeros_like(acc_ref)

    hn = hn_ref[...]
    gate = _dot(hn, wg_ref[0])
    up = _dot(hn, wu_ref[0])
    lane = lax.broadcasted_iota(jnp.int32, comb_ref.shape, 1)
    weight = jnp.sum(jnp.where(lane == e, comb_ref[...], 0.0), axis=1, keepdims=True)
    hidden = (gate * jax.nn.sigmoid(gate) * up * weight).astype(BF16)
    acc_ref[...] += _dot(hidden, wd_ref[0])

    @pl.when(e == pl.num_programs(1) - 1)
    def _():
        y = h_ref[...] + acc_ref[...]
        out_ref[...] = y * lax.rsqrt(jnp.mean(y * y, axis=-1, keepdims=True) + EPS) * g_ref[...]


def _moe(h, hn, comb, w_gate, w_up, w_down, gain):
    n, d = h.shape
    tm = min(MOE_TM, n)
    row = lambda w: pl.BlockSpec((tm, w), lambda i, e: (i, 0))
    return pl.pallas_call(
        _moe_kernel,
        grid=(n // tm, N_EXPERTS),
        in_specs=[row(d), row(d), row(LANES),
                  pl.BlockSpec((1, d, D_EXPERT), lambda i, e: (e, 0, 0)),
                  pl.BlockSpec((1, d, D_EXPERT), lambda i, e: (e, 0, 0)),
                  pl.BlockSpec((1, D_EXPERT, d), lambda i, e: (e, 0, 0)),
                  pl.BlockSpec((1, d), lambda i, e: (0, 0))],
        out_specs=row(d),
        out_shape=jax.ShapeDtypeStruct((n, d), F32),
        scratch_shapes=[pltpu.VMEM((tm, d), F32)],
        compiler_params=pltpu.CompilerParams(dimension_semantics=("parallel", "arbitrary"), vmem_limit_bytes=VMEM_LIMIT),
        name="moe_experts",
    )(h, hn, comb, w_gate, w_up, w_down, gain.reshape(1, d))


def _permute_w_out(w_out):
    d = w_out.shape[1]
    wb = w_out[DA:].reshape(NSA_KV_HEADS, NSA_GROUP, HEAD_DIM, d).transpose(1, 0, 2, 3).reshape(DB, d)
    return jnp.concatenate([w_out[:DA], wb], axis=0).astype(BF16)


def _router_weights(w_group, b_group, w_expert, b_expert):
    d = w_group.shape[0]
    w = jnp.concatenate([w_expert.reshape(d, N_EXPERTS), w_group], axis=1)
    b = jnp.concatenate([b_expert.reshape(N_EXPERTS), b_group])
    pad = LANES - w.shape[1]
    return jnp.pad(w, ((0, 0), (0, pad))).astype(F32), jnp.pad(b, (0, pad)).reshape(1, LANES).astype(F32)


def _layer(h, rel_bias, norm_mix, w_in, w_out, cmp_pos_k, cmp_pos_v, cmp_k_w1, cmp_k_w2, cmp_v_w1, cmp_v_w2,
           norm_ffn, w_rg, b_rg, w_re, b_re, w_gate, w_up, w_down, out_gain):
    b, t, d = h.shape
    n = b * t
    assert t % (QB * DIL_PATTERNS[-1][1]) == 0 and t // SEL_BLOCK <= LANES and n % MOE_TM == 0
    x2d = h.reshape(n, d)
    qa, ka, va, qb, kc, vc, ksaug, vsaug, kw, vwaug, gates = _inproj(x2d, norm_mix, _permute_w_in(w_in), t)
    seq = lambda a: a.reshape(b, t, a.shape[-1])
    qa, ka, va, qb, kc, vc, ksaug, vsaug, kw, vwaug, gates = map(seq, (qa, ka, va, qb, kc, vc, ksaug, vsaug, kw, vwaug, gates))
    f_a, f_b = _bias_1d(rel_bias[:, :DIL_HEADS]), _bias_1d(rel_bias[:, DIL_HEADS:])
    a_parts = []
    for _, dil in DIL_PATTERNS:
        o, lse = _dilated(qa, ka, va, _dil_bias(f_a, dil), dil)
        a_parts += [o.reshape(n, DA), lse.reshape(n, DA)]
    kcmp, vcmp = _compress(kc, vc, cmp_pos_k, cmp_pos_v, cmp_k_w1, cmp_k_w2, cmp_v_w1, cmp_v_w2)
    o_cmp, sel = _compressed_branch(qb, kcmp, vcmp, gates, f_b)
    o_sel = _selected_branch(qb, sel, gates, ksaug, vsaug, f_b)
    o_win = _window_branch(qb, gates, kw, vwaug, f_b)
    b_parts = [o.reshape(n, DB) for o in (o_cmp, o_sel, o_win)]
    w_router, b_router = _router_weights(w_rg, b_rg, w_re, b_re)
    h2, hn, comb = _outproj(x2d, a_parts, b_parts, _permute_w_out(w_out), norm_ffn, w_router, b_router)
    return _moe(h2, hn, comb, w_gate.astype(BF16), w_up.astype(BF16), w_down.astype(BF16), out_gain)


def kernel(x, rel_bias, norm_mix, w_in, w_out, cmp_pos_k, cmp_pos_v, cmp_k_w1, cmp_k_w2, cmp_v_w1, cmp_v_w2,
           norm_ffn, w_router_group, b_router_group, w_router_expert, b_router_expert, w_gate, w_up, w_down,
           norm_final):
    depth = norm_mix.shape[0]
    assert depth == 1, "the final RMSNorm is fused into the last layer's expert kernel"
    out = _layer(x, rel_bias, norm_mix[0], w_in[0], w_out[0], cmp_pos_k[0], cmp_pos_v[0], cmp_k_w1[0], cmp_k_w2[0],
                 cmp_v_w1[0], cmp_v_w2[0], norm_ffn[0], w_router_group[0], b_router_group[0], w_router_expert[0],
                 b_router_expert[0], w_gate[0], w_up[0], w_down[0], norm_final)
    return out.reshape(x.shape)
```

```python
import functools
import math

import jax
import jax.numpy as jnp
import numpy as np
from jax import lax
from jax.experimental import pallas as pl
from jax.experimental.pallas import tpu as pltpu

HEAD_DIM = 64
DIL_HEADS = 6
NSA_KV_HEADS = 2
NSA_GROUP = 5
NSA_HEADS = NSA_KV_HEADS * NSA_GROUP
N_HEADS = DIL_HEADS + NSA_HEADS
DIL_PATTERNS = ((128, 1), (512, 4), (2048, 16))
CMP_BLOCK = 32
CMP_STRIDE = 16
CMP_HIDDEN = 256
SEL_BLOCK = 64
SEL_TOPK = 16
WIN = 512
FORCE_SCORE = 1.0e4
N_BUCKETS = 32
MAX_DISTANCE = 2048
N_GROUPS = 4
EXPERTS_PER_GROUP = 4
N_EXPERTS = N_GROUPS * EXPERTS_PER_GROUP
D_EXPERT = 512
EPS = 1e-6

LANES = 128
QB = 128
NEG = -1.0e30
LOG2E = math.log2(math.e)
DA = DIL_HEADS * HEAD_DIM
DB = NSA_HEADS * HEAD_DIM
N_ROWGROUPS = NSA_HEADS
SEL_KT = 256
VMEM_LIMIT = 56 * 1024 * 1024

F32 = jnp.float32
BF16 = jnp.bfloat16
NT_DIMS = (((1,), (1,)), ((), ()))


def _nt_dot(a, b):
    return lax.dot_general(a, b, NT_DIMS, preferred_element_type=F32)


def _dot(a, b):
    return jnp.dot(a, b, preferred_element_type=F32)


def _bucket_np(dist):
    dist = np.maximum(np.asarray(dist, np.int64), 0)
    max_exact = N_BUCKETS // 2
    x = np.maximum(dist, 1).astype(np.float32) / np.float32(max_exact)
    large = max_exact + (np.log(x) / np.float32(math.log(MAX_DISTANCE / max_exact))
                         * np.float32(N_BUCKETS - max_exact)).astype(np.int32)
    large = np.minimum(large, N_BUCKETS - 1)
    return np.where(dist < max_exact, dist, large).astype(np.int32)


BIAS_LEN = 4096


def _bias_1d(rel_bias_heads):
    onehot = (_bucket_np(np.arange(BIAS_LEN))[None, :] == np.arange(N_BUCKETS)[:, None]).astype(np.float32)
    return jnp.dot(rel_bias_heads.T.astype(F32), jnp.asarray(onehot), precision=lax.Precision.HIGHEST)


def _extend(f, lo, hi):
    assert hi <= f.shape[-1]
    if lo >= 0:
        return f[..., lo:hi]
    pad = jnp.full(f.shape[:-1] + (-lo,), NEG, f.dtype)
    return jnp.concatenate([pad, f[..., :hi]], axis=-1)


def _toeplitz(w, q, c):
    n = q + c - 1
    assert w.shape[-1] == n
    lead = w.shape[:-1]
    wp = jnp.concatenate([w, jnp.zeros(lead + (1,), w.dtype)], axis=-1)
    flat = jnp.broadcast_to(wp[..., None, :], lead + (q, n + 1)).reshape(lead + (q * (n + 1),))
    return flat[..., :q * n].reshape(lead + (q, n))[..., q - 1:q - 1 + c]


def _toeplitz_of(fn_vals, lo, q, c):
    return _toeplitz(jnp.flip(fn_vals, axis=-1), q, c)


IN_TM = 512
C_QA, C_KA, C_VA = 0, DA, 2 * DA
C_QB = 3 * DA
C_KC = C_QB + DB
C_VC, C_KS, C_VS, C_KW, C_VW, C_GT = (C_KC + LANES * i for i in range(1, 7))
N_COLS = C_GT + LANES


def _permute_w_in(w_in):
    scale = 1.0 / math.sqrt(HEAD_DIM)
    sizes = [DA] * 3 + [DB] + [NSA_KV_HEADS * HEAD_DIM] * 6 + [3 * NSA_HEADS]
    offs = np.concatenate([[0], np.cumsum(sizes)])
    part = lambda i: w_in[:, offs[i]:offs[i + 1]]
    d = w_in.shape[0]
    qb = part(3).reshape(d, NSA_KV_HEADS, NSA_GROUP, HEAD_DIM).transpose(0, 2, 1, 3).reshape(d, DB)
    gt = part(10).reshape(d, NSA_KV_HEADS, NSA_GROUP, 3).transpose(0, 3, 2, 1).reshape(d, 3 * NSA_HEADS)
    gt = jnp.pad(gt, ((0, 0), (0, LANES - 3 * NSA_HEADS)))
    cols = [part(0) * scale, part(1), part(2), qb * (scale * LOG2E)] + [part(i) for i in range(4, 10)] + [gt]
    return jnp.concatenate(cols, axis=1).astype(BF16)


def _inproj_kernel(seq_len, x_ref, g_ref, w_ref, qa_ref, ka_ref, va_ref, qb_ref, kc_ref, vc_ref,
                   ksaug_ref, vs0_ref, vs1_ref, kw_ref, vw0_ref, vw1_ref, gates_ref):
    x = x_ref[...]
    xn = (x * lax.rsqrt(jnp.mean(x * x, axis=-1, keepdims=True) + EPS) * g_ref[...]).astype(BF16)
    seg = lambda a, n: _dot(xn, w_ref[:, a:a + n])
    qa_ref[...] = seg(C_QA, DA).astype(BF16)
    ka_ref[...] = seg(C_KA, DA).astype(BF16)
    va_ref[...] = seg(C_VA, DA).astype(BF16)
    qb_ref[...] = seg(C_QB, DB).astype(BF16)
    kc_ref[...] = seg(C_KC, LANES).astype(BF16)
    vc_ref[...] = seg(C_VC, LANES).astype(BF16)
    kw_ref[...] = seg(C_KW, LANES).astype(BF16)
    tm = x.shape[0]
    tok = (pl.program_id(0) * tm) % seq_len + lax.broadcasted_iota(jnp.int32, (tm, LANES), 0)
    lane = lax.broadcasted_iota(jnp.int32, (tm, LANES), 1)
    ksaug_ref[:, 0:LANES] = seg(C_KS, LANES).astype(BF16)
    ksaug_ref[:, LANES:2 * LANES] = jnp.where(lane == tok // SEL_BLOCK, 1.0, 0.0).astype(BF16)
    low = lane < HEAD_DIM
    for col, ref0, ref1 in ((C_VS, vs0_ref, vs1_ref), (C_VW, vw0_ref, vw1_ref)):
        v = seg(col, LANES)
        ref0[...] = jnp.where(low, v, 1.0).astype(BF16)
        ref1[...] = jnp.where(low, 1.0, v).astype(BF16)
    gates_ref[...] = jax.nn.sigmoid(seg(C_GT, LANES))


def _inproj(x2d, gain, w_perm, seq_len):
    n, d = x2d.shape
    row = lambda w: pl.BlockSpec((IN_TM, w), lambda i: (i, 0))
    widths = [DA, DA, DA, DB, LANES, LANES, 2 * LANES, LANES, LANES, LANES, LANES, LANES]
    out_shape = [jax.ShapeDtypeStruct((n, w), BF16) for w in widths] + [jax.ShapeDtypeStruct((n, LANES), F32)]
    return pl.pallas_call(
        functools.partial(_inproj_kernel, seq_len),
        grid=(n // IN_TM,),
        in_specs=[row(d), pl.BlockSpec((1, d), lambda i: (0, 0)), pl.BlockSpec((d, N_COLS), lambda i: (0, 0))],
        out_specs=[row(w) for w in widths] + [row(LANES)],
        out_shape=out_shape,
        compiler_params=pltpu.CompilerParams(dimension_semantics=("parallel",), vmem_limit_bytes=VMEM_LIMIT),
        name="inproj",
    )(x2d, gain.reshape(1, d), w_perm)


def _embed_pair(w, n_tok):
    c = w.shape[1]
    w4 = w.reshape(n_tok, 1, HEAD_DIM, 1, c) * jnp.eye(NSA_KV_HEADS, dtype=w.dtype).reshape(1, 2, 1, 2, 1)
    return w4.reshape(n_tok * 2 * HEAD_DIM, 2 * c)


def _gelu_tanh(x):
    return 0.5 * x * (1.0 + jnp.tanh(math.sqrt(2.0 / math.pi) * (x + 0.044715 * (x * x * x))))


def _compress_kernel(ck_ref, cv_ref, posk_ref, posv_ref, wk1a, wk1b, wk2, wv1a, wv1b, wv2,
                     kout_ref, vout_ref, shift_ref):
    ncp = ck_ref.shape[1]
    for c_ref, pos_ref, w1a, w1b, w2, out_ref in ((ck_ref, posk_ref, wk1a, wk1b, wk2, kout_ref),
                                                  (cv_ref, posv_ref, wv1a, wv1b, wv2, vout_ref)):
        c = c_ref[0].astype(F32)
        first = _dot((c + pos_ref[0:1, :]).astype(BF16), w1a[...])
        second = _dot((c + pos_ref[1:2, :]).astype(BF16), w1b[...])
        shift_ref[0:ncp, :] = second
        shift_ref[ncp:ncp + 8, :] = jnp.zeros((8, second.shape[1]), F32)
        hidden = _gelu_tanh(first + shift_ref[1:ncp + 1, :])
        out_ref[0] = _dot(hidden.astype(BF16), w2[...]).astype(BF16)


def _compress(kc, vc, pos_k, pos_v, k_w1, k_w2, v_w1, v_w2):
    b, t, _ = kc.shape
    ncp = t // CMP_STRIDE
    half = CMP_STRIDE * HEAD_DIM
    wide = CMP_STRIDE * LANES

    def prep(w1, w2, pos):
        pos_pair = jnp.broadcast_to(pos.reshape(2, CMP_STRIDE, 1, HEAD_DIM), (2, CMP_STRIDE, 2, HEAD_DIM))
        return (_embed_pair(w1[:half], CMP_STRIDE).astype(BF16), _embed_pair(w1[half:], CMP_STRIDE).astype(BF16),
                jnp.kron(jnp.eye(NSA_KV_HEADS, dtype=w2.dtype), w2).astype(BF16),
                pos_pair.reshape(2, wide).astype(F32))

    wk1a, wk1b, wk2, posk = prep(k_w1, k_w2, pos_k)
    wv1a, wv1b, wv2, posv = prep(v_w1, v_w2, pos_v)
    full = lambda a: pl.BlockSpec(a.shape, lambda i: (0,) * a.ndim)
    tok = pl.BlockSpec((1, ncp, wide), lambda i: (i, 0, 0))
    out = pl.BlockSpec((1, ncp, LANES), lambda i: (i, 0, 0))
    return pl.pallas_call(
        _compress_kernel,
        grid=(b,),
        in_specs=[tok, tok, full(posk), full(posv), full(wk1a), full(wk1b), full(wk2), full(wv1a), full(wv1b), full(wv2)],
        out_specs=[out, out],
        out_shape=[jax.ShapeDtypeStruct((b, ncp, LANES), BF16)] * 2,
        scratch_shapes=[pltpu.VMEM((ncp + 8, 2 * CMP_HIDDEN), F32)],
        compiler_params=pltpu.CompilerParams(dimension_semantics=("parallel",), vmem_limit_bytes=VMEM_LIMIT),
        name="compress",
    )(kc.reshape(b, ncp, wide), vc.reshape(b, ncp, wide), posk, posv, wk1a, wk1b, wk2, wv1a, wv1b, wv2)


def _pair_masks(rows):
    lane = lax.broadcasted_iota(jnp.int32, (rows, LANES), 1)
    return lane < HEAD_DIM


def _dil_kernel(q_ref, kp_ref, kc_ref, vp_ref, vc_ref, bias_ref, o_ref, lse_ref):
    first = pl.program_id(2) == 0
    low = _pair_masks(QB)
    lane2 = lax.broadcasted_iota(jnp.int32, (QB, 2 * QB), 1)
    prev_mask = jnp.where(jnp.logical_and(lane2 < QB, first), NEG, 0.0)
    zero = jnp.zeros((QB, LANES), BF16)
    for p in range(DIL_HEADS // 2):
        cs = slice(p * LANES, (p + 1) * LANES)
        q = q_ref[0, :, cs]
        lhs = jnp.concatenate([jnp.where(low, q, zero), jnp.where(low, zero, q)], axis=0)
        keys = jnp.concatenate([kp_ref[0, :, cs], kc_ref[0, :, cs]], axis=0)
        vals = jnp.concatenate([vp_ref[0, :, cs], vc_ref[0, :, cs]], axis=0)
        s = _nt_dot(lhs, keys)
        bias = jnp.concatenate([bias_ref[2 * p] + prev_mask, bias_ref[2 * p + 1] + prev_mask], axis=0)
        s = s + bias
        m = jnp.max(s, axis=1, keepdims=True)
        e = jnp.exp(s - m)
        l = jnp.sum(e, axis=1, keepdims=True)
        pv = _dot(e.astype(BF16), vals) * (1.0 / l)
        lse = m + jnp.log(l)
        o_ref[0, :, cs] = jnp.where(low, pv[:QB], pv[QB:])
        lse_ref[0, :, cs] = jnp.where(low, jnp.broadcast_to(lse[:QB], (QB, LANES)),
                                      jnp.broadcast_to(lse[QB:], (QB, LANES)))


def _dilated(qa, ka, va, bias, dil):
    b, t, _ = qa.shape
    tr = t // dil
    view = lambda a: a.reshape(b, tr, dil * DA)
    cur = pl.BlockSpec((1, QB, DA), lambda bi, r, i: (bi, i, r))
    prev = pl.BlockSpec((1, QB, DA), lambda bi, r, i: (bi, jnp.maximum(i - 1, 0), r))
    o, lse = pl.pallas_call(
        _dil_kernel,
        grid=(b, dil, tr // QB),
        in_specs=[cur, prev, cur, prev, cur, pl.BlockSpec(bias.shape, lambda bi, r, i: (0, 0, 0))],
        out_specs=[cur, cur],
        out_shape=[jax.ShapeDtypeStruct((b, tr, dil * DA), F32)] * 2,
        compiler_params=pltpu.CompilerParams(dimension_semantics=("parallel", "parallel", "parallel"),
                                             vmem_limit_bytes=VMEM_LIMIT),
        name=f"dilated_{dil}",
    )(view(qa), view(ka), view(ka), view(va), view(va), bias)
    return o.reshape(b, t, DA), lse.reshape(b, t, DA)


def _dil_bias(f_a, dil):
    steps = DIL_PATTERNS[0][0]
    g = f_a[:, 0:dil * steps + 1:dil]
    lo, hi = QB - (2 * QB - 1), QB + QB
    vals = jnp.concatenate([jnp.full((DIL_HEADS, -lo), NEG, F32), g, jnp.full((DIL_HEADS, hi - steps - 1), NEG, F32)], axis=1)
    return _toeplitz_of(vals, lo, QB, 2 * QB)


CMP_TILE_KEYS = LANES
CMP_TILE_SPAN = CMP_TILE_KEYS * CMP_STRIDE // QB
CMP_CONST_DELTA = 28


def _cmp_bias(f_b):
    per = QB // CMP_STRIDE
    n_rows = per * (CMP_CONST_DELTA + 1)
    m_lo, m_hi = -(CMP_TILE_KEYS - 1), n_rows
    base = _extend(f_b, CMP_STRIDE * m_lo - (CMP_BLOCK - 1), CMP_STRIDE * m_hi - (CMP_BLOCK - 1))
    g = base.reshape(NSA_HEADS, m_hi - m_lo, CMP_STRIDE).transpose(0, 2, 1)
    t = _toeplitz_of(g, m_lo, n_rows, CMP_TILE_KEYS)
    t = t.reshape(NSA_HEADS, CMP_STRIDE, CMP_CONST_DELTA + 1, per, CMP_TILE_KEYS).transpose(2, 0, 3, 1, 4)
    t = t.reshape(CMP_CONST_DELTA + 1, NSA_HEADS, QB, CMP_TILE_KEYS)
    return jnp.concatenate([jnp.full((1,) + t.shape[1:], NEG, F32), t], axis=0)


def _overlap_matrix_t(ncp, n_sel_pad):
    n = np.arange(ncp)[None, :] * CMP_STRIDE
    s = np.arange(n_sel_pad)[:, None] * SEL_BLOCK
    ov = np.clip(np.minimum(n + CMP_BLOCK, s + SEL_BLOCK) - np.maximum(n, s), 0, None) / CMP_BLOCK
    return jnp.asarray(ov, BF16)


def _gate_tile(gates_ref, branch, g):
    c = branch * NSA_HEADS + g * 2
    low = _pair_masks(QB)
    return jnp.where(low, jnp.broadcast_to(gates_ref[0, :, c:c + 1], (QB, LANES)),
                     jnp.broadcast_to(gates_ref[0, :, c + 1:c + 2], (QB, LANES)))


def _masked_q(qb_ref, g, kv):
    q = qb_ref[0, :, g * LANES:(g + 1) * LANES]
    low = _pair_masks(QB)
    keep = low if kv == 0 else jnp.logical_not(low)
    return jnp.where(keep, q, jnp.zeros_like(q))


def _cmp_kernel(n_tiles, qb_ref, kcmp_ref, vcmp_ref, gates_ref, ov_ref, *rest):
    tbl_refs, (oc_ref, sel_ref) = rest[:n_tiles], rest[n_tiles:]
    t0 = pl.program_id(1) * QB
    low = _pair_masks(QB)
    blk = lax.broadcasted_iota(jnp.int32, (LANES, QB), 0)
    cur = (t0 + lax.broadcasted_iota(jnp.int32, (LANES, QB), 1)) // SEL_BLOCK
    blk_f = blk.astype(F32)
    forced = (blk == cur) | (blk == cur - 1) | (blk == 0)
    causal = blk <= cur
    keys, vals, ov_t = kcmp_ref[0], vcmp_ref[0], ov_ref[...]
    outs, scores = [], []
    for kv in range(NSA_KV_HEADS):
        psum = jnp.zeros((QB, keys.shape[0]), F32)
        outs.append([])
        for g in range(NSA_GROUP):
            r = kv * NSA_GROUP + g
            s = _nt_dot(_masked_q(qb_ref, g, kv), keys)
            s = s + jnp.concatenate([tbl_refs[c][0, r] for c in range(n_tiles)], axis=1)
            m = jnp.max(s, axis=1, keepdims=True)
            e = jnp.exp2(s - m)
            den = jnp.sum(e, axis=1, keepdims=True)
            p = jnp.where(m > 0.5 * NEG, e * (1.0 / den), 0.0)
            psum = psum + p
            outs[kv].append(_dot(p.astype(BF16), vals))
        hi = psum.astype(BF16)
        lo = (psum - hi.astype(F32)).astype(BF16)
        imp_t = _nt_dot(ov_t, hi) + _nt_dot(ov_t, lo)
        scores.append(jnp.where(forced, FORCE_SCORE, jnp.where(causal, imp_t, -1.0)))

    def pick(_, carry):
        new = []
        for val, sel in carry:
            mx = jnp.max(val, axis=0, keepdims=True)
            idx = jnp.min(jnp.where(val == mx, blk_f, float(LANES)), axis=0, keepdims=True)
            hit = blk_f == idx
            new.append((jnp.where(hit, -jnp.inf, val), jnp.where(hit, 1.0, sel)))
        return tuple(new)

    picked = lax.fori_loop(0, SEL_TOPK, pick, tuple((v, jnp.zeros((LANES, QB), F32)) for v in scores))
    eye = jnp.where(lax.broadcasted_iota(jnp.int32, (QB, QB), 0) == lax.broadcasted_iota(jnp.int32, (QB, QB), 1),
                    1.0, 0.0).astype(BF16)
    for kv in range(NSA_KV_HEADS):
        sel_t = jnp.where(causal, picked[kv][1], 0.0).astype(BF16)
        sel_ref[0, kv] = _nt_dot(eye, sel_t).astype(BF16)
    for g in range(NSA_GROUP):
        oc_ref[0, :, g * LANES:(g + 1) * LANES] = jnp.where(low, outs[0][g], outs[1][g]) * _gate_tile(gates_ref, 0, g)


def _compressed_branch(qb, kcmp, vcmp, gates, f_b):
    b, t, _ = qb.shape
    ncp = kcmp.shape[1]
    n_tiles = ncp // CMP_TILE_KEYS
    tbl = _cmp_bias(f_b)
    ov = _overlap_matrix_t(ncp, LANES)

    def tbl_spec(c):
        return pl.BlockSpec((1, NSA_HEADS, QB, CMP_TILE_KEYS),
                            lambda bi, i: (jnp.clip(i - CMP_TILE_SPAN * c, -1, CMP_CONST_DELTA) + 1, 0, 0, 0))

    blockq = lambda w: pl.BlockSpec((1, QB, w), lambda bi, i: (bi, i, 0))
    batch = lambda a: pl.BlockSpec((1,) + a.shape[1:], lambda bi, i: (bi, 0, 0))
    return pl.pallas_call(
        functools.partial(_cmp_kernel, n_tiles),
        grid=(b, t // QB),
        in_specs=[blockq(DB), batch(kcmp), batch(vcmp), blockq(LANES), pl.BlockSpec(ov.shape, lambda bi, i: (0, 0))]
                 + [tbl_spec(c) for c in range(n_tiles)],
        out_specs=[blockq(DB), pl.BlockSpec((1, NSA_KV_HEADS, QB, LANES), lambda bi, i: (bi, 0, i, 0))],
        out_shape=[jax.ShapeDtypeStruct((b, t, DB), F32), jax.ShapeDtypeStruct((b, NSA_KV_HEADS, t, LANES), BF16)],
        compiler_params=pltpu.CompilerParams(dimension_semantics=("parallel", "parallel"), vmem_limit_bytes=VMEM_LIMIT),
        name="nsa_compressed",
    )(qb, kcmp, vcmp, gates, ov, *([tbl] * n_tiles))


SEL_NEAR = 13


def _sel_bias(f_b):
    n_off = SEL_NEAR + 1
    cols = QB * n_off
    rel = f_b - f_b[:, BIAS_LEN - 1:]
    lo = -(QB - 1)
    big = _toeplitz_of(_extend(rel, lo, lo + QB + cols - 1), lo, QB, cols)
    tiles = jnp.flip(big.reshape(NSA_HEADS, QB, n_off, QB).transpose(2, 0, 1, 3), axis=0)
    return jnp.concatenate([jnp.zeros((1,) + tiles.shape[1:], F32), tiles], axis=0)


def _pair_ratio(acc0, acc1):
    low = _pair_masks(acc0.shape[0])
    den = pltpu.roll(jnp.where(low, acc1, acc0), HEAD_DIM, axis=1)
    return jnp.where(low, acc0, acc1) * (1.0 / den)


def _sel_kernel(qb_ref, sel_ref, gates_ref, cfar_ref, ks_ref, vs0_ref, vs1_ref, tbl_ref, out_ref,
                qaug_ref, s_ref, s1_ref, acc_ref, m_ref, alpha_ref):
    qblk = pl.program_id(1)
    for kv in range(NSA_KV_HEADS):
        unchosen = jnp.where(sel_ref[0, kv].astype(F32) > 0.0, 0.0, NEG)
        for g in range(NSA_GROUP):
            r = kv * NSA_GROUP + g
            rows = slice(r * QB, (r + 1) * QB)
            qaug_ref[rows, 0:LANES] = _masked_q(qb_ref, g, kv)
            qaug_ref[rows, LANES:2 * LANES] = (unchosen + cfar_ref[r:r + 1, :]).astype(BF16)
    acc_ref[...] = jnp.zeros_like(acc_ref)
    m_ref[...] = jnp.full_like(m_ref, NEG)

    last_tile = ks_ref.shape[1] // SEL_KT - 1

    def scores(j, dst_ref):
        start = pl.multiple_of(jnp.minimum(j, last_tile) * SEL_KT, SEL_KT)
        dst_ref[...] = _nt_dot(qaug_ref[...], ks_ref[0, pl.ds(start, SEL_KT), :])

    def consume(j, src_ref, near):
        start = pl.multiple_of(j * SEL_KT, SEL_KT)
        for r in range(N_ROWGROUPS):
            rows = slice(r * QB, (r + 1) * QB)
            s = src_ref[rows, :]
            if near:
                e1 = jnp.clip(qblk - 2 * j + 1, 0, SEL_NEAR + 1)
                e2 = jnp.clip(qblk - 2 * j, 0, SEL_NEAR + 1)
                s = s + jnp.concatenate([tbl_ref[e1, r], tbl_ref[e2, r]], axis=1)
                src_ref[rows, :] = s
            m_old = m_ref[rows, :]
            m_new = jnp.maximum(m_old, jnp.max(s, axis=1, keepdims=True))
            alpha_ref[rows, :] = jnp.exp2(m_old - m_new)
            m_ref[rows, :] = m_new
        for r in range(N_ROWGROUPS):
            rows = slice(r * QB, (r + 1) * QB)
            vals = (vs0_ref if r < NSA_GROUP else vs1_ref)[0, pl.ds(start, SEL_KT), :]
            m_new = m_ref[rows, :]
            p = jnp.exp2(src_ref[rows, :] - jnp.concatenate([m_new, m_new], axis=1))
            acc_ref[rows, :] = alpha_ref[rows, :] * acc_ref[rows, :] + _dot(p.astype(BF16), vals)

    def tile_pair(i, near):
        scores(2 * i + 1, s1_ref)
        consume(2 * i, s_ref, near)
        scores(2 * i + 2, s_ref)
        consume(2 * i + 1, s1_ref, near)

    n_pairs = ((qblk + 2) // 2 + 1) // 2
    n_far = jnp.maximum((qblk - (SEL_NEAR - 1)) // 2, 0) // 2
    scores(0, s_ref)
    lax.fori_loop(0, n_far, lambda i, c: (tile_pair(i, False), c)[1], 0)
    lax.fori_loop(n_far, n_pairs, lambda i, c: (tile_pair(i, True), c)[1], 0)
    for g in range(NSA_GROUP):
        ratio = _pair_ratio(acc_ref[g * QB:(g + 1) * QB, :], acc_ref[(NSA_GROUP + g) * QB:(NSA_GROUP + g + 1) * QB, :])
        out_ref[0, :, g * LANES:(g + 1) * LANES] = ratio * _gate_tile(gates_ref, 1, g)


def _selected_branch(qb, sel, gates, ksaug, vs0aug, vs1aug, f_b):
    b, t, _ = qb.shape
    tbl = _sel_bias(f_b)
    cfar = jnp.broadcast_to(f_b[:, BIAS_LEN - 1:], (NSA_HEADS, LANES))
    cfar = jnp.pad(cfar, ((0, 16 - NSA_HEADS), (0, 0)))
    blockq = lambda w: pl.BlockSpec((1, QB, w), lambda bi, i: (bi, i, 0))
    batch = lambda a: pl.BlockSpec((1,) + a.shape[1:], lambda bi, i: (bi, 0, 0))
    rows = N_ROWGROUPS * QB
    return pl.pallas_call(
        _sel_kernel,
        grid=(b, t // QB),
        in_specs=[blockq(DB), pl.BlockSpec((1, NSA_KV_HEADS, QB, LANES), lambda bi, i: (bi, 0, i, 0)), blockq(LANES),
                  pl.BlockSpec(cfar.shape, lambda bi, i: (0, 0)), batch(ksaug), batch(vs0aug), batch(vs1aug),
                  pl.BlockSpec(tbl.shape, lambda bi, i: (0, 0, 0, 0))],
        out_specs=blockq(DB),
        out_shape=jax.ShapeDtypeStruct((b, t, DB), F32),
        scratch_shapes=[pltpu.VMEM((rows, 2 * LANES), BF16), pltpu.VMEM((rows, SEL_KT), F32), pltpu.VMEM((rows, SEL_KT), F32),
                        pltpu.VMEM((rows, LANES), F32), pltpu.VMEM((rows, LANES), F32), pltpu.VMEM((rows, LANES), F32)],
        compiler_params=pltpu.CompilerParams(dimension_semantics=("parallel", "parallel"), vmem_limit_bytes=VMEM_LIMIT),
        name="nsa_selected",
    )(qb, sel, gates, cfar, ksaug, vs0aug, vs1aug, tbl)


WIN_KEYS = WIN + QB


def _win_bias(f_b):
    lo = WIN - (WIN_KEYS - 1)
    vals = _extend(f_b[:, :WIN], lo, WIN)
    vals = jnp.concatenate([vals, jnp.full((NSA_HEADS, lo + QB + WIN_KEYS - 1 - WIN), NEG, F32)], axis=1)
    return _toeplitz_of(vals, lo, QB, WIN_KEYS)


def _win_kernel(qb_ref, gates_ref, kw_ref, vw0_ref, vw1_ref, tbl_ref, out_ref, q_ref, s_ref):
    qblk = pl.program_id(1)
    for kv in range(NSA_KV_HEADS):
        for g in range(NSA_GROUP):
            r = kv * NSA_GROUP + g
            q_ref[r * QB:(r + 1) * QB, :] = _masked_q(qb_ref, g, kv)
    start = pl.multiple_of(qblk * QB, QB)
    s_ref[...] = _nt_dot(q_ref[...], kw_ref[0, pl.ds(start, WIN_KEYS), :])
    col = lax.broadcasted_iota(jnp.int32, (1, WIN_KEYS), 1)
    pad_mask = jnp.where(col + qblk * QB >= WIN, 0.0, NEG)
    outs = []
    for r in range(N_ROWGROUPS):
        vals = (vw0_ref if r < NSA_GROUP else vw1_ref)[0, pl.ds(start, WIN_KEYS), :]
        s = s_ref[r * QB:(r + 1) * QB, :] + tbl_ref[r] + pad_mask
        e = jnp.exp2(s - jnp.max(s, axis=1, keepdims=True))
        outs.append(_dot(e.astype(BF16), vals))
    for g in range(NSA_GROUP):
        out_ref[0, :, g * LANES:(g + 1) * LANES] = _pair_ratio(outs[g], outs[NSA_GROUP + g]) * _gate_tile(gates_ref, 2, g)


def _window_branch(qb, gates, kw, vw0aug, vw1aug, f_b):
    b, t, _ = qb.shape
    tbl = _win_bias(f_b)
    pad_front = lambda a: jnp.pad(a, ((0, 0), (WIN, 0), (0, 0)))
    kw_pad, vw0_pad, vw1_pad = pad_front(kw), pad_front(vw0aug), pad_front(vw1aug)
    blockq = lambda w: pl.BlockSpec((1, QB, w), lambda bi, i: (bi, i, 0))
    batch = lambda a: pl.BlockSpec((1,) + a.shape[1:], lambda bi, i: (bi, 0, 0))
    rows = N_ROWGROUPS * QB
    return pl.pallas_call(
        _win_kernel,
        grid=(b, t // QB),
        in_specs=[blockq(DB), blockq(LANES), batch(kw_pad), batch(vw0_pad), batch(vw1_pad),pl.BlockSpec(tbl.shape, lambda bi, i: (0, 0, 0))],
        out_specs=blockq(DB),
        out_shape=jax.ShapeDtypeStruct((b, t, DB), F32),
        scratch_shapes=[pltpu.VMEM((rows, LANES), BF16), pltpu.VMEM((rows, WIN_KEYS), F32)],
        compiler_params=pltpu.CompilerParams(dimension_semantics=("parallel", "parallel"), vmem_limit_bytes=VMEM_LIMIT),
        name="nsa_window",
    )(qb, gates, kw_pad, vw0_pad, vw1_pad, tbl)


OUT_TM = 256
C_GROUP = N_EXPERTS


def _outproj_kernel(x_ref, o1_ref, l1_ref, o4_ref, l4_ref, o16_ref, l16_ref, oc_ref, os_ref, ow_ref,
                    wout_ref, g_ref, wr_ref, br_ref, h_ref, hn_ref, comb_ref):
    l1, l4, l16 = l1_ref[...], l4_ref[...], l16_ref[...]
    mx = jnp.maximum(jnp.maximum(l1, l4), l16)
    w1, w4, w16 = jnp.exp(l1 - mx), jnp.exp(l4 - mx), jnp.exp(l16 - mx)
    oa = (w1 * o1_ref[...] + w4 * o4_ref[...] + w16 * o16_ref[...]) * (1.0 / (w1 + w4 + w16))
    ob = oc_ref[...] + os_ref[...] + ow_ref[...]
    y = _dot(oa.astype(BF16), wout_ref[0:DA, :]) + _dot(ob.astype(BF16), wout_ref[DA:DA + DB, :])
    h = x_ref[...] + y
    h_ref[...] = h
    hn = h * lax.rsqrt(jnp.mean(h * h, axis=-1, keepdims=True) + EPS) * g_ref[...]
    hn_ref[...] = hn.astype(BF16)
    logits = jnp.dot(hn, wr_ref[...], precision=lax.Precision.HIGHEST, preferred_element_type=F32) + br_ref[...]
    lane = lax.broadcasted_iota(jnp.int32, logits.shape, 1)
    lane_f = lane.astype(F32)
    big = float(LANES)
    gl = jnp.where((lane >= C_GROUP) & (lane < C_GROUP + N_GROUPS), logits, -jnp.inf)
    gmax = jnp.max(gl, axis=1, keepdims=True)
    gidx = jnp.min(jnp.where(gl == gmax, lane_f, big), axis=1, keepdims=True) - C_GROUP
    gprob = 1.0 / jnp.sum(jnp.exp(gl - gmax), axis=1, keepdims=True)
    grp_of_lane = (lane // EXPERTS_PER_GROUP).astype(F32)
    el = jnp.where((lane < N_EXPERTS) & (grp_of_lane == gidx), logits, -jnp.inf)
    v1 = jnp.max(el, axis=1, keepdims=True)
    i1 = jnp.min(jnp.where(el == v1, lane_f, big), axis=1, keepdims=True)
    el2 = jnp.where(lane_f == i1, -jnp.inf, el)
    v2 = jnp.max(el2, axis=1, keepdims=True)
    i2 = jnp.min(jnp.where(el2 == v2, lane_f, big), axis=1, keepdims=True)
    e2 = jnp.exp(v2 - v1)
    p1 = 1.0 / (1.0 + e2)
    comb_ref[...] = gprob * (jnp.where(lane_f == i1, p1, 0.0) + jnp.where(lane_f == i2, e2 * p1, 0.0))


def _outproj(x2d, a_parts, b_parts, w_out_perm, gain, w_router, b_router):
    n, d = x2d.shape
    row = lambda w: pl.BlockSpec((OUT_TM, w), lambda i: (i, 0))
    full = lambda a: pl.BlockSpec(a.shape, lambda i: (0, 0))
    return pl.pallas_call(
        _outproj_kernel,
        grid=(n // OUT_TM,),
        in_specs=[row(d)] + [row(DA)] * 6 + [row(DB)] * 3 + [full(w_out_perm), pl.BlockSpec((1, d), lambda i: (0, 0)),
                                                            full(w_router), full(b_router)],
        out_specs=[row(d), row(d), row(LANES)],
        out_shape=[jax.ShapeDtypeStruct((n, d), F32), jax.ShapeDtypeStruct((n, d), BF16),
                   jax.ShapeDtypeStruct((n, LANES), F32)],
        compiler_params=pltpu.CompilerParams(dimension_semantics=("parallel",), vmem_limit_bytes=VMEM_LIMIT),
        name="outproj_router",
    )(x2d, *a_parts, *b_parts, w_out_perm, gain.reshape(1, d), w_router, b_router)


MOE_TM = 512


def _moe_kernel(h_ref, hn_ref, comb_ref, wg_ref, wu_ref, wd_ref, g_ref, out_ref, acc_ref):
    e = pl.program_id(1)

    @pl.when(e == 0)
    def _():
        acc_ref[...] = jnp.zeros_like(acc_ref)

    hn = hn_ref[...]
    gate = _dot(hn, wg_ref[0])
    up = _dot(hn, wu_ref[0])
    lane = lax.broadcasted_iota(jnp.int32, comb_ref.shape, 1)
    weight = jnp.sum(jnp.where(lane == e, comb_ref[...], 0.0), axis=1, keepdims=True)
    hidden = (gate * jax.nn.sigmoid(gate) * up * weight).astype(BF16)
    acc_ref[...] += _dot(hidden, wd_ref[0])

    @pl.when(e == pl.num_programs(1) - 1)
    def _():
        y = h_ref[...] + acc_ref[...]
        out_ref[...] = y * lax.rsqrt(jnp.mean(y * y, axis=-1, keepdims=True) + EPS) * g_ref[...]


def _moe(h, hn, comb, w_gate, w_up, w_down, gain):
    n, d = h.shape
    tm = min(MOE_TM, n)
    row = lambda w: pl.BlockSpec((tm, w), lambda i, e: (i, 0))
    return pl.pallas_call(
        _moe_kernel,
        grid=(n // tm, N_EXPERTS),
        in_specs=[row(d), row(d), row(LANES),
                  pl.BlockSpec((1, d, D_EXPERT), lambda i, e: (e, 0, 0)),
                  pl.BlockSpec((1, d, D_EXPERT), lambda i, e: (e, 0, 0)),
                  pl.BlockSpec((1, D_EXPERT, d), lambda i, e: (e, 0, 0)),
                  pl.BlockSpec((1, d), lambda i, e: (0, 0))],
        out_specs=row(d),
        out_shape=jax.ShapeDtypeStruct((n, d), F32),
        scratch_shapes=[pltpu.VMEM((tm, d), F32)],
        compiler_params=pltpu.CompilerParams(dimension_semantics=("parallel", "arbitrary"), vmem_limit_bytes=VMEM_LIMIT),
        name="moe_experts",
    )(h, hn, comb, w_gate, w_up, w_down, gain.reshape(1, d))


def _permute_w_out(w_out):
    d = w_out.shape[1]
    wb = w_out[DA:].reshape(NSA_KV_HEADS, NSA_GROUP, HEAD_DIM, d).transpose(1, 0, 2, 3).reshape(DB, d)
    return jnp.concatenate([w_out[:DA], wb], axis=0).astype(BF16)


def _router_weights(w_group, b_group, w_expert, b_expert):
    d = w_group.shape[0]
    w = jnp.concatenate([w_expert.reshape(d, N_EXPERTS), w_group], axis=1)
    b = jnp.concatenate([b_expert.reshape(N_EXPERTS), b_group])
    pad = LANES - w.shape[1]
    return jnp.pad(w, ((0, 0), (0, pad))).astype(F32), jnp.pad(b, (0, pad)).reshape(1, LANES).astype(F32)


def _layer(h, rel_bias, norm_mix, w_in, w_out, cmp_pos_k, cmp_pos_v, cmp_k_w1, cmp_k_w2, cmp_v_w1, cmp_v_w2,
           norm_ffn, w_rg, b_rg, w_re, b_re, w_gate, w_up, w_down, out_gain):
    b, t, d = h.shape
    n = b * t
    assert t % (QB * DIL_PATTERNS[-1][1]) == 0 and t // SEL_BLOCK <= LANES and n % MOE_TM == 0
    x2d = h.reshape(n, d)
    seq = lambda a: a.reshape(b, t, a.shape[-1])
    qa, ka, va, qb, kc, vc, ksaug, vs0aug, vs1aug, kw, vw0aug, vw1aug, gates = map(
        seq, _inproj(x2d, norm_mix, _permute_w_in(w_in), t))
    f_a = _bias_1d(rel_bias[:, :DIL_HEADS])
    f_b = _bias_1d(rel_bias[:, DIL_HEADS:]) * LOG2E
    a_parts = []
    for _, dil in DIL_PATTERNS:
        o, lse = _dilated(qa, ka, va, _dil_bias(f_a, dil), dil)
        a_parts += [o.reshape(n, DA), lse.reshape(n, DA)]
    kcmp, vcmp = _compress(kc, vc, cmp_pos_k, cmp_pos_v, cmp_k_w1, cmp_k_w2, cmp_v_w1, cmp_v_w2)
    o_cmp, sel = _compressed_branch(qb, kcmp, vcmp, gates, f_b)
    o_sel = _selected_branch(qb, sel, gates, ksaug, vs0aug, vs1aug, f_b)
    o_win = _window_branch(qb, gates, kw, vw0aug, vw1aug, f_b)
    b_parts = [o.reshape(n, DB) for o in (o_cmp, o_sel, o_win)]
    w_router, b_router = _router_weights(w_rg, b_rg, w_re, b_re)
    h2, hn, comb = _outproj(x2d, a_parts, b_parts, _permute_w_out(w_out), norm_ffn, w_router, b_router)
    return _moe(h2, hn, comb, w_gate.astype(BF16), w_up.astype(BF16), w_down.astype(BF16), out_gain)


def kernel(x, rel_bias, norm_mix, w_in, w_out, cmp_pos_k, cmp_pos_v, cmp_k_w1, cmp_k_w2, cmp_v_w1, cmp_v_w2,
           norm_ffn, w_router_group, b_router_group, w_router_expert, b_router_expert, w_gate, w_up, w_down,
           norm_final):
    depth = norm_mix.shape[0]
    assert depth == 1, "the final RMSNorm is fused into the last layer's expert kernel"
    out = _layer(x, rel_bias, norm_mix[0], w_in[0], w_out[0], cmp_pos_k[0], cmp_pos_v[0], cmp_k_w1[0], cmp_k_w2[0],
                 cmp_v_w1[0], cmp_v_w2[0], norm_ffn[0], w_router_group[0], b_router_group[0], w_router_expert[0],
                 b_router_expert[0], w_gate[0], w_up[0], w_down[0], norm_final)
    return out.reshape(x.shape)
```

```python
import functools
import math

import jax
import jax.numpy as jnp
import numpy as np
from jax import lax
from jax.experimental import pallas as pl
from jax.experimental.pallas import tpu as pltpu

HEAD_DIM = 64
DIL_HEADS = 6
NSA_KV_HEADS = 2
NSA_GROUP = 5
NSA_HEADS = NSA_KV_HEADS * NSA_GROUP
N_HEADS = DIL_HEADS + NSA_HEADS
DIL_PATTERNS = ((128, 1), (512, 4), (2048, 16))
CMP_BLOCK = 32
CMP_STRIDE = 16
CMP_HIDDEN = 256
SEL_BLOCK = 64
SEL_TOPK = 16
WIN = 512
FORCE_SCORE = 1.0e4
N_BUCKETS = 32
MAX_DISTANCE = 2048
N_GROUPS = 4
EXPERTS_PER_GROUP = 4
N_EXPERTS = N_GROUPS * EXPERTS_PER_GROUP
D_EXPERT = 512
EPS = 1e-6

LANES = 128
QB = 128
NEG = -1.0e30
LOG2E = math.log2(math.e)
DA = DIL_HEADS * HEAD_DIM
DB = NSA_HEADS * HEAD_DIM
N_ROWGROUPS = NSA_HEADS
SEL_KT = 256
VMEM_LIMIT = 56 * 1024 * 1024

F32 = jnp.float32
BF16 = jnp.bfloat16
NT_DIMS = (((1,), (1,)), ((), ()))


def _nt_dot(a, b):
    return lax.dot_general(a, b, NT_DIMS, preferred_element_type=F32)


def _dot(a, b):
    return jnp.dot(a, b, preferred_element_type=F32)


def _bucket_np(dist):
    dist = np.maximum(np.asarray(dist, np.int64), 0)
    max_exact = N_BUCKETS // 2
    x = np.maximum(dist, 1).astype(np.float32) / np.float32(max_exact)
    large = max_exact + (np.log(x) / np.float32(math.log(MAX_DISTANCE / max_exact))
                         * np.float32(N_BUCKETS - max_exact)).astype(np.int32)
    large = np.minimum(large, N_BUCKETS - 1)
    return np.where(dist < max_exact, dist, large).astype(np.int32)


BIAS_LEN = 4096


def _bias_1d(rel_bias_heads):
    onehot = (_bucket_np(np.arange(BIAS_LEN))[None, :] == np.arange(N_BUCKETS)[:, None]).astype(np.float32)
    return jnp.dot(rel_bias_heads.T.astype(F32), jnp.asarray(onehot), precision=lax.Precision.HIGHEST)


def _extend(f, lo, hi):
    assert hi <= f.shape[-1]
    if lo >= 0:
        return f[..., lo:hi]
    pad = jnp.full(f.shape[:-1] + (-lo,), NEG, f.dtype)
    return jnp.concatenate([pad, f[..., :hi]], axis=-1)


def _toeplitz(w, q, c):
    n = q + c - 1
    assert w.shape[-1] == n
    lead = w.shape[:-1]
    wp = jnp.concatenate([w, jnp.zeros(lead + (1,), w.dtype)], axis=-1)
    flat = jnp.broadcast_to(wp[..., None, :], lead + (q, n + 1)).reshape(lead + (q * (n + 1),))
    return flat[..., :q * n].reshape(lead + (q, n))[..., q - 1:q - 1 + c]


def _toeplitz_of(fn_vals, lo, q, c):
    return _toeplitz(jnp.flip(fn_vals, axis=-1), q, c)


IN_TM = 512
C_QA, C_KA, C_VA = 0, DA, 2 * DA
C_QB = 3 * DA
C_KC = C_QB + DB
C_VC, C_KS, C_VS, C_KW, C_VW, C_GT = (C_KC + LANES * i for i in range(1, 7))
N_COLS = C_GT + LANES


def _permute_w_in(w_in):
    scale = 1.0 / math.sqrt(HEAD_DIM)
    sizes = [DA] * 3 + [DB] + [NSA_KV_HEADS * HEAD_DIM] * 6 + [3 * NSA_HEADS]
    offs = np.concatenate([[0], np.cumsum(sizes)])
    part = lambda i: w_in[:, offs[i]:offs[i + 1]]
    d = w_in.shape[0]
    qb = part(3).reshape(d, NSA_KV_HEADS, NSA_GROUP, HEAD_DIM).transpose(0, 2, 1, 3).reshape(d, DB)
    gt = part(10).reshape(d, NSA_KV_HEADS, NSA_GROUP, 3).transpose(0, 3, 2, 1).reshape(d, 3 * NSA_HEADS)
    gt = jnp.pad(gt, ((0, 0), (0, LANES - 3 * NSA_HEADS)))
    cols = [part(0) * scale, part(1), part(2), qb * (scale * LOG2E)] + [part(i) for i in range(4, 10)] + [gt]
    return jnp.concatenate(cols, axis=1).astype(BF16)


def _inproj_kernel(seq_len, x_ref, g_ref, w_ref, qa_ref, ka_ref, va_ref, qb_ref, kc_ref, vc_ref,
                   ksaug_ref, vs0_ref, vs1_ref, kw_ref, vw0_ref, vw1_ref, gates_ref):
    x = x_ref[...]
    xn = (x * lax.rsqrt(jnp.mean(x * x, axis=-1, keepdims=True) + EPS) * g_ref[...]).astype(BF16)
    seg = lambda a, n: _dot(xn, w_ref[:, a:a + n])
    qa_ref[...] = seg(C_QA, DA).astype(BF16)
    ka_ref[...] = seg(C_KA, DA).astype(BF16)
    va_ref[...] = seg(C_VA, DA).astype(BF16)
    qb_ref[...] = seg(C_QB, DB).astype(BF16)
    kc_ref[...] = seg(C_KC, LANES).astype(BF16)
    vc_ref[...] = seg(C_VC, LANES).astype(BF16)
    kw_ref[...] = seg(C_KW, LANES).astype(BF16)
    tm = x.shape[0]
    tok = (pl.program_id(0) * tm) % seq_len + lax.broadcasted_iota(jnp.int32, (tm, LANES), 0)
    lane = lax.broadcasted_iota(jnp.int32, (tm, LANES), 1)
    ksaug_ref[:, 0:LANES] = seg(C_KS, LANES).astype(BF16)
    ksaug_ref[:, LANES:2 * LANES] = jnp.where(lane == tok // SEL_BLOCK, 1.0, 0.0).astype(BF16)
    low = lane < HEAD_DIM
    for col, ref0, ref1 in ((C_VS, vs0_ref, vs1_ref), (C_VW, vw0_ref, vw1_ref)):
        v = seg(col, LANES)
        ref0[...] = jnp.where(low, v, 1.0).astype(BF16)
        ref1[...] = jnp.where(low, 1.0, v).astype(BF16)
    gates_ref[...] = jax.nn.sigmoid(seg(C_GT, LANES))


def _inproj(x2d, gain, w_perm, seq_len):
    n, d = x2d.shape
    row = lambda w: pl.BlockSpec((IN_TM, w), lambda i: (i, 0))
    widths = [DA, DA, DA, DB, LANES, LANES, 2 * LANES, LANES, LANES, LANES, LANES, LANES]
    out_shape = [jax.ShapeDtypeStruct((n, w), BF16) for w in widths] + [jax.ShapeDtypeStruct((n, LANES), F32)]
    return pl.pallas_call(
        functools.partial(_inproj_kernel, seq_len),
        grid=(n // IN_TM,),
        in_specs=[row(d), pl.BlockSpec((1, d), lambda i: (0, 0)), pl.BlockSpec((d, N_COLS), lambda i: (0, 0))],
        out_specs=[row(w) for w in widths] + [row(LANES)],
        out_shape=out_shape,
        compiler_params=pltpu.CompilerParams(dimension_semantics=("parallel",), vmem_limit_bytes=VMEM_LIMIT),
        name="inproj",
    )(x2d, gain.reshape(1, d), w_perm)


def _embed_pair(w, n_tok):
    c = w.shape[1]
    w4 = w.reshape(n_tok, 1, HEAD_DIM, 1, c) * jnp.eye(NSA_KV_HEADS, dtype=w.dtype).reshape(1, 2, 1, 2, 1)
    return w4.reshape(n_tok * 2 * HEAD_DIM, 2 * c)


def _gelu_tanh(x):
    return 0.5 * x * (1.0 + jnp.tanh(math.sqrt(2.0 / math.pi) * (x + 0.044715 * (x * x * x))))


def _compress_kernel(ck_ref, cv_ref, posk_ref, posv_ref, wk1a, wk1b, wk2, wv1a, wv1b, wv2,
                     kout_ref, vout_ref, shift_ref):
    ncp = ck_ref.shape[1]
    for c_ref, pos_ref, w1a, w1b, w2, out_ref in ((ck_ref, posk_ref, wk1a, wk1b, wk2, kout_ref),
                                                  (cv_ref, posv_ref, wv1a, wv1b, wv2, vout_ref)):
        c = c_ref[0].astype(F32)
        first = _dot((c + pos_ref[0:1, :]).astype(BF16), w1a[...])
        second = _dot((c + pos_ref[1:2, :]).astype(BF16), w1b[...])
        shift_ref[0:ncp, :] = second
        shift_ref[ncp:ncp + 8, :] = jnp.zeros((8, second.shape[1]), F32)
        hidden = _gelu_tanh(first + shift_ref[1:ncp + 1, :])
        out_ref[0] = _dot(hidden.astype(BF16), w2[...]).astype(BF16)


def _compress(kc, vc, pos_k, pos_v, k_w1, k_w2, v_w1, v_w2):
    b, t, _ = kc.shape
    ncp = t // CMP_STRIDE
    half = CMP_STRIDE * HEAD_DIM
    wide = CMP_STRIDE * LANES

    def prep(w1, w2, pos):
        pos_pair = jnp.broadcast_to(pos.reshape(2, CMP_STRIDE, 1, HEAD_DIM), (2, CMP_STRIDE, 2, HEAD_DIM))
        return (_embed_pair(w1[:half], CMP_STRIDE).astype(BF16), _embed_pair(w1[half:], CMP_STRIDE).astype(BF16),
                jnp.kron(jnp.eye(NSA_KV_HEADS, dtype=w2.dtype), w2).astype(BF16),
                pos_pair.reshape(2, wide).astype(F32))

    wk1a, wk1b, wk2, posk = prep(k_w1, k_w2, pos_k)
    wv1a, wv1b, wv2, posv = prep(v_w1, v_w2, pos_v)
    full = lambda a: pl.BlockSpec(a.shape, lambda i: (0,) * a.ndim)
    tok = pl.BlockSpec((1, ncp, wide), lambda i: (i, 0, 0))
    out = pl.BlockSpec((1, ncp, LANES), lambda i: (i, 0, 0))
    return pl.pallas_call(
        _compress_kernel,
        grid=(b,),
        in_specs=[tok, tok, full(posk), full(posv), full(wk1a), full(wk1b), full(wk2), full(wv1a), full(wv1b), full(wv2)],
        out_specs=[out, out],
        out_shape=[jax.ShapeDtypeStruct((b, ncp, LANES), BF16)] * 2,
        scratch_shapes=[pltpu.VMEM((ncp + 8, 2 * CMP_HIDDEN), F32)],
        compiler_params=pltpu.CompilerParams(dimension_semantics=("parallel",), vmem_limit_bytes=VMEM_LIMIT),
        name="compress",
    )(kc.reshape(b, ncp, wide), vc.reshape(b, ncp, wide), posk, posv, wk1a, wk1b, wk2, wv1a, wv1b, wv2)


def _pair_masks(rows):
    lane = lax.broadcasted_iota(jnp.int32, (rows, LANES), 1)
    return lane < HEAD_DIM


def _dil_kernel(q_ref, kp_ref, kc_ref, vp_ref, vc_ref, bias_ref, o_ref, lse_ref):
    first = pl.program_id(2) == 0
    low = _pair_masks(QB)
    lane2 = lax.broadcasted_iota(jnp.int32, (QB, 2 * QB), 1)
    prev_mask = jnp.where(jnp.logical_and(lane2 < QB, first), NEG, 0.0)
    zero = jnp.zeros((QB, LANES), BF16)
    for p in range(DIL_HEADS // 2):
        cs = slice(p * LANES, (p + 1) * LANES)
        q = q_ref[0, :, cs]
        lhs = jnp.concatenate([jnp.where(low, q, zero), jnp.where(low, zero, q)], axis=0)
        keys = jnp.concatenate([kp_ref[0, :, cs], kc_ref[0, :, cs]], axis=0)
        vals = jnp.concatenate([vp_ref[0, :, cs], vc_ref[0, :, cs]], axis=0)
        s = _nt_dot(lhs, keys)
        bias = jnp.concatenate([bias_ref[2 * p] + prev_mask, bias_ref[2 * p + 1] + prev_mask], axis=0)
        s = s + bias
        m = jnp.max(s, axis=1, keepdims=True)
        e = jnp.exp(s - m)
        l = jnp.sum(e, axis=1, keepdims=True)
        pv = _dot(e.astype(BF16), vals) * (1.0 / l)
        lse = m + jnp.log(l)
        o_ref[0, :, cs] = jnp.where(low, pv[:QB], pv[QB:])
        lse_ref[0, :, cs] = jnp.where(low, jnp.broadcast_to(lse[:QB], (QB, LANES)),
                                      jnp.broadcast_to(lse[QB:], (QB, LANES)))


def _dilated(qa, ka, va, bias, dil):
    b, t, _ = qa.shape
    tr = t // dil
    view = lambda a: a.reshape(b, tr, dil * DA)
    cur = pl.BlockSpec((1, QB, DA), lambda bi, r, i: (bi, i, r))
    prev = pl.BlockSpec((1, QB, DA), lambda bi, r, i: (bi, jnp.maximum(i - 1, 0), r))
    o, lse = pl.pallas_call(
        _dil_kernel,
        grid=(b, dil, tr // QB),
        in_specs=[cur, prev, cur, prev, cur, pl.BlockSpec(bias.shape, lambda bi, r, i: (0, 0, 0))],
        out_specs=[cur, cur],
        out_shape=[jax.ShapeDtypeStruct((b, tr, dil * DA), F32)] * 2,
        compiler_params=pltpu.CompilerParams(dimension_semantics=("parallel", "parallel", "parallel"),
                                             vmem_limit_bytes=VMEM_LIMIT),
        name=f"dilated_{dil}",
    )(view(qa), view(ka), view(ka), view(va), view(va), bias)
    return o.reshape(b, t, DA), lse.reshape(b, t, DA)


def _dil_bias(f_a, dil):
    steps = DIL_PATTERNS[0][0]
    g = f_a[:, 0:dil * steps + 1:dil]
    lo, hi = QB - (2 * QB - 1), QB + QB
    vals = jnp.concatenate([jnp.full((DIL_HEADS, -lo), NEG, F32), g, jnp.full((DIL_HEADS, hi - steps - 1), NEG, F32)], axis=1)
    return _toeplitz_of(vals, lo, QB, 2 * QB)


CMP_TILE_KEYS = LANES
CMP_TILE_SPAN = CMP_TILE_KEYS * CMP_STRIDE // QB
CMP_CONST_DELTA = 28


def _cmp_bias(f_b):
    per = QB // CMP_STRIDE
    n_rows = per * (CMP_CONST_DELTA + 1)
    m_lo, m_hi = -(CMP_TILE_KEYS - 1), n_rows
    base = _extend(f_b, CMP_STRIDE * m_lo - (CMP_BLOCK - 1), CMP_STRIDE * m_hi - (CMP_BLOCK - 1))
    g = base.reshape(NSA_HEADS, m_hi - m_lo, CMP_STRIDE).transpose(0, 2, 1)
    t = _toeplitz_of(g, m_lo, n_rows, CMP_TILE_KEYS)
    t = t.reshape(NSA_HEADS, CMP_STRIDE, CMP_CONST_DELTA + 1, per, CMP_TILE_KEYS).transpose(2, 0, 3, 1, 4)
    t = t.reshape(CMP_CONST_DELTA + 1, NSA_HEADS, QB, CMP_TILE_KEYS)
    return jnp.concatenate([jnp.full((1,) + t.shape[1:], NEG, F32), t], axis=0)


def _overlap_matrix_t(ncp, n_sel_pad):
    n = np.arange(ncp)[None, :] * CMP_STRIDE
    s = np.arange(n_sel_pad)[:, None] * SEL_BLOCK
    ov = np.clip(np.minimum(n + CMP_BLOCK, s + SEL_BLOCK) - np.maximum(n, s), 0, None) / CMP_BLOCK
    return jnp.asarray(ov, BF16)


def _gate_tile(gates_ref, branch, g):
    c = branch * NSA_HEADS + g * 2
    low = _pair_masks(QB)
    return jnp.where(low, jnp.broadcast_to(gates_ref[0, :, c:c + 1], (QB, LANES)),
                     jnp.broadcast_to(gates_ref[0, :, c + 1:c + 2], (QB, LANES)))


def _masked_q(qb_ref, g, kv):
    q = qb_ref[0, :, g * LANES:(g + 1) * LANES]
    low = _pair_masks(QB)
    keep = low if kv == 0 else jnp.logical_not(low)
    return jnp.where(keep, q, jnp.zeros_like(q))


def _cmp_kernel(n_tiles, qb_ref, kcmp_ref, vcmp_ref, gates_ref, ov_ref, *rest):
    tbl_refs, (oc_ref, sel_ref, q_ref, s_ref, p_ref, pv_ref, imp_ref) = rest[:n_tiles], rest[n_tiles:]
    qblk = pl.program_id(1)
    t0 = qblk * QB
    low = _pair_masks(QB)
    for kv in range(NSA_KV_HEADS):
        for g in range(NSA_GROUP):
            r = kv * NSA_GROUP + g
            q_ref[r * QB:(r + 1) * QB, :] = _masked_q(qb_ref, g, kv)

    def attend(n_vis):
        kc = n_vis * CMP_TILE_KEYS
        s_ref[:, 0:kc] = _nt_dot(q_ref[...], kcmp_ref[0, 0:kc, :])
        for kv in range(NSA_KV_HEADS):
            psum = jnp.zeros((QB, kc), F32)
            for g in range(NSA_GROUP):
                r = kv * NSA_GROUP + g
                rows = slice(r * QB, (r + 1) * QB)
                s = s_ref[rows, 0:kc] + jnp.concatenate([tbl_refs[c][0, r] for c in range(n_vis)], axis=1)
                m = jnp.max(s, axis=1, keepdims=True)
                e = jnp.exp2(s - m)
                den = jnp.sum(e, axis=1, keepdims=True)
                p = jnp.where(m > 0.5 * NEG, e * (1.0 / den), 0.0)
                psum = psum + p
                p_ref[rows, 0:kc] = p.astype(BF16)
            hi = psum.astype(BF16)
            lo = (psum - hi.astype(F32)).astype(BF16)
            ov_t = ov_ref[:, 0:kc]
            imp_ref[kv] = _nt_dot(ov_t, hi) + _nt_dot(ov_t, lo)
        pv_ref[...] = _dot(p_ref[:, 0:kc], vcmp_ref[0, 0:kc, :])

    n_vis = qblk // CMP_TILE_SPAN + 1
    for w in range(1, n_tiles + 1):
        pl.when(n_vis == w)(functools.partial(attend, w))

    blk = lax.broadcasted_iota(jnp.int32, (LANES, QB), 0)
    cur = (t0 + lax.broadcasted_iota(jnp.int32, (LANES, QB), 1)) // SEL_BLOCK
    blk_f = blk.astype(F32)
    forced = (blk == cur) | (blk == cur - 1) | (blk == 0)
    causal = blk <= cur
    scores = [jnp.where(forced, FORCE_SCORE, jnp.where(causal, imp_ref[kv], -1.0)) for kv in range(NSA_KV_HEADS)]

    def pick(_, carry):
        new = []
        for val, sel in carry:
            mx = jnp.max(val, axis=0, keepdims=True)
            idx = jnp.min(jnp.where(val == mx, blk_f, float(LANES)), axis=0, keepdims=True)
            hit = blk_f == idx
            new.append((jnp.where(hit, -jnp.inf, val), jnp.where(hit, 1.0, sel)))
        return tuple(new)

    picked = lax.fori_loop(0, SEL_TOPK, pick, tuple((v, jnp.zeros((LANES, QB), F32)) for v in scores))
    eye = jnp.where(lax.broadcasted_iota(jnp.int32, (QB, QB), 0) == lax.broadcasted_iota(jnp.int32, (QB, QB), 1),
                    1.0, 0.0).astype(BF16)
    for kv in range(NSA_KV_HEADS):
        sel_t = jnp.where(causal, picked[kv][1], 0.0).astype(BF16)
        sel_ref[0, kv] = _nt_dot(eye, sel_t).astype(BF16)
    for g in range(NSA_GROUP):
        o0, o1 = pv_ref[g * QB:(g + 1) * QB, :], pv_ref[(NSA_GROUP + g) * QB:(NSA_GROUP + g + 1) * QB, :]
        oc_ref[0, :, g * LANES:(g + 1) * LANES] = jnp.where(low, o0, o1) * _gate_tile(gates_ref, 0, g)


def _compressed_branch(qb, kcmp, vcmp, gates, f_b):
    b, t, _ = qb.shape
    ncp = kcmp.shape[1]
    n_tiles = ncp // CMP_TILE_KEYS
    tbl = _cmp_bias(f_b)
    ov = _overlap_matrix_t(ncp, LANES)

    def tbl_spec(c):
        return pl.BlockSpec((1, NSA_HEADS, QB, CMP_TILE_KEYS),
                            lambda bi, i: (jnp.clip(i - CMP_TILE_SPAN * c, -1, CMP_CONST_DELTA) + 1, 0, 0, 0))

    blockq = lambda w: pl.BlockSpec((1, QB, w), lambda bi, i: (bi, i, 0))
    batch = lambda a: pl.BlockSpec((1,) + a.shape[1:], lambda bi, i: (bi, 0, 0))
    return pl.pallas_call(
        functools.partial(_cmp_kernel, n_tiles),
        grid=(b, t // QB),
        in_specs=[blockq(DB), batch(kcmp), batch(vcmp), blockq(LANES), pl.BlockSpec(ov.shape, lambda bi, i: (0, 0))]
                 + [tbl_spec(c) for c in range(n_tiles)],
        out_specs=[blockq(DB), pl.BlockSpec((1, NSA_KV_HEADS, QB, LANES), lambda bi, i: (bi, 0, i, 0))],
        out_shape=[jax.ShapeDtypeStruct((b, t, DB), F32), jax.ShapeDtypeStruct((b, NSA_KV_HEADS, t, LANES), BF16)],
        scratch_shapes=[pltpu.VMEM((N_ROWGROUPS * QB, LANES), BF16), pltpu.VMEM((N_ROWGROUPS * QB, ncp), F32),
                        pltpu.VMEM((N_ROWGROUPS * QB, ncp), BF16), pltpu.VMEM((N_ROWGROUPS * QB, LANES), F32),
                        pltpu.VMEM((NSA_KV_HEADS, LANES, QB), F32)],
        compiler_params=pltpu.CompilerParams(dimension_semantics=("parallel", "parallel"), vmem_limit_bytes=VMEM_LIMIT),
        name="nsa_compressed",
    )(qb, kcmp, vcmp, gates, ov, *([tbl] * n_tiles))


SEL_NEAR = 13


def _sel_bias(f_b):
    n_off = SEL_NEAR + 1
    cols = QB * n_off
    rel = f_b - f_b[:, BIAS_LEN - 1:]
    lo = -(QB - 1)
    big = _toeplitz_of(_extend(rel, lo, lo + QB + cols - 1), lo, QB, cols)
    tiles = jnp.flip(big.reshape(NSA_HEADS, QB, n_off, QB).transpose(2, 0, 1, 3), axis=0)
    return jnp.concatenate([jnp.zeros((1,) + tiles.shape[1:], F32), tiles], axis=0)


def _pair_ratio(acc0, acc1):
    low = _pair_masks(acc0.shape[0])
    den = pltpu.roll(jnp.where(low, acc1, acc0), HEAD_DIM, axis=1)
    return jnp.where(low, acc0, acc1) * (1.0 / den)


def _sel_kernel(qb_ref, sel_ref, gates_ref, cfar_ref, ks_ref, vs0_ref, vs1_ref, tbl_ref, out_ref,
                qaug_ref, s_ref, s1_ref, acc_ref, m_ref, alpha_ref):
    qblk = pl.program_id(1)
    for kv in range(NSA_KV_HEADS):
        unchosen = jnp.where(sel_ref[0, kv].astype(F32) > 0.0, 0.0, NEG)
        for g in range(NSA_GROUP):
            r = kv * NSA_GROUP + g
            rows = slice(r * QB, (r + 1) * QB)
            qaug_ref[rows, 0:LANES] = _masked_q(qb_ref, g, kv)
            qaug_ref[rows, LANES:2 * LANES] = (unchosen + cfar_ref[r:r + 1, :]).astype(BF16)
    acc_ref[...] = jnp.zeros_like(acc_ref)
    m_ref[...] = jnp.full_like(m_ref, NEG)

    last_tile = ks_ref.shape[1] // SEL_KT - 1

    def scores(j, dst_ref):
        start = pl.multiple_of(jnp.minimum(j, last_tile) * SEL_KT, SEL_KT)
        dst_ref[...] = _nt_dot(qaug_ref[...], ks_ref[0, pl.ds(start, SEL_KT), :])

    def consume(j, src_ref, near):
        start = pl.multiple_of(j * SEL_KT, SEL_KT)
        for r in range(N_ROWGROUPS):
            rows = slice(r * QB, (r + 1) * QB)
            s = src_ref[rows, :]
            if near:
                e1 = jnp.clip(qblk - 2 * j + 1, 0, SEL_NEAR + 1)
                e2 = jnp.clip(qblk - 2 * j, 0, SEL_NEAR + 1)
                s = s + jnp.concatenate([tbl_ref[e1, r], tbl_ref[e2, r]], axis=1)
                src_ref[rows, :] = s
            m_old = m_ref[rows, :]
            m_new = jnp.maximum(m_old, jnp.max(s, axis=1, keepdims=True))
            alpha_ref[rows, :] = jnp.exp2(m_old - m_new)
            m_ref[rows, :] = m_new
        for r in range(N_ROWGROUPS):
            rows = slice(r * QB, (r + 1) * QB)
            vals = (vs0_ref if r < NSA_GROUP else vs1_ref)[0, pl.ds(start, SEL_KT), :]
            m_new = m_ref[rows, :]
            p = jnp.exp2(src_ref[rows, :] - jnp.concatenate([m_new, m_new], axis=1))
            acc_ref[rows, :] = alpha_ref[rows, :] * acc_ref[rows, :] + _dot(p.astype(BF16), vals)

    def tile_pair(i, near):
        scores(2 * i + 1, s1_ref)
        consume(2 * i, s_ref, near)
        scores(2 * i + 2, s_ref)
        consume(2 * i + 1, s1_ref, near)

    n_pairs = ((qblk + 2) // 2 + 1) // 2
    n_far = jnp.maximum((qblk - (SEL_NEAR - 1)) // 2, 0) // 2
    scores(0, s_ref)
    lax.fori_loop(0, n_far, lambda i, c: (tile_pair(i, False), c)[1], 0)
    lax.fori_loop(n_far, n_pairs, lambda i, c: (tile_pair(i, True), c)[1], 0)
    for g in range(NSA_GROUP):
        ratio = _pair_ratio(acc_ref[g * QB:(g + 1) * QB, :], acc_ref[(NSA_GROUP + g) * QB:(NSA_GROUP + g + 1) * QB, :])
        out_ref[0, :, g * LANES:(g + 1) * LANES] = ratio * _gate_tile(gates_ref, 1, g)


def _selected_branch(qb, sel, gates, ksaug, vs0aug, vs1aug, f_b):
    b, t, _ = qb.shape
    tbl = _sel_bias(f_b)
    cfar = jnp.broadcast_to(f_b[:, BIAS_LEN - 1:], (NSA_HEADS, LANES))
    cfar = jnp.pad(cfar, ((0, 16 - NSA_HEADS), (0, 0)))
    blockq = lambda w: pl.BlockSpec((1, QB, w), lambda bi, i: (bi, i, 0))
    batch = lambda a: pl.BlockSpec((1,) + a.shape[1:], lambda bi, i: (bi, 0, 0))
    rows = N_ROWGROUPS * QB
    return pl.pallas_call(
        _sel_kernel,
        grid=(b, t // QB),
        in_specs=[blockq(DB), pl.BlockSpec((1, NSA_KV_HEADS, QB, LANES), lambda bi, i: (bi, 0, i, 0)), blockq(LANES),
                  pl.BlockSpec(cfar.shape, lambda bi, i: (0, 0)), batch(ksaug), batch(vs0aug), batch(vs1aug),
                  pl.BlockSpec(tbl.shape, lambda bi, i: (0, 0, 0, 0))],
        out_specs=blockq(DB),
        out_shape=jax.ShapeDtypeStruct((b, t, DB), F32),
        scratch_shapes=[pltpu.VMEM((rows, 2 * LANES), BF16), pltpu.VMEM((rows, SEL_KT), F32), pltpu.VMEM((rows, SEL_KT), F32),
                        pltpu.VMEM((rows, LANES), F32), pltpu.VMEM((rows, LANES), F32), pltpu.VMEM((rows, LANES), F32)],
        compiler_params=pltpu.CompilerParams(dimension_semantics=("parallel", "parallel"), vmem_limit_bytes=VMEM_LIMIT),
        name="nsa_selected",
    )(qb, sel, gates, cfar, ksaug, vs0aug, vs1aug, tbl)


WIN_KEYS = WIN + QB


def _win_bias(f_b):
    lo = WIN - (WIN_KEYS - 1)
    vals = _extend(f_b[:, :WIN], lo, WIN)
    vals = jnp.concatenate([vals, jnp.full((NSA_HEADS, lo + QB + WIN_KEYS - 1 - WIN), NEG, F32)], axis=1)
    return _toeplitz_of(vals, lo, QB, WIN_KEYS)


def _win_kernel(qb_ref, gates_ref, kw_ref, vw0_ref, vw1_ref, tbl_ref, out_ref, q_ref, s_ref):
    qblk = pl.program_id(1)
    for kv in range(NSA_KV_HEADS):
        for g in range(NSA_GROUP):
            r = kv * NSA_GROUP + g
            q_ref[r * QB:(r + 1) * QB, :] = _masked_q(qb_ref, g, kv)
    start = pl.multiple_of(qblk * QB, QB)
    s_ref[...] = _nt_dot(q_ref[...], kw_ref[0, pl.ds(start, WIN_KEYS), :])
    col = lax.broadcasted_iota(jnp.int32, (1, WIN_KEYS), 1)
    pad_mask = jnp.where(col + qblk * QB >= WIN, 0.0, NEG)
    outs = []
    for r in range(N_ROWGROUPS):
        vals = (vw0_ref if r < NSA_GROUP else vw1_ref)[0, pl.ds(start, WIN_KEYS), :]
        s = s_ref[r * QB:(r + 1) * QB, :] + tbl_ref[r] + pad_mask
        e = jnp.exp2(s - jnp.max(s, axis=1, keepdims=True))
        outs.append(_dot(e.astype(BF16), vals))
    for g in range(NSA_GROUP):
        out_ref[0, :, g * LANES:(g + 1) * LANES] = _pair_ratio(outs[g], outs[NSA_GROUP + g]) * _gate_tile(gates_ref, 2, g)


def _window_branch(qb, gates, kw, vw0aug, vw1aug, f_b):
    b, t, _ = qb.shape
    tbl = _win_bias(f_b)
    pad_front = lambda a: jnp.pad(a, ((0, 0), (WIN, 0), (0, 0)))
    kw_pad, vw0_pad, vw1_pad = pad_front(kw), pad_front(vw0aug), pad_front(vw1aug)
    blockq = lambda w: pl.BlockSpec((1, QB, w), lambda bi, i: (bi, i, 0))
    batch = lambda a: pl.BlockSpec((1,) + a.shape[1:], lambda bi, i: (bi, 0, 0))
    rows = N_ROWGROUPS * QB
    return pl.pallas_call(
        _win_kernel,
        grid=(b, t // QB),
        in_specs=[blockq(DB), blockq(LANES), batch(kw_pad), batch(vw0_pad), batch(vw1_pad),pl.BlockSpec(tbl.shape, lambda bi, i: (0, 0, 0))],
        out_specs=blockq(DB),
        out_shape=jax.ShapeDtypeStruct((b, t, DB), F32),
        scratch_shapes=[pltpu.VMEM((rows, LANES), BF16), pltpu.VMEM((rows, WIN_KEYS), F32)],
        compiler_params=pltpu.CompilerParams(dimension_semantics=("parallel", "parallel"), vmem_limit_bytes=VMEM_LIMIT),
        name="nsa_window",
    )(qb, gates, kw_pad, vw0_pad, vw1_pad, tbl)


OUT_TM = 256
C_GROUP = N_EXPERTS


def _outproj_kernel(x_ref, o1_ref, l1_ref, o4_ref, l4_ref, o16_ref, l16_ref, oc_ref, os_ref, ow_ref,
                    wout_ref, g_ref, wr_ref, br_ref, h_ref, hn_ref, comb_ref):
    l1, l4, l16 = l1_ref[...], l4_ref[...], l16_ref[...]
    mx = jnp.maximum(jnp.maximum(l1, l4), l16)
    w1, w4, w16 = jnp.exp(l1 - mx), jnp.exp(l4 - mx), jnp.exp(l16 - mx)
    oa = (w1 * o1_ref[...] + w4 * o4_ref[...] + w16 * o16_ref[...]) * (1.0 / (w1 + w4 + w16))
    ob = oc_ref[...] + os_ref[...] + ow_ref[...]
    y = _dot(oa.astype(BF16), wout_ref[0:DA, :]) + _dot(ob.astype(BF16), wout_ref[DA:DA + DB, :])
    h = x_ref[...] + y
    h_ref[...] = h
    hn = h * lax.rsqrt(jnp.mean(h * h, axis=-1, keepdims=True) + EPS) * g_ref[...]
    hn_ref[...] = hn.astype(BF16)
    logits = jnp.dot(hn, wr_ref[...], precision=lax.Precision.HIGHEST, preferred_element_type=F32) + br_ref[...]
    lane = lax.broadcasted_iota(jnp.int32, logits.shape, 1)
    lane_f = lane.astype(F32)
    big = float(LANES)
    gl = jnp.where((lane >= C_GROUP) & (lane < C_GROUP + N_GROUPS), logits, -jnp.inf)
    gmax = jnp.max(gl, axis=1, keepdims=True)
    gidx = jnp.min(jnp.where(gl == gmax, lane_f, big), axis=1, keepdims=True) - C_GROUP
    gprob = 1.0 / jnp.sum(jnp.exp(gl - gmax), axis=1, keepdims=True)
    grp_of_lane = (lane // EXPERTS_PER_GROUP).astype(F32)
    el = jnp.where((lane < N_EXPERTS) & (grp_of_lane == gidx), logits, -jnp.inf)
    v1 = jnp.max(el, axis=1, keepdims=True)
    i1 = jnp.min(jnp.where(el == v1, lane_f, big), axis=1, keepdims=True)
    el2 = jnp.where(lane_f == i1, -jnp.inf, el)
    v2 = jnp.max(el2, axis=1, keepdims=True)
    i2 = jnp.min(jnp.where(el2 == v2, lane_f, big), axis=1, keepdims=True)
    e2 = jnp.exp(v2 - v1)
    p1 = 1.0 / (1.0 + e2)
    comb_ref[...] = gprob * (jnp.where(lane_f == i1, p1, 0.0) + jnp.where(lane_f == i2, e2 * p1, 0.0))


def _outproj(x2d, a_parts, b_parts, w_out_perm, gain, w_router, b_router):
    n, d = x2d.shape
    row = lambda w: pl.BlockSpec((OUT_TM, w), lambda i: (i, 0))
    full = lambda a: pl.BlockSpec(a.shape, lambda i: (0, 0))
    return pl.pallas_call(
        _outproj_kernel,
        grid=(n // OUT_TM,),
        in_specs=[row(d)] + [row(DA)] * 6 + [row(DB)] * 3 + [full(w_out_perm), pl.BlockSpec((1, d), lambda i: (0, 0)),
                                                            full(w_router), full(b_router)],
        out_specs=[row(d), row(d), row(LANES)],
        out_shape=[jax.ShapeDtypeStruct((n, d), F32), jax.ShapeDtypeStruct((n, d), BF16),
                   jax.ShapeDtypeStruct((n, LANES), F32)],
        compiler_params=pltpu.CompilerParams(dimension_semantics=("parallel",), vmem_limit_bytes=VMEM_LIMIT),
        name="outproj_router",
    )(x2d, *a_parts, *b_parts, w_out_perm, gain.reshape(1, d), w_router, b_router)


MOE_TM = 512


def _moe_kernel(h_ref, hn_ref, comb_ref, wg_ref, wu_ref, wd_ref, g_ref, out_ref, acc_ref):
    e = pl.program_id(1)

    @pl.when(e == 0)
    def _():
        acc_ref[...] = jnp.zeros_like(acc_ref)

    hn = hn_ref[...]
    gate = _dot(hn, wg_ref[0])
    up = _dot(hn, wu_ref[0])
    lane = lax.broadcasted_iota(jnp.int32, comb_ref.shape, 1)
    weight = jnp.sum(jnp.where(lane == e, comb_ref[...], 0.0), axis=1, keepdims=True)
    hidden = (gate * jax.nn.sigmoid(gate) * up * weight).astype(BF16)
    acc_ref[...] += _dot(hidden, wd_ref[0])

    @pl.when(e == pl.num_programs(1) - 1)
    def _():
        y = h_ref[...] + acc_ref[...]
        out_ref[...] = y * lax.rsqrt(jnp.mean(y * y, axis=-1, keepdims=True) + EPS) * g_ref[...]


def _moe(h, hn, comb, w_gate, w_up, w_down, gain):
    n, d = h.shape
    tm = min(MOE_TM, n)
    row = lambda w: pl.BlockSpec((tm, w), lambda i, e: (i, 0))
    return pl.pallas_call(
        _moe_kernel,
        grid=(n // tm, N_EXPERTS),
        in_specs=[row(d), row(d), row(LANES),
                  pl.BlockSpec((1, d, D_EXPERT), lambda i, e: (e, 0, 0)),
                  pl.BlockSpec((1, d, D_EXPERT), lambda i, e: (e, 0, 0)),
                  pl.BlockSpec((1, D_EXPERT, d), lambda i, e: (e, 0, 0)),
                  pl.BlockSpec((1, d), lambda i, e: (0, 0))],
        out_specs=row(d),
        out_shape=jax.ShapeDtypeStruct((n, d), F32),
        scratch_shapes=[pltpu.VMEM((tm, d), F32)],
        compiler_params=pltpu.CompilerParams(dimension_semantics=("parallel", "arbitrary"), vmem_limit_bytes=VMEM_LIMIT),
        name="moe_experts",
    )(h, hn, comb, w_gate, w_up, w_down, gain.reshape(1, d))


def _permute_w_out(w_out):
    d = w_out.shape[1]
    wb = w_out[DA:].reshape(NSA_KV_HEADS, NSA_GROUP, HEAD_DIM, d).transpose(1, 0, 2, 3).reshape(DB, d)
    return jnp.concatenate([w_out[:DA], wb], axis=0).astype(BF16)


def _router_weights(w_group, b_group, w_expert, b_expert):
    d = w_group.shape[0]
    w = jnp.concatenate([w_expert.reshape(d, N_EXPERTS), w_group], axis=1)
    b = jnp.concatenate([b_expert.reshape(N_EXPERTS), b_group])
    pad = LANES - w.shape[1]
    return jnp.pad(w, ((0, 0), (0, pad))).astype(F32), jnp.pad(b, (0, pad)).reshape(1, LANES).astype(F32)


def _layer(h, rel_bias, norm_mix, w_in, w_out, cmp_pos_k, cmp_pos_v, cmp_k_w1, cmp_k_w2, cmp_v_w1, cmp_v_w2,
           norm_ffn, w_rg, b_rg, w_re, b_re, w_gate, w_up, w_down, out_gain):
    b, t, d = h.shape
    n = b * t
    assert t % (QB * DIL_PATTERNS[-1][1]) == 0 and t // SEL_BLOCK <= LANES and n % MOE_TM == 0
    x2d = h.reshape(n, d)
    seq = lambda a: a.reshape(b, t, a.shape[-1])
    qa, ka, va, qb, kc, vc, ksaug, vs0aug, vs1aug, kw, vw0aug, vw1aug, gates = map(
        seq, _inproj(x2d, norm_mix, _permute_w_in(w_in), t))
    f_a = _bias_1d(rel_bias[:, :DIL_HEADS])
    f_b = _bias_1d(rel_bias[:, DIL_HEADS:]) * LOG2E
    a_parts = []
    for _, dil in DIL_PATTERNS:
        o, lse = _dilated(qa, ka, va, _dil_bias(f_a, dil), dil)
        a_parts += [o.reshape(n, DA), lse.reshape(n, DA)]
    kcmp, vcmp = _compress(kc, vc, cmp_pos_k, cmp_pos_v, cmp_k_w1, cmp_k_w2, cmp_v_w1, cmp_v_w2)
    o_cmp, sel = _compressed_branch(qb, kcmp, vcmp, gates, f_b)
    o_sel = _selected_branch(qb, sel, gates, ksaug, vs0aug, vs1aug, f_b)
    o_win = _window_branch(qb, gates, kw, vw0aug, vw1aug, f_b)
    b_parts = [o.reshape(n, DB) for o in (o_cmp, o_sel, o_win)]
    w_router, b_router = _router_weights(w_rg, b_rg, w_re, b_re)
    h2, hn, comb = _outproj(x2d, a_parts, b_parts, _permute_w_out(w_out), norm_ffn, w_router, b_router)
    return _moe(h2, hn, comb, w_gate.astype(BF16), w_up.astype(BF16), w_down.astype(BF16), out_gain)


def kernel(x, rel_bias, norm_mix, w_in, w_out, cmp_pos_k, cmp_pos_v, cmp_k_w1, cmp_k_w2, cmp_v_w1, cmp_v_w2,
           norm_ffn, w_router_group, b_router_group, w_router_expert, b_router_expert, w_gate, w_up, w_down,
           norm_final):
    depth = norm_mix.shape[0]
    assert depth == 1, "the final RMSNorm is fused into the last layer's expert kernel"
    out = _layer(x, rel_bias, norm_mix[0], w_in[0], w_out[0], cmp_pos_k[0], cmp_pos_v[0], cmp_k_w1[0], cmp_k_w2[0],
                 cmp_v_w1[0], cmp_v_w2[0], norm_ffn[0], w_router_group[0], b_router_group[0], w_router_expert[0],
                 b_router_expert[0], w_gate[0], w_up[0], w_down[0], norm_final)
    return out.reshape(x.shape)
```

```python
import functools
import math

import jax
import jax.numpy as jnp
import numpy as np
from jax import lax
from jax.experimental import pallas as pl
from jax.experimental.pallas import tpu as pltpu

HEAD_DIM = 64
DIL_HEADS = 6
NSA_KV_HEADS = 2
NSA_GROUP = 5
NSA_HEADS = NSA_KV_HEADS * NSA_GROUP
N_HEADS = DIL_HEADS + NSA_HEADS
DIL_PATTERNS = ((128, 1), (512, 4), (2048, 16))
CMP_BLOCK = 32
CMP_STRIDE = 16
CMP_HIDDEN = 256
SEL_BLOCK = 64
SEL_TOPK = 16
WIN = 512
FORCE_SCORE = 1.0e4
N_BUCKETS = 32
MAX_DISTANCE = 2048
N_GROUPS = 4
EXPERTS_PER_GROUP = 4
N_EXPERTS = N_GROUPS * EXPERTS_PER_GROUP
D_EXPERT = 512
EPS = 1e-6

LANES = 128
QB = 128
NEG = -1.0e30
LOG2E = math.log2(math.e)
DA = DIL_HEADS * HEAD_DIM
DB = NSA_HEADS * HEAD_DIM
N_ROWGROUPS = NSA_HEADS
SEL_KT = 256
VMEM_LIMIT = 56 * 1024 * 1024

F32 = jnp.float32
BF16 = jnp.bfloat16
NT_DIMS = (((1,), (1,)), ((), ()))


def _nt_dot(a, b):
    return lax.dot_general(a, b, NT_DIMS, preferred_element_type=F32)


def _dot(a, b):
    return jnp.dot(a, b, preferred_element_type=F32)


def _bucket_np(dist):
    dist = np.maximum(np.asarray(dist, np.int64), 0)
    max_exact = N_BUCKETS // 2
    x = np.maximum(dist, 1).astype(np.float32) / np.float32(max_exact)
    large = max_exact + (np.log(x) / np.float32(math.log(MAX_DISTANCE / max_exact))
                         * np.float32(N_BUCKETS - max_exact)).astype(np.int32)
    large = np.minimum(large, N_BUCKETS - 1)
    return np.where(dist < max_exact, dist, large).astype(np.int32)


BIAS_LEN = 4096


def _bias_1d(rel_bias_heads):
    onehot = (_bucket_np(np.arange(BIAS_LEN))[None, :] == np.arange(N_BUCKETS)[:, None]).astype(np.float32)
    return jnp.dot(rel_bias_heads.T.astype(F32), jnp.asarray(onehot), precision=lax.Precision.HIGHEST)


def _extend(f, lo, hi):
    assert hi <= f.shape[-1]
    if lo >= 0:
        return f[..., lo:hi]
    pad = jnp.full(f.shape[:-1] + (-lo,), NEG, f.dtype)
    return jnp.concatenate([pad, f[..., :hi]], axis=-1)


def _toeplitz(w, q, c):
    n = q + c - 1
    assert w.shape[-1] == n
    lead = w.shape[:-1]
    wp = jnp.concatenate([w, jnp.zeros(lead + (1,), w.dtype)], axis=-1)
    flat = jnp.broadcast_to(wp[..., None, :], lead + (q, n + 1)).reshape(lead + (q * (n + 1),))
    return flat[..., :q * n].reshape(lead + (q, n))[..., q - 1:q - 1 + c]


def _toeplitz_of(fn_vals, lo, q, c):
    return _toeplitz(jnp.flip(fn_vals, axis=-1), q, c)


IN_TM = 512
C_QA, C_KA, C_VA = 0, DA, 2 * DA
C_QB = 3 * DA
C_KC = C_QB + DB
C_VC, C_KS, C_VS, C_KW, C_VW, C_GT = (C_KC + LANES * i for i in range(1, 7))
N_COLS = C_GT + LANES


def _permute_w_in(w_in):
    scale = 1.0 / math.sqrt(HEAD_DIM)
    sizes = [DA] * 3 + [DB] + [NSA_KV_HEADS * HEAD_DIM] * 6 + [3 * NSA_HEADS]
    offs = np.concatenate([[0], np.cumsum(sizes)])
    part = lambda i: w_in[:, offs[i]:offs[i + 1]]
    d = w_in.shape[0]
    qb = part(3).reshape(d, NSA_KV_HEADS, NSA_GROUP, HEAD_DIM).transpose(0, 2, 1, 3).reshape(d, DB)
    gt = part(10).reshape(d, NSA_KV_HEADS, NSA_GROUP, 3).transpose(0, 3, 2, 1).reshape(d, 3 * NSA_HEADS)
    gt = jnp.pad(gt, ((0, 0), (0, LANES - 3 * NSA_HEADS)))
    cols = [part(0) * scale, part(1), part(2), qb * (scale * LOG2E)] + [part(i) for i in range(4, 10)] + [gt]
    return jnp.concatenate(cols, axis=1).astype(BF16)


def _inproj_kernel(seq_len, x_ref, g_ref, w_ref, qa_ref, ka_ref, va_ref, qb_ref, kc_ref, vc_ref,
                   ksaug_ref, vs0_ref, vs1_ref, kw_ref, vw0_ref, vw1_ref, gates_ref):
    x = x_ref[...]
    xn = (x * lax.rsqrt(jnp.mean(x * x, axis=-1, keepdims=True) + EPS) * g_ref[...]).astype(BF16)
    seg = lambda a, n: _dot(xn, w_ref[:, a:a + n])
    qa_ref[...] = seg(C_QA, DA).astype(BF16)
    ka_ref[...] = seg(C_KA, DA).astype(BF16)
    va_ref[...] = seg(C_VA, DA).astype(BF16)
    qb_ref[...] = seg(C_QB, DB).astype(BF16)
    kc_ref[...] = seg(C_KC, LANES).astype(BF16)
    vc_ref[...] = seg(C_VC, LANES).astype(BF16)
    kw_ref[...] = seg(C_KW, LANES).astype(BF16)
    tm = x.shape[0]
    tok = (pl.program_id(0) * tm) % seq_len + lax.broadcasted_iota(jnp.int32, (tm, LANES), 0)
    lane = lax.broadcasted_iota(jnp.int32, (tm, LANES), 1)
    ksaug_ref[:, 0:LANES] = seg(C_KS, LANES).astype(BF16)
    ksaug_ref[:, LANES:2 * LANES] = jnp.where(lane == tok // SEL_BLOCK, 1.0, 0.0).astype(BF16)
    low = lane < HEAD_DIM
    for col, ref0, ref1 in ((C_VS, vs0_ref, vs1_ref), (C_VW, vw0_ref, vw1_ref)):
        v = seg(col, LANES)
        ref0[...] = jnp.where(low, v, 1.0).astype(BF16)
        ref1[...] = jnp.where(low, 1.0, v).astype(BF16)
    gates_ref[...] = jax.nn.sigmoid(seg(C_GT, LANES))


def _inproj(x2d, gain, w_perm, seq_len):
    n, d = x2d.shape
    row = lambda w: pl.BlockSpec((IN_TM, w), lambda i: (i, 0))
    widths = [DA, DA, DA, DB, LANES, LANES, 2 * LANES, LANES, LANES, LANES, LANES, LANES]
    out_shape = [jax.ShapeDtypeStruct((n, w), BF16) for w in widths] + [jax.ShapeDtypeStruct((n, LANES), F32)]
    return pl.pallas_call(
        functools.partial(_inproj_kernel, seq_len),
        grid=(n // IN_TM,),
        in_specs=[row(d), pl.BlockSpec((1, d), lambda i: (0, 0)), pl.BlockSpec((d, N_COLS), lambda i: (0, 0))],
        out_specs=[row(w) for w in widths] + [row(LANES)],
        out_shape=out_shape,
        compiler_params=pltpu.CompilerParams(dimension_semantics=("parallel",), vmem_limit_bytes=VMEM_LIMIT),
        name="inproj",
    )(x2d, gain.reshape(1, d), w_perm)


def _embed_pair(w, n_tok):
    c = w.shape[1]
    w4 = w.reshape(n_tok, 1, HEAD_DIM, 1, c) * jnp.eye(NSA_KV_HEADS, dtype=w.dtype).reshape(1, 2, 1, 2, 1)
    return w4.reshape(n_tok * 2 * HEAD_DIM, 2 * c)


def _gelu_tanh(x):
    return 0.5 * x * (1.0 + jnp.tanh(math.sqrt(2.0 / math.pi) * (x + 0.044715 * (x * x * x))))


def _compress_kernel(ck_ref, cv_ref, posk_ref, posv_ref, wk1a, wk1b, wk2, wv1a, wv1b, wv2,
                     kout_ref, vout_ref, shift_ref):
    ncp = ck_ref.shape[1]
    for c_ref, pos_ref, w1a, w1b, w2, out_ref in ((ck_ref, posk_ref, wk1a, wk1b, wk2, kout_ref),
                                                  (cv_ref, posv_ref, wv1a, wv1b, wv2, vout_ref)):
        c = c_ref[0].astype(F32)
        first = _dot((c + pos_ref[0:1, :]).astype(BF16), w1a[...])
        second = _dot((c + pos_ref[1:2, :]).astype(BF16), w1b[...])
        shift_ref[0:ncp, :] = second
        shift_ref[ncp:ncp + 8, :] = jnp.zeros((8, second.shape[1]), F32)
        hidden = _gelu_tanh(first + shift_ref[1:ncp + 1, :])
        out_ref[0] = _dot(hidden.astype(BF16), w2[...]).astype(BF16)


def _compress(kc, vc, pos_k, pos_v, k_w1, k_w2, v_w1, v_w2):
    b, t, _ = kc.shape
    ncp = t // CMP_STRIDE
    half = CMP_STRIDE * HEAD_DIM
    wide = CMP_STRIDE * LANES

    def prep(w1, w2, pos):
        pos_pair = jnp.broadcast_to(pos.reshape(2, CMP_STRIDE, 1, HEAD_DIM), (2, CMP_STRIDE, 2, HEAD_DIM))
        return (_embed_pair(w1[:half], CMP_STRIDE).astype(BF16), _embed_pair(w1[half:], CMP_STRIDE).astype(BF16),
                jnp.kron(jnp.eye(NSA_KV_HEADS, dtype=w2.dtype), w2).astype(BF16),
                pos_pair.reshape(2, wide).astype(F32))

    wk1a, wk1b, wk2, posk = prep(k_w1, k_w2, pos_k)
    wv1a, wv1b, wv2, posv = prep(v_w1, v_w2, pos_v)
    full = lambda a: pl.BlockSpec(a.shape, lambda i: (0,) * a.ndim)
    tok = pl.BlockSpec((1, ncp, wide), lambda i: (i, 0, 0))
    out = pl.BlockSpec((1, ncp, LANES), lambda i: (i, 0, 0))
    return pl.pallas_call(
        _compress_kernel,
        grid=(b,),
        in_specs=[tok, tok, full(posk), full(posv), full(wk1a), full(wk1b), full(wk2), full(wv1a), full(wv1b), full(wv2)],
        out_specs=[out, out],
        out_shape=[jax.ShapeDtypeStruct((b, ncp, LANES), BF16)] * 2,
        scratch_shapes=[pltpu.VMEM((ncp + 8, 2 * CMP_HIDDEN), F32)],
        compiler_params=pltpu.CompilerParams(dimension_semantics=("parallel",), vmem_limit_bytes=VMEM_LIMIT),
        name="compress",
    )(kc.reshape(b, ncp, wide), vc.reshape(b, ncp, wide), posk, posv, wk1a, wk1b, wk2, wv1a, wv1b, wv2)


def _pair_masks(rows):
    lane = lax.broadcasted_iota(jnp.int32, (rows, LANES), 1)
    return lane < HEAD_DIM


A_SUPER = QB * DIL_PATTERNS[-1][1]


def _mixer_a_kernel(q_ref, kp_ref, kc_ref, vp_ref, vc_ref, bias_ref, out_ref, qf_ref, kf_ref, vf_ref, o_ref, lse_ref):
    first = pl.program_id(1) == 0
    n_pairs = DIL_HEADS // 2
    for p in range(n_pairs):
        cs = slice(p * LANES, (p + 1) * LANES)
        qf_ref[p] = q_ref[0, :, cs].astype(F32)
        kf_ref[p, 0:A_SUPER, :] = kp_ref[0, :, cs].astype(F32)
        kf_ref[p, A_SUPER:2 * A_SUPER, :] = kc_ref[0, :, cs].astype(F32)
        vf_ref[p, 0:A_SUPER, :] = vp_ref[0, :, cs].astype(F32)
        vf_ref[p, A_SUPER:2 * A_SUPER, :] = vc_ref[0, :, cs].astype(F32)
    low = _pair_masks(QB)
    in_prev = lax.broadcasted_iota(jnp.int32, (QB, 2 * QB), 1) < QB
    zero = jnp.zeros((QB, LANES), BF16)

    def chunk(idx, dil, q_base, k_base, q_span, k_span, off, at_start):
        rows = lambda size: pl.ds(off, size) if dil == 1 else pl.ds(off, size, stride=dil)
        q_win = lambda ref: ref.at[pl.ds(pl.multiple_of(q_base, 8), q_span), :]
        k_win = lambda ref: ref.at[pl.ds(pl.multiple_of(k_base, 8), k_span), :]
        q_rows = rows(QB)
        prev_mask = jnp.where(jnp.logical_and(in_prev, jnp.logical_and(first, at_start)), NEG, 0.0)
        for p in range(n_pairs):
            q = q_win(qf_ref.at[p])[q_rows, :].astype(BF16)
            keys = k_win(kf_ref.at[p])[rows(2 * QB), :].astype(BF16)
            vals = k_win(vf_ref.at[p])[rows(2 * QB), :].astype(BF16)
            o_win, lse_win = q_win(o_ref.at[p]), q_win(lse_ref.at[p])
            lhs = jnp.concatenate([jnp.where(low, q, zero), jnp.where(low, zero, q)], axis=0)
            bias = jnp.concatenate([bias_ref[idx, 2 * p] + prev_mask, bias_ref[idx, 2 * p + 1] + prev_mask], axis=0)
            s = _nt_dot(lhs, keys) + bias
            m = jnp.max(s, axis=1, keepdims=True)
            e = jnp.exp(s - m)
            l = jnp.sum(e, axis=1, keepdims=True)
            pv = _dot(e.astype(BF16), vals) * (1.0 / l)
            lse = m + jnp.log(l)
            o_new = jnp.where(low, pv[:QB], pv[QB:])
            l_new = jnp.where(low, jnp.broadcast_to(lse[:QB], (QB, LANES)), jnp.broadcast_to(lse[QB:], (QB, LANES)))
            if idx > 0:
                o_old, l_old = o_win[q_rows, :], lse_win[q_rows, :]
                mx = jnp.maximum(l_old, l_new)
                w_old, w_new = jnp.exp(l_old - mx), jnp.exp(l_new - mx)
                tot = w_old + w_new
                o_new = (w_old * o_old + w_new * o_new) * (1.0 / tot)
                l_new = mx + jnp.log(tot)
            o_win[q_rows, :] = o_new
            lse_win[q_rows, :] = l_new

    def loop(n, body):
        lax.fori_loop(0, n, lambda i, carry: (body(i), carry)[1], 0)

    for idx, (_, dil) in enumerate(DIL_PATTERNS):
        span = QB * dil
        n_chunks = A_SUPER // span
        if n_chunks > 1:
            for r in range(dil):
                loop(n_chunks, lambda c, idx=idx, dil=dil, span=span, r=r:
                     chunk(idx, dil, span * c, A_SUPER + span * (c - 1), span, 2 * span, r, c == 0))
        else:
            for r in range(8):
                loop(dil // 8, lambda hi, idx=idx, dil=dil, span=span, r=r:
                     chunk(idx, dil, 8 * hi, 8 * hi, span - 8, 2 * span - 8, r, True))
    for p in range(n_pairs):
        out_ref[0, :, p * LANES:(p + 1) * LANES] = o_ref[p].astype(BF16)


def _mixer_a(qa, ka, va, bias):
    b, t, _ = qa.shape
    cur = pl.BlockSpec((1, A_SUPER, DA), lambda bi, i: (bi, i, 0))
    prev = pl.BlockSpec((1, A_SUPER, DA), lambda bi, i: (bi, jnp.maximum(i - 1, 0), 0))
    return pl.pallas_call(
        _mixer_a_kernel,
        grid=(b, t // A_SUPER),
        in_specs=[cur, prev, cur, prev, cur, pl.BlockSpec(bias.shape, lambda bi, i: (0, 0, 0, 0))],
        out_specs=cur,
        out_shape=jax.ShapeDtypeStruct((b, t, DA), BF16),
        scratch_shapes=[pltpu.VMEM((DIL_HEADS // 2, rows, LANES), F32) for rows in (A_SUPER, 2 * A_SUPER, 2 * A_SUPER, A_SUPER, A_SUPER)],
        compiler_params=pltpu.CompilerParams(dimension_semantics=("parallel", "parallel"), vmem_limit_bytes=VMEM_LIMIT),
        name="mixer_a",
    )(qa, ka, ka, va, va, bias)


def _dil_bias(f_a, dil):
    steps = DIL_PATTERNS[0][0]
    g = f_a[:, 0:dil * steps + 1:dil]
    lo, hi = QB - (2 * QB - 1), QB + QB
    vals = jnp.concatenate([jnp.full((DIL_HEADS, -lo), NEG, F32), g, jnp.full((DIL_HEADS, hi - steps - 1), NEG, F32)], axis=1)
    return _toeplitz_of(vals, lo, QB, 2 * QB)


CMP_TILE_KEYS = LANES
CMP_TILE_SPAN = CMP_TILE_KEYS * CMP_STRIDE // QB
CMP_CONST_DELTA = 28


def _cmp_bias(f_b):
    per = QB // CMP_STRIDE
    n_rows = per * (CMP_CONST_DELTA + 1)
    m_lo, m_hi = -(CMP_TILE_KEYS - 1), n_rows
    base = _extend(f_b, CMP_STRIDE * m_lo - (CMP_BLOCK - 1), CMP_STRIDE * m_hi - (CMP_BLOCK - 1))
    g = base.reshape(NSA_HEADS, m_hi - m_lo, CMP_STRIDE).transpose(0, 2, 1)
    t = _toeplitz_of(g, m_lo, n_rows, CMP_TILE_KEYS)
    t = t.reshape(NSA_HEADS, CMP_STRIDE, CMP_CONST_DELTA + 1, per, CMP_TILE_KEYS).transpose(2, 0, 3, 1, 4)
    t = t.reshape(CMP_CONST_DELTA + 1, NSA_HEADS, QB, CMP_TILE_KEYS)
    return jnp.concatenate([jnp.full((1,) + t.shape[1:], NEG, F32), t], axis=0)


def _overlap_matrix_t(ncp, n_sel_pad):
    n = np.arange(ncp)[None, :] * CMP_STRIDE
    s = np.arange(n_sel_pad)[:, None] * SEL_BLOCK
    ov = np.clip(np.minimum(n + CMP_BLOCK, s + SEL_BLOCK) - np.maximum(n, s), 0, None) / CMP_BLOCK
    return jnp.asarray(ov, BF16)


def _gate_tile(gates_ref, branch, g):
    c = branch * NSA_HEADS + g * 2
    low = _pair_masks(QB)
    return jnp.where(low, jnp.broadcast_to(gates_ref[0, :, c:c + 1], (QB, LANES)),
                     jnp.broadcast_to(gates_ref[0, :, c + 1:c + 2], (QB, LANES)))


def _masked_q(qb_ref, g, kv):
    q = qb_ref[0, :, g * LANES:(g + 1) * LANES]
    low = _pair_masks(QB)
    keep = low if kv == 0 else jnp.logical_not(low)
    return jnp.where(keep, q, jnp.zeros_like(q))


def _cmp_kernel(n_tiles, qb_ref, kcmp_ref, vcmp_ref, gates_ref, ov_ref, *rest):
    tbl_refs, (oc_ref, sel_ref, q_ref, s_ref, p_ref, pv_ref, imp_ref) = rest[:n_tiles], rest[n_tiles:]
    qblk = pl.program_id(1)
    t0 = qblk * QB
    low = _pair_masks(QB)
    for kv in range(NSA_KV_HEADS):
        for g in range(NSA_GROUP):
            r = kv * NSA_GROUP + g
            q_ref[r * QB:(r + 1) * QB, :] = _masked_q(qb_ref, g, kv)

    def attend(n_vis):
        kc = n_vis * CMP_TILE_KEYS
        s_ref[:, 0:kc] = _nt_dot(q_ref[...], kcmp_ref[0, 0:kc, :])
        for kv in range(NSA_KV_HEADS):
            psum = jnp.zeros((QB, kc), F32)
            for g in range(NSA_GROUP):
                r = kv * NSA_GROUP + g
                rows = slice(r * QB, (r + 1) * QB)
                s = s_ref[rows, 0:kc] + jnp.concatenate([tbl_refs[c][0, r] for c in range(n_vis)], axis=1)
                m = jnp.max(s, axis=1, keepdims=True)
                e = jnp.exp2(s - m)
                den = jnp.sum(e, axis=1, keepdims=True)
                p = jnp.where(m > 0.5 * NEG, e * (1.0 / den), 0.0)
                psum = psum + p
                p_ref[rows, 0:kc] = p.astype(BF16)
            hi = psum.astype(BF16)
            lo = (psum - hi.astype(F32)).astype(BF16)
            ov_t = ov_ref[:, 0:kc]
            imp_ref[kv] = _nt_dot(ov_t, hi) + _nt_dot(ov_t, lo)
        pv_ref[...] = _dot(p_ref[:, 0:kc], vcmp_ref[0, 0:kc, :])

    n_vis = qblk // CMP_TILE_SPAN + 1
    for w in range(1, n_tiles + 1):
        pl.when(n_vis == w)(functools.partial(attend, w))

    blk = lax.broadcasted_iota(jnp.int32, (LANES, QB), 0)
    cur = (t0 + lax.broadcasted_iota(jnp.int32, (LANES, QB), 1)) // SEL_BLOCK
    blk_f = blk.astype(F32)
    forced = (blk == cur) | (blk == cur - 1) | (blk == 0)
    causal = blk <= cur
    scores = [jnp.where(forced, FORCE_SCORE, jnp.where(causal, imp_ref[kv], -1.0)) for kv in range(NSA_KV_HEADS)]

    def pick(_, carry):
        new = []
        for val, sel in carry:
            mx = jnp.max(val, axis=0, keepdims=True)
            idx = jnp.min(jnp.where(val == mx, blk_f, float(LANES)), axis=0, keepdims=True)
            hit = blk_f == idx
            new.append((jnp.where(hit, -jnp.inf, val), jnp.where(hit, 1.0, sel)))
        return tuple(new)

    picked = lax.fori_loop(0, SEL_TOPK, pick, tuple((v, jnp.zeros((LANES, QB), F32)) for v in scores))
    eye = jnp.where(lax.broadcasted_iota(jnp.int32, (QB, QB), 0) == lax.broadcasted_iota(jnp.int32, (QB, QB), 1),
                    1.0, 0.0).astype(BF16)
    for kv in range(NSA_KV_HEADS):
        sel_t = jnp.where(causal, picked[kv][1], 0.0).astype(BF16)
        sel_ref[0, kv] = _nt_dot(eye, sel_t).astype(BF16)
    for g in range(NSA_GROUP):
        o0, o1 = pv_ref[g * QB:(g + 1) * QB, :], pv_ref[(NSA_GROUP + g) * QB:(NSA_GROUP + g + 1) * QB, :]
        oc_ref[0, :, g * LANES:(g + 1) * LANES] = jnp.where(low, o0, o1) * _gate_tile(gates_ref, 0, g)


def _compressed_branch(qb, kcmp, vcmp, gates, f_b):
    b, t, _ = qb.shape
    ncp = kcmp.shape[1]
    n_tiles = ncp // CMP_TILE_KEYS
    tbl = _cmp_bias(f_b)
    ov = _overlap_matrix_t(ncp, LANES)

    def tbl_spec(c):
        return pl.BlockSpec((1, NSA_HEADS, QB, CMP_TILE_KEYS),
                            lambda bi, i: (jnp.clip(i - CMP_TILE_SPAN * c, -1, CMP_CONST_DELTA) + 1, 0, 0, 0))

    blockq = lambda w: pl.BlockSpec((1, QB, w), lambda bi, i: (bi, i, 0))
    batch = lambda a: pl.BlockSpec((1,) + a.shape[1:], lambda bi, i: (bi, 0, 0))
    return pl.pallas_call(
        functools.partial(_cmp_kernel, n_tiles),
        grid=(b, t // QB),
        in_specs=[blockq(DB), batch(kcmp), batch(vcmp), blockq(LANES), pl.BlockSpec(ov.shape, lambda bi, i: (0, 0))]
                 + [tbl_spec(c) for c in range(n_tiles)],
        out_specs=[blockq(DB), pl.BlockSpec((1, NSA_KV_HEADS, QB, LANES), lambda bi, i: (bi, 0, i, 0))],
        out_shape=[jax.ShapeDtypeStruct((b, t, DB), F32), jax.ShapeDtypeStruct((b, NSA_KV_HEADS, t, LANES), BF16)],
        scratch_shapes=[pltpu.VMEM((N_ROWGROUPS * QB, LANES), BF16), pltpu.VMEM((N_ROWGROUPS * QB, ncp), F32),
                        pltpu.VMEM((N_ROWGROUPS * QB, ncp), BF16), pltpu.VMEM((N_ROWGROUPS * QB, LANES), F32),
                        pltpu.VMEM((NSA_KV_HEADS, LANES, QB), F32)],
        compiler_params=pltpu.CompilerParams(dimension_semantics=("parallel", "parallel"), vmem_limit_bytes=VMEM_LIMIT),
        name="nsa_compressed",
    )(qb, kcmp, vcmp, gates, ov, *([tbl] * n_tiles))


SEL_NEAR = 13


def _sel_bias(f_b):
    n_off = SEL_NEAR + 1
    cols = QB * n_off
    rel = f_b - f_b[:, BIAS_LEN - 1:]
    lo = -(QB - 1)
    big = _toeplitz_of(_extend(rel, lo, lo + QB + cols - 1), lo, QB, cols)
    tiles = jnp.flip(big.reshape(NSA_HEADS, QB, n_off, QB).transpose(2, 0, 1, 3), axis=0)
    return jnp.concatenate([jnp.zeros((1,) + tiles.shape[1:], F32), tiles], axis=0)


def _pair_ratio(acc0, acc1):
    low = _pair_masks(acc0.shape[0])
    den = pltpu.roll(jnp.where(low, acc1, acc0), HEAD_DIM, axis=1)
    return jnp.where(low, acc0, acc1) * (1.0 / den)


def _sel_kernel(qb_ref, sel_ref, gates_ref, cfar_ref, ks_ref, vs0_ref, vs1_ref, tbl_ref, out_ref,
                qaug_ref, s_ref, s1_ref, acc_ref, m_ref, alpha_ref):
    qblk = pl.program_id(1)
    for kv in range(NSA_KV_HEADS):
        unchosen = jnp.where(sel_ref[0, kv].astype(F32) > 0.0, 0.0, NEG)
        for g in range(NSA_GROUP):
            r = kv * NSA_GROUP + g
            rows = slice(r * QB, (r + 1) * QB)
            qaug_ref[rows, 0:LANES] = _masked_q(qb_ref, g, kv)
            qaug_ref[rows, LANES:2 * LANES] = (unchosen + cfar_ref[r:r + 1, :]).astype(BF16)
    acc_ref[...] = jnp.zeros_like(acc_ref)
    m_ref[...] = jnp.full_like(m_ref, NEG)

    last_tile = ks_ref.shape[1] // SEL_KT - 1

    def scores(j, dst_ref):
        start = pl.multiple_of(jnp.minimum(j, last_tile) * SEL_KT, SEL_KT)
        dst_ref[...] = _nt_dot(qaug_ref[...], ks_ref[0, pl.ds(start, SEL_KT), :])

    def consume(j, src_ref, near):
        start = pl.multiple_of(j * SEL_KT, SEL_KT)
        for r in range(N_ROWGROUPS):
            rows = slice(r * QB, (r + 1) * QB)
            s = src_ref[rows, :]
            if near:
                e1 = jnp.clip(qblk - 2 * j + 1, 0, SEL_NEAR + 1)
                e2 = jnp.clip(qblk - 2 * j, 0, SEL_NEAR + 1)
                s = s + jnp.concatenate([tbl_ref[e1, r], tbl_ref[e2, r]], axis=1)
                src_ref[rows, :] = s
            m_old = m_ref[rows, :]
            m_new = jnp.maximum(m_old, jnp.max(s, axis=1, keepdims=True))
            alpha_ref[rows, :] = jnp.exp2(m_old - m_new)
            m_ref[rows, :] = m_new
        for r in range(N_ROWGROUPS):
            rows = slice(r * QB, (r + 1) * QB)
            vals = (vs0_ref if r < NSA_GROUP else vs1_ref)[0, pl.ds(start, SEL_KT), :]
            m_new = m_ref[rows, :]
            p = jnp.exp2(src_ref[rows, :] - jnp.concatenate([m_new, m_new], axis=1))
            acc_ref[rows, :] = alpha_ref[rows, :] * acc_ref[rows, :] + _dot(p.astype(BF16), vals)

    def tile_pair(i, near):
        scores(2 * i + 1, s1_ref)
        consume(2 * i, s_ref, near)
        scores(2 * i + 2, s_ref)
        consume(2 * i + 1, s1_ref, near)

    n_pairs = ((qblk + 2) // 2 + 1) // 2
    n_far = jnp.maximum((qblk - (SEL_NEAR - 1)) // 2, 0) // 2
    scores(0, s_ref)
    lax.fori_loop(0, n_far, lambda i, c: (tile_pair(i, False), c)[1], 0)
    lax.fori_loop(n_far, n_pairs, lambda i, c: (tile_pair(i, True), c)[1], 0)
    for g in range(NSA_GROUP):
        ratio = _pair_ratio(acc_ref[g * QB:(g + 1) * QB, :], acc_ref[(NSA_GROUP + g) * QB:(NSA_GROUP + g + 1) * QB, :])
        out_ref[0, :, g * LANES:(g + 1) * LANES] = ratio * _gate_tile(gates_ref, 1, g)


def _selected_branch(qb, sel, gates, ksaug, vs0aug, vs1aug, f_b):
    b, t, _ = qb.shape
    tbl = _sel_bias(f_b)
    cfar = jnp.broadcast_to(f_b[:, BIAS_LEN - 1:], (NSA_HEADS, LANES))
    cfar = jnp.pad(cfar, ((0, 16 - NSA_HEADS), (0, 0)))
    blockq = lambda w: pl.BlockSpec((1, QB, w), lambda bi, i: (bi, i, 0))
    batch = lambda a: pl.BlockSpec((1,) + a.shape[1:], lambda bi, i: (bi, 0, 0))
    rows = N_ROWGROUPS * QB
    return pl.pallas_call(
        _sel_kernel,
        grid=(b, t // QB),
        in_specs=[blockq(DB), pl.BlockSpec((1, NSA_KV_HEADS, QB, LANES), lambda bi, i: (bi, 0, i, 0)), blockq(LANES),
                  pl.BlockSpec(cfar.shape, lambda bi, i: (0, 0)), batch(ksaug), batch(vs0aug), batch(vs1aug),
                  pl.BlockSpec(tbl.shape, lambda bi, i: (0, 0, 0, 0))],
        out_specs=blockq(DB),
        out_shape=jax.ShapeDtypeStruct((b, t, DB), F32),
        scratch_shapes=[pltpu.VMEM((rows, 2 * LANES), BF16), pltpu.VMEM((rows, SEL_KT), F32), pltpu.VMEM((rows, SEL_KT), F32),
                        pltpu.VMEM((rows, LANES), F32), pltpu.VMEM((rows, LANES), F32), pltpu.VMEM((rows, LANES), F32)],
        compiler_params=pltpu.CompilerParams(dimension_semantics=("parallel", "parallel"), vmem_limit_bytes=VMEM_LIMIT),
        name="nsa_selected",
    )(qb, sel, gates, cfar, ksaug, vs0aug, vs1aug, tbl)


WIN_KEYS = WIN + QB


def _win_bias(f_b):
    lo = WIN - (WIN_KEYS - 1)
    vals = _extend(f_b[:, :WIN], lo, WIN)
    vals = jnp.concatenate([vals, jnp.full((NSA_HEADS, lo + QB + WIN_KEYS - 1 - WIN), NEG, F32)], axis=1)
    return _toeplitz_of(vals, lo, QB, WIN_KEYS)


def _win_kernel(qb_ref, gates_ref, kw_ref, vw0_ref, vw1_ref, tbl_ref, out_ref, q_ref, s_ref):
    qblk = pl.program_id(1)
    for kv in range(NSA_KV_HEADS):
        for g in range(NSA_GROUP):
            r = kv * NSA_GROUP + g
            q_ref[r * QB:(r + 1) * QB, :] = _masked_q(qb_ref, g, kv)
    start = pl.multiple_of(qblk * QB, QB)
    s_ref[...] = _nt_dot(q_ref[...], kw_ref[0, pl.ds(start, WIN_KEYS), :])
    col = lax.broadcasted_iota(jnp.int32, (1, WIN_KEYS), 1)
    pad_mask = jnp.where(col + qblk * QB >= WIN, 0.0, NEG)
    outs = []
    for r in range(N_ROWGROUPS):
        vals = (vw0_ref if r < NSA_GROUP else vw1_ref)[0, pl.ds(start, WIN_KEYS), :]
        s = s_ref[r * QB:(r + 1) * QB, :] + tbl_ref[r] + pad_mask
        e = jnp.exp2(s - jnp.max(s, axis=1, keepdims=True))
        outs.append(_dot(e.astype(BF16), vals))
    for g in range(NSA_GROUP):
        out_ref[0, :, g * LANES:(g + 1) * LANES] = _pair_ratio(outs[g], outs[NSA_GROUP + g]) * _gate_tile(gates_ref, 2, g)


def _window_branch(qb, gates, kw, vw0aug, vw1aug, f_b):
    b, t, _ = qb.shape
    tbl = _win_bias(f_b)
    pad_front = lambda a: jnp.pad(a, ((0, 0), (WIN, 0), (0, 0)))
    kw_pad, vw0_pad, vw1_pad = pad_front(kw), pad_front(vw0aug), pad_front(vw1aug)
    blockq = lambda w: pl.BlockSpec((1, QB, w), lambda bi, i: (bi, i, 0))
    batch = lambda a: pl.BlockSpec((1,) + a.shape[1:], lambda bi, i: (bi, 0, 0))
    rows = N_ROWGROUPS * QB
    return pl.pallas_call(
        _win_kernel,
        grid=(b, t // QB),
        in_specs=[blockq(DB), blockq(LANES), batch(kw_pad), batch(vw0_pad), batch(vw1_pad),pl.BlockSpec(tbl.shape, lambda bi, i: (0, 0, 0))],
        out_specs=blockq(DB),
        out_shape=jax.ShapeDtypeStruct((b, t, DB), F32),
        scratch_shapes=[pltpu.VMEM((rows, LANES), BF16), pltpu.VMEM((rows, WIN_KEYS), F32)],
        compiler_params=pltpu.CompilerParams(dimension_semantics=("parallel", "parallel"), vmem_limit_bytes=VMEM_LIMIT),
        name="nsa_window",
    )(qb, gates, kw_pad, vw0_pad, vw1_pad, tbl)


OUT_TM = 256
C_GROUP = N_EXPERTS


def _outproj_kernel(x_ref, oa_ref, oc_ref, os_ref, ow_ref,
                    wout_ref, g_ref, wr_ref, br_ref, h_ref, hn_ref, comb_ref):
    ob = oc_ref[...] + os_ref[...] + ow_ref[...]
    y = _dot(oa_ref[...], wout_ref[0:DA, :]) + _dot(ob.astype(BF16), wout_ref[DA:DA + DB, :])
    h = x_ref[...] + y
    h_ref[...] = h
    hn = h * lax.rsqrt(jnp.mean(h * h, axis=-1, keepdims=True) + EPS) * g_ref[...]
    hn_ref[...] = hn.astype(BF16)
    logits = jnp.dot(hn, wr_ref[...], precision=lax.Precision.HIGHEST, preferred_element_type=F32) + br_ref[...]
    lane = lax.broadcasted_iota(jnp.int32, logits.shape, 1)
    lane_f = lane.astype(F32)
    big = float(LANES)
    gl = jnp.where((lane >= C_GROUP) & (lane < C_GROUP + N_GROUPS), logits, -jnp.inf)
    gmax = jnp.max(gl, axis=1, keepdims=True)
    gidx = jnp.min(jnp.where(gl == gmax, lane_f, big), axis=1, keepdims=True) - C_GROUP
    gprob = 1.0 / jnp.sum(jnp.exp(gl - gmax), axis=1, keepdims=True)
    grp_of_lane = (lane // EXPERTS_PER_GROUP).astype(F32)
    el = jnp.where((lane < N_EXPERTS) & (grp_of_lane == gidx), logits, -jnp.inf)
    v1 = jnp.max(el, axis=1, keepdims=True)
    i1 = jnp.min(jnp.where(el == v1, lane_f, big), axis=1, keepdims=True)
    el2 = jnp.where(lane_f == i1, -jnp.inf, el)
    v2 = jnp.max(el2, axis=1, keepdims=True)
    i2 = jnp.min(jnp.where(el2 == v2, lane_f, big), axis=1, keepdims=True)
    e2 = jnp.exp(v2 - v1)
    p1 = 1.0 / (1.0 + e2)
    comb_ref[...] = gprob * (jnp.where(lane_f == i1, p1, 0.0) + jnp.where(lane_f == i2, e2 * p1, 0.0))


def _outproj(x2d, o_a, b_parts, w_out_perm, gain, w_router, b_router):
    n, d = x2d.shape
    row = lambda w: pl.BlockSpec((OUT_TM, w), lambda i: (i, 0))
    full = lambda a: pl.BlockSpec(a.shape, lambda i: (0, 0))
    return pl.pallas_call(
        _outproj_kernel,
        grid=(n // OUT_TM,),
        in_specs=[row(d), row(DA)] + [row(DB)] * 3 + [full(w_out_perm), pl.BlockSpec((1, d), lambda i: (0, 0)),
                                                      full(w_router), full(b_router)],
        out_specs=[row(d), row(d), row(LANES)],
        out_shape=[jax.ShapeDtypeStruct((n, d), F32), jax.ShapeDtypeStruct((n, d), BF16),
                   jax.ShapeDtypeStruct((n, LANES), F32)],
        compiler_params=pltpu.CompilerParams(dimension_semantics=("parallel",), vmem_limit_bytes=VMEM_LIMIT),
        name="outproj_router",
    )(x2d, o_a, *b_parts, w_out_perm, gain.reshape(1, d), w_router, b_router)


MOE_TM = 512


def _moe_kernel(h_ref, hn_ref, comb_ref, wg_ref, wu_ref, wd_ref, g_ref, out_ref, acc_ref):
    e = pl.program_id(1)

    @pl.when(e == 0)
    def _():
        acc_ref[...] = jnp.zeros_like(acc_ref)

    hn = hn_ref[...]
    gate = _dot(hn, wg_ref[0])
    up = _dot(hn, wu_ref[0])
    lane = lax.broadcasted_iota(jnp.int32, comb_ref.shape, 1)
    weight = jnp.sum(jnp.where(lane == e, comb_ref[...], 0.0), axis=1, keepdims=True)
    hidden = (gate * jax.nn.sigmoid(gate) * up * weight).astype(BF16)
    acc_ref[...] += _dot(hidden, wd_ref[0])

    @pl.when(e == pl.num_programs(1) - 1)
    def _():
        y = h_ref[...] + acc_ref[...]
        out_ref[...] = y * lax.rsqrt(jnp.mean(y * y, axis=-1, keepdims=True) + EPS) * g_ref[...]


def _moe(h, hn, comb, w_gate, w_up, w_down, gain):
    n, d = h.shape
    tm = min(MOE_TM, n)
    row = lambda w: pl.BlockSpec((tm, w), lambda i, e: (i, 0))
    return pl.pallas_call(
        _moe_kernel,
        grid=(n // tm, N_EXPERTS),
        in_specs=[row(d), row(d), row(LANES),
                  pl.BlockSpec((1, d, D_EXPERT), lambda i, e: (e, 0, 0)),
                  pl.BlockSpec((1, d, D_EXPERT), lambda i, e: (e, 0, 0)),
                  pl.BlockSpec((1, D_EXPERT, d), lambda i, e: (e, 0, 0)),
                  pl.BlockSpec((1, d), lambda i, e: (0, 0))],
        out_specs=row(d),
        out_shape=jax.ShapeDtypeStruct((n, d), F32),
        scratch_shapes=[pltpu.VMEM((tm, d), F32)],
        compiler_params=pltpu.CompilerParams(dimension_semantics=("parallel", "arbitrary"), vmem_limit_bytes=VMEM_LIMIT),
        name="moe_experts",
    )(h, hn, comb, w_gate, w_up, w_down, gain.reshape(1, d))


def _permute_w_out(w_out):
    d = w_out.shape[1]
    wb = w_out[DA:].reshape(NSA_KV_HEADS, NSA_GROUP, HEAD_DIM, d).transpose(1, 0, 2, 3).reshape(DB, d)
    return jnp.concatenate([w_out[:DA], wb], axis=0).astype(BF16)


def _router_weights(w_group, b_group, w_expert, b_expert):
    d = w_group.shape[0]
    w = jnp.concatenate([w_expert.reshape(d, N_EXPERTS), w_group], axis=1)
    b = jnp.concatenate([b_expert.reshape(N_EXPERTS), b_group])
    pad = LANES - w.shape[1]
    return jnp.pad(w, ((0, 0), (0, pad))).astype(F32), jnp.pad(b, (0, pad)).reshape(1, LANES).astype(F32)


def _layer(h, rel_bias, norm_mix, w_in, w_out, cmp_pos_k, cmp_pos_v, cmp_k_w1, cmp_k_w2, cmp_v_w1, cmp_v_w2,
           norm_ffn, w_rg, b_rg, w_re, b_re, w_gate, w_up, w_down, out_gain):
    b, t, d = h.shape
    n = b * t
    assert t % (QB * DIL_PATTERNS[-1][1]) == 0 and t // SEL_BLOCK <= LANES and n % MOE_TM == 0
    x2d = h.reshape(n, d)
    seq = lambda a: a.reshape(b, t, a.shape[-1])
    qa, ka, va, qb, kc, vc, ksaug, vs0aug, vs1aug, kw, vw0aug, vw1aug, gates = map(
        seq, _inproj(x2d, norm_mix, _permute_w_in(w_in), t))
    f_a = _bias_1d(rel_bias[:, :DIL_HEADS])
    f_b = _bias_1d(rel_bias[:, DIL_HEADS:]) * LOG2E
    o_a = _mixer_a(qa, ka, va, jnp.stack([_dil_bias(f_a, dil) for _, dil in DIL_PATTERNS]))
    kcmp, vcmp = _compress(kc, vc, cmp_pos_k, cmp_pos_v, cmp_k_w1, cmp_k_w2, cmp_v_w1, cmp_v_w2)
    o_cmp, sel = _compressed_branch(qb, kcmp, vcmp, gates, f_b)
    o_sel = _selected_branch(qb, sel, gates, ksaug, vs0aug, vs1aug, f_b)
    o_win = _window_branch(qb, gates, kw, vw0aug, vw1aug, f_b)
    b_parts = [o.reshape(n, DB) for o in (o_cmp, o_sel, o_win)]
    w_router, b_router = _router_weights(w_rg, b_rg, w_re, b_re)
    h2, hn, comb = _outproj(x2d, o_a.reshape(n, DA), b_parts, _permute_w_out(w_out), norm_ffn, w_router, b_router)
    return _moe(h2, hn, comb, w_gate.astype(BF16), w_up.astype(BF16), w_down.astype(BF16), out_gain)


def kernel(x, rel_bias, norm_mix, w_in, w_out, cmp_pos_k, cmp_pos_v, cmp_k_w1, cmp_k_w2, cmp_v_w1, cmp_v_w2,
           norm_ffn, w_router_group, b_router_group, w_router_expert, b_router_expert, w_gate, w_up, w_down,
           norm_final):
    depth = norm_mix.shape[0]
    assert depth == 1, "the final RMSNorm is fused into the last layer's expert kernel"
    out = _layer(x, rel_bias, norm_mix[0], w_in[0], w_out[0], cmp_pos_k[0], cmp_pos_v[0], cmp_k_w1[0], cmp_k_w2[0],
                 cmp_v_w1[0], cmp_v_w2[0], norm_ffn[0], w_router_group[0], b_router_group[0], w_router_expert[0],
                 b_router_expert[0], w_gate[0], w_up[0], w_down[0], norm_final)
    return out.reshape(x.shape)
```

```python
import functools
import math

import jax
import jax.numpy as jnp
import numpy as np
from jax import lax
from jax.experimental import pallas as pl
from jax.experimental.pallas import tpu as pltpu

HEAD_DIM = 64
DIL_HEADS = 6
NSA_KV_HEADS = 2
NSA_GROUP = 5
NSA_HEADS = NSA_KV_HEADS * NSA_GROUP
N_HEADS = DIL_HEADS + NSA_HEADS
DIL_PATTERNS = ((128, 1), (512, 4), (2048, 16))
CMP_BLOCK = 32
CMP_STRIDE = 16
CMP_HIDDEN = 256
SEL_BLOCK = 64
SEL_TOPK = 16
WIN = 512
FORCE_SCORE = 1.0e4
N_BUCKETS = 32
MAX_DISTANCE = 2048
N_GROUPS = 4
EXPERTS_PER_GROUP = 4
N_EXPERTS = N_GROUPS * EXPERTS_PER_GROUP
D_EXPERT = 512
EPS = 1e-6

LANES = 128
QB = 128
NEG = -1.0e30
LOG2E = math.log2(math.e)
DA = DIL_HEADS * HEAD_DIM
DB = NSA_HEADS * HEAD_DIM
N_ROWGROUPS = NSA_HEADS
SEL_KT = 256
VMEM_LIMIT = 56 * 1024 * 1024

F32 = jnp.float32
BF16 = jnp.bfloat16
NT_DIMS = (((1,), (1,)), ((), ()))


def _nt_dot(a, b):
    return lax.dot_general(a, b, NT_DIMS, preferred_element_type=F32)


def _dot(a, b):
    return jnp.dot(a, b, preferred_element_type=F32)


def _bucket_np(dist):
    dist = np.maximum(np.asarray(dist, np.int64), 0)
    max_exact = N_BUCKETS // 2
    x = np.maximum(dist, 1).astype(np.float32) / np.float32(max_exact)
    large = max_exact + (np.log(x) / np.float32(math.log(MAX_DISTANCE / max_exact))
                         * np.float32(N_BUCKETS - max_exact)).astype(np.int32)
    large = np.minimum(large, N_BUCKETS - 1)
    return np.where(dist < max_exact, dist, large).astype(np.int32)


BIAS_LEN = 4096


def _bias_1d(rel_bias_heads):
    onehot = (_bucket_np(np.arange(BIAS_LEN))[None, :] == np.arange(N_BUCKETS)[:, None]).astype(np.float32)
    return jnp.dot(rel_bias_heads.T.astype(F32), jnp.asarray(onehot), precision=lax.Precision.HIGHEST)


def _extend(f, lo, hi):
    assert hi <= f.shape[-1]
    if lo >= 0:
        return f[..., lo:hi]
    pad = jnp.full(f.shape[:-1] + (-lo,), NEG, f.dtype)
    return jnp.concatenate([pad, f[..., :hi]], axis=-1)


def _toeplitz(w, q, c):
    n = q + c - 1
    assert w.shape[-1] == n
    lead = w.shape[:-1]
    wp = jnp.concatenate([w, jnp.zeros(lead + (1,), w.dtype)], axis=-1)
    flat = jnp.broadcast_to(wp[..., None, :], lead + (q, n + 1)).reshape(lead + (q * (n + 1),))
    return flat[..., :q * n].reshape(lead + (q, n))[..., q - 1:q - 1 + c]


def _toeplitz_of(fn_vals, lo, q, c):
    return _toeplitz(jnp.flip(fn_vals, axis=-1), q, c)


IN_TM = 512
C_QA, C_KA, C_VA = 0, DA, 2 * DA
C_QB = 3 * DA
C_KC = C_QB + DB
C_VC, C_KS, C_VS, C_KW, C_VW, C_GT = (C_KC + LANES * i for i in range(1, 7))
N_COLS = C_GT + LANES


def _permute_w_in(w_in):
    scale = 1.0 / math.sqrt(HEAD_DIM)
    sizes = [DA] * 3 + [DB] + [NSA_KV_HEADS * HEAD_DIM] * 6 + [3 * NSA_HEADS]
    offs = np.concatenate([[0], np.cumsum(sizes)])
    part = lambda i: w_in[:, offs[i]:offs[i + 1]]
    d = w_in.shape[0]
    qb = part(3).reshape(d, NSA_KV_HEADS, NSA_GROUP, HEAD_DIM).transpose(0, 2, 1, 3).reshape(d, DB)
    gt = part(10).reshape(d, NSA_KV_HEADS, NSA_GROUP, 3).transpose(0, 3, 2, 1).reshape(d, 3 * NSA_HEADS)
    gt = jnp.pad(gt, ((0, 0), (0, LANES - 3 * NSA_HEADS)))
    cols = [part(0) * scale, part(1), part(2), qb * (scale * LOG2E)] + [part(i) for i in range(4, 10)] + [gt]
    return jnp.concatenate(cols, axis=1).astype(BF16)


def _inproj_kernel(seq_len, x_ref, g_ref, w_ref, qa_ref, ka_ref, va_ref, qb_ref, kc_ref, vc_ref,
                   ksaug_ref, vs0_ref, vs1_ref, kw_ref, vw0_ref, vw1_ref, gates_ref):
    x = x_ref[...]
    xn = (x * lax.rsqrt(jnp.mean(x * x, axis=-1, keepdims=True) + EPS) * g_ref[...]).astype(BF16)
    seg = lambda a, n: _dot(xn, w_ref[:, a:a + n])
    qa_ref[...] = seg(C_QA, DA).astype(BF16)
    ka_ref[...] = seg(C_KA, DA).astype(BF16)
    va_ref[...] = seg(C_VA, DA).astype(BF16)
    qb_ref[...] = seg(C_QB, DB).astype(BF16)
    kc_ref[...] = seg(C_KC, LANES).astype(BF16)
    vc_ref[...] = seg(C_VC, LANES).astype(BF16)
    kw_ref[...] = seg(C_KW, LANES).astype(BF16)
    tm = x.shape[0]
    tok = (pl.program_id(0) * tm) % seq_len + lax.broadcasted_iota(jnp.int32, (tm, LANES), 0)
    lane = lax.broadcasted_iota(jnp.int32, (tm, LANES), 1)
    ksaug_ref[:, 0:LANES] = seg(C_KS, LANES).astype(BF16)
    ksaug_ref[:, LANES:2 * LANES] = jnp.where(lane == tok // SEL_BLOCK, 1.0, 0.0).astype(BF16)
    low = lane < HEAD_DIM
    for col, ref0, ref1 in ((C_VS, vs0_ref, vs1_ref), (C_VW, vw0_ref, vw1_ref)):
        v = seg(col, LANES)
        ref0[...] = jnp.where(low, v, 1.0).astype(BF16)
        ref1[...] = jnp.where(low, 1.0, v).astype(BF16)
    gates_ref[...] = jax.nn.sigmoid(seg(C_GT, LANES))


def _inproj(x2d, gain, w_perm, seq_len):
    n, d = x2d.shape
    row = lambda w: pl.BlockSpec((IN_TM, w), lambda i: (i, 0))
    widths = [DA, DA, DA, DB, LANES, LANES, 2 * LANES, LANES, LANES, LANES, LANES, LANES]
    out_shape = [jax.ShapeDtypeStruct((n, w), BF16) for w in widths] + [jax.ShapeDtypeStruct((n, LANES), F32)]
    return pl.pallas_call(
        functools.partial(_inproj_kernel, seq_len),
        grid=(n // IN_TM,),
        in_specs=[row(d), pl.BlockSpec((1, d), lambda i: (0, 0)), pl.BlockSpec((d, N_COLS), lambda i: (0, 0))],
        out_specs=[row(w) for w in widths] + [row(LANES)],
        out_shape=out_shape,
        compiler_params=pltpu.CompilerParams(dimension_semantics=("parallel",), vmem_limit_bytes=VMEM_LIMIT),
        name="inproj",
    )(x2d, gain.reshape(1, d), w_perm)


def _embed_pair(w, n_tok):
    c = w.shape[1]
    w4 = w.reshape(n_tok, 1, HEAD_DIM, 1, c) * jnp.eye(NSA_KV_HEADS, dtype=w.dtype).reshape(1, 2, 1, 2, 1)
    return w4.reshape(n_tok * 2 * HEAD_DIM, 2 * c)


def _gelu_tanh(x):
    return 0.5 * x * (1.0 + jnp.tanh(math.sqrt(2.0 / math.pi) * (x + 0.044715 * (x * x * x))))


def _compress_kernel(ck_ref, cv_ref, posk_ref, posv_ref, wk1a, wk1b, wk2, wv1a, wv1b, wv2,
                     kout_ref, vout_ref, shift_ref):
    ncp = ck_ref.shape[1]
    for c_ref, pos_ref, w1a, w1b, w2, out_ref in ((ck_ref, posk_ref, wk1a, wk1b, wk2, kout_ref),
                                                  (cv_ref, posv_ref, wv1a, wv1b, wv2, vout_ref)):
        c = c_ref[0].astype(F32)
        first = _dot((c + pos_ref[0:1, :]).astype(BF16), w1a[...])
        second = _dot((c + pos_ref[1:2, :]).astype(BF16), w1b[...])
        shift_ref[0:ncp, :] = second
        shift_ref[ncp:ncp + 8, :] = jnp.zeros((8, second.shape[1]), F32)
        hidden = _gelu_tanh(first + shift_ref[1:ncp + 1, :])
        out_ref[0] = _dot(hidden.astype(BF16), w2[...]).astype(BF16)


def _compress(kc, vc, pos_k, pos_v, k_w1, k_w2, v_w1, v_w2):
    b, t, _ = kc.shape
    ncp = t // CMP_STRIDE
    half = CMP_STRIDE * HEAD_DIM
    wide = CMP_STRIDE * LANES

    def prep(w1, w2, pos):
        pos_pair = jnp.broadcast_to(pos.reshape(2, CMP_STRIDE, 1, HEAD_DIM), (2, CMP_STRIDE, 2, HEAD_DIM))
        return (_embed_pair(w1[:half], CMP_STRIDE).astype(BF16), _embed_pair(w1[half:], CMP_STRIDE).astype(BF16),
                jnp.kron(jnp.eye(NSA_KV_HEADS, dtype=w2.dtype), w2).astype(BF16),
                pos_pair.reshape(2, wide).astype(F32))

    wk1a, wk1b, wk2, posk = prep(k_w1, k_w2, pos_k)
    wv1a, wv1b, wv2, posv = prep(v_w1, v_w2, pos_v)
    full = lambda a: pl.BlockSpec(a.shape, lambda i: (0,) * a.ndim)
    tok = pl.BlockSpec((1, ncp, wide), lambda i: (i, 0, 0))
    out = pl.BlockSpec((1, ncp, LANES), lambda i: (i, 0, 0))
    return pl.pallas_call(
        _compress_kernel,
        grid=(b,),
        in_specs=[tok, tok, full(posk), full(posv), full(wk1a), full(wk1b), full(wk2), full(wv1a), full(wv1b), full(wv2)],
        out_specs=[out, out],
        out_shape=[jax.ShapeDtypeStruct((b, ncp, LANES), BF16)] * 2,
        scratch_shapes=[pltpu.VMEM((ncp + 8, 2 * CMP_HIDDEN), F32)],
        compiler_params=pltpu.CompilerParams(dimension_semantics=("parallel",), vmem_limit_bytes=VMEM_LIMIT),
        name="compress",
    )(kc.reshape(b, ncp, wide), vc.reshape(b, ncp, wide), posk, posv, wk1a, wk1b, wk2, wv1a, wv1b, wv2)


def _pair_masks(rows):
    lane = lax.broadcasted_iota(jnp.int32, (rows, LANES), 1)
    return lane < HEAD_DIM


A_SUPER = QB * DIL_PATTERNS[-1][1]


def _mixer_a_kernel(q_ref, kp_ref, kc_ref, vp_ref, vc_ref, bias_ref, out_ref, qf_ref, kf_ref, vf_ref, o_ref, lse_ref):
    first = pl.program_id(1) == 0
    n_pairs = DIL_HEADS // 2
    for p in range(n_pairs):
        cs = slice(p * LANES, (p + 1) * LANES)
        qf_ref[p] = q_ref[0, :, cs].astype(F32)
        kf_ref[p, 0:A_SUPER, :] = kp_ref[0, :, cs].astype(F32)
        kf_ref[p, A_SUPER:2 * A_SUPER, :] = kc_ref[0, :, cs].astype(F32)
        vf_ref[p, 0:A_SUPER, :] = vp_ref[0, :, cs].astype(F32)
        vf_ref[p, A_SUPER:2 * A_SUPER, :] = vc_ref[0, :, cs].astype(F32)
    low = _pair_masks(QB)
    in_prev = lax.broadcasted_iota(jnp.int32, (QB, 2 * QB), 1) < QB
    zero = jnp.zeros((QB, LANES), BF16)

    def chunk(idx, dil, q_base, k_base, q_span, k_span, off, at_start):
        rows = lambda size: pl.ds(off, size) if dil == 1 else pl.ds(off, size, stride=dil)
        q_win = lambda ref: ref.at[pl.ds(pl.multiple_of(q_base, 8), q_span), :]
        k_win = lambda ref: ref.at[pl.ds(pl.multiple_of(k_base, 8), k_span), :]
        q_rows = rows(QB)
        prev_mask = jnp.where(jnp.logical_and(in_prev, jnp.logical_and(first, at_start)), NEG, 0.0)
        for p in range(n_pairs):
            q = q_win(qf_ref.at[p])[q_rows, :].astype(BF16)
            keys = k_win(kf_ref.at[p])[rows(2 * QB), :].astype(BF16)
            vals = k_win(vf_ref.at[p])[rows(2 * QB), :].astype(BF16)
            o_win, lse_win = q_win(o_ref.at[p]), q_win(lse_ref.at[p])
            lhs = jnp.concatenate([jnp.where(low, q, zero), jnp.where(low, zero, q)], axis=0)
            bias = jnp.concatenate([bias_ref[idx, 2 * p] + prev_mask, bias_ref[idx, 2 * p + 1] + prev_mask], axis=0)
            s = _nt_dot(lhs, keys) + bias
            m = jnp.max(s, axis=1, keepdims=True)
            e = jnp.exp(s - m)
            l = jnp.sum(e, axis=1, keepdims=True)
            pv = _dot(e.astype(BF16), vals) * (1.0 / l)
            lse = m + jnp.log(l)
            o_new = jnp.where(low, pv[:QB], pv[QB:])
            l_new = jnp.where(low, jnp.broadcast_to(lse[:QB], (QB, LANES)), jnp.broadcast_to(lse[QB:], (QB, LANES)))
            if idx > 0:
                o_old, l_old = o_win[q_rows, :], lse_win[q_rows, :]
                mx = jnp.maximum(l_old, l_new)
                w_old, w_new = jnp.exp(l_old - mx), jnp.exp(l_new - mx)
                tot = w_old + w_new
                o_new = (w_old * o_old + w_new * o_new) * (1.0 / tot)
                l_new = mx + jnp.log(tot)
            o_win[q_rows, :] = o_new
            lse_win[q_rows, :] = l_new

    def loop(n, body):
        lax.fori_loop(0, n, lambda i, carry: (body(i), carry)[1], 0)

    for idx, (_, dil) in enumerate(DIL_PATTERNS):
        span = QB * dil
        n_chunks = A_SUPER // span
        if n_chunks > 1:
            for r in range(dil):
                loop(n_chunks, lambda c, idx=idx, dil=dil, span=span, r=r:
                     chunk(idx, dil, span * c, A_SUPER + span * (c - 1), span, 2 * span, r, c == 0))
        else:
            for r in range(8):
                loop(dil // 8, lambda hi, idx=idx, dil=dil, span=span, r=r:
                     chunk(idx, dil, 8 * hi, 8 * hi, span - 8, 2 * span - 8, r, True))
    for p in range(n_pairs):
        out_ref[0, :, p * LANES:(p + 1) * LANES] = o_ref[p].astype(BF16)


def _mixer_a(qa, ka, va, bias):
    b, t, _ = qa.shape
    cur = pl.BlockSpec((1, A_SUPER, DA), lambda bi, i: (bi, i, 0))
    prev = pl.BlockSpec((1, A_SUPER, DA), lambda bi, i: (bi, jnp.maximum(i - 1, 0), 0))
    return pl.pallas_call(
        _mixer_a_kernel,
        grid=(b, t // A_SUPER),
        in_specs=[cur, prev, cur, prev, cur, pl.BlockSpec(bias.shape, lambda bi, i: (0, 0, 0, 0))],
        out_specs=cur,
        out_shape=jax.ShapeDtypeStruct((b, t, DA), BF16),
        scratch_shapes=[pltpu.VMEM((DIL_HEADS // 2, rows, LANES), F32) for rows in (A_SUPER, 2 * A_SUPER, 2 * A_SUPER, A_SUPER, A_SUPER)],
        compiler_params=pltpu.CompilerParams(dimension_semantics=("parallel", "parallel"), vmem_limit_bytes=VMEM_LIMIT),
        name="mixer_a",
    )(qa, ka, ka, va, va, bias)


def _dil_bias(f_a, dil):
    steps = DIL_PATTERNS[0][0]
    g = f_a[:, 0:dil * steps + 1:dil]
    lo, hi = QB - (2 * QB - 1), QB + QB
    vals = jnp.concatenate([jnp.full((DIL_HEADS, -lo), NEG, F32), g, jnp.full((DIL_HEADS, hi - steps - 1), NEG, F32)], axis=1)
    return _toeplitz_of(vals, lo, QB, 2 * QB)


CMP_TILE_KEYS = LANES
CMP_TILE_SPAN = CMP_TILE_KEYS * CMP_STRIDE // QB
CMP_CONST_DELTA = 28


def _cmp_bias(f_b):
    per = QB // CMP_STRIDE
    n_rows = per * (CMP_CONST_DELTA + 1)
    m_lo, m_hi = -(CMP_TILE_KEYS - 1), n_rows
    f_b = f_b.astype(BF16)
    base = _extend(f_b, CMP_STRIDE * m_lo - (CMP_BLOCK - 1), CMP_STRIDE * m_hi - (CMP_BLOCK - 1))
    g = base.reshape(NSA_HEADS, m_hi - m_lo, CMP_STRIDE).transpose(0, 2, 1)
    t = _toeplitz_of(g, m_lo, n_rows, CMP_TILE_KEYS)
    t = t.reshape(NSA_HEADS, CMP_STRIDE, CMP_CONST_DELTA + 1, per, CMP_TILE_KEYS).transpose(2, 0, 3, 1, 4)
    t = t.reshape(CMP_CONST_DELTA + 1, NSA_HEADS, QB, CMP_TILE_KEYS)
    return jnp.concatenate([jnp.full((1,) + t.shape[1:], NEG, t.dtype), t], axis=0)


def _overlap_matrix_t(ncp, n_sel_pad):
    n = np.arange(ncp)[None, :] * CMP_STRIDE
    s = np.arange(n_sel_pad)[:, None] * SEL_BLOCK
    ov = np.clip(np.minimum(n + CMP_BLOCK, s + SEL_BLOCK) - np.maximum(n, s), 0, None) / CMP_BLOCK
    return jnp.asarray(ov, BF16)


def _gate_tile(gates_ref, branch, g):
    c = branch * NSA_HEADS + g * 2
    low = _pair_masks(QB)
    return jnp.where(low, jnp.broadcast_to(gates_ref[0, :, c:c + 1], (QB, LANES)),
                     jnp.broadcast_to(gates_ref[0, :, c + 1:c + 2], (QB, LANES)))


def _masked_q(qb_ref, g, kv):
    q = qb_ref[0, :, g * LANES:(g + 1) * LANES]
    low = _pair_masks(QB)
    keep = low if kv == 0 else jnp.logical_not(low)
    return jnp.where(keep, q, jnp.zeros_like(q))


def _cmp_kernel(n_tiles, qb_ref, kcmp_ref, vcmp_ref, gates_ref, ov_ref, *rest):
    tbl_refs, (oc_ref, sel_ref, q_ref, s_ref, p_ref, pv_ref, imp_ref) = rest[:n_tiles], rest[n_tiles:]
    qblk = pl.program_id(1)
    t0 = qblk * QB
    low = _pair_masks(QB)
    for kv in range(NSA_KV_HEADS):
        for g in range(NSA_GROUP):
            r = kv * NSA_GROUP + g
            q_ref[r * QB:(r + 1) * QB, :] = _masked_q(qb_ref, g, kv)

    def attend(n_vis):
        kc = n_vis * CMP_TILE_KEYS
        s_ref[:, 0:kc] = _nt_dot(q_ref[...], kcmp_ref[0, 0:kc, :])
        for kv in range(NSA_KV_HEADS):
            psum = jnp.zeros((QB, kc), F32)
            for g in range(NSA_GROUP):
                r = kv * NSA_GROUP + g
                rows = slice(r * QB, (r + 1) * QB)
                s = s_ref[rows, 0:kc] + jnp.concatenate([tbl_refs[c][0, r].astype(F32) for c in range(n_vis)], axis=1)
                m = jnp.max(s, axis=1, keepdims=True)
                e = jnp.exp2(s - m)
                den = jnp.sum(e, axis=1, keepdims=True)
                p = jnp.where(m > 0.5 * NEG, e * (1.0 / den), 0.0)
                psum = psum + p
                p_ref[rows, 0:kc] = p.astype(BF16)
            hi = psum.astype(BF16)
            lo = (psum - hi.astype(F32)).astype(BF16)
            ov_t = ov_ref[:, 0:kc]
            imp_ref[kv] = _nt_dot(ov_t, hi) + _nt_dot(ov_t, lo)
        pv_ref[...] = _dot(p_ref[:, 0:kc], vcmp_ref[0, 0:kc, :])

    n_vis = qblk // CMP_TILE_SPAN + 1
    for w in range(1, n_tiles + 1):
        pl.when(n_vis == w)(functools.partial(attend, w))

    blk = lax.broadcasted_iota(jnp.int32, (LANES, QB), 0)
    cur = (t0 + lax.broadcasted_iota(jnp.int32, (LANES, QB), 1)) // SEL_BLOCK
    blk_f = blk.astype(F32)
    forced = (blk == cur) | (blk == cur - 1) | (blk == 0)
    causal = blk <= cur
    scores = [jnp.where(forced, FORCE_SCORE, jnp.where(causal, imp_ref[kv], -1.0)) for kv in range(NSA_KV_HEADS)]

    def pick(_, carry):
        new = []
        for val, sel in carry:
            mx = jnp.max(val, axis=0, keepdims=True)
            idx = jnp.min(jnp.where(val == mx, blk_f, float(LANES)), axis=0, keepdims=True)
            hit = blk_f == idx
            new.append((jnp.where(hit, -jnp.inf, val), jnp.where(hit, 1.0, sel)))
        return tuple(new)

    picked = lax.fori_loop(0, SEL_TOPK, pick, tuple((v, jnp.zeros((LANES, QB), F32)) for v in scores))
    eye = jnp.where(lax.broadcasted_iota(jnp.int32, (QB, QB), 0) == lax.broadcasted_iota(jnp.int32, (QB, QB), 1),
                    1.0, 0.0).astype(BF16)
    for kv in range(NSA_KV_HEADS):
        sel_t = jnp.where(causal, picked[kv][1], 0.0).astype(BF16)
        sel_ref[0, kv] = _nt_dot(eye, sel_t).astype(BF16)
    for g in range(NSA_GROUP):
        o0, o1 = pv_ref[g * QB:(g + 1) * QB, :], pv_ref[(NSA_GROUP + g) * QB:(NSA_GROUP + g + 1) * QB, :]
        oc_ref[0, :, g * LANES:(g + 1) * LANES] = (jnp.where(low, o0, o1) * _gate_tile(gates_ref, 0, g)).astype(BF16)


def _compressed_branch(qb, kcmp, vcmp, gates, f_b):
    b, t, _ = qb.shape
    ncp = kcmp.shape[1]
    n_tiles = ncp // CMP_TILE_KEYS
    tbl = _cmp_bias(f_b)
    ov = _overlap_matrix_t(ncp, LANES)

    def tbl_spec(c):
        return pl.BlockSpec((1, NSA_HEADS, QB, CMP_TILE_KEYS),
                            lambda bi, i: (jnp.clip(i - CMP_TILE_SPAN * c, -1, CMP_CONST_DELTA) + 1, 0, 0, 0))

    blockq = lambda w: pl.BlockSpec((1, QB, w), lambda bi, i: (bi, i, 0))
    batch = lambda a: pl.BlockSpec((1,) + a.shape[1:], lambda bi, i: (bi, 0, 0))
    return pl.pallas_call(
        functools.partial(_cmp_kernel, n_tiles),
        grid=(b, t // QB),
        in_specs=[blockq(DB), batch(kcmp), batch(vcmp), blockq(LANES), pl.BlockSpec(ov.shape, lambda bi, i: (0, 0))]
                 + [tbl_spec(c) for c in range(n_tiles)],
        out_specs=[blockq(DB), pl.BlockSpec((1, NSA_KV_HEADS, QB, LANES), lambda bi, i: (bi, 0, i, 0))],
        out_shape=[jax.ShapeDtypeStruct((b, t, DB), BF16), jax.ShapeDtypeStruct((b, NSA_KV_HEADS, t, LANES), BF16)],
        scratch_shapes=[pltpu.VMEM((N_ROWGROUPS * QB, LANES), BF16), pltpu.VMEM((N_ROWGROUPS * QB, ncp), F32),
                        pltpu.VMEM((N_ROWGROUPS * QB, ncp), BF16), pltpu.VMEM((N_ROWGROUPS * QB, LANES), F32),
                        pltpu.VMEM((NSA_KV_HEADS, LANES, QB), F32)],
        compiler_params=pltpu.CompilerParams(dimension_semantics=("parallel", "parallel"), vmem_limit_bytes=VMEM_LIMIT),
        name="nsa_compressed",
    )(qb, kcmp, vcmp, gates, ov, *([tbl] * n_tiles))


SEL_NEAR = 13


def _sel_bias(f_b):
    n_off = SEL_NEAR + 1
    cols = QB * n_off
    rel = f_b - f_b[:, BIAS_LEN - 1:]
    lo = -(QB - 1)
    big = _toeplitz_of(_extend(rel, lo, lo + QB + cols - 1), lo, QB, cols)
    tiles = jnp.flip(big.reshape(NSA_HEADS, QB, n_off, QB).transpose(2, 0, 1, 3), axis=0)
    return jnp.concatenate([jnp.zeros((1,) + tiles.shape[1:], F32), tiles], axis=0)


def _pair_ratio(acc0, acc1):
    low = _pair_masks(acc0.shape[0])
    den = pltpu.roll(jnp.where(low, acc1, acc0), HEAD_DIM, axis=1)
    return jnp.where(low, acc0, acc1) * (1.0 / den)


def _sel_kernel(qb_ref, sel_ref, gates_ref, cfar_ref, ks_ref, vs0_ref, vs1_ref, tbl_ref, out_ref,
                qaug_ref, s_ref, s1_ref, acc_ref, m_ref, alpha_ref):
    qblk = pl.program_id(1)
    for kv in range(NSA_KV_HEADS):
        unchosen = jnp.where(sel_ref[0, kv].astype(F32) > 0.0, 0.0, NEG)
        for g in range(NSA_GROUP):
            r = kv * NSA_GROUP + g
            rows = slice(r * QB, (r + 1) * QB)
            qaug_ref[rows, 0:LANES] = _masked_q(qb_ref, g, kv)
            qaug_ref[rows, LANES:2 * LANES] = (unchosen + cfar_ref[r:r + 1, :]).astype(BF16)
    acc_ref[...] = jnp.zeros_like(acc_ref)
    m_ref[...] = jnp.full_like(m_ref, NEG)

    last_tile = ks_ref.shape[1] // SEL_KT - 1

    def scores(j, dst_ref):
        start = pl.multiple_of(jnp.minimum(j, last_tile) * SEL_KT, SEL_KT)
        dst_ref[...] = _nt_dot(qaug_ref[...], ks_ref[0, pl.ds(start, SEL_KT), :])

    def consume(j, src_ref, near):
        start = pl.multiple_of(j * SEL_KT, SEL_KT)
        for r in range(N_ROWGROUPS):
            rows = slice(r * QB, (r + 1) * QB)
            s = src_ref[rows, :]
            if near:
                e1 = jnp.clip(qblk - 2 * j + 1, 0, SEL_NEAR + 1)
                e2 = jnp.clip(qblk - 2 * j, 0, SEL_NEAR + 1)
                s = s + jnp.concatenate([tbl_ref[e1, r], tbl_ref[e2, r]], axis=1)
                src_ref[rows, :] = s
            m_old = m_ref[rows, :]
            m_new = jnp.maximum(m_old, jnp.max(s, axis=1, keepdims=True))
            alpha_ref[rows, :] = jnp.exp2(m_old - m_new)
            m_ref[rows, :] = m_new
        for r in range(N_ROWGROUPS):
            rows = slice(r * QB, (r + 1) * QB)
            vals = (vs0_ref if r < NSA_GROUP else vs1_ref)[0, pl.ds(start, SEL_KT), :]
            m_new = m_ref[rows, :]
            p = jnp.exp2(src_ref[rows, :] - jnp.concatenate([m_new, m_new], axis=1))
            acc_ref[rows, :] = alpha_ref[rows, :] * acc_ref[rows, :] + _dot(p.astype(BF16), vals)

    def tile_pair(i, near):
        scores(2 * i + 1, s1_ref)
        consume(2 * i, s_ref, near)
        scores(2 * i + 2, s_ref)
        consume(2 * i + 1, s1_ref, near)

    n_pairs = ((qblk + 2) // 2 + 1) // 2
    n_far = jnp.maximum((qblk - (SEL_NEAR - 1)) // 2, 0) // 2
    scores(0, s_ref)
    lax.fori_loop(0, n_far, lambda i, c: (tile_pair(i, False), c)[1], 0)
    lax.fori_loop(n_far, n_pairs, lambda i, c: (tile_pair(i, True), c)[1], 0)
    for g in range(NSA_GROUP):
        ratio = _pair_ratio(acc_ref[g * QB:(g + 1) * QB, :], acc_ref[(NSA_GROUP + g) * QB:(NSA_GROUP + g + 1) * QB, :])
        out_ref[0, :, g * LANES:(g + 1) * LANES] = (ratio * _gate_tile(gates_ref, 1, g)).astype(BF16)


def _selected_branch(qb, sel, gates, ksaug, vs0aug, vs1aug, f_b):
    b, t, _ = qb.shape
    tbl = _sel_bias(f_b)
    cfar = jnp.broadcast_to(f_b[:, BIAS_LEN - 1:], (NSA_HEADS, LANES))
    cfar = jnp.pad(cfar, ((0, 16 - NSA_HEADS), (0, 0)))
    blockq = lambda w: pl.BlockSpec((1, QB, w), lambda bi, i: (bi, i, 0))
    batch = lambda a: pl.BlockSpec((1,) + a.shape[1:], lambda bi, i: (bi, 0, 0))
    rows = N_ROWGROUPS * QB
    return pl.pallas_call(
        _sel_kernel,
        grid=(b, t // QB),
        in_specs=[blockq(DB), pl.BlockSpec((1, NSA_KV_HEADS, QB, LANES), lambda bi, i: (bi, 0, i, 0)), blockq(LANES),
                  pl.BlockSpec(cfar.shape, lambda bi, i: (0, 0)), batch(ksaug), batch(vs0aug), batch(vs1aug),
                  pl.BlockSpec(tbl.shape, lambda bi, i: (0, 0, 0, 0))],
        out_specs=blockq(DB),
        out_shape=jax.ShapeDtypeStruct((b, t, DB), BF16),
        scratch_shapes=[pltpu.VMEM((rows, 2 * LANES), BF16), pltpu.VMEM((rows, SEL_KT), F32), pltpu.VMEM((rows, SEL_KT), F32),
                        pltpu.VMEM((rows, LANES), F32), pltpu.VMEM((rows, LANES), F32), pltpu.VMEM((rows, LANES), F32)],
        compiler_params=pltpu.CompilerParams(dimension_semantics=("parallel", "parallel"), vmem_limit_bytes=VMEM_LIMIT),
        name="nsa_selected",
    )(qb, sel, gates, cfar, ksaug, vs0aug, vs1aug, tbl)


WIN_KEYS = WIN + QB


def _win_bias(f_b):
    lo = WIN - (WIN_KEYS - 1)
    vals = _extend(f_b[:, :WIN], lo, WIN)
    vals = jnp.concatenate([vals, jnp.full((NSA_HEADS, lo + QB + WIN_KEYS - 1 - WIN), NEG, F32)], axis=1)
    return _toeplitz_of(vals, lo, QB, WIN_KEYS)


def _win_kernel(qb_ref, gates_ref, kw_ref, vw0_ref, vw1_ref, tbl_ref, out_ref, q_ref, s_ref):
    qblk = pl.program_id(1)
    for kv in range(NSA_KV_HEADS):
        for g in range(NSA_GROUP):
            r = kv * NSA_GROUP + g
            q_ref[r * QB:(r + 1) * QB, :] = _masked_q(qb_ref, g, kv)
    start = pl.multiple_of(qblk * QB, QB)
    s_ref[...] = _nt_dot(q_ref[...], kw_ref[0, pl.ds(start, WIN_KEYS), :])
    col = lax.broadcasted_iota(jnp.int32, (1, WIN_KEYS), 1)
    pad_mask = jnp.where(col + qblk * QB >= WIN, 0.0, NEG)
    outs = []
    for r in range(N_ROWGROUPS):
        vals = (vw0_ref if r < NSA_GROUP else vw1_ref)[0, pl.ds(start, WIN_KEYS), :]
        s = s_ref[r * QB:(r + 1) * QB, :] + tbl_ref[r] + pad_mask
        e = jnp.exp2(s - jnp.max(s, axis=1, keepdims=True))
        outs.append(_dot(e.astype(BF16), vals))
    for g in range(NSA_GROUP):
        out_ref[0, :, g * LANES:(g + 1) * LANES] = (_pair_ratio(outs[g], outs[NSA_GROUP + g])
                                                    * _gate_tile(gates_ref, 2, g)).astype(BF16)


def _window_branch(qb, gates, kw, vw0aug, vw1aug, f_b):
    b, t, _ = qb.shape
    tbl = _win_bias(f_b)
    pad_front = lambda a: jnp.pad(a, ((0, 0), (WIN, 0), (0, 0)))
    kw_pad, vw0_pad, vw1_pad = pad_front(kw), pad_front(vw0aug), pad_front(vw1aug)
    blockq = lambda w: pl.BlockSpec((1, QB, w), lambda bi, i: (bi, i, 0))
    batch = lambda a: pl.BlockSpec((1,) + a.shape[1:], lambda bi, i: (bi, 0, 0))
    rows = N_ROWGROUPS * QB
    return pl.pallas_call(
        _win_kernel,
        grid=(b, t // QB),
        in_specs=[blockq(DB), blockq(LANES), batch(kw_pad), batch(vw0_pad), batch(vw1_pad),pl.BlockSpec(tbl.shape, lambda bi, i: (0, 0, 0))],
        out_specs=blockq(DB),
        out_shape=jax.ShapeDtypeStruct((b, t, DB), BF16),
        scratch_shapes=[pltpu.VMEM((rows, LANES), BF16), pltpu.VMEM((rows, WIN_KEYS), F32)],
        compiler_params=pltpu.CompilerParams(dimension_semantics=("parallel", "parallel"), vmem_limit_bytes=VMEM_LIMIT),
        name="nsa_window",
    )(qb, gates, kw_pad, vw0_pad, vw1_pad, tbl)


OUT_TM = 256
C_GROUP = N_EXPERTS


def _outproj_kernel(x_ref, oa_ref, oc_ref, os_ref, ow_ref,
                    wout_ref, g_ref, wr_ref, br_ref, h_ref, hn_ref, comb_ref):
    ob = oc_ref[...].astype(F32) + os_ref[...].astype(F32) + ow_ref[...].astype(F32)
    y = _dot(oa_ref[...], wout_ref[0:DA, :]) + _dot(ob.astype(BF16), wout_ref[DA:DA + DB, :])
    h = x_ref[...] + y
    h_ref[...] = h
    hn = h * lax.rsqrt(jnp.mean(h * h, axis=-1, keepdims=True) + EPS) * g_ref[...]
    hn_hi = hn.astype(BF16)
    hn_ref[...] = hn_hi
    hn_lo = (hn - hn_hi.astype(F32)).astype(BF16)
    both = _dot(hn_hi, wr_ref[...])
    logits = both[:, 0:LANES] + both[:, LANES:2 * LANES] + _dot(hn_lo, wr_ref[:, 0:LANES]) + br_ref[...]
    lane = lax.broadcasted_iota(jnp.int32, logits.shape, 1)
    lane_f = lane.astype(F32)
    big = float(LANES)
    gl = jnp.where((lane >= C_GROUP) & (lane < C_GROUP + N_GROUPS), logits, -jnp.inf)
    gmax = jnp.max(gl, axis=1, keepdims=True)
    gidx = jnp.min(jnp.where(gl == gmax, lane_f, big), axis=1, keepdims=True) - C_GROUP
    gprob = 1.0 / jnp.sum(jnp.exp(gl - gmax), axis=1, keepdims=True)
    grp_of_lane = (lane // EXPERTS_PER_GROUP).astype(F32)
    el = jnp.where((lane < N_EXPERTS) & (grp_of_lane == gidx), logits, -jnp.inf)
    v1 = jnp.max(el, axis=1, keepdims=True)
    i1 = jnp.min(jnp.where(el == v1, lane_f, big), axis=1, keepdims=True)
    el2 = jnp.where(lane_f == i1, -jnp.inf, el)
    v2 = jnp.max(el2, axis=1, keepdims=True)
    i2 = jnp.min(jnp.where(el2 == v2, lane_f, big), axis=1, keepdims=True)
    e2 = jnp.exp(v2 - v1)
    p1 = 1.0 / (1.0 + e2)
    comb_ref[...] = gprob * (jnp.where(lane_f == i1, p1, 0.0) + jnp.where(lane_f == i2, e2 * p1, 0.0))


def _outproj(x2d, o_a, b_parts, w_out_perm, gain, w_router, b_router):
    n, d = x2d.shape
    row = lambda w: pl.BlockSpec((OUT_TM, w), lambda i: (i, 0))
    full = lambda a: pl.BlockSpec(a.shape, lambda i: (0, 0))
    return pl.pallas_call(
        _outproj_kernel,
        grid=(n // OUT_TM,),
        in_specs=[row(d), row(DA)] + [row(DB)] * 3 + [full(w_out_perm), pl.BlockSpec((1, d), lambda i: (0, 0)),
                                                      full(w_router), full(b_router)],
        out_specs=[row(d), row(d), row(LANES)],
        out_shape=[jax.ShapeDtypeStruct((n, d), F32), jax.ShapeDtypeStruct((n, d), BF16),
                   jax.ShapeDtypeStruct((n, LANES), F32)],
        compiler_params=pltpu.CompilerParams(dimension_semantics=("parallel",), vmem_limit_bytes=VMEM_LIMIT),
        name="outproj_router",
    )(x2d, o_a, *b_parts, w_out_perm, gain.reshape(1, d), w_router, b_router)


MOE_TM = 1024


def _moe_kernel(h_ref, hn_ref, comb_ref, wg_ref, wu_ref, wd_ref, g_ref, out_ref, acc_ref):
    e = pl.program_id(1)

    @pl.when(e == 0)
    def _():
        acc_ref[...] = jnp.zeros_like(acc_ref)

    hn = hn_ref[...]
    gate = _dot(hn, wg_ref[0])
    up = _dot(hn, wu_ref[0])
    lane = lax.broadcasted_iota(jnp.int32, comb_ref.shape, 1)
    weight = jnp.sum(jnp.where(lane == e, comb_ref[...], 0.0), axis=1, keepdims=True)
    hidden = (gate * jax.nn.sigmoid(gate) * up * weight).astype(BF16)
    acc_ref[...] += _dot(hidden, wd_ref[0])

    @pl.when(e == pl.num_programs(1) - 1)
    def _():
        y = h_ref[...] + acc_ref[...]
        out_ref[...] = y * lax.rsqrt(jnp.mean(y * y, axis=-1, keepdims=True) + EPS) * g_ref[...]


def _moe(h, hn, comb, w_gate, w_up, w_down, gain):
    n, d = h.shape
    tm = min(MOE_TM, n)
    row = lambda w: pl.BlockSpec((tm, w), lambda i, e: (i, 0))
    return pl.pallas_call(
        _moe_kernel,
        grid=(n // tm, N_EXPERTS),
        in_specs=[row(d), row(d), row(LANES),
                  pl.BlockSpec((1, d, D_EXPERT), lambda i, e: (e, 0, 0)),
                  pl.BlockSpec((1, d, D_EXPERT), lambda i, e: (e, 0, 0)),
                  pl.BlockSpec((1, D_EXPERT, d), lambda i, e: (e, 0, 0)),
                  pl.BlockSpec((1, d), lambda i, e: (0, 0))],
        out_specs=row(d),
        out_shape=jax.ShapeDtypeStruct((n, d), F32),
        scratch_shapes=[pltpu.VMEM((tm, d), F32)],
        compiler_params=pltpu.CompilerParams(dimension_semantics=("parallel", "arbitrary"), vmem_limit_bytes=VMEM_LIMIT),
        name="moe_experts",
    )(h, hn, comb, w_gate, w_up, w_down, gain.reshape(1, d))


def _permute_w_out(w_out):
    d = w_out.shape[1]
    wb = w_out[DA:].reshape(NSA_KV_HEADS, NSA_GROUP, HEAD_DIM, d).transpose(1, 0, 2, 3).reshape(DB, d)
    return jnp.concatenate([w_out[:DA], wb], axis=0).astype(BF16)


def _router_weights(w_group, b_group, w_expert, b_expert):
    d = w_group.shape[0]
    w = jnp.concatenate([w_expert.reshape(d, N_EXPERTS), w_group], axis=1)
    b = jnp.concatenate([b_expert.reshape(N_EXPERTS), b_group])
    pad = LANES - w.shape[1]
    w = jnp.pad(w, ((0, 0), (0, pad))).astype(F32)
    w_hi = w.astype(BF16)
    w_lo = (w - w_hi.astype(F32)).astype(BF16)
    return jnp.concatenate([w_hi, w_lo], axis=1), jnp.pad(b, (0, pad)).reshape(1, LANES).astype(F32)


def _layer(h, rel_bias, norm_mix, w_in, w_out, cmp_pos_k, cmp_pos_v, cmp_k_w1, cmp_k_w2, cmp_v_w1, cmp_v_w2,
           norm_ffn, w_rg, b_rg, w_re, b_re, w_gate, w_up, w_down, out_gain):
    b, t, d = h.shape
    n = b * t
    assert t % (QB * DIL_PATTERNS[-1][1]) == 0 and t // SEL_BLOCK <= LANES and n % MOE_TM == 0
    x2d = h.reshape(n, d)
    seq = lambda a: a.reshape(b, t, a.shape[-1])
    qa, ka, va, qb, kc, vc, ksaug, vs0aug, vs1aug, kw, vw0aug, vw1aug, gates = map(
        seq, _inproj(x2d, norm_mix, _permute_w_in(w_in), t))
    f_a = _bias_1d(rel_bias[:, :DIL_HEADS])
    f_b = _bias_1d(rel_bias[:, DIL_HEADS:]) * LOG2E
    o_a = _mixer_a(qa, ka, va, jnp.stack([_dil_bias(f_a, dil) for _, dil in DIL_PATTERNS]))
    kcmp, vcmp = _compress(kc, vc, cmp_pos_k, cmp_pos_v, cmp_k_w1, cmp_k_w2, cmp_v_w1, cmp_v_w2)
    o_cmp, sel = _compressed_branch(qb, kcmp, vcmp, gates, f_b)
    o_sel = _selected_branch(qb, sel, gates, ksaug, vs0aug, vs1aug, f_b)
    o_win = _window_branch(qb, gates, kw, vw0aug, vw1aug, f_b)
    b_parts = [o.reshape(n, DB) for o in (o_cmp, o_sel, o_win)]
    w_router, b_router = _router_weights(w_rg, b_rg, w_re, b_re)
    h2, hn, comb = _outproj(x2d, o_a.reshape(n, DA), b_parts, _permute_w_out(w_out), norm_ffn, w_router, b_router)
    return _moe(h2, hn, comb, w_gate.astype(BF16), w_up.astype(BF16), w_down.astype(BF16), out_gain)


def kernel(x, rel_bias, norm_mix, w_in, w_out, cmp_pos_k, cmp_pos_v, cmp_k_w1, cmp_k_w2, cmp_v_w1, cmp_v_w2,
           norm_ffn, w_router_group, b_router_group, w_router_expert, b_router_expert, w_gate, w_up, w_down,
           norm_final):
    depth = norm_mix.shape[0]
    assert depth == 1, "the final RMSNorm is fused into the last layer's expert kernel"
    out = _layer(x, rel_bias, norm_mix[0], w_in[0], w_out[0], cmp_pos_k[0], cmp_pos_v[0], cmp_k_w1[0], cmp_k_w2[0],
                 cmp_v_w1[0], cmp_v_w2[0], norm_ffn[0], w_router_group[0], b_router_group[0], w_router_expert[0],
                 b_router_expert[0], w_gate[0], w_up[0], w_down[0], norm_final)
    return out.reshape(x.shape)
```

```python
import functools
import math

import jax
import jax.numpy as jnp
import numpy as np
from jax import lax
from jax.experimental import pallas as pl
from jax.experimental.pallas import tpu as pltpu

HEAD_DIM = 64
DIL_HEADS = 6
NSA_KV_HEADS = 2
NSA_GROUP = 5
NSA_HEADS = NSA_KV_HEADS * NSA_GROUP
N_HEADS = DIL_HEADS + NSA_HEADS
DIL_PATTERNS = ((128, 1), (512, 4), (2048, 16))
CMP_BLOCK = 32
CMP_STRIDE = 16
CMP_HIDDEN = 256
SEL_BLOCK = 64
SEL_TOPK = 16
WIN = 512
FORCE_SCORE = 1.0e4
N_BUCKETS = 32
MAX_DISTANCE = 2048
N_GROUPS = 4
EXPERTS_PER_GROUP = 4
N_EXPERTS = N_GROUPS * EXPERTS_PER_GROUP
D_EXPERT = 512
EPS = 1e-6

LANES = 128
QB = 128
NEG = -1.0e30
LOG2E = math.log2(math.e)
DA = DIL_HEADS * HEAD_DIM
DB = NSA_HEADS * HEAD_DIM
N_ROWGROUPS = NSA_HEADS
SEL_KT = 256
VMEM_LIMIT = 56 * 1024 * 1024

F32 = jnp.float32
BF16 = jnp.bfloat16
NT_DIMS = (((1,), (1,)), ((), ()))


def _nt_dot(a, b):
    return lax.dot_general(a, b, NT_DIMS, preferred_element_type=F32)


def _dot(a, b):
    return jnp.dot(a, b, preferred_element_type=F32)


def _bucket_np(dist):
    dist = np.maximum(np.asarray(dist, np.int64), 0)
    max_exact = N_BUCKETS // 2
    x = np.maximum(dist, 1).astype(np.float32) / np.float32(max_exact)
    large = max_exact + (np.log(x) / np.float32(math.log(MAX_DISTANCE / max_exact))
                         * np.float32(N_BUCKETS - max_exact)).astype(np.int32)
    large = np.minimum(large, N_BUCKETS - 1)
    return np.where(dist < max_exact, dist, large).astype(np.int32)


BIAS_LEN = 4096


def _bias_1d(rel_bias_heads):
    onehot = (_bucket_np(np.arange(BIAS_LEN))[None, :] == np.arange(N_BUCKETS)[:, None]).astype(np.float32)
    return jnp.dot(rel_bias_heads.T.astype(F32), jnp.asarray(onehot), precision=lax.Precision.HIGHEST)


def _extend(f, lo, hi):
    assert hi <= f.shape[-1]
    if lo >= 0:
        return f[..., lo:hi]
    pad = jnp.full(f.shape[:-1] + (-lo,), NEG, f.dtype)
    return jnp.concatenate([pad, f[..., :hi]], axis=-1)


def _toeplitz(w, q, c):
    n = q + c - 1
    assert w.shape[-1] == n
    lead = w.shape[:-1]
    wp = jnp.concatenate([w, jnp.zeros(lead + (1,), w.dtype)], axis=-1)
    flat = jnp.broadcast_to(wp[..., None, :], lead + (q, n + 1)).reshape(lead + (q * (n + 1),))
    return flat[..., :q * n].reshape(lead + (q, n))[..., q - 1:q - 1 + c]


def _toeplitz_of(fn_vals, lo, q, c):
    return _toeplitz(jnp.flip(fn_vals, axis=-1), q, c)


IN_TM = 512
C_QA, C_KA, C_VA = 0, DA, 2 * DA
C_QB = 3 * DA
C_KC = C_QB + DB
C_VC, C_KS, C_VS, C_KW, C_VW, C_GT = (C_KC + LANES * i for i in range(1, 7))
N_COLS = C_GT + LANES


def _permute_w_in(w_in):
    scale = 1.0 / math.sqrt(HEAD_DIM)
    sizes = [DA] * 3 + [DB] + [NSA_KV_HEADS * HEAD_DIM] * 6 + [3 * NSA_HEADS]
    offs = np.concatenate([[0], np.cumsum(sizes)])
    part = lambda i: w_in[:, offs[i]:offs[i + 1]]
    d = w_in.shape[0]
    qb = part(3).reshape(d, NSA_KV_HEADS, NSA_GROUP, HEAD_DIM).transpose(0, 2, 1, 3).reshape(d, DB)
    gt = part(10).reshape(d, NSA_KV_HEADS, NSA_GROUP, 3).transpose(0, 3, 2, 1).reshape(d, 3 * NSA_HEADS)
    gt = jnp.pad(gt, ((0, 0), (0, LANES - 3 * NSA_HEADS)))
    cols = [part(0) * scale, part(1), part(2), qb * (scale * LOG2E)] + [part(i) for i in range(4, 10)] + [gt]
    return jnp.concatenate(cols, axis=1).astype(BF16)


def _inproj_kernel(seq_len, x_ref, g_ref, w_ref, qa_ref, ka_ref, va_ref, qb_ref, kc_ref, vc_ref,
                   ksaug_ref, vs0_ref, vs1_ref, kw_ref, vw0_ref, vw1_ref, gates_ref):
    x = x_ref[...]
    xn = (x * lax.rsqrt(jnp.mean(x * x, axis=-1, keepdims=True) + EPS) * g_ref[...]).astype(BF16)
    seg = lambda a, n: _dot(xn, w_ref[:, a:a + n])
    qa_ref[...] = seg(C_QA, DA).astype(BF16)
    ka_ref[...] = seg(C_KA, DA).astype(BF16)
    va_ref[...] = seg(C_VA, DA).astype(BF16)
    qb_ref[...] = seg(C_QB, DB).astype(BF16)
    kc_ref[...] = seg(C_KC, LANES).astype(BF16)
    vc_ref[...] = seg(C_VC, LANES).astype(BF16)
    kw_ref[...] = seg(C_KW, LANES).astype(BF16)
    tm = x.shape[0]
    tok = (pl.program_id(0) * tm) % seq_len + lax.broadcasted_iota(jnp.int32, (tm, LANES), 0)
    lane = lax.broadcasted_iota(jnp.int32, (tm, LANES), 1)
    ksaug_ref[:, 0:LANES] = seg(C_KS, LANES).astype(BF16)
    ksaug_ref[:, LANES:2 * LANES] = jnp.where(lane == tok // SEL_BLOCK, 1.0, 0.0).astype(BF16)
    low = lane < HEAD_DIM
    for col, ref0, ref1 in ((C_VS, vs0_ref, vs1_ref), (C_VW, vw0_ref, vw1_ref)):
        v = seg(col, LANES)
        ref0[...] = jnp.where(low, v, 1.0).astype(BF16)
        ref1[...] = jnp.where(low, 1.0, v).astype(BF16)
    gates_ref[...] = jax.nn.sigmoid(seg(C_GT, LANES))


def _inproj(x2d, gain, w_perm, seq_len):
    n, d = x2d.shape
    row = lambda w: pl.BlockSpec((IN_TM, w), lambda i: (i, 0))
    widths = [DA, DA, DA, DB, LANES, LANES, 2 * LANES, LANES, LANES, LANES, LANES, LANES]
    out_shape = [jax.ShapeDtypeStruct((n, w), BF16) for w in widths] + [jax.ShapeDtypeStruct((n, LANES), F32)]
    return pl.pallas_call(
        functools.partial(_inproj_kernel, seq_len),
        grid=(n // IN_TM,),
        in_specs=[row(d), pl.BlockSpec((1, d), lambda i: (0, 0)), pl.BlockSpec((d, N_COLS), lambda i: (0, 0))],
        out_specs=[row(w) for w in widths] + [row(LANES)],
        out_shape=out_shape,
        compiler_params=pltpu.CompilerParams(dimension_semantics=("parallel",), vmem_limit_bytes=VMEM_LIMIT),
        name="inproj",
    )(x2d, gain.reshape(1, d), w_perm)


def _embed_pair(w, n_tok):
    c = w.shape[1]
    w4 = w.reshape(n_tok, 1, HEAD_DIM, 1, c) * jnp.eye(NSA_KV_HEADS, dtype=w.dtype).reshape(1, 2, 1, 2, 1)
    return w4.reshape(n_tok * 2 * HEAD_DIM, 2 * c)


def _gelu_tanh(x):
    return 0.5 * x * (1.0 + jnp.tanh(math.sqrt(2.0 / math.pi) * (x + 0.044715 * (x * x * x))))


def _compress_kernel(ck_ref, cv_ref, posk_ref, posv_ref, wk1a, wk1b, wk2, wv1a, wv1b, wv2,
                     kout_ref, vout_ref, shift_ref):
    ncp = ck_ref.shape[1]
    for c_ref, pos_ref, w1a, w1b, w2, out_ref in ((ck_ref, posk_ref, wk1a, wk1b, wk2, kout_ref),
                                                  (cv_ref, posv_ref, wv1a, wv1b, wv2, vout_ref)):
        c = c_ref[0].astype(F32)
        first = _dot((c + pos_ref[0:1, :]).astype(BF16), w1a[...])
        second = _dot((c + pos_ref[1:2, :]).astype(BF16), w1b[...])
        shift_ref[0:ncp, :] = second
        shift_ref[ncp:ncp + 8, :] = jnp.zeros((8, second.shape[1]), F32)
        hidden = _gelu_tanh(first + shift_ref[1:ncp + 1, :])
        out_ref[0] = _dot(hidden.astype(BF16), w2[...]).astype(BF16)


def _compress(kc, vc, pos_k, pos_v, k_w1, k_w2, v_w1, v_w2):
    b, t, _ = kc.shape
    ncp = t // CMP_STRIDE
    half = CMP_STRIDE * HEAD_DIM
    wide = CMP_STRIDE * LANES

    def prep(w1, w2, pos):
        pos_pair = jnp.broadcast_to(pos.reshape(2, CMP_STRIDE, 1, HEAD_DIM), (2, CMP_STRIDE, 2, HEAD_DIM))
        return (_embed_pair(w1[:half], CMP_STRIDE).astype(BF16), _embed_pair(w1[half:], CMP_STRIDE).astype(BF16),
                jnp.kron(jnp.eye(NSA_KV_HEADS, dtype=w2.dtype), w2).astype(BF16),
                pos_pair.reshape(2, wide).astype(F32))

    wk1a, wk1b, wk2, posk = prep(k_w1, k_w2, pos_k)
    wv1a, wv1b, wv2, posv = prep(v_w1, v_w2, pos_v)
    full = lambda a: pl.BlockSpec(a.shape, lambda i: (0,) * a.ndim)
    tok = pl.BlockSpec((1, ncp, wide), lambda i: (i, 0, 0))
    out = pl.BlockSpec((1, ncp, LANES), lambda i: (i, 0, 0))
    return pl.pallas_call(
        _compress_kernel,
        grid=(b,),
        in_specs=[tok, tok, full(posk), full(posv), full(wk1a), full(wk1b), full(wk2), full(wv1a), full(wv1b), full(wv2)],
        out_specs=[out, out],
        out_shape=[jax.ShapeDtypeStruct((b, ncp, LANES), BF16)] * 2,
        scratch_shapes=[pltpu.VMEM((ncp + 8, 2 * CMP_HIDDEN), F32)],
        compiler_params=pltpu.CompilerParams(dimension_semantics=("parallel",), vmem_limit_bytes=VMEM_LIMIT),
        name="compress",
    )(kc.reshape(b, ncp, wide), vc.reshape(b, ncp, wide), posk, posv, wk1a, wk1b, wk2, wv1a, wv1b, wv2)


def _pair_masks(rows):
    lane = lax.broadcasted_iota(jnp.int32, (rows, LANES), 1)
    return lane < HEAD_DIM


A_SUPER = QB * DIL_PATTERNS[-1][1]


def _mixer_a_kernel(q_ref, kp_ref, kc_ref, vp_ref, vc_ref, bias_ref, out_ref, qf_ref, kf_ref, vf_ref, o_ref, lse_ref):
    first = pl.program_id(1) == 0
    n_pairs = DIL_HEADS // 2
    for p in range(n_pairs):
        cs = slice(p * LANES, (p + 1) * LANES)
        qf_ref[p] = q_ref[0, :, cs].astype(F32)
        kf_ref[p, 0:A_SUPER, :] = kp_ref[0, :, cs].astype(F32)
        kf_ref[p, A_SUPER:2 * A_SUPER, :] = kc_ref[0, :, cs].astype(F32)
        vf_ref[p, 0:A_SUPER, :] = vp_ref[0, :, cs].astype(F32)
        vf_ref[p, A_SUPER:2 * A_SUPER, :] = vc_ref[0, :, cs].astype(F32)
    low = _pair_masks(QB)
    in_prev = lax.broadcasted_iota(jnp.int32, (QB, 2 * QB), 1) < QB
    zero = jnp.zeros((QB, LANES), BF16)

    def chunk(idx, dil, q_base, k_base, q_span, k_span, off, at_start):
        rows = lambda size: pl.ds(off, size) if dil == 1 else pl.ds(off, size, stride=dil)
        q_win = lambda ref: ref.at[pl.ds(pl.multiple_of(q_base, 8), q_span), :]
        k_win = lambda ref: ref.at[pl.ds(pl.multiple_of(k_base, 8), k_span), :]
        q_rows = rows(QB)
        prev_mask = jnp.where(jnp.logical_and(in_prev, jnp.logical_and(first, at_start)), NEG, 0.0)
        for p in range(n_pairs):
            q = q_win(qf_ref.at[p])[q_rows, :].astype(BF16)
            keys = k_win(kf_ref.at[p])[rows(2 * QB), :].astype(BF16)
            vals = k_win(vf_ref.at[p])[rows(2 * QB), :].astype(BF16)
            o_win, lse_win = q_win(o_ref.at[p]), q_win(lse_ref.at[p])
            lhs = jnp.concatenate([jnp.where(low, q, zero), jnp.where(low, zero, q)], axis=0)
            bias = jnp.concatenate([bias_ref[idx, 2 * p] + prev_mask, bias_ref[idx, 2 * p + 1] + prev_mask], axis=0)
            s = _nt_dot(lhs, keys) + bias
            m = jnp.max(s, axis=1, keepdims=True)
            e = jnp.exp(s - m)
            l = jnp.sum(e, axis=1, keepdims=True)
            pv = _dot(e.astype(BF16), vals) * (1.0 / l)
            lse = m + jnp.log(l)
            o_new = jnp.where(low, pv[:QB], pv[QB:])
            l_new = jnp.where(low, jnp.broadcast_to(lse[:QB], (QB, LANES)), jnp.broadcast_to(lse[QB:], (QB, LANES)))
            if idx > 0:
                o_old, l_old = o_win[q_rows, :], lse_win[q_rows, :]
                mx = jnp.maximum(l_old, l_new)
                w_old, w_new = jnp.exp(l_old - mx), jnp.exp(l_new - mx)
                tot = w_old + w_new
                o_new = (w_old * o_old + w_new * o_new) * (1.0 / tot)
                l_new = mx + jnp.log(tot)
            o_win[q_rows, :] = o_new
            lse_win[q_rows, :] = l_new

    def loop(n, body):
        lax.fori_loop(0, n, lambda i, carry: (body(i), carry)[1], 0)

    for idx, (_, dil) in enumerate(DIL_PATTERNS):
        span = QB * dil
        n_chunks = A_SUPER // span
        if n_chunks > 1:
            for r in range(dil):
                loop(n_chunks, lambda c, idx=idx, dil=dil, span=span, r=r:
                     chunk(idx, dil, span * c, A_SUPER + span * (c - 1), span, 2 * span, r, c == 0))
        else:
            for r in range(8):
                loop(dil // 8, lambda hi, idx=idx, dil=dil, span=span, r=r:
                     chunk(idx, dil, 8 * hi, 8 * hi, span - 8, 2 * span - 8, r, True))
    for p in range(n_pairs):
        out_ref[0, :, p * LANES:(p + 1) * LANES] = o_ref[p].astype(BF16)


def _mixer_a(qa, ka, va, bias):
    b, t, _ = qa.shape
    cur = pl.BlockSpec((1, A_SUPER, DA), lambda bi, i: (bi, i, 0))
    prev = pl.BlockSpec((1, A_SUPER, DA), lambda bi, i: (bi, jnp.maximum(i - 1, 0), 0))
    return pl.pallas_call(
        _mixer_a_kernel,
        grid=(b, t // A_SUPER),
        in_specs=[cur, prev, cur, prev, cur, pl.BlockSpec(bias.shape, lambda bi, i: (0, 0, 0, 0))],
        out_specs=cur,
        out_shape=jax.ShapeDtypeStruct((b, t, DA), BF16),
        scratch_shapes=[pltpu.VMEM((DIL_HEADS // 2, rows, LANES), F32) for rows in (A_SUPER, 2 * A_SUPER, 2 * A_SUPER, A_SUPER, A_SUPER)],
        compiler_params=pltpu.CompilerParams(dimension_semantics=("parallel", "parallel"), vmem_limit_bytes=VMEM_LIMIT),
        name="mixer_a",
    )(qa, ka, ka, va, va, bias)


def _dil_bias(f_a, dil):
    steps = DIL_PATTERNS[0][0]
    g = f_a[:, 0:dil * steps + 1:dil]
    lo, hi = QB - (2 * QB - 1), QB + QB
    vals = jnp.concatenate([jnp.full((DIL_HEADS, -lo), NEG, F32), g, jnp.full((DIL_HEADS, hi - steps - 1), NEG, F32)], axis=1)
    return _toeplitz_of(vals, lo, QB, 2 * QB)


CMP_TILE_KEYS = LANES
CMP_TILE_SPAN = CMP_TILE_KEYS * CMP_STRIDE // QB
CMP_CONST_DELTA = 28


def _cmp_bias(f_b):
    per = QB // CMP_STRIDE
    n_rows = per * (CMP_CONST_DELTA + 1)
    m_lo, m_hi = -(CMP_TILE_KEYS - 1), n_rows
    f_b = f_b.astype(BF16)
    base = _extend(f_b, CMP_STRIDE * m_lo - (CMP_BLOCK - 1), CMP_STRIDE * m_hi - (CMP_BLOCK - 1))
    g = base.reshape(NSA_HEADS, m_hi - m_lo, CMP_STRIDE).transpose(0, 2, 1)
    t = _toeplitz_of(g, m_lo, n_rows, CMP_TILE_KEYS)
    t = t.reshape(NSA_HEADS, CMP_STRIDE, CMP_CONST_DELTA + 1, per, CMP_TILE_KEYS).transpose(2, 0, 3, 1, 4)
    t = t.reshape(CMP_CONST_DELTA + 1, NSA_HEADS, QB, CMP_TILE_KEYS)
    return jnp.concatenate([jnp.full((1,) + t.shape[1:], NEG, t.dtype), t], axis=0)


def _overlap_matrix_t(ncp, n_sel_pad):
    n = np.arange(ncp)[None, :] * CMP_STRIDE
    s = np.arange(n_sel_pad)[:, None] * SEL_BLOCK
    ov = np.clip(np.minimum(n + CMP_BLOCK, s + SEL_BLOCK) - np.maximum(n, s), 0, None) / CMP_BLOCK
    return jnp.asarray(ov, BF16)


def _gate_tile(gates_ref, branch, g):
    c = branch * NSA_HEADS + g * 2
    low = _pair_masks(QB)
    return jnp.where(low, jnp.broadcast_to(gates_ref[0, :, c:c + 1], (QB, LANES)),
                     jnp.broadcast_to(gates_ref[0, :, c + 1:c + 2], (QB, LANES)))


def _masked_q(qb_ref, g, kv):
    q = qb_ref[0, :, g * LANES:(g + 1) * LANES]
    low = _pair_masks(QB)
    keep = low if kv == 0 else jnp.logical_not(low)
    return jnp.where(keep, q, jnp.zeros_like(q))


def _cmp_kernel(n_tiles, qb_ref, kcmp_ref, vcmp_ref, gates_ref, ov_ref, *rest):
    tbl_refs, (oc_ref, sel_ref, q_ref, s_ref, p_ref, pv_ref, imp_ref) = rest[:n_tiles], rest[n_tiles:]
    qblk = pl.program_id(1)
    t0 = qblk * QB
    low = _pair_masks(QB)
    for kv in range(NSA_KV_HEADS):
        for g in range(NSA_GROUP):
            r = kv * NSA_GROUP + g
            q_ref[r * QB:(r + 1) * QB, :] = _masked_q(qb_ref, g, kv)

    def attend(n_vis):
        kc = n_vis * CMP_TILE_KEYS
        s_ref[:, 0:kc] = _nt_dot(q_ref[...], kcmp_ref[0, 0:kc, :])
        for kv in range(NSA_KV_HEADS):
            psum = jnp.zeros((QB, kc), F32)
            for g in range(NSA_GROUP):
                r = kv * NSA_GROUP + g
                rows = slice(r * QB, (r + 1) * QB)
                s = s_ref[rows, 0:kc] + jnp.concatenate([tbl_refs[c][0, r].astype(F32) for c in range(n_vis)], axis=1)
                m = jnp.max(s, axis=1, keepdims=True)
                e = jnp.exp2(s - m)
                den = jnp.sum(e, axis=1, keepdims=True)
                p = jnp.where(m > 0.5 * NEG, e * (1.0 / den), 0.0)
                psum = psum + p
                p_ref[rows, 0:kc] = p.astype(BF16)
            hi = psum.astype(BF16)
            lo = (psum - hi.astype(F32)).astype(BF16)
            ov_t = ov_ref[:, 0:kc]
            imp_ref[kv] = _nt_dot(ov_t, hi) + _nt_dot(ov_t, lo)
        pv_ref[...] = _dot(p_ref[:, 0:kc], vcmp_ref[0, 0:kc, :])

    n_vis = qblk // CMP_TILE_SPAN + 1
    for w in range(1, n_tiles + 1):
        pl.when(n_vis == w)(functools.partial(attend, w))

    blk = lax.broadcasted_iota(jnp.int32, (LANES, QB), 0)
    cur = (t0 + lax.broadcasted_iota(jnp.int32, (LANES, QB), 1)) // SEL_BLOCK
    blk_f = blk.astype(F32)
    forced = (blk == cur) | (blk == cur - 1) | (blk == 0)
    causal = blk <= cur
    scores = [jnp.where(forced, FORCE_SCORE, jnp.where(causal, imp_ref[kv], -1.0)) for kv in range(NSA_KV_HEADS)]

    def pick(_, carry):
        new = []
        for val, sel in carry:
            mx = jnp.max(val, axis=0, keepdims=True)
            idx = jnp.min(jnp.where(val == mx, blk_f, float(LANES)), axis=0, keepdims=True)
            hit = blk_f == idx
            new.append((jnp.where(hit, -jnp.inf, val), jnp.where(hit, 1.0, sel)))
        return tuple(new)

    picked = lax.fori_loop(0, SEL_TOPK, pick, tuple((v, jnp.zeros((LANES, QB), F32)) for v in scores))
    eye = jnp.where(lax.broadcasted_iota(jnp.int32, (QB, QB), 0) == lax.broadcasted_iota(jnp.int32, (QB, QB), 1),
                    1.0, 0.0).astype(BF16)
    for kv in range(NSA_KV_HEADS):
        sel_t = jnp.where(causal, picked[kv][1], 0.0).astype(BF16)
        sel_ref[0, kv] = _nt_dot(eye, sel_t).astype(BF16)
    for g in range(NSA_GROUP):
        o0, o1 = pv_ref[g * QB:(g + 1) * QB, :], pv_ref[(NSA_GROUP + g) * QB:(NSA_GROUP + g + 1) * QB, :]
        oc_ref[0, :, g * LANES:(g + 1) * LANES] = (jnp.where(low, o0, o1) * _gate_tile(gates_ref, 0, g)).astype(BF16)


def _compressed_branch(qb, kcmp, vcmp, gates, f_b):
    b, t, _ = qb.shape
    ncp = kcmp.shape[1]
    n_tiles = ncp // CMP_TILE_KEYS
    tbl = _cmp_bias(f_b)
    ov = _overlap_matrix_t(ncp, LANES)

    def tbl_spec(c):
        return pl.BlockSpec((1, NSA_HEADS, QB, CMP_TILE_KEYS),
                            lambda bi, i: (jnp.clip(i - CMP_TILE_SPAN * c, -1, CMP_CONST_DELTA) + 1, 0, 0, 0))

    blockq = lambda w: pl.BlockSpec((1, QB, w), lambda bi, i: (bi, i, 0))
    batch = lambda a: pl.BlockSpec((1,) + a.shape[1:], lambda bi, i: (bi, 0, 0))
    return pl.pallas_call(
        functools.partial(_cmp_kernel, n_tiles),
        grid=(b, t // QB),
        in_specs=[blockq(DB), batch(kcmp), batch(vcmp), blockq(LANES), pl.BlockSpec(ov.shape, lambda bi, i: (0, 0))]
                 + [tbl_spec(c) for c in range(n_tiles)],
        out_specs=[blockq(DB), pl.BlockSpec((1, NSA_KV_HEADS, QB, LANES), lambda bi, i: (bi, 0, i, 0))],
        out_shape=[jax.ShapeDtypeStruct((b, t, DB), BF16), jax.ShapeDtypeStruct((b, NSA_KV_HEADS, t, LANES), BF16)],
        scratch_shapes=[pltpu.VMEM((N_ROWGROUPS * QB, LANES), BF16), pltpu.VMEM((N_ROWGROUPS * QB, ncp), F32),
                        pltpu.VMEM((N_ROWGROUPS * QB, ncp), BF16), pltpu.VMEM((N_ROWGROUPS * QB, LANES), F32),
                        pltpu.VMEM((NSA_KV_HEADS, LANES, QB), F32)],
        compiler_params=pltpu.CompilerParams(dimension_semantics=("parallel", "parallel"), vmem_limit_bytes=VMEM_LIMIT),
        name="nsa_compressed",
    )(qb, kcmp, vcmp, gates, ov, *([tbl] * n_tiles))


SEL_NEAR = 13


def _sel_bias(f_b):
    n_off = SEL_NEAR + 1
    cols = QB * n_off
    rel = f_b - f_b[:, BIAS_LEN - 1:]
    lo = -(QB - 1)
    big = _toeplitz_of(_extend(rel, lo, lo + QB + cols - 1), lo, QB, cols)
    tiles = jnp.flip(big.reshape(NSA_HEADS, QB, n_off, QB).transpose(2, 0, 1, 3), axis=0)
    return jnp.concatenate([jnp.zeros((1,) + tiles.shape[1:], F32), tiles], axis=0)


def _pair_ratio(acc0, acc1):
    low = _pair_masks(acc0.shape[0])
    den = pltpu.roll(jnp.where(low, acc1, acc0), HEAD_DIM, axis=1)
    return jnp.where(low, acc0, acc1) * (1.0 / den)


def _sel_kernel(qb_ref, sel_ref, gates_ref, cfar_ref, ks_ref, vs0_ref, vs1_ref, tbl_ref, out_ref,
                qaug_ref, s_ref, s1_ref, acc_ref, m_ref, alpha_ref):
    qblk = pl.program_id(1)
    for kv in range(NSA_KV_HEADS):
        unchosen = jnp.where(sel_ref[0, kv].astype(F32) > 0.0, 0.0, NEG)
        for g in range(NSA_GROUP):
            r = kv * NSA_GROUP + g
            rows = slice(r * QB, (r + 1) * QB)
            qaug_ref[rows, 0:LANES] = _masked_q(qb_ref, g, kv)
            qaug_ref[rows, LANES:2 * LANES] = (unchosen + cfar_ref[r:r + 1, :]).astype(BF16)
    acc_ref[...] = jnp.zeros_like(acc_ref)
    m_ref[...] = jnp.full_like(m_ref, NEG)

    last_tile = ks_ref.shape[2] // SEL_KT - 1

    def scores(j, dst_ref):
        start = pl.multiple_of(jnp.minimum(j, last_tile) * SEL_KT, SEL_KT)
        dst_ref[...] = _dot(qaug_ref[...], ks_ref[0, :, pl.ds(start, SEL_KT)])

    def consume(j, src_ref, near):
        start = pl.multiple_of(j * SEL_KT, SEL_KT)
        for r in range(N_ROWGROUPS):
            rows = slice(r * QB, (r + 1) * QB)
            s = src_ref[rows, :]
            if near:
                e1 = jnp.clip(qblk - 2 * j + 1, 0, SEL_NEAR + 1)
                e2 = jnp.clip(qblk - 2 * j, 0, SEL_NEAR + 1)
                s = s + jnp.concatenate([tbl_ref[e1, r], tbl_ref[e2, r]], axis=1)
                src_ref[rows, :] = s
            m_old = m_ref[rows, :]
            m_new = jnp.maximum(m_old, jnp.max(s, axis=1, keepdims=True))
            alpha_ref[rows, :] = jnp.exp2(m_old - m_new)
            m_ref[rows, :] = m_new
        for r in range(N_ROWGROUPS):
            rows = slice(r * QB, (r + 1) * QB)
            vals = (vs0_ref if r < NSA_GROUP else vs1_ref)[0, pl.ds(start, SEL_KT), :]
            m_new = m_ref[rows, :]
            p = jnp.exp2(src_ref[rows, :] - jnp.concatenate([m_new, m_new], axis=1))
            acc_ref[rows, :] = alpha_ref[rows, :] * acc_ref[rows, :] + _dot(p.astype(BF16), vals)

    def tile_pair(i, near):
        scores(2 * i + 1, s1_ref)
        consume(2 * i, s_ref, near)
        scores(2 * i + 2, s_ref)
        consume(2 * i + 1, s1_ref, near)

    n_pairs = ((qblk + 2) // 2 + 1) // 2
    n_far = jnp.maximum((qblk - (SEL_NEAR - 1)) // 2, 0) // 2
    scores(0, s_ref)
    lax.fori_loop(0, n_far, lambda i, c: (tile_pair(i, False), c)[1], 0)
    lax.fori_loop(n_far, n_pairs, lambda i, c: (tile_pair(i, True), c)[1], 0)
    for g in range(NSA_GROUP):
        ratio = _pair_ratio(acc_ref[g * QB:(g + 1) * QB, :], acc_ref[(NSA_GROUP + g) * QB:(NSA_GROUP + g + 1) * QB, :])
        out_ref[0, :, g * LANES:(g + 1) * LANES] = (ratio * _gate_tile(gates_ref, 1, g)).astype(BF16)


def _selected_branch(qb, sel, gates, ksaug, vs0aug, vs1aug, f_b):
    b, t, _ = qb.shape
    tbl = _sel_bias(f_b)
    ks_t = jnp.swapaxes(ksaug, 1, 2)
    cfar = jnp.broadcast_to(f_b[:, BIAS_LEN - 1:], (NSA_HEADS, LANES))
    cfar = jnp.pad(cfar, ((0, 16 - NSA_HEADS), (0, 0)))
    blockq = lambda w: pl.BlockSpec((1, QB, w), lambda bi, i: (bi, i, 0))
    batch = lambda a: pl.BlockSpec((1,) + a.shape[1:], lambda bi, i: (bi, 0, 0))
    rows = N_ROWGROUPS * QB
    return pl.pallas_call(
        _sel_kernel,
        grid=(b, t // QB),
        in_specs=[blockq(DB), pl.BlockSpec((1, NSA_KV_HEADS, QB, LANES), lambda bi, i: (bi, 0, i, 0)), blockq(LANES),
                  pl.BlockSpec(cfar.shape, lambda bi, i: (0, 0)), batch(ks_t), batch(vs0aug), batch(vs1aug),
                  pl.BlockSpec(tbl.shape, lambda bi, i: (0, 0, 0, 0))],
        out_specs=blockq(DB),
        out_shape=jax.ShapeDtypeStruct((b, t, DB), BF16),
        scratch_shapes=[pltpu.VMEM((rows, 2 * LANES), BF16), pltpu.VMEM((rows, SEL_KT), F32), pltpu.VMEM((rows, SEL_KT), F32),
                        pltpu.VMEM((rows, LANES), F32), pltpu.VMEM((rows, LANES), F32), pltpu.VMEM((rows, LANES), F32)],
        compiler_params=pltpu.CompilerParams(dimension_semantics=("parallel", "parallel"), vmem_limit_bytes=VMEM_LIMIT),
        name="nsa_selected",
    )(qb, sel, gates, cfar, ks_t, vs0aug, vs1aug, tbl)


WIN_KEYS = WIN + QB


def _win_bias(f_b):
    lo = WIN - (WIN_KEYS - 1)
    vals = _extend(f_b[:, :WIN], lo, WIN)
    vals = jnp.concatenate([vals, jnp.full((NSA_HEADS, lo + QB + WIN_KEYS - 1 - WIN), NEG, F32)], axis=1)
    return _toeplitz_of(vals, lo, QB, WIN_KEYS)


def _win_kernel(qb_ref, gates_ref, kw_ref, vw0_ref, vw1_ref, tbl_ref, out_ref, q_ref, s_ref):
    qblk = pl.program_id(1)
    for kv in range(NSA_KV_HEADS):
        for g in range(NSA_GROUP):
            r = kv * NSA_GROUP + g
            q_ref[r * QB:(r + 1) * QB, :] = _masked_q(qb_ref, g, kv)
    start = pl.multiple_of(qblk * QB, QB)
    s_ref[...] = _nt_dot(q_ref[...], kw_ref[0, pl.ds(start, WIN_KEYS), :])
    col = lax.broadcasted_iota(jnp.int32, (1, WIN_KEYS), 1)
    pad_mask = jnp.where(col + qblk * QB >= WIN, 0.0, NEG)
    outs = []
    for r in range(N_ROWGROUPS):
        vals = (vw0_ref if r < NSA_GROUP else vw1_ref)[0, pl.ds(start, WIN_KEYS), :]
        s = s_ref[r * QB:(r + 1) * QB, :] + tbl_ref[r] + pad_mask
        e = jnp.exp2(s - jnp.max(s, axis=1, keepdims=True))
        outs.append(_dot(e.astype(BF16), vals))
    for g in range(NSA_GROUP):
        out_ref[0, :, g * LANES:(g + 1) * LANES] = (_pair_ratio(outs[g], outs[NSA_GROUP + g])
                                                    * _gate_tile(gates_ref, 2, g)).astype(BF16)


def _window_branch(qb, gates, kw, vw0aug, vw1aug, f_b):
    b, t, _ = qb.shape
    tbl = _win_bias(f_b)
    pad_front = lambda a: jnp.pad(a, ((0, 0), (WIN, 0), (0, 0)))
    kw_pad, vw0_pad, vw1_pad = pad_front(kw), pad_front(vw0aug), pad_front(vw1aug)
    blockq = lambda w: pl.BlockSpec((1, QB, w), lambda bi, i: (bi, i, 0))
    batch = lambda a: pl.BlockSpec((1,) + a.shape[1:], lambda bi, i: (bi, 0, 0))
    rows = N_ROWGROUPS * QB
    return pl.pallas_call(
        _win_kernel,
        grid=(b, t // QB),
        in_specs=[blockq(DB), blockq(LANES), batch(kw_pad), batch(vw0_pad), batch(vw1_pad),pl.BlockSpec(tbl.shape, lambda bi, i: (0, 0, 0))],
        out_specs=blockq(DB),
        out_shape=jax.ShapeDtypeStruct((b, t, DB), BF16),
        scratch_shapes=[pltpu.VMEM((rows, LANES), BF16), pltpu.VMEM((rows, WIN_KEYS), F32)],
        compiler_params=pltpu.CompilerParams(dimension_semantics=("parallel", "parallel"), vmem_limit_bytes=VMEM_LIMIT),
        name="nsa_window",
    )(qb, gates, kw_pad, vw0_pad, vw1_pad, tbl)


OUT_TM = 256
C_GROUP = N_EXPERTS


def _outproj_kernel(x_ref, oa_ref, oc_ref, os_ref, ow_ref,
                    wout_ref, g_ref, wr_ref, br_ref, h_ref, hn_ref, comb_ref):
    ob = oc_ref[...].astype(F32) + os_ref[...].astype(F32) + ow_ref[...].astype(F32)
    y = _dot(oa_ref[...], wout_ref[0:DA, :]) + _dot(ob.astype(BF16), wout_ref[DA:DA + DB, :])
    h = x_ref[...] + y
    h_ref[...] = h
    hn = h * lax.rsqrt(jnp.mean(h * h, axis=-1, keepdims=True) + EPS) * g_ref[...]
    hn_hi = hn.astype(BF16)
    hn_ref[...] = hn_hi
    hn_lo = (hn - hn_hi.astype(F32)).astype(BF16)
    both = _dot(hn_hi, wr_ref[...])
    logits = both[:, 0:LANES] + both[:, LANES:2 * LANES] + _dot(hn_lo, wr_ref[:, 0:LANES]) + br_ref[...]
    lane = lax.broadcasted_iota(jnp.int32, logits.shape, 1)
    lane_f = lane.astype(F32)
    big = float(LANES)
    gl = jnp.where((lane >= C_GROUP) & (lane < C_GROUP + N_GROUPS), logits, -jnp.inf)
    gmax = jnp.max(gl, axis=1, keepdims=True)
    gidx = jnp.min(jnp.where(gl == gmax, lane_f, big), axis=1, keepdims=True) - C_GROUP
    gprob = 1.0 / jnp.sum(jnp.exp(gl - gmax), axis=1, keepdims=True)
    grp_of_lane = (lane // EXPERTS_PER_GROUP).astype(F32)
    el = jnp.where((lane < N_EXPERTS) & (grp_of_lane == gidx), logits, -jnp.inf)
    v1 = jnp.max(el, axis=1, keepdims=True)
    i1 = jnp.min(jnp.where(el == v1, lane_f, big), axis=1, keepdims=True)
    el2 = jnp.where(lane_f == i1, -jnp.inf, el)
    v2 = jnp.max(el2, axis=1, keepdims=True)
    i2 = jnp.min(jnp.where(el2 == v2, lane_f, big), axis=1, keepdims=True)
    e2 = jnp.exp(v2 - v1)
    p1 = 1.0 / (1.0 + e2)
    comb_ref[...] = gprob * (jnp.where(lane_f == i1, p1, 0.0) + jnp.where(lane_f == i2, e2 * p1, 0.0))


def _outproj(x2d, o_a, b_parts, w_out_perm, gain, w_router, b_router):
    n, d = x2d.shape
    row = lambda w: pl.BlockSpec((OUT_TM, w), lambda i: (i, 0))
    full = lambda a: pl.BlockSpec(a.shape, lambda i: (0, 0))
    return pl.pallas_call(
        _outproj_kernel,
        grid=(n // OUT_TM,),
        in_specs=[row(d), row(DA)] + [row(DB)] * 3 + [full(w_out_perm), pl.BlockSpec((1, d), lambda i: (0, 0)),
                                                      full(w_router), full(b_router)],
        out_specs=[row(d), row(d), row(LANES)],
        out_shape=[jax.ShapeDtypeStruct((n, d), F32), jax.ShapeDtypeStruct((n, d), BF16),
                   jax.ShapeDtypeStruct((n, LANES), F32)],
        compiler_params=pltpu.CompilerParams(dimension_semantics=("parallel",), vmem_limit_bytes=VMEM_LIMIT),
        name="outproj_router",
    )(x2d, o_a, *b_parts, w_out_perm, gain.reshape(1, d), w_router, b_router)


MOE_TM = 1024


def _moe_kernel(h_ref, hn_ref, comb_ref, wg_ref, wu_ref, wd_ref, g_ref, out_ref, acc_ref):
    e = pl.program_id(1)

    @pl.when(e == 0)
    def _():
        acc_ref[...] = jnp.zeros_like(acc_ref)

    hn = hn_ref[...]
    gate = _dot(hn, wg_ref[0])
    up = _dot(hn, wu_ref[0])
    lane = lax.broadcasted_iota(jnp.int32, comb_ref.shape, 1)
    weight = jnp.sum(jnp.where(lane == e, comb_ref[...], 0.0), axis=1, keepdims=True)
    hidden = (gate * jax.nn.sigmoid(gate) * up * weight).astype(BF16)
    acc_ref[...] += _dot(hidden, wd_ref[0])

    @pl.when(e == pl.num_programs(1) - 1)
    def _():
        y = h_ref[...] + acc_ref[...]
        out_ref[...] = y * lax.rsqrt(jnp.mean(y * y, axis=-1, keepdims=True) + EPS) * g_ref[...]


def _moe(h, hn, comb, w_gate, w_up, w_down, gain):
    n, d = h.shape
    tm = min(MOE_TM, n)
    row = lambda w: pl.BlockSpec((tm, w), lambda i, e: (i, 0))
    return pl.pallas_call(
        _moe_kernel,
        grid=(n // tm, N_EXPERTS),
        in_specs=[row(d), row(d), row(LANES),
                  pl.BlockSpec((1, d, D_EXPERT), lambda i, e: (e, 0, 0)),
                  pl.BlockSpec((1, d, D_EXPERT), lambda i, e: (e, 0, 0)),
                  pl.BlockSpec((1, D_EXPERT, d), lambda i, e: (e, 0, 0)),
                  pl.BlockSpec((1, d), lambda i, e: (0, 0))],
        out_specs=row(d),
        out_shape=jax.ShapeDtypeStruct((n, d), F32),
        scratch_shapes=[pltpu.VMEM((tm, d), F32)],
        compiler_params=pltpu.CompilerParams(dimension_semantics=("parallel", "arbitrary"), vmem_limit_bytes=VMEM_LIMIT),
        name="moe_experts",
    )(h, hn, comb, w_gate, w_up, w_down, gain.reshape(1, d))


def _permute_w_out(w_out):
    d = w_out.shape[1]
    wb = w_out[DA:].reshape(NSA_KV_HEADS, NSA_GROUP, HEAD_DIM, d).transpose(1, 0, 2, 3).reshape(DB, d)
    return jnp.concatenate([w_out[:DA], wb], axis=0).astype(BF16)


def _router_weights(w_group, b_group, w_expert, b_expert):
    d = w_group.shape[0]
    w = jnp.concatenate([w_expert.reshape(d, N_EXPERTS), w_group], axis=1)
    b = jnp.concatenate([b_expert.reshape(N_EXPERTS), b_group])
    pad = LANES - w.shape[1]
    w = jnp.pad(w, ((0, 0), (0, pad))).astype(F32)
    w_hi = w.astype(BF16)
    w_lo = (w - w_hi.astype(F32)).astype(BF16)
    return jnp.concatenate([w_hi, w_lo], axis=1), jnp.pad(b, (0, pad)).reshape(1, LANES).astype(F32)


def _layer(h, rel_bias, norm_mix, w_in, w_out, cmp_pos_k, cmp_pos_v, cmp_k_w1, cmp_k_w2, cmp_v_w1, cmp_v_w2,
           norm_ffn, w_rg, b_rg, w_re, b_re, w_gate, w_up, w_down, out_gain):
    b, t, d = h.shape
    n = b * t
    assert t % (QB * DIL_PATTERNS[-1][1]) == 0 and t // SEL_BLOCK <= LANES and n % MOE_TM == 0
    x2d = h.reshape(n, d)
    seq = lambda a: a.reshape(b, t, a.shape[-1])
    qa, ka, va, qb, kc, vc, ksaug, vs0aug, vs1aug, kw, vw0aug, vw1aug, gates = map(
        seq, _inproj(x2d, norm_mix, _permute_w_in(w_in), t))
    f_a = _bias_1d(rel_bias[:, :DIL_HEADS])
    f_b = _bias_1d(rel_bias[:, DIL_HEADS:]) * LOG2E
    o_a = _mixer_a(qa, ka, va, jnp.stack([_dil_bias(f_a, dil) for _, dil in DIL_PATTERNS]))
    kcmp, vcmp = _compress(kc, vc, cmp_pos_k, cmp_pos_v, cmp_k_w1, cmp_k_w2, cmp_v_w1, cmp_v_w2)
    o_cmp, sel = _compressed_branch(qb, kcmp, vcmp, gates, f_b)
    o_sel = _selected_branch(qb, sel, gates, ksaug, vs0aug, vs1aug, f_b)
    o_win = _window_branch(qb, gates, kw, vw0aug, vw1aug, f_b)
    b_parts = [o.reshape(n, DB) for o in (o_cmp, o_sel, o_win)]
    w_router, b_router = _router_weights(w_rg, b_rg, w_re, b_re)
    h2, hn, comb = _outproj(x2d, o_a.reshape(n, DA), b_parts, _permute_w_out(w_out), norm_ffn, w_router, b_router)
    return _moe(h2, hn, comb, w_gate.astype(BF16), w_up.astype(BF16), w_down.astype(BF16), out_gain)


def kernel(x, rel_bias, norm_mix, w_in, w_out, cmp_pos_k, cmp_pos_v, cmp_k_w1, cmp_k_w2, cmp_v_w1, cmp_v_w2,
           norm_ffn, w_router_group, b_router_group, w_router_expert, b_router_expert, w_gate, w_up, w_down,
           norm_final):
    depth = norm_mix.shape[0]
    assert depth == 1, "the final RMSNorm is fused into the last layer's expert kernel"
    out = _layer(x, rel_bias, norm_mix[0], w_in[0], w_out[0], cmp_pos_k[0], cmp_pos_v[0], cmp_k_w1[0], cmp_k_w2[0],
                 cmp_v_w1[0], cmp_v_w2[0], norm_ffn[0], w_router_group[0], b_router_group[0], w_router_expert[0],
                 b_router_expert[0], w_gate[0], w_up[0], w_down[0], norm_final)
    return out.reshape(x.shape)
```

```python
import functools
import math

import jax
import jax.numpy as jnp
import numpy as np
from jax import lax
from jax.experimental import pallas as pl
from jax.experimental.pallas import tpu as pltpu

HEAD_DIM = 64
DIL_HEADS = 6
NSA_KV_HEADS = 2
NSA_GROUP = 5
NSA_HEADS = NSA_KV_HEADS * NSA_GROUP
N_HEADS = DIL_HEADS + NSA_HEADS
DIL_PATTERNS = ((128, 1), (512, 4), (2048, 16))
CMP_BLOCK = 32
CMP_STRIDE = 16
CMP_HIDDEN = 256
SEL_BLOCK = 64
SEL_TOPK = 16
WIN = 512
FORCE_SCORE = 1.0e4
N_BUCKETS = 32
MAX_DISTANCE = 2048
N_GROUPS = 4
EXPERTS_PER_GROUP = 4
N_EXPERTS = N_GROUPS * EXPERTS_PER_GROUP
D_EXPERT = 512
EPS = 1e-6

LANES = 128
QB = 128
NEG = -1.0e30
LOG2E = math.log2(math.e)
DA = DIL_HEADS * HEAD_DIM
DB = NSA_HEADS * HEAD_DIM
N_ROWGROUPS = NSA_HEADS
SEL_KT = 256
VMEM_LIMIT = 56 * 1024 * 1024

F32 = jnp.float32
BF16 = jnp.bfloat16
NT_DIMS = (((1,), (1,)), ((), ()))


def _nt_dot(a, b):
    return lax.dot_general(a, b, NT_DIMS, preferred_element_type=F32)


def _dot(a, b):
    return jnp.dot(a, b, preferred_element_type=F32)


def _bucket_np(dist):
    dist = np.maximum(np.asarray(dist, np.int64), 0)
    max_exact = N_BUCKETS // 2
    x = np.maximum(dist, 1).astype(np.float32) / np.float32(max_exact)
    large = max_exact + (np.log(x) / np.float32(math.log(MAX_DISTANCE / max_exact))
                         * np.float32(N_BUCKETS - max_exact)).astype(np.int32)
    large = np.minimum(large, N_BUCKETS - 1)
    return np.where(dist < max_exact, dist, large).astype(np.int32)


BIAS_LEN = 4096


def _bias_1d(rel_bias_heads):
    onehot = (_bucket_np(np.arange(BIAS_LEN))[None, :] == np.arange(N_BUCKETS)[:, None]).astype(np.float32)
    return jnp.dot(rel_bias_heads.T.astype(F32), jnp.asarray(onehot), precision=lax.Precision.HIGHEST)


def _extend(f, lo, hi):
    assert hi <= f.shape[-1]
    if lo >= 0:
        return f[..., lo:hi]
    pad = jnp.full(f.shape[:-1] + (-lo,), NEG, f.dtype)
    return jnp.concatenate([pad, f[..., :hi]], axis=-1)


def _toeplitz(w, q, c):
    n = q + c - 1
    assert w.shape[-1] == n
    lead = w.shape[:-1]
    wp = jnp.concatenate([w, jnp.zeros(lead + (1,), w.dtype)], axis=-1)
    flat = jnp.broadcast_to(wp[..., None, :], lead + (q, n + 1)).reshape(lead + (q * (n + 1),))
    return flat[..., :q * n].reshape(lead + (q, n))[..., q - 1:q - 1 + c]


def _toeplitz_of(fn_vals, lo, q, c):
    return _toeplitz(jnp.flip(fn_vals, axis=-1), q, c)


IN_TM = 512
C_QA, C_KA, C_VA = 0, DA, 2 * DA
C_QB = 3 * DA
C_KC = C_QB + DB
C_VC, C_KS, C_VS, C_KW, C_VW, C_GT = (C_KC + LANES * i for i in range(1, 7))
N_COLS = C_GT + LANES


def _permute_w_in(w_in):
    scale = 1.0 / math.sqrt(HEAD_DIM)
    sizes = [DA] * 3 + [DB] + [NSA_KV_HEADS * HEAD_DIM] * 6 + [3 * NSA_HEADS]
    offs = np.concatenate([[0], np.cumsum(sizes)])
    part = lambda i: w_in[:, offs[i]:offs[i + 1]]
    d = w_in.shape[0]
    qb = part(3).reshape(d, NSA_KV_HEADS, NSA_GROUP, HEAD_DIM).transpose(0, 2, 1, 3).reshape(d, DB)
    gt = part(10).reshape(d, NSA_KV_HEADS, NSA_GROUP, 3).transpose(0, 3, 2, 1).reshape(d, 3 * NSA_HEADS)
    gt = jnp.pad(gt, ((0, 0), (0, LANES - 3 * NSA_HEADS)))
    cols = [part(0) * scale, part(1), part(2), qb * (scale * LOG2E)] + [part(i) for i in range(4, 10)] + [gt]
    return jnp.concatenate(cols, axis=1).astype(BF16)


def _inproj_kernel(seq_len, x_ref, g_ref, w_ref, qa_ref, ka_ref, va_ref, qb_ref, kc_ref, vc_ref,
                   ksaug_ref, vs0_ref, vs1_ref, kw_ref, vw0_ref, vw1_ref, gates_ref):
    x = x_ref[...]
    xn = (x * lax.rsqrt(jnp.mean(x * x, axis=-1, keepdims=True) + EPS) * g_ref[...]).astype(BF16)
    seg = lambda a, n: _dot(xn, w_ref[:, a:a + n])
    qa_ref[...] = seg(C_QA, DA).astype(BF16)
    ka_ref[...] = seg(C_KA, DA).astype(BF16)
    va_ref[...] = seg(C_VA, DA).astype(BF16)
    qb_ref[...] = seg(C_QB, DB).astype(BF16)
    kc_ref[...] = seg(C_KC, LANES).astype(BF16)
    vc_ref[...] = seg(C_VC, LANES).astype(BF16)
    kw_ref[...] = seg(C_KW, LANES).astype(BF16)
    tm = x.shape[0]
    tok = (pl.program_id(0) * tm) % seq_len + lax.broadcasted_iota(jnp.int32, (tm, LANES), 0)
    lane = lax.broadcasted_iota(jnp.int32, (tm, LANES), 1)
    ksaug_ref[:, 0:LANES] = seg(C_KS, LANES).astype(BF16)
    ksaug_ref[:, LANES:2 * LANES] = jnp.where(lane == tok // SEL_BLOCK, 1.0, 0.0).astype(BF16)
    low = lane < HEAD_DIM
    for col, ref0, ref1 in ((C_VS, vs0_ref, vs1_ref), (C_VW, vw0_ref, vw1_ref)):
        v = seg(col, LANES)
        ref0[...] = jnp.where(low, v, 1.0).astype(BF16)
        ref1[...] = jnp.where(low, 1.0, v).astype(BF16)
    gates_ref[...] = jax.nn.sigmoid(seg(C_GT, LANES))


def _inproj(x2d, gain, w_perm, seq_len):
    n, d = x2d.shape
    row = lambda w: pl.BlockSpec((IN_TM, w), lambda i: (i, 0))
    widths = [DA, DA, DA, DB, LANES, LANES, 2 * LANES, LANES, LANES, LANES, LANES, LANES]
    out_shape = [jax.ShapeDtypeStruct((n, w), BF16) for w in widths] + [jax.ShapeDtypeStruct((n, LANES), F32)]
    return pl.pallas_call(
        functools.partial(_inproj_kernel, seq_len),
        grid=(n // IN_TM,),
        in_specs=[row(d), pl.BlockSpec((1, d), lambda i: (0, 0)), pl.BlockSpec((d, N_COLS), lambda i: (0, 0))],
        out_specs=[row(w) for w in widths] + [row(LANES)],
        out_shape=out_shape,
        compiler_params=pltpu.CompilerParams(dimension_semantics=("parallel",), vmem_limit_bytes=VMEM_LIMIT),
        name="inproj",
    )(x2d, gain.reshape(1, d), w_perm)


def _embed_pair(w, n_tok):
    c = w.shape[1]
    w4 = w.reshape(n_tok, 1, HEAD_DIM, 1, c) * jnp.eye(NSA_KV_HEADS, dtype=w.dtype).reshape(1, 2, 1, 2, 1)
    return w4.reshape(n_tok * 2 * HEAD_DIM, 2 * c)


def _gelu_tanh(x):
    return 0.5 * x * (1.0 + jnp.tanh(math.sqrt(2.0 / math.pi) * (x + 0.044715 * (x * x * x))))


def _compress_kernel(ck_ref, cv_ref, posk_ref, posv_ref, wk1a, wk1b, wk2, wv1a, wv1b, wv2,
                     kout_ref, vout_ref, shift_ref):
    ncp = ck_ref.shape[1]
    for c_ref, pos_ref, w1a, w1b, w2, out_ref in ((ck_ref, posk_ref, wk1a, wk1b, wk2, kout_ref),
                                                  (cv_ref, posv_ref, wv1a, wv1b, wv2, vout_ref)):
        c = c_ref[0].astype(F32)
        first = _dot((c + pos_ref[0:1, :]).astype(BF16), w1a[...])
        second = _dot((c + pos_ref[1:2, :]).astype(BF16), w1b[...])
        shift_ref[0:ncp, :] = second
        shift_ref[ncp:ncp + 8, :] = jnp.zeros((8, second.shape[1]), F32)
        hidden = _gelu_tanh(first + shift_ref[1:ncp + 1, :])
        out_ref[0] = _dot(hidden.astype(BF16), w2[...]).astype(BF16)


def _compress(kc, vc, pos_k, pos_v, k_w1, k_w2, v_w1, v_w2):
    b, t, _ = kc.shape
    ncp = t // CMP_STRIDE
    half = CMP_STRIDE * HEAD_DIM
    wide = CMP_STRIDE * LANES

    def prep(w1, w2, pos):
        pos_pair = jnp.broadcast_to(pos.reshape(2, CMP_STRIDE, 1, HEAD_DIM), (2, CMP_STRIDE, 2, HEAD_DIM))
        return (_embed_pair(w1[:half], CMP_STRIDE).astype(BF16), _embed_pair(w1[half:], CMP_STRIDE).astype(BF16),
                jnp.kron(jnp.eye(NSA_KV_HEADS, dtype=w2.dtype), w2).astype(BF16),
                pos_pair.reshape(2, wide).astype(F32))

    wk1a, wk1b, wk2, posk = prep(k_w1, k_w2, pos_k)
    wv1a, wv1b, wv2, posv = prep(v_w1, v_w2, pos_v)
    full = lambda a: pl.BlockSpec(a.shape, lambda i: (0,) * a.ndim)
    tok = pl.BlockSpec((1, ncp, wide), lambda i: (i, 0, 0))
    out = pl.BlockSpec((1, ncp, LANES), lambda i: (i, 0, 0))
    return pl.pallas_call(
        _compress_kernel,
        grid=(b,),
        in_specs=[tok, tok, full(posk), full(posv), full(wk1a), full(wk1b), full(wk2), full(wv1a), full(wv1b), full(wv2)],
        out_specs=[out, out],
        out_shape=[jax.ShapeDtypeStruct((b, ncp, LANES), BF16)] * 2,
        scratch_shapes=[pltpu.VMEM((ncp + 8, 2 * CMP_HIDDEN), F32)],
        compiler_params=pltpu.CompilerParams(dimension_semantics=("parallel",), vmem_limit_bytes=VMEM_LIMIT),
        name="compress",
    )(kc.reshape(b, ncp, wide), vc.reshape(b, ncp, wide), posk, posv, wk1a, wk1b, wk2, wv1a, wv1b, wv2)


def _pair_masks(rows):
    lane = lax.broadcasted_iota(jnp.int32, (rows, LANES), 1)
    return lane < HEAD_DIM


A_SUPER = QB * DIL_PATTERNS[-1][1]


def _mixer_a_kernel(q_ref, kp_ref, kc_ref, vp_ref, vc_ref, bias_ref, out_ref, qf_ref, kf_ref, vf_ref, o_ref, lse_ref):
    first = pl.program_id(1) == 0
    n_pairs = DIL_HEADS // 2
    for p in range(n_pairs):
        cs = slice(p * LANES, (p + 1) * LANES)
        qf_ref[p] = q_ref[0, :, cs].astype(F32)
        kf_ref[p, 0:A_SUPER, :] = kp_ref[0, :, cs].astype(F32)
        kf_ref[p, A_SUPER:2 * A_SUPER, :] = kc_ref[0, :, cs].astype(F32)
        vf_ref[p, 0:A_SUPER, :] = vp_ref[0, :, cs].astype(F32)
        vf_ref[p, A_SUPER:2 * A_SUPER, :] = vc_ref[0, :, cs].astype(F32)
    low = _pair_masks(QB)
    in_prev = lax.broadcasted_iota(jnp.int32, (QB, 2 * QB), 1) < QB
    zero = jnp.zeros((QB, LANES), BF16)

    def chunk(idx, dil, q_base, k_base, q_span, k_span, off, at_start):
        rows = lambda size: pl.ds(off, size) if dil == 1 else pl.ds(off, size, stride=dil)
        q_win = lambda ref: ref.at[pl.ds(pl.multiple_of(q_base, 8), q_span), :]
        k_win = lambda ref: ref.at[pl.ds(pl.multiple_of(k_base, 8), k_span), :]
        q_rows = rows(QB)
        prev_mask = jnp.where(jnp.logical_and(in_prev, jnp.logical_and(first, at_start)), NEG, 0.0)
        for p in range(n_pairs):
            q = q_win(qf_ref.at[p])[q_rows, :].astype(BF16)
            keys = k_win(kf_ref.at[p])[rows(2 * QB), :].astype(BF16)
            vals = k_win(vf_ref.at[p])[rows(2 * QB), :].astype(BF16)
            o_win, lse_win = q_win(o_ref.at[p]), q_win(lse_ref.at[p])
            lhs = jnp.concatenate([jnp.where(low, q, zero), jnp.where(low, zero, q)], axis=0)
            bias = jnp.concatenate([bias_ref[idx, 2 * p] + prev_mask, bias_ref[idx, 2 * p + 1] + prev_mask], axis=0)
            s = _nt_dot(lhs, keys) + bias
            m = jnp.max(s, axis=1, keepdims=True)
            e = jnp.exp(s - m)
            l = jnp.sum(e, axis=1, keepdims=True)
            pv = _dot(e.astype(BF16), vals) * (1.0 / l)
            lse = m + jnp.log(l)
            o_new = jnp.where(low, pv[:QB], pv[QB:])
            l_new = jnp.where(low, jnp.broadcast_to(lse[:QB], (QB, LANES)), jnp.broadcast_to(lse[QB:], (QB, LANES)))
            if idx > 0:
                o_old, l_old = o_win[q_rows, :], lse_win[q_rows, :]
                mx = jnp.maximum(l_old, l_new)
                w_old, w_new = jnp.exp(l_old - mx), jnp.exp(l_new - mx)
                tot = w_old + w_new
                o_new = (w_old * o_old + w_new * o_new) * (1.0 / tot)
                l_new = mx + jnp.log(tot)
            o_win[q_rows, :] = o_new
            lse_win[q_rows, :] = l_new

    def loop(n, body):
        lax.fori_loop(0, n, lambda i, carry: (body(i), carry)[1], 0)

    for idx, (_, dil) in enumerate(DIL_PATTERNS):
        span = QB * dil
        n_chunks = A_SUPER // span
        if n_chunks > 1:
            for r in range(dil):
                loop(n_chunks, lambda c, idx=idx, dil=dil, span=span, r=r:
                     chunk(idx, dil, span * c, A_SUPER + span * (c - 1), span, 2 * span, r, c == 0))
        else:
            for r in range(8):
                loop(dil // 8, lambda hi, idx=idx, dil=dil, span=span, r=r:
                     chunk(idx, dil, 8 * hi, 8 * hi, span - 8, 2 * span - 8, r, True))
    for p in range(n_pairs):
        out_ref[0, :, p * LANES:(p + 1) * LANES] = o_ref[p].astype(BF16)


def _mixer_a(qa, ka, va, bias):
    b, t, _ = qa.shape
    cur = pl.BlockSpec((1, A_SUPER, DA), lambda bi, i: (bi, i, 0))
    prev = pl.BlockSpec((1, A_SUPER, DA), lambda bi, i: (bi, jnp.maximum(i - 1, 0), 0))
    return pl.pallas_call(
        _mixer_a_kernel,
        grid=(b, t // A_SUPER),
        in_specs=[cur, prev, cur, prev, cur, pl.BlockSpec(bias.shape, lambda bi, i: (0, 0, 0, 0))],
        out_specs=cur,
        out_shape=jax.ShapeDtypeStruct((b, t, DA), BF16),
        scratch_shapes=[pltpu.VMEM((DIL_HEADS // 2, rows, LANES), F32) for rows in (A_SUPER, 2 * A_SUPER, 2 * A_SUPER, A_SUPER, A_SUPER)],
        compiler_params=pltpu.CompilerParams(dimension_semantics=("parallel", "parallel"), vmem_limit_bytes=VMEM_LIMIT),
        name="mixer_a",
    )(qa, ka, ka, va, va, bias)


def _dil_bias(f_a, dil):
    steps = DIL_PATTERNS[0][0]
    g = f_a[:, 0:dil * steps + 1:dil]
    lo, hi = QB - (2 * QB - 1), QB + QB
    vals = jnp.concatenate([jnp.full((DIL_HEADS, -lo), NEG, F32), g, jnp.full((DIL_HEADS, hi - steps - 1), NEG, F32)], axis=1)
    return _toeplitz_of(vals, lo, QB, 2 * QB)


CMP_TILE_KEYS = LANES
CMP_TILE_SPAN = CMP_TILE_KEYS * CMP_STRIDE // QB
CMP_CONST_DELTA = 28


def _cmp_bias(f_b):
    per = QB // CMP_STRIDE
    n_rows = per * (CMP_CONST_DELTA + 1)
    m_lo, m_hi = -(CMP_TILE_KEYS - 1), n_rows
    f_b = f_b.astype(BF16)
    base = _extend(f_b, CMP_STRIDE * m_lo - (CMP_BLOCK - 1), CMP_STRIDE * m_hi - (CMP_BLOCK - 1))
    g = base.reshape(NSA_HEADS, m_hi - m_lo, CMP_STRIDE).transpose(0, 2, 1)
    t = _toeplitz_of(g, m_lo, n_rows, CMP_TILE_KEYS)
    t = t.reshape(NSA_HEADS, CMP_STRIDE, CMP_CONST_DELTA + 1, per, CMP_TILE_KEYS).transpose(2, 0, 3, 1, 4)
    t = t.reshape(CMP_CONST_DELTA + 1, NSA_HEADS, QB, CMP_TILE_KEYS)
    return jnp.concatenate([jnp.full((1,) + t.shape[1:], NEG, t.dtype), t], axis=0)


def _overlap_matrix_t(ncp, n_sel_pad):
    n = np.arange(ncp)[None, :] * CMP_STRIDE
    s = np.arange(n_sel_pad)[:, None] * SEL_BLOCK
    ov = np.clip(np.minimum(n + CMP_BLOCK, s + SEL_BLOCK) - np.maximum(n, s), 0, None) / CMP_BLOCK
    return jnp.asarray(ov, BF16)


def _gate_tile(gates_ref, branch, g):
    c = branch * NSA_HEADS + g * 2
    low = _pair_masks(QB)
    return jnp.where(low, jnp.broadcast_to(gates_ref[0, :, c:c + 1], (QB, LANES)),
                     jnp.broadcast_to(gates_ref[0, :, c + 1:c + 2], (QB, LANES)))


def _masked_q(qb_ref, g, kv):
    q = qb_ref[0, :, g * LANES:(g + 1) * LANES]
    low = _pair_masks(QB)
    keep = low if kv == 0 else jnp.logical_not(low)
    return jnp.where(keep, q, jnp.zeros_like(q))


def _cmp_kernel(n_tiles, qb_ref, kcmp_ref, vcmp_ref, gates_ref, ov_ref, *rest):
    tbl_refs, (oc_ref, sel_ref, q_ref, s_ref, p_ref, pv_ref, imp_ref) = rest[:n_tiles], rest[n_tiles:]
    qblk = pl.program_id(1)
    t0 = qblk * QB
    low = _pair_masks(QB)
    for kv in range(NSA_KV_HEADS):
        for g in range(NSA_GROUP):
            r = kv * NSA_GROUP + g
            q_ref[r * QB:(r + 1) * QB, :] = _masked_q(qb_ref, g, kv)

    def attend(n_vis):
        kc = n_vis * CMP_TILE_KEYS
        s_ref[:, 0:kc] = _nt_dot(q_ref[...], kcmp_ref[0, 0:kc, :])
        for kv in range(NSA_KV_HEADS):
            psum = jnp.zeros((QB, kc), F32)
            for g in range(NSA_GROUP):
                r = kv * NSA_GROUP + g
                rows = slice(r * QB, (r + 1) * QB)
                s = s_ref[rows, 0:kc] + jnp.concatenate([tbl_refs[c][0, r].astype(F32) for c in range(n_vis)], axis=1)
                m = jnp.max(s, axis=1, keepdims=True)
                e = jnp.exp2(s - m)
                den = jnp.sum(e, axis=1, keepdims=True)
                p = jnp.where(m > 0.5 * NEG, e * (1.0 / den), 0.0)
                psum = psum + p
                p_ref[rows, 0:kc] = p.astype(BF16)
            hi = psum.astype(BF16)
            lo = (psum - hi.astype(F32)).astype(BF16)
            ov_t = ov_ref[:, 0:kc]
            imp_ref[kv] = _nt_dot(ov_t, hi) + _nt_dot(ov_t, lo)
        pv_ref[...] = _dot(p_ref[:, 0:kc], vcmp_ref[0, 0:kc, :])

    n_vis = qblk // CMP_TILE_SPAN + 1
    for w in range(1, n_tiles + 1):
        pl.when(n_vis == w)(functools.partial(attend, w))

    blk = lax.broadcasted_iota(jnp.int32, (LANES, QB), 0)
    cur = (t0 + lax.broadcasted_iota(jnp.int32, (LANES, QB), 1)) // SEL_BLOCK
    blk_f = blk.astype(F32)
    forced = (blk == cur) | (blk == cur - 1) | (blk == 0)
    causal = blk <= cur
    scores = [jnp.where(forced, FORCE_SCORE, jnp.where(causal, imp_ref[kv], -1.0)) for kv in range(NSA_KV_HEADS)]

    def pick(_, carry):
        new = []
        for val, sel in carry:
            mx = jnp.max(val, axis=0, keepdims=True)
            idx = jnp.min(jnp.where(val == mx, blk_f, float(LANES)), axis=0, keepdims=True)
            hit = blk_f == idx
            new.append((jnp.where(hit, -jnp.inf, val), jnp.where(hit, 1.0, sel)))
        return tuple(new)

    picked = lax.fori_loop(0, SEL_TOPK, pick, tuple((v, jnp.zeros((LANES, QB), F32)) for v in scores))
    eye = jnp.where(lax.broadcasted_iota(jnp.int32, (QB, QB), 0) == lax.broadcasted_iota(jnp.int32, (QB, QB), 1),
                    1.0, 0.0).astype(BF16)
    for kv in range(NSA_KV_HEADS):
        sel_t = jnp.where(causal, picked[kv][1], 0.0).astype(BF16)
        sel_ref[0, kv] = _nt_dot(eye, sel_t).astype(BF16)
    for g in range(NSA_GROUP):
        o0, o1 = pv_ref[g * QB:(g + 1) * QB, :], pv_ref[(NSA_GROUP + g) * QB:(NSA_GROUP + g + 1) * QB, :]
        oc_ref[0, :, g * LANES:(g + 1) * LANES] = (jnp.where(low, o0, o1) * _gate_tile(gates_ref, 0, g)).astype(BF16)


def _compressed_branch(qb, kcmp, vcmp, gates, f_b):
    b, t, _ = qb.shape
    ncp = kcmp.shape[1]
    n_tiles = ncp // CMP_TILE_KEYS
    tbl = _cmp_bias(f_b)
    ov = _overlap_matrix_t(ncp, LANES)

    def tbl_spec(c):
        return pl.BlockSpec((1, NSA_HEADS, QB, CMP_TILE_KEYS),
                            lambda bi, i: (jnp.clip(i - CMP_TILE_SPAN * c, -1, CMP_CONST_DELTA) + 1, 0, 0, 0))

    blockq = lambda w: pl.BlockSpec((1, QB, w), lambda bi, i: (bi, i, 0))
    batch = lambda a: pl.BlockSpec((1,) + a.shape[1:], lambda bi, i: (bi, 0, 0))
    return pl.pallas_call(
        functools.partial(_cmp_kernel, n_tiles),
        grid=(b, t // QB),
        in_specs=[blockq(DB), batch(kcmp), batch(vcmp), blockq(LANES), pl.BlockSpec(ov.shape, lambda bi, i: (0, 0))]
                 + [tbl_spec(c) for c in range(n_tiles)],
        out_specs=[blockq(DB), pl.BlockSpec((1, NSA_KV_HEADS, QB, LANES), lambda bi, i: (bi, 0, i, 0))],
        out_shape=[jax.ShapeDtypeStruct((b, t, DB), BF16), jax.ShapeDtypeStruct((b, NSA_KV_HEADS, t, LANES), BF16)],
        scratch_shapes=[pltpu.VMEM((N_ROWGROUPS * QB, LANES), BF16), pltpu.VMEM((N_ROWGROUPS * QB, ncp), F32),
                        pltpu.VMEM((N_ROWGROUPS * QB, ncp), BF16), pltpu.VMEM((N_ROWGROUPS * QB, LANES), F32),
                        pltpu.VMEM((NSA_KV_HEADS, LANES, QB), F32)],
        compiler_params=pltpu.CompilerParams(dimension_semantics=("parallel", "parallel"), vmem_limit_bytes=VMEM_LIMIT),
        name="nsa_compressed",
    )(qb, kcmp, vcmp, gates, ov, *([tbl] * n_tiles))


SEL_NEAR = 13


def _sel_bias(f_b):
    n_off = SEL_NEAR + 1
    cols = QB * n_off
    rel = f_b - f_b[:, BIAS_LEN - 1:]
    lo = -(QB - 1)
    big = _toeplitz_of(_extend(rel, lo, lo + QB + cols - 1), lo, QB, cols)
    tiles = jnp.flip(big.reshape(NSA_HEADS, QB, n_off, QB).transpose(2, 0, 1, 3), axis=0)
    return jnp.concatenate([jnp.zeros((1,) + tiles.shape[1:], F32), tiles], axis=0)


def _pair_ratio(acc0, acc1):
    low = _pair_masks(acc0.shape[0])
    den = pltpu.roll(jnp.where(low, acc1, acc0), HEAD_DIM, axis=1)
    return jnp.where(low, acc0, acc1) * (1.0 / den)


def _sel_kernel(qb_ref, sel_ref, gates_ref, cfar_ref, ks_ref, vs0_ref, vs1_ref, tbl_ref, out_ref,
                qaug_ref, s_ref, s1_ref, acc_ref, m_ref, alpha_ref):
    qblk = pl.program_id(1)
    for kv in range(NSA_KV_HEADS):
        unchosen = jnp.where(sel_ref[0, kv].astype(F32) > 0.0, 0.0, NEG)
        for g in range(NSA_GROUP):
            r = kv * NSA_GROUP + g
            rows = slice(r * QB, (r + 1) * QB)
            qaug_ref[rows, 0:LANES] = _masked_q(qb_ref, g, kv)
            qaug_ref[rows, LANES:2 * LANES] = (unchosen + cfar_ref[r:r + 1, :]).astype(BF16)
    acc_ref[...] = jnp.zeros_like(acc_ref)
    m_ref[...] = jnp.full_like(m_ref, NEG)

    last_tile = ks_ref.shape[2] // SEL_KT - 1

    def scores(j, dst_ref):
        start = pl.multiple_of(jnp.minimum(j, last_tile) * SEL_KT, SEL_KT)
        dst_ref[...] = _dot(qaug_ref[...], ks_ref[0, :, pl.ds(start, SEL_KT)])

    def consume(j, src_ref, near):
        start = pl.multiple_of(j * SEL_KT, SEL_KT)
        for r in range(N_ROWGROUPS):
            rows = slice(r * QB, (r + 1) * QB)
            s = src_ref[rows, :]
            if near:
                e1 = jnp.clip(qblk - 2 * j + 1, 0, SEL_NEAR + 1)
                e2 = jnp.clip(qblk - 2 * j, 0, SEL_NEAR + 1)
                s = s + jnp.concatenate([tbl_ref[e1, r], tbl_ref[e2, r]], axis=1)
                src_ref[rows, :] = s
            m_old = m_ref[rows, :]
            m_new = jnp.maximum(m_old, jnp.max(s, axis=1, keepdims=True))
            alpha_ref[rows, :] = jnp.exp2(m_old - m_new)
            m_ref[rows, :] = m_new
        for r in range(N_ROWGROUPS):
            rows = slice(r * QB, (r + 1) * QB)
            vals = (vs0_ref if r < NSA_GROUP else vs1_ref)[0, pl.ds(start, SEL_KT), :]
            m_new = m_ref[rows, :]
            p = jnp.exp2(src_ref[rows, :] - jnp.concatenate([m_new, m_new], axis=1))
            acc_ref[rows, :] = alpha_ref[rows, :] * acc_ref[rows, :] + _dot(p.astype(BF16), vals)

    def tile_pair(i, near):
        scores(2 * i + 1, s1_ref)
        consume(2 * i, s_ref, near)
        scores(2 * i + 2, s_ref)
        consume(2 * i + 1, s1_ref, near)

    n_pairs = ((qblk + 2) // 2 + 1) // 2
    n_far = jnp.maximum((qblk - (SEL_NEAR - 1)) // 2, 0) // 2
    scores(0, s_ref)
    lax.fori_loop(0, n_far, lambda i, c: (tile_pair(i, False), c)[1], 0)
    lax.fori_loop(n_far, n_pairs, lambda i, c: (tile_pair(i, True), c)[1], 0)
    for g in range(NSA_GROUP):
        ratio = _pair_ratio(acc_ref[g * QB:(g + 1) * QB, :], acc_ref[(NSA_GROUP + g) * QB:(NSA_GROUP + g + 1) * QB, :])
        out_ref[0, :, g * LANES:(g + 1) * LANES] = (ratio * _gate_tile(gates_ref, 1, g)).astype(BF16)


def _selected_branch(qb, sel, gates, ksaug, vs0aug, vs1aug, f_b):
    b, t, _ = qb.shape
    tbl = _sel_bias(f_b)
    ks_t = jnp.swapaxes(ksaug, 1, 2)
    cfar = jnp.broadcast_to(f_b[:, BIAS_LEN - 1:], (NSA_HEADS, LANES))
    cfar = jnp.pad(cfar, ((0, 16 - NSA_HEADS), (0, 0)))
    blockq = lambda w: pl.BlockSpec((1, QB, w), lambda bi, i: (bi, i, 0))
    batch = lambda a: pl.BlockSpec((1,) + a.shape[1:], lambda bi, i: (bi, 0, 0))
    rows = N_ROWGROUPS * QB
    return pl.pallas_call(
        _sel_kernel,
        grid=(b, t // QB),
        in_specs=[blockq(DB), pl.BlockSpec((1, NSA_KV_HEADS, QB, LANES), lambda bi, i: (bi, 0, i, 0)), blockq(LANES),
                  pl.BlockSpec(cfar.shape, lambda bi, i: (0, 0)), batch(ks_t), batch(vs0aug), batch(vs1aug),
                  pl.BlockSpec(tbl.shape, lambda bi, i: (0, 0, 0, 0))],
        out_specs=blockq(DB),
        out_shape=jax.ShapeDtypeStruct((b, t, DB), BF16),
        scratch_shapes=[pltpu.VMEM((rows, 2 * LANES), BF16), pltpu.VMEM((rows, SEL_KT), F32), pltpu.VMEM((rows, SEL_KT), F32),
                        pltpu.VMEM((rows, LANES), F32), pltpu.VMEM((rows, LANES), F32), pltpu.VMEM((rows, LANES), F32)],
        compiler_params=pltpu.CompilerParams(dimension_semantics=("parallel", "parallel"), vmem_limit_bytes=VMEM_LIMIT),
        name="nsa_selected",
    )(qb, sel, gates, cfar, ks_t, vs0aug, vs1aug, tbl)


WIN_KEYS = WIN + QB


def _win_bias(f_b):
    lo = WIN - (WIN_KEYS - 1)
    vals = _extend(f_b[:, :WIN], lo, WIN)
    vals = jnp.concatenate([vals, jnp.full((NSA_HEADS, lo + QB + WIN_KEYS - 1 - WIN), NEG, F32)], axis=1)
    return _toeplitz_of(vals, lo, QB, WIN_KEYS)


def _win_kernel(qb_ref, gates_ref, kw_ref, vw0_ref, vw1_ref, tbl_ref, out_ref, q_ref, s_ref):
    qblk = pl.program_id(1)
    for kv in range(NSA_KV_HEADS):
        for g in range(NSA_GROUP):
            r = kv * NSA_GROUP + g
            q_ref[r * QB:(r + 1) * QB, :] = _masked_q(qb_ref, g, kv)
    start = pl.multiple_of(qblk * QB, QB)
    s_ref[...] = _nt_dot(q_ref[...], kw_ref[0, pl.ds(start, WIN_KEYS), :])
    col = lax.broadcasted_iota(jnp.int32, (1, WIN_KEYS), 1)
    pad_mask = jnp.where(col + qblk * QB >= WIN, 0.0, NEG)
    outs = []
    for r in range(N_ROWGROUPS):
        vals = (vw0_ref if r < NSA_GROUP else vw1_ref)[0, pl.ds(start, WIN_KEYS), :]
        s = s_ref[r * QB:(r + 1) * QB, :] + tbl_ref[r] + pad_mask
        e = jnp.exp2(s - jnp.max(s, axis=1, keepdims=True))
        outs.append(_dot(e.astype(BF16), vals))
    for g in range(NSA_GROUP):
        out_ref[0, :, g * LANES:(g + 1) * LANES] = (_pair_ratio(outs[g], outs[NSA_GROUP + g])
                                                    * _gate_tile(gates_ref, 2, g)).astype(BF16)


def _window_branch(qb, gates, kw, vw0aug, vw1aug, f_b):
    b, t, _ = qb.shape
    tbl = _win_bias(f_b)
    pad_front = lambda a: jnp.pad(a, ((0, 0), (WIN, 0), (0, 0)))
    kw_pad, vw0_pad, vw1_pad = pad_front(kw), pad_front(vw0aug), pad_front(vw1aug)
    blockq = lambda w: pl.BlockSpec((1, QB, w), lambda bi, i: (bi, i, 0))
    batch = lambda a: pl.BlockSpec((1,) + a.shape[1:], lambda bi, i: (bi, 0, 0))
    rows = N_ROWGROUPS * QB
    return pl.pallas_call(
        _win_kernel,
        grid=(b, t // QB),
        in_specs=[blockq(DB), blockq(LANES), batch(kw_pad), batch(vw0_pad), batch(vw1_pad),pl.BlockSpec(tbl.shape, lambda bi, i: (0, 0, 0))],
        out_specs=blockq(DB),
        out_shape=jax.ShapeDtypeStruct((b, t, DB), BF16),
        scratch_shapes=[pltpu.VMEM((rows, LANES), BF16), pltpu.VMEM((rows, WIN_KEYS), F32)],
        compiler_params=pltpu.CompilerParams(dimension_semantics=("parallel", "parallel"), vmem_limit_bytes=VMEM_LIMIT),
        name="nsa_window",
    )(qb, gates, kw_pad, vw0_pad, vw1_pad, tbl)


OUT_TM = 256
C_GROUP = N_EXPERTS


def _outproj_kernel(x_ref, oa_ref, oc_ref, os_ref, ow_ref,
                    wout_ref, g_ref, wr_ref, br_ref, h_ref, hn_ref, comb_ref):
    ob = oc_ref[...].astype(F32) + os_ref[...].astype(F32) + ow_ref[...].astype(F32)
    y = _dot(oa_ref[...], wout_ref[0:DA, :]) + _dot(ob.astype(BF16), wout_ref[DA:DA + DB, :])
    h = x_ref[...] + y
    h_ref[...] = h
    hn = h * lax.rsqrt(jnp.mean(h * h, axis=-1, keepdims=True) + EPS) * g_ref[...]
    hn_hi = hn.astype(BF16)
    hn_ref[...] = hn_hi
    hn_lo = (hn - hn_hi.astype(F32)).astype(BF16)
    both = _dot(hn_hi, wr_ref[...])
    logits = both[:, 0:LANES] + both[:, LANES:2 * LANES] + _dot(hn_lo, wr_ref[:, 0:LANES]) + br_ref[...]
    lane = lax.broadcasted_iota(jnp.int32, logits.shape, 1)
    lane_f = lane.astype(F32)
    big = float(LANES)
    gl = jnp.where((lane >= C_GROUP) & (lane < C_GROUP + N_GROUPS), logits, -jnp.inf)
    gmax = jnp.max(gl, axis=1, keepdims=True)
    gidx = jnp.min(jnp.where(gl == gmax, lane_f, big), axis=1, keepdims=True) - C_GROUP
    gprob = 1.0 / jnp.sum(jnp.exp(gl - gmax), axis=1, keepdims=True)
    grp_of_lane = (lane // EXPERTS_PER_GROUP).astype(F32)
    el = jnp.where((lane < N_EXPERTS) & (grp_of_lane == gidx), logits, -jnp.inf)
    v1 = jnp.max(el, axis=1, keepdims=True)
    i1 = jnp.min(jnp.where(el == v1, lane_f, big), axis=1, keepdims=True)
    el2 = jnp.where(lane_f == i1, -jnp.inf, el)
    v2 = jnp.max(el2, axis=1, keepdims=True)
    i2 = jnp.min(jnp.where(el2 == v2, lane_f, big), axis=1, keepdims=True)
    e2 = jnp.exp(v2 - v1)
    p1 = 1.0 / (1.0 + e2)
    comb_ref[...] = (gprob * (jnp.where(lane_f == i1, p1, 0.0) + jnp.where(lane_f == i2, e2 * p1, 0.0))
                     + jnp.where(lane == C_GROUP, gidx, 0.0))


def _outproj(x2d, o_a, b_parts, w_out_perm, gain, w_router, b_router):
    n, d = x2d.shape
    row = lambda w: pl.BlockSpec((OUT_TM, w), lambda i: (i, 0))
    full = lambda a: pl.BlockSpec(a.shape, lambda i: (0, 0))
    return pl.pallas_call(
        _outproj_kernel,
        grid=(n // OUT_TM,),
        in_specs=[row(d), row(DA)] + [row(DB)] * 3 + [full(w_out_perm), pl.BlockSpec((1, d), lambda i: (0, 0)),
                                                      full(w_router), full(b_router)],
        out_specs=[row(d), row(d), row(LANES)],
        out_shape=[jax.ShapeDtypeStruct((n, d), F32), jax.ShapeDtypeStruct((n, d), BF16),
                   jax.ShapeDtypeStruct((n, LANES), F32)],
        compiler_params=pltpu.CompilerParams(dimension_semantics=("parallel",), vmem_limit_bytes=VMEM_LIMIT),
        name="outproj_router",
    )(x2d, o_a, *b_parts, w_out_perm, gain.reshape(1, d), w_router, b_router)


MOE_TM = 1024


MOE_SUB = 256
MOE_FINAL_TM = 512


def _moe_kernel(hn_ref, comb_ref, wg_ref, wu_ref, wd_ref, y_ref, perm_ref, hs_ref, cs_ref, ys_ref, start_ref, nsub_ref):
    grp = pl.program_id(1)
    tm, d = hn_ref.shape
    n_pad = perm_ref.shape[0]

    @pl.when(grp == 0)
    def _sort():
        comb = comb_ref[...]
        lane_f = lax.broadcasted_iota(jnp.int32, (tm, LANES), 1).astype(F32)
        onehot = jnp.where(lane_f == comb[:, C_GROUP:C_GROUP + 1], 1.0, 0.0).astype(BF16)
        eye8 = jnp.where(lax.broadcasted_iota(jnp.int32, (8, LANES), 0) == lax.broadcasted_iota(jnp.int32, (8, LANES), 1),
                         1.0, 0.0).astype(BF16)
        onehot_t = _nt_dot(eye8, onehot)
        upper = jnp.where(lax.broadcasted_iota(jnp.int32, (tm, tm), 0) <= lax.broadcasted_iota(jnp.int32, (tm, tm), 1),
                          1.0, 0.0).astype(BF16)
        cum_t = _dot(onehot_t.astype(BF16), upper)
        start = jnp.zeros((1, 1), F32)
        pos_t = jnp.zeros((1, tm), F32)
        for k in range(N_GROUPS):
            padded = jnp.ceil(cum_t[k:k + 1, tm - 1:tm] * (1.0 / MOE_SUB)) * MOE_SUB
            pos_t = pos_t + onehot_t[k:k + 1, :] * (start + cum_t[k:k + 1, :] - 1.0)
            start_ref[k] = start[0, 0].astype(jnp.int32)
            nsub_ref[k] = (padded[0, 0] * (1.0 / MOE_SUB)).astype(jnp.int32)
            start = start + padded
        perm = jnp.where(lax.broadcasted_iota(jnp.int32, (n_pad, tm), 0) == pos_t.astype(jnp.int32), 1.0, 0.0).astype(BF16)
        perm_ref[...] = perm
        hs_ref[...] = _dot(perm, hn_ref[...]).astype(BF16)
        c_hi = comb.astype(BF16)
        rest = comb - c_hi.astype(F32)
        c_mid = rest.astype(BF16)
        c_lo = (rest - c_mid.astype(F32)).astype(BF16)
        cs_ref[...] = _dot(perm, c_hi) + _dot(perm, c_mid) + _dot(perm, c_lo)
        ys_ref[...] = jnp.zeros_like(ys_ref)

    lane = lax.broadcasted_iota(jnp.int32, (MOE_SUB, LANES), 1)

    def segment(s, carry):
        rows = pl.ds(pl.multiple_of(start_ref[grp] + s * MOE_SUB, MOE_SUB), MOE_SUB)
        x = hs_ref[rows, :]
        weights = cs_ref[rows, :]
        acc = jnp.zeros((MOE_SUB, d), F32)
        for j in range(EXPERTS_PER_GROUP):
            gate = _dot(x, wg_ref[j])
            up = _dot(x, wu_ref[j])
            w = jnp.sum(jnp.where(lane == grp * EXPERTS_PER_GROUP + j, weights, 0.0), axis=1, keepdims=True)
            acc = acc + _dot((gate * jax.nn.sigmoid(gate) * up * w).astype(BF16), wd_ref[j])
        ys_ref[rows, :] = acc.astype(BF16)
        return carry

    lax.fori_loop(0, nsub_ref[grp], segment, 0)

    @pl.when(grp == pl.num_programs(1) - 1)
    def _unsort():
        y_ref[...] = lax.dot_general(perm_ref[...], ys_ref[...], (((0,), (0,)), ((), ())),
                                     preferred_element_type=F32).astype(BF16)


def _final_kernel(h_ref, y_ref, g_ref, out_ref):
    y = h_ref[...] + y_ref[...].astype(F32)
    out_ref[...] = y * lax.rsqrt(jnp.mean(y * y, axis=-1, keepdims=True) + EPS) * g_ref[...]


def _moe(h, hn, comb, w_gate, w_up, w_down, gain):
    n, d = h.shape
    tm = min(MOE_TM, n)
    n_pad = tm + (N_GROUPS - 1) * MOE_SUB
    row = lambda w: pl.BlockSpec((tm, w), lambda i, g: (i, 0))
    group_w = lambda a: pl.BlockSpec((EXPERTS_PER_GROUP,) + a.shape[1:], lambda i, g: (g, 0, 0))
    y = pl.pallas_call(
        _moe_kernel,
        grid=(n // tm, N_GROUPS),
        in_specs=[row(d), row(LANES), group_w(w_gate), group_w(w_up), group_w(w_down)],
        out_specs=row(d),
        out_shape=jax.ShapeDtypeStruct((n, d), BF16),
        scratch_shapes=[pltpu.VMEM((n_pad, tm), BF16), pltpu.VMEM((n_pad, d), BF16), pltpu.VMEM((n_pad, LANES), F32),
                        pltpu.VMEM((n_pad, d), BF16), pltpu.SMEM((N_GROUPS,), jnp.int32), pltpu.SMEM((N_GROUPS,), jnp.int32)],
        compiler_params=pltpu.CompilerParams(dimension_semantics=("parallel", "arbitrary"), vmem_limit_bytes=VMEM_LIMIT),
        name="moe_experts",
    )(hn, comb, w_gate, w_up, w_down)
    rowf = lambda w: pl.BlockSpec((MOE_FINAL_TM, w), lambda i: (i, 0))
    return pl.pallas_call(
        _final_kernel,
        grid=(n // MOE_FINAL_TM,),
        in_specs=[rowf(d), rowf(d), pl.BlockSpec((1, d), lambda i: (0, 0))],
        out_specs=rowf(d),
        out_shape=jax.ShapeDtypeStruct((n, d), F32),
        compiler_params=pltpu.CompilerParams(dimension_semantics=("parallel",), vmem_limit_bytes=VMEM_LIMIT),
        name="residual_final_norm",
    )(h, y, gain.reshape(1, d))


def _permute_w_out(w_out):
    d = w_out.shape[1]
    wb = w_out[DA:].reshape(NSA_KV_HEADS, NSA_GROUP, HEAD_DIM, d).transpose(1, 0, 2, 3).reshape(DB, d)
    return jnp.concatenate([w_out[:DA], wb], axis=0).astype(BF16)


def _router_weights(w_group, b_group, w_expert, b_expert):
    d = w_group.shape[0]
    w = jnp.concatenate([w_expert.reshape(d, N_EXPERTS), w_group], axis=1)
    b = jnp.concatenate([b_expert.reshape(N_EXPERTS), b_group])
    pad = LANES - w.shape[1]
    w = jnp.pad(w, ((0, 0), (0, pad))).astype(F32)
    w_hi = w.astype(BF16)
    w_lo = (w - w_hi.astype(F32)).astype(BF16)
    return jnp.concatenate([w_hi, w_lo], axis=1), jnp.pad(b, (0, pad)).reshape(1, LANES).astype(F32)


def _layer(h, rel_bias, norm_mix, w_in, w_out, cmp_pos_k, cmp_pos_v, cmp_k_w1, cmp_k_w2, cmp_v_w1, cmp_v_w2,
           norm_ffn, w_rg, b_rg, w_re, b_re, w_gate, w_up, w_down, out_gain):
    b, t, d = h.shape
    n = b * t
    assert t % (QB * DIL_PATTERNS[-1][1]) == 0 and t // SEL_BLOCK <= LANES and n % MOE_TM == 0
    x2d = h.reshape(n, d)
    seq = lambda a: a.reshape(b, t, a.shape[-1])
    qa, ka, va, qb, kc, vc, ksaug, vs0aug, vs1aug, kw, vw0aug, vw1aug, gates = map(
        seq, _inproj(x2d, norm_mix, _permute_w_in(w_in), t))
    f_a = _bias_1d(rel_bias[:, :DIL_HEADS])
    f_b = _bias_1d(rel_bias[:, DIL_HEADS:]) * LOG2E
    o_a = _mixer_a(qa, ka, va, jnp.stack([_dil_bias(f_a, dil) for _, dil in DIL_PATTERNS]))
    kcmp, vcmp = _compress(kc, vc, cmp_pos_k, cmp_pos_v, cmp_k_w1, cmp_k_w2, cmp_v_w1, cmp_v_w2)
    o_cmp, sel = _compressed_branch(qb, kcmp, vcmp, gates, f_b)
    o_sel = _selected_branch(qb, sel, gates, ksaug, vs0aug, vs1aug, f_b)
    o_win = _window_branch(qb, gates, kw, vw0aug, vw1aug, f_b)
    b_parts = [o.reshape(n, DB) for o in (o_cmp, o_sel, o_win)]
    w_router, b_router = _router_weights(w_rg, b_rg, w_re, b_re)
    h2, hn, comb = _outproj(x2d, o_a.reshape(n, DA), b_parts, _permute_w_out(w_out), norm_ffn, w_router, b_router)
    return _moe(h2, hn, comb, w_gate.astype(BF16), w_up.astype(BF16), w_down.astype(BF16), out_gain)


def kernel(x, rel_bias, norm_mix, w_in, w_out, cmp_pos_k, cmp_pos_v, cmp_k_w1, cmp_k_w2, cmp_v_w1, cmp_v_w2,
           norm_ffn, w_router_group, b_router_group, w_router_expert, b_router_expert, w_gate, w_up, w_down,
           norm_final):
    depth = norm_mix.shape[0]
    assert depth == 1, "the final RMSNorm is fused into the last layer's expert kernel"
    out = _layer(x, rel_bias, norm_mix[0], w_in[0], w_out[0], cmp_pos_k[0], cmp_pos_v[0], cmp_k_w1[0], cmp_k_w2[0],
                 cmp_v_w1[0], cmp_v_w2[0], norm_ffn[0], w_router_group[0], b_router_group[0], w_router_expert[0],
                 b_router_expert[0], w_gate[0], w_up[0], w_down[0], norm_final)
    return out.reshape(x.shape)
```

```python
import functools
import math

import jax
import jax.numpy as jnp
import numpy as np
from jax import lax
from jax.experimental import pallas as pl
from jax.experimental.pallas import tpu as pltpu

HEAD_DIM = 64
DIL_HEADS = 6
NSA_KV_HEADS = 2
NSA_GROUP = 5
NSA_HEADS = NSA_KV_HEADS * NSA_GROUP
N_HEADS = DIL_HEADS + NSA_HEADS
DIL_PATTERNS = ((128, 1), (512, 4), (2048, 16))
CMP_BLOCK = 32
CMP_STRIDE = 16
CMP_HIDDEN = 256
SEL_BLOCK = 64
SEL_TOPK = 16
WIN = 512
N_FORCED = 3
N_BUCKETS = 32
MAX_DISTANCE = 2048
N_GROUPS = 4
EXPERTS_PER_GROUP = 4
N_EXPERTS = N_GROUPS * EXPERTS_PER_GROUP
D_EXPERT = 512
EPS = 1e-6

LANES = 128
QB = 128
NEG = -1.0e30
LOG2E = math.log2(math.e)
DA = DIL_HEADS * HEAD_DIM
DB = NSA_HEADS * HEAD_DIM
N_ROWGROUPS = NSA_HEADS
SEL_KT = 256
VMEM_LIMIT = 56 * 1024 * 1024

F32 = jnp.float32
BF16 = jnp.bfloat16
NT_DIMS = (((1,), (1,)), ((), ()))


def _nt_dot(a, b):
    return lax.dot_general(a, b, NT_DIMS, preferred_element_type=F32)


def _dot(a, b):
    return jnp.dot(a, b, preferred_element_type=F32)


def _bucket_np(dist):
    dist = np.maximum(np.asarray(dist, np.int64), 0)
    max_exact = N_BUCKETS // 2
    x = np.maximum(dist, 1).astype(np.float32) / np.float32(max_exact)
    large = max_exact + (np.log(x) / np.float32(math.log(MAX_DISTANCE / max_exact))
                         * np.float32(N_BUCKETS - max_exact)).astype(np.int32)
    large = np.minimum(large, N_BUCKETS - 1)
    return np.where(dist < max_exact, dist, large).astype(np.int32)


BIAS_LEN = 4096


def _bias_1d(rel_bias_heads):
    onehot = (_bucket_np(np.arange(BIAS_LEN))[None, :] == np.arange(N_BUCKETS)[:, None]).astype(np.float32)
    return jnp.dot(rel_bias_heads.T.astype(F32), jnp.asarray(onehot), precision=lax.Precision.HIGHEST)


def _extend(f, lo, hi):
    assert hi <= f.shape[-1]
    if lo >= 0:
        return f[..., lo:hi]
    pad = jnp.full(f.shape[:-1] + (-lo,), NEG, f.dtype)
    return jnp.concatenate([pad, f[..., :hi]], axis=-1)


def _toeplitz(w, q, c):
    n = q + c - 1
    assert w.shape[-1] == n
    lead = w.shape[:-1]
    wp = jnp.concatenate([w, jnp.zeros(lead + (1,), w.dtype)], axis=-1)
    flat = jnp.broadcast_to(wp[..., None, :], lead + (q, n + 1)).reshape(lead + (q * (n + 1),))
    return flat[..., :q * n].reshape(lead + (q, n))[..., q - 1:q - 1 + c]


def _toeplitz_of(fn_vals, lo, q, c):
    return _toeplitz(jnp.flip(fn_vals, axis=-1), q, c)


IN_TM = 512
C_QA, C_KA, C_VA = 0, DA, 2 * DA
C_QB = 3 * DA
C_KC = C_QB + DB
C_VC, C_KS, C_VS, C_KW, C_VW, C_GT = (C_KC + LANES * i for i in range(1, 7))
N_COLS = C_GT + LANES


def _permute_w_in(w_in):
    scale = 1.0 / math.sqrt(HEAD_DIM)
    sizes = [DA] * 3 + [DB] + [NSA_KV_HEADS * HEAD_DIM] * 6 + [3 * NSA_HEADS]
    offs = np.concatenate([[0], np.cumsum(sizes)])
    part = lambda i: w_in[:, offs[i]:offs[i + 1]]
    d = w_in.shape[0]
    qb = part(3).reshape(d, NSA_KV_HEADS, NSA_GROUP, HEAD_DIM).transpose(0, 2, 1, 3).reshape(d, DB)
    gt = part(10).reshape(d, NSA_KV_HEADS, NSA_GROUP, 3).transpose(0, 3, 2, 1).reshape(d, 3 * NSA_HEADS)
    gt = jnp.pad(gt, ((0, 0), (0, LANES - 3 * NSA_HEADS)))
    cols = [part(0) * scale, part(1), part(2), qb * (scale * LOG2E)] + [part(i) for i in range(4, 10)] + [gt]
    return jnp.concatenate(cols, axis=1).astype(BF16)


def _inproj_kernel(seq_len, x_ref, g_ref, w_ref, qa_ref, ka_ref, va_ref, qb_ref, kc_ref, vc_ref,
                   ksaug_ref, vs0_ref, vs1_ref, kw_ref, vw0_ref, vw1_ref, gates_ref):
    x = x_ref[...]
    xn = (x * lax.rsqrt(jnp.mean(x * x, axis=-1, keepdims=True) + EPS) * g_ref[...]).astype(BF16)
    seg = lambda a, n: _dot(xn, w_ref[:, a:a + n])
    qa_ref[...] = seg(C_QA, DA).astype(BF16)
    ka_ref[...] = seg(C_KA, DA).astype(BF16)
    va_ref[...] = seg(C_VA, DA).astype(BF16)
    qb_ref[...] = seg(C_QB, DB).astype(BF16)
    kc_ref[...] = seg(C_KC, LANES).astype(BF16)
    vc_ref[...] = seg(C_VC, LANES).astype(BF16)
    kw_ref[...] = seg(C_KW, LANES).astype(BF16)
    tm = x.shape[0]
    tok = (pl.program_id(0) * tm) % seq_len + lax.broadcasted_iota(jnp.int32, (tm, LANES), 0)
    lane = lax.broadcasted_iota(jnp.int32, (tm, LANES), 1)
    ksaug_ref[:, 0:LANES] = seg(C_KS, LANES).astype(BF16)
    ksaug_ref[:, LANES:2 * LANES] = jnp.where(lane == tok // SEL_BLOCK, 1.0, 0.0).astype(BF16)
    low = lane < HEAD_DIM
    for col, ref0, ref1 in ((C_VS, vs0_ref, vs1_ref), (C_VW, vw0_ref, vw1_ref)):
        v = seg(col, LANES)
        ref0[...] = jnp.where(low, v, 1.0).astype(BF16)
        ref1[...] = jnp.where(low, 1.0, v).astype(BF16)
    gates_ref[...] = jax.nn.sigmoid(seg(C_GT, LANES))


def _inproj(x2d, gain, w_perm, seq_len):
    n, d = x2d.shape
    row = lambda w: pl.BlockSpec((IN_TM, w), lambda i: (i, 0))
    widths = [DA, DA, DA, DB, LANES, LANES, 2 * LANES, LANES, LANES, LANES, LANES, LANES]
    out_shape = [jax.ShapeDtypeStruct((n, w), BF16) for w in widths] + [jax.ShapeDtypeStruct((n, LANES), F32)]
    return pl.pallas_call(
        functools.partial(_inproj_kernel, seq_len),
        grid=(n // IN_TM,),
        in_specs=[row(d), pl.BlockSpec((1, d), lambda i: (0, 0)), pl.BlockSpec((d, N_COLS), lambda i: (0, 0))],
        out_specs=[row(w) for w in widths] + [row(LANES)],
        out_shape=out_shape,
        compiler_params=pltpu.CompilerParams(dimension_semantics=("parallel",), vmem_limit_bytes=VMEM_LIMIT),
        name="inproj",
    )(x2d, gain.reshape(1, d), w_perm)


def _embed_pair(w, n_tok):
    c = w.shape[1]
    w4 = w.reshape(n_tok, 1, HEAD_DIM, 1, c) * jnp.eye(NSA_KV_HEADS, dtype=w.dtype).reshape(1, 2, 1, 2, 1)
    return w4.reshape(n_tok * 2 * HEAD_DIM, 2 * c)


def _gelu_tanh(x):
    return 0.5 * x * (1.0 + jnp.tanh(math.sqrt(2.0 / math.pi) * (x + 0.044715 * (x * x * x))))


def _compress_kernel(ck_ref, cv_ref, posk_ref, posv_ref, wk1a, wk1b, wk2, wv1a, wv1b, wv2,
                     kout_ref, vout_ref, shift_ref):
    ncp = ck_ref.shape[1]
    for c_ref, pos_ref, w1a, w1b, w2, out_ref in ((ck_ref, posk_ref, wk1a, wk1b, wk2, kout_ref),
                                                  (cv_ref, posv_ref, wv1a, wv1b, wv2, vout_ref)):
        c = c_ref[0].astype(F32)
        first = _dot((c + pos_ref[0:1, :]).astype(BF16), w1a[...])
        second = _dot((c + pos_ref[1:2, :]).astype(BF16), w1b[...])
        shift_ref[0:ncp, :] = second
        shift_ref[ncp:ncp + 8, :] = jnp.zeros((8, second.shape[1]), F32)
        hidden = _gelu_tanh(first + shift_ref[1:ncp + 1, :])
        out_ref[0] = _dot(hidden.astype(BF16), w2[...]).astype(BF16)


def _compress(kc, vc, pos_k, pos_v, k_w1, k_w2, v_w1, v_w2):
    b, t, _ = kc.shape
    ncp = t // CMP_STRIDE
    half = CMP_STRIDE * HEAD_DIM
    wide = CMP_STRIDE * LANES

    def prep(w1, w2, pos):
        pos_pair = jnp.broadcast_to(pos.reshape(2, CMP_STRIDE, 1, HEAD_DIM), (2, CMP_STRIDE, 2, HEAD_DIM))
        return (_embed_pair(w1[:half], CMP_STRIDE).astype(BF16), _embed_pair(w1[half:], CMP_STRIDE).astype(BF16),
                jnp.kron(jnp.eye(NSA_KV_HEADS, dtype=w2.dtype), w2).astype(BF16),
                pos_pair.reshape(2, wide).astype(F32))

    wk1a, wk1b, wk2, posk = prep(k_w1, k_w2, pos_k)
    wv1a, wv1b, wv2, posv = prep(v_w1, v_w2, pos_v)
    full = lambda a: pl.BlockSpec(a.shape, lambda i: (0,) * a.ndim)
    tok = pl.BlockSpec((1, ncp, wide), lambda i: (i, 0, 0))
    out = pl.BlockSpec((1, ncp, LANES), lambda i: (i, 0, 0))
    return pl.pallas_call(
        _compress_kernel,
        grid=(b,),
        in_specs=[tok, tok, full(posk), full(posv), full(wk1a), full(wk1b), full(wk2), full(wv1a), full(wv1b), full(wv2)],
        out_specs=[out, out],
        out_shape=[jax.ShapeDtypeStruct((b, ncp, LANES), BF16)] * 2,
        scratch_shapes=[pltpu.VMEM((ncp + 8, 2 * CMP_HIDDEN), F32)],
        compiler_params=pltpu.CompilerParams(dimension_semantics=("parallel",), vmem_limit_bytes=VMEM_LIMIT),
        name="compress",
    )(kc.reshape(b, ncp, wide), vc.reshape(b, ncp, wide), posk, posv, wk1a, wk1b, wk2, wv1a, wv1b, wv2)


def _pair_masks(rows):
    lane = lax.broadcasted_iota(jnp.int32, (rows, LANES), 1)
    return lane < HEAD_DIM


A_SUPER = QB * DIL_PATTERNS[-1][1]


def _mixer_a_kernel(q_ref, kp_ref, kc_ref, vp_ref, vc_ref, bias_ref, out_ref, qf_ref, kf_ref, vf_ref, o_ref, lse_ref):
    first = pl.program_id(1) == 0
    n_pairs = DIL_HEADS // 2
    for p in range(n_pairs):
        cs = slice(p * LANES, (p + 1) * LANES)
        qf_ref[p] = q_ref[0, :, cs].astype(F32)
        kf_ref[p, 0:A_SUPER, :] = kp_ref[0, :, cs].astype(F32)
        kf_ref[p, A_SUPER:2 * A_SUPER, :] = kc_ref[0, :, cs].astype(F32)
        vf_ref[p, 0:A_SUPER, :] = vp_ref[0, :, cs].astype(F32)
        vf_ref[p, A_SUPER:2 * A_SUPER, :] = vc_ref[0, :, cs].astype(F32)
    low = _pair_masks(QB)
    in_prev = lax.broadcasted_iota(jnp.int32, (QB, 2 * QB), 1) < QB
    zero = jnp.zeros((QB, LANES), BF16)

    def chunk(idx, dil, q_base, k_base, q_span, k_span, off, at_start):
        rows = lambda size: pl.ds(off, size) if dil == 1 else pl.ds(off, size, stride=dil)
        q_win = lambda ref: ref.at[pl.ds(pl.multiple_of(q_base, 8), q_span), :]
        k_win = lambda ref: ref.at[pl.ds(pl.multiple_of(k_base, 8), k_span), :]
        q_rows = rows(QB)
        prev_mask = jnp.where(jnp.logical_and(in_prev, jnp.logical_and(first, at_start)), NEG, 0.0)
        for p in range(n_pairs):
            q = q_win(qf_ref.at[p])[q_rows, :].astype(BF16)
            keys = k_win(kf_ref.at[p])[rows(2 * QB), :].astype(BF16)
            vals = k_win(vf_ref.at[p])[rows(2 * QB), :].astype(BF16)
            o_win, lse_win = q_win(o_ref.at[p]), q_win(lse_ref.at[p])
            lhs = jnp.concatenate([jnp.where(low, q, zero), jnp.where(low, zero, q)], axis=0)
            bias = jnp.concatenate([bias_ref[idx, 2 * p] + prev_mask, bias_ref[idx, 2 * p + 1] + prev_mask], axis=0)
            s = _nt_dot(lhs, keys) + bias
            m = jnp.max(s, axis=1, keepdims=True)
            e = jnp.exp(s - m)
            l = jnp.sum(e, axis=1, keepdims=True)
            pv = _dot(e.astype(BF16), vals) * (1.0 / l)
            lse = m + jnp.log(l)
            o_new = jnp.where(low, pv[:QB], pv[QB:])
            l_new = jnp.where(low, jnp.broadcast_to(lse[:QB], (QB, LANES)), jnp.broadcast_to(lse[QB:], (QB, LANES)))
            if idx > 0:
                o_old, l_old = o_win[q_rows, :], lse_win[q_rows, :]
                mx = jnp.maximum(l_old, l_new)
                w_old, w_new = jnp.exp(l_old - mx), jnp.exp(l_new - mx)
                tot = w_old + w_new
                o_new = (w_old * o_old + w_new * o_new) * (1.0 / tot)
                l_new = mx + jnp.log(tot)
            o_win[q_rows, :] = o_new
            lse_win[q_rows, :] = l_new

    def loop(n, body):
        lax.fori_loop(0, n, lambda i, carry: (body(i), carry)[1], 0)

    for idx, (_, dil) in enumerate(DIL_PATTERNS):
        span = QB * dil
        n_chunks = A_SUPER // span
        if n_chunks > 1:
            for r in range(dil):
                loop(n_chunks, lambda c, idx=idx, dil=dil, span=span, r=r:
                     chunk(idx, dil, span * c, A_SUPER + span * (c - 1), span, 2 * span, r, c == 0))
        else:
            for r in range(8):
                loop(dil // 8, lambda hi, idx=idx, dil=dil, span=span, r=r:
                     chunk(idx, dil, 8 * hi, 8 * hi, span - 8, 2 * span - 8, r, True))
    for p in range(n_pairs):
        out_ref[0, :, p * LANES:(p + 1) * LANES] = o_ref[p].astype(BF16)


def _mixer_a(qa, ka, va, bias):
    b, t, _ = qa.shape
    cur = pl.BlockSpec((1, A_SUPER, DA), lambda bi, i: (bi, i, 0))
    prev = pl.BlockSpec((1, A_SUPER, DA), lambda bi, i: (bi, jnp.maximum(i - 1, 0), 0))
    return pl.pallas_call(
        _mixer_a_kernel,
        grid=(b, t // A_SUPER),
        in_specs=[cur, prev, cur, prev, cur, pl.BlockSpec(bias.shape, lambda bi, i: (0, 0, 0, 0))],
        out_specs=cur,
        out_shape=jax.ShapeDtypeStruct((b, t, DA), BF16),
        scratch_shapes=[pltpu.VMEM((DIL_HEADS // 2, rows, LANES), F32) for rows in (A_SUPER, 2 * A_SUPER, 2 * A_SUPER, A_SUPER, A_SUPER)],
        compiler_params=pltpu.CompilerParams(dimension_semantics=("parallel", "parallel"), vmem_limit_bytes=VMEM_LIMIT),
        name="mixer_a",
    )(qa, ka, ka, va, va, bias)


def _dil_bias(f_a, dil):
    steps = DIL_PATTERNS[0][0]
    g = f_a[:, 0:dil * steps + 1:dil]
    lo, hi = QB - (2 * QB - 1), QB + QB
    vals = jnp.concatenate([jnp.full((DIL_HEADS, -lo), NEG, F32), g, jnp.full((DIL_HEADS, hi - steps - 1), NEG, F32)], axis=1)
    return _toeplitz_of(vals, lo, QB, 2 * QB)


CMP_TILE_KEYS = LANES
CMP_TILE_SPAN = CMP_TILE_KEYS * CMP_STRIDE // QB
CMP_CONST_DELTA = 28


def _cmp_bias(f_b):
    per = QB // CMP_STRIDE
    n_rows = per * (CMP_CONST_DELTA + 1)
    m_lo, m_hi = -(CMP_TILE_KEYS - 1), n_rows
    f_b = f_b.astype(BF16)
    base = _extend(f_b, CMP_STRIDE * m_lo - (CMP_BLOCK - 1), CMP_STRIDE * m_hi - (CMP_BLOCK - 1))
    g = base.reshape(NSA_HEADS, m_hi - m_lo, CMP_STRIDE).transpose(0, 2, 1)
    t = _toeplitz_of(g, m_lo, n_rows, CMP_TILE_KEYS)
    t = t.reshape(NSA_HEADS, CMP_STRIDE, CMP_CONST_DELTA + 1, per, CMP_TILE_KEYS).transpose(2, 0, 3, 1, 4)
    t = t.reshape(CMP_CONST_DELTA + 1, NSA_HEADS, QB, CMP_TILE_KEYS)
    return jnp.concatenate([jnp.full((1,) + t.shape[1:], NEG, t.dtype), t], axis=0)


def _overlap_matrix_t(ncp, n_sel_pad):
    n = np.arange(ncp)[None, :] * CMP_STRIDE
    s = np.arange(n_sel_pad)[:, None] * SEL_BLOCK
    ov = np.clip(np.minimum(n + CMP_BLOCK, s + SEL_BLOCK) - np.maximum(n, s), 0, None) / CMP_BLOCK
    return jnp.asarray(ov, BF16)


def _gate_tile(gates_ref, branch, g):
    c = branch * NSA_HEADS + g * 2
    low = _pair_masks(QB)
    return jnp.where(low, jnp.broadcast_to(gates_ref[0, :, c:c + 1], (QB, LANES)),
                     jnp.broadcast_to(gates_ref[0, :, c + 1:c + 2], (QB, LANES)))


def _masked_q(qb_ref, g, kv):
    q = qb_ref[0, :, g * LANES:(g + 1) * LANES]
    low = _pair_masks(QB)
    keep = low if kv == 0 else jnp.logical_not(low)
    return jnp.where(keep, q, jnp.zeros_like(q))


def _cmp_kernel(n_tiles, qb_ref, kcmp_ref, vcmp_ref, gates_ref, ov_ref, *rest):
    tbl_refs, (oc_ref, sel_ref, q_ref, s_ref, p_ref, pv_ref) = rest[:n_tiles], rest[n_tiles:]
    qblk = pl.program_id(1)
    t0 = qblk * QB
    low = _pair_masks(QB)
    for kv in range(NSA_KV_HEADS):
        for g in range(NSA_GROUP):
            r = kv * NSA_GROUP + g
            q_ref[r * QB:(r + 1) * QB, :] = _masked_q(qb_ref, g, kv)

    def attend(n_vis):
        kc = n_vis * CMP_TILE_KEYS
        n_blk = n_vis * CMP_TILE_SPAN * QB // SEL_BLOCK
        s_ref[:, 0:kc] = _nt_dot(q_ref[...], kcmp_ref[0, 0:kc, :])
        blk = lax.broadcasted_iota(jnp.int32, (n_blk, QB), 0)
        cur = (t0 + lax.broadcasted_iota(jnp.int32, (n_blk, QB), 1)) // SEL_BLOCK
        blk_f = blk.astype(F32)
        forced = (blk == cur) | (blk == cur - 1) | (blk == 0)
        causal = blk <= cur
        scores = []
        for kv in range(NSA_KV_HEADS):
            psum = jnp.zeros((QB, kc), F32)
            for g in range(NSA_GROUP):
                r = kv * NSA_GROUP + g
                rows = slice(r * QB, (r + 1) * QB)
                s = s_ref[rows, 0:kc] + jnp.concatenate([tbl_refs[c][0, r].astype(F32) for c in range(n_vis)], axis=1)
                m = jnp.max(s, axis=1, keepdims=True)
                e = jnp.exp2(s - m)
                den = jnp.sum(e, axis=1, keepdims=True)
                p = jnp.where(m > 0.5 * NEG, e * (1.0 / den), 0.0)
                psum = psum + p
                p_ref[rows, 0:kc] = p.astype(BF16)
            hi = psum.astype(BF16)
            lo = (psum - hi.astype(F32)).astype(BF16)
            ov_t = ov_ref[0:n_blk, 0:kc]
            imp_t = _nt_dot(ov_t, hi) + _nt_dot(ov_t, lo)
            scores.append(jnp.where(forced, -jnp.inf, jnp.where(causal, imp_t, -1.0)))
        pv_ref[...] = _dot(p_ref[:, 0:kc], vcmp_ref[0, 0:kc, :])

        def pick(_, carry):
            new = []
            for val, sel in carry:
                mx = jnp.max(val, axis=0, keepdims=True)
                idx = jnp.min(jnp.where(val == mx, blk_f, float(n_blk)), axis=0, keepdims=True)
                hit = blk_f == idx
                new.append((jnp.where(hit, -jnp.inf, val), jnp.where(hit, 1.0, sel)))
            return tuple(new)

        taken = jnp.where(forced, 1.0, 0.0)
        picked = lax.fori_loop(0, SEL_TOPK - N_FORCED, pick, tuple((v, taken) for v in scores))
        eye = jnp.where(lax.broadcasted_iota(jnp.int32, (QB, QB), 0) == lax.broadcasted_iota(jnp.int32, (QB, QB), 1),
                        1.0, 0.0).astype(BF16)
        for kv in range(NSA_KV_HEADS):
            sel_t = jnp.where(causal, picked[kv][1], 0.0).astype(BF16)
            sel_ref[0, kv, :, 0:n_blk] = _nt_dot(eye, sel_t).astype(BF16)
            if n_blk < LANES:
                sel_ref[0, kv, :, n_blk:LANES] = jnp.zeros((QB, LANES - n_blk), BF16)

    n_vis = qblk // CMP_TILE_SPAN + 1
    for w in range(1, n_tiles + 1):
        pl.when(n_vis == w)(functools.partial(attend, w))

    for g in range(NSA_GROUP):
        o0, o1 = pv_ref[g * QB:(g + 1) * QB, :], pv_ref[(NSA_GROUP + g) * QB:(NSA_GROUP + g + 1) * QB, :]
        oc_ref[0, :, g * LANES:(g + 1) * LANES] = (jnp.where(low, o0, o1) * _gate_tile(gates_ref, 0, g)).astype(BF16)


def _compressed_branch(qb, kcmp, vcmp, gates, f_b):
    b, t, _ = qb.shape
    ncp = kcmp.shape[1]
    n_tiles = ncp // CMP_TILE_KEYS
    tbl = _cmp_bias(f_b)
    ov = _overlap_matrix_t(ncp, LANES)

    def tbl_spec(c):
        return pl.BlockSpec((1, NSA_HEADS, QB, CMP_TILE_KEYS),
                            lambda bi, i: (jnp.clip(i - CMP_TILE_SPAN * c, -1, CMP_CONST_DELTA) + 1, 0, 0, 0))

    blockq = lambda w: pl.BlockSpec((1, QB, w), lambda bi, i: (bi, i, 0))
    batch = lambda a: pl.BlockSpec((1,) + a.shape[1:], lambda bi, i: (bi, 0, 0))
    return pl.pallas_call(
        functools.partial(_cmp_kernel, n_tiles),
        grid=(b, t // QB),
        in_specs=[blockq(DB), batch(kcmp), batch(vcmp), blockq(LANES), pl.BlockSpec(ov.shape, lambda bi, i: (0, 0))]
                 + [tbl_spec(c) for c in range(n_tiles)],
        out_specs=[blockq(DB), pl.BlockSpec((1, NSA_KV_HEADS, QB, LANES), lambda bi, i: (bi, 0, i, 0))],
        out_shape=[jax.ShapeDtypeStruct((b, t, DB), BF16), jax.ShapeDtypeStruct((b, NSA_KV_HEADS, t, LANES), BF16)],
        scratch_shapes=[pltpu.VMEM((N_ROWGROUPS * QB, LANES), BF16), pltpu.VMEM((N_ROWGROUPS * QB, ncp), F32),
                        pltpu.VMEM((N_ROWGROUPS * QB, ncp), BF16), pltpu.VMEM((N_ROWGROUPS * QB, LANES), F32)],
        compiler_params=pltpu.CompilerParams(dimension_semantics=("parallel", "parallel"), vmem_limit_bytes=VMEM_LIMIT),
        name="nsa_compressed",
    )(qb, kcmp, vcmp, gates, ov, *([tbl] * n_tiles))


SEL_NEAR = 13


def _sel_bias(f_b):
    n_off = SEL_NEAR + 1
    cols = QB * n_off
    rel = f_b - f_b[:, BIAS_LEN - 1:]
    lo = -(QB - 1)
    big = _toeplitz_of(_extend(rel, lo, lo + QB + cols - 1), lo, QB, cols)
    tiles = jnp.flip(big.reshape(NSA_HEADS, QB, n_off, QB).transpose(2, 0, 1, 3), axis=0)
    return jnp.concatenate([jnp.zeros((1,) + tiles.shape[1:], F32), tiles], axis=0)


def _pair_ratio(acc0, acc1):
    low = _pair_masks(acc0.shape[0])
    den = pltpu.roll(jnp.where(low, acc1, acc0), HEAD_DIM, axis=1)
    return jnp.where(low, acc0, acc1) * (1.0 / den)


def _sel_kernel(qb_ref, sel_ref, gates_ref, cfar_ref, ks_ref, vs0_ref, vs1_ref, tbl_ref, out_ref,
                qaug_ref, s_ref, s1_ref, acc_ref, m_ref, alpha_ref):
    qblk = pl.program_id(1)
    for kv in range(NSA_KV_HEADS):
        unchosen = jnp.where(sel_ref[0, kv].astype(F32) > 0.0, 0.0, NEG)
        for g in range(NSA_GROUP):
            r = kv * NSA_GROUP + g
            rows = slice(r * QB, (r + 1) * QB)
            qaug_ref[rows, 0:LANES] = _masked_q(qb_ref, g, kv)
            qaug_ref[rows, LANES:2 * LANES] = (unchosen + cfar_ref[r:r + 1, :]).astype(BF16)
    acc_ref[...] = jnp.zeros_like(acc_ref)
    m_ref[...] = jnp.full_like(m_ref, NEG)

    last_tile = ks_ref.shape[2] // SEL_KT - 1

    def scores(j, dst_ref):
        start = pl.multiple_of(jnp.minimum(j, last_tile) * SEL_KT, SEL_KT)
        dst_ref[...] = _dot(qaug_ref[...], ks_ref[0, :, pl.ds(start, SEL_KT)])

    def consume(j, src_ref, near):
        start = pl.multiple_of(j * SEL_KT, SEL_KT)
        for r in range(N_ROWGROUPS):
            rows = slice(r * QB, (r + 1) * QB)
            s = src_ref[rows, :]
            if near:
                e1 = jnp.clip(qblk - 2 * j + 1, 0, SEL_NEAR + 1)
                e2 = jnp.clip(qblk - 2 * j, 0, SEL_NEAR + 1)
                s = s + jnp.concatenate([tbl_ref[e1, r], tbl_ref[e2, r]], axis=1)
                src_ref[rows, :] = s
            m_old = m_ref[rows, :]
            m_new = jnp.maximum(m_old, jnp.max(s, axis=1, keepdims=True))
            alpha_ref[rows, :] = jnp.exp2(m_old - m_new)
            m_ref[rows, :] = m_new
        for r in range(N_ROWGROUPS):
            rows = slice(r * QB, (r + 1) * QB)
            vals = (vs0_ref if r < NSA_GROUP else vs1_ref)[0, pl.ds(start, SEL_KT), :]
            m_new = m_ref[rows, :]
            p = jnp.exp2(src_ref[rows, :] - jnp.concatenate([m_new, m_new], axis=1))
            acc_ref[rows, :] = alpha_ref[rows, :] * acc_ref[rows, :] + _dot(p.astype(BF16), vals)

    def tile_pair(i, near):
        scores(2 * i + 1, s1_ref)
        consume(2 * i, s_ref, near)
        scores(2 * i + 2, s_ref)
        consume(2 * i + 1, s1_ref, near)

    n_pairs = ((qblk + 2) // 2 + 1) // 2
    n_far = jnp.maximum((qblk - (SEL_NEAR - 1)) // 2, 0) // 2
    scores(0, s_ref)
    lax.fori_loop(0, n_far, lambda i, c: (tile_pair(i, False), c)[1], 0)
    lax.fori_loop(n_far, n_pairs, lambda i, c: (tile_pair(i, True), c)[1], 0)
    for g in range(NSA_GROUP):
        ratio = _pair_ratio(acc_ref[g * QB:(g + 1) * QB, :], acc_ref[(NSA_GROUP + g) * QB:(NSA_GROUP + g + 1) * QB, :])
        out_ref[0, :, g * LANES:(g + 1) * LANES] = (ratio * _gate_tile(gates_ref, 1, g)).astype(BF16)


def _selected_branch(qb, sel, gates, ksaug, vs0aug, vs1aug, f_b):
    b, t, _ = qb.shape
    tbl = _sel_bias(f_b)
    ks_t = jnp.swapaxes(ksaug, 1, 2)
    cfar = jnp.broadcast_to(f_b[:, BIAS_LEN - 1:], (NSA_HEADS, LANES))
    cfar = jnp.pad(cfar, ((0, 16 - NSA_HEADS), (0, 0)))
    blockq = lambda w: pl.BlockSpec((1, QB, w), lambda bi, i: (bi, i, 0))
    batch = lambda a: pl.BlockSpec((1,) + a.shape[1:], lambda bi, i: (bi, 0, 0))
    rows = N_ROWGROUPS * QB
    return pl.pallas_call(
        _sel_kernel,
        grid=(b, t // QB),
        in_specs=[blockq(DB), pl.BlockSpec((1, NSA_KV_HEADS, QB, LANES), lambda bi, i: (bi, 0, i, 0)), blockq(LANES),
                  pl.BlockSpec(cfar.shape, lambda bi, i: (0, 0)), batch(ks_t), batch(vs0aug), batch(vs1aug),
                  pl.BlockSpec(tbl.shape, lambda bi, i: (0, 0, 0, 0))],
        out_specs=blockq(DB),
        out_shape=jax.ShapeDtypeStruct((b, t, DB), BF16),
        scratch_shapes=[pltpu.VMEM((rows, 2 * LANES), BF16), pltpu.VMEM((rows, SEL_KT), F32), pltpu.VMEM((rows, SEL_KT), F32),
                        pltpu.VMEM((rows, LANES), F32), pltpu.VMEM((rows, LANES), F32), pltpu.VMEM((rows, LANES), F32)],
        compiler_params=pltpu.CompilerParams(dimension_semantics=("parallel", "parallel"), vmem_limit_bytes=VMEM_LIMIT),
        name="nsa_selected",
    )(qb, sel, gates, cfar, ks_t, vs0aug, vs1aug, tbl)


WIN_KEYS = WIN + QB


def _win_bias(f_b):
    lo = WIN - (WIN_KEYS - 1)
    vals = _extend(f_b[:, :WIN], lo, WIN)
    vals = jnp.concatenate([vals, jnp.full((NSA_HEADS, lo + QB + WIN_KEYS - 1 - WIN), NEG, F32)], axis=1)
    return _toeplitz_of(vals, lo, QB, WIN_KEYS)


def _win_kernel(qb_ref, gates_ref, kw_ref, vw0_ref, vw1_ref, tbl_ref, out_ref, q_ref, s_ref):
    qblk = pl.program_id(1)
    for kv in range(NSA_KV_HEADS):
        for g in range(NSA_GROUP):
            r = kv * NSA_GROUP + g
            q_ref[r * QB:(r + 1) * QB, :] = _masked_q(qb_ref, g, kv)
    start = pl.multiple_of(qblk * QB, QB)
    s_ref[...] = _nt_dot(q_ref[...], kw_ref[0, pl.ds(start, WIN_KEYS), :])
    col = lax.broadcasted_iota(jnp.int32, (1, WIN_KEYS), 1)
    pad_mask = jnp.where(col + qblk * QB >= WIN, 0.0, NEG)
    outs = []
    for r in range(N_ROWGROUPS):
        vals = (vw0_ref if r < NSA_GROUP else vw1_ref)[0, pl.ds(start, WIN_KEYS), :]
        s = s_ref[r * QB:(r + 1) * QB, :] + tbl_ref[r] + pad_mask
        e = jnp.exp2(s - jnp.max(s, axis=1, keepdims=True))
        outs.append(_dot(e.astype(BF16), vals))
    for g in range(NSA_GROUP):
        out_ref[0, :, g * LANES:(g + 1) * LANES] = (_pair_ratio(outs[g], outs[NSA_GROUP + g])
                                                    * _gate_tile(gates_ref, 2, g)).astype(BF16)


def _window_branch(qb, gates, kw, vw0aug, vw1aug, f_b):
    b, t, _ = qb.shape
    tbl = _win_bias(f_b)
    pad_front = lambda a: jnp.pad(a, ((0, 0), (WIN, 0), (0, 0)))
    kw_pad, vw0_pad, vw1_pad = pad_front(kw), pad_front(vw0aug), pad_front(vw1aug)
    blockq = lambda w: pl.BlockSpec((1, QB, w), lambda bi, i: (bi, i, 0))
    batch = lambda a: pl.BlockSpec((1,) + a.shape[1:], lambda bi, i: (bi, 0, 0))
    rows = N_ROWGROUPS * QB
    return pl.pallas_call(
        _win_kernel,
        grid=(b, t // QB),
        in_specs=[blockq(DB), blockq(LANES), batch(kw_pad), batch(vw0_pad), batch(vw1_pad),pl.BlockSpec(tbl.shape, lambda bi, i: (0, 0, 0))],
        out_specs=blockq(DB),
        out_shape=jax.ShapeDtypeStruct((b, t, DB), BF16),
        scratch_shapes=[pltpu.VMEM((rows, LANES), BF16), pltpu.VMEM((rows, WIN_KEYS), F32)],
        compiler_params=pltpu.CompilerParams(dimension_semantics=("parallel", "parallel"), vmem_limit_bytes=VMEM_LIMIT),
        name="nsa_window",
    )(qb, gates, kw_pad, vw0_pad, vw1_pad, tbl)


OUT_TM = 256
C_GROUP = N_EXPERTS


def _outproj_kernel(x_ref, oa_ref, oc_ref, os_ref, ow_ref,
                    wout_ref, g_ref, wr_ref, br_ref, h_ref, hn_ref, comb_ref):
    ob = oc_ref[...].astype(F32) + os_ref[...].astype(F32) + ow_ref[...].astype(F32)
    y = _dot(oa_ref[...], wout_ref[0:DA, :]) + _dot(ob.astype(BF16), wout_ref[DA:DA + DB, :])
    h = x_ref[...] + y
    h_ref[...] = h
    hn = h * lax.rsqrt(jnp.mean(h * h, axis=-1, keepdims=True) + EPS) * g_ref[...]
    hn_hi = hn.astype(BF16)
    hn_ref[...] = hn_hi
    hn_lo = (hn - hn_hi.astype(F32)).astype(BF16)
    both = _dot(hn_hi, wr_ref[...])
    logits = both[:, 0:LANES] + both[:, LANES:2 * LANES] + _dot(hn_lo, wr_ref[:, 0:LANES]) + br_ref[...]
    lane = lax.broadcasted_iota(jnp.int32, logits.shape, 1)
    lane_f = lane.astype(F32)
    big = float(LANES)
    gl = jnp.where((lane >= C_GROUP) & (lane < C_GROUP + N_GROUPS), logits, -jnp.inf)
    gmax = jnp.max(gl, axis=1, keepdims=True)
    gidx = jnp.min(jnp.where(gl == gmax, lane_f, big), axis=1, keepdims=True) - C_GROUP
    gprob = 1.0 / jnp.sum(jnp.exp(gl - gmax), axis=1, keepdims=True)
    grp_of_lane = (lane // EXPERTS_PER_GROUP).astype(F32)
    el = jnp.where((lane < N_EXPERTS) & (grp_of_lane == gidx), logits, -jnp.inf)
    v1 = jnp.max(el, axis=1, keepdims=True)
    i1 = jnp.min(jnp.where(el == v1, lane_f, big), axis=1, keepdims=True)
    el2 = jnp.where(lane_f == i1, -jnp.inf, el)
    v2 = jnp.max(el2, axis=1, keepdims=True)
    i2 = jnp.min(jnp.where(el2 == v2, lane_f, big), axis=1, keepdims=True)
    e2 = jnp.exp(v2 - v1)
    p1 = 1.0 / (1.0 + e2)
    comb_ref[...] = (gprob * (jnp.where(lane_f == i1, p1, 0.0) + jnp.where(lane_f == i2, e2 * p1, 0.0))
                     + jnp.where(lane == C_GROUP, gidx, 0.0))


def _outproj(x2d, o_a, b_parts, w_out_perm, gain, w_router, b_router):
    n, d = x2d.shape
    row = lambda w: pl.BlockSpec((OUT_TM, w), lambda i: (i, 0))
    full = lambda a: pl.BlockSpec(a.shape, lambda i: (0, 0))
    return pl.pallas_call(
        _outproj_kernel,
        grid=(n // OUT_TM,),
        in_specs=[row(d), row(DA)] + [row(DB)] * 3 + [full(w_out_perm), pl.BlockSpec((1, d), lambda i: (0, 0)),
                                                      full(w_router), full(b_router)],
        out_specs=[row(d), row(d), row(LANES)],
        out_shape=[jax.ShapeDtypeStruct((n, d), F32), jax.ShapeDtypeStruct((n, d), BF16),
                   jax.ShapeDtypeStruct((n, LANES), F32)],
        compiler_params=pltpu.CompilerParams(dimension_semantics=("parallel",), vmem_limit_bytes=VMEM_LIMIT),
        name="outproj_router",
    )(x2d, o_a, *b_parts, w_out_perm, gain.reshape(1, d), w_router, b_router)


MOE_TM = 1024


MOE_SUB = 128
MOE_FINAL_TM = 512


def _moe_kernel(hn_ref, comb_ref, wg_ref, wu_ref, wd_ref, y_ref, perm_ref, hs_ref, cs_ref, ys_ref, start_ref, nsub_ref):
    grp = pl.program_id(1)
    tm, d = hn_ref.shape
    n_pad = perm_ref.shape[0]

    @pl.when(grp == 0)
    def _sort():
        comb = comb_ref[...]
        lane_f = lax.broadcasted_iota(jnp.int32, (tm, LANES), 1).astype(F32)
        onehot = jnp.where(lane_f == comb[:, C_GROUP:C_GROUP + 1], 1.0, 0.0).astype(BF16)
        eye8 = jnp.where(lax.broadcasted_iota(jnp.int32, (8, LANES), 0) == lax.broadcasted_iota(jnp.int32, (8, LANES), 1),
                         1.0, 0.0).astype(BF16)
        onehot_t = _nt_dot(eye8, onehot)
        upper = jnp.where(lax.broadcasted_iota(jnp.int32, (tm, tm), 0) <= lax.broadcasted_iota(jnp.int32, (tm, tm), 1),
                          1.0, 0.0).astype(BF16)
        cum_t = _dot(onehot_t.astype(BF16), upper)
        start = jnp.zeros((1, 1), F32)
        pos_t = jnp.zeros((1, tm), F32)
        for k in range(N_GROUPS):
            padded = jnp.ceil(cum_t[k:k + 1, tm - 1:tm] * (1.0 / MOE_SUB)) * MOE_SUB
            pos_t = pos_t + onehot_t[k:k + 1, :] * (start + cum_t[k:k + 1, :] - 1.0)
            start_ref[k] = start[0, 0].astype(jnp.int32)
            nsub_ref[k] = (padded[0, 0] * (1.0 / MOE_SUB)).astype(jnp.int32)
            start = start + padded
        perm = jnp.where(lax.broadcasted_iota(jnp.int32, (n_pad, tm), 0) == pos_t.astype(jnp.int32), 1.0, 0.0).astype(BF16)
        perm_ref[...] = perm
        hs_ref[...] = _dot(perm, hn_ref[...]).astype(BF16)
        c_hi = comb.astype(BF16)
        rest = comb - c_hi.astype(F32)
        c_mid = rest.astype(BF16)
        c_lo = (rest - c_mid.astype(F32)).astype(BF16)
        cs_ref[...] = _dot(perm, c_hi) + _dot(perm, c_mid) + _dot(perm, c_lo)
        ys_ref[...] = jnp.zeros_like(ys_ref)

    lane = lax.broadcasted_iota(jnp.int32, (MOE_SUB, LANES), 1)

    def segment(s, carry):
        rows = pl.ds(pl.multiple_of(start_ref[grp] + s * MOE_SUB, MOE_SUB), MOE_SUB)
        x = hs_ref[rows, :]
        weights = cs_ref[rows, :]
        acc = jnp.zeros((MOE_SUB, d), F32)
        for j in range(EXPERTS_PER_GROUP):
            gate = _dot(x, wg_ref[j])
            up = _dot(x, wu_ref[j])
            w = jnp.sum(jnp.where(lane == grp * EXPERTS_PER_GROUP + j, weights, 0.0), axis=1, keepdims=True)
            acc = acc + _dot((gate * jax.nn.sigmoid(gate) * up * w).astype(BF16), wd_ref[j])
        ys_ref[rows, :] = acc.astype(BF16)
        return carry

    lax.fori_loop(0, nsub_ref[grp], segment, 0)

    @pl.when(grp == pl.num_programs(1) - 1)
    def _unsort():
        y_ref[...] = lax.dot_general(perm_ref[...], ys_ref[...], (((0,), (0,)), ((), ())),
                                     preferred_element_type=F32).astype(BF16)


def _final_kernel(h_ref, y_ref, g_ref, out_ref):
    y = h_ref[...] + y_ref[...].astype(F32)
    out_ref[...] = y * lax.rsqrt(jnp.mean(y * y, axis=-1, keepdims=True) + EPS) * g_ref[...]


def _moe(h, hn, comb, w_gate, w_up, w_down, gain):
    n, d = h.shape
    tm = min(MOE_TM, n)
    n_pad = tm + (N_GROUPS - 1) * MOE_SUB
    row = lambda w: pl.BlockSpec((tm, w), lambda i, g: (i, 0))
    group_w = lambda a: pl.BlockSpec((EXPERTS_PER_GROUP,) + a.shape[1:], lambda i, g: (g, 0, 0))
    y = pl.pallas_call(
        _moe_kernel,
        grid=(n // tm, N_GROUPS),
        in_specs=[row(d), row(LANES), group_w(w_gate), group_w(w_up), group_w(w_down)],
        out_specs=row(d),
        out_shape=jax.ShapeDtypeStruct((n, d), BF16),
        scratch_shapes=[pltpu.VMEM((n_pad, tm), BF16), pltpu.VMEM((n_pad, d), BF16), pltpu.VMEM((n_pad, LANES), F32),
                        pltpu.VMEM((n_pad, d), BF16), pltpu.SMEM((N_GROUPS,), jnp.int32), pltpu.SMEM((N_GROUPS,), jnp.int32)],
        compiler_params=pltpu.CompilerParams(dimension_semantics=("parallel", "arbitrary"), vmem_limit_bytes=VMEM_LIMIT),
        name="moe_experts",
    )(hn, comb, w_gate, w_up, w_down)
    rowf = lambda w: pl.BlockSpec((MOE_FINAL_TM, w), lambda i: (i, 0))
    return pl.pallas_call(
        _final_kernel,
        grid=(n // MOE_FINAL_TM,),
        in_specs=[rowf(d), rowf(d), pl.BlockSpec((1, d), lambda i: (0, 0))],
        out_specs=rowf(d),
        out_shape=jax.ShapeDtypeStruct((n, d), F32),
        compiler_params=pltpu.CompilerParams(dimension_semantics=("parallel",), vmem_limit_bytes=VMEM_LIMIT),
        name="residual_final_norm",
    )(h, y, gain.reshape(1, d))


def _permute_w_out(w_out):
    d = w_out.shape[1]
    wb = w_out[DA:].reshape(NSA_KV_HEADS, NSA_GROUP, HEAD_DIM, d).transpose(1, 0, 2, 3).reshape(DB, d)
    return jnp.concatenate([w_out[:DA], wb], axis=0).astype(BF16)


def _router_weights(w_group, b_group, w_expert, b_expert):
    d = w_group.shape[0]
    w = jnp.concatenate([w_expert.reshape(d, N_EXPERTS), w_group], axis=1)
    b = jnp.concatenate([b_expert.reshape(N_EXPERTS), b_group])
    pad = LANES - w.shape[1]
    w = jnp.pad(w, ((0, 0), (0, pad))).astype(F32)
    w_hi = w.astype(BF16)
    w_lo = (w - w_hi.astype(F32)).astype(BF16)
    return jnp.concatenate([w_hi, w_lo], axis=1), jnp.pad(b, (0, pad)).reshape(1, LANES).astype(F32)


def _layer(h, rel_bias, norm_mix, w_in, w_out, cmp_pos_k, cmp_pos_v, cmp_k_w1, cmp_k_w2, cmp_v_w1, cmp_v_w2,
           norm_ffn, w_rg, b_rg, w_re, b_re, w_gate, w_up, w_down, out_gain):
    b, t, d = h.shape
    n = b * t
    assert t % (QB * DIL_PATTERNS[-1][1]) == 0 and t // SEL_BLOCK <= LANES and n % MOE_TM == 0
    x2d = h.reshape(n, d)
    seq = lambda a: a.reshape(b, t, a.shape[-1])
    qa, ka, va, qb, kc, vc, ksaug, vs0aug, vs1aug, kw, vw0aug, vw1aug, gates = map(
        seq, _inproj(x2d, norm_mix, _permute_w_in(w_in), t))
    f_a = _bias_1d(rel_bias[:, :DIL_HEADS])
    f_b = _bias_1d(rel_bias[:, DIL_HEADS:]) * LOG2E
    o_a = _mixer_a(qa, ka, va, jnp.stack([_dil_bias(f_a, dil) for _, dil in DIL_PATTERNS]))
    kcmp, vcmp = _compress(kc, vc, cmp_pos_k, cmp_pos_v, cmp_k_w1, cmp_k_w2, cmp_v_w1, cmp_v_w2)
    o_cmp, sel = _compressed_branch(qb, kcmp, vcmp, gates, f_b)
    o_sel = _selected_branch(qb, sel, gates, ksaug, vs0aug, vs1aug, f_b)
    o_win = _window_branch(qb, gates, kw, vw0aug, vw1aug, f_b)
    b_parts = [o.reshape(n, DB) for o in (o_cmp, o_sel, o_win)]
    w_router, b_router = _router_weights(w_rg, b_rg, w_re, b_re)
    h2, hn, comb = _outproj(x2d, o_a.reshape(n, DA), b_parts, _permute_w_out(w_out), norm_ffn, w_router, b_router)
    return _moe(h2, hn, comb, w_gate.astype(BF16), w_up.astype(BF16), w_down.astype(BF16), out_gain)


def kernel(x, rel_bias, norm_mix, w_in, w_out, cmp_pos_k, cmp_pos_v, cmp_k_w1, cmp_k_w2, cmp_v_w1, cmp_v_w2,
           norm_ffn, w_router_group, b_router_group, w_router_expert, b_router_expert, w_gate, w_up, w_down,
           norm_final):
    depth = norm_mix.shape[0]
    assert depth == 1, "the final RMSNorm is fused into the last layer's expert kernel"
    out = _layer(x, rel_bias, norm_mix[0], w_in[0], w_out[0], cmp_pos_k[0], cmp_pos_v[0], cmp_k_w1[0], cmp_k_w2[0],
                 cmp_v_w1[0], cmp_v_w2[0], norm_ffn[0], w_router_group[0], b_router_group[0], w_router_expert[0],
                 b_router_expert[0], w_gate[0], w_up[0], w_down[0], norm_final)
    return out.reshape(x.shape)
```

```python
import functools
import math

import jax
import jax.numpy as jnp
import numpy as np
from jax import lax
from jax.experimental import pallas as pl
from jax.experimental.pallas import tpu as pltpu

HEAD_DIM = 64
DIL_HEADS = 6
NSA_KV_HEADS = 2
NSA_GROUP = 5
NSA_HEADS = NSA_KV_HEADS * NSA_GROUP
N_HEADS = DIL_HEADS + NSA_HEADS
DIL_PATTERNS = ((128, 1), (512, 4), (2048, 16))
CMP_BLOCK = 32
CMP_STRIDE = 16
CMP_HIDDEN = 256
SEL_BLOCK = 64
SEL_TOPK = 16
WIN = 512
N_FORCED = 3
N_BUCKETS = 32
MAX_DISTANCE = 2048
N_GROUPS = 4
EXPERTS_PER_GROUP = 4
N_EXPERTS = N_GROUPS * EXPERTS_PER_GROUP
D_EXPERT = 512
EPS = 1e-6

LANES = 128
QB = 128
NEG = -1.0e30
LOG2E = math.log2(math.e)
DA = DIL_HEADS * HEAD_DIM
DB = NSA_HEADS * HEAD_DIM
N_ROWGROUPS = NSA_HEADS
SEL_KT = 256
VMEM_LIMIT = 56 * 1024 * 1024

F32 = jnp.float32
BF16 = jnp.bfloat16
NT_DIMS = (((1,), (1,)), ((), ()))


def _nt_dot(a, b):
    return lax.dot_general(a, b, NT_DIMS, preferred_element_type=F32)


def _dot(a, b):
    return jnp.dot(a, b, preferred_element_type=F32)


def _bucket_np(dist):
    dist = np.maximum(np.asarray(dist, np.int64), 0)
    max_exact = N_BUCKETS // 2
    x = np.maximum(dist, 1).astype(np.float32) / np.float32(max_exact)
    large = max_exact + (np.log(x) / np.float32(math.log(MAX_DISTANCE / max_exact))
                         * np.float32(N_BUCKETS - max_exact)).astype(np.int32)
    large = np.minimum(large, N_BUCKETS - 1)
    return np.where(dist < max_exact, dist, large).astype(np.int32)


BIAS_LEN = 4096


def _bias_1d(rel_bias_heads):
    onehot = (_bucket_np(np.arange(BIAS_LEN))[None, :] == np.arange(N_BUCKETS)[:, None]).astype(np.float32)
    return jnp.dot(rel_bias_heads.T.astype(F32), jnp.asarray(onehot), precision=lax.Precision.HIGHEST)


def _extend(f, lo, hi):
    assert hi <= f.shape[-1]
    if lo >= 0:
        return f[..., lo:hi]
    pad = jnp.full(f.shape[:-1] + (-lo,), NEG, f.dtype)
    return jnp.concatenate([pad, f[..., :hi]], axis=-1)


def _toeplitz(w, q, c):
    n = q + c - 1
    assert w.shape[-1] == n
    lead = w.shape[:-1]
    wp = jnp.concatenate([w, jnp.zeros(lead + (1,), w.dtype)], axis=-1)
    flat = jnp.broadcast_to(wp[..., None, :], lead + (q, n + 1)).reshape(lead + (q * (n + 1),))
    return flat[..., :q * n].reshape(lead + (q, n))[..., q - 1:q - 1 + c]


def _toeplitz_of(fn_vals, lo, q, c):
    return _toeplitz(jnp.flip(fn_vals, axis=-1), q, c)


IN_TM = 512
C_QA, C_KA, C_VA = 0, DA, 2 * DA
C_QB = 3 * DA
C_KC = C_QB + DB
C_VC, C_KS, C_VS, C_KW, C_VW, C_GT = (C_KC + LANES * i for i in range(1, 7))
N_COLS = C_GT + LANES


def _permute_w_in(w_in):
    scale = 1.0 / math.sqrt(HEAD_DIM)
    sizes = [DA] * 3 + [DB] + [NSA_KV_HEADS * HEAD_DIM] * 6 + [3 * NSA_HEADS]
    offs = np.concatenate([[0], np.cumsum(sizes)])
    part = lambda i: w_in[:, offs[i]:offs[i + 1]]
    d = w_in.shape[0]
    qb = part(3).reshape(d, NSA_KV_HEADS, NSA_GROUP, HEAD_DIM).transpose(0, 2, 1, 3).reshape(d, DB)
    gt = part(10).reshape(d, NSA_KV_HEADS, NSA_GROUP, 3).transpose(0, 3, 2, 1).reshape(d, 3 * NSA_HEADS)
    gt = jnp.pad(gt, ((0, 0), (0, LANES - 3 * NSA_HEADS)))
    cols = [part(0) * scale, part(1), part(2), qb * (scale * LOG2E)] + [part(i) for i in range(4, 10)] + [gt]
    return jnp.concatenate(cols, axis=1).astype(BF16)


def _inproj_kernel(seq_len, x_ref, g_ref, w_ref, qa_ref, ka_ref, va_ref, qb_ref, kc_ref, vc_ref,
                   ksaug_ref, vs0_ref, vs1_ref, kw_ref, vw0_ref, vw1_ref, gates_ref):
    x = x_ref[...]
    xn = (x * lax.rsqrt(jnp.mean(x * x, axis=-1, keepdims=True) + EPS) * g_ref[...]).astype(BF16)
    seg = lambda a, n: _dot(xn, w_ref[:, a:a + n])
    qa_ref[...] = seg(C_QA, DA).astype(BF16)
    ka_ref[...] = seg(C_KA, DA).astype(BF16)
    va_ref[...] = seg(C_VA, DA).astype(BF16)
    qb_ref[...] = seg(C_QB, DB).astype(BF16)
    kc_ref[...] = seg(C_KC, LANES).astype(BF16)
    vc_ref[...] = seg(C_VC, LANES).astype(BF16)
    kw_ref[...] = seg(C_KW, LANES).astype(BF16)
    tm = x.shape[0]
    tok = (pl.program_id(0) * tm) % seq_len + lax.broadcasted_iota(jnp.int32, (tm, LANES), 0)
    lane = lax.broadcasted_iota(jnp.int32, (tm, LANES), 1)
    ksaug_ref[:, 0:LANES] = seg(C_KS, LANES).astype(BF16)
    ksaug_ref[:, LANES:2 * LANES] = jnp.where(lane == tok // SEL_BLOCK, 1.0, 0.0).astype(BF16)
    low = lane < HEAD_DIM
    for col, ref0, ref1 in ((C_VS, vs0_ref, vs1_ref), (C_VW, vw0_ref, vw1_ref)):
        v = seg(col, LANES)
        ref0[...] = jnp.where(low, v, 1.0).astype(BF16)
        ref1[...] = jnp.where(low, 1.0, v).astype(BF16)
    gates_ref[...] = jax.nn.sigmoid(seg(C_GT, LANES))


def _inproj(x2d, gain, w_perm, seq_len):
    n, d = x2d.shape
    row = lambda w: pl.BlockSpec((IN_TM, w), lambda i: (i, 0))
    widths = [DA, DA, DA, DB, LANES, LANES, 2 * LANES, LANES, LANES, LANES, LANES, LANES]
    out_shape = [jax.ShapeDtypeStruct((n, w), BF16) for w in widths] + [jax.ShapeDtypeStruct((n, LANES), F32)]
    return pl.pallas_call(
        functools.partial(_inproj_kernel, seq_len),
        grid=(n // IN_TM,),
        in_specs=[row(d), pl.BlockSpec((1, d), lambda i: (0, 0)), pl.BlockSpec((d, N_COLS), lambda i: (0, 0))],
        out_specs=[row(w) for w in widths] + [row(LANES)],
        out_shape=out_shape,
        compiler_params=pltpu.CompilerParams(dimension_semantics=("parallel",), vmem_limit_bytes=VMEM_LIMIT),
        name="inproj",
    )(x2d, gain.reshape(1, d), w_perm)


def _embed_pair(w, n_tok):
    c = w.shape[1]
    w4 = w.reshape(n_tok, 1, HEAD_DIM, 1, c) * jnp.eye(NSA_KV_HEADS, dtype=w.dtype).reshape(1, 2, 1, 2, 1)
    return w4.reshape(n_tok * 2 * HEAD_DIM, 2 * c)


def _gelu_tanh(x):
    return 0.5 * x * (1.0 + jnp.tanh(math.sqrt(2.0 / math.pi) * (x + 0.044715 * (x * x * x))))


def _compress_kernel(ck_ref, cv_ref, posk_ref, posv_ref, wk1a, wk1b, wk2, wv1a, wv1b, wv2,
                     kout_ref, vout_ref, shift_ref):
    ncp = ck_ref.shape[1]
    for c_ref, pos_ref, w1a, w1b, w2, out_ref in ((ck_ref, posk_ref, wk1a, wk1b, wk2, kout_ref),
                                                  (cv_ref, posv_ref, wv1a, wv1b, wv2, vout_ref)):
        c = c_ref[0].astype(F32)
        first = _dot((c + pos_ref[0:1, :]).astype(BF16), w1a[...])
        second = _dot((c + pos_ref[1:2, :]).astype(BF16), w1b[...])
        shift_ref[0:ncp, :] = second
        shift_ref[ncp:ncp + 8, :] = jnp.zeros((8, second.shape[1]), F32)
        hidden = _gelu_tanh(first + shift_ref[1:ncp + 1, :])
        out_ref[0] = _dot(hidden.astype(BF16), w2[...]).astype(BF16)


def _compress(kc, vc, pos_k, pos_v, k_w1, k_w2, v_w1, v_w2):
    b, t, _ = kc.shape
    ncp = t // CMP_STRIDE
    half = CMP_STRIDE * HEAD_DIM
    wide = CMP_STRIDE * LANES

    def prep(w1, w2, pos):
        pos_pair = jnp.broadcast_to(pos.reshape(2, CMP_STRIDE, 1, HEAD_DIM), (2, CMP_STRIDE, 2, HEAD_DIM))
        return (_embed_pair(w1[:half], CMP_STRIDE).astype(BF16), _embed_pair(w1[half:], CMP_STRIDE).astype(BF16),
                jnp.kron(jnp.eye(NSA_KV_HEADS, dtype=w2.dtype), w2).astype(BF16),
                pos_pair.reshape(2, wide).astype(F32))

    wk1a, wk1b, wk2, posk = prep(k_w1, k_w2, pos_k)
    wv1a, wv1b, wv2, posv = prep(v_w1, v_w2, pos_v)
    full = lambda a: pl.BlockSpec(a.shape, lambda i: (0,) * a.ndim)
    tok = pl.BlockSpec((1, ncp, wide), lambda i: (i, 0, 0))
    out = pl.BlockSpec((1, ncp, LANES), lambda i: (i, 0, 0))
    return pl.pallas_call(
        _compress_kernel,
        grid=(b,),
        in_specs=[tok, tok, full(posk), full(posv), full(wk1a), full(wk1b), full(wk2), full(wv1a), full(wv1b), full(wv2)],
        out_specs=[out, out],
        out_shape=[jax.ShapeDtypeStruct((b, ncp, LANES), BF16)] * 2,
        scratch_shapes=[pltpu.VMEM((ncp + 8, 2 * CMP_HIDDEN), F32)],
        compiler_params=pltpu.CompilerParams(dimension_semantics=("parallel",), vmem_limit_bytes=VMEM_LIMIT),
        name="compress",
    )(kc.reshape(b, ncp, wide), vc.reshape(b, ncp, wide), posk, posv, wk1a, wk1b, wk2, wv1a, wv1b, wv2)


def _pair_masks(rows):
    lane = lax.broadcasted_iota(jnp.int32, (rows, LANES), 1)
    return lane < HEAD_DIM


A_SUPER = QB * DIL_PATTERNS[-1][1]


def _mixer_a_kernel(q_ref, kp_ref, kc_ref, vp_ref, vc_ref, bias_ref, out_ref, qf_ref, kf_ref, vf_ref, o_ref, lse_ref):
    first = pl.program_id(1) == 0
    n_pairs = DIL_HEADS // 2
    for p in range(n_pairs):
        cs = slice(p * LANES, (p + 1) * LANES)
        qf_ref[p] = q_ref[0, :, cs].astype(F32)
        kf_ref[p, 0:A_SUPER, :] = kp_ref[0, :, cs].astype(F32)
        kf_ref[p, A_SUPER:2 * A_SUPER, :] = kc_ref[0, :, cs].astype(F32)
        vf_ref[p, 0:A_SUPER, :] = vp_ref[0, :, cs].astype(F32)
        vf_ref[p, A_SUPER:2 * A_SUPER, :] = vc_ref[0, :, cs].astype(F32)
    low = _pair_masks(QB)
    in_prev = lax.broadcasted_iota(jnp.int32, (QB, 2 * QB), 1) < QB
    zero = jnp.zeros((QB, LANES), BF16)

    def chunk(idx, dil, q_base, k_base, q_span, k_span, off, at_start):
        rows = lambda size: pl.ds(off, size) if dil == 1 else pl.ds(off, size, stride=dil)
        q_win = lambda ref: ref.at[pl.ds(pl.multiple_of(q_base, 8), q_span), :]
        k_win = lambda ref: ref.at[pl.ds(pl.multiple_of(k_base, 8), k_span), :]
        q_rows = rows(QB)
        prev_mask = jnp.where(jnp.logical_and(in_prev, jnp.logical_and(first, at_start)), NEG, 0.0)
        for p in range(n_pairs):
            q = q_win(qf_ref.at[p])[q_rows, :].astype(BF16)
            keys = k_win(kf_ref.at[p])[rows(2 * QB), :].astype(BF16)
            vals = k_win(vf_ref.at[p])[rows(2 * QB), :].astype(BF16)
            o_win, lse_win = q_win(o_ref.at[p]), q_win(lse_ref.at[p])
            lhs = jnp.concatenate([jnp.where(low, q, zero), jnp.where(low, zero, q)], axis=0)
            bias = jnp.concatenate([bias_ref[idx, 2 * p] + prev_mask, bias_ref[idx, 2 * p + 1] + prev_mask], axis=0)
            s = _nt_dot(lhs, keys) + bias
            m = jnp.max(s, axis=1, keepdims=True)
            e = jnp.exp(s - m)
            l = jnp.sum(e, axis=1, keepdims=True)
            pv = _dot(e.astype(BF16), vals) * (1.0 / l)
            lse = m + jnp.log(l)
            o_new = jnp.where(low, pv[:QB], pv[QB:])
            l_new = jnp.where(low, jnp.broadcast_to(lse[:QB], (QB, LANES)), jnp.broadcast_to(lse[QB:], (QB, LANES)))
            if idx > 0:
                o_old, l_old = o_win[q_rows, :], lse_win[q_rows, :]
                mx = jnp.maximum(l_old, l_new)
                w_old, w_new = jnp.exp(l_old - mx), jnp.exp(l_new - mx)
                tot = w_old + w_new
                o_new = (w_old * o_old + w_new * o_new) * (1.0 / tot)
                l_new = mx + jnp.log(tot)
            o_win[q_rows, :] = o_new
            lse_win[q_rows, :] = l_new

    def loop(n, body):
        lax.fori_loop(0, n, lambda i, carry: (body(i), carry)[1], 0, unroll=2)

    for idx, (_, dil) in enumerate(DIL_PATTERNS):
        span = QB * dil
        n_chunks = A_SUPER // span
        if n_chunks > 1:
            for r in range(dil):
                loop(n_chunks, lambda c, idx=idx, dil=dil, span=span, r=r:
                     chunk(idx, dil, span * c, A_SUPER + span * (c - 1), span, 2 * span, r, c == 0))
        else:
            for r in range(8):
                loop(dil // 8, lambda hi, idx=idx, dil=dil, span=span, r=r:
                     chunk(idx, dil, 8 * hi, 8 * hi, span - 8, 2 * span - 8, r, True))
    for p in range(n_pairs):
        out_ref[0, :, p * LANES:(p + 1) * LANES] = o_ref[p].astype(BF16)


def _mixer_a(qa, ka, va, bias):
    b, t, _ = qa.shape
    cur = pl.BlockSpec((1, A_SUPER, DA), lambda bi, i: (bi, i, 0))
    prev = pl.BlockSpec((1, A_SUPER, DA), lambda bi, i: (bi, jnp.maximum(i - 1, 0), 0))
    return pl.pallas_call(
        _mixer_a_kernel,
        grid=(b, t // A_SUPER),
        in_specs=[cur, prev, cur, prev, cur, pl.BlockSpec(bias.shape, lambda bi, i: (0, 0, 0, 0))],
        out_specs=cur,
        out_shape=jax.ShapeDtypeStruct((b, t, DA), BF16),
        scratch_shapes=[pltpu.VMEM((DIL_HEADS // 2, rows, LANES), F32) for rows in (A_SUPER, 2 * A_SUPER, 2 * A_SUPER, A_SUPER, A_SUPER)],
        compiler_params=pltpu.CompilerParams(dimension_semantics=("parallel", "parallel"), vmem_limit_bytes=VMEM_LIMIT),
        name="mixer_a",
    )(qa, ka, ka, va, va, bias)


def _dil_bias(f_a, dil):
    steps = DIL_PATTERNS[0][0]
    g = f_a[:, 0:dil * steps + 1:dil]
    lo, hi = QB - (2 * QB - 1), QB + QB
    vals = jnp.concatenate([jnp.full((DIL_HEADS, -lo), NEG, F32), g, jnp.full((DIL_HEADS, hi - steps - 1), NEG, F32)], axis=1)
    return _toeplitz_of(vals, lo, QB, 2 * QB)


CMP_TILE_KEYS = LANES
CMP_TILE_SPAN = CMP_TILE_KEYS * CMP_STRIDE // QB
CMP_CONST_DELTA = 28


def _cmp_bias(f_b):
    per = QB // CMP_STRIDE
    n_rows = per * (CMP_CONST_DELTA + 1)
    m_lo, m_hi = -(CMP_TILE_KEYS - 1), n_rows
    f_b = f_b.astype(BF16)
    base = _extend(f_b, CMP_STRIDE * m_lo - (CMP_BLOCK - 1), CMP_STRIDE * m_hi - (CMP_BLOCK - 1))
    g = base.reshape(NSA_HEADS, m_hi - m_lo, CMP_STRIDE).transpose(0, 2, 1)
    t = _toeplitz_of(g, m_lo, n_rows, CMP_TILE_KEYS)
    t = t.reshape(NSA_HEADS, CMP_STRIDE, CMP_CONST_DELTA + 1, per, CMP_TILE_KEYS).transpose(2, 0, 3, 1, 4)
    t = t.reshape(CMP_CONST_DELTA + 1, NSA_HEADS, QB, CMP_TILE_KEYS)
    return jnp.concatenate([jnp.full((1,) + t.shape[1:], NEG, t.dtype), t], axis=0)


def _overlap_matrix_t(ncp, n_sel_pad):
    n = np.arange(ncp)[None, :] * CMP_STRIDE
    s = np.arange(n_sel_pad)[:, None] * SEL_BLOCK
    ov = np.clip(np.minimum(n + CMP_BLOCK, s + SEL_BLOCK) - np.maximum(n, s), 0, None) / CMP_BLOCK
    return jnp.asarray(ov, BF16)


def _gate_tile(gates_ref, branch, g):
    c = branch * NSA_HEADS + g * 2
    low = _pair_masks(QB)
    return jnp.where(low, jnp.broadcast_to(gates_ref[0, :, c:c + 1], (QB, LANES)),
                     jnp.broadcast_to(gates_ref[0, :, c + 1:c + 2], (QB, LANES)))


def _masked_q(qb_ref, g, kv):
    q = qb_ref[0, :, g * LANES:(g + 1) * LANES]
    low = _pair_masks(QB)
    keep = low if kv == 0 else jnp.logical_not(low)
    return jnp.where(keep, q, jnp.zeros_like(q))


def _cmp_kernel(n_tiles, qb_ref, kcmp_ref, vcmp_ref, gates_ref, ov_ref, *rest):
    tbl_refs, (oc_ref, sel_ref, q_ref, s_ref, p_ref, pv_ref) = rest[:n_tiles], rest[n_tiles:]
    qblk = pl.program_id(1)
    t0 = qblk * QB
    low = _pair_masks(QB)
    for kv in range(NSA_KV_HEADS):
        for g in range(NSA_GROUP):
            r = kv * NSA_GROUP + g
            q_ref[r * QB:(r + 1) * QB, :] = _masked_q(qb_ref, g, kv)

    def attend(n_vis):
        kc = n_vis * CMP_TILE_KEYS
        n_blk = n_vis * CMP_TILE_SPAN * QB // SEL_BLOCK
        s_ref[:, 0:kc] = _nt_dot(q_ref[...], kcmp_ref[0, 0:kc, :])
        blk = lax.broadcasted_iota(jnp.int32, (n_blk, QB), 0)
        cur = (t0 + lax.broadcasted_iota(jnp.int32, (n_blk, QB), 1)) // SEL_BLOCK
        blk_f = blk.astype(F32)
        forced = (blk == cur) | (blk == cur - 1) | (blk == 0)
        causal = blk <= cur
        scores = []
        for kv in range(NSA_KV_HEADS):
            psum = jnp.zeros((QB, kc), F32)
            for g in range(NSA_GROUP):
                r = kv * NSA_GROUP + g
                rows = slice(r * QB, (r + 1) * QB)
                s = s_ref[rows, 0:kc] + jnp.concatenate([tbl_refs[c][0, r].astype(F32) for c in range(n_vis)], axis=1)
                m = jnp.max(s, axis=1, keepdims=True)
                e = jnp.exp2(s - m)
                den = jnp.sum(e, axis=1, keepdims=True)
                p = jnp.where(m > 0.5 * NEG, e * (1.0 / den), 0.0)
                psum = psum + p
                p_ref[rows, 0:kc] = p.astype(BF16)
            hi = psum.astype(BF16)
            lo = (psum - hi.astype(F32)).astype(BF16)
            ov_t = ov_ref[0:n_blk, 0:kc]
            imp_t = _nt_dot(ov_t, hi) + _nt_dot(ov_t, lo)
            scores.append(jnp.where(forced, -jnp.inf, jnp.where(causal, imp_t, -1.0)))
        pv_ref[...] = _dot(p_ref[:, 0:kc], vcmp_ref[0, 0:kc, :])

        def pick(_, carry):
            new = []
            for val, sel in carry:
                mx = jnp.max(val, axis=0, keepdims=True)
                idx = jnp.min(jnp.where(val == mx, blk_f, float(n_blk)), axis=0, keepdims=True)
                hit = blk_f == idx
                new.append((jnp.where(hit, -jnp.inf, val), jnp.where(hit, 1.0, sel)))
            return tuple(new)

        taken = jnp.where(forced, 1.0, 0.0)
        picked = lax.fori_loop(0, SEL_TOPK - N_FORCED, pick, tuple((v, taken) for v in scores))
        eye = jnp.where(lax.broadcasted_iota(jnp.int32, (QB, QB), 0) == lax.broadcasted_iota(jnp.int32, (QB, QB), 1),
                        1.0, 0.0).astype(BF16)
        for kv in range(NSA_KV_HEADS):
            sel_t = jnp.where(causal, picked[kv][1], 0.0).astype(BF16)
            sel_ref[0, kv, :, 0:n_blk] = _nt_dot(eye, sel_t).astype(BF16)
            if n_blk < LANES:
                sel_ref[0, kv, :, n_blk:LANES] = jnp.zeros((QB, LANES - n_blk), BF16)

    n_vis = qblk // CMP_TILE_SPAN + 1
    for w in range(1, n_tiles + 1):
        pl.when(n_vis == w)(functools.partial(attend, w))

    for g in range(NSA_GROUP):
        o0, o1 = pv_ref[g * QB:(g + 1) * QB, :], pv_ref[(NSA_GROUP + g) * QB:(NSA_GROUP + g + 1) * QB, :]
        oc_ref[0, :, g * LANES:(g + 1) * LANES] = (jnp.where(low, o0, o1) * _gate_tile(gates_ref, 0, g)).astype(BF16)


def _compressed_branch(qb, kcmp, vcmp, gates, f_b):
    b, t, _ = qb.shape
    ncp = kcmp.shape[1]
    n_tiles = ncp // CMP_TILE_KEYS
    tbl = _cmp_bias(f_b)
    ov = _overlap_matrix_t(ncp, LANES)

    def tbl_spec(c):
        return pl.BlockSpec((1, NSA_HEADS, QB, CMP_TILE_KEYS),
                            lambda bi, i: (jnp.clip(i - CMP_TILE_SPAN * c, -1, CMP_CONST_DELTA) + 1, 0, 0, 0))

    blockq = lambda w: pl.BlockSpec((1, QB, w), lambda bi, i: (bi, i, 0))
    batch = lambda a: pl.BlockSpec((1,) + a.shape[1:], lambda bi, i: (bi, 0, 0))
    return pl.pallas_call(
        functools.partial(_cmp_kernel, n_tiles),
        grid=(b, t // QB),
        in_specs=[blockq(DB), batch(kcmp), batch(vcmp), blockq(LANES), pl.BlockSpec(ov.shape, lambda bi, i: (0, 0))]
                 + [tbl_spec(c) for c in range(n_tiles)],
        out_specs=[blockq(DB), pl.BlockSpec((1, NSA_KV_HEADS, QB, LANES), lambda bi, i: (bi, 0, i, 0))],
        out_shape=[jax.ShapeDtypeStruct((b, t, DB), BF16), jax.ShapeDtypeStruct((b, NSA_KV_HEADS, t, LANES), BF16)],
        scratch_shapes=[pltpu.VMEM((N_ROWGROUPS * QB, LANES), BF16), pltpu.VMEM((N_ROWGROUPS * QB, ncp), F32),
                        pltpu.VMEM((N_ROWGROUPS * QB, ncp), BF16), pltpu.VMEM((N_ROWGROUPS * QB, LANES), F32)],
        compiler_params=pltpu.CompilerParams(dimension_semantics=("parallel", "parallel"), vmem_limit_bytes=VMEM_LIMIT),
        name="nsa_compressed",
    )(qb, kcmp, vcmp, gates, ov, *([tbl] * n_tiles))


SEL_NEAR = 13


def _sel_bias(f_b):
    n_off = SEL_NEAR + 1
    cols = QB * n_off
    rel = f_b - f_b[:, BIAS_LEN - 1:]
    lo = -(QB - 1)
    big = _toeplitz_of(_extend(rel, lo, lo + QB + cols - 1), lo, QB, cols)
    tiles = jnp.flip(big.reshape(NSA_HEADS, QB, n_off, QB).transpose(2, 0, 1, 3), axis=0)
    return jnp.concatenate([jnp.zeros((1,) + tiles.shape[1:], F32), tiles], axis=0)


def _pair_ratio(acc0, acc1):
    low = _pair_masks(acc0.shape[0])
    den = pltpu.roll(jnp.where(low, acc1, acc0), HEAD_DIM, axis=1)
    return jnp.where(low, acc0, acc1) * (1.0 / den)


def _sel_kernel(qb_ref, sel_ref, gates_ref, cfar_ref, ks_ref, vs0_ref, vs1_ref, tbl_ref, out_ref,
                qaug_ref, s_ref, s1_ref, acc_ref, m_ref, alpha_ref):
    qblk = pl.program_id(1)
    for kv in range(NSA_KV_HEADS):
        unchosen = jnp.where(sel_ref[0, kv].astype(F32) > 0.0, 0.0, NEG)
        for g in range(NSA_GROUP):
            r = kv * NSA_GROUP + g
            rows = slice(r * QB, (r + 1) * QB)
            qaug_ref[rows, 0:LANES] = _masked_q(qb_ref, g, kv)
            qaug_ref[rows, LANES:2 * LANES] = (unchosen + cfar_ref[r:r + 1, :]).astype(BF16)
    acc_ref[...] = jnp.zeros_like(acc_ref)
    m_ref[...] = jnp.full_like(m_ref, NEG)

    last_tile = ks_ref.shape[2] // SEL_KT - 1

    def scores(j, dst_ref):
        start = pl.multiple_of(jnp.minimum(j, last_tile) * SEL_KT, SEL_KT)
        dst_ref[...] = _dot(qaug_ref[...], ks_ref[0, :, pl.ds(start, SEL_KT)])

    def consume(j, src_ref, near):
        start = pl.multiple_of(j * SEL_KT, SEL_KT)
        for r in range(N_ROWGROUPS):
            rows = slice(r * QB, (r + 1) * QB)
            s = src_ref[rows, :]
            if near:
                e1 = jnp.clip(qblk - 2 * j + 1, 0, SEL_NEAR + 1)
                e2 = jnp.clip(qblk - 2 * j, 0, SEL_NEAR + 1)
                s = s + jnp.concatenate([tbl_ref[e1, r], tbl_ref[e2, r]], axis=1)
                src_ref[rows, :] = s
            m_old = m_ref[rows, :]
            m_new = jnp.maximum(m_old, jnp.max(s, axis=1, keepdims=True))
            alpha_ref[rows, :] = jnp.exp2(m_old - m_new)
            m_ref[rows, :] = m_new
        for r in range(N_ROWGROUPS):
            rows = slice(r * QB, (r + 1) * QB)
            vals = (vs0_ref if r < NSA_GROUP else vs1_ref)[0, pl.ds(start, SEL_KT), :]
            m_new = m_ref[rows, :]
            p = jnp.exp2(src_ref[rows, :] - jnp.concatenate([m_new, m_new], axis=1))
            acc_ref[rows, :] = alpha_ref[rows, :] * acc_ref[rows, :] + _dot(p.astype(BF16), vals)

    def tile_pair(i, near):
        scores(2 * i + 1, s1_ref)
        consume(2 * i, s_ref, near)
        scores(2 * i + 2, s_ref)
        consume(2 * i + 1, s1_ref, near)

    n_pairs = ((qblk + 2) // 2 + 1) // 2
    n_far = jnp.maximum((qblk - (SEL_NEAR - 1)) // 2, 0) // 2
    scores(0, s_ref)
    lax.fori_loop(0, n_far, lambda i, c: (tile_pair(i, False), c)[1], 0)
    lax.fori_loop(n_far, n_pairs, lambda i, c: (tile_pair(i, True), c)[1], 0)
    for g in range(NSA_GROUP):
        ratio = _pair_ratio(acc_ref[g * QB:(g + 1) * QB, :], acc_ref[(NSA_GROUP + g) * QB:(NSA_GROUP + g + 1) * QB, :])
        out_ref[0, :, g * LANES:(g + 1) * LANES] = (ratio * _gate_tile(gates_ref, 1, g)).astype(BF16)


def _selected_branch(qb, sel, gates, ksaug, vs0aug, vs1aug, f_b):
    b, t, _ = qb.shape
    tbl = _sel_bias(f_b)
    ks_t = jnp.swapaxes(ksaug, 1, 2)
    cfar = jnp.broadcast_to(f_b[:, BIAS_LEN - 1:], (NSA_HEADS, LANES))
    cfar = jnp.pad(cfar, ((0, 16 - NSA_HEADS), (0, 0)))
    blockq = lambda w: pl.BlockSpec((1, QB, w), lambda bi, i: (bi, i, 0))
    batch = lambda a: pl.BlockSpec((1,) + a.shape[1:], lambda bi, i: (bi, 0, 0))
    rows = N_ROWGROUPS * QB
    return pl.pallas_call(
        _sel_kernel,
        grid=(b, t // QB),
        in_specs=[blockq(DB), pl.BlockSpec((1, NSA_KV_HEADS, QB, LANES), lambda bi, i: (bi, 0, i, 0)), blockq(LANES),
                  pl.BlockSpec(cfar.shape, lambda bi, i: (0, 0)), batch(ks_t), batch(vs0aug), batch(vs1aug),
                  pl.BlockSpec(tbl.shape, lambda bi, i: (0, 0, 0, 0))],
        out_specs=blockq(DB),
        out_shape=jax.ShapeDtypeStruct((b, t, DB), BF16),
        scratch_shapes=[pltpu.VMEM((rows, 2 * LANES), BF16), pltpu.VMEM((rows, SEL_KT), F32), pltpu.VMEM((rows, SEL_KT), F32),
                        pltpu.VMEM((rows, LANES), F32), pltpu.VMEM((rows, LANES), F32), pltpu.VMEM((rows, LANES), F32)],
        compiler_params=pltpu.CompilerParams(dimension_semantics=("parallel", "parallel"), vmem_limit_bytes=VMEM_LIMIT),
        name="nsa_selected",
    )(qb, sel, gates, cfar, ks_t, vs0aug, vs1aug, tbl)


WIN_KEYS = WIN + QB


def _win_bias(f_b):
    lo = WIN - (WIN_KEYS - 1)
    vals = _extend(f_b[:, :WIN], lo, WIN)
    vals = jnp.concatenate([vals, jnp.full((NSA_HEADS, lo + QB + WIN_KEYS - 1 - WIN), NEG, F32)], axis=1)
    return _toeplitz_of(vals, lo, QB, WIN_KEYS)


def _win_kernel(qb_ref, gates_ref, kw_ref, vw0_ref, vw1_ref, tbl_ref, out_ref, q_ref, s_ref):
    qblk = pl.program_id(1)
    for kv in range(NSA_KV_HEADS):
        for g in range(NSA_GROUP):
            r = kv * NSA_GROUP + g
            q_ref[r * QB:(r + 1) * QB, :] = _masked_q(qb_ref, g, kv)
    start = pl.multiple_of(qblk * QB, QB)
    s_ref[...] = _nt_dot(q_ref[...], kw_ref[0, pl.ds(start, WIN_KEYS), :])
    col = lax.broadcasted_iota(jnp.int32, (1, WIN_KEYS), 1)
    pad_mask = jnp.where(col + qblk * QB >= WIN, 0.0, NEG)
    outs = []
    for r in range(N_ROWGROUPS):
        vals = (vw0_ref if r < NSA_GROUP else vw1_ref)[0, pl.ds(start, WIN_KEYS), :]
        s = s_ref[r * QB:(r + 1) * QB, :] + tbl_ref[r] + pad_mask
        e = jnp.exp2(s - jnp.max(s, axis=1, keepdims=True))
        outs.append(_dot(e.astype(BF16), vals))
    for g in range(NSA_GROUP):
        out_ref[0, :, g * LANES:(g + 1) * LANES] = (_pair_ratio(outs[g], outs[NSA_GROUP + g])
                                                    * _gate_tile(gates_ref, 2, g)).astype(BF16)


def _window_branch(qb, gates, kw, vw0aug, vw1aug, f_b):
    b, t, _ = qb.shape
    tbl = _win_bias(f_b)
    pad_front = lambda a: jnp.pad(a, ((0, 0), (WIN, 0), (0, 0)))
    kw_pad, vw0_pad, vw1_pad = pad_front(kw), pad_front(vw0aug), pad_front(vw1aug)
    blockq = lambda w: pl.BlockSpec((1, QB, w), lambda bi, i: (bi, i, 0))
    batch = lambda a: pl.BlockSpec((1,) + a.shape[1:], lambda bi, i: (bi, 0, 0))
    rows = N_ROWGROUPS * QB
    return pl.pallas_call(
        _win_kernel,
        grid=(b, t // QB),
        in_specs=[blockq(DB), blockq(LANES), batch(kw_pad), batch(vw0_pad), batch(vw1_pad),pl.BlockSpec(tbl.shape, lambda bi, i: (0, 0, 0))],
        out_specs=blockq(DB),
        out_shape=jax.ShapeDtypeStruct((b, t, DB), BF16),
        scratch_shapes=[pltpu.VMEM((rows, LANES), BF16), pltpu.VMEM((rows, WIN_KEYS), F32)],
        compiler_params=pltpu.CompilerParams(dimension_semantics=("parallel", "parallel"), vmem_limit_bytes=VMEM_LIMIT),
        name="nsa_window",
    )(qb, gates, kw_pad, vw0_pad, vw1_pad, tbl)


OUT_TM = 256
C_GROUP = N_EXPERTS


def _outproj_kernel(x_ref, oa_ref, oc_ref, os_ref, ow_ref,
                    wout_ref, g_ref, wr_ref, br_ref, h_ref, hn_ref, comb_ref):
    ob = oc_ref[...].astype(F32) + os_ref[...].astype(F32) + ow_ref[...].astype(F32)
    y = _dot(oa_ref[...], wout_ref[0:DA, :]) + _dot(ob.astype(BF16), wout_ref[DA:DA + DB, :])
    h = x_ref[...] + y
    h_ref[...] = h
    hn = h * lax.rsqrt(jnp.mean(h * h, axis=-1, keepdims=True) + EPS) * g_ref[...]
    hn_hi = hn.astype(BF16)
    hn_ref[...] = hn_hi
    hn_lo = (hn - hn_hi.astype(F32)).astype(BF16)
    both = _dot(hn_hi, wr_ref[...])
    logits = both[:, 0:LANES] + both[:, LANES:2 * LANES] + _dot(hn_lo, wr_ref[:, 0:LANES]) + br_ref[...]
    lane = lax.broadcasted_iota(jnp.int32, logits.shape, 1)
    lane_f = lane.astype(F32)
    big = float(LANES)
    gl = jnp.where((lane >= C_GROUP) & (lane < C_GROUP + N_GROUPS), logits, -jnp.inf)
    gmax = jnp.max(gl, axis=1, keepdims=True)
    gidx = jnp.min(jnp.where(gl == gmax, lane_f, big), axis=1, keepdims=True) - C_GROUP
    gprob = 1.0 / jnp.sum(jnp.exp(gl - gmax), axis=1, keepdims=True)
    grp_of_lane = (lane // EXPERTS_PER_GROUP).astype(F32)
    el = jnp.where((lane < N_EXPERTS) & (grp_of_lane == gidx), logits, -jnp.inf)
    v1 = jnp.max(el, axis=1, keepdims=True)
    i1 = jnp.min(jnp.where(el == v1, lane_f, big), axis=1, keepdims=True)
    el2 = jnp.where(lane_f == i1, -jnp.inf, el)
    v2 = jnp.max(el2, axis=1, keepdims=True)
    i2 = jnp.min(jnp.where(el2 == v2, lane_f, big), axis=1, keepdims=True)
    e2 = jnp.exp(v2 - v1)
    p1 = 1.0 / (1.0 + e2)
    comb_ref[...] = (gprob * (jnp.where(lane_f == i1, p1, 0.0) + jnp.where(lane_f == i2, e2 * p1, 0.0))
                     + jnp.where(lane == C_GROUP, gidx, 0.0))


def _outproj(x2d, o_a, b_parts, w_out_perm, gain, w_router, b_router):
    n, d = x2d.shape
    row = lambda w: pl.BlockSpec((OUT_TM, w), lambda i: (i, 0))
    full = lambda a: pl.BlockSpec(a.shape, lambda i: (0, 0))
    return pl.pallas_call(
        _outproj_kernel,
        grid=(n // OUT_TM,),
        in_specs=[row(d), row(DA)] + [row(DB)] * 3 + [full(w_out_perm), pl.BlockSpec((1, d), lambda i: (0, 0)),
                                                      full(w_router), full(b_router)],
        out_specs=[row(d), row(d), row(LANES)],
        out_shape=[jax.ShapeDtypeStruct((n, d), F32), jax.ShapeDtypeStruct((n, d), BF16),
                   jax.ShapeDtypeStruct((n, LANES), F32)],
        compiler_params=pltpu.CompilerParams(dimension_semantics=("parallel",), vmem_limit_bytes=VMEM_LIMIT),
        name="outproj_router",
    )(x2d, o_a, *b_parts, w_out_perm, gain.reshape(1, d), w_router, b_router)


MOE_TM = 1024


MOE_SUB = 128
MOE_FINAL_TM = 512


def _moe_kernel(hn_ref, comb_ref, wg_ref, wu_ref, wd_ref, y_ref, perm_ref, hs_ref, cs_ref, ys_ref, start_ref, nsub_ref):
    grp = pl.program_id(1)
    tm, d = hn_ref.shape
    n_pad = perm_ref.shape[0]

    @pl.when(grp == 0)
    def _sort():
        comb = comb_ref[...]
        lane_f = lax.broadcasted_iota(jnp.int32, (tm, LANES), 1).astype(F32)
        onehot = jnp.where(lane_f == comb[:, C_GROUP:C_GROUP + 1], 1.0, 0.0).astype(BF16)
        eye8 = jnp.where(lax.broadcasted_iota(jnp.int32, (8, LANES), 0) == lax.broadcasted_iota(jnp.int32, (8, LANES), 1),
                         1.0, 0.0).astype(BF16)
        onehot_t = _nt_dot(eye8, onehot)
        upper = jnp.where(lax.broadcasted_iota(jnp.int32, (tm, tm), 0) <= lax.broadcasted_iota(jnp.int32, (tm, tm), 1),
                          1.0, 0.0).astype(BF16)
        cum_t = _dot(onehot_t.astype(BF16), upper)
        start = jnp.zeros((1, 1), F32)
        pos_t = jnp.zeros((1, tm), F32)
        for k in range(N_GROUPS):
            padded = jnp.ceil(cum_t[k:k + 1, tm - 1:tm] * (1.0 / MOE_SUB)) * MOE_SUB
            pos_t = pos_t + onehot_t[k:k + 1, :] * (start + cum_t[k:k + 1, :] - 1.0)
            start_ref[k] = start[0, 0].astype(jnp.int32)
            nsub_ref[k] = (padded[0, 0] * (1.0 / MOE_SUB)).astype(jnp.int32)
            start = start + padded
        perm = jnp.where(lax.broadcasted_iota(jnp.int32, (n_pad, tm), 0) == pos_t.astype(jnp.int32), 1.0, 0.0).astype(BF16)
        perm_ref[...] = perm
        hs_ref[...] = _dot(perm, hn_ref[...]).astype(BF16)
        c_hi = comb.astype(BF16)
        rest = comb - c_hi.astype(F32)
        c_mid = rest.astype(BF16)
        c_lo = (rest - c_mid.astype(F32)).astype(BF16)
        cs_ref[...] = _dot(perm, c_hi) + _dot(perm, c_mid) + _dot(perm, c_lo)
        ys_ref[...] = jnp.zeros_like(ys_ref)

    lane = lax.broadcasted_iota(jnp.int32, (MOE_SUB, LANES), 1)

    def segment(s, carry):
        rows = pl.ds(pl.multiple_of(start_ref[grp] + s * MOE_SUB, MOE_SUB), MOE_SUB)
        x = hs_ref[rows, :]
        weights = cs_ref[rows, :]
        acc = jnp.zeros((MOE_SUB, d), F32)
        for j in range(EXPERTS_PER_GROUP):
            gate = _dot(x, wg_ref[j])
            up = _dot(x, wu_ref[j])
            w = jnp.sum(jnp.where(lane == grp * EXPERTS_PER_GROUP + j, weights, 0.0), axis=1, keepdims=True)
            acc = acc + _dot((gate * jax.nn.sigmoid(gate) * up * w).astype(BF16), wd_ref[j])
        ys_ref[rows, :] = acc.astype(BF16)
        return carry

    lax.fori_loop(0, nsub_ref[grp], segment, 0)

    @pl.when(grp == pl.num_programs(1) - 1)
    def _unsort():
        y_ref[...] = lax.dot_general(perm_ref[...], ys_ref[...], (((0,), (0,)), ((), ())),
                                     preferred_element_type=F32).astype(BF16)


def _final_kernel(h_ref, y_ref, g_ref, out_ref):
    y = h_ref[...] + y_ref[...].astype(F32)
    out_ref[...] = y * lax.rsqrt(jnp.mean(y * y, axis=-1, keepdims=True) + EPS) * g_ref[...]


def _moe(h, hn, comb, w_gate, w_up, w_down, gain):
    n, d = h.shape
    tm = min(MOE_TM, n)
    n_pad = tm + (N_GROUPS - 1) * MOE_SUB
    row = lambda w: pl.BlockSpec((tm, w), lambda i, g: (i, 0))
    group_w = lambda a: pl.BlockSpec((EXPERTS_PER_GROUP,) + a.shape[1:], lambda i, g: (g, 0, 0))
    y = pl.pallas_call(
        _moe_kernel,
        grid=(n // tm, N_GROUPS),
        in_specs=[row(d), row(LANES), group_w(w_gate), group_w(w_up), group_w(w_down)],
        out_specs=row(d),
        out_shape=jax.ShapeDtypeStruct((n, d), BF16),
        scratch_shapes=[pltpu.VMEM((n_pad, tm), BF16), pltpu.VMEM((n_pad, d), BF16), pltpu.VMEM((n_pad, LANES), F32),
                        pltpu.VMEM((n_pad, d), BF16), pltpu.SMEM((N_GROUPS,), jnp.int32), pltpu.SMEM((N_GROUPS,), jnp.int32)],
        compiler_params=pltpu.CompilerParams(dimension_semantics=("parallel", "arbitrary"), vmem_limit_bytes=VMEM_LIMIT),
        name="moe_experts",
    )(hn, comb, w_gate, w_up, w_down)
    rowf = lambda w: pl.BlockSpec((MOE_FINAL_TM, w), lambda i: (i, 0))
    return pl.pallas_call(
        _final_kernel,
        grid=(n // MOE_FINAL_TM,),
        in_specs=[rowf(d), rowf(d), pl.BlockSpec((1, d), lambda i: (0, 0))],
        out_specs=rowf(d),
        out_shape=jax.ShapeDtypeStruct((n, d), F32),
        compiler_params=pltpu.CompilerParams(dimension_semantics=("parallel",), vmem_limit_bytes=VMEM_LIMIT),
        name="residual_final_norm",
    )(h, y, gain.reshape(1, d))


def _permute_w_out(w_out):
    d = w_out.shape[1]
    wb = w_out[DA:].reshape(NSA_KV_HEADS, NSA_GROUP, HEAD_DIM, d).transpose(1, 0, 2, 3).reshape(DB, d)
    return jnp.concatenate([w_out[:DA], wb], axis=0).astype(BF16)


def _router_weights(w_group, b_group, w_expert, b_expert):
    d = w_group.shape[0]
    w = jnp.concatenate([w_expert.reshape(d, N_EXPERTS), w_group], axis=1)
    b = jnp.concatenate([b_expert.reshape(N_EXPERTS), b_group])
    pad = LANES - w.shape[1]
    w = jnp.pad(w, ((0, 0), (0, pad))).astype(F32)
    w_hi = w.astype(BF16)
    w_lo = (w - w_hi.astype(F32)).astype(BF16)
    return jnp.concatenate([w_hi, w_lo], axis=1), jnp.pad(b, (0, pad)).reshape(1, LANES).astype(F32)


def _layer(h, rel_bias, norm_mix, w_in, w_out, cmp_pos_k, cmp_pos_v, cmp_k_w1, cmp_k_w2, cmp_v_w1, cmp_v_w2,
           norm_ffn, w_rg, b_rg, w_re, b_re, w_gate, w_up, w_down, out_gain):
    b, t, d = h.shape
    n = b * t
    assert t % (QB * DIL_PATTERNS[-1][1]) == 0 and t // SEL_BLOCK <= LANES and n % MOE_TM == 0
    x2d = h.reshape(n, d)
    seq = lambda a: a.reshape(b, t, a.shape[-1])
    qa, ka, va, qb, kc, vc, ksaug, vs0aug, vs1aug, kw, vw0aug, vw1aug, gates = map(
        seq, _inproj(x2d, norm_mix, _permute_w_in(w_in), t))
    f_a = _bias_1d(rel_bias[:, :DIL_HEADS])
    f_b = _bias_1d(rel_bias[:, DIL_HEADS:]) * LOG2E
    o_a = _mixer_a(qa, ka, va, jnp.stack([_dil_bias(f_a, dil) for _, dil in DIL_PATTERNS]))
    kcmp, vcmp = _compress(kc, vc, cmp_pos_k, cmp_pos_v, cmp_k_w1, cmp_k_w2, cmp_v_w1, cmp_v_w2)
    o_cmp, sel = _compressed_branch(qb, kcmp, vcmp, gates, f_b)
    o_sel = _selected_branch(qb, sel, gates, ksaug, vs0aug, vs1aug, f_b)
    o_win = _window_branch(qb, gates, kw, vw0aug, vw1aug, f_b)
    b_parts = [o.reshape(n, DB) for o in (o_cmp, o_sel, o_win)]
    w_router, b_router = _router_weights(w_rg, b_rg, w_re, b_re)
    h2, hn, comb = _outproj(x2d, o_a.reshape(n, DA), b_parts, _permute_w_out(w_out), norm_ffn, w_router, b_router)
    return _moe(h2, hn, comb, w_gate.astype(BF16), w_up.astype(BF16), w_down.astype(BF16), out_gain)


def kernel(x, rel_bias, norm_mix, w_in, w_out, cmp_pos_k, cmp_pos_v, cmp_k_w1, cmp_k_w2, cmp_v_w1, cmp_v_w2,
           norm_ffn, w_router_group, b_router_group, w_router_expert, b_router_expert, w_gate, w_up, w_down,
           norm_final):
    depth = norm_mix.shape[0]
    assert depth == 1, "the final RMSNorm is fused into the last layer's expert kernel"
    out = _layer(x, rel_bias, norm_mix[0], w_in[0], w_out[0], cmp_pos_k[0], cmp_pos_v[0], cmp_k_w1[0], cmp_k_w2[0],
                 cmp_v_w1[0], cmp_v_w2[0], norm_ffn[0], w_router_group[0], b_router_group[0], w_router_expert[0],
                 b_router_expert[0], w_gate[0], w_up[0], w_down[0], norm_final)
    return out.reshape(x.shape)
```

```python
import functools
import math

import jax
import jax.numpy as jnp
import numpy as np
from jax import lax
from jax.experimental import pallas as pl
from jax.experimental.pallas import tpu as pltpu

HEAD_DIM = 64
DIL_HEADS = 6
NSA_KV_HEADS = 2
NSA_GROUP = 5
NSA_HEADS = NSA_KV_HEADS * NSA_GROUP
N_HEADS = DIL_HEADS + NSA_HEADS
DIL_PATTERNS = ((128, 1), (512, 4), (2048, 16))
CMP_BLOCK = 32
CMP_STRIDE = 16
CMP_HIDDEN = 256
SEL_BLOCK = 64
SEL_TOPK = 16
WIN = 512
N_FORCED = 3
N_BUCKETS = 32
MAX_DISTANCE = 2048
N_GROUPS = 4
EXPERTS_PER_GROUP = 4
N_EXPERTS = N_GROUPS * EXPERTS_PER_GROUP
D_EXPERT = 512
EPS = 1e-6

LANES = 128
QB = 128
NEG = -1.0e30
LOG2E = math.log2(math.e)
DA = DIL_HEADS * HEAD_DIM
DB = NSA_HEADS * HEAD_DIM
N_ROWGROUPS = NSA_HEADS
SEL_KT = 256
VMEM_LIMIT = 56 * 1024 * 1024

F32 = jnp.float32
BF16 = jnp.bfloat16
NT_DIMS = (((1,), (1,)), ((), ()))


def _nt_dot(a, b):
    return lax.dot_general(a, b, NT_DIMS, preferred_element_type=F32)


def _dot(a, b):
    return jnp.dot(a, b, preferred_element_type=F32)


def _bucket_np(dist):
    dist = np.maximum(np.asarray(dist, np.int64), 0)
    max_exact = N_BUCKETS // 2
    x = np.maximum(dist, 1).astype(np.float32) / np.float32(max_exact)
    large = max_exact + (np.log(x) / np.float32(math.log(MAX_DISTANCE / max_exact))
                         * np.float32(N_BUCKETS - max_exact)).astype(np.int32)
    large = np.minimum(large, N_BUCKETS - 1)
    return np.where(dist < max_exact, dist, large).astype(np.int32)


BIAS_LEN = 4096


def _bias_1d(rel_bias_heads):
    onehot = (_bucket_np(np.arange(BIAS_LEN))[None, :] == np.arange(N_BUCKETS)[:, None]).astype(np.float32)
    return jnp.dot(rel_bias_heads.T.astype(F32), jnp.asarray(onehot), precision=lax.Precision.HIGHEST)


def _extend(f, lo, hi):
    assert hi <= f.shape[-1]
    if lo >= 0:
        return f[..., lo:hi]
    pad = jnp.full(f.shape[:-1] + (-lo,), NEG, f.dtype)
    return jnp.concatenate([pad, f[..., :hi]], axis=-1)


def _toeplitz(w, q, c):
    n = q + c - 1
    assert w.shape[-1] == n
    lead = w.shape[:-1]
    wp = jnp.concatenate([w, jnp.zeros(lead + (1,), w.dtype)], axis=-1)
    flat = jnp.broadcast_to(wp[..., None, :], lead + (q, n + 1)).reshape(lead + (q * (n + 1),))
    return flat[..., :q * n].reshape(lead + (q, n))[..., q - 1:q - 1 + c]


def _toeplitz_of(fn_vals, lo, q, c):
    return _toeplitz(jnp.flip(fn_vals, axis=-1), q, c)


IN_TM = 512
C_QA, C_KA, C_VA = 0, DA, 2 * DA
C_QB = 3 * DA
C_KC = C_QB + DB
C_VC, C_KS, C_VS, C_KW, C_VW, C_GT = (C_KC + LANES * i for i in range(1, 7))
N_COLS = C_GT + LANES


def _permute_w_in(w_in):
    scale = 1.0 / math.sqrt(HEAD_DIM)
    sizes = [DA] * 3 + [DB] + [NSA_KV_HEADS * HEAD_DIM] * 6 + [3 * NSA_HEADS]
    offs = np.concatenate([[0], np.cumsum(sizes)])
    part = lambda i: w_in[:, offs[i]:offs[i + 1]]
    d = w_in.shape[0]
    qb = part(3).reshape(d, NSA_KV_HEADS, NSA_GROUP, HEAD_DIM).transpose(0, 2, 1, 3).reshape(d, DB)
    gt = part(10).reshape(d, NSA_KV_HEADS, NSA_GROUP, 3).transpose(0, 3, 2, 1).reshape(d, 3 * NSA_HEADS)
    gt = jnp.pad(gt, ((0, 0), (0, LANES - 3 * NSA_HEADS)))
    cols = [part(0) * scale, part(1), part(2), qb * (scale * LOG2E)] + [part(i) for i in range(4, 10)] + [gt]
    return jnp.concatenate(cols, axis=1).astype(BF16)


def _inproj_kernel(seq_len, x_ref, g_ref, w_ref, qa_ref, ka_ref, va_ref, qb_ref, kc_ref, vc_ref,
                   ksaug_ref, vs0_ref, vs1_ref, kw_ref, vw0_ref, vw1_ref, gates_ref):
    x = x_ref[...]
    xn = (x * lax.rsqrt(jnp.mean(x * x, axis=-1, keepdims=True) + EPS) * g_ref[...]).astype(BF16)
    seg = lambda a, n: _dot(xn, w_ref[:, a:a + n])
    qa_ref[...] = seg(C_QA, DA).astype(BF16)
    ka_ref[...] = seg(C_KA, DA).astype(BF16)
    va_ref[...] = seg(C_VA, DA).astype(BF16)
    qb_ref[...] = seg(C_QB, DB).astype(BF16)
    kc_ref[...] = seg(C_KC, LANES).astype(BF16)
    vc_ref[...] = seg(C_VC, LANES).astype(BF16)
    kw_ref[...] = seg(C_KW, LANES).astype(BF16)
    tm = x.shape[0]
    tok = (pl.program_id(0) * tm) % seq_len + lax.broadcasted_iota(jnp.int32, (tm, LANES), 0)
    lane = lax.broadcasted_iota(jnp.int32, (tm, LANES), 1)
    ksaug_ref[:, 0:LANES] = seg(C_KS, LANES).astype(BF16)
    ksaug_ref[:, LANES:2 * LANES] = jnp.where(lane == tok // SEL_BLOCK, 1.0, 0.0).astype(BF16)
    low = lane < HEAD_DIM
    for col, ref0, ref1 in ((C_VS, vs0_ref, vs1_ref), (C_VW, vw0_ref, vw1_ref)):
        v = seg(col, LANES)
        ref0[...] = jnp.where(low, v, 1.0).astype(BF16)
        ref1[...] = jnp.where(low, 1.0, v).astype(BF16)
    gates_ref[...] = jax.nn.sigmoid(seg(C_GT, LANES))


def _inproj(x2d, gain, w_perm, seq_len):
    n, d = x2d.shape
    row = lambda w: pl.BlockSpec((IN_TM, w), lambda i: (i, 0))
    widths = [DA, DA, DA, DB, LANES, LANES, 2 * LANES, LANES, LANES, LANES, LANES, LANES]
    out_shape = [jax.ShapeDtypeStruct((n, w), BF16) for w in widths] + [jax.ShapeDtypeStruct((n, LANES), F32)]
    return pl.pallas_call(
        functools.partial(_inproj_kernel, seq_len),
        grid=(n // IN_TM,),
        in_specs=[row(d), pl.BlockSpec((1, d), lambda i: (0, 0)), pl.BlockSpec((d, N_COLS), lambda i: (0, 0))],
        out_specs=[row(w) for w in widths] + [row(LANES)],
        out_shape=out_shape,
        compiler_params=pltpu.CompilerParams(dimension_semantics=("parallel",), vmem_limit_bytes=VMEM_LIMIT),
        name="inproj",
    )(x2d, gain.reshape(1, d), w_perm)


def _embed_pair(w, n_tok):
    c = w.shape[1]
    w4 = w.reshape(n_tok, 1, HEAD_DIM, 1, c) * jnp.eye(NSA_KV_HEADS, dtype=w.dtype).reshape(1, 2, 1, 2, 1)
    return w4.reshape(n_tok * 2 * HEAD_DIM, 2 * c)


def _gelu_tanh(x):
    return 0.5 * x * (1.0 + jnp.tanh(math.sqrt(2.0 / math.pi) * (x + 0.044715 * (x * x * x))))


def _compress_kernel(ck_ref, cv_ref, posk_ref, posv_ref, wk1a, wk1b, wk2, wv1a, wv1b, wv2,
                     kout_ref, vout_ref, shift_ref):
    ncp = ck_ref.shape[1]
    for c_ref, pos_ref, w1a, w1b, w2, out_ref in ((ck_ref, posk_ref, wk1a, wk1b, wk2, kout_ref),
                                                  (cv_ref, posv_ref, wv1a, wv1b, wv2, vout_ref)):
        c = c_ref[0].astype(F32)
        first = _dot((c + pos_ref[0:1, :]).astype(BF16), w1a[...])
        second = _dot((c + pos_ref[1:2, :]).astype(BF16), w1b[...])
        shift_ref[0:ncp, :] = second
        shift_ref[ncp:ncp + 8, :] = jnp.zeros((8, second.shape[1]), F32)
        hidden = _gelu_tanh(first + shift_ref[1:ncp + 1, :])
        out_ref[0] = _dot(hidden.astype(BF16), w2[...]).astype(BF16)


def _compress(kc, vc, pos_k, pos_v, k_w1, k_w2, v_w1, v_w2):
    b, t, _ = kc.shape
    ncp = t // CMP_STRIDE
    half = CMP_STRIDE * HEAD_DIM
    wide = CMP_STRIDE * LANES

    def prep(w1, w2, pos):
        pos_pair = jnp.broadcast_to(pos.reshape(2, CMP_STRIDE, 1, HEAD_DIM), (2, CMP_STRIDE, 2, HEAD_DIM))
        return (_embed_pair(w1[:half], CMP_STRIDE).astype(BF16), _embed_pair(w1[half:], CMP_STRIDE).astype(BF16),
                jnp.kron(jnp.eye(NSA_KV_HEADS, dtype=w2.dtype), w2).astype(BF16),
                pos_pair.reshape(2, wide).astype(F32))

    wk1a, wk1b, wk2, posk = prep(k_w1, k_w2, pos_k)
    wv1a, wv1b, wv2, posv = prep(v_w1, v_w2, pos_v)
    full = lambda a: pl.BlockSpec(a.shape, lambda i: (0,) * a.ndim)
    tok = pl.BlockSpec((1, ncp, wide), lambda i: (i, 0, 0))
    out = pl.BlockSpec((1, ncp, LANES), lambda i: (i, 0, 0))
    return pl.pallas_call(
        _compress_kernel,
        grid=(b,),
        in_specs=[tok, tok, full(posk), full(posv), full(wk1a), full(wk1b), full(wk2), full(wv1a), full(wv1b), full(wv2)],
        out_specs=[out, out],
        out_shape=[jax.ShapeDtypeStruct((b, ncp, LANES), BF16)] * 2,
        scratch_shapes=[pltpu.VMEM((ncp + 8, 2 * CMP_HIDDEN), F32)],
        compiler_params=pltpu.CompilerParams(dimension_semantics=("parallel",), vmem_limit_bytes=VMEM_LIMIT),
        name="compress",
    )(kc.reshape(b, ncp, wide), vc.reshape(b, ncp, wide), posk, posv, wk1a, wk1b, wk2, wv1a, wv1b, wv2)


def _pair_masks(rows):
    lane = lax.broadcasted_iota(jnp.int32, (rows, LANES), 1)
    return lane < HEAD_DIM


A_SUPER = QB * DIL_PATTERNS[-1][1]


def _mixer_a_kernel(q_ref, kp_ref, kc_ref, vp_ref, vc_ref, bias_ref, out_ref, qf_ref, kf_ref, vf_ref, o_ref, lse_ref):
    first = pl.program_id(1) == 0
    n_pairs = DIL_HEADS // 2
    for p in range(n_pairs):
        cs = slice(p * LANES, (p + 1) * LANES)
        qf_ref[p] = q_ref[0, :, cs].astype(F32)
        kf_ref[p, 0:A_SUPER, :] = kp_ref[0, :, cs].astype(F32)
        kf_ref[p, A_SUPER:2 * A_SUPER, :] = kc_ref[0, :, cs].astype(F32)
        vf_ref[p, 0:A_SUPER, :] = vp_ref[0, :, cs].astype(F32)
        vf_ref[p, A_SUPER:2 * A_SUPER, :] = vc_ref[0, :, cs].astype(F32)
    low = _pair_masks(QB)
    in_prev = lax.broadcasted_iota(jnp.int32, (QB, 2 * QB), 1) < QB
    zero = jnp.zeros((QB, LANES), BF16)

    def chunk(idx, dil, q_base, k_base, q_span, k_span, off, at_start):
        rows = lambda size: pl.ds(off, size) if dil == 1 else pl.ds(off, size, stride=dil)
        q_win = lambda ref: ref.at[pl.ds(pl.multiple_of(q_base, 8), q_span), :]
        k_win = lambda ref: ref.at[pl.ds(pl.multiple_of(k_base, 8), k_span), :]
        q_rows = rows(QB)
        prev_mask = jnp.where(jnp.logical_and(in_prev, jnp.logical_and(first, at_start)), NEG, 0.0)
        for p in range(n_pairs):
            q = q_win(qf_ref.at[p])[q_rows, :].astype(BF16)
            keys = k_win(kf_ref.at[p])[rows(2 * QB), :].astype(BF16)
            vals = k_win(vf_ref.at[p])[rows(2 * QB), :].astype(BF16)
            o_win, lse_win = q_win(o_ref.at[p]), q_win(lse_ref.at[p])
            lhs = jnp.concatenate([jnp.where(low, q, zero), jnp.where(low, zero, q)], axis=0)
            bias = jnp.concatenate([bias_ref[idx, 2 * p] + prev_mask, bias_ref[idx, 2 * p + 1] + prev_mask], axis=0)
            s = _nt_dot(lhs, keys) + bias
            m = jnp.max(s, axis=1, keepdims=True)
            e = jnp.exp(s - m)
            l = jnp.sum(e, axis=1, keepdims=True)
            pv = _dot(e.astype(BF16), vals) * (1.0 / l)
            lse = m + jnp.log(l)
            o_new = jnp.where(low, pv[:QB], pv[QB:])
            l_new = jnp.where(low, jnp.broadcast_to(lse[:QB], (QB, LANES)), jnp.broadcast_to(lse[QB:], (QB, LANES)))
            if idx > 0:
                o_old, l_old = o_win[q_rows, :], lse_win[q_rows, :]
                mx = jnp.maximum(l_old, l_new)
                w_old, w_new = jnp.exp(l_old - mx), jnp.exp(l_new - mx)
                tot = w_old + w_new
                o_new = (w_old * o_old + w_new * o_new) * (1.0 / tot)
                l_new = mx + jnp.log(tot)
            o_win[q_rows, :] = o_new
            lse_win[q_rows, :] = l_new

    def loop(n, body):
        lax.fori_loop(0, n, lambda i, carry: (body(i), carry)[1], 0, unroll=2)

    for idx, (_, dil) in enumerate(DIL_PATTERNS):
        span = QB * dil
        n_chunks = A_SUPER // span
        if n_chunks > 1:
            for r in range(dil):
                loop(n_chunks, lambda c, idx=idx, dil=dil, span=span, r=r:
                     chunk(idx, dil, span * c, A_SUPER + span * (c - 1), span, 2 * span, r, c == 0))
        else:
            for r in range(8):
                loop(dil // 8, lambda hi, idx=idx, dil=dil, span=span, r=r:
                     chunk(idx, dil, 8 * hi, 8 * hi, span - 8, 2 * span - 8, r, True))
    for p in range(n_pairs):
        out_ref[0, :, p * LANES:(p + 1) * LANES] = o_ref[p].astype(BF16)


def _mixer_a(qa, ka, va, bias):
    b, t, _ = qa.shape
    cur = pl.BlockSpec((1, A_SUPER, DA), lambda bi, i: (bi, i, 0))
    prev = pl.BlockSpec((1, A_SUPER, DA), lambda bi, i: (bi, jnp.maximum(i - 1, 0), 0))
    return pl.pallas_call(
        _mixer_a_kernel,
        grid=(b, t // A_SUPER),
        in_specs=[cur, prev, cur, prev, cur, pl.BlockSpec(bias.shape, lambda bi, i: (0, 0, 0, 0))],
        out_specs=cur,
        out_shape=jax.ShapeDtypeStruct((b, t, DA), BF16),
        scratch_shapes=[pltpu.VMEM((DIL_HEADS // 2, rows, LANES), F32) for rows in (A_SUPER, 2 * A_SUPER, 2 * A_SUPER, A_SUPER, A_SUPER)],
        compiler_params=pltpu.CompilerParams(dimension_semantics=("parallel", "parallel"), vmem_limit_bytes=VMEM_LIMIT),
        name="mixer_a",
    )(qa, ka, ka, va, va, bias)


def _dil_bias(f_a, dil):
    steps = DIL_PATTERNS[0][0]
    g = f_a[:, 0:dil * steps + 1:dil]
    lo, hi = QB - (2 * QB - 1), QB + QB
    vals = jnp.concatenate([jnp.full((DIL_HEADS, -lo), NEG, F32), g, jnp.full((DIL_HEADS, hi - steps - 1), NEG, F32)], axis=1)
    return _toeplitz_of(vals, lo, QB, 2 * QB)


CMP_TILE_KEYS = LANES
CMP_TILE_SPAN = CMP_TILE_KEYS * CMP_STRIDE // QB
CMP_CONST_DELTA = 28


def _cmp_bias(f_b):
    per = QB // CMP_STRIDE
    n_rows = per * (CMP_CONST_DELTA + 1)
    m_lo, m_hi = -(CMP_TILE_KEYS - 1), n_rows
    f_b = f_b.astype(BF16)
    base = _extend(f_b, CMP_STRIDE * m_lo - (CMP_BLOCK - 1), CMP_STRIDE * m_hi - (CMP_BLOCK - 1))
    g = base.reshape(NSA_HEADS, m_hi - m_lo, CMP_STRIDE).transpose(0, 2, 1)
    t = _toeplitz_of(g, m_lo, n_rows, CMP_TILE_KEYS)
    t = t.reshape(NSA_HEADS, CMP_STRIDE, CMP_CONST_DELTA + 1, per, CMP_TILE_KEYS).transpose(2, 0, 3, 1, 4)
    t = t.reshape(CMP_CONST_DELTA + 1, NSA_HEADS, QB, CMP_TILE_KEYS)
    return jnp.concatenate([jnp.full((1,) + t.shape[1:], NEG, t.dtype), t], axis=0)


def _overlap_matrix_t(ncp, n_sel_pad):
    n = np.arange(ncp)[None, :] * CMP_STRIDE
    s = np.arange(n_sel_pad)[:, None] * SEL_BLOCK
    ov = np.clip(np.minimum(n + CMP_BLOCK, s + SEL_BLOCK) - np.maximum(n, s), 0, None) / CMP_BLOCK
    return jnp.asarray(ov, BF16)


def _gate_tile(gates_ref, branch, g):
    c = branch * NSA_HEADS + g * 2
    low = _pair_masks(QB)
    return jnp.where(low, jnp.broadcast_to(gates_ref[0, :, c:c + 1], (QB, LANES)),
                     jnp.broadcast_to(gates_ref[0, :, c + 1:c + 2], (QB, LANES)))


def _masked_q(qb_ref, g, kv):
    q = qb_ref[0, :, g * LANES:(g + 1) * LANES]
    low = _pair_masks(QB)
    keep = low if kv == 0 else jnp.logical_not(low)
    return jnp.where(keep, q, jnp.zeros_like(q))


def _cmp_kernel(n_tiles, qb_ref, kcmp_ref, vcmp_ref, gates_ref, ov_ref, *rest):
    tbl_refs, (oc_ref, sel_ref, q_ref, s_ref, p_ref, pv_ref) = rest[:n_tiles], rest[n_tiles:]
    qblk = pl.program_id(1)
    t0 = qblk * QB
    low = _pair_masks(QB)
    for kv in range(NSA_KV_HEADS):
        for g in range(NSA_GROUP):
            r = kv * NSA_GROUP + g
            q_ref[r * QB:(r + 1) * QB, :] = _masked_q(qb_ref, g, kv)

    def attend(n_vis):
        kc = n_vis * CMP_TILE_KEYS
        n_blk = n_vis * CMP_TILE_SPAN * QB // SEL_BLOCK
        s_ref[:, 0:kc] = _nt_dot(q_ref[...], kcmp_ref[0, 0:kc, :])
        blk = lax.broadcasted_iota(jnp.int32, (n_blk, QB), 0)
        cur = (t0 + lax.broadcasted_iota(jnp.int32, (n_blk, QB), 1)) // SEL_BLOCK
        blk_f = blk.astype(F32)
        forced = (blk == cur) | (blk == cur - 1) | (blk == 0)
        causal = blk <= cur
        scores = []
        for kv in range(NSA_KV_HEADS):
            psum = jnp.zeros((QB, kc), F32)
            for g in range(NSA_GROUP):
                r = kv * NSA_GROUP + g
                rows = slice(r * QB, (r + 1) * QB)
                s = s_ref[rows, 0:kc] + jnp.concatenate([tbl_refs[c][0, r].astype(F32) for c in range(n_vis)], axis=1)
                m = jnp.max(s, axis=1, keepdims=True)
                e = jnp.exp2(s - m)
                den = jnp.sum(e, axis=1, keepdims=True)
                p = jnp.where(m > 0.5 * NEG, e * (1.0 / den), 0.0)
                psum = psum + p
                p_ref[rows, 0:kc] = p.astype(BF16)
            hi = psum.astype(BF16)
            lo = (psum - hi.astype(F32)).astype(BF16)
            ov_t = ov_ref[0:n_blk, 0:kc]
            imp_t = _nt_dot(ov_t, hi) + _nt_dot(ov_t, lo)
            scores.append(jnp.where(forced, -jnp.inf, jnp.where(causal, imp_t, -1.0)))
        pv_ref[...] = _dot(p_ref[:, 0:kc], vcmp_ref[0, 0:kc, :])

        def pick(_, carry):
            new = []
            for val, sel in carry:
                mx = jnp.max(val, axis=0, keepdims=True)
                idx = jnp.min(jnp.where(val == mx, blk_f, float(n_blk)), axis=0, keepdims=True)
                hit = blk_f == idx
                new.append((jnp.where(hit, -jnp.inf, val), jnp.where(hit, 1.0, sel)))
            return tuple(new)

        taken = jnp.where(forced, 1.0, 0.0)
        picked = lax.fori_loop(0, SEL_TOPK - N_FORCED, pick, tuple((v, taken) for v in scores))
        eye = jnp.where(lax.broadcasted_iota(jnp.int32, (QB, QB), 0) == lax.broadcasted_iota(jnp.int32, (QB, QB), 1),
                        1.0, 0.0).astype(BF16)
        for kv in range(NSA_KV_HEADS):
            sel_t = jnp.where(causal, picked[kv][1], 0.0).astype(BF16)
            sel_ref[0, kv, :, 0:n_blk] = _nt_dot(eye, sel_t).astype(BF16)
            if n_blk < LANES:
                sel_ref[0, kv, :, n_blk:LANES] = jnp.zeros((QB, LANES - n_blk), BF16)

    n_vis = qblk // CMP_TILE_SPAN + 1
    for w in range(1, n_tiles + 1):
        pl.when(n_vis == w)(functools.partial(attend, w))

    for g in range(NSA_GROUP):
        o0, o1 = pv_ref[g * QB:(g + 1) * QB, :], pv_ref[(NSA_GROUP + g) * QB:(NSA_GROUP + g + 1) * QB, :]
        oc_ref[0, :, g * LANES:(g + 1) * LANES] = (jnp.where(low, o0, o1) * _gate_tile(gates_ref, 0, g)).astype(BF16)


def _compressed_branch(qb, kcmp, vcmp, gates, f_b):
    b, t, _ = qb.shape
    ncp = kcmp.shape[1]
    n_tiles = ncp // CMP_TILE_KEYS
    tbl = _cmp_bias(f_b)
    ov = _overlap_matrix_t(ncp, LANES)

    def tbl_spec(c):
        return pl.BlockSpec((1, NSA_HEADS, QB, CMP_TILE_KEYS),
                            lambda bi, i: (jnp.clip(i - CMP_TILE_SPAN * c, -1, CMP_CONST_DELTA) + 1, 0, 0, 0))

    blockq = lambda w: pl.BlockSpec((1, QB, w), lambda bi, i: (bi, i, 0))
    batch = lambda a: pl.BlockSpec((1,) + a.shape[1:], lambda bi, i: (bi, 0, 0))
    return pl.pallas_call(
        functools.partial(_cmp_kernel, n_tiles),
        grid=(b, t // QB),
        in_specs=[blockq(DB), batch(kcmp), batch(vcmp), blockq(LANES), pl.BlockSpec(ov.shape, lambda bi, i: (0, 0))]
                 + [tbl_spec(c) for c in range(n_tiles)],
        out_specs=[blockq(DB), pl.BlockSpec((1, NSA_KV_HEADS, QB, LANES), lambda bi, i: (bi, 0, i, 0))],
        out_shape=[jax.ShapeDtypeStruct((b, t, DB), BF16), jax.ShapeDtypeStruct((b, NSA_KV_HEADS, t, LANES), BF16)],
        scratch_shapes=[pltpu.VMEM((N_ROWGROUPS * QB, LANES), BF16), pltpu.VMEM((N_ROWGROUPS * QB, ncp), F32),
                        pltpu.VMEM((N_ROWGROUPS * QB, ncp), BF16), pltpu.VMEM((N_ROWGROUPS * QB, LANES), F32)],
        compiler_params=pltpu.CompilerParams(dimension_semantics=("parallel", "parallel"), vmem_limit_bytes=VMEM_LIMIT),
        name="nsa_compressed",
    )(qb, kcmp, vcmp, gates, ov, *([tbl] * n_tiles))


SEL_NEAR = 13


def _sel_bias(f_b):
    n_off = SEL_NEAR + 1
    cols = QB * n_off
    rel = f_b - f_b[:, BIAS_LEN - 1:]
    lo = -(QB - 1)
    big = _toeplitz_of(_extend(rel, lo, lo + QB + cols - 1), lo, QB, cols)
    tiles = jnp.flip(big.reshape(NSA_HEADS, QB, n_off, QB).transpose(2, 0, 1, 3), axis=0)
    return jnp.concatenate([jnp.zeros((1,) + tiles.shape[1:], F32), tiles], axis=0)


def _pair_ratio(acc0, acc1):
    low = _pair_masks(acc0.shape[0])
    den = pltpu.roll(jnp.where(low, acc1, acc0), HEAD_DIM, axis=1)
    return jnp.where(low, acc0, acc1) * (1.0 / den)


def _sel_kernel(qb_ref, sel_ref, gates_ref, cfar_ref, ks_ref, vs0_ref, vs1_ref, tbl_ref, out_ref,
                qaug_ref, s_ref, s1_ref, acc_ref, m_ref, alpha_ref):
    qblk = pl.program_id(1)
    for kv in range(NSA_KV_HEADS):
        unchosen = jnp.where(sel_ref[0, kv].astype(F32) > 0.0, 0.0, NEG)
        for g in range(NSA_GROUP):
            r = kv * NSA_GROUP + g
            rows = slice(r * QB, (r + 1) * QB)
            qaug_ref[rows, 0:LANES] = _masked_q(qb_ref, g, kv)
            qaug_ref[rows, LANES:2 * LANES] = (unchosen + cfar_ref[r:r + 1, :]).astype(BF16)
    acc_ref[...] = jnp.zeros_like(acc_ref)
    m_ref[...] = jnp.full_like(m_ref, NEG)

    last_tile = ks_ref.shape[2] // SEL_KT - 1

    def scores(j, dst_ref):
        start = pl.multiple_of(jnp.minimum(j, last_tile) * SEL_KT, SEL_KT)
        dst_ref[...] = _dot(qaug_ref[...], ks_ref[0, :, pl.ds(start, SEL_KT)])

    def consume(j, src_ref, near):
        start = pl.multiple_of(j * SEL_KT, SEL_KT)
        for r in range(N_ROWGROUPS):
            rows = slice(r * QB, (r + 1) * QB)
            s = src_ref[rows, :]
            if near:
                e1 = jnp.clip(qblk - 2 * j + 1, 0, SEL_NEAR + 1)
                e2 = jnp.clip(qblk - 2 * j, 0, SEL_NEAR + 1)
                s = s + jnp.concatenate([tbl_ref[e1, r], tbl_ref[e2, r]], axis=1)
                src_ref[rows, :] = s
            m_old = m_ref[rows, :]
            m_new = jnp.maximum(m_old, jnp.max(s, axis=1, keepdims=True))
            alpha_ref[rows, :] = jnp.exp2(m_old - m_new)
            m_ref[rows, :] = m_new
        for r in range(N_ROWGROUPS):
            rows = slice(r * QB, (r + 1) * QB)
            vals = (vs0_ref if r < NSA_GROUP else vs1_ref)[0, pl.ds(start, SEL_KT), :]
            m_new = m_ref[rows, :]
            p = jnp.exp2(src_ref[rows, :] - jnp.concatenate([m_new, m_new], axis=1))
            acc_ref[rows, :] = alpha_ref[rows, :] * acc_ref[rows, :] + _dot(p.astype(BF16), vals)

    def tile_run(first, count, near):
        bufs = (s_ref, s1_ref)
        for u in range(count):
            scores(first + u + 1, bufs[(u + 1) % 2])
            consume(first + u, bufs[u % 2], near)

    def run_pairs(first, pairs, near):
        lax.fori_loop(0, pairs // 2, lambda i, c: (tile_run(first + 4 * i, 4, near), c)[1], 0)
        pl.when(pairs % 2 == 1)(lambda: tile_run(first + 4 * (pairs // 2), 2, near))

    n_pairs = ((qblk + 2) // 2 + 1) // 2
    n_far = jnp.maximum((qblk - (SEL_NEAR - 1)) // 2, 0) // 2
    scores(0, s_ref)
    run_pairs(0, n_far, False)
    run_pairs(2 * n_far, n_pairs - n_far, True)
    for g in range(NSA_GROUP):
        ratio = _pair_ratio(acc_ref[g * QB:(g + 1) * QB, :], acc_ref[(NSA_GROUP + g) * QB:(NSA_GROUP + g + 1) * QB, :])
        out_ref[0, :, g * LANES:(g + 1) * LANES] = (ratio * _gate_tile(gates_ref, 1, g)).astype(BF16)


def _selected_branch(qb, sel, gates, ksaug, vs0aug, vs1aug, f_b):
    b, t, _ = qb.shape
    tbl = _sel_bias(f_b)
    ks_t = jnp.swapaxes(ksaug, 1, 2)
    cfar = jnp.broadcast_to(f_b[:, BIAS_LEN - 1:], (NSA_HEADS, LANES))
    cfar = jnp.pad(cfar, ((0, 16 - NSA_HEADS), (0, 0)))
    blockq = lambda w: pl.BlockSpec((1, QB, w), lambda bi, i: (bi, i, 0))
    batch = lambda a: pl.BlockSpec((1,) + a.shape[1:], lambda bi, i: (bi, 0, 0))
    rows = N_ROWGROUPS * QB
    return pl.pallas_call(
        _sel_kernel,
        grid=(b, t // QB),
        in_specs=[blockq(DB), pl.BlockSpec((1, NSA_KV_HEADS, QB, LANES), lambda bi, i: (bi, 0, i, 0)), blockq(LANES),
                  pl.BlockSpec(cfar.shape, lambda bi, i: (0, 0)), batch(ks_t), batch(vs0aug), batch(vs1aug),
                  pl.BlockSpec(tbl.shape, lambda bi, i: (0, 0, 0, 0))],
        out_specs=blockq(DB),
        out_shape=jax.ShapeDtypeStruct((b, t, DB), BF16),
        scratch_shapes=[pltpu.VMEM((rows, 2 * LANES), BF16), pltpu.VMEM((rows, SEL_KT), F32), pltpu.VMEM((rows, SEL_KT), F32),
                        pltpu.VMEM((rows, LANES), F32), pltpu.VMEM((rows, LANES), F32), pltpu.VMEM((rows, LANES), F32)],
        compiler_params=pltpu.CompilerParams(dimension_semantics=("parallel", "parallel"), vmem_limit_bytes=VMEM_LIMIT),
        name="nsa_selected",
    )(qb, sel, gates, cfar, ks_t, vs0aug, vs1aug, tbl)


WIN_KEYS = WIN + QB


def _win_bias(f_b):
    lo = WIN - (WIN_KEYS - 1)
    vals = _extend(f_b[:, :WIN], lo, WIN)
    vals = jnp.concatenate([vals, jnp.full((NSA_HEADS, lo + QB + WIN_KEYS - 1 - WIN), NEG, F32)], axis=1)
    return _toeplitz_of(vals, lo, QB, WIN_KEYS)


def _win_kernel(qb_ref, gates_ref, kw_ref, vw0_ref, vw1_ref, tbl_ref, out_ref, q_ref, s_ref):
    qblk = pl.program_id(1)
    for kv in range(NSA_KV_HEADS):
        for g in range(NSA_GROUP):
            r = kv * NSA_GROUP + g
            q_ref[r * QB:(r + 1) * QB, :] = _masked_q(qb_ref, g, kv)
    start = pl.multiple_of(qblk * QB, QB)
    s_ref[...] = _nt_dot(q_ref[...], kw_ref[0, pl.ds(start, WIN_KEYS), :])
    col = lax.broadcasted_iota(jnp.int32, (1, WIN_KEYS), 1)
    pad_mask = jnp.where(col + qblk * QB >= WIN, 0.0, NEG)
    outs = []
    for r in range(N_ROWGROUPS):
        vals = (vw0_ref if r < NSA_GROUP else vw1_ref)[0, pl.ds(start, WIN_KEYS), :]
        s = s_ref[r * QB:(r + 1) * QB, :] + tbl_ref[r] + pad_mask
        e = jnp.exp2(s - jnp.max(s, axis=1, keepdims=True))
        outs.append(_dot(e.astype(BF16), vals))
    for g in range(NSA_GROUP):
        out_ref[0, :, g * LANES:(g + 1) * LANES] = (_pair_ratio(outs[g], outs[NSA_GROUP + g])
                                                    * _gate_tile(gates_ref, 2, g)).astype(BF16)


def _window_branch(qb, gates, kw, vw0aug, vw1aug, f_b):
    b, t, _ = qb.shape
    tbl = _win_bias(f_b)
    pad_front = lambda a: jnp.pad(a, ((0, 0), (WIN, 0), (0, 0)))
    kw_pad, vw0_pad, vw1_pad = pad_front(kw), pad_front(vw0aug), pad_front(vw1aug)
    blockq = lambda w: pl.BlockSpec((1, QB, w), lambda bi, i: (bi, i, 0))
    batch = lambda a: pl.BlockSpec((1,) + a.shape[1:], lambda bi, i: (bi, 0, 0))
    rows = N_ROWGROUPS * QB
    return pl.pallas_call(
        _win_kernel,
        grid=(b, t // QB),
        in_specs=[blockq(DB), blockq(LANES), batch(kw_pad), batch(vw0_pad), batch(vw1_pad),pl.BlockSpec(tbl.shape, lambda bi, i: (0, 0, 0))],
        out_specs=blockq(DB),
        out_shape=jax.ShapeDtypeStruct((b, t, DB), BF16),
        scratch_shapes=[pltpu.VMEM((rows, LANES), BF16), pltpu.VMEM((rows, WIN_KEYS), F32)],
        compiler_params=pltpu.CompilerParams(dimension_semantics=("parallel", "parallel"), vmem_limit_bytes=VMEM_LIMIT),
        name="nsa_window",
    )(qb, gates, kw_pad, vw0_pad, vw1_pad, tbl)


OUT_TM = 256
C_GROUP = N_EXPERTS


def _outproj_kernel(x_ref, oa_ref, oc_ref, os_ref, ow_ref,
                    wout_ref, g_ref, wr_ref, br_ref, h_ref, hn_ref, comb_ref):
    ob = oc_ref[...].astype(F32) + os_ref[...].astype(F32) + ow_ref[...].astype(F32)
    y = _dot(oa_ref[...], wout_ref[0:DA, :]) + _dot(ob.astype(BF16), wout_ref[DA:DA + DB, :])
    h = x_ref[...] + y
    h_ref[...] = h
    hn = h * lax.rsqrt(jnp.mean(h * h, axis=-1, keepdims=True) + EPS) * g_ref[...]
    hn_hi = hn.astype(BF16)
    hn_ref[...] = hn_hi
    hn_lo = (hn - hn_hi.astype(F32)).astype(BF16)
    both = _dot(hn_hi, wr_ref[...])
    logits = both[:, 0:LANES] + both[:, LANES:2 * LANES] + _dot(hn_lo, wr_ref[:, 0:LANES]) + br_ref[...]
    lane = lax.broadcasted_iota(jnp.int32, logits.shape, 1)
    lane_f = lane.astype(F32)
    big = float(LANES)
    gl = jnp.where((lane >= C_GROUP) & (lane < C_GROUP + N_GROUPS), logits, -jnp.inf)
    gmax = jnp.max(gl, axis=1, keepdims=True)
    gidx = jnp.min(jnp.where(gl == gmax, lane_f, big), axis=1, keepdims=True) - C_GROUP
    gprob = 1.0 / jnp.sum(jnp.exp(gl - gmax), axis=1, keepdims=True)
    grp_of_lane = (lane // EXPERTS_PER_GROUP).astype(F32)
    el = jnp.where((lane < N_EXPERTS) & (grp_of_lane == gidx), logits, -jnp.inf)
    v1 = jnp.max(el, axis=1, keepdims=True)
    i1 = jnp.min(jnp.where(el == v1, lane_f, big), axis=1, keepdims=True)
    el2 = jnp.where(lane_f == i1, -jnp.inf, el)
    v2 = jnp.max(el2, axis=1, keepdims=True)
    i2 = jnp.min(jnp.where(el2 == v2, lane_f, big), axis=1, keepdims=True)
    e2 = jnp.exp(v2 - v1)
    p1 = 1.0 / (1.0 + e2)
    comb_ref[...] = (gprob * (jnp.where(lane_f == i1, p1, 0.0) + jnp.where(lane_f == i2, e2 * p1, 0.0))
                     + jnp.where(lane == C_GROUP, gidx, 0.0))


def _outproj(x2d, o_a, b_parts, w_out_perm, gain, w_router, b_router):
    n, d = x2d.shape
    row = lambda w: pl.BlockSpec((OUT_TM, w), lambda i: (i, 0))
    full = lambda a: pl.BlockSpec(a.shape, lambda i: (0, 0))
    return pl.pallas_call(
        _outproj_kernel,
        grid=(n // OUT_TM,),
        in_specs=[row(d), row(DA)] + [row(DB)] * 3 + [full(w_out_perm), pl.BlockSpec((1, d), lambda i: (0, 0)),
                                                      full(w_router), full(b_router)],
        out_specs=[row(d), row(d), row(LANES)],
        out_shape=[jax.ShapeDtypeStruct((n, d), F32), jax.ShapeDtypeStruct((n, d), BF16),
                   jax.ShapeDtypeStruct((n, LANES), F32)],
        compiler_params=pltpu.CompilerParams(dimension_semantics=("parallel",), vmem_limit_bytes=VMEM_LIMIT),
        name="outproj_router",
    )(x2d, o_a, *b_parts, w_out_perm, gain.reshape(1, d), w_router, b_router)


MOE_TM = 1024


MOE_SUB = 128
MOE_FINAL_TM = 512


def _moe_kernel(hn_ref, comb_ref, wg_ref, wu_ref, wd_ref, y_ref, perm_ref, hs_ref, cs_ref, ys_ref, start_ref, nsub_ref):
    grp = pl.program_id(1)
    tm, d = hn_ref.shape
    n_pad = perm_ref.shape[0]

    @pl.when(grp == 0)
    def _sort():
        comb = comb_ref[...]
        lane_f = lax.broadcasted_iota(jnp.int32, (tm, LANES), 1).astype(F32)
        onehot = jnp.where(lane_f == comb[:, C_GROUP:C_GROUP + 1], 1.0, 0.0).astype(BF16)
        eye8 = jnp.where(lax.broadcasted_iota(jnp.int32, (8, LANES), 0) == lax.broadcasted_iota(jnp.int32, (8, LANES), 1),
                         1.0, 0.0).astype(BF16)
        onehot_t = _nt_dot(eye8, onehot)
        upper = jnp.where(lax.broadcasted_iota(jnp.int32, (tm, tm), 0) <= lax.broadcasted_iota(jnp.int32, (tm, tm), 1),
                          1.0, 0.0).astype(BF16)
        cum_t = _dot(onehot_t.astype(BF16), upper)
        start = jnp.zeros((1, 1), F32)
        pos_t = jnp.zeros((1, tm), F32)
        for k in range(N_GROUPS):
            padded = jnp.ceil(cum_t[k:k + 1, tm - 1:tm] * (1.0 / MOE_SUB)) * MOE_SUB
            pos_t = pos_t + onehot_t[k:k + 1, :] * (start + cum_t[k:k + 1, :] - 1.0)
            start_ref[k] = start[0, 0].astype(jnp.int32)
            nsub_ref[k] = (padded[0, 0] * (1.0 / MOE_SUB)).astype(jnp.int32)
            start = start + padded
        perm = jnp.where(lax.broadcasted_iota(jnp.int32, (n_pad, tm), 0) == pos_t.astype(jnp.int32), 1.0, 0.0).astype(BF16)
        perm_ref[...] = perm
        hs_ref[...] = _dot(perm, hn_ref[...]).astype(BF16)
        c_hi = comb.astype(BF16)
        rest = comb - c_hi.astype(F32)
        c_mid = rest.astype(BF16)
        c_lo = (rest - c_mid.astype(F32)).astype(BF16)
        cs_ref[...] = _dot(perm, c_hi) + _dot(perm, c_mid) + _dot(perm, c_lo)
        ys_ref[...] = jnp.zeros_like(ys_ref)

    lane = lax.broadcasted_iota(jnp.int32, (MOE_SUB, LANES), 1)

    def segment(s, carry):
        rows = pl.ds(pl.multiple_of(start_ref[grp] + s * MOE_SUB, MOE_SUB), MOE_SUB)
        x = hs_ref[rows, :]
        weights = cs_ref[rows, :]
        acc = jnp.zeros((MOE_SUB, d), F32)
        for j in range(EXPERTS_PER_GROUP):
            gate = _dot(x, wg_ref[j])
            up = _dot(x, wu_ref[j])
            w = jnp.sum(jnp.where(lane == grp * EXPERTS_PER_GROUP + j, weights, 0.0), axis=1, keepdims=True)
            acc = acc + _dot((gate * jax.nn.sigmoid(gate) * up * w).astype(BF16), wd_ref[j])
        ys_ref[rows, :] = acc.astype(BF16)
        return carry

    lax.fori_loop(0, nsub_ref[grp], segment, 0)

    @pl.when(grp == pl.num_programs(1) - 1)
    def _unsort():
        y_ref[...] = lax.dot_general(perm_ref[...], ys_ref[...], (((0,), (0,)), ((), ())),
                                     preferred_element_type=F32).astype(BF16)


def _final_kernel(h_ref, y_ref, g_ref, out_ref):
    y = h_ref[...] + y_ref[...].astype(F32)
    out_ref[...] = y * lax.rsqrt(jnp.mean(y * y, axis=-1, keepdims=True) + EPS) * g_ref[...]


def _moe(h, hn, comb, w_gate, w_up, w_down, gain):
    n, d = h.shape
    tm = min(MOE_TM, n)
    n_pad = tm + (N_GROUPS - 1) * MOE_SUB
    row = lambda w: pl.BlockSpec((tm, w), lambda i, g: (i, 0))
    group_w = lambda a: pl.BlockSpec((EXPERTS_PER_GROUP,) + a.shape[1:], lambda i, g: (g, 0, 0))
    y = pl.pallas_call(
        _moe_kernel,
        grid=(n // tm, N_GROUPS),
        in_specs=[row(d), row(LANES), group_w(w_gate), group_w(w_up), group_w(w_down)],
        out_specs=row(d),
        out_shape=jax.ShapeDtypeStruct((n, d), BF16),
        scratch_shapes=[pltpu.VMEM((n_pad, tm), BF16), pltpu.VMEM((n_pad, d), BF16), pltpu.VMEM((n_pad, LANES), F32),
                        pltpu.VMEM((n_pad, d), BF16), pltpu.SMEM((N_GROUPS,), jnp.int32), pltpu.SMEM((N_GROUPS,), jnp.int32)],
        compiler_params=pltpu.CompilerParams(dimension_semantics=("parallel", "arbitrary"), vmem_limit_bytes=VMEM_LIMIT),
        name="moe_experts",
    )(hn, comb, w_gate, w_up, w_down)
    rowf = lambda w: pl.BlockSpec((MOE_FINAL_TM, w), lambda i: (i, 0))
    return pl.pallas_call(
        _final_kernel,
        grid=(n // MOE_FINAL_TM,),
        in_specs=[rowf(d), rowf(d), pl.BlockSpec((1, d), lambda i: (0, 0))],
        out_specs=rowf(d),
        out_shape=jax.ShapeDtypeStruct((n, d), F32),
        compiler_params=pltpu.CompilerParams(dimension_semantics=("parallel",), vmem_limit_bytes=VMEM_LIMIT),
        name="residual_final_norm",
    )(h, y, gain.reshape(1, d))


def _permute_w_out(w_out):
    d = w_out.shape[1]
    wb = w_out[DA:].reshape(NSA_KV_HEADS, NSA_GROUP, HEAD_DIM, d).transpose(1, 0, 2, 3).reshape(DB, d)
    return jnp.concatenate([w_out[:DA], wb], axis=0).astype(BF16)


def _router_weights(w_group, b_group, w_expert, b_expert):
    d = w_group.shape[0]
    w = jnp.concatenate([w_expert.reshape(d, N_EXPERTS), w_group], axis=1)
    b = jnp.concatenate([b_expert.reshape(N_EXPERTS), b_group])
    pad = LANES - w.shape[1]
    w = jnp.pad(w, ((0, 0), (0, pad))).astype(F32)
    w_hi = w.astype(BF16)
    w_lo = (w - w_hi.astype(F32)).astype(BF16)
    return jnp.concatenate([w_hi, w_lo], axis=1), jnp.pad(b, (0, pad)).reshape(1, LANES).astype(F32)


def _layer(h, rel_bias, norm_mix, w_in, w_out, cmp_pos_k, cmp_pos_v, cmp_k_w1, cmp_k_w2, cmp_v_w1, cmp_v_w2,
           norm_ffn, w_rg, b_rg, w_re, b_re, w_gate, w_up, w_down, out_gain):
    b, t, d = h.shape
    n = b * t
    assert t % (QB * DIL_PATTERNS[-1][1]) == 0 and t // SEL_BLOCK <= LANES and n % MOE_TM == 0
    x2d = h.reshape(n, d)
    seq = lambda a: a.reshape(b, t, a.shape[-1])
    qa, ka, va, qb, kc, vc, ksaug, vs0aug, vs1aug, kw, vw0aug, vw1aug, gates = map(
        seq, _inproj(x2d, norm_mix, _permute_w_in(w_in), t))
    f_a = _bias_1d(rel_bias[:, :DIL_HEADS])
    f_b = _bias_1d(rel_bias[:, DIL_HEADS:]) * LOG2E
    o_a = _mixer_a(qa, ka, va, jnp.stack([_dil_bias(f_a, dil) for _, dil in DIL_PATTERNS]))
    kcmp, vcmp = _compress(kc, vc, cmp_pos_k, cmp_pos_v, cmp_k_w1, cmp_k_w2, cmp_v_w1, cmp_v_w2)
    o_cmp, sel = _compressed_branch(qb, kcmp, vcmp, gates, f_b)
    o_sel = _selected_branch(qb, sel, gates, ksaug, vs0aug, vs1aug, f_b)
    o_win = _window_branch(qb, gates, kw, vw0aug, vw1aug, f_b)
    b_parts = [o.reshape(n, DB) for o in (o_cmp, o_sel, o_win)]
    w_router, b_router = _router_weights(w_rg, b_rg, w_re, b_re)
    h2, hn, comb = _outproj(x2d, o_a.reshape(n, DA), b_parts, _permute_w_out(w_out), norm_ffn, w_router, b_router)
    return _moe(h2, hn, comb, w_gate.astype(BF16), w_up.astype(BF16), w_down.astype(BF16), out_gain)


def kernel(x, rel_bias, norm_mix, w_in, w_out, cmp_pos_k, cmp_pos_v, cmp_k_w1, cmp_k_w2, cmp_v_w1, cmp_v_w2,
           norm_ffn, w_router_group, b_router_group, w_router_expert, b_router_expert, w_gate, w_up, w_down,
           norm_final):
    depth = norm_mix.shape[0]
    assert depth == 1, "the final RMSNorm is fused into the last layer's expert kernel"
    out = _layer(x, rel_bias, norm_mix[0], w_in[0], w_out[0], cmp_pos_k[0], cmp_pos_v[0], cmp_k_w1[0], cmp_k_w2[0],
                 cmp_v_w1[0], cmp_v_w2[0], norm_ffn[0], w_router_group[0], b_router_group[0], w_router_expert[0],
                 b_router_expert[0], w_gate[0], w_up[0], w_down[0], norm_final)
    return out.reshape(x.shape)
```

```python
import functools
import math

import jax
import jax.numpy as jnp
import numpy as np
from jax import lax
from jax.experimental import pallas as pl
from jax.experimental.pallas import tpu as pltpu

HEAD_DIM = 64
DIL_HEADS = 6
NSA_KV_HEADS = 2
NSA_GROUP = 5
NSA_HEADS = NSA_KV_HEADS * NSA_GROUP
N_HEADS = DIL_HEADS + NSA_HEADS
DIL_PATTERNS = ((128, 1), (512, 4), (2048, 16))
CMP_BLOCK = 32
CMP_STRIDE = 16
CMP_HIDDEN = 256
SEL_BLOCK = 64
SEL_TOPK = 16
WIN = 512
N_FORCED = 3
N_BUCKETS = 32
MAX_DISTANCE = 2048
N_GROUPS = 4
EXPERTS_PER_GROUP = 4
N_EXPERTS = N_GROUPS * EXPERTS_PER_GROUP
D_EXPERT = 512
EPS = 1e-6

LANES = 128
QB = 128
NEG = -1.0e30
LOG2E = math.log2(math.e)
DA = DIL_HEADS * HEAD_DIM
DB = NSA_HEADS * HEAD_DIM
N_ROWGROUPS = NSA_HEADS
SEL_KT = 256
VMEM_LIMIT = 56 * 1024 * 1024

F32 = jnp.float32
BF16 = jnp.bfloat16
NT_DIMS = (((1,), (1,)), ((), ()))


def _nt_dot(a, b):
    return lax.dot_general(a, b, NT_DIMS, preferred_element_type=F32)


def _dot(a, b):
    return jnp.dot(a, b, preferred_element_type=F32)


def _bucket_np(dist):
    dist = np.maximum(np.asarray(dist, np.int64), 0)
    max_exact = N_BUCKETS // 2
    x = np.maximum(dist, 1).astype(np.float32) / np.float32(max_exact)
    large = max_exact + (np.log(x) / np.float32(math.log(MAX_DISTANCE / max_exact))
                         * np.float32(N_BUCKETS - max_exact)).astype(np.int32)
    large = np.minimum(large, N_BUCKETS - 1)
    return np.where(dist < max_exact, dist, large).astype(np.int32)


BIAS_LEN = 4096


def _bias_1d(rel_bias_heads):
    onehot = (_bucket_np(np.arange(BIAS_LEN))[None, :] == np.arange(N_BUCKETS)[:, None]).astype(np.float32)
    return jnp.dot(rel_bias_heads.T.astype(F32), jnp.asarray(onehot), precision=lax.Precision.HIGHEST)


def _extend(f, lo, hi):
    assert hi <= f.shape[-1]
    if lo >= 0:
        return f[..., lo:hi]
    pad = jnp.full(f.shape[:-1] + (-lo,), NEG, f.dtype)
    return jnp.concatenate([pad, f[..., :hi]], axis=-1)


def _toeplitz(w, q, c):
    n = q + c - 1
    assert w.shape[-1] == n
    lead = w.shape[:-1]
    wp = jnp.concatenate([w, jnp.zeros(lead + (1,), w.dtype)], axis=-1)
    flat = jnp.broadcast_to(wp[..., None, :], lead + (q, n + 1)).reshape(lead + (q * (n + 1),))
    return flat[..., :q * n].reshape(lead + (q, n))[..., q - 1:q - 1 + c]


def _toeplitz_of(fn_vals, lo, q, c):
    return _toeplitz(jnp.flip(fn_vals, axis=-1), q, c)


IN_TM = 512
C_QA, C_KA, C_VA = 0, DA, 2 * DA
C_QB = 3 * DA
C_KC = C_QB + DB
C_VC, C_KS, C_VS, C_KW, C_VW, C_GT = (C_KC + LANES * i for i in range(1, 7))
N_COLS = C_GT + LANES


def _permute_w_in(w_in):
    scale = 1.0 / math.sqrt(HEAD_DIM)
    sizes = [DA] * 3 + [DB] + [NSA_KV_HEADS * HEAD_DIM] * 6 + [3 * NSA_HEADS]
    offs = np.concatenate([[0], np.cumsum(sizes)])
    part = lambda i: w_in[:, offs[i]:offs[i + 1]]
    d = w_in.shape[0]
    qb = part(3).reshape(d, NSA_KV_HEADS, NSA_GROUP, HEAD_DIM).transpose(0, 2, 1, 3).reshape(d, DB)
    gt = part(10).reshape(d, NSA_KV_HEADS, NSA_GROUP, 3).transpose(0, 3, 2, 1).reshape(d, 3 * NSA_HEADS)
    gt = jnp.pad(gt, ((0, 0), (0, LANES - 3 * NSA_HEADS)))
    cols = [part(0) * scale, part(1), part(2), qb * (scale * LOG2E)] + [part(i) for i in range(4, 10)] + [gt]
    return jnp.concatenate(cols, axis=1).astype(BF16)


def _inproj_kernel(seq_len, x_ref, g_ref, w_ref, qa_ref, ka_ref, va_ref, qb_ref, kc_ref, vc_ref,
                   kst_ref, vs0_ref, vs1_ref, kw_ref, vw0_ref, vw1_ref, gates_ref, stage_ref):
    x = x_ref[...]
    xn = (x * lax.rsqrt(jnp.mean(x * x, axis=-1, keepdims=True) + EPS) * g_ref[...]).astype(BF16)
    seg = lambda a, n: _dot(xn, w_ref[:, a:a + n])
    qa_ref[...] = seg(C_QA, DA).astype(BF16)
    ka_ref[...] = seg(C_KA, DA).astype(BF16)
    va_ref[...] = seg(C_VA, DA).astype(BF16)
    qb_ref[...] = seg(C_QB, DB).astype(BF16)
    kw_ref[...] = seg(C_KW, LANES).astype(BF16)
    tm = x.shape[0]
    for col, out_ref in ((C_KC, kc_ref), (C_VC, vc_ref)):
        stage_ref[...] = seg(col, LANES)
        for j in range(CMP_STRIDE):
            out_ref[:, j * LANES:(j + 1) * LANES] = stage_ref[pl.ds(j, tm // CMP_STRIDE, stride=CMP_STRIDE), :].astype(BF16)
    tok_t = (pl.program_id(0) * tm) % seq_len + lax.broadcasted_iota(jnp.int32, (LANES, tm), 1)
    blk_t = lax.broadcasted_iota(jnp.int32, (LANES, tm), 0)
    stage_ref[...] = seg(C_KS, LANES)
    kst_ref[0, 0:LANES, :] = stage_ref[...].T.astype(BF16)
    kst_ref[0, LANES:2 * LANES, :] = jnp.where(blk_t == tok_t // SEL_BLOCK, 1.0, 0.0).astype(BF16)
    low = lax.broadcasted_iota(jnp.int32, (tm, LANES), 1) < HEAD_DIM
    for col, ref0, ref1 in ((C_VS, vs0_ref, vs1_ref), (C_VW, vw0_ref, vw1_ref)):
        v = seg(col, LANES)
        ref0[...] = jnp.where(low, v, 1.0).astype(BF16)
        ref1[...] = jnp.where(low, 1.0, v).astype(BF16)
    gates_ref[...] = jax.nn.sigmoid(seg(C_GT, LANES))


def _inproj(x2d, gain, w_perm, seq_len):
    n, d = x2d.shape
    row = lambda w: pl.BlockSpec((IN_TM, w), lambda i: (i, 0))
    rows = lambda w: (jax.ShapeDtypeStruct((n, w), BF16), row(w))
    chunks = (jax.ShapeDtypeStruct((n // CMP_STRIDE, CMP_STRIDE * LANES), BF16),
              pl.BlockSpec((IN_TM // CMP_STRIDE, CMP_STRIDE * LANES), lambda i: (i, 0)))
    per_seq = seq_len // IN_TM
    keys_t = (jax.ShapeDtypeStruct((n // seq_len, 2 * LANES, seq_len), BF16),
              pl.BlockSpec((1, 2 * LANES, IN_TM), lambda i: (i // per_seq, 0, i % per_seq)))
    outs = [rows(DA), rows(DA), rows(DA), rows(DB), chunks, chunks, keys_t] + [rows(LANES)] * 5
    outs.append((jax.ShapeDtypeStruct((n, LANES), F32), row(LANES)))
    return pl.pallas_call(
        functools.partial(_inproj_kernel, seq_len),
        grid=(n // IN_TM,),
        in_specs=[row(d), pl.BlockSpec((1, d), lambda i: (0, 0)), pl.BlockSpec((d, N_COLS), lambda i: (0, 0))],
        out_specs=[spec for _, spec in outs],
        out_shape=[shape for shape, _ in outs],
        scratch_shapes=[pltpu.VMEM((IN_TM, LANES), F32)],
        compiler_params=pltpu.CompilerParams(dimension_semantics=("parallel",), vmem_limit_bytes=VMEM_LIMIT),
        name="inproj",
    )(x2d, gain.reshape(1, d), w_perm)


def _embed_pair(w, n_tok):
    c = w.shape[1]
    w4 = w.reshape(n_tok, 1, HEAD_DIM, 1, c) * jnp.eye(NSA_KV_HEADS, dtype=w.dtype).reshape(1, 2, 1, 2, 1)
    return w4.reshape(n_tok * 2 * HEAD_DIM, 2 * c)


def _gelu_tanh(x):
    return 0.5 * x * (1.0 + jnp.tanh(math.sqrt(2.0 / math.pi) * (x + 0.044715 * (x * x * x))))


def _compress_kernel(ck_ref, cv_ref, posk_ref, posv_ref, wk1a, wk1b, wk2, wv1a, wv1b, wv2,
                     kout_ref, vout_ref, shift_ref):
    ncp = ck_ref.shape[1]
    for c_ref, pos_ref, w1a, w1b, w2, out_ref in ((ck_ref, posk_ref, wk1a, wk1b, wk2, kout_ref),
                                                  (cv_ref, posv_ref, wv1a, wv1b, wv2, vout_ref)):
        c = c_ref[0].astype(F32)
        first = _dot((c + pos_ref[0:1, :]).astype(BF16), w1a[...])
        second = _dot((c + pos_ref[1:2, :]).astype(BF16), w1b[...])
        shift_ref[0:ncp, :] = second
        shift_ref[ncp:ncp + 8, :] = jnp.zeros((8, second.shape[1]), F32)
        hidden = _gelu_tanh(first + shift_ref[1:ncp + 1, :])
        out_ref[0] = _dot(hidden.astype(BF16), w2[...]).astype(BF16)


def _compress(kc, vc, pos_k, pos_v, k_w1, k_w2, v_w1, v_w2):
    b, ncp, wide = kc.shape
    half = CMP_STRIDE * HEAD_DIM

    def prep(w1, w2, pos):
        pos_pair = jnp.broadcast_to(pos.reshape(2, CMP_STRIDE, 1, HEAD_DIM), (2, CMP_STRIDE, 2, HEAD_DIM))
        return (_embed_pair(w1[:half], CMP_STRIDE).astype(BF16), _embed_pair(w1[half:], CMP_STRIDE).astype(BF16),
                jnp.kron(jnp.eye(NSA_KV_HEADS, dtype=w2.dtype), w2).astype(BF16),
                pos_pair.reshape(2, wide).astype(F32))

    wk1a, wk1b, wk2, posk = prep(k_w1, k_w2, pos_k)
    wv1a, wv1b, wv2, posv = prep(v_w1, v_w2, pos_v)
    full = lambda a: pl.BlockSpec(a.shape, lambda i: (0,) * a.ndim)
    tok = pl.BlockSpec((1, ncp, wide), lambda i: (i, 0, 0))
    out = pl.BlockSpec((1, ncp, LANES), lambda i: (i, 0, 0))
    return pl.pallas_call(
        _compress_kernel,
        grid=(b,),
        in_specs=[tok, tok, full(posk), full(posv), full(wk1a), full(wk1b), full(wk2), full(wv1a), full(wv1b), full(wv2)],
        out_specs=[out, out],
        out_shape=[jax.ShapeDtypeStruct((b, ncp, LANES), BF16)] * 2,
        scratch_shapes=[pltpu.VMEM((ncp + 8, 2 * CMP_HIDDEN), F32)],
        compiler_params=pltpu.CompilerParams(dimension_semantics=("parallel",), vmem_limit_bytes=VMEM_LIMIT),
        name="compress",
    )(kc, vc, posk, posv, wk1a, wk1b, wk2, wv1a, wv1b, wv2)


def _pair_masks(rows):
    lane = lax.broadcasted_iota(jnp.int32, (rows, LANES), 1)
    return lane < HEAD_DIM


A_SUPER = QB * DIL_PATTERNS[-1][1]


def _mixer_a_kernel(q_ref, kp_ref, kc_ref, vp_ref, vc_ref, bias_ref, out_ref, qf_ref, kf_ref, vf_ref, o_ref, lse_ref):
    first = pl.program_id(1) == 0
    n_pairs = DIL_HEADS // 2
    for p in range(n_pairs):
        cs = slice(p * LANES, (p + 1) * LANES)
        qf_ref[p] = q_ref[0, :, cs].astype(F32)
        kf_ref[p, 0:A_SUPER, :] = kp_ref[0, :, cs].astype(F32)
        kf_ref[p, A_SUPER:2 * A_SUPER, :] = kc_ref[0, :, cs].astype(F32)
        vf_ref[p, 0:A_SUPER, :] = vp_ref[0, :, cs].astype(F32)
        vf_ref[p, A_SUPER:2 * A_SUPER, :] = vc_ref[0, :, cs].astype(F32)
    low = _pair_masks(QB)
    in_prev = lax.broadcasted_iota(jnp.int32, (QB, 2 * QB), 1) < QB
    zero = jnp.zeros((QB, LANES), BF16)

    def chunk(idx, dil, q_base, k_base, q_span, k_span, off, at_start):
        rows = lambda size: pl.ds(off, size) if dil == 1 else pl.ds(off, size, stride=dil)
        q_win = lambda ref: ref.at[pl.ds(pl.multiple_of(q_base, 8), q_span), :]
        k_win = lambda ref: ref.at[pl.ds(pl.multiple_of(k_base, 8), k_span), :]
        q_rows = rows(QB)
        prev_mask = jnp.where(jnp.logical_and(in_prev, jnp.logical_and(first, at_start)), NEG, 0.0)
        for p in range(n_pairs):
            q = q_win(qf_ref.at[p])[q_rows, :].astype(BF16)
            keys = k_win(kf_ref.at[p])[rows(2 * QB), :].astype(BF16)
            vals = k_win(vf_ref.at[p])[rows(2 * QB), :].astype(BF16)
            o_win, lse_win = q_win(o_ref.at[p]), q_win(lse_ref.at[p])
            lhs = jnp.concatenate([jnp.where(low, q, zero), jnp.where(low, zero, q)], axis=0)
            bias = jnp.concatenate([bias_ref[idx, 2 * p] + prev_mask, bias_ref[idx, 2 * p + 1] + prev_mask], axis=0)
            s = _nt_dot(lhs, keys) + bias
            m = jnp.max(s, axis=1, keepdims=True)
            e = jnp.exp(s - m)
            l = jnp.sum(e, axis=1, keepdims=True)
            pv = _dot(e.astype(BF16), vals) * (1.0 / l)
            lse = m + jnp.log(l)
            o_new = jnp.where(low, pv[:QB], pv[QB:])
            l_new = jnp.where(low, jnp.broadcast_to(lse[:QB], (QB, LANES)), jnp.broadcast_to(lse[QB:], (QB, LANES)))
            if idx > 0:
                o_old, l_old = o_win[q_rows, :], lse_win[q_rows, :]
                mx = jnp.maximum(l_old, l_new)
                w_old, w_new = jnp.exp(l_old - mx), jnp.exp(l_new - mx)
                tot = w_old + w_new
                o_new = (w_old * o_old + w_new * o_new) * (1.0 / tot)
                l_new = mx + jnp.log(tot)
            o_win[q_rows, :] = o_new
            lse_win[q_rows, :] = l_new

    def loop(n, body):
        lax.fori_loop(0, n, lambda i, carry: (body(i), carry)[1], 0, unroll=2)

    for idx, (_, dil) in enumerate(DIL_PATTERNS):
        span = QB * dil
        n_chunks = A_SUPER // span
        if n_chunks > 1:
            for r in range(dil):
                loop(n_chunks, lambda c, idx=idx, dil=dil, span=span, r=r:
                     chunk(idx, dil, span * c, A_SUPER + span * (c - 1), span, 2 * span, r, c == 0))
        else:
            for r in range(8):
                loop(dil // 8, lambda hi, idx=idx, dil=dil, span=span, r=r:
                     chunk(idx, dil, 8 * hi, 8 * hi, span - 8, 2 * span - 8, r, True))
    for p in range(n_pairs):
        out_ref[0, :, p * LANES:(p + 1) * LANES] = o_ref[p].astype(BF16)


def _mixer_a(qa, ka, va, bias):
    b, t, _ = qa.shape
    cur = pl.BlockSpec((1, A_SUPER, DA), lambda bi, i: (bi, i, 0))
    prev = pl.BlockSpec((1, A_SUPER, DA), lambda bi, i: (bi, jnp.maximum(i - 1, 0), 0))
    return pl.pallas_call(
        _mixer_a_kernel,
        grid=(b, t // A_SUPER),
        in_specs=[cur, prev, cur, prev, cur, pl.BlockSpec(bias.shape, lambda bi, i: (0, 0, 0, 0))],
        out_specs=cur,
        out_shape=jax.ShapeDtypeStruct((b, t, DA), BF16),
        scratch_shapes=[pltpu.VMEM((DIL_HEADS // 2, rows, LANES), F32) for rows in (A_SUPER, 2 * A_SUPER, 2 * A_SUPER, A_SUPER, A_SUPER)],
        compiler_params=pltpu.CompilerParams(dimension_semantics=("parallel", "parallel"), vmem_limit_bytes=VMEM_LIMIT),
        name="mixer_a",
    )(qa, ka, ka, va, va, bias)


def _dil_bias(f_a, dil):
    steps = DIL_PATTERNS[0][0]
    g = f_a[:, 0:dil * steps + 1:dil]
    lo, hi = QB - (2 * QB - 1), QB + QB
    vals = jnp.concatenate([jnp.full((DIL_HEADS, -lo), NEG, F32), g, jnp.full((DIL_HEADS, hi - steps - 1), NEG, F32)], axis=1)
    return _toeplitz_of(vals, lo, QB, 2 * QB)


CMP_TILE_KEYS = LANES
CMP_TILE_SPAN = CMP_TILE_KEYS * CMP_STRIDE // QB
CMP_CONST_DELTA = 28


def _cmp_bias(f_b):
    per = QB // CMP_STRIDE
    n_rows = per * (CMP_CONST_DELTA + 1)
    m_lo, m_hi = -(CMP_TILE_KEYS - 1), n_rows
    f_b = f_b.astype(BF16)
    base = _extend(f_b, CMP_STRIDE * m_lo - (CMP_BLOCK - 1), CMP_STRIDE * m_hi - (CMP_BLOCK - 1))
    g = base.reshape(NSA_HEADS, m_hi - m_lo, CMP_STRIDE).transpose(0, 2, 1)
    t = _toeplitz_of(g, m_lo, n_rows, CMP_TILE_KEYS)
    t = t.reshape(NSA_HEADS, CMP_STRIDE, CMP_CONST_DELTA + 1, per, CMP_TILE_KEYS).transpose(2, 0, 3, 1, 4)
    t = t.reshape(CMP_CONST_DELTA + 1, NSA_HEADS, QB, CMP_TILE_KEYS)
    return jnp.concatenate([jnp.full((1,) + t.shape[1:], NEG, t.dtype), t], axis=0)


def _overlap_matrix_t(ncp, n_sel_pad):
    n = np.arange(ncp)[None, :] * CMP_STRIDE
    s = np.arange(n_sel_pad)[:, None] * SEL_BLOCK
    ov = np.clip(np.minimum(n + CMP_BLOCK, s + SEL_BLOCK) - np.maximum(n, s), 0, None) / CMP_BLOCK
    return jnp.asarray(ov, BF16)


def _gate_tile(gates_ref, branch, g, row0=0):
    c = branch * NSA_HEADS + g * 2
    low = _pair_masks(QB)
    return jnp.where(low, jnp.broadcast_to(gates_ref[0, row0:row0 + QB, c:c + 1], (QB, LANES)),
                     jnp.broadcast_to(gates_ref[0, row0:row0 + QB, c + 1:c + 2], (QB, LANES)))


def _masked_q(qb_ref, g, kv, row0=0):
    q = qb_ref[0, row0:row0 + QB, g * LANES:(g + 1) * LANES]
    low = _pair_masks(QB)
    keep = low if kv == 0 else jnp.logical_not(low)
    return jnp.where(keep, q, jnp.zeros_like(q))


def _cmp_kernel(n_tiles, qb_ref, kcmp_ref, vcmp_ref, gates_ref, ov_ref, *rest):
    tbl_refs, (oc_ref, sel_ref, q_ref, s_ref, p_ref, pv_ref) = rest[:n_tiles], rest[n_tiles:]
    qblk = pl.program_id(1)
    t0 = qblk * QB
    low = _pair_masks(QB)
    for kv in range(NSA_KV_HEADS):
        for g in range(NSA_GROUP):
            r = kv * NSA_GROUP + g
            q_ref[r * QB:(r + 1) * QB, :] = _masked_q(qb_ref, g, kv)

    def attend(n_vis):
        kc = n_vis * CMP_TILE_KEYS
        n_blk = n_vis * CMP_TILE_SPAN * QB // SEL_BLOCK
        s_ref[:, 0:kc] = _nt_dot(q_ref[...], kcmp_ref[0, 0:kc, :])
        blk = lax.broadcasted_iota(jnp.int32, (n_blk, QB), 0)
        cur = (t0 + lax.broadcasted_iota(jnp.int32, (n_blk, QB), 1)) // SEL_BLOCK
        blk_f = blk.astype(F32)
        forced = (blk == cur) | (blk == cur - 1) | (blk == 0)
        causal = blk <= cur
        scores = []
        for kv in range(NSA_KV_HEADS):
            psum = jnp.zeros((QB, kc), F32)
            for g in range(NSA_GROUP):
                r = kv * NSA_GROUP + g
                rows = slice(r * QB, (r + 1) * QB)
                s = s_ref[rows, 0:kc] + jnp.concatenate([tbl_refs[c][0, r].astype(F32) for c in range(n_vis)], axis=1)
                m = jnp.max(s, axis=1, keepdims=True)
                e = jnp.exp2(s - m)
                den = jnp.sum(e, axis=1, keepdims=True)
                p = jnp.where(m > 0.5 * NEG, e * (1.0 / den), 0.0)
                psum = psum + p
                p_ref[rows, 0:kc] = p.astype(BF16)
            hi = psum.astype(BF16)
            lo = (psum - hi.astype(F32)).astype(BF16)
            ov_t = ov_ref[0:n_blk, 0:kc]
            imp_t = _nt_dot(ov_t, hi) + _nt_dot(ov_t, lo)
            scores.append(jnp.where(forced, -jnp.inf, jnp.where(causal, imp_t, -1.0)))
        pv_ref[...] = _dot(p_ref[:, 0:kc], vcmp_ref[0, 0:kc, :])

        def pick(_, carry):
            new = []
            for val, sel in carry:
                mx = jnp.max(val, axis=0, keepdims=True)
                idx = jnp.min(jnp.where(val == mx, blk_f, float(n_blk)), axis=0, keepdims=True)
                hit = blk_f == idx
                new.append((jnp.where(hit, -jnp.inf, val), jnp.where(hit, 1.0, sel)))
            return tuple(new)

        taken = jnp.where(forced, 1.0, 0.0)
        picked = lax.fori_loop(0, SEL_TOPK - N_FORCED, pick, tuple((v, taken) for v in scores))
        eye = jnp.where(lax.broadcasted_iota(jnp.int32, (QB, QB), 0) == lax.broadcasted_iota(jnp.int32, (QB, QB), 1),
                        1.0, 0.0).astype(BF16)
        for kv in range(NSA_KV_HEADS):
            sel_t = jnp.where(causal, picked[kv][1], 0.0).astype(BF16)
            sel_ref[0, kv, :, 0:n_blk] = _nt_dot(eye, sel_t).astype(BF16)
            if n_blk < LANES:
                sel_ref[0, kv, :, n_blk:LANES] = jnp.zeros((QB, LANES - n_blk), BF16)

    n_vis = qblk // CMP_TILE_SPAN + 1
    for w in range(1, n_tiles + 1):
        pl.when(n_vis == w)(functools.partial(attend, w))

    for g in range(NSA_GROUP):
        o0, o1 = pv_ref[g * QB:(g + 1) * QB, :], pv_ref[(NSA_GROUP + g) * QB:(NSA_GROUP + g + 1) * QB, :]
        oc_ref[0, :, g * LANES:(g + 1) * LANES] = (jnp.where(low, o0, o1) * _gate_tile(gates_ref, 0, g)).astype(BF16)


def _compressed_branch(qb, kcmp, vcmp, gates, f_b):
    b, t, _ = qb.shape
    ncp = kcmp.shape[1]
    n_tiles = ncp // CMP_TILE_KEYS
    tbl = _cmp_bias(f_b)
    ov = _overlap_matrix_t(ncp, LANES)

    def tbl_spec(c):
        return pl.BlockSpec((1, NSA_HEADS, QB, CMP_TILE_KEYS),
                            lambda bi, i: (jnp.clip(i - CMP_TILE_SPAN * c, -1, CMP_CONST_DELTA) + 1, 0, 0, 0))

    blockq = lambda w: pl.BlockSpec((1, QB, w), lambda bi, i: (bi, i, 0))
    batch = lambda a: pl.BlockSpec((1,) + a.shape[1:], lambda bi, i: (bi, 0, 0))
    return pl.pallas_call(
        functools.partial(_cmp_kernel, n_tiles),
        grid=(b, t // QB),
        in_specs=[blockq(DB), batch(kcmp), batch(vcmp), blockq(LANES), pl.BlockSpec(ov.shape, lambda bi, i: (0, 0))]
                 + [tbl_spec(c) for c in range(n_tiles)],
        out_specs=[blockq(DB), pl.BlockSpec((1, NSA_KV_HEADS, QB, LANES), lambda bi, i: (bi, 0, i, 0))],
        out_shape=[jax.ShapeDtypeStruct((b, t, DB), BF16), jax.ShapeDtypeStruct((b, NSA_KV_HEADS, t, LANES), BF16)],
        scratch_shapes=[pltpu.VMEM((N_ROWGROUPS * QB, LANES), BF16), pltpu.VMEM((N_ROWGROUPS * QB, ncp), F32),
                        pltpu.VMEM((N_ROWGROUPS * QB, ncp), BF16), pltpu.VMEM((N_ROWGROUPS * QB, LANES), F32)],
        compiler_params=pltpu.CompilerParams(dimension_semantics=("parallel", "parallel"), vmem_limit_bytes=VMEM_LIMIT),
        name="nsa_compressed",
    )(qb, kcmp, vcmp, gates, ov, *([tbl] * n_tiles))


SEL_NEAR = 13


def _sel_bias(f_b):
    n_off = SEL_NEAR + 1
    cols = QB * n_off
    rel = f_b - f_b[:, BIAS_LEN - 1:]
    lo = -(QB - 1)
    big = _toeplitz_of(_extend(rel, lo, lo + QB + cols - 1), lo, QB, cols)
    tiles = jnp.flip(big.reshape(NSA_HEADS, QB, n_off, QB).transpose(2, 0, 1, 3), axis=0)
    return jnp.concatenate([jnp.zeros((1,) + tiles.shape[1:], F32), tiles], axis=0)


def _pair_ratio(acc0, acc1):
    low = _pair_masks(acc0.shape[0])
    den = pltpu.roll(jnp.where(low, acc1, acc0), HEAD_DIM, axis=1)
    return jnp.where(low, acc0, acc1) * (1.0 / den)


def _sel_kernel(qb_ref, sel_ref, gates_ref, cfar_ref, ks_ref, vs0_ref, vs1_ref, tbl_ref, out_ref,
                qaug_ref, s_ref, s1_ref, acc_ref, m_ref, alpha_ref):
    qblk = pl.program_id(1)
    for kv in range(NSA_KV_HEADS):
        unchosen = jnp.where(sel_ref[0, kv].astype(F32) > 0.0, 0.0, NEG)
        for g in range(NSA_GROUP):
            r = kv * NSA_GROUP + g
            rows = slice(r * QB, (r + 1) * QB)
            qaug_ref[rows, 0:LANES] = _masked_q(qb_ref, g, kv)
            qaug_ref[rows, LANES:2 * LANES] = (unchosen + cfar_ref[r:r + 1, :]).astype(BF16)
    acc_ref[...] = jnp.zeros_like(acc_ref)
    m_ref[...] = jnp.full_like(m_ref, NEG)

    last_tile = ks_ref.shape[2] // SEL_KT - 1

    def scores(j, dst_ref):
        start = pl.multiple_of(jnp.minimum(j, last_tile) * SEL_KT, SEL_KT)
        dst_ref[...] = _dot(qaug_ref[...], ks_ref[0, :, pl.ds(start, SEL_KT)])

    def consume(j, src_ref, near):
        start = pl.multiple_of(j * SEL_KT, SEL_KT)
        for r in range(N_ROWGROUPS):
            rows = slice(r * QB, (r + 1) * QB)
            s = src_ref[rows, :]
            if near:
                e1 = jnp.clip(qblk - 2 * j + 1, 0, SEL_NEAR + 1)
                e2 = jnp.clip(qblk - 2 * j, 0, SEL_NEAR + 1)
                s = s + jnp.concatenate([tbl_ref[e1, r], tbl_ref[e2, r]], axis=1)
                src_ref[rows, :] = s
            m_old = m_ref[rows, :]
            m_new = jnp.maximum(m_old, jnp.max(s, axis=1, keepdims=True))
            alpha_ref[rows, :] = jnp.exp2(m_old - m_new)
            m_ref[rows, :] = m_new
        for r in range(N_ROWGROUPS):
            rows = slice(r * QB, (r + 1) * QB)
            vals = (vs0_ref if r < NSA_GROUP else vs1_ref)[0, pl.ds(start, SEL_KT), :]
            m_new = m_ref[rows, :]
            p = jnp.exp2(src_ref[rows, :] - jnp.concatenate([m_new, m_new], axis=1))
            acc_ref[rows, :] = alpha_ref[rows, :] * acc_ref[rows, :] + _dot(p.astype(BF16), vals)

    def tile_run(first, count, near):
        bufs = (s_ref, s1_ref)
        for u in range(count):
            scores(first + u + 1, bufs[(u + 1) % 2])
            consume(first + u, bufs[u % 2], near)

    def run_pairs(first, pairs, near):
        lax.fori_loop(0, pairs // 2, lambda i, c: (tile_run(first + 4 * i, 4, near), c)[1], 0)
        pl.when(pairs % 2 == 1)(lambda: tile_run(first + 4 * (pairs // 2), 2, near))

    n_pairs = ((qblk + 2) // 2 + 1) // 2
    n_far = jnp.maximum((qblk - (SEL_NEAR - 1)) // 2, 0) // 2
    scores(0, s_ref)
    run_pairs(0, n_far, False)
    run_pairs(2 * n_far, n_pairs - n_far, True)
    for g in range(NSA_GROUP):
        ratio = _pair_ratio(acc_ref[g * QB:(g + 1) * QB, :], acc_ref[(NSA_GROUP + g) * QB:(NSA_GROUP + g + 1) * QB, :])
        out_ref[0, :, g * LANES:(g + 1) * LANES] = (ratio * _gate_tile(gates_ref, 1, g)).astype(BF16)


def _selected_branch(qb, sel, gates, ks_t, vs0aug, vs1aug, f_b):
    b, t, _ = qb.shape
    tbl = _sel_bias(f_b)
    cfar = jnp.broadcast_to(f_b[:, BIAS_LEN - 1:], (NSA_HEADS, LANES))
    cfar = jnp.pad(cfar, ((0, 16 - NSA_HEADS), (0, 0)))
    blockq = lambda w: pl.BlockSpec((1, QB, w), lambda bi, i: (bi, i, 0))
    batch = lambda a: pl.BlockSpec((1,) + a.shape[1:], lambda bi, i: (bi, 0, 0))
    rows = N_ROWGROUPS * QB
    return pl.pallas_call(
        _sel_kernel,
        grid=(b, t // QB),
        in_specs=[blockq(DB), pl.BlockSpec((1, NSA_KV_HEADS, QB, LANES), lambda bi, i: (bi, 0, i, 0)), blockq(LANES),
                  pl.BlockSpec(cfar.shape, lambda bi, i: (0, 0)), batch(ks_t), batch(vs0aug), batch(vs1aug),
                  pl.BlockSpec(tbl.shape, lambda bi, i: (0, 0, 0, 0))],
        out_specs=blockq(DB),
        out_shape=jax.ShapeDtypeStruct((b, t, DB), BF16),
        scratch_shapes=[pltpu.VMEM((rows, 2 * LANES), BF16), pltpu.VMEM((rows, SEL_KT), F32), pltpu.VMEM((rows, SEL_KT), F32),
                        pltpu.VMEM((rows, LANES), F32), pltpu.VMEM((rows, LANES), F32), pltpu.VMEM((rows, LANES), F32)],
        compiler_params=pltpu.CompilerParams(dimension_semantics=("parallel", "parallel"), vmem_limit_bytes=VMEM_LIMIT),
        name="nsa_selected",
    )(qb, sel, gates, cfar, ks_t, vs0aug, vs1aug, tbl)


WIN_KEYS = WIN + QB


def _win_bias(f_b):
    lo = WIN - (WIN_KEYS - 1)
    vals = _extend(f_b[:, :WIN], lo, WIN)
    vals = jnp.concatenate([vals, jnp.full((NSA_HEADS, lo + QB + WIN_KEYS - 1 - WIN), NEG, F32)], axis=1)
    return _toeplitz_of(vals, lo, QB, WIN_KEYS)


WIN_QBLOCKS = 2


def _win_kernel(qb_ref, gates_ref, kw_ref, vw0_ref, vw1_ref, tbl_ref, out_ref, q_ref, s_ref):
    col = lax.broadcasted_iota(jnp.int32, (1, WIN_KEYS), 1)
    for h in range(WIN_QBLOCKS):
        qblk = pl.program_id(1) * WIN_QBLOCKS + h
        row0 = h * QB
        for kv in range(NSA_KV_HEADS):
            for g in range(NSA_GROUP):
                r = kv * NSA_GROUP + g
                q_ref[h, r * QB:(r + 1) * QB, :] = _masked_q(qb_ref, g, kv, row0)
        start = pl.multiple_of(qblk * QB, QB)
        s_ref[h] = _nt_dot(q_ref[h], kw_ref[0, pl.ds(start, WIN_KEYS), :])
        pad_mask = jnp.where(col + qblk * QB >= WIN, 0.0, NEG)
        outs = []
        for r in range(N_ROWGROUPS):
            vals = (vw0_ref if r < NSA_GROUP else vw1_ref)[0, pl.ds(start, WIN_KEYS), :]
            s = s_ref[h, r * QB:(r + 1) * QB, :] + tbl_ref[r] + pad_mask
            e = jnp.exp2(s - jnp.max(s, axis=1, keepdims=True))
            outs.append(_dot(e.astype(BF16), vals))
        for g in range(NSA_GROUP):
            out_ref[0, row0:row0 + QB, g * LANES:(g + 1) * LANES] = (
                _pair_ratio(outs[g], outs[NSA_GROUP + g]) * _gate_tile(gates_ref, 2, g, row0)).astype(BF16)


def _window_branch(qb, gates, kw, vw0aug, vw1aug, f_b):
    b, t, _ = qb.shape
    tbl = _win_bias(f_b)
    pad_front = lambda a: jnp.pad(a, ((0, 0), (WIN, 0), (0, 0)))
    kw_pad, vw0_pad, vw1_pad = pad_front(kw), pad_front(vw0aug), pad_front(vw1aug)
    blockq = lambda w: pl.BlockSpec((1, WIN_QBLOCKS * QB, w), lambda bi, i: (bi, i, 0))
    batch = lambda a: pl.BlockSpec((1,) + a.shape[1:], lambda bi, i: (bi, 0, 0))
    rows = N_ROWGROUPS * QB
    return pl.pallas_call(
        _win_kernel,
        grid=(b, t // (WIN_QBLOCKS * QB)),
        in_specs=[blockq(DB), blockq(LANES), batch(kw_pad), batch(vw0_pad), batch(vw1_pad),
                  pl.BlockSpec(tbl.shape, lambda bi, i: (0, 0, 0))],
        out_specs=blockq(DB),
        out_shape=jax.ShapeDtypeStruct((b, t, DB), BF16),
        scratch_shapes=[pltpu.VMEM((WIN_QBLOCKS, rows, LANES), BF16), pltpu.VMEM((WIN_QBLOCKS, rows, WIN_KEYS), F32)],
        compiler_params=pltpu.CompilerParams(dimension_semantics=("parallel", "parallel"), vmem_limit_bytes=VMEM_LIMIT),
        name="nsa_window",
    )(qb, gates, kw_pad, vw0_pad, vw1_pad, tbl)


OUT_TM = 256
C_GROUP = N_EXPERTS


def _outproj_kernel(x_ref, oa_ref, oc_ref, os_ref, ow_ref,
                    wout_ref, g_ref, wr_ref, br_ref, h_ref, hn_ref, comb_ref):
    ob = oc_ref[...].astype(F32) + os_ref[...].astype(F32) + ow_ref[...].astype(F32)
    y = _dot(oa_ref[...], wout_ref[0:DA, :]) + _dot(ob.astype(BF16), wout_ref[DA:DA + DB, :])
    h = x_ref[...] + y
    h_ref[...] = h
    hn = h * lax.rsqrt(jnp.mean(h * h, axis=-1, keepdims=True) + EPS) * g_ref[...]
    hn_hi = hn.astype(BF16)
    hn_ref[...] = hn_hi
    hn_lo = (hn - hn_hi.astype(F32)).astype(BF16)
    both = _dot(hn_hi, wr_ref[...])
    logits = both[:, 0:LANES] + both[:, LANES:2 * LANES] + _dot(hn_lo, wr_ref[:, 0:LANES]) + br_ref[...]
    lane = lax.broadcasted_iota(jnp.int32, logits.shape, 1)
    lane_f = lane.astype(F32)
    big = float(LANES)
    gl = jnp.where((lane >= C_GROUP) & (lane < C_GROUP + N_GROUPS), logits, -jnp.inf)
    gmax = jnp.max(gl, axis=1, keepdims=True)
    gidx = jnp.min(jnp.where(gl == gmax, lane_f, big), axis=1, keepdims=True) - C_GROUP
    gprob = 1.0 / jnp.sum(jnp.exp(gl - gmax), axis=1, keepdims=True)
    grp_of_lane = (lane // EXPERTS_PER_GROUP).astype(F32)
    el = jnp.where((lane < N_EXPERTS) & (grp_of_lane == gidx), logits, -jnp.inf)
    v1 = jnp.max(el, axis=1, keepdims=True)
    i1 = jnp.min(jnp.where(el == v1, lane_f, big), axis=1, keepdims=True)
    el2 = jnp.where(lane_f == i1, -jnp.inf, el)
    v2 = jnp.max(el2, axis=1, keepdims=True)
    i2 = jnp.min(jnp.where(el2 == v2, lane_f, big), axis=1, keepdims=True)
    e2 = jnp.exp(v2 - v1)
    p1 = 1.0 / (1.0 + e2)
    comb_ref[...] = (gprob * (jnp.where(lane_f == i1, p1, 0.0) + jnp.where(lane_f == i2, e2 * p1, 0.0))
                     + jnp.where(lane == C_GROUP, gidx, 0.0))


def _outproj(x2d, o_a, b_parts, w_out_perm, gain, w_router, b_router):
    n, d = x2d.shape
    row = lambda w: pl.BlockSpec((OUT_TM, w), lambda i: (i, 0))
    full = lambda a: pl.BlockSpec(a.shape, lambda i: (0, 0))
    return pl.pallas_call(
        _outproj_kernel,
        grid=(n // OUT_TM,),
        in_specs=[row(d), row(DA)] + [row(DB)] * 3 + [full(w_out_perm), pl.BlockSpec((1, d), lambda i: (0, 0)),
                                                      full(w_router), full(b_router)],
        out_specs=[row(d), row(d), row(LANES)],
        out_shape=[jax.ShapeDtypeStruct((n, d), F32), jax.ShapeDtypeStruct((n, d), BF16),
                   jax.ShapeDtypeStruct((n, LANES), F32)],
        compiler_params=pltpu.CompilerParams(dimension_semantics=("parallel",), vmem_limit_bytes=VMEM_LIMIT),
        name="outproj_router",
    )(x2d, o_a, *b_parts, w_out_perm, gain.reshape(1, d), w_router, b_router)


MOE_TM = 1024


MOE_SUB = 128
MOE_FINAL_TM = 512


def _moe_kernel(hn_ref, comb_ref, wg_ref, wu_ref, wd_ref, y_ref, perm_ref, hs_ref, cs_ref, ys_ref, start_ref, nsub_ref):
    grp = pl.program_id(1)
    tm, d = hn_ref.shape
    n_pad = perm_ref.shape[0]

    @pl.when(grp == 0)
    def _sort():
        comb = comb_ref[...]
        lane_f = lax.broadcasted_iota(jnp.int32, (tm, LANES), 1).astype(F32)
        onehot = jnp.where(lane_f == comb[:, C_GROUP:C_GROUP + 1], 1.0, 0.0).astype(BF16)
        eye8 = jnp.where(lax.broadcasted_iota(jnp.int32, (8, LANES), 0) == lax.broadcasted_iota(jnp.int32, (8, LANES), 1),
                         1.0, 0.0).astype(BF16)
        onehot_t = _nt_dot(eye8, onehot)
        upper = jnp.where(lax.broadcasted_iota(jnp.int32, (tm, tm), 0) <= lax.broadcasted_iota(jnp.int32, (tm, tm), 1),
                          1.0, 0.0).astype(BF16)
        cum_t = _dot(onehot_t.astype(BF16), upper)
        start = jnp.zeros((1, 1), F32)
        pos_t = jnp.zeros((1, tm), F32)
        for k in range(N_GROUPS):
            padded = jnp.ceil(cum_t[k:k + 1, tm - 1:tm] * (1.0 / MOE_SUB)) * MOE_SUB
            pos_t = pos_t + onehot_t[k:k + 1, :] * (start + cum_t[k:k + 1, :] - 1.0)
            start_ref[k] = start[0, 0].astype(jnp.int32)
            nsub_ref[k] = (padded[0, 0] * (1.0 / MOE_SUB)).astype(jnp.int32)
            start = start + padded
        perm = jnp.where(lax.broadcasted_iota(jnp.int32, (n_pad, tm), 0) == pos_t.astype(jnp.int32), 1.0, 0.0).astype(BF16)
        perm_ref[...] = perm
        hs_ref[...] = _dot(perm, hn_ref[...]).astype(BF16)
        c_hi = comb.astype(BF16)
        rest = comb - c_hi.astype(F32)
        c_mid = rest.astype(BF16)
        c_lo = (rest - c_mid.astype(F32)).astype(BF16)
        cs_ref[...] = _dot(perm, c_hi) + _dot(perm, c_mid) + _dot(perm, c_lo)
        ys_ref[...] = jnp.zeros_like(ys_ref)

    lane = lax.broadcasted_iota(jnp.int32, (MOE_SUB, LANES), 1)

    def segment(s, carry):
        rows = pl.ds(pl.multiple_of(start_ref[grp] + s * MOE_SUB, MOE_SUB), MOE_SUB)
        x = hs_ref[rows, :]
        weights = cs_ref[rows, :]
        acc = jnp.zeros((MOE_SUB, d), F32)
        for j in range(EXPERTS_PER_GROUP):
            gate = _dot(x, wg_ref[j])
            up = _dot(x, wu_ref[j])
            w = jnp.sum(jnp.where(lane == grp * EXPERTS_PER_GROUP + j, weights, 0.0), axis=1, keepdims=True)
            acc = acc + _dot((gate * jax.nn.sigmoid(gate) * up * w).astype(BF16), wd_ref[j])
        ys_ref[rows, :] = acc.astype(BF16)
        return carry

    lax.fori_loop(0, nsub_ref[grp], segment, 0)

    @pl.when(grp == pl.num_programs(1) - 1)
    def _unsort():
        y_ref[...] = lax.dot_general(perm_ref[...], ys_ref[...], (((0,), (0,)), ((), ())),
                                     preferred_element_type=F32).astype(BF16)


def _final_kernel(h_ref, y_ref, g_ref, out_ref):
    y = h_ref[...] + y_ref[...].astype(F32)
    out_ref[...] = y * lax.rsqrt(jnp.mean(y * y, axis=-1, keepdims=True) + EPS) * g_ref[...]


def _moe(h, hn, comb, w_gate, w_up, w_down, gain):
    n, d = h.shape
    tm = min(MOE_TM, n)
    n_pad = tm + (N_GROUPS - 1) * MOE_SUB
    row = lambda w: pl.BlockSpec((tm, w), lambda i, g: (i, 0))
    group_w = lambda a: pl.BlockSpec((EXPERTS_PER_GROUP,) + a.shape[1:], lambda i, g: (g, 0, 0))
    y = pl.pallas_call(
        _moe_kernel,
        grid=(n // tm, N_GROUPS),
        in_specs=[row(d), row(LANES), group_w(w_gate), group_w(w_up), group_w(w_down)],
        out_specs=row(d),
        out_shape=jax.ShapeDtypeStruct((n, d), BF16),
        scratch_shapes=[pltpu.VMEM((n_pad, tm), BF16), pltpu.VMEM((n_pad, d), BF16), pltpu.VMEM((n_pad, LANES), F32),
                        pltpu.VMEM((n_pad, d), BF16), pltpu.SMEM((N_GROUPS,), jnp.int32), pltpu.SMEM((N_GROUPS,), jnp.int32)],
        compiler_params=pltpu.CompilerParams(dimension_semantics=("parallel", "arbitrary"), vmem_limit_bytes=VMEM_LIMIT),
        name="moe_experts",
    )(hn, comb, w_gate, w_up, w_down)
    rowf = lambda w: pl.BlockSpec((MOE_FINAL_TM, w), lambda i: (i, 0))
    return pl.pallas_call(
        _final_kernel,
        grid=(n // MOE_FINAL_TM,),
        in_specs=[rowf(d), rowf(d), pl.BlockSpec((1, d), lambda i: (0, 0))],
        out_specs=rowf(d),
        out_shape=jax.ShapeDtypeStruct((n, d), F32),
        compiler_params=pltpu.CompilerParams(dimension_semantics=("parallel",), vmem_limit_bytes=VMEM_LIMIT),
        name="residual_final_norm",
    )(h, y, gain.reshape(1, d))


def _permute_w_out(w_out):
    d = w_out.shape[1]
    wb = w_out[DA:].reshape(NSA_KV_HEADS, NSA_GROUP, HEAD_DIM, d).transpose(1, 0, 2, 3).reshape(DB, d)
    return jnp.concatenate([w_out[:DA], wb], axis=0).astype(BF16)


def _router_weights(w_group, b_group, w_expert, b_expert):
    d = w_group.shape[0]
    w = jnp.concatenate([w_expert.reshape(d, N_EXPERTS), w_group], axis=1)
    b = jnp.concatenate([b_expert.reshape(N_EXPERTS), b_group])
    pad = LANES - w.shape[1]
    w = jnp.pad(w, ((0, 0), (0, pad))).astype(F32)
    w_hi = w.astype(BF16)
    w_lo = (w - w_hi.astype(F32)).astype(BF16)
    return jnp.concatenate([w_hi, w_lo], axis=1), jnp.pad(b, (0, pad)).reshape(1, LANES).astype(F32)


def _layer(h, rel_bias, norm_mix, w_in, w_out, cmp_pos_k, cmp_pos_v, cmp_k_w1, cmp_k_w2, cmp_v_w1, cmp_v_w2,
           norm_ffn, w_rg, b_rg, w_re, b_re, w_gate, w_up, w_down, out_gain):
    b, t, d = h.shape
    n = b * t
    assert t % (QB * DIL_PATTERNS[-1][1]) == 0 and t // SEL_BLOCK <= LANES and n % MOE_TM == 0
    x2d = h.reshape(n, d)
    seq = lambda a: a if a.ndim == 3 else a.reshape(b, a.shape[0] // b, a.shape[-1])
    qa, ka, va, qb, kc, vc, ks_t, vs0aug, vs1aug, kw, vw0aug, vw1aug, gates = map(
        seq, _inproj(x2d, norm_mix, _permute_w_in(w_in), t))
    f_a = _bias_1d(rel_bias[:, :DIL_HEADS])
    f_b = _bias_1d(rel_bias[:, DIL_HEADS:]) * LOG2E
    o_a = _mixer_a(qa, ka, va, jnp.stack([_dil_bias(f_a, dil) for _, dil in DIL_PATTERNS]))
    kcmp, vcmp = _compress(kc, vc, cmp_pos_k, cmp_pos_v, cmp_k_w1, cmp_k_w2, cmp_v_w1, cmp_v_w2)
    o_cmp, sel = _compressed_branch(qb, kcmp, vcmp, gates, f_b)
    o_sel = _selected_branch(qb, sel, gates, ks_t, vs0aug, vs1aug, f_b)
    o_win = _window_branch(qb, gates, kw, vw0aug, vw1aug, f_b)
    b_parts = [o.reshape(n, DB) for o in (o_cmp, o_sel, o_win)]
    w_router, b_router = _router_weights(w_rg, b_rg, w_re, b_re)
    h2, hn, comb = _outproj(x2d, o_a.reshape(n, DA), b_parts, _permute_w_out(w_out), norm_ffn, w_router, b_router)
    return _moe(h2, hn, comb, w_gate.astype(BF16), w_up.astype(BF16), w_down.astype(BF16), out_gain)


def kernel(x, rel_bias, norm_mix, w_in, w_out, cmp_pos_k, cmp_pos_v, cmp_k_w1, cmp_k_w2, cmp_v_w1, cmp_v_w2,
           norm_ffn, w_router_group, b_router_group, w_router_expert, b_router_expert, w_gate, w_up, w_down,
           norm_final):
    depth = norm_mix.shape[0]
    assert depth == 1, "the final RMSNorm is fused into the last layer's expert kernel"
    out = _layer(x, rel_bias, norm_mix[0], w_in[0], w_out[0], cmp_pos_k[0], cmp_pos_v[0], cmp_k_w1[0], cmp_k_w2[0],
                 cmp_v_w1[0], cmp_v_w2[0], norm_ffn[0], w_router_group[0], b_router_group[0], w_router_expert[0],
                 b_router_expert[0], w_gate[0], w_up[0], w_down[0], norm_final)
    return out.reshape(x.shape)
```

```python
import functools
import math

import jax
import jax.numpy as jnp
import numpy as np
from jax import lax
from jax.experimental import pallas as pl
from jax.experimental.pallas import tpu as pltpu

HEAD_DIM = 64
DIL_HEADS = 6
NSA_KV_HEADS = 2
NSA_GROUP = 5
NSA_HEADS = NSA_KV_HEADS * NSA_GROUP
N_HEADS = DIL_HEADS + NSA_HEADS
DIL_PATTERNS = ((128, 1), (512, 4), (2048, 16))
CMP_BLOCK = 32
CMP_STRIDE = 16
CMP_HIDDEN = 256
SEL_BLOCK = 64
SEL_TOPK = 16
WIN = 512
N_FORCED = 3
N_BUCKETS = 32
MAX_DISTANCE = 2048
N_GROUPS = 4
EXPERTS_PER_GROUP = 4
N_EXPERTS = N_GROUPS * EXPERTS_PER_GROUP
D_EXPERT = 512
EPS = 1e-6

LANES = 128
QB = 128
NEG = -1.0e30
LOG2E = math.log2(math.e)
DA = DIL_HEADS * HEAD_DIM
DB = NSA_HEADS * HEAD_DIM
N_ROWGROUPS = NSA_HEADS
SEL_KT = 256
VMEM_LIMIT = 56 * 1024 * 1024

F32 = jnp.float32
BF16 = jnp.bfloat16
NT_DIMS = (((1,), (1,)), ((), ()))


def _nt_dot(a, b):
    return lax.dot_general(a, b, NT_DIMS, preferred_element_type=F32)


def _dot(a, b):
    return jnp.dot(a, b, preferred_element_type=F32)


def _bucket_np(dist):
    dist = np.maximum(np.asarray(dist, np.int64), 0)
    max_exact = N_BUCKETS // 2
    x = np.maximum(dist, 1).astype(np.float32) / np.float32(max_exact)
    large = max_exact + (np.log(x) / np.float32(math.log(MAX_DISTANCE / max_exact))
                         * np.float32(N_BUCKETS - max_exact)).astype(np.int32)
    large = np.minimum(large, N_BUCKETS - 1)
    return np.where(dist < max_exact, dist, large).astype(np.int32)


BIAS_LEN = 4096


def _bias_1d(rel_bias_heads):
    onehot = (_bucket_np(np.arange(BIAS_LEN))[None, :] == np.arange(N_BUCKETS)[:, None]).astype(np.float32)
    return jnp.dot(rel_bias_heads.T.astype(F32), jnp.asarray(onehot), precision=lax.Precision.HIGHEST)


def _extend(f, lo, hi):
    assert hi <= f.shape[-1]
    if lo >= 0:
        return f[..., lo:hi]
    pad = jnp.full(f.shape[:-1] + (-lo,), NEG, f.dtype)
    return jnp.concatenate([pad, f[..., :hi]], axis=-1)


def _toeplitz(w, q, c):
    n = q + c - 1
    assert w.shape[-1] == n
    lead = w.shape[:-1]
    wp = jnp.concatenate([w, jnp.zeros(lead + (1,), w.dtype)], axis=-1)
    flat = jnp.broadcast_to(wp[..., None, :], lead + (q, n + 1)).reshape(lead + (q * (n + 1),))
    return flat[..., :q * n].reshape(lead + (q, n))[..., q - 1:q - 1 + c]


def _toeplitz_of(fn_vals, lo, q, c):
    return _toeplitz(jnp.flip(fn_vals, axis=-1), q, c)


IN_TM = 512
C_QA, C_KA, C_VA = 0, DA, 2 * DA
C_QB = 3 * DA
C_KC = C_QB + DB
C_VC, C_KS, C_VS, C_KW, C_VW, C_GT = (C_KC + LANES * i for i in range(1, 7))
N_COLS = C_GT + LANES


def _permute_w_in(w_in):
    scale = 1.0 / math.sqrt(HEAD_DIM)
    sizes = [DA] * 3 + [DB] + [NSA_KV_HEADS * HEAD_DIM] * 6 + [3 * NSA_HEADS]
    offs = np.concatenate([[0], np.cumsum(sizes)])
    part = lambda i: w_in[:, offs[i]:offs[i + 1]]
    d = w_in.shape[0]
    qb = part(3).reshape(d, NSA_KV_HEADS, NSA_GROUP, HEAD_DIM).transpose(0, 2, 1, 3).reshape(d, DB)
    gt = part(10).reshape(d, NSA_KV_HEADS, NSA_GROUP, 3).transpose(0, 3, 2, 1).reshape(d, 3 * NSA_HEADS)
    gt = jnp.pad(gt, ((0, 0), (0, LANES - 3 * NSA_HEADS)))
    cols = [part(0) * scale, part(1), part(2), qb * (scale * LOG2E)] + [part(i) for i in range(4, 10)] + [gt]
    return jnp.concatenate(cols, axis=1).astype(BF16)


def _inproj_kernel(seq_len, x_ref, g_ref, w_ref, qa_ref, ka_ref, va_ref, qb_ref, kc_ref, vc_ref,
                   kst_ref, vs0_ref, vs1_ref, kw_ref, vw0_ref, vw1_ref, gates_ref, stage_ref):
    x = x_ref[...]
    xn = (x * lax.rsqrt(jnp.mean(x * x, axis=-1, keepdims=True) + EPS) * g_ref[...]).astype(BF16)
    seg = lambda a, n: _dot(xn, w_ref[:, a:a + n])
    qa_ref[...] = seg(C_QA, DA).astype(BF16)
    ka_ref[...] = seg(C_KA, DA).astype(BF16)
    va_ref[...] = seg(C_VA, DA).astype(BF16)
    qb_ref[...] = seg(C_QB, DB).astype(BF16)
    kw_ref[...] = seg(C_KW, LANES).astype(BF16)
    tm = x.shape[0]
    for col, out_ref in ((C_KC, kc_ref), (C_VC, vc_ref)):
        stage_ref[...] = seg(col, LANES)
        for j in range(CMP_STRIDE):
            out_ref[:, j * LANES:(j + 1) * LANES] = stage_ref[pl.ds(j, tm // CMP_STRIDE, stride=CMP_STRIDE), :].astype(BF16)
    tok_t = (pl.program_id(0) * tm) % seq_len + lax.broadcasted_iota(jnp.int32, (LANES, tm), 1)
    blk_t = lax.broadcasted_iota(jnp.int32, (LANES, tm), 0)
    stage_ref[...] = seg(C_KS, LANES)
    kst_ref[0, 0:LANES, :] = stage_ref[...].T.astype(BF16)
    kst_ref[0, LANES:2 * LANES, :] = jnp.where(blk_t == tok_t // SEL_BLOCK, 1.0, 0.0).astype(BF16)
    low = lax.broadcasted_iota(jnp.int32, (tm, LANES), 1) < HEAD_DIM
    for col, ref0, ref1 in ((C_VS, vs0_ref, vs1_ref), (C_VW, vw0_ref, vw1_ref)):
        v = seg(col, LANES)
        ref0[...] = jnp.where(low, v, 1.0).astype(BF16)
        ref1[...] = jnp.where(low, 1.0, v).astype(BF16)
    gates_ref[...] = jax.nn.sigmoid(seg(C_GT, LANES))


def _inproj(x2d, gain, w_perm, seq_len):
    n, d = x2d.shape
    row = lambda w: pl.BlockSpec((IN_TM, w), lambda i: (i, 0))
    rows = lambda w: (jax.ShapeDtypeStruct((n, w), BF16), row(w))
    chunks = (jax.ShapeDtypeStruct((n // CMP_STRIDE, CMP_STRIDE * LANES), BF16),
              pl.BlockSpec((IN_TM // CMP_STRIDE, CMP_STRIDE * LANES), lambda i: (i, 0)))
    per_seq = seq_len // IN_TM
    keys_t = (jax.ShapeDtypeStruct((n // seq_len, 2 * LANES, seq_len), BF16),
              pl.BlockSpec((1, 2 * LANES, IN_TM), lambda i: (i // per_seq, 0, i % per_seq)))
    outs = [rows(DA), rows(DA), rows(DA), rows(DB), chunks, chunks, keys_t] + [rows(LANES)] * 5
    outs.append((jax.ShapeDtypeStruct((n, LANES), F32), row(LANES)))
    return pl.pallas_call(
        functools.partial(_inproj_kernel, seq_len),
        grid=(n // IN_TM,),
        in_specs=[row(d), pl.BlockSpec((1, d), lambda i: (0, 0)), pl.BlockSpec((d, N_COLS), lambda i: (0, 0))],
        out_specs=[spec for _, spec in outs],
        out_shape=[shape for shape, _ in outs],
        scratch_shapes=[pltpu.VMEM((IN_TM, LANES), F32)],
        compiler_params=pltpu.CompilerParams(dimension_semantics=("parallel",), vmem_limit_bytes=VMEM_LIMIT),
        name="inproj",
    )(x2d, gain.reshape(1, d), w_perm)


def _embed_pair(w, n_tok):
    c = w.shape[1]
    w4 = w.reshape(n_tok, 1, HEAD_DIM, 1, c) * jnp.eye(NSA_KV_HEADS, dtype=w.dtype).reshape(1, 2, 1, 2, 1)
    return w4.reshape(n_tok * 2 * HEAD_DIM, 2 * c)


def _gelu_tanh(x):
    return 0.5 * x * (1.0 + jnp.tanh(math.sqrt(2.0 / math.pi) * (x + 0.044715 * (x * x * x))))


def _compress_kernel(ck_ref, cv_ref, posk_ref, posv_ref, wk1a, wk1b, wk2, wv1a, wv1b, wv2,
                     kout_ref, vout_ref, shift_ref):
    ncp = ck_ref.shape[1]
    for c_ref, pos_ref, w1a, w1b, w2, out_ref in ((ck_ref, posk_ref, wk1a, wk1b, wk2, kout_ref),
                                                  (cv_ref, posv_ref, wv1a, wv1b, wv2, vout_ref)):
        c = c_ref[0].astype(F32)
        first = _dot((c + pos_ref[0:1, :]).astype(BF16), w1a[...])
        second = _dot((c + pos_ref[1:2, :]).astype(BF16), w1b[...])
        shift_ref[0:ncp, :] = second
        shift_ref[ncp:ncp + 8, :] = jnp.zeros((8, second.shape[1]), F32)
        hidden = _gelu_tanh(first + shift_ref[1:ncp + 1, :])
        out_ref[0] = _dot(hidden.astype(BF16), w2[...]).astype(BF16)


def _compress(kc, vc, pos_k, pos_v, k_w1, k_w2, v_w1, v_w2):
    b, ncp, wide = kc.shape
    half = CMP_STRIDE * HEAD_DIM

    def prep(w1, w2, pos):
        pos_pair = jnp.broadcast_to(pos.reshape(2, CMP_STRIDE, 1, HEAD_DIM), (2, CMP_STRIDE, 2, HEAD_DIM))
        return (_embed_pair(w1[:half], CMP_STRIDE).astype(BF16), _embed_pair(w1[half:], CMP_STRIDE).astype(BF16),
                jnp.kron(jnp.eye(NSA_KV_HEADS, dtype=w2.dtype), w2).astype(BF16),
                pos_pair.reshape(2, wide).astype(F32))

    wk1a, wk1b, wk2, posk = prep(k_w1, k_w2, pos_k)
    wv1a, wv1b, wv2, posv = prep(v_w1, v_w2, pos_v)
    full = lambda a: pl.BlockSpec(a.shape, lambda i: (0,) * a.ndim)
    tok = pl.BlockSpec((1, ncp, wide), lambda i: (i, 0, 0))
    out = pl.BlockSpec((1, ncp, LANES), lambda i: (i, 0, 0))
    return pl.pallas_call(
        _compress_kernel,
        grid=(b,),
        in_specs=[tok, tok, full(posk), full(posv), full(wk1a), full(wk1b), full(wk2), full(wv1a), full(wv1b), full(wv2)],
        out_specs=[out, out],
        out_shape=[jax.ShapeDtypeStruct((b, ncp, LANES), BF16)] * 2,
        scratch_shapes=[pltpu.VMEM((ncp + 8, 2 * CMP_HIDDEN), F32)],
        compiler_params=pltpu.CompilerParams(dimension_semantics=("parallel",), vmem_limit_bytes=VMEM_LIMIT),
        name="compress",
    )(kc, vc, posk, posv, wk1a, wk1b, wk2, wv1a, wv1b, wv2)


def _pair_masks(rows):
    lane = lax.broadcasted_iota(jnp.int32, (rows, LANES), 1)
    return lane < HEAD_DIM


A_SUPER = QB * DIL_PATTERNS[-1][1]


def _mixer_a_kernel(q_ref, kp_ref, kc_ref, vp_ref, vc_ref, bias_ref, out_ref, qf_ref, kf_ref, vf_ref, o_ref, lse_ref):
    first = pl.program_id(1) == 0
    n_pairs = DIL_HEADS // 2
    for p in range(n_pairs):
        cs = slice(p * LANES, (p + 1) * LANES)
        qf_ref[p] = q_ref[0, :, cs].astype(F32)
        kf_ref[p, 0:A_SUPER, :] = kp_ref[0, :, cs].astype(F32)
        kf_ref[p, A_SUPER:2 * A_SUPER, :] = kc_ref[0, :, cs].astype(F32)
        vf_ref[p, 0:A_SUPER, :] = vp_ref[0, :, cs].astype(F32)
        vf_ref[p, A_SUPER:2 * A_SUPER, :] = vc_ref[0, :, cs].astype(F32)
    low = _pair_masks(QB)
    in_prev = lax.broadcasted_iota(jnp.int32, (QB, 2 * QB), 1) < QB
    zero = jnp.zeros((QB, LANES), BF16)

    def chunk(idx, dil, q_base, k_base, q_span, k_span, off, at_start):
        rows = lambda size: pl.ds(off, size) if dil == 1 else pl.ds(off, size, stride=dil)
        q_win = lambda ref: ref.at[pl.ds(pl.multiple_of(q_base, 8), q_span), :]
        k_win = lambda ref: ref.at[pl.ds(pl.multiple_of(k_base, 8), k_span), :]
        q_rows = rows(QB)
        prev_mask = jnp.where(jnp.logical_and(in_prev, jnp.logical_and(first, at_start)), NEG, 0.0)
        for p in range(n_pairs):
            q = q_win(qf_ref.at[p])[q_rows, :].astype(BF16)
            keys = k_win(kf_ref.at[p])[rows(2 * QB), :].astype(BF16)
            vals = k_win(vf_ref.at[p])[rows(2 * QB), :].astype(BF16)
            o_win, lse_win = q_win(o_ref.at[p]), q_win(lse_ref.at[p])
            lhs = jnp.concatenate([jnp.where(low, q, zero), jnp.where(low, zero, q)], axis=0)
            bias = jnp.concatenate([bias_ref[idx, 2 * p] + prev_mask, bias_ref[idx, 2 * p + 1] + prev_mask], axis=0)
            s = _nt_dot(lhs, keys) + bias
            m = jnp.max(s, axis=1, keepdims=True)
            e = jnp.exp(s - m)
            l = jnp.sum(e, axis=1, keepdims=True)
            pv = _dot(e.astype(BF16), vals) * (1.0 / l)
            lse = m + jnp.log(l)
            o_new = jnp.where(low, pv[:QB], pv[QB:])
            l_new = jnp.where(low, jnp.broadcast_to(lse[:QB], (QB, LANES)), jnp.broadcast_to(lse[QB:], (QB, LANES)))
            if idx > 0:
                o_old, l_old = o_win[q_rows, :], lse_win[q_rows, :]
                mx = jnp.maximum(l_old, l_new)
                w_old, w_new = jnp.exp(l_old - mx), jnp.exp(l_new - mx)
                tot = w_old + w_new
                o_new = (w_old * o_old + w_new * o_new) * (1.0 / tot)
                l_new = mx + jnp.log(tot)
            o_win[q_rows, :] = o_new
            lse_win[q_rows, :] = l_new

    def loop(n, body):
        lax.fori_loop(0, n, lambda i, carry: (body(i), carry)[1], 0, unroll=2)

    for idx, (_, dil) in enumerate(DIL_PATTERNS):
        span = QB * dil
        n_chunks = A_SUPER // span
        if n_chunks > 1:
            for r in range(dil):
                loop(n_chunks, lambda c, idx=idx, dil=dil, span=span, r=r:
                     chunk(idx, dil, span * c, A_SUPER + span * (c - 1), span, 2 * span, r, c == 0))
        else:
            for r in range(8):
                loop(dil // 8, lambda hi, idx=idx, dil=dil, span=span, r=r:
                     chunk(idx, dil, 8 * hi, 8 * hi, span - 8, 2 * span - 8, r, True))
    for p in range(n_pairs):
        out_ref[0, :, p * LANES:(p + 1) * LANES] = o_ref[p].astype(BF16)


def _mixer_a(qa, ka, va, bias):
    b, t, _ = qa.shape
    cur = pl.BlockSpec((1, A_SUPER, DA), lambda bi, i: (bi, i, 0))
    prev = pl.BlockSpec((1, A_SUPER, DA), lambda bi, i: (bi, jnp.maximum(i - 1, 0), 0))
    return pl.pallas_call(
        _mixer_a_kernel,
        grid=(b, t // A_SUPER),
        in_specs=[cur, prev, cur, prev, cur, pl.BlockSpec(bias.shape, lambda bi, i: (0, 0, 0, 0))],
        out_specs=cur,
        out_shape=jax.ShapeDtypeStruct((b, t, DA), BF16),
        scratch_shapes=[pltpu.VMEM((DIL_HEADS // 2, rows, LANES), F32) for rows in (A_SUPER, 2 * A_SUPER, 2 * A_SUPER, A_SUPER, A_SUPER)],
        compiler_params=pltpu.CompilerParams(dimension_semantics=("parallel", "parallel"), vmem_limit_bytes=VMEM_LIMIT),
        name="mixer_a",
    )(qa, ka, ka, va, va, bias)


def _dil_bias(f_a, dil):
    steps = DIL_PATTERNS[0][0]
    g = f_a[:, 0:dil * steps + 1:dil]
    lo, hi = QB - (2 * QB - 1), QB + QB
    vals = jnp.concatenate([jnp.full((DIL_HEADS, -lo), NEG, F32), g, jnp.full((DIL_HEADS, hi - steps - 1), NEG, F32)], axis=1)
    return _toeplitz_of(vals, lo, QB, 2 * QB)


CMP_TILE_KEYS = LANES
CMP_TILE_SPAN = CMP_TILE_KEYS * CMP_STRIDE // QB
CMP_CONST_DELTA = 28


def _cmp_bias(f_b):
    per = QB // CMP_STRIDE
    n_rows = per * (CMP_CONST_DELTA + 1)
    m_lo, m_hi = -(CMP_TILE_KEYS - 1), n_rows
    f_b = f_b.astype(BF16)
    base = _extend(f_b, CMP_STRIDE * m_lo - (CMP_BLOCK - 1), CMP_STRIDE * m_hi - (CMP_BLOCK - 1))
    g = base.reshape(NSA_HEADS, m_hi - m_lo, CMP_STRIDE).transpose(0, 2, 1)
    t = _toeplitz_of(g, m_lo, n_rows, CMP_TILE_KEYS)
    t = t.reshape(NSA_HEADS, CMP_STRIDE, CMP_CONST_DELTA + 1, per, CMP_TILE_KEYS).transpose(2, 0, 3, 1, 4)
    t = t.reshape(CMP_CONST_DELTA + 1, NSA_HEADS, QB, CMP_TILE_KEYS)
    return jnp.concatenate([jnp.full((1,) + t.shape[1:], NEG, t.dtype), t], axis=0)


def _overlap_matrix_t(ncp, n_sel_pad):
    n = np.arange(ncp)[None, :] * CMP_STRIDE
    s = np.arange(n_sel_pad)[:, None] * SEL_BLOCK
    ov = np.clip(np.minimum(n + CMP_BLOCK, s + SEL_BLOCK) - np.maximum(n, s), 0, None) / CMP_BLOCK
    return jnp.asarray(ov, BF16)


def _gate_tile(gates_ref, branch, g, row0=0):
    c = branch * NSA_HEADS + g * 2
    low = _pair_masks(QB)
    return jnp.where(low, jnp.broadcast_to(gates_ref[0, row0:row0 + QB, c:c + 1], (QB, LANES)),
                     jnp.broadcast_to(gates_ref[0, row0:row0 + QB, c + 1:c + 2], (QB, LANES)))


def _masked_q(qb_ref, g, kv, row0=0):
    q = qb_ref[0, row0:row0 + QB, g * LANES:(g + 1) * LANES]
    low = _pair_masks(QB)
    keep = low if kv == 0 else jnp.logical_not(low)
    return jnp.where(keep, q, jnp.zeros_like(q))


def _cmp_kernel(n_tiles, qb_ref, kcmp_ref, vcmp_ref, gates_ref, ov_ref, *rest):
    tbl_refs, (oc_ref, sel_ref, q_ref, s_ref, p_ref, pv_ref) = rest[:n_tiles], rest[n_tiles:]
    qblk = pl.program_id(1)
    t0 = qblk * QB
    low = _pair_masks(QB)
    for kv in range(NSA_KV_HEADS):
        for g in range(NSA_GROUP):
            r = kv * NSA_GROUP + g
            q_ref[r * QB:(r + 1) * QB, :] = _masked_q(qb_ref, g, kv)

    def attend(n_vis):
        kc = n_vis * CMP_TILE_KEYS
        n_blk = n_vis * CMP_TILE_SPAN * QB // SEL_BLOCK
        s_ref[:, 0:kc] = _nt_dot(q_ref[...], kcmp_ref[0, 0:kc, :])
        blk = lax.broadcasted_iota(jnp.int32, (n_blk, QB), 0)
        cur = (t0 + lax.broadcasted_iota(jnp.int32, (n_blk, QB), 1)) // SEL_BLOCK
        blk_f = blk.astype(F32)
        forced = (blk == cur) | (blk == cur - 1) | (blk == 0)
        causal = blk <= cur
        scores = []
        for kv in range(NSA_KV_HEADS):
            psum = jnp.zeros((QB, kc), F32)
            for g in range(NSA_GROUP):
                r = kv * NSA_GROUP + g
                rows = slice(r * QB, (r + 1) * QB)
                s = s_ref[rows, 0:kc] + jnp.concatenate([tbl_refs[c][0, r].astype(F32) for c in range(n_vis)], axis=1)
                m = jnp.max(s, axis=1, keepdims=True)
                e = jnp.exp2(s - m)
                den = jnp.sum(e, axis=1, keepdims=True)
                p = jnp.where(m > 0.5 * NEG, e * (1.0 / den), 0.0)
                psum = psum + p
                p_ref[rows, 0:kc] = p.astype(BF16)
            hi = psum.astype(BF16)
            lo = (psum - hi.astype(F32)).astype(BF16)
            ov_t = ov_ref[0:n_blk, 0:kc]
            imp_t = _nt_dot(ov_t, hi) + _nt_dot(ov_t, lo)
            scores.append(jnp.where(forced, -jnp.inf, jnp.where(causal, imp_t, -1.0)))
        pv_ref[...] = _dot(p_ref[:, 0:kc], vcmp_ref[0, 0:kc, :])

        def pick(_, carry):
            new = []
            for val, sel in carry:
                mx = jnp.max(val, axis=0, keepdims=True)
                idx = jnp.min(jnp.where(val == mx, blk_f, float(n_blk)), axis=0, keepdims=True)
                hit = blk_f == idx
                new.append((jnp.where(hit, -jnp.inf, val), jnp.where(hit, 1.0, sel)))
            return tuple(new)

        taken = jnp.where(forced, 1.0, 0.0)
        picked = lax.fori_loop(0, SEL_TOPK - N_FORCED, pick, tuple((v, taken) for v in scores))
        eye = jnp.where(lax.broadcasted_iota(jnp.int32, (QB, QB), 0) == lax.broadcasted_iota(jnp.int32, (QB, QB), 1),
                        1.0, 0.0).astype(BF16)
        for kv in range(NSA_KV_HEADS):
            sel_t = jnp.where(causal, picked[kv][1], 0.0).astype(BF16)
            sel_ref[0, kv, :, 0:n_blk] = _nt_dot(eye, sel_t).astype(BF16)
            if n_blk < LANES:
                sel_ref[0, kv, :, n_blk:LANES] = jnp.zeros((QB, LANES - n_blk), BF16)

    n_vis = qblk // CMP_TILE_SPAN + 1
    for w in range(1, n_tiles + 1):
        pl.when(n_vis == w)(functools.partial(attend, w))

    for g in range(NSA_GROUP):
        o0, o1 = pv_ref[g * QB:(g + 1) * QB, :], pv_ref[(NSA_GROUP + g) * QB:(NSA_GROUP + g + 1) * QB, :]
        oc_ref[0, :, g * LANES:(g + 1) * LANES] = (jnp.where(low, o0, o1) * _gate_tile(gates_ref, 0, g)).astype(BF16)


def _compressed_branch(qb, kcmp, vcmp, gates, f_b):
    b, t, _ = qb.shape
    ncp = kcmp.shape[1]
    n_tiles = ncp // CMP_TILE_KEYS
    tbl = _cmp_bias(f_b)
    ov = _overlap_matrix_t(ncp, LANES)

    def tbl_spec(c):
        return pl.BlockSpec((1, NSA_HEADS, QB, CMP_TILE_KEYS),
                            lambda bi, i: (jnp.clip(i - CMP_TILE_SPAN * c, -1, CMP_CONST_DELTA) + 1, 0, 0, 0))

    blockq = lambda w: pl.BlockSpec((1, QB, w), lambda bi, i: (bi, i, 0))
    batch = lambda a: pl.BlockSpec((1,) + a.shape[1:], lambda bi, i: (bi, 0, 0))
    return pl.pallas_call(
        functools.partial(_cmp_kernel, n_tiles),
        grid=(b, t // QB),
        in_specs=[blockq(DB), batch(kcmp), batch(vcmp), blockq(LANES), pl.BlockSpec(ov.shape, lambda bi, i: (0, 0))]
                 + [tbl_spec(c) for c in range(n_tiles)],
        out_specs=[blockq(DB), pl.BlockSpec((1, NSA_KV_HEADS, QB, LANES), lambda bi, i: (bi, 0, i, 0))],
        out_shape=[jax.ShapeDtypeStruct((b, t, DB), BF16), jax.ShapeDtypeStruct((b, NSA_KV_HEADS, t, LANES), BF16)],
        scratch_shapes=[pltpu.VMEM((N_ROWGROUPS * QB, LANES), BF16), pltpu.VMEM((N_ROWGROUPS * QB, ncp), F32),
                        pltpu.VMEM((N_ROWGROUPS * QB, ncp), BF16), pltpu.VMEM((N_ROWGROUPS * QB, LANES), F32)],
        compiler_params=pltpu.CompilerParams(dimension_semantics=("parallel", "parallel"), vmem_limit_bytes=VMEM_LIMIT),
        name="nsa_compressed",
    )(qb, kcmp, vcmp, gates, ov, *([tbl] * n_tiles))


SEL_NEAR = 13
SEL_FAR_BLOCK, SEL_NEAR_BLOCK = 8, 4


def _sel_bias(f_b):
    n_off = SEL_NEAR + 1
    cols = QB * n_off
    rel = f_b - f_b[:, BIAS_LEN - 1:]
    lo = -(QB - 1)
    big = _toeplitz_of(_extend(rel, lo, lo + QB + cols - 1), lo, QB, cols)
    tiles = jnp.flip(big.reshape(NSA_HEADS, QB, n_off, QB).transpose(2, 0, 1, 3), axis=0)
    return jnp.concatenate([jnp.zeros((1,) + tiles.shape[1:], F32), tiles], axis=0)


def _pair_ratio(acc0, acc1):
    low = _pair_masks(acc0.shape[0])
    den = pltpu.roll(jnp.where(low, acc1, acc0), HEAD_DIM, axis=1)
    return jnp.where(low, acc0, acc1) * (1.0 / den)


def _sel_kernel(qb_ref, sel_ref, gates_ref, cfar_ref, ks_ref, vs0_ref, vs1_ref, tbl_ref, out_ref,
                qaug_ref, s_ref, s1_ref, acc_ref, m_ref, alpha_ref):
    qblk = pl.program_id(1)
    for kv in range(NSA_KV_HEADS):
        unchosen = jnp.where(sel_ref[0, kv].astype(F32) > 0.0, 0.0, NEG)
        for g in range(NSA_GROUP):
            r = kv * NSA_GROUP + g
            rows = slice(r * QB, (r + 1) * QB)
            qaug_ref[rows, 0:LANES] = _masked_q(qb_ref, g, kv)
            qaug_ref[rows, LANES:2 * LANES] = (unchosen + cfar_ref[r:r + 1, :]).astype(BF16)
    acc_ref[...] = jnp.zeros_like(acc_ref)
    m_ref[...] = jnp.full_like(m_ref, NEG)

    last_tile = ks_ref.shape[2] // SEL_KT - 1

    def scores(j, dst_ref):
        start = pl.multiple_of(jnp.minimum(j, last_tile) * SEL_KT, SEL_KT)
        dst_ref[...] = _dot(qaug_ref[...], ks_ref[0, :, pl.ds(start, SEL_KT)])

    def consume(j, src_ref, near):
        start = pl.multiple_of(j * SEL_KT, SEL_KT)
        for r in range(N_ROWGROUPS):
            rows = slice(r * QB, (r + 1) * QB)
            s = src_ref[rows, :]
            if near:
                e1 = jnp.clip(qblk - 2 * j + 1, 0, SEL_NEAR + 1)
                e2 = jnp.clip(qblk - 2 * j, 0, SEL_NEAR + 1)
                s = s + jnp.concatenate([tbl_ref[e1, r], tbl_ref[e2, r]], axis=1)
                src_ref[rows, :] = s
            m_old = m_ref[rows, :]
            m_new = jnp.maximum(m_old, jnp.max(s, axis=1, keepdims=True))
            alpha_ref[rows, :] = jnp.exp2(m_old - m_new)
            m_ref[rows, :] = m_new
        for r in range(N_ROWGROUPS):
            rows = slice(r * QB, (r + 1) * QB)
            vals = (vs0_ref if r < NSA_GROUP else vs1_ref)[0, pl.ds(start, SEL_KT), :]
            m_new = m_ref[rows, :]
            p = jnp.exp2(src_ref[rows, :] - jnp.concatenate([m_new, m_new], axis=1))
            acc_ref[rows, :] = alpha_ref[rows, :] * acc_ref[rows, :] + _dot(p.astype(BF16), vals)

    def tile_run(first, count, near):
        bufs = (s_ref, s1_ref)
        for u in range(count):
            scores(first + u + 1, bufs[(u + 1) % 2])
            consume(first + u, bufs[u % 2], near)

    def run_pairs(first, pairs, near, max_block):
        tiles = 2 * pairs
        lax.fori_loop(0, tiles // max_block, lambda i, c: (tile_run(first + max_block * i, max_block, near), c)[1], 0)
        done = tiles // max_block * max_block
        size = max_block // 2
        while size >= 2:
            start = first + done
            pl.when((tiles - done) >= size)(functools.partial(tile_run, start, size, near))
            done = done + jnp.where((tiles - done) >= size, size, 0)
            size //= 2

    n_pairs = ((qblk + 2) // 2 + 1) // 2
    n_far = jnp.maximum((qblk - (SEL_NEAR - 1)) // 2, 0) // 2
    scores(0, s_ref)
    run_pairs(0, n_far, False, SEL_FAR_BLOCK)
    run_pairs(2 * n_far, n_pairs - n_far, True, SEL_NEAR_BLOCK)
    for g in range(NSA_GROUP):
        ratio = _pair_ratio(acc_ref[g * QB:(g + 1) * QB, :], acc_ref[(NSA_GROUP + g) * QB:(NSA_GROUP + g + 1) * QB, :])
        out_ref[0, :, g * LANES:(g + 1) * LANES] = (ratio * _gate_tile(gates_ref, 1, g)).astype(BF16)


def _selected_branch(qb, sel, gates, ks_t, vs0aug, vs1aug, f_b):
    b, t, _ = qb.shape
    tbl = _sel_bias(f_b)
    cfar = jnp.broadcast_to(f_b[:, BIAS_LEN - 1:], (NSA_HEADS, LANES))
    cfar = jnp.pad(cfar, ((0, 16 - NSA_HEADS), (0, 0)))
    blockq = lambda w: pl.BlockSpec((1, QB, w), lambda bi, i: (bi, i, 0))
    batch = lambda a: pl.BlockSpec((1,) + a.shape[1:], lambda bi, i: (bi, 0, 0))
    rows = N_ROWGROUPS * QB
    return pl.pallas_call(
        _sel_kernel,
        grid=(b, t // QB),
        in_specs=[blockq(DB), pl.BlockSpec((1, NSA_KV_HEADS, QB, LANES), lambda bi, i: (bi, 0, i, 0)), blockq(LANES),
                  pl.BlockSpec(cfar.shape, lambda bi, i: (0, 0)), batch(ks_t), batch(vs0aug), batch(vs1aug),
                  pl.BlockSpec(tbl.shape, lambda bi, i: (0, 0, 0, 0))],
        out_specs=blockq(DB),
        out_shape=jax.ShapeDtypeStruct((b, t, DB), BF16),
        scratch_shapes=[pltpu.VMEM((rows, 2 * LANES), BF16), pltpu.VMEM((rows, SEL_KT), F32), pltpu.VMEM((rows, SEL_KT), F32),
                        pltpu.VMEM((rows, LANES), F32), pltpu.VMEM((rows, LANES), F32), pltpu.VMEM((rows, LANES), F32)],
        compiler_params=pltpu.CompilerParams(dimension_semantics=("parallel", "parallel"), vmem_limit_bytes=VMEM_LIMIT),
        name="nsa_selected",
    )(qb, sel, gates, cfar, ks_t, vs0aug, vs1aug, tbl)


WIN_KEYS = WIN + QB


def _win_bias(f_b):
    lo = WIN - (WIN_KEYS - 1)
    vals = _extend(f_b[:, :WIN], lo, WIN)
    vals = jnp.concatenate([vals, jnp.full((NSA_HEADS, lo + QB + WIN_KEYS - 1 - WIN), NEG, F32)], axis=1)
    return _toeplitz_of(vals, lo, QB, WIN_KEYS)


WIN_QBLOCKS = 2


def _win_kernel(qb_ref, gates_ref, kw_ref, vw0_ref, vw1_ref, tbl_ref, out_ref, q_ref, s_ref):
    col = lax.broadcasted_iota(jnp.int32, (1, WIN_KEYS), 1)
    for h in range(WIN_QBLOCKS):
        qblk = pl.program_id(1) * WIN_QBLOCKS + h
        row0 = h * QB
        for kv in range(NSA_KV_HEADS):
            for g in range(NSA_GROUP):
                r = kv * NSA_GROUP + g
                q_ref[h, r * QB:(r + 1) * QB, :] = _masked_q(qb_ref, g, kv, row0)
        start = pl.multiple_of(qblk * QB, QB)
        s_ref[h] = _nt_dot(q_ref[h], kw_ref[0, pl.ds(start, WIN_KEYS), :])
        pad_mask = jnp.where(col + qblk * QB >= WIN, 0.0, NEG)
        outs = []
        for r in range(N_ROWGROUPS):
            vals = (vw0_ref if r < NSA_GROUP else vw1_ref)[0, pl.ds(start, WIN_KEYS), :]
            s = s_ref[h, r * QB:(r + 1) * QB, :] + tbl_ref[r] + pad_mask
            e = jnp.exp2(s - jnp.max(s, axis=1, keepdims=True))
            outs.append(_dot(e.astype(BF16), vals))
        for g in range(NSA_GROUP):
            out_ref[0, row0:row0 + QB, g * LANES:(g + 1) * LANES] = (
                _pair_ratio(outs[g], outs[NSA_GROUP + g]) * _gate_tile(gates_ref, 2, g, row0)).astype(BF16)


def _window_branch(qb, gates, kw, vw0aug, vw1aug, f_b):
    b, t, _ = qb.shape
    tbl = _win_bias(f_b)
    pad_front = lambda a: jnp.pad(a, ((0, 0), (WIN, 0), (0, 0)))
    kw_pad, vw0_pad, vw1_pad = pad_front(kw), pad_front(vw0aug), pad_front(vw1aug)
    blockq = lambda w: pl.BlockSpec((1, WIN_QBLOCKS * QB, w), lambda bi, i: (bi, i, 0))
    batch = lambda a: pl.BlockSpec((1,) + a.shape[1:], lambda bi, i: (bi, 0, 0))
    rows = N_ROWGROUPS * QB
    return pl.pallas_call(
        _win_kernel,
        grid=(b, t // (WIN_QBLOCKS * QB)),
        in_specs=[blockq(DB), blockq(LANES), batch(kw_pad), batch(vw0_pad), batch(vw1_pad),
                  pl.BlockSpec(tbl.shape, lambda bi, i: (0, 0, 0))],
        out_specs=blockq(DB),
        out_shape=jax.ShapeDtypeStruct((b, t, DB), BF16),
        scratch_shapes=[pltpu.VMEM((WIN_QBLOCKS, rows, LANES), BF16), pltpu.VMEM((WIN_QBLOCKS, rows, WIN_KEYS), F32)],
        compiler_params=pltpu.CompilerParams(dimension_semantics=("parallel", "parallel"), vmem_limit_bytes=VMEM_LIMIT),
        name="nsa_window",
    )(qb, gates, kw_pad, vw0_pad, vw1_pad, tbl)


OUT_TM = 256
C_GROUP = N_EXPERTS


def _outproj_kernel(x_ref, oa_ref, oc_ref, os_ref, ow_ref,
                    wout_ref, g_ref, wr_ref, br_ref, h_ref, hn_ref, comb_ref):
    ob = oc_ref[...].astype(F32) + os_ref[...].astype(F32) + ow_ref[...].astype(F32)
    y = _dot(oa_ref[...], wout_ref[0:DA, :]) + _dot(ob.astype(BF16), wout_ref[DA:DA + DB, :])
    h = x_ref[...] + y
    h_ref[...] = h
    hn = h * lax.rsqrt(jnp.mean(h * h, axis=-1, keepdims=True) + EPS) * g_ref[...]
    hn_hi = hn.astype(BF16)
    hn_ref[...] = hn_hi
    hn_lo = (hn - hn_hi.astype(F32)).astype(BF16)
    both = _dot(hn_hi, wr_ref[...])
    logits = both[:, 0:LANES] + both[:, LANES:2 * LANES] + _dot(hn_lo, wr_ref[:, 0:LANES]) + br_ref[...]
    lane = lax.broadcasted_iota(jnp.int32, logits.shape, 1)
    lane_f = lane.astype(F32)
    big = float(LANES)
    gl = jnp.where((lane >= C_GROUP) & (lane < C_GROUP + N_GROUPS), logits, -jnp.inf)
    gmax = jnp.max(gl, axis=1, keepdims=True)
    gidx = jnp.min(jnp.where(gl == gmax, lane_f, big), axis=1, keepdims=True) - C_GROUP
    gprob = 1.0 / jnp.sum(jnp.exp(gl - gmax), axis=1, keepdims=True)
    grp_of_lane = (lane // EXPERTS_PER_GROUP).astype(F32)
    el = jnp.where((lane < N_EXPERTS) & (grp_of_lane == gidx), logits, -jnp.inf)
    v1 = jnp.max(el, axis=1, keepdims=True)
    i1 = jnp.min(jnp.where(el == v1, lane_f, big), axis=1, keepdims=True)
    el2 = jnp.where(lane_f == i1, -jnp.inf, el)
    v2 = jnp.max(el2, axis=1, keepdims=True)
    i2 = jnp.min(jnp.where(el2 == v2, lane_f, big), axis=1, keepdims=True)
    e2 = jnp.exp(v2 - v1)
    p1 = 1.0 / (1.0 + e2)
    comb_ref[...] = (gprob * (jnp.where(lane_f == i1, p1, 0.0) + jnp.where(lane_f == i2, e2 * p1, 0.0))
                     + jnp.where(lane == C_GROUP, gidx, 0.0))


def _outproj(x2d, o_a, b_parts, w_out_perm, gain, w_router, b_router):
    n, d = x2d.shape
    row = lambda w: pl.BlockSpec((OUT_TM, w), lambda i: (i, 0))
    full = lambda a: pl.BlockSpec(a.shape, lambda i: (0, 0))
    return pl.pallas_call(
        _outproj_kernel,
        grid=(n // OUT_TM,),
        in_specs=[row(d), row(DA)] + [row(DB)] * 3 + [full(w_out_perm), pl.BlockSpec((1, d), lambda i: (0, 0)),
                                                      full(w_router), full(b_router)],
        out_specs=[row(d), row(d), row(LANES)],
        out_shape=[jax.ShapeDtypeStruct((n, d), F32), jax.ShapeDtypeStruct((n, d), BF16),
                   jax.ShapeDtypeStruct((n, LANES), F32)],
        compiler_params=pltpu.CompilerParams(dimension_semantics=("parallel",), vmem_limit_bytes=VMEM_LIMIT),
        name="outproj_router",
    )(x2d, o_a, *b_parts, w_out_perm, gain.reshape(1, d), w_router, b_router)


MOE_TM = 1024


MOE_SUB = 128
MOE_FINAL_TM = 512


def _moe_kernel(hn_ref, comb_ref, wg_ref, wu_ref, wd_ref, y_ref, perm_ref, hs_ref, cs_ref, ys_ref, start_ref, nsub_ref):
    grp = pl.program_id(1)
    tm, d = hn_ref.shape
    n_pad = perm_ref.shape[0]

    @pl.when(grp == 0)
    def _sort():
        comb = comb_ref[...]
        lane_f = lax.broadcasted_iota(jnp.int32, (tm, LANES), 1).astype(F32)
        onehot = jnp.where(lane_f == comb[:, C_GROUP:C_GROUP + 1], 1.0, 0.0).astype(BF16)
        eye8 = jnp.where(lax.broadcasted_iota(jnp.int32, (8, LANES), 0) == lax.broadcasted_iota(jnp.int32, (8, LANES), 1),
                         1.0, 0.0).astype(BF16)
        onehot_t = _nt_dot(eye8, onehot)
        upper = jnp.where(lax.broadcasted_iota(jnp.int32, (tm, tm), 0) <= lax.broadcasted_iota(jnp.int32, (tm, tm), 1),
                          1.0, 0.0).astype(BF16)
        cum_t = _dot(onehot_t.astype(BF16), upper)
        start = jnp.zeros((1, 1), F32)
        pos_t = jnp.zeros((1, tm), F32)
        for k in range(N_GROUPS):
            padded = jnp.ceil(cum_t[k:k + 1, tm - 1:tm] * (1.0 / MOE_SUB)) * MOE_SUB
            pos_t = pos_t + onehot_t[k:k + 1, :] * (start + cum_t[k:k + 1, :] - 1.0)
            start_ref[k] = start[0, 0].astype(jnp.int32)
            nsub_ref[k] = (padded[0, 0] * (1.0 / MOE_SUB)).astype(jnp.int32)
            start = start + padded
        perm = jnp.where(lax.broadcasted_iota(jnp.int32, (n_pad, tm), 0) == pos_t.astype(jnp.int32), 1.0, 0.0).astype(BF16)
        perm_ref[...] = perm
        hs_ref[...] = _dot(perm, hn_ref[...]).astype(BF16)
        c_hi = comb.astype(BF16)
        rest = comb - c_hi.astype(F32)
        c_mid = rest.astype(BF16)
        c_lo = (rest - c_mid.astype(F32)).astype(BF16)
        cs_ref[...] = _dot(perm, c_hi) + _dot(perm, c_mid) + _dot(perm, c_lo)
        ys_ref[...] = jnp.zeros_like(ys_ref)

    lane = lax.broadcasted_iota(jnp.int32, (MOE_SUB, LANES), 1)

    def segment(s, carry):
        rows = pl.ds(pl.multiple_of(start_ref[grp] + s * MOE_SUB, MOE_SUB), MOE_SUB)
        x = hs_ref[rows, :]
        weights = cs_ref[rows, :]
        acc = jnp.zeros((MOE_SUB, d), F32)
        for j in range(EXPERTS_PER_GROUP):
            gate = _dot(x, wg_ref[j])
            up = _dot(x, wu_ref[j])
            w = jnp.sum(jnp.where(lane == grp * EXPERTS_PER_GROUP + j, weights, 0.0), axis=1, keepdims=True)
            acc = acc + _dot((gate * jax.nn.sigmoid(gate) * up * w).astype(BF16), wd_ref[j])
        ys_ref[rows, :] = acc.astype(BF16)
        return carry

    lax.fori_loop(0, nsub_ref[grp], segment, 0)

    @pl.when(grp == pl.num_programs(1) - 1)
    def _unsort():
        y_ref[...] = lax.dot_general(perm_ref[...], ys_ref[...], (((0,), (0,)), ((), ())),
                                     preferred_element_type=F32).astype(BF16)


def _final_kernel(h_ref, y_ref, g_ref, out_ref):
    y = h_ref[...] + y_ref[...].astype(F32)
    out_ref[...] = y * lax.rsqrt(jnp.mean(y * y, axis=-1, keepdims=True) + EPS) * g_ref[...]


def _moe(h, hn, comb, w_gate, w_up, w_down, gain):
    n, d = h.shape
    tm = min(MOE_TM, n)
    n_pad = tm + (N_GROUPS - 1) * MOE_SUB
    row = lambda w: pl.BlockSpec((tm, w), lambda i, g: (i, 0))
    group_w = lambda a: pl.BlockSpec((EXPERTS_PER_GROUP,) + a.shape[1:], lambda i, g: (g, 0, 0))
    y = pl.pallas_call(
        _moe_kernel,
        grid=(n // tm, N_GROUPS),
        in_specs=[row(d), row(LANES), group_w(w_gate), group_w(w_up), group_w(w_down)],
        out_specs=row(d),
        out_shape=jax.ShapeDtypeStruct((n, d), BF16),
        scratch_shapes=[pltpu.VMEM((n_pad, tm), BF16), pltpu.VMEM((n_pad, d), BF16), pltpu.VMEM((n_pad, LANES), F32),
                        pltpu.VMEM((n_pad, d), BF16), pltpu.SMEM((N_GROUPS,), jnp.int32), pltpu.SMEM((N_GROUPS,), jnp.int32)],
        compiler_params=pltpu.CompilerParams(dimension_semantics=("parallel", "arbitrary"), vmem_limit_bytes=VMEM_LIMIT),
        name="moe_experts",
    )(hn, comb, w_gate, w_up, w_down)
    rowf = lambda w: pl.BlockSpec((MOE_FINAL_TM, w), lambda i: (i, 0))
    return pl.pallas_call(
        _final_kernel,
        grid=(n // MOE_FINAL_TM,),
        in_specs=[rowf(d), rowf(d), pl.BlockSpec((1, d), lambda i: (0, 0))],
        out_specs=rowf(d),
        out_shape=jax.ShapeDtypeStruct((n, d), F32),
        compiler_params=pltpu.CompilerParams(dimension_semantics=("parallel",), vmem_limit_bytes=VMEM_LIMIT),
        name="residual_final_norm",
    )(h, y, gain.reshape(1, d))


def _permute_w_out(w_out):
    d = w_out.shape[1]
    wb = w_out[DA:].reshape(NSA_KV_HEADS, NSA_GROUP, HEAD_DIM, d).transpose(1, 0, 2, 3).reshape(DB, d)
    return jnp.concatenate([w_out[:DA], wb], axis=0).astype(BF16)


def _router_weights(w_group, b_group, w_expert, b_expert):
    d = w_group.shape[0]
    w = jnp.concatenate([w_expert.reshape(d, N_EXPERTS), w_group], axis=1)
    b = jnp.concatenate([b_expert.reshape(N_EXPERTS), b_group])
    pad = LANES - w.shape[1]
    w = jnp.pad(w, ((0, 0), (0, pad))).astype(F32)
    w_hi = w.astype(BF16)
    w_lo = (w - w_hi.astype(F32)).astype(BF16)
    return jnp.concatenate([w_hi, w_lo], axis=1), jnp.pad(b, (0, pad)).reshape(1, LANES).astype(F32)


def _layer(h, rel_bias, norm_mix, w_in, w_out, cmp_pos_k, cmp_pos_v, cmp_k_w1, cmp_k_w2, cmp_v_w1, cmp_v_w2,
           norm_ffn, w_rg, b_rg, w_re, b_re, w_gate, w_up, w_down, out_gain):
    b, t, d = h.shape
    n = b * t
    assert t % (QB * DIL_PATTERNS[-1][1]) == 0 and t // SEL_BLOCK <= LANES and n % MOE_TM == 0
    x2d = h.reshape(n, d)
    seq = lambda a: a if a.ndim == 3 else a.reshape(b, a.shape[0] // b, a.shape[-1])
    qa, ka, va, qb, kc, vc, ks_t, vs0aug, vs1aug, kw, vw0aug, vw1aug, gates = map(
        seq, _inproj(x2d, norm_mix, _permute_w_in(w_in), t))
    f_a = _bias_1d(rel_bias[:, :DIL_HEADS])
    f_b = _bias_1d(rel_bias[:, DIL_HEADS:]) * LOG2E
    o_a = _mixer_a(qa, ka, va, jnp.stack([_dil_bias(f_a, dil) for _, dil in DIL_PATTERNS]))
    kcmp, vcmp = _compress(kc, vc, cmp_pos_k, cmp_pos_v, cmp_k_w1, cmp_k_w2, cmp_v_w1, cmp_v_w2)
    o_cmp, sel = _compressed_branch(qb, kcmp, vcmp, gates, f_b)
    o_sel = _selected_branch(qb, sel, gates, ks_t, vs0aug, vs1aug, f_b)
    o_win = _window_branch(qb, gates, kw, vw0aug, vw1aug, f_b)
    b_parts = [o.reshape(n, DB) for o in (o_cmp, o_sel, o_win)]
    w_router, b_router = _router_weights(w_rg, b_rg, w_re, b_re)
    h2, hn, comb = _outproj(x2d, o_a.reshape(n, DA), b_parts, _permute_w_out(w_out), norm_ffn, w_router, b_router)
    return _moe(h2, hn, comb, w_gate.astype(BF16), w_up.astype(BF16), w_down.astype(BF16), out_gain)


def kernel(x, rel_bias, norm_mix, w_in, w_out, cmp_pos_k, cmp_pos_v, cmp_k_w1, cmp_k_w2, cmp_v_w1, cmp_v_w2,
           norm_ffn, w_router_group, b_router_group, w_router_expert, b_router_expert, w_gate, w_up, w_down,
           norm_final):
    depth = norm_mix.shape[0]
    assert depth == 1, "the final RMSNorm is fused into the last layer's expert kernel"
    out = _layer(x, rel_bias, norm_mix[0], w_in[0], w_out[0], cmp_pos_k[0], cmp_pos_v[0], cmp_k_w1[0], cmp_k_w2[0],
                 cmp_v_w1[0], cmp_v_w2[0], norm_ffn[0], w_router_group[0], b_router_group[0], w_router_expert[0],
                 b_router_expert[0], w_gate[0], w_up[0], w_down[0], norm_final)
    return out.reshape(x.shape)
```

```python
import functools
import math

import jax
import jax.numpy as jnp
import numpy as np
from jax import lax
from jax.experimental import pallas as pl
from jax.experimental.pallas import tpu as pltpu

HEAD_DIM = 64
DIL_HEADS = 6
NSA_KV_HEADS = 2
NSA_GROUP = 5
NSA_HEADS = NSA_KV_HEADS * NSA_GROUP
N_HEADS = DIL_HEADS + NSA_HEADS
DIL_PATTERNS = ((128, 1), (512, 4), (2048, 16))
CMP_BLOCK = 32
CMP_STRIDE = 16
CMP_HIDDEN = 256
SEL_BLOCK = 64
SEL_TOPK = 16
WIN = 512
N_FORCED = 3
N_BUCKETS = 32
MAX_DISTANCE = 2048
N_GROUPS = 4
EXPERTS_PER_GROUP = 4
N_EXPERTS = N_GROUPS * EXPERTS_PER_GROUP
D_EXPERT = 512
EPS = 1e-6

LANES = 128
QB = 128
NEG = -1.0e30
LOG2E = math.log2(math.e)
DA = DIL_HEADS * HEAD_DIM
DB = NSA_HEADS * HEAD_DIM
N_ROWGROUPS = NSA_HEADS
SEL_KT = 256
VMEM_LIMIT = 56 * 1024 * 1024

F32 = jnp.float32
BF16 = jnp.bfloat16
NT_DIMS = (((1,), (1,)), ((), ()))


def _nt_dot(a, b):
    return lax.dot_general(a, b, NT_DIMS, preferred_element_type=F32)


def _dot(a, b):
    return jnp.dot(a, b, preferred_element_type=F32)


def _bucket_np(dist):
    dist = np.maximum(np.asarray(dist, np.int64), 0)
    max_exact = N_BUCKETS // 2
    x = np.maximum(dist, 1).astype(np.float32) / np.float32(max_exact)
    large = max_exact + (np.log(x) / np.float32(math.log(MAX_DISTANCE / max_exact))
                         * np.float32(N_BUCKETS - max_exact)).astype(np.int32)
    large = np.minimum(large, N_BUCKETS - 1)
    return np.where(dist < max_exact, dist, large).astype(np.int32)


BIAS_LEN = 4096


def _bias_1d(rel_bias_heads):
    onehot = (_bucket_np(np.arange(BIAS_LEN))[None, :] == np.arange(N_BUCKETS)[:, None]).astype(np.float32)
    return jnp.dot(rel_bias_heads.T.astype(F32), jnp.asarray(onehot), precision=lax.Precision.HIGHEST)


def _extend(f, lo, hi):
    assert hi <= f.shape[-1]
    if lo >= 0:
        return f[..., lo:hi]
    pad = jnp.full(f.shape[:-1] + (-lo,), NEG, f.dtype)
    return jnp.concatenate([pad, f[..., :hi]], axis=-1)


def _toeplitz(w, q, c):
    n = q + c - 1
    assert w.shape[-1] == n
    lead = w.shape[:-1]
    wp = jnp.concatenate([w, jnp.zeros(lead + (1,), w.dtype)], axis=-1)
    flat = jnp.broadcast_to(wp[..., None, :], lead + (q, n + 1)).reshape(lead + (q * (n + 1),))
    return flat[..., :q * n].reshape(lead + (q, n))[..., q - 1:q - 1 + c]


def _toeplitz_of(fn_vals, lo, q, c):
    return _toeplitz(jnp.flip(fn_vals, axis=-1), q, c)


IN_TM = 512
C_QA, C_KA, C_VA = 0, DA, 2 * DA
C_QB = 3 * DA
C_KC = C_QB + DB
C_VC, C_KS, C_VS, C_KW, C_VW, C_GT = (C_KC + LANES * i for i in range(1, 7))
N_COLS = C_GT + LANES


def _permute_w_in(w_in):
    scale = 1.0 / math.sqrt(HEAD_DIM)
    sizes = [DA] * 3 + [DB] + [NSA_KV_HEADS * HEAD_DIM] * 6 + [3 * NSA_HEADS]
    offs = np.concatenate([[0], np.cumsum(sizes)])
    part = lambda i: w_in[:, offs[i]:offs[i + 1]]
    d = w_in.shape[0]
    qb = part(3).reshape(d, NSA_KV_HEADS, NSA_GROUP, HEAD_DIM).transpose(0, 2, 1, 3).reshape(d, DB)
    gt = part(10).reshape(d, NSA_KV_HEADS, NSA_GROUP, 3).transpose(0, 3, 2, 1).reshape(d, 3 * NSA_HEADS)
    gt = jnp.pad(gt, ((0, 0), (0, LANES - 3 * NSA_HEADS)))
    cols = [part(0) * scale, part(1), part(2), qb * (scale * LOG2E)] + [part(i) for i in range(4, 10)] + [gt]
    return jnp.concatenate(cols, axis=1).astype(BF16)


def _inproj_kernel(seq_len, x_ref, g_ref, w_ref, qa_ref, ka_ref, va_ref, qb_ref, kc_ref, vc_ref,
                   kst_ref, vs0_ref, vs1_ref, kw_ref, vw0_ref, vw1_ref, gates_ref, stage_ref):
    x = x_ref[...]
    xn = (x * lax.rsqrt(jnp.mean(x * x, axis=-1, keepdims=True) + EPS) * g_ref[...]).astype(BF16)
    seg = lambda a, n: _dot(xn, w_ref[:, a:a + n])
    qa_ref[...] = seg(C_QA, DA).astype(BF16)
    ka_ref[...] = seg(C_KA, DA).astype(BF16)
    va_ref[...] = seg(C_VA, DA).astype(BF16)
    qb_ref[...] = seg(C_QB, DB).astype(BF16)
    kw_ref[...] = seg(C_KW, LANES).astype(BF16)
    tm = x.shape[0]
    for col, out_ref in ((C_KC, kc_ref), (C_VC, vc_ref)):
        stage_ref[...] = seg(col, LANES)
        for j in range(CMP_STRIDE):
            out_ref[:, j * LANES:(j + 1) * LANES] = stage_ref[pl.ds(j, tm // CMP_STRIDE, stride=CMP_STRIDE), :].astype(BF16)
    tok_t = (pl.program_id(0) * tm) % seq_len + lax.broadcasted_iota(jnp.int32, (LANES, tm), 1)
    blk_t = lax.broadcasted_iota(jnp.int32, (LANES, tm), 0)
    stage_ref[...] = seg(C_KS, LANES)
    kst_ref[0, 0:LANES, :] = stage_ref[...].T.astype(BF16)
    kst_ref[0, LANES:2 * LANES, :] = jnp.where(blk_t == tok_t // SEL_BLOCK, 1.0, 0.0).astype(BF16)
    low = lax.broadcasted_iota(jnp.int32, (tm, LANES), 1) < HEAD_DIM
    for col, ref0, ref1 in ((C_VS, vs0_ref, vs1_ref), (C_VW, vw0_ref, vw1_ref)):
        v = seg(col, LANES)
        ref0[...] = jnp.where(low, v, 1.0).astype(BF16)
        ref1[...] = jnp.where(low, 1.0, v).astype(BF16)
    gates_ref[...] = jax.nn.sigmoid(seg(C_GT, LANES))


def _inproj(x2d, gain, w_perm, seq_len):
    n, d = x2d.shape
    row = lambda w: pl.BlockSpec((IN_TM, w), lambda i: (i, 0))
    rows = lambda w: (jax.ShapeDtypeStruct((n, w), BF16), row(w))
    chunks = (jax.ShapeDtypeStruct((n // CMP_STRIDE, CMP_STRIDE * LANES), BF16),
              pl.BlockSpec((IN_TM // CMP_STRIDE, CMP_STRIDE * LANES), lambda i: (i, 0)))
    per_seq = seq_len // IN_TM
    keys_t = (jax.ShapeDtypeStruct((n // seq_len, 2 * LANES, seq_len), BF16),
              pl.BlockSpec((1, 2 * LANES, IN_TM), lambda i: (i // per_seq, 0, i % per_seq)))
    outs = [rows(DA), rows(DA), rows(DA), rows(DB), chunks, chunks, keys_t] + [rows(LANES)] * 5
    outs.append((jax.ShapeDtypeStruct((n, LANES), F32), row(LANES)))
    return pl.pallas_call(
        functools.partial(_inproj_kernel, seq_len),
        grid=(n // IN_TM,),
        in_specs=[row(d), pl.BlockSpec((1, d), lambda i: (0, 0)), pl.BlockSpec((d, N_COLS), lambda i: (0, 0))],
        out_specs=[spec for _, spec in outs],
        out_shape=[shape for shape, _ in outs],
        scratch_shapes=[pltpu.VMEM((IN_TM, LANES), F32)],
        compiler_params=pltpu.CompilerParams(dimension_semantics=("parallel",), vmem_limit_bytes=VMEM_LIMIT),
        name="inproj",
    )(x2d, gain.reshape(1, d), w_perm)


def _embed_pair(w, n_tok):
    c = w.shape[1]
    w4 = w.reshape(n_tok, 1, HEAD_DIM, 1, c) * jnp.eye(NSA_KV_HEADS, dtype=w.dtype).reshape(1, 2, 1, 2, 1)
    return w4.reshape(n_tok * 2 * HEAD_DIM, 2 * c)


def _gelu_tanh(x):
    return 0.5 * x * (1.0 + jnp.tanh(math.sqrt(2.0 / math.pi) * (x + 0.044715 * (x * x * x))))


def _compress_kernel(ck_ref, cv_ref, posk_ref, posv_ref, wk1a, wk1b, wk2, wv1a, wv1b, wv2,
                     kout_ref, vout_ref, shift_ref):
    ncp = ck_ref.shape[1]
    for c_ref, pos_ref, w1a, w1b, w2, out_ref in ((ck_ref, posk_ref, wk1a, wk1b, wk2, kout_ref),
                                                  (cv_ref, posv_ref, wv1a, wv1b, wv2, vout_ref)):
        c = c_ref[0].astype(F32)
        first = _dot((c + pos_ref[0:1, :]).astype(BF16), w1a[...])
        second = _dot((c + pos_ref[1:2, :]).astype(BF16), w1b[...])
        shift_ref[0:ncp, :] = second
        shift_ref[ncp:ncp + 8, :] = jnp.zeros((8, second.shape[1]), F32)
        hidden = _gelu_tanh(first + shift_ref[1:ncp + 1, :])
        out_ref[0] = _dot(hidden.astype(BF16), w2[...]).astype(BF16)


def _compress(kc, vc, pos_k, pos_v, k_w1, k_w2, v_w1, v_w2):
    b, ncp, wide = kc.shape
    half = CMP_STRIDE * HEAD_DIM

    def prep(w1, w2, pos):
        pos_pair = jnp.broadcast_to(pos.reshape(2, CMP_STRIDE, 1, HEAD_DIM), (2, CMP_STRIDE, 2, HEAD_DIM))
        return (_embed_pair(w1[:half], CMP_STRIDE).astype(BF16), _embed_pair(w1[half:], CMP_STRIDE).astype(BF16),
                jnp.kron(jnp.eye(NSA_KV_HEADS, dtype=w2.dtype), w2).astype(BF16),
                pos_pair.reshape(2, wide).astype(F32))

    wk1a, wk1b, wk2, posk = prep(k_w1, k_w2, pos_k)
    wv1a, wv1b, wv2, posv = prep(v_w1, v_w2, pos_v)
    full = lambda a: pl.BlockSpec(a.shape, lambda i: (0,) * a.ndim)
    tok = pl.BlockSpec((1, ncp, wide), lambda i: (i, 0, 0))
    out = pl.BlockSpec((1, ncp, LANES), lambda i: (i, 0, 0))
    return pl.pallas_call(
        _compress_kernel,
        grid=(b,),
        in_specs=[tok, tok, full(posk), full(posv), full(wk1a), full(wk1b), full(wk2), full(wv1a), full(wv1b), full(wv2)],
        out_specs=[out, out],
        out_shape=[jax.ShapeDtypeStruct((b, ncp, LANES), BF16)] * 2,
        scratch_shapes=[pltpu.VMEM((ncp + 8, 2 * CMP_HIDDEN), F32)],
        compiler_params=pltpu.CompilerParams(dimension_semantics=("parallel",), vmem_limit_bytes=VMEM_LIMIT),
        name="compress",
    )(kc, vc, posk, posv, wk1a, wk1b, wk2, wv1a, wv1b, wv2)


def _pair_masks(rows):
    lane = lax.broadcasted_iota(jnp.int32, (rows, LANES), 1)
    return lane < HEAD_DIM


A_SUPER = QB * DIL_PATTERNS[-1][1]


def _mixer_a_kernel(q_ref, kp_ref, kc_ref, vp_ref, vc_ref, bias_ref, out_ref, qf_ref, kf_ref, vf_ref, o_ref, lse_ref):
    first = pl.program_id(1) == 0
    n_pairs = DIL_HEADS // 2
    for p in range(n_pairs):
        cs = slice(p * LANES, (p + 1) * LANES)
        qf_ref[p] = q_ref[0, :, cs].astype(F32)
        kf_ref[p, 0:A_SUPER, :] = kp_ref[0, :, cs].astype(F32)
        kf_ref[p, A_SUPER:2 * A_SUPER, :] = kc_ref[0, :, cs].astype(F32)
        vf_ref[p, 0:A_SUPER, :] = vp_ref[0, :, cs].astype(F32)
        vf_ref[p, A_SUPER:2 * A_SUPER, :] = vc_ref[0, :, cs].astype(F32)
    low = _pair_masks(QB)
    in_prev = lax.broadcasted_iota(jnp.int32, (QB, 2 * QB), 1) < QB
    zero = jnp.zeros((QB, LANES), BF16)

    def chunk(idx, dil, q_base, k_base, q_span, k_span, off, at_start):
        rows = lambda size: pl.ds(off, size) if dil == 1 else pl.ds(off, size, stride=dil)
        q_win = lambda ref: ref.at[pl.ds(pl.multiple_of(q_base, 8), q_span), :]
        k_win = lambda ref: ref.at[pl.ds(pl.multiple_of(k_base, 8), k_span), :]
        q_rows = rows(QB)
        prev_mask = jnp.where(jnp.logical_and(in_prev, jnp.logical_and(first, at_start)), NEG, 0.0)
        for p in range(n_pairs):
            q = q_win(qf_ref.at[p])[q_rows, :].astype(BF16)
            keys = k_win(kf_ref.at[p])[rows(2 * QB), :].astype(BF16)
            vals = k_win(vf_ref.at[p])[rows(2 * QB), :].astype(BF16)
            o_win, lse_win = q_win(o_ref.at[p]), q_win(lse_ref.at[p])
            lhs = jnp.concatenate([jnp.where(low, q, zero), jnp.where(low, zero, q)], axis=0)
            bias = jnp.concatenate([bias_ref[idx, 2 * p] + prev_mask, bias_ref[idx, 2 * p + 1] + prev_mask], axis=0)
            s = _nt_dot(lhs, keys) + bias
            m = jnp.max(s, axis=1, keepdims=True)
            e = jnp.exp(s - m)
            l = jnp.sum(e, axis=1, keepdims=True)
            pv = _dot(e.astype(BF16), vals) * (1.0 / l)
            lse = m + jnp.log(l)
            o_new = jnp.where(low, pv[:QB], pv[QB:])
            l_new = jnp.where(low, jnp.broadcast_to(lse[:QB], (QB, LANES)), jnp.broadcast_to(lse[QB:], (QB, LANES)))
            if idx > 0:
                o_old, l_old = o_win[q_rows, :], lse_win[q_rows, :]
                mx = jnp.maximum(l_old, l_new)
                w_old, w_new = jnp.exp(l_old - mx), jnp.exp(l_new - mx)
                tot = w_old + w_new
                o_new = (w_old * o_old + w_new * o_new) * (1.0 / tot)
                l_new = mx + jnp.log(tot)
            o_win[q_rows, :] = o_new
            lse_win[q_rows, :] = l_new

    def loop(n, body):
        lax.fori_loop(0, n, lambda i, carry: (body(i), carry)[1], 0, unroll=2)

    for idx, (_, dil) in enumerate(DIL_PATTERNS):
        span = QB * dil
        n_chunks = A_SUPER // span
        if n_chunks > 1:
            for r in range(dil):
                loop(n_chunks, lambda c, idx=idx, dil=dil, span=span, r=r:
                     chunk(idx, dil, span * c, A_SUPER + span * (c - 1), span, 2 * span, r, c == 0))
        else:
            for r in range(8):
                loop(dil // 8, lambda hi, idx=idx, dil=dil, span=span, r=r:
                     chunk(idx, dil, 8 * hi, 8 * hi, span - 8, 2 * span - 8, r, True))
    for p in range(n_pairs):
        out_ref[0, :, p * LANES:(p + 1) * LANES] = o_ref[p].astype(BF16)


def _mixer_a(qa, ka, va, bias):
    b, t, _ = qa.shape
    cur = pl.BlockSpec((1, A_SUPER, DA), lambda bi, i: (bi, i, 0))
    prev = pl.BlockSpec((1, A_SUPER, DA), lambda bi, i: (bi, jnp.maximum(i - 1, 0), 0))
    return pl.pallas_call(
        _mixer_a_kernel,
        grid=(b, t // A_SUPER),
        in_specs=[cur, prev, cur, prev, cur, pl.BlockSpec(bias.shape, lambda bi, i: (0, 0, 0, 0))],
        out_specs=cur,
        out_shape=jax.ShapeDtypeStruct((b, t, DA), BF16),
        scratch_shapes=[pltpu.VMEM((DIL_HEADS // 2, rows, LANES), F32) for rows in (A_SUPER, 2 * A_SUPER, 2 * A_SUPER, A_SUPER, A_SUPER)],
        compiler_params=pltpu.CompilerParams(dimension_semantics=("parallel", "parallel"), vmem_limit_bytes=VMEM_LIMIT),
        name="mixer_a",
    )(qa, ka, ka, va, va, bias)


def _dil_bias(f_a, dil):
    steps = DIL_PATTERNS[0][0]
    g = f_a[:, 0:dil * steps + 1:dil]
    lo, hi = QB - (2 * QB - 1), QB + QB
    vals = jnp.concatenate([jnp.full((DIL_HEADS, -lo), NEG, F32), g, jnp.full((DIL_HEADS, hi - steps - 1), NEG, F32)], axis=1)
    return _toeplitz_of(vals, lo, QB, 2 * QB)


CMP_TILE_KEYS = LANES
CMP_TILE_SPAN = CMP_TILE_KEYS * CMP_STRIDE // QB
CMP_CONST_DELTA = 28


def _cmp_bias(f_b):
    per = QB // CMP_STRIDE
    n_rows = per * (CMP_CONST_DELTA + 1)
    m_lo, m_hi = -(CMP_TILE_KEYS - 1), n_rows
    f_b = f_b.astype(BF16)
    base = _extend(f_b, CMP_STRIDE * m_lo - (CMP_BLOCK - 1), CMP_STRIDE * m_hi - (CMP_BLOCK - 1))
    g = base.reshape(NSA_HEADS, m_hi - m_lo, CMP_STRIDE).transpose(0, 2, 1)
    t = _toeplitz_of(g, m_lo, n_rows, CMP_TILE_KEYS)
    t = t.reshape(NSA_HEADS, CMP_STRIDE, CMP_CONST_DELTA + 1, per, CMP_TILE_KEYS).transpose(2, 0, 3, 1, 4)
    t = t.reshape(CMP_CONST_DELTA + 1, NSA_HEADS, QB, CMP_TILE_KEYS)
    return jnp.concatenate([jnp.full((1,) + t.shape[1:], NEG, t.dtype), t], axis=0)


def _overlap_matrix_t(ncp, n_sel_pad):
    n = np.arange(ncp)[None, :] * CMP_STRIDE
    s = np.arange(n_sel_pad)[:, None] * SEL_BLOCK
    ov = np.clip(np.minimum(n + CMP_BLOCK, s + SEL_BLOCK) - np.maximum(n, s), 0, None) / CMP_BLOCK
    return jnp.asarray(ov, BF16)


def _gate_tile(gates_ref, branch, g, row0=0):
    c = branch * NSA_HEADS + g * 2
    low = _pair_masks(QB)
    return jnp.where(low, jnp.broadcast_to(gates_ref[0, row0:row0 + QB, c:c + 1], (QB, LANES)),
                     jnp.broadcast_to(gates_ref[0, row0:row0 + QB, c + 1:c + 2], (QB, LANES)))


def _masked_q(qb_ref, g, kv, row0=0):
    q = qb_ref[0, row0:row0 + QB, g * LANES:(g + 1) * LANES]
    low = _pair_masks(QB)
    keep = low if kv == 0 else jnp.logical_not(low)
    return jnp.where(keep, q, jnp.zeros_like(q))


def _cmp_kernel(n_tiles, qb_ref, kcmp_ref, vcmp_ref, gates_ref, ov_ref, *rest):
    tbl_refs, (oc_ref, sel_ref, q_ref, s_ref, p_ref, pv_ref) = rest[:n_tiles], rest[n_tiles:]
    qblk = pl.program_id(1)
    t0 = qblk * QB
    low = _pair_masks(QB)
    for kv in range(NSA_KV_HEADS):
        for g in range(NSA_GROUP):
            r = kv * NSA_GROUP + g
            q_ref[r * QB:(r + 1) * QB, :] = _masked_q(qb_ref, g, kv)

    def attend(n_vis):
        kc = n_vis * CMP_TILE_KEYS
        n_blk = n_vis * CMP_TILE_SPAN * QB // SEL_BLOCK
        s_ref[:, 0:kc] = _nt_dot(q_ref[...], kcmp_ref[0, 0:kc, :])
        blk = lax.broadcasted_iota(jnp.int32, (n_blk, QB), 0)
        cur = (t0 + lax.broadcasted_iota(jnp.int32, (n_blk, QB), 1)) // SEL_BLOCK
        blk_f = blk.astype(F32)
        forced = (blk == cur) | (blk == cur - 1) | (blk == 0)
        causal = blk <= cur
        scores = []
        for kv in range(NSA_KV_HEADS):
            psum = jnp.zeros((QB, kc), F32)
            for g in range(NSA_GROUP):
                r = kv * NSA_GROUP + g
                rows = slice(r * QB, (r + 1) * QB)
                s = s_ref[rows, 0:kc] + jnp.concatenate([tbl_refs[c][0, r].astype(F32) for c in range(n_vis)], axis=1)
                m = jnp.max(s, axis=1, keepdims=True)
                e = jnp.exp2(s - m)
                den = jnp.sum(e, axis=1, keepdims=True)
                p = jnp.where(m > 0.5 * NEG, e * (1.0 / den), 0.0)
                psum = psum + p
                p_ref[rows, 0:kc] = p.astype(BF16)
            hi = psum.astype(BF16)
            lo = (psum - hi.astype(F32)).astype(BF16)
            ov_t = ov_ref[0:n_blk, 0:kc]
            imp_t = _nt_dot(ov_t, hi) + _nt_dot(ov_t, lo)
            scores.append(jnp.where(forced, -jnp.inf, jnp.where(causal, imp_t, -1.0)))
        pv_ref[...] = _dot(p_ref[:, 0:kc], vcmp_ref[0, 0:kc, :])

        def pick(_, carry):
            new = []
            for val, sel in carry:
                mx = jnp.max(val, axis=0, keepdims=True)
                idx = jnp.min(jnp.where(val == mx, blk_f, float(n_blk)), axis=0, keepdims=True)
                hit = blk_f == idx
                new.append((jnp.where(hit, -jnp.inf, val), jnp.where(hit, 1.0, sel)))
            return tuple(new)

        taken = jnp.where(forced, 1.0, 0.0)
        picked = lax.fori_loop(0, SEL_TOPK - N_FORCED, pick, tuple((v, taken) for v in scores))
        eye = jnp.where(lax.broadcasted_iota(jnp.int32, (QB, QB), 0) == lax.broadcasted_iota(jnp.int32, (QB, QB), 1),
                        1.0, 0.0).astype(BF16)
        for kv in range(NSA_KV_HEADS):
            sel_t = jnp.where(causal, picked[kv][1], 0.0).astype(BF16)
            sel_ref[0, kv, :, 0:n_blk] = _nt_dot(eye, sel_t).astype(BF16)
            if n_blk < LANES:
                sel_ref[0, kv, :, n_blk:LANES] = jnp.zeros((QB, LANES - n_blk), BF16)

    n_vis = qblk // CMP_TILE_SPAN + 1
    for w in range(1, n_tiles + 1):
        pl.when(n_vis == w)(functools.partial(attend, w))

    for g in range(NSA_GROUP):
        o0, o1 = pv_ref[g * QB:(g + 1) * QB, :], pv_ref[(NSA_GROUP + g) * QB:(NSA_GROUP + g + 1) * QB, :]
        oc_ref[0, :, g * LANES:(g + 1) * LANES] = (jnp.where(low, o0, o1) * _gate_tile(gates_ref, 0, g)).astype(BF16)


def _compressed_branch(qb, kcmp, vcmp, gates, f_b):
    b, t, _ = qb.shape
    ncp = kcmp.shape[1]
    n_tiles = ncp // CMP_TILE_KEYS
    tbl = _cmp_bias(f_b)
    ov = _overlap_matrix_t(ncp, LANES)

    def tbl_spec(c):
        return pl.BlockSpec((1, NSA_HEADS, QB, CMP_TILE_KEYS),
                            lambda bi, i: (jnp.clip(i - CMP_TILE_SPAN * c, -1, CMP_CONST_DELTA) + 1, 0, 0, 0))

    blockq = lambda w: pl.BlockSpec((1, QB, w), lambda bi, i: (bi, i, 0))
    batch = lambda a: pl.BlockSpec((1,) + a.shape[1:], lambda bi, i: (bi, 0, 0))
    return pl.pallas_call(
        functools.partial(_cmp_kernel, n_tiles),
        grid=(b, t // QB),
        in_specs=[blockq(DB), batch(kcmp), batch(vcmp), blockq(LANES), pl.BlockSpec(ov.shape, lambda bi, i: (0, 0))]
                 + [tbl_spec(c) for c in range(n_tiles)],
        out_specs=[blockq(DB), pl.BlockSpec((1, NSA_KV_HEADS, QB, LANES), lambda bi, i: (bi, 0, i, 0))],
        out_shape=[jax.ShapeDtypeStruct((b, t, DB), BF16), jax.ShapeDtypeStruct((b, NSA_KV_HEADS, t, LANES), BF16)],
        scratch_shapes=[pltpu.VMEM((N_ROWGROUPS * QB, LANES), BF16), pltpu.VMEM((N_ROWGROUPS * QB, ncp), F32),
                        pltpu.VMEM((N_ROWGROUPS * QB, ncp), BF16), pltpu.VMEM((N_ROWGROUPS * QB, LANES), F32)],
        compiler_params=pltpu.CompilerParams(dimension_semantics=("parallel", "parallel"), vmem_limit_bytes=VMEM_LIMIT),
        name="nsa_compressed",
    )(qb, kcmp, vcmp, gates, ov, *([tbl] * n_tiles))


SEL_NEAR = 13
SEL_FAR_BLOCK, SEL_NEAR_BLOCK = 8, 8


def _sel_bias(f_b):
    n_off = SEL_NEAR + 1
    cols = QB * n_off
    rel = f_b - f_b[:, BIAS_LEN - 1:]
    lo = -(QB - 1)
    big = _toeplitz_of(_extend(rel, lo, lo + QB + cols - 1), lo, QB, cols)
    tiles = jnp.flip(big.reshape(NSA_HEADS, QB, n_off, QB).transpose(2, 0, 1, 3), axis=0)
    return jnp.concatenate([jnp.zeros((1,) + tiles.shape[1:], F32), tiles], axis=0)


def _pair_ratio(acc0, acc1):
    low = _pair_masks(acc0.shape[0])
    den = pltpu.roll(jnp.where(low, acc1, acc0), HEAD_DIM, axis=1)
    return jnp.where(low, acc0, acc1) * (1.0 / den)


def _sel_kernel(qb_ref, sel_ref, gates_ref, cfar_ref, ks_ref, vs0_ref, vs1_ref, tbl_ref, out_ref,
                qaug_ref, s_ref, s1_ref, acc_ref, m_ref, alpha_ref):
    qblk = pl.program_id(1)
    for kv in range(NSA_KV_HEADS):
        unchosen = jnp.where(sel_ref[0, kv].astype(F32) > 0.0, 0.0, NEG)
        for g in range(NSA_GROUP):
            r = kv * NSA_GROUP + g
            rows = slice(r * QB, (r + 1) * QB)
            qaug_ref[rows, 0:LANES] = _masked_q(qb_ref, g, kv)
            qaug_ref[rows, LANES:2 * LANES] = (unchosen + cfar_ref[r:r + 1, :]).astype(BF16)
    acc_ref[...] = jnp.zeros_like(acc_ref)
    m_ref[...] = jnp.full_like(m_ref, NEG)

    last_tile = ks_ref.shape[2] // SEL_KT - 1

    def scores(j, dst_ref):
        start = pl.multiple_of(jnp.minimum(j, last_tile) * SEL_KT, SEL_KT)
        dst_ref[...] = _dot(qaug_ref[...], ks_ref[0, :, pl.ds(start, SEL_KT)])

    def consume(j, src_ref, near):
        start = pl.multiple_of(j * SEL_KT, SEL_KT)
        for r in range(N_ROWGROUPS):
            rows = slice(r * QB, (r + 1) * QB)
            s = src_ref[rows, :]
            if near:
                e1 = jnp.clip(qblk - 2 * j + 1, 0, SEL_NEAR + 1)
                e2 = jnp.clip(qblk - 2 * j, 0, SEL_NEAR + 1)
                s = s + jnp.concatenate([tbl_ref[e1, r], tbl_ref[e2, r]], axis=1)
                src_ref[rows, :] = s
            m_old = m_ref[rows, :]
            m_new = jnp.maximum(m_old, jnp.max(s, axis=1, keepdims=True))
            alpha_ref[rows, :] = jnp.exp2(m_old - m_new)
            m_ref[rows, :] = m_new
        for r in range(N_ROWGROUPS):
            rows = slice(r * QB, (r + 1) * QB)
            vals = (vs0_ref if r < NSA_GROUP else vs1_ref)[0, pl.ds(start, SEL_KT), :]
            m_new = m_ref[rows, :]
            p = jnp.exp2(src_ref[rows, :] - jnp.concatenate([m_new, m_new], axis=1))
            acc_ref[rows, :] = alpha_ref[rows, :] * acc_ref[rows, :] + _dot(p.astype(BF16), vals)

    def tile_run(first, count, near):
        bufs = (s_ref, s1_ref)
        for u in range(count):
            scores(first + u + 1, bufs[(u + 1) % 2])
            consume(first + u, bufs[u % 2], near)

    def run_pairs(first, pairs, near, max_block):
        tiles = 2 * pairs
        lax.fori_loop(0, tiles // max_block, lambda i, c: (tile_run(first + max_block * i, max_block, near), c)[1], 0)
        done = tiles // max_block * max_block
        size = max_block // 2
        while size >= 2:
            start = first + done
            pl.when((tiles - done) >= size)(functools.partial(tile_run, start, size, near))
            done = done + jnp.where((tiles - done) >= size, size, 0)
            size //= 2

    n_pairs = ((qblk + 2) // 2 + 1) // 2
    n_far = jnp.maximum((qblk - (SEL_NEAR - 1)) // 2, 0) // 2
    scores(0, s_ref)
    run_pairs(0, n_far, False, SEL_FAR_BLOCK)
    run_pairs(2 * n_far, n_pairs - n_far, True, SEL_NEAR_BLOCK)
    for g in range(NSA_GROUP):
        ratio = _pair_ratio(acc_ref[g * QB:(g + 1) * QB, :], acc_ref[(NSA_GROUP + g) * QB:(NSA_GROUP + g + 1) * QB, :])
        out_ref[0, :, g * LANES:(g + 1) * LANES] = (ratio * _gate_tile(gates_ref, 1, g)).astype(BF16)


def _selected_branch(qb, sel, gates, ks_t, vs0aug, vs1aug, f_b):
    b, t, _ = qb.shape
    tbl = _sel_bias(f_b)
    cfar = jnp.broadcast_to(f_b[:, BIAS_LEN - 1:], (NSA_HEADS, LANES))
    cfar = jnp.pad(cfar, ((0, 16 - NSA_HEADS), (0, 0)))
    blockq = lambda w: pl.BlockSpec((1, QB, w), lambda bi, i: (bi, i, 0))
    batch = lambda a: pl.BlockSpec((1,) + a.shape[1:], lambda bi, i: (bi, 0, 0))
    rows = N_ROWGROUPS * QB
    return pl.pallas_call(
        _sel_kernel,
        grid=(b, t // QB),
        in_specs=[blockq(DB), pl.BlockSpec((1, NSA_KV_HEADS, QB, LANES), lambda bi, i: (bi, 0, i, 0)), blockq(LANES),
                  pl.BlockSpec(cfar.shape, lambda bi, i: (0, 0)), batch(ks_t), batch(vs0aug), batch(vs1aug),
                  pl.BlockSpec(tbl.shape, lambda bi, i: (0, 0, 0, 0))],
        out_specs=blockq(DB),
        out_shape=jax.ShapeDtypeStruct((b, t, DB), BF16),
        scratch_shapes=[pltpu.VMEM((rows, 2 * LANES), BF16), pltpu.VMEM((rows, SEL_KT), F32), pltpu.VMEM((rows, SEL_KT), F32),
                        pltpu.VMEM((rows, LANES), F32), pltpu.VMEM((rows, LANES), F32), pltpu.VMEM((rows, LANES), F32)],
        compiler_params=pltpu.CompilerParams(dimension_semantics=("parallel", "parallel"), vmem_limit_bytes=VMEM_LIMIT),
        name="nsa_selected",
    )(qb, sel, gates, cfar, ks_t, vs0aug, vs1aug, tbl)


WIN_KEYS = WIN + QB


def _win_bias(f_b):
    lo = WIN - (WIN_KEYS - 1)
    vals = _extend(f_b[:, :WIN], lo, WIN)
    vals = jnp.concatenate([vals, jnp.full((NSA_HEADS, lo + QB + WIN_KEYS - 1 - WIN), NEG, F32)], axis=1)
    return _toeplitz_of(vals, lo, QB, WIN_KEYS)


WIN_QBLOCKS = 2


def _win_kernel(qb_ref, gates_ref, kw_ref, vw0_ref, vw1_ref, tbl_ref, out_ref, q_ref, s_ref):
    col = lax.broadcasted_iota(jnp.int32, (1, WIN_KEYS), 1)
    for h in range(WIN_QBLOCKS):
        qblk = pl.program_id(1) * WIN_QBLOCKS + h
        row0 = h * QB
        for kv in range(NSA_KV_HEADS):
            for g in range(NSA_GROUP):
                r = kv * NSA_GROUP + g
                q_ref[h, r * QB:(r + 1) * QB, :] = _masked_q(qb_ref, g, kv, row0)
        start = pl.multiple_of(qblk * QB, QB)
        s_ref[h] = _nt_dot(q_ref[h], kw_ref[0, pl.ds(start, WIN_KEYS), :])
        pad_mask = jnp.where(col + qblk * QB >= WIN, 0.0, NEG)
        outs = []
        for r in range(N_ROWGROUPS):
            vals = (vw0_ref if r < NSA_GROUP else vw1_ref)[0, pl.ds(start, WIN_KEYS), :]
            s = s_ref[h, r * QB:(r + 1) * QB, :] + tbl_ref[r] + pad_mask
            e = jnp.exp2(s - jnp.max(s, axis=1, keepdims=True))
            outs.append(_dot(e.astype(BF16), vals))
        for g in range(NSA_GROUP):
            out_ref[0, row0:row0 + QB, g * LANES:(g + 1) * LANES] = (
                _pair_ratio(outs[g], outs[NSA_GROUP + g]) * _gate_tile(gates_ref, 2, g, row0)).astype(BF16)


def _window_branch(qb, gates, kw, vw0aug, vw1aug, f_b):
    b, t, _ = qb.shape
    tbl = _win_bias(f_b)
    pad_front = lambda a: jnp.pad(a, ((0, 0), (WIN, 0), (0, 0)))
    kw_pad, vw0_pad, vw1_pad = pad_front(kw), pad_front(vw0aug), pad_front(vw1aug)
    blockq = lambda w: pl.BlockSpec((1, WIN_QBLOCKS * QB, w), lambda bi, i: (bi, i, 0))
    batch = lambda a: pl.BlockSpec((1,) + a.shape[1:], lambda bi, i: (bi, 0, 0))
    rows = N_ROWGROUPS * QB
    return pl.pallas_call(
        _win_kernel,
        grid=(b, t // (WIN_QBLOCKS * QB)),
        in_specs=[blockq(DB), blockq(LANES), batch(kw_pad), batch(vw0_pad), batch(vw1_pad),
                  pl.BlockSpec(tbl.shape, lambda bi, i: (0, 0, 0))],
        out_specs=blockq(DB),
        out_shape=jax.ShapeDtypeStruct((b, t, DB), BF16),
        scratch_shapes=[pltpu.VMEM((WIN_QBLOCKS, rows, LANES), BF16), pltpu.VMEM((WIN_QBLOCKS, rows, WIN_KEYS), F32)],
        compiler_params=pltpu.CompilerParams(dimension_semantics=("parallel", "parallel"), vmem_limit_bytes=VMEM_LIMIT),
        name="nsa_window",
    )(qb, gates, kw_pad, vw0_pad, vw1_pad, tbl)


OUT_TM = 256
C_GROUP = N_EXPERTS


def _outproj_kernel(x_ref, oa_ref, oc_ref, os_ref, ow_ref,
                    wout_ref, g_ref, wr_ref, br_ref, h_ref, hn_ref, comb_ref):
    ob = oc_ref[...].astype(F32) + os_ref[...].astype(F32) + ow_ref[...].astype(F32)
    y = _dot(oa_ref[...], wout_ref[0:DA, :]) + _dot(ob.astype(BF16), wout_ref[DA:DA + DB, :])
    h = x_ref[...] + y
    h_ref[...] = h
    hn = h * lax.rsqrt(jnp.mean(h * h, axis=-1, keepdims=True) + EPS) * g_ref[...]
    hn_hi = hn.astype(BF16)
    hn_ref[...] = hn_hi
    hn_lo = (hn - hn_hi.astype(F32)).astype(BF16)
    both = _dot(hn_hi, wr_ref[...])
    logits = both[:, 0:LANES] + both[:, LANES:2 * LANES] + _dot(hn_lo, wr_ref[:, 0:LANES]) + br_ref[...]
    lane = lax.broadcasted_iota(jnp.int32, logits.shape, 1)
    lane_f = lane.astype(F32)
    big = float(LANES)
    gl = jnp.where((lane >= C_GROUP) & (lane < C_GROUP + N_GROUPS), logits, -jnp.inf)
    gmax = jnp.max(gl, axis=1, keepdims=True)
    gidx = jnp.min(jnp.where(gl == gmax, lane_f, big), axis=1, keepdims=True) - C_GROUP
    gprob = 1.0 / jnp.sum(jnp.exp(gl - gmax), axis=1, keepdims=True)
    grp_of_lane = (lane // EXPERTS_PER_GROUP).astype(F32)
    el = jnp.where((lane < N_EXPERTS) & (grp_of_lane == gidx), logits, -jnp.inf)
    v1 = jnp.max(el, axis=1, keepdims=True)
    i1 = jnp.min(jnp.where(el == v1, lane_f, big), axis=1, keepdims=True)
    el2 = jnp.where(lane_f == i1, -jnp.inf, el)
    v2 = jnp.max(el2, axis=1, keepdims=True)
    i2 = jnp.min(jnp.where(el2 == v2, lane_f, big), axis=1, keepdims=True)
    e2 = jnp.exp(v2 - v1)
    p1 = 1.0 / (1.0 + e2)
    comb_ref[...] = (gprob * (jnp.where(lane_f == i1, p1, 0.0) + jnp.where(lane_f == i2, e2 * p1, 0.0))
                     + jnp.where(lane == C_GROUP, gidx, 0.0))


def _outproj(x2d, o_a, b_parts, w_out_perm, gain, w_router, b_router):
    n, d = x2d.shape
    row = lambda w: pl.BlockSpec((OUT_TM, w), lambda i: (i, 0))
    full = lambda a: pl.BlockSpec(a.shape, lambda i: (0, 0))
    return pl.pallas_call(
        _outproj_kernel,
        grid=(n // OUT_TM,),
        in_specs=[row(d), row(DA)] + [row(DB)] * 3 + [full(w_out_perm), pl.BlockSpec((1, d), lambda i: (0, 0)),
                                                      full(w_router), full(b_router)],
        out_specs=[row(d), row(d), row(LANES)],
        out_shape=[jax.ShapeDtypeStruct((n, d), F32), jax.ShapeDtypeStruct((n, d), BF16),
                   jax.ShapeDtypeStruct((n, LANES), F32)],
        compiler_params=pltpu.CompilerParams(dimension_semantics=("parallel",), vmem_limit_bytes=VMEM_LIMIT),
        name="outproj_router",
    )(x2d, o_a, *b_parts, w_out_perm, gain.reshape(1, d), w_router, b_router)


MOE_TM = 1024


MOE_SUB = 128
MOE_FINAL_TM = 512


def _moe_kernel(hn_ref, comb_ref, upper_ref, wg_ref, wu_ref, wd_ref, y_ref,
                perm_ref, hs_ref, cs_ref, ys_ref, start_ref, nsub_ref):
    grp = pl.program_id(1)
    tm, d = hn_ref.shape
    n_pad = perm_ref.shape[0]

    @pl.when(grp == 0)
    def _sort():
        comb = comb_ref[...]
        lane_f = lax.broadcasted_iota(jnp.int32, (tm, LANES), 1).astype(F32)
        onehot = jnp.where(lane_f == comb[:, C_GROUP:C_GROUP + 1], 1.0, 0.0).astype(BF16)
        eye8 = jnp.where(lax.broadcasted_iota(jnp.int32, (8, LANES), 0) == lax.broadcasted_iota(jnp.int32, (8, LANES), 1),
                         1.0, 0.0).astype(BF16)
        onehot_t = _nt_dot(eye8, onehot)
        cum_t = _dot(onehot_t.astype(BF16), upper_ref[...])
        start = jnp.zeros((1, 1), F32)
        pos_t = jnp.zeros((1, tm), F32)
        for k in range(N_GROUPS):
            padded = jnp.ceil(cum_t[k:k + 1, tm - 1:tm] * (1.0 / MOE_SUB)) * MOE_SUB
            pos_t = pos_t + onehot_t[k:k + 1, :] * (start + cum_t[k:k + 1, :] - 1.0)
            start_ref[k] = start[0, 0].astype(jnp.int32)
            nsub_ref[k] = (padded[0, 0] * (1.0 / MOE_SUB)).astype(jnp.int32)
            start = start + padded
        perm = jnp.where(lax.broadcasted_iota(jnp.int32, (n_pad, tm), 0) == pos_t.astype(jnp.int32), 1.0, 0.0).astype(BF16)
        perm_ref[...] = perm
        hs_ref[...] = _dot(perm, hn_ref[...]).astype(BF16)
        c_hi = comb.astype(BF16)
        c_lo = (comb - c_hi.astype(F32)).astype(BF16)
        both = _dot(perm, jnp.concatenate([c_hi, c_lo], axis=1))
        cs_ref[...] = both[:, 0:LANES] + both[:, LANES:2 * LANES]
        ys_ref[...] = jnp.zeros_like(ys_ref)

    lane = lax.broadcasted_iota(jnp.int32, (MOE_SUB, LANES), 1)

    def segment(s, carry):
        rows = pl.ds(pl.multiple_of(start_ref[grp] + s * MOE_SUB, MOE_SUB), MOE_SUB)
        x = hs_ref[rows, :]
        weights = cs_ref[rows, :]
        acc = jnp.zeros((MOE_SUB, d), F32)
        for j in range(EXPERTS_PER_GROUP):
            gate = _dot(x, wg_ref[j])
            up = _dot(x, wu_ref[j])
            w = jnp.sum(jnp.where(lane == grp * EXPERTS_PER_GROUP + j, weights, 0.0), axis=1, keepdims=True)
            acc = acc + _dot((gate * jax.nn.sigmoid(gate) * up * w).astype(BF16), wd_ref[j])
        ys_ref[rows, :] = acc.astype(BF16)
        return carry

    lax.fori_loop(0, nsub_ref[grp], segment, 0)

    @pl.when(grp == pl.num_programs(1) - 1)
    def _unsort():
        y_ref[...] = lax.dot_general(perm_ref[...], ys_ref[...], (((0,), (0,)), ((), ())),
                                     preferred_element_type=F32).astype(BF16)


def _final_kernel(h_ref, y_ref, g_ref, out_ref):
    y = h_ref[...] + y_ref[...].astype(F32)
    out_ref[...] = y * lax.rsqrt(jnp.mean(y * y, axis=-1, keepdims=True) + EPS) * g_ref[...]


def _moe(h, hn, comb, w_gate, w_up, w_down, gain):
    n, d = h.shape
    tm = min(MOE_TM, n)
    n_pad = tm + (N_GROUPS - 1) * MOE_SUB
    row = lambda w: pl.BlockSpec((tm, w), lambda i, g: (i, 0))
    group_w = lambda a: pl.BlockSpec((EXPERTS_PER_GROUP,) + a.shape[1:], lambda i, g: (g, 0, 0))
    upper = jnp.asarray(np.triu(np.ones((tm, tm), np.float32)), BF16)
    y = pl.pallas_call(
        _moe_kernel,
        grid=(n // tm, N_GROUPS),
        in_specs=[row(d), row(LANES), pl.BlockSpec((tm, tm), lambda i, g: (0, 0)),
                  group_w(w_gate), group_w(w_up), group_w(w_down)],
        out_specs=row(d),
        out_shape=jax.ShapeDtypeStruct((n, d), BF16),
        scratch_shapes=[pltpu.VMEM((n_pad, tm), BF16), pltpu.VMEM((n_pad, d), BF16), pltpu.VMEM((n_pad, LANES), F32),
                        pltpu.VMEM((n_pad, d), BF16), pltpu.SMEM((N_GROUPS,), jnp.int32), pltpu.SMEM((N_GROUPS,), jnp.int32)],
        compiler_params=pltpu.CompilerParams(dimension_semantics=("parallel", "arbitrary"), vmem_limit_bytes=VMEM_LIMIT),
        name="moe_experts",
    )(hn, comb, upper, w_gate, w_up, w_down)
    rowf = lambda w: pl.BlockSpec((MOE_FINAL_TM, w), lambda i: (i, 0))
    return pl.pallas_call(
        _final_kernel,
        grid=(n // MOE_FINAL_TM,),
        in_specs=[rowf(d), rowf(d), pl.BlockSpec((1, d), lambda i: (0, 0))],
        out_specs=rowf(d),
        out_shape=jax.ShapeDtypeStruct((n, d), F32),
        compiler_params=pltpu.CompilerParams(dimension_semantics=("parallel",), vmem_limit_bytes=VMEM_LIMIT),
        name="residual_final_norm",
    )(h, y, gain.reshape(1, d))


def _permute_w_out(w_out):
    d = w_out.shape[1]
    wb = w_out[DA:].reshape(NSA_KV_HEADS, NSA_GROUP, HEAD_DIM, d).transpose(1, 0, 2, 3).reshape(DB, d)
    return jnp.concatenate([w_out[:DA], wb], axis=0).astype(BF16)


def _router_weights(w_group, b_group, w_expert, b_expert):
    d = w_group.shape[0]
    w = jnp.concatenate([w_expert.reshape(d, N_EXPERTS), w_group], axis=1)
    b = jnp.concatenate([b_expert.reshape(N_EXPERTS), b_group])
    pad = LANES - w.shape[1]
    w = jnp.pad(w, ((0, 0), (0, pad))).astype(F32)
    w_hi = w.astype(BF16)
    w_lo = (w - w_hi.astype(F32)).astype(BF16)
    return jnp.concatenate([w_hi, w_lo], axis=1), jnp.pad(b, (0, pad)).reshape(1, LANES).astype(F32)


def _layer(h, rel_bias, norm_mix, w_in, w_out, cmp_pos_k, cmp_pos_v, cmp_k_w1, cmp_k_w2, cmp_v_w1, cmp_v_w2,
           norm_ffn, w_rg, b_rg, w_re, b_re, w_gate, w_up, w_down, out_gain):
    b, t, d = h.shape
    n = b * t
    assert t % (QB * DIL_PATTERNS[-1][1]) == 0 and t // SEL_BLOCK <= LANES and n % MOE_TM == 0
    x2d = h.reshape(n, d)
    seq = lambda a: a if a.ndim == 3 else a.reshape(b, a.shape[0] // b, a.shape[-1])
    qa, ka, va, qb, kc, vc, ks_t, vs0aug, vs1aug, kw, vw0aug, vw1aug, gates = map(
        seq, _inproj(x2d, norm_mix, _permute_w_in(w_in), t))
    f_a = _bias_1d(rel_bias[:, :DIL_HEADS])
    f_b = _bias_1d(rel_bias[:, DIL_HEADS:]) * LOG2E
    o_a = _mixer_a(qa, ka, va, jnp.stack([_dil_bias(f_a, dil) for _, dil in DIL_PATTERNS]))
    kcmp, vcmp = _compress(kc, vc, cmp_pos_k, cmp_pos_v, cmp_k_w1, cmp_k_w2, cmp_v_w1, cmp_v_w2)
    o_cmp, sel = _compressed_branch(qb, kcmp, vcmp, gates, f_b)
    o_sel = _selected_branch(qb, sel, gates, ks_t, vs0aug, vs1aug, f_b)
    o_win = _window_branch(qb, gates, kw, vw0aug, vw1aug, f_b)
    b_parts = [o.reshape(n, DB) for o in (o_cmp, o_sel, o_win)]
    w_router, b_router = _router_weights(w_rg, b_rg, w_re, b_re)
    h2, hn, comb = _outproj(x2d, o_a.reshape(n, DA), b_parts, _permute_w_out(w_out), norm_ffn, w_router, b_router)
    return _moe(h2, hn, comb, w_gate.astype(BF16), w_up.astype(BF16), w_down.astype(BF16), out_gain)


def kernel(x, rel_bias, norm_mix, w_in, w_out, cmp_pos_k, cmp_pos_v, cmp_k_w1, cmp_k_w2, cmp_v_w1, cmp_v_w2,
           norm_ffn, w_router_group, b_router_group, w_router_expert, b_router_expert, w_gate, w_up, w_down,
           norm_final):
    depth = norm_mix.shape[0]
    assert depth == 1, "the final RMSNorm is fused into the last layer's expert kernel"
    out = _layer(x, rel_bias, norm_mix[0], w_in[0], w_out[0], cmp_pos_k[0], cmp_pos_v[0], cmp_k_w1[0], cmp_k_w2[0],
                 cmp_v_w1[0], cmp_v_w2[0], norm_ffn[0], w_router_group[0], b_router_group[0], w_router_expert[0],
                 b_router_expert[0], w_gate[0], w_up[0], w_down[0], norm_final)
    return out.reshape(x.shape)
```

```python
import functools
import math

import jax
import jax.numpy as jnp
import numpy as np
from jax import lax
from jax.experimental import pallas as pl
from jax.experimental.pallas import tpu as pltpu

HEAD_DIM = 64
DIL_HEADS = 6
NSA_KV_HEADS = 2
NSA_GROUP = 5
NSA_HEADS = NSA_KV_HEADS * NSA_GROUP
N_HEADS = DIL_HEADS + NSA_HEADS
DIL_PATTERNS = ((128, 1), (512, 4), (2048, 16))
CMP_BLOCK = 32
CMP_STRIDE = 16
CMP_HIDDEN = 256
SEL_BLOCK = 64
SEL_TOPK = 16
WIN = 512
N_FORCED = 3
N_BUCKETS = 32
MAX_DISTANCE = 2048
N_GROUPS = 4
EXPERTS_PER_GROUP = 4
N_EXPERTS = N_GROUPS * EXPERTS_PER_GROUP
D_EXPERT = 512
EPS = 1e-6

LANES = 128
QB = 128
NEG = -1.0e30
LOG2E = math.log2(math.e)
DA = DIL_HEADS * HEAD_DIM
DB = NSA_HEADS * HEAD_DIM
N_ROWGROUPS = NSA_HEADS
SEL_KT = 256
VMEM_LIMIT = 56 * 1024 * 1024

F32 = jnp.float32
BF16 = jnp.bfloat16
NT_DIMS = (((1,), (1,)), ((), ()))


def _nt_dot(a, b):
    return lax.dot_general(a, b, NT_DIMS, preferred_element_type=F32)


def _dot(a, b):
    return jnp.dot(a, b, preferred_element_type=F32)


def _bucket_np(dist):
    dist = np.maximum(np.asarray(dist, np.int64), 0)
    max_exact = N_BUCKETS // 2
    x = np.maximum(dist, 1).astype(np.float32) / np.float32(max_exact)
    large = max_exact + (np.log(x) / np.float32(math.log(MAX_DISTANCE / max_exact))
                         * np.float32(N_BUCKETS - max_exact)).astype(np.int32)
    large = np.minimum(large, N_BUCKETS - 1)
    return np.where(dist < max_exact, dist, large).astype(np.int32)


BIAS_LEN = 4096


def _bias_1d(rel_bias_heads):
    onehot = (_bucket_np(np.arange(BIAS_LEN))[None, :] == np.arange(N_BUCKETS)[:, None]).astype(np.float32)
    return jnp.dot(rel_bias_heads.T.astype(F32), jnp.asarray(onehot), precision=lax.Precision.HIGHEST)


def _extend(f, lo, hi):
    assert hi <= f.shape[-1]
    if lo >= 0:
        return f[..., lo:hi]
    pad = jnp.full(f.shape[:-1] + (-lo,), NEG, f.dtype)
    return jnp.concatenate([pad, f[..., :hi]], axis=-1)


def _toeplitz(w, q, c):
    n = q + c - 1
    assert w.shape[-1] == n
    lead = w.shape[:-1]
    wp = jnp.concatenate([w, jnp.zeros(lead + (1,), w.dtype)], axis=-1)
    flat = jnp.broadcast_to(wp[..., None, :], lead + (q, n + 1)).reshape(lead + (q * (n + 1),))
    return flat[..., :q * n].reshape(lead + (q, n))[..., q - 1:q - 1 + c]


def _toeplitz_of(fn_vals, lo, q, c):
    return _toeplitz(jnp.flip(fn_vals, axis=-1), q, c)


IN_TM = 512
C_QA, C_KA, C_VA = 0, DA, 2 * DA
C_QB = 3 * DA
C_KC = C_QB + DB
C_VC, C_KS, C_VS, C_KW, C_VW, C_GT = (C_KC + LANES * i for i in range(1, 7))
N_COLS = C_GT + LANES


def _permute_w_in(w_in):
    scale = 1.0 / math.sqrt(HEAD_DIM)
    sizes = [DA] * 3 + [DB] + [NSA_KV_HEADS * HEAD_DIM] * 6 + [3 * NSA_HEADS]
    offs = np.concatenate([[0], np.cumsum(sizes)])
    part = lambda i: w_in[:, offs[i]:offs[i + 1]]
    d = w_in.shape[0]
    qb = part(3).reshape(d, NSA_KV_HEADS, NSA_GROUP, HEAD_DIM).transpose(0, 2, 1, 3).reshape(d, DB)
    gt = part(10).reshape(d, NSA_KV_HEADS, NSA_GROUP, 3).transpose(0, 3, 2, 1).reshape(d, 3 * NSA_HEADS)
    gt = jnp.pad(gt, ((0, 0), (0, LANES - 3 * NSA_HEADS)))
    cols = [part(0) * scale, part(1), part(2), qb * (scale * LOG2E)] + [part(i) for i in range(4, 10)] + [gt]
    return jnp.concatenate(cols, axis=1).astype(BF16)


def _inproj_kernel(seq_len, x_ref, g_ref, w_ref, qa_ref, ka_ref, va_ref, qb_ref, kc_ref, vc_ref,
                   kst_ref, vs0_ref, vs1_ref, kw_ref, vw0_ref, vw1_ref, gates_ref, stage_ref):
    x = x_ref[...]
    xn = (x * lax.rsqrt(jnp.mean(x * x, axis=-1, keepdims=True) + EPS) * g_ref[...]).astype(BF16)
    seg = lambda a, n: _dot(xn, w_ref[:, a:a + n])
    qa_ref[...] = seg(C_QA, DA).astype(BF16)
    ka_ref[...] = seg(C_KA, DA).astype(BF16)
    va_ref[...] = seg(C_VA, DA).astype(BF16)
    qb_ref[...] = seg(C_QB, DB).astype(BF16)
    kw_ref[...] = seg(C_KW, LANES).astype(BF16)
    tm = x.shape[0]
    for col, out_ref in ((C_KC, kc_ref), (C_VC, vc_ref)):
        stage_ref[...] = seg(col, LANES)
        for j in range(CMP_STRIDE):
            out_ref[:, j * LANES:(j + 1) * LANES] = stage_ref[pl.ds(j, tm // CMP_STRIDE, stride=CMP_STRIDE), :].astype(BF16)
    tok_t = (pl.program_id(0) * tm) % seq_len + lax.broadcasted_iota(jnp.int32, (LANES, tm), 1)
    blk_t = lax.broadcasted_iota(jnp.int32, (LANES, tm), 0)
    stage_ref[...] = seg(C_KS, LANES)
    kst_ref[0, 0:LANES, :] = stage_ref[...].T.astype(BF16)
    kst_ref[0, LANES:2 * LANES, :] = jnp.where(blk_t == tok_t // SEL_BLOCK, 1.0, 0.0).astype(BF16)
    low = lax.broadcasted_iota(jnp.int32, (tm, LANES), 1) < HEAD_DIM
    for col, ref0, ref1 in ((C_VS, vs0_ref, vs1_ref), (C_VW, vw0_ref, vw1_ref)):
        v = seg(col, LANES)
        ref0[...] = jnp.where(low, v, 1.0).astype(BF16)
        ref1[...] = jnp.where(low, 1.0, v).astype(BF16)
    gates_ref[...] = jax.nn.sigmoid(seg(C_GT, LANES))


def _inproj(x2d, gain, w_perm, seq_len):
    n, d = x2d.shape
    row = lambda w: pl.BlockSpec((IN_TM, w), lambda i: (i, 0))
    rows = lambda w: (jax.ShapeDtypeStruct((n, w), BF16), row(w))
    chunks = (jax.ShapeDtypeStruct((n // CMP_STRIDE, CMP_STRIDE * LANES), BF16),
              pl.BlockSpec((IN_TM // CMP_STRIDE, CMP_STRIDE * LANES), lambda i: (i, 0)))
    per_seq = seq_len // IN_TM
    keys_t = (jax.ShapeDtypeStruct((n // seq_len, 2 * LANES, seq_len), BF16),
              pl.BlockSpec((1, 2 * LANES, IN_TM), lambda i: (i // per_seq, 0, i % per_seq)))
    outs = [rows(DA), rows(DA), rows(DA), rows(DB), chunks, chunks, keys_t] + [rows(LANES)] * 5
    outs.append((jax.ShapeDtypeStruct((n, LANES), F32), row(LANES)))
    return pl.pallas_call(
        functools.partial(_inproj_kernel, seq_len),
        grid=(n // IN_TM,),
        in_specs=[row(d), pl.BlockSpec((1, d), lambda i: (0, 0)), pl.BlockSpec((d, N_COLS), lambda i: (0, 0))],
        out_specs=[spec for _, spec in outs],
        out_shape=[shape for shape, _ in outs],
        scratch_shapes=[pltpu.VMEM((IN_TM, LANES), F32)],
        compiler_params=pltpu.CompilerParams(dimension_semantics=("parallel",), vmem_limit_bytes=VMEM_LIMIT),
        name="inproj",
    )(x2d, gain.reshape(1, d), w_perm)


def _embed_pair(w, n_tok):
    c = w.shape[1]
    w4 = w.reshape(n_tok, 1, HEAD_DIM, 1, c) * jnp.eye(NSA_KV_HEADS, dtype=w.dtype).reshape(1, 2, 1, 2, 1)
    return w4.reshape(n_tok * 2 * HEAD_DIM, 2 * c)


def _gelu_tanh(x):
    return 0.5 * x * (1.0 + jnp.tanh(math.sqrt(2.0 / math.pi) * (x + 0.044715 * (x * x * x))))


def _compress_kernel(ck_ref, cv_ref, posk_ref, posv_ref, wk1a, wk1b, wk2, wv1a, wv1b, wv2,
                     kout_ref, vout_ref, shift_ref):
    ncp = ck_ref.shape[1]
    for c_ref, pos_ref, w1a, w1b, w2, out_ref in ((ck_ref, posk_ref, wk1a, wk1b, wk2, kout_ref),
                                                  (cv_ref, posv_ref, wv1a, wv1b, wv2, vout_ref)):
        c = c_ref[0].astype(F32)
        first = _dot((c + pos_ref[0:1, :]).astype(BF16), w1a[...])
        second = _dot((c + pos_ref[1:2, :]).astype(BF16), w1b[...])
        shift_ref[0:ncp, :] = second
        shift_ref[ncp:ncp + 8, :] = jnp.zeros((8, second.shape[1]), F32)
        hidden = _gelu_tanh(first + shift_ref[1:ncp + 1, :])
        out_ref[0] = _dot(hidden.astype(BF16), w2[...]).astype(BF16)


def _compress(kc, vc, pos_k, pos_v, k_w1, k_w2, v_w1, v_w2):
    b, ncp, wide = kc.shape
    half = CMP_STRIDE * HEAD_DIM

    def prep(w1, w2, pos):
        pos_pair = jnp.broadcast_to(pos.reshape(2, CMP_STRIDE, 1, HEAD_DIM), (2, CMP_STRIDE, 2, HEAD_DIM))
        return (_embed_pair(w1[:half], CMP_STRIDE).astype(BF16), _embed_pair(w1[half:], CMP_STRIDE).astype(BF16),
                jnp.kron(jnp.eye(NSA_KV_HEADS, dtype=w2.dtype), w2).astype(BF16),
                pos_pair.reshape(2, wide).astype(F32))

    wk1a, wk1b, wk2, posk = prep(k_w1, k_w2, pos_k)
    wv1a, wv1b, wv2, posv = prep(v_w1, v_w2, pos_v)
    full = lambda a: pl.BlockSpec(a.shape, lambda i: (0,) * a.ndim)
    tok = pl.BlockSpec((1, ncp, wide), lambda i: (i, 0, 0))
    out = pl.BlockSpec((1, ncp, LANES), lambda i: (i, 0, 0))
    return pl.pallas_call(
        _compress_kernel,
        grid=(b,),
        in_specs=[tok, tok, full(posk), full(posv), full(wk1a), full(wk1b), full(wk2), full(wv1a), full(wv1b), full(wv2)],
        out_specs=[out, out],
        out_shape=[jax.ShapeDtypeStruct((b, ncp, LANES), BF16)] * 2,
        scratch_shapes=[pltpu.VMEM((ncp + 8, 2 * CMP_HIDDEN), F32)],
        compiler_params=pltpu.CompilerParams(dimension_semantics=("parallel",), vmem_limit_bytes=VMEM_LIMIT),
        name="compress",
    )(kc, vc, posk, posv, wk1a, wk1b, wk2, wv1a, wv1b, wv2)


def _pair_masks(rows):
    lane = lax.broadcasted_iota(jnp.int32, (rows, LANES), 1)
    return lane < HEAD_DIM


A_SUPER = QB * DIL_PATTERNS[-1][1]


def _mixer_a_kernel(q_ref, kp_ref, kc_ref, vp_ref, vc_ref, bias_ref, out_ref, qf_ref, kf_ref, vf_ref, o_ref, lse_ref):
    first = pl.program_id(1) == 0
    n_pairs = DIL_HEADS // 2
    for p in range(n_pairs):
        cs = slice(p * LANES, (p + 1) * LANES)
        qf_ref[p] = q_ref[0, :, cs].astype(F32)
        kf_ref[p, 0:A_SUPER, :] = kp_ref[0, :, cs].astype(F32)
        kf_ref[p, A_SUPER:2 * A_SUPER, :] = kc_ref[0, :, cs].astype(F32)
        vf_ref[p, 0:A_SUPER, :] = vp_ref[0, :, cs].astype(F32)
        vf_ref[p, A_SUPER:2 * A_SUPER, :] = vc_ref[0, :, cs].astype(F32)
    low = _pair_masks(QB)
    in_prev = lax.broadcasted_iota(jnp.int32, (QB, 2 * QB), 1) < QB
    zero = jnp.zeros((QB, LANES), BF16)

    def chunk(idx, dil, q_base, k_base, q_span, k_span, off, at_start):
        rows = lambda size: pl.ds(off, size) if dil == 1 else pl.ds(off, size, stride=dil)
        q_win = lambda ref: ref.at[pl.ds(pl.multiple_of(q_base, 8), q_span), :]
        k_win = lambda ref: ref.at[pl.ds(pl.multiple_of(k_base, 8), k_span), :]
        q_rows = rows(QB)
        prev_mask = jnp.where(jnp.logical_and(in_prev, jnp.logical_and(first, at_start)), NEG, 0.0)
        for p in range(n_pairs):
            q = q_win(qf_ref.at[p])[q_rows, :].astype(BF16)
            keys = k_win(kf_ref.at[p])[rows(2 * QB), :].astype(BF16)
            vals = k_win(vf_ref.at[p])[rows(2 * QB), :].astype(BF16)
            o_win, lse_win = q_win(o_ref.at[p]), q_win(lse_ref.at[p])
            lhs = jnp.concatenate([jnp.where(low, q, zero), jnp.where(low, zero, q)], axis=0)
            bias = jnp.concatenate([bias_ref[idx, 2 * p] + prev_mask, bias_ref[idx, 2 * p + 1] + prev_mask], axis=0)
            s = _nt_dot(lhs, keys) + bias
            m = jnp.max(s, axis=1, keepdims=True)
            e = jnp.exp(s - m)
            l = jnp.sum(e, axis=1, keepdims=True)
            pv = _dot(e.astype(BF16), vals) * (1.0 / l)
            lse = m + jnp.log(l)
            o_new = jnp.where(low, pv[:QB], pv[QB:])
            l_new = jnp.where(low, jnp.broadcast_to(lse[:QB], (QB, LANES)), jnp.broadcast_to(lse[QB:], (QB, LANES)))
            if idx > 0:
                o_old, l_old = o_win[q_rows, :], lse_win[q_rows, :]
                mx = jnp.maximum(l_old, l_new)
                w_old, w_new = jnp.exp(l_old - mx), jnp.exp(l_new - mx)
                tot = w_old + w_new
                o_new = (w_old * o_old + w_new * o_new) * (1.0 / tot)
                l_new = mx + jnp.log(tot)
            o_win[q_rows, :] = o_new
            lse_win[q_rows, :] = l_new

    def loop(n, body):
        lax.fori_loop(0, n, lambda i, carry: (body(i), carry)[1], 0, unroll=min(n, 4))

    for idx, (_, dil) in enumerate(DIL_PATTERNS):
        span = QB * dil
        n_chunks = A_SUPER // span
        if n_chunks > 1:
            for r in range(dil):
                loop(n_chunks, lambda c, idx=idx, dil=dil, span=span, r=r:
                     chunk(idx, dil, span * c, A_SUPER + span * (c - 1), span, 2 * span, r, c == 0))
        else:
            for r in range(8):
                loop(dil // 8, lambda hi, idx=idx, dil=dil, span=span, r=r:
                     chunk(idx, dil, 8 * hi, 8 * hi, span - 8, 2 * span - 8, r, True))
    for p in range(n_pairs):
        out_ref[0, :, p * LANES:(p + 1) * LANES] = o_ref[p].astype(BF16)


def _mixer_a(qa, ka, va, bias):
    b, t, _ = qa.shape
    cur = pl.BlockSpec((1, A_SUPER, DA), lambda bi, i: (bi, i, 0))
    prev = pl.BlockSpec((1, A_SUPER, DA), lambda bi, i: (bi, jnp.maximum(i - 1, 0), 0))
    return pl.pallas_call(
        _mixer_a_kernel,
        grid=(b, t // A_SUPER),
        in_specs=[cur, prev, cur, prev, cur, pl.BlockSpec(bias.shape, lambda bi, i: (0, 0, 0, 0))],
        out_specs=cur,
        out_shape=jax.ShapeDtypeStruct((b, t, DA), BF16),
        scratch_shapes=[pltpu.VMEM((DIL_HEADS // 2, rows, LANES), F32) for rows in (A_SUPER, 2 * A_SUPER, 2 * A_SUPER, A_SUPER, A_SUPER)],
        compiler_params=pltpu.CompilerParams(dimension_semantics=("parallel", "parallel"), vmem_limit_bytes=VMEM_LIMIT),
        name="mixer_a",
    )(qa, ka, ka, va, va, bias)


def _dil_bias(f_a, dil):
    steps = DIL_PATTERNS[0][0]
    g = f_a[:, 0:dil * steps + 1:dil]
    lo, hi = QB - (2 * QB - 1), QB + QB
    vals = jnp.concatenate([jnp.full((DIL_HEADS, -lo), NEG, F32), g, jnp.full((DIL_HEADS, hi - steps - 1), NEG, F32)], axis=1)
    return _toeplitz_of(vals, lo, QB, 2 * QB)


CMP_TILE_KEYS = LANES
CMP_TILE_SPAN = CMP_TILE_KEYS * CMP_STRIDE // QB
CMP_CONST_DELTA = 28


def _cmp_bias(f_b):
    per = QB // CMP_STRIDE
    n_rows = per * (CMP_CONST_DELTA + 1)
    m_lo, m_hi = -(CMP_TILE_KEYS - 1), n_rows
    f_b = f_b.astype(BF16)
    base = _extend(f_b, CMP_STRIDE * m_lo - (CMP_BLOCK - 1), CMP_STRIDE * m_hi - (CMP_BLOCK - 1))
    g = base.reshape(NSA_HEADS, m_hi - m_lo, CMP_STRIDE).transpose(0, 2, 1)
    t = _toeplitz_of(g, m_lo, n_rows, CMP_TILE_KEYS)
    t = t.reshape(NSA_HEADS, CMP_STRIDE, CMP_CONST_DELTA + 1, per, CMP_TILE_KEYS).transpose(2, 0, 3, 1, 4)
    t = t.reshape(CMP_CONST_DELTA + 1, NSA_HEADS, QB, CMP_TILE_KEYS)
    return jnp.concatenate([jnp.full((1,) + t.shape[1:], NEG, t.dtype), t], axis=0)


def _overlap_matrix_t(ncp, n_sel_pad):
    n = np.arange(ncp)[None, :] * CMP_STRIDE
    s = np.arange(n_sel_pad)[:, None] * SEL_BLOCK
    ov = np.clip(np.minimum(n + CMP_BLOCK, s + SEL_BLOCK) - np.maximum(n, s), 0, None) / CMP_BLOCK
    return jnp.asarray(ov, BF16)


def _gate_tile(gates_ref, branch, g, row0=0):
    c = branch * NSA_HEADS + g * 2
    low = _pair_masks(QB)
    return jnp.where(low, jnp.broadcast_to(gates_ref[0, row0:row0 + QB, c:c + 1], (QB, LANES)),
                     jnp.broadcast_to(gates_ref[0, row0:row0 + QB, c + 1:c + 2], (QB, LANES)))


def _masked_q(qb_ref, g, kv, row0=0):
    q = qb_ref[0, row0:row0 + QB, g * LANES:(g + 1) * LANES]
    low = _pair_masks(QB)
    keep = low if kv == 0 else jnp.logical_not(low)
    return jnp.where(keep, q, jnp.zeros_like(q))


def _cmp_kernel(n_tiles, qb_ref, kcmp_ref, vcmp_ref, gates_ref, ov_ref, *rest):
    tbl_refs, (oc_ref, sel_ref, q_ref, s_ref, p_ref, pv_ref) = rest[:n_tiles], rest[n_tiles:]
    qblk = pl.program_id(1)
    t0 = qblk * QB
    low = _pair_masks(QB)
    for kv in range(NSA_KV_HEADS):
        for g in range(NSA_GROUP):
            r = kv * NSA_GROUP + g
            q_ref[r * QB:(r + 1) * QB, :] = _masked_q(qb_ref, g, kv)

    def attend(n_vis):
        kc = n_vis * CMP_TILE_KEYS
        n_blk = n_vis * CMP_TILE_SPAN * QB // SEL_BLOCK
        s_ref[:, 0:kc] = _nt_dot(q_ref[...], kcmp_ref[0, 0:kc, :])
        blk = lax.broadcasted_iota(jnp.int32, (n_blk, QB), 0)
        cur = (t0 + lax.broadcasted_iota(jnp.int32, (n_blk, QB), 1)) // SEL_BLOCK
        blk_f = blk.astype(F32)
        forced = (blk == cur) | (blk == cur - 1) | (blk == 0)
        causal = blk <= cur
        scores = []
        for kv in range(NSA_KV_HEADS):
            psum = jnp.zeros((QB, kc), F32)
            for g in range(NSA_GROUP):
                r = kv * NSA_GROUP + g
                rows = slice(r * QB, (r + 1) * QB)
                s = s_ref[rows, 0:kc] + jnp.concatenate([tbl_refs[c][0, r].astype(F32) for c in range(n_vis)], axis=1)
                m = jnp.max(s, axis=1, keepdims=True)
                e = jnp.exp2(s - m)
                den = jnp.sum(e, axis=1, keepdims=True)
                p = e * jnp.where(m > 0.5 * NEG, 1.0 / den, 0.0)
                psum = psum + p
                p_ref[rows, 0:kc] = p.astype(BF16)
            hi = psum.astype(BF16)
            lo = (psum - hi.astype(F32)).astype(BF16)
            ov_t = ov_ref[0:n_blk, 0:kc]
            imp_t = _nt_dot(ov_t, hi) + _nt_dot(ov_t, lo)
            scores.append(jnp.where(forced, -jnp.inf, jnp.where(causal, imp_t, -1.0)))
        pv_ref[...] = _dot(p_ref[:, 0:kc], vcmp_ref[0, 0:kc, :])

        def pick(_, carry):
            new = []
            for val, sel in carry:
                mx = jnp.max(val, axis=0, keepdims=True)
                idx = jnp.min(jnp.where(val == mx, blk_f, float(n_blk)), axis=0, keepdims=True)
                hit = blk_f == idx
                new.append((jnp.where(hit, -jnp.inf, val), jnp.where(hit, 1.0, sel)))
            return tuple(new)

        taken = jnp.where(forced, 1.0, 0.0)
        picked = lax.fori_loop(0, SEL_TOPK - N_FORCED, pick, tuple((v, taken) for v in scores))
        eye = jnp.where(lax.broadcasted_iota(jnp.int32, (QB, QB), 0) == lax.broadcasted_iota(jnp.int32, (QB, QB), 1),
                        1.0, 0.0).astype(BF16)
        for kv in range(NSA_KV_HEADS):
            sel_t = jnp.where(causal, picked[kv][1], 0.0).astype(BF16)
            sel_ref[0, kv, :, 0:n_blk] = _nt_dot(eye, sel_t).astype(BF16)
            if n_blk < LANES:
                sel_ref[0, kv, :, n_blk:LANES] = jnp.zeros((QB, LANES - n_blk), BF16)

    n_vis = qblk // CMP_TILE_SPAN + 1
    for w in range(1, n_tiles + 1):
        pl.when(n_vis == w)(functools.partial(attend, w))

    for g in range(NSA_GROUP):
        o0, o1 = pv_ref[g * QB:(g + 1) * QB, :], pv_ref[(NSA_GROUP + g) * QB:(NSA_GROUP + g + 1) * QB, :]
        oc_ref[0, :, g * LANES:(g + 1) * LANES] = (jnp.where(low, o0, o1) * _gate_tile(gates_ref, 0, g)).astype(BF16)


def _compressed_branch(qb, kcmp, vcmp, gates, f_b):
    b, t, _ = qb.shape
    ncp = kcmp.shape[1]
    n_tiles = ncp // CMP_TILE_KEYS
    tbl = _cmp_bias(f_b)
    ov = _overlap_matrix_t(ncp, LANES)

    def tbl_spec(c):
        return pl.BlockSpec((1, NSA_HEADS, QB, CMP_TILE_KEYS),
                            lambda bi, i: (jnp.clip(i - CMP_TILE_SPAN * c, -1, CMP_CONST_DELTA) + 1, 0, 0, 0))

    blockq = lambda w: pl.BlockSpec((1, QB, w), lambda bi, i: (bi, i, 0))
    batch = lambda a: pl.BlockSpec((1,) + a.shape[1:], lambda bi, i: (bi, 0, 0))
    return pl.pallas_call(
        functools.partial(_cmp_kernel, n_tiles),
        grid=(b, t // QB),
        in_specs=[blockq(DB), batch(kcmp), batch(vcmp), blockq(LANES), pl.BlockSpec(ov.shape, lambda bi, i: (0, 0))]
                 + [tbl_spec(c) for c in range(n_tiles)],
        out_specs=[blockq(DB), pl.BlockSpec((1, NSA_KV_HEADS, QB, LANES), lambda bi, i: (bi, 0, i, 0))],
        out_shape=[jax.ShapeDtypeStruct((b, t, DB), BF16), jax.ShapeDtypeStruct((b, NSA_KV_HEADS, t, LANES), BF16)],
        scratch_shapes=[pltpu.VMEM((N_ROWGROUPS * QB, LANES), BF16), pltpu.VMEM((N_ROWGROUPS * QB, ncp), F32),
                        pltpu.VMEM((N_ROWGROUPS * QB, ncp), BF16), pltpu.VMEM((N_ROWGROUPS * QB, LANES), F32)],
        compiler_params=pltpu.CompilerParams(dimension_semantics=("parallel", "parallel"), vmem_limit_bytes=VMEM_LIMIT),
        name="nsa_compressed",
    )(qb, kcmp, vcmp, gates, ov, *([tbl] * n_tiles))


SEL_NEAR = 13
SEL_FAR_BLOCK, SEL_NEAR_BLOCK = 8, 8


def _sel_bias(f_b):
    n_off = SEL_NEAR + 1
    cols = QB * n_off
    rel = f_b - f_b[:, BIAS_LEN - 1:]
    lo = -(QB - 1)
    big = _toeplitz_of(_extend(rel, lo, lo + QB + cols - 1), lo, QB, cols)
    tiles = jnp.flip(big.reshape(NSA_HEADS, QB, n_off, QB).transpose(2, 0, 1, 3), axis=0)
    return jnp.concatenate([jnp.zeros((1,) + tiles.shape[1:], F32), tiles], axis=0)


def _pair_ratio(acc0, acc1):
    low = _pair_masks(acc0.shape[0])
    den = pltpu.roll(jnp.where(low, acc1, acc0), HEAD_DIM, axis=1)
    return jnp.where(low, acc0, acc1) * (1.0 / den)


def _sel_kernel(qb_ref, sel_ref, gates_ref, cfar_ref, ks_ref, vs0_ref, vs1_ref, tbl_ref, out_ref,
                qaug_ref, s_ref, s1_ref, acc_ref, m_ref, alpha_ref):
    qblk = pl.program_id(1)
    for kv in range(NSA_KV_HEADS):
        unchosen = jnp.where(sel_ref[0, kv].astype(F32) > 0.0, 0.0, NEG)
        for g in range(NSA_GROUP):
            r = kv * NSA_GROUP + g
            rows = slice(r * QB, (r + 1) * QB)
            qaug_ref[rows, 0:LANES] = _masked_q(qb_ref, g, kv)
            qaug_ref[rows, LANES:2 * LANES] = (unchosen + cfar_ref[r:r + 1, :]).astype(BF16)
    acc_ref[...] = jnp.zeros_like(acc_ref)
    m_ref[...] = jnp.full_like(m_ref, NEG)

    last_tile = ks_ref.shape[2] // SEL_KT - 1

    def scores(j, dst_ref):
        start = pl.multiple_of(jnp.minimum(j, last_tile) * SEL_KT, SEL_KT)
        dst_ref[...] = _dot(qaug_ref[...], ks_ref[0, :, pl.ds(start, SEL_KT)])

    def consume(j, src_ref, near):
        start = pl.multiple_of(j * SEL_KT, SEL_KT)
        for r in range(N_ROWGROUPS):
            rows = slice(r * QB, (r + 1) * QB)
            s = src_ref[rows, :]
            if near:
                e1 = jnp.clip(qblk - 2 * j + 1, 0, SEL_NEAR + 1)
                e2 = jnp.clip(qblk - 2 * j, 0, SEL_NEAR + 1)
                s = s + jnp.concatenate([tbl_ref[e1, r], tbl_ref[e2, r]], axis=1)
                src_ref[rows, :] = s
            m_old = m_ref[rows, :]
            m_new = jnp.maximum(m_old, jnp.max(s, axis=1, keepdims=True))
            alpha_ref[rows, :] = jnp.exp2(m_old - m_new)
            m_ref[rows, :] = m_new
        for r in range(N_ROWGROUPS):
            rows = slice(r * QB, (r + 1) * QB)
            vals = (vs0_ref if r < NSA_GROUP else vs1_ref)[0, pl.ds(start, SEL_KT), :]
            m_new = m_ref[rows, :]
            p = jnp.exp2(src_ref[rows, :] - jnp.concatenate([m_new, m_new], axis=1))
            acc_ref[rows, :] = alpha_ref[rows, :] * acc_ref[rows, :] + _dot(p.astype(BF16), vals)

    def tile_run(first, count, near):
        bufs = (s_ref, s1_ref)
        for u in range(count):
            scores(first + u + 1, bufs[(u + 1) % 2])
            consume(first + u, bufs[u % 2], near)

    def run_pairs(first, pairs, near, max_block):
        tiles = 2 * pairs
        lax.fori_loop(0, tiles // max_block, lambda i, c: (tile_run(first + max_block * i, max_block, near), c)[1], 0)
        done = tiles // max_block * max_block
        size = max_block // 2
        while size >= 2:
            start = first + done
            pl.when((tiles - done) >= size)(functools.partial(tile_run, start, size, near))
            done = done + jnp.where((tiles - done) >= size, size, 0)
            size //= 2

    n_pairs = ((qblk + 2) // 2 + 1) // 2
    n_far = jnp.maximum((qblk - (SEL_NEAR - 1)) // 2, 0) // 2
    scores(0, s_ref)
    run_pairs(0, n_far, False, SEL_FAR_BLOCK)
    run_pairs(2 * n_far, n_pairs - n_far, True, SEL_NEAR_BLOCK)
    for g in range(NSA_GROUP):
        ratio = _pair_ratio(acc_ref[g * QB:(g + 1) * QB, :], acc_ref[(NSA_GROUP + g) * QB:(NSA_GROUP + g + 1) * QB, :])
        out_ref[0, :, g * LANES:(g + 1) * LANES] = (ratio * _gate_tile(gates_ref, 1, g)).astype(BF16)


def _selected_branch(qb, sel, gates, ks_t, vs0aug, vs1aug, f_b):
    b, t, _ = qb.shape
    tbl = _sel_bias(f_b)
    cfar = jnp.broadcast_to(f_b[:, BIAS_LEN - 1:], (NSA_HEADS, LANES))
    cfar = jnp.pad(cfar, ((0, 16 - NSA_HEADS), (0, 0)))
    blockq = lambda w: pl.BlockSpec((1, QB, w), lambda bi, i: (bi, i, 0))
    batch = lambda a: pl.BlockSpec((1,) + a.shape[1:], lambda bi, i: (bi, 0, 0))
    rows = N_ROWGROUPS * QB
    return pl.pallas_call(
        _sel_kernel,
        grid=(b, t // QB),
        in_specs=[blockq(DB), pl.BlockSpec((1, NSA_KV_HEADS, QB, LANES), lambda bi, i: (bi, 0, i, 0)), blockq(LANES),
                  pl.BlockSpec(cfar.shape, lambda bi, i: (0, 0)), batch(ks_t), batch(vs0aug), batch(vs1aug),
                  pl.BlockSpec(tbl.shape, lambda bi, i: (0, 0, 0, 0))],
        out_specs=blockq(DB),
        out_shape=jax.ShapeDtypeStruct((b, t, DB), BF16),
        scratch_shapes=[pltpu.VMEM((rows, 2 * LANES), BF16), pltpu.VMEM((rows, SEL_KT), F32), pltpu.VMEM((rows, SEL_KT), F32),
                        pltpu.VMEM((rows, LANES), F32), pltpu.VMEM((rows, LANES), F32), pltpu.VMEM((rows, LANES), F32)],
        compiler_params=pltpu.CompilerParams(dimension_semantics=("parallel", "parallel"), vmem_limit_bytes=VMEM_LIMIT),
        name="nsa_selected",
    )(qb, sel, gates, cfar, ks_t, vs0aug, vs1aug, tbl)


WIN_KEYS = WIN + QB


def _win_bias(f_b):
    lo = WIN - (WIN_KEYS - 1)
    vals = _extend(f_b[:, :WIN], lo, WIN)
    vals = jnp.concatenate([vals, jnp.full((NSA_HEADS, lo + QB + WIN_KEYS - 1 - WIN), NEG, F32)], axis=1)
    return _toeplitz_of(vals, lo, QB, WIN_KEYS)


WIN_QBLOCKS = 2


def _win_kernel(qb_ref, gates_ref, kw_ref, vw0_ref, vw1_ref, tbl_ref, out_ref, q_ref, s_ref):
    col = lax.broadcasted_iota(jnp.int32, (1, WIN_KEYS), 1)
    for h in range(WIN_QBLOCKS):
        qblk = pl.program_id(1) * WIN_QBLOCKS + h
        row0 = h * QB
        for kv in range(NSA_KV_HEADS):
            for g in range(NSA_GROUP):
                r = kv * NSA_GROUP + g
                q_ref[h, r * QB:(r + 1) * QB, :] = _masked_q(qb_ref, g, kv, row0)
        start = pl.multiple_of(qblk * QB, QB)
        s_ref[h] = _nt_dot(q_ref[h], kw_ref[0, pl.ds(start, WIN_KEYS), :])
        pad_mask = jnp.where(col + qblk * QB >= WIN, 0.0, NEG)
        outs = []
        for r in range(N_ROWGROUPS):
            vals = (vw0_ref if r < NSA_GROUP else vw1_ref)[0, pl.ds(start, WIN_KEYS), :]
            s = s_ref[h, r * QB:(r + 1) * QB, :] + tbl_ref[r] + pad_mask
            e = jnp.exp2(s - jnp.max(s, axis=1, keepdims=True))
            outs.append(_dot(e.astype(BF16), vals))
        for g in range(NSA_GROUP):
            out_ref[0, row0:row0 + QB, g * LANES:(g + 1) * LANES] = (
                _pair_ratio(outs[g], outs[NSA_GROUP + g]) * _gate_tile(gates_ref, 2, g, row0)).astype(BF16)


def _window_branch(qb, gates, kw, vw0aug, vw1aug, f_b):
    b, t, _ = qb.shape
    tbl = _win_bias(f_b)
    pad_front = lambda a: jnp.pad(a, ((0, 0), (WIN, 0), (0, 0)))
    kw_pad, vw0_pad, vw1_pad = pad_front(kw), pad_front(vw0aug), pad_front(vw1aug)
    blockq = lambda w: pl.BlockSpec((1, WIN_QBLOCKS * QB, w), lambda bi, i: (bi, i, 0))
    batch = lambda a: pl.BlockSpec((1,) + a.shape[1:], lambda bi, i: (bi, 0, 0))
    rows = N_ROWGROUPS * QB
    return pl.pallas_call(
        _win_kernel,
        grid=(b, t // (WIN_QBLOCKS * QB)),
        in_specs=[blockq(DB), blockq(LANES), batch(kw_pad), batch(vw0_pad), batch(vw1_pad),
                  pl.BlockSpec(tbl.shape, lambda bi, i: (0, 0, 0))],
        out_specs=blockq(DB),
        out_shape=jax.ShapeDtypeStruct((b, t, DB), BF16),
        scratch_shapes=[pltpu.VMEM((WIN_QBLOCKS, rows, LANES), BF16), pltpu.VMEM((WIN_QBLOCKS, rows, WIN_KEYS), F32)],
        compiler_params=pltpu.CompilerParams(dimension_semantics=("parallel", "parallel"), vmem_limit_bytes=VMEM_LIMIT),
        name="nsa_window",
    )(qb, gates, kw_pad, vw0_pad, vw1_pad, tbl)


OUT_TM = 256
C_GROUP = N_EXPERTS


def _outproj_kernel(x_ref, oa_ref, oc_ref, os_ref, ow_ref,
                    wout_ref, g_ref, wr_ref, br_ref, h_ref, hn_ref, comb_ref):
    ob = oc_ref[...].astype(F32) + os_ref[...].astype(F32) + ow_ref[...].astype(F32)
    y = _dot(oa_ref[...], wout_ref[0:DA, :]) + _dot(ob.astype(BF16), wout_ref[DA:DA + DB, :])
    h = x_ref[...] + y
    h_ref[...] = h
    hn = h * lax.rsqrt(jnp.mean(h * h, axis=-1, keepdims=True) + EPS) * g_ref[...]
    hn_hi = hn.astype(BF16)
    hn_ref[...] = hn_hi
    hn_lo = (hn - hn_hi.astype(F32)).astype(BF16)
    both = _dot(hn_hi, wr_ref[...])
    logits = both[:, 0:LANES] + both[:, LANES:2 * LANES] + _dot(hn_lo, wr_ref[:, 0:LANES]) + br_ref[...]
    lane = lax.broadcasted_iota(jnp.int32, logits.shape, 1)
    lane_f = lane.astype(F32)
    big = float(LANES)
    gl = jnp.where((lane >= C_GROUP) & (lane < C_GROUP + N_GROUPS), logits, -jnp.inf)
    gmax = jnp.max(gl, axis=1, keepdims=True)
    gidx = jnp.min(jnp.where(gl == gmax, lane_f, big), axis=1, keepdims=True) - C_GROUP
    gprob = 1.0 / jnp.sum(jnp.exp(gl - gmax), axis=1, keepdims=True)
    grp_of_lane = (lane // EXPERTS_PER_GROUP).astype(F32)
    el = jnp.where((lane < N_EXPERTS) & (grp_of_lane == gidx), logits, -jnp.inf)
    v1 = jnp.max(el, axis=1, keepdims=True)
    i1 = jnp.min(jnp.where(el == v1, lane_f, big), axis=1, keepdims=True)
    el2 = jnp.where(lane_f == i1, -jnp.inf, el)
    v2 = jnp.max(el2, axis=1, keepdims=True)
    i2 = jnp.min(jnp.where(el2 == v2, lane_f, big), axis=1, keepdims=True)
    e2 = jnp.exp(v2 - v1)
    p1 = 1.0 / (1.0 + e2)
    comb_ref[...] = (gprob * (jnp.where(lane_f == i1, p1, 0.0) + jnp.where(lane_f == i2, e2 * p1, 0.0))
                     + jnp.where(lane == C_GROUP, gidx, 0.0))


def _outproj(x2d, o_a, b_parts, w_out_perm, gain, w_router, b_router):
    n, d = x2d.shape
    row = lambda w: pl.BlockSpec((OUT_TM, w), lambda i: (i, 0))
    full = lambda a: pl.BlockSpec(a.shape, lambda i: (0, 0))
    return pl.pallas_call(
        _outproj_kernel,
        grid=(n // OUT_TM,),
        in_specs=[row(d), row(DA)] + [row(DB)] * 3 + [full(w_out_perm), pl.BlockSpec((1, d), lambda i: (0, 0)),
                                                      full(w_router), full(b_router)],
        out_specs=[row(d), row(d), row(LANES)],
        out_shape=[jax.ShapeDtypeStruct((n, d), F32), jax.ShapeDtypeStruct((n, d), BF16),
                   jax.ShapeDtypeStruct((n, LANES), F32)],
        compiler_params=pltpu.CompilerParams(dimension_semantics=("parallel",), vmem_limit_bytes=VMEM_LIMIT),
        name="outproj_router",
    )(x2d, o_a, *b_parts, w_out_perm, gain.reshape(1, d), w_router, b_router)


MOE_TM = 1024


MOE_SUB = 128
MOE_FINAL_TM = 512


def _moe_kernel(hn_ref, comb_ref, upper_ref, wg_ref, wu_ref, wd_ref, y_ref,
                perm_ref, hs_ref, cs_ref, ys_ref, start_ref, nsub_ref):
    grp = pl.program_id(1)
    tm, d = hn_ref.shape
    n_pad = perm_ref.shape[0]

    @pl.when(grp == 0)
    def _sort():
        comb = comb_ref[...]
        lane_f = lax.broadcasted_iota(jnp.int32, (tm, LANES), 1).astype(F32)
        onehot = jnp.where(lane_f == comb[:, C_GROUP:C_GROUP + 1], 1.0, 0.0).astype(BF16)
        eye8 = jnp.where(lax.broadcasted_iota(jnp.int32, (8, LANES), 0) == lax.broadcasted_iota(jnp.int32, (8, LANES), 1),
                         1.0, 0.0).astype(BF16)
        onehot_t = _nt_dot(eye8, onehot)
        cum_t = _dot(onehot_t.astype(BF16), upper_ref[...])
        start = jnp.zeros((1, 1), F32)
        pos_t = jnp.zeros((1, tm), F32)
        for k in range(N_GROUPS):
            padded = jnp.ceil(cum_t[k:k + 1, tm - 1:tm] * (1.0 / MOE_SUB)) * MOE_SUB
            pos_t = pos_t + onehot_t[k:k + 1, :] * (start + cum_t[k:k + 1, :] - 1.0)
            start_ref[k] = start[0, 0].astype(jnp.int32)
            nsub_ref[k] = (padded[0, 0] * (1.0 / MOE_SUB)).astype(jnp.int32)
            start = start + padded
        perm = jnp.where(lax.broadcasted_iota(jnp.int32, (n_pad, tm), 0) == pos_t.astype(jnp.int32), 1.0, 0.0).astype(BF16)
        perm_ref[...] = perm
        hs_ref[...] = _dot(perm, hn_ref[...]).astype(BF16)
        c_hi = comb.astype(BF16)
        c_lo = (comb - c_hi.astype(F32)).astype(BF16)
        both = _dot(perm, jnp.concatenate([c_hi, c_lo], axis=1))
        cs_ref[...] = both[:, 0:LANES] + both[:, LANES:2 * LANES]
        ys_ref[...] = jnp.zeros_like(ys_ref)

    lane = lax.broadcasted_iota(jnp.int32, (MOE_SUB, LANES), 1)

    def segment(s, carry):
        rows = pl.ds(pl.multiple_of(start_ref[grp] + s * MOE_SUB, MOE_SUB), MOE_SUB)
        x = hs_ref[rows, :]
        weights = cs_ref[rows, :]
        acc = jnp.zeros((MOE_SUB, d), F32)
        for j in range(EXPERTS_PER_GROUP):
            gate = _dot(x, wg_ref[j])
            up = _dot(x, wu_ref[j])
            w = jnp.sum(jnp.where(lane == grp * EXPERTS_PER_GROUP + j, weights, 0.0), axis=1, keepdims=True)
            acc = acc + _dot((gate * jax.nn.sigmoid(gate) * up * w).astype(BF16), wd_ref[j])
        ys_ref[rows, :] = acc.astype(BF16)
        return carry

    lax.fori_loop(0, nsub_ref[grp], segment, 0)

    @pl.when(grp == pl.num_programs(1) - 1)
    def _unsort():
        y_ref[...] = lax.dot_general(perm_ref[...], ys_ref[...], (((0,), (0,)), ((), ())),
                                     preferred_element_type=F32).astype(BF16)


def _final_kernel(h_ref, y_ref, g_ref, out_ref):
    y = h_ref[...] + y_ref[...].astype(F32)
    out_ref[...] = y * lax.rsqrt(jnp.mean(y * y, axis=-1, keepdims=True) + EPS) * g_ref[...]


def _moe(h, hn, comb, w_gate, w_up, w_down, gain):
    n, d = h.shape
    tm = min(MOE_TM, n)
    n_pad = tm + (N_GROUPS - 1) * MOE_SUB
    row = lambda w: pl.BlockSpec((tm, w), lambda i, g: (i, 0))
    group_w = lambda a: pl.BlockSpec((EXPERTS_PER_GROUP,) + a.shape[1:], lambda i, g: (g, 0, 0))
    upper = jnp.asarray(np.triu(np.ones((tm, tm), np.float32)), BF16)
    y = pl.pallas_call(
        _moe_kernel,
        grid=(n // tm, N_GROUPS),
        in_specs=[row(d), row(LANES), pl.BlockSpec((tm, tm), lambda i, g: (0, 0)),
                  group_w(w_gate), group_w(w_up), group_w(w_down)],
        out_specs=row(d),
        out_shape=jax.ShapeDtypeStruct((n, d), BF16),
        scratch_shapes=[pltpu.VMEM((n_pad, tm), BF16), pltpu.VMEM((n_pad, d), BF16), pltpu.VMEM((n_pad, LANES), F32),
                        pltpu.VMEM((n_pad, d), BF16), pltpu.SMEM((N_GROUPS,), jnp.int32), pltpu.SMEM((N_GROUPS,), jnp.int32)],
        compiler_params=pltpu.CompilerParams(dimension_semantics=("parallel", "arbitrary"), vmem_limit_bytes=VMEM_LIMIT),
        name="moe_experts",
    )(hn, comb, upper, w_gate, w_up, w_down)
    rowf = lambda w: pl.BlockSpec((MOE_FINAL_TM, w), lambda i: (i, 0))
    return pl.pallas_call(
        _final_kernel,
        grid=(n // MOE_FINAL_TM,),
        in_specs=[rowf(d), rowf(d), pl.BlockSpec((1, d), lambda i: (0, 0))],
        out_specs=rowf(d),
        out_shape=jax.ShapeDtypeStruct((n, d), F32),
        compiler_params=pltpu.CompilerParams(dimension_semantics=("parallel",), vmem_limit_bytes=VMEM_LIMIT),
        name="residual_final_norm",
    )(h, y, gain.reshape(1, d))


def _permute_w_out(w_out):
    d = w_out.shape[1]
    wb = w_out[DA:].reshape(NSA_KV_HEADS, NSA_GROUP, HEAD_DIM, d).transpose(1, 0, 2, 3).reshape(DB, d)
    return jnp.concatenate([w_out[:DA], wb], axis=0).astype(BF16)


def _router_weights(w_group, b_group, w_expert, b_expert):
    d = w_group.shape[0]
    w = jnp.concatenate([w_expert.reshape(d, N_EXPERTS), w_group], axis=1)
    b = jnp.concatenate([b_expert.reshape(N_EXPERTS), b_group])
    pad = LANES - w.shape[1]
    w = jnp.pad(w, ((0, 0), (0, pad))).astype(F32)
    w_hi = w.astype(BF16)
    w_lo = (w - w_hi.astype(F32)).astype(BF16)
    return jnp.concatenate([w_hi, w_lo], axis=1), jnp.pad(b, (0, pad)).reshape(1, LANES).astype(F32)


def _layer(h, rel_bias, norm_mix, w_in, w_out, cmp_pos_k, cmp_pos_v, cmp_k_w1, cmp_k_w2, cmp_v_w1, cmp_v_w2,
           norm_ffn, w_rg, b_rg, w_re, b_re, w_gate, w_up, w_down, out_gain):
    b, t, d = h.shape
    n = b * t
    assert t % (QB * DIL_PATTERNS[-1][1]) == 0 and t // SEL_BLOCK <= LANES and n % MOE_TM == 0
    x2d = h.reshape(n, d)
    seq = lambda a: a if a.ndim == 3 else a.reshape(b, a.shape[0] // b, a.shape[-1])
    qa, ka, va, qb, kc, vc, ks_t, vs0aug, vs1aug, kw, vw0aug, vw1aug, gates = map(
        seq, _inproj(x2d, norm_mix, _permute_w_in(w_in), t))
    f_a = _bias_1d(rel_bias[:, :DIL_HEADS])
    f_b = _bias_1d(rel_bias[:, DIL_HEADS:]) * LOG2E
    o_a = _mixer_a(qa, ka, va, jnp.stack([_dil_bias(f_a, dil) for _, dil in DIL_PATTERNS]))
    kcmp, vcmp = _compress(kc, vc, cmp_pos_k, cmp_pos_v, cmp_k_w1, cmp_k_w2, cmp_v_w1, cmp_v_w2)
    o_cmp, sel = _compressed_branch(qb, kcmp, vcmp, gates, f_b)
    o_sel = _selected_branch(qb, sel, gates, ks_t, vs0aug, vs1aug, f_b)
    o_win = _window_branch(qb, gates, kw, vw0aug, vw1aug, f_b)
    b_parts = [o.reshape(n, DB) for o in (o_cmp, o_sel, o_win)]
    w_router, b_router = _router_weights(w_rg, b_rg, w_re, b_re)
    h2, hn, comb = _outproj(x2d, o_a.reshape(n, DA), b_parts, _permute_w_out(w_out), norm_ffn, w_router, b_router)
    return _moe(h2, hn, comb, w_gate.astype(BF16), w_up.astype(BF16), w_down.astype(BF16), out_gain)


def kernel(x, rel_bias, norm_mix, w_in, w_out, cmp_pos_k, cmp_pos_v, cmp_k_w1, cmp_k_w2, cmp_v_w1, cmp_v_w2,
           norm_ffn, w_router_group, b_router_group, w_router_expert, b_router_expert, w_gate, w_up, w_down,
           norm_final):
    depth = norm_mix.shape[0]
    assert depth == 1, "the final RMSNorm is fused into the last layer's expert kernel"
    out = _layer(x, rel_bias, norm_mix[0], w_in[0], w_out[0], cmp_pos_k[0], cmp_pos_v[0], cmp_k_w1[0], cmp_k_w2[0],
                 cmp_v_w1[0], cmp_v_w2[0], norm_ffn[0], w_router_group[0], b_router_group[0], w_router_expert[0],
                 b_router_expert[0], w_gate[0], w_up[0], w_down[0], norm_final)
    return out.reshape(x.shape)
```

```python
import functools
import math

import jax
import jax.numpy as jnp
import numpy as np
from jax import lax
from jax.experimental import pallas as pl
from jax.experimental.pallas import tpu as pltpu

HEAD_DIM = 64
DIL_HEADS = 6
NSA_KV_HEADS = 2
NSA_GROUP = 5
NSA_HEADS = NSA_KV_HEADS * NSA_GROUP
N_HEADS = DIL_HEADS + NSA_HEADS
DIL_PATTERNS = ((128, 1), (512, 4), (2048, 16))
CMP_BLOCK = 32
CMP_STRIDE = 16
CMP_HIDDEN = 256
SEL_BLOCK = 64
SEL_TOPK = 16
WIN = 512
N_FORCED = 3
N_BUCKETS = 32
MAX_DISTANCE = 2048
N_GROUPS = 4
EXPERTS_PER_GROUP = 4
N_EXPERTS = N_GROUPS * EXPERTS_PER_GROUP
D_EXPERT = 512
EPS = 1e-6

LANES = 128
QB = 128
NEG = -1.0e30
LOG2E = math.log2(math.e)
DA = DIL_HEADS * HEAD_DIM
DB = NSA_HEADS * HEAD_DIM
N_ROWGROUPS = NSA_HEADS
SEL_KT = 256
VMEM_LIMIT = 56 * 1024 * 1024

F32 = jnp.float32
BF16 = jnp.bfloat16
NT_DIMS = (((1,), (1,)), ((), ()))


def _nt_dot(a, b):
    return lax.dot_general(a, b, NT_DIMS, preferred_element_type=F32)


def _dot(a, b):
    return jnp.dot(a, b, preferred_element_type=F32)


def _bucket_np(dist):
    dist = np.maximum(np.asarray(dist, np.int64), 0)
    max_exact = N_BUCKETS // 2
    x = np.maximum(dist, 1).astype(np.float32) / np.float32(max_exact)
    large = max_exact + (np.log(x) / np.float32(math.log(MAX_DISTANCE / max_exact))
                         * np.float32(N_BUCKETS - max_exact)).astype(np.int32)
    large = np.minimum(large, N_BUCKETS - 1)
    return np.where(dist < max_exact, dist, large).astype(np.int32)


BIAS_LEN = 4096


def _bias_1d(rel_bias_heads):
    onehot = (_bucket_np(np.arange(BIAS_LEN))[None, :] == np.arange(N_BUCKETS)[:, None]).astype(np.float32)
    return jnp.dot(rel_bias_heads.T.astype(F32), jnp.asarray(onehot), precision=lax.Precision.HIGHEST)


def _extend(f, lo, hi):
    assert hi <= f.shape[-1]
    if lo >= 0:
        return f[..., lo:hi]
    pad = jnp.full(f.shape[:-1] + (-lo,), NEG, f.dtype)
    return jnp.concatenate([pad, f[..., :hi]], axis=-1)


def _toeplitz(w, q, c):
    n = q + c - 1
    assert w.shape[-1] == n
    lead = w.shape[:-1]
    wp = jnp.concatenate([w, jnp.zeros(lead + (1,), w.dtype)], axis=-1)
    flat = jnp.broadcast_to(wp[..., None, :], lead + (q, n + 1)).reshape(lead + (q * (n + 1),))
    return flat[..., :q * n].reshape(lead + (q, n))[..., q - 1:q - 1 + c]


def _toeplitz_of(fn_vals, lo, q, c):
    return _toeplitz(jnp.flip(fn_vals, axis=-1), q, c)


IN_TM = 512
C_QA, C_KA, C_VA = 0, DA, 2 * DA
C_QB = 3 * DA
C_KC = C_QB + DB
C_VC, C_KS, C_VS, C_KW, C_VW, C_GT = (C_KC + LANES * i for i in range(1, 7))
N_COLS = C_GT + LANES


def _permute_w_in(w_in):
    scale = 1.0 / math.sqrt(HEAD_DIM)
    sizes = [DA] * 3 + [DB] + [NSA_KV_HEADS * HEAD_DIM] * 6 + [3 * NSA_HEADS]
    offs = np.concatenate([[0], np.cumsum(sizes)])
    part = lambda i: w_in[:, offs[i]:offs[i + 1]]
    d = w_in.shape[0]
    qb = part(3).reshape(d, NSA_KV_HEADS, NSA_GROUP, HEAD_DIM).transpose(0, 2, 1, 3).reshape(d, DB)
    gt = part(10).reshape(d, NSA_KV_HEADS, NSA_GROUP, 3).transpose(0, 3, 2, 1).reshape(d, 3 * NSA_HEADS)
    gt = jnp.pad(gt, ((0, 0), (0, LANES - 3 * NSA_HEADS)))
    cols = [part(0) * scale, part(1), part(2), qb * (scale * LOG2E)] + [part(i) for i in range(4, 10)] + [gt]
    return jnp.concatenate(cols, axis=1).astype(BF16)


def _inproj_kernel(seq_len, x_ref, g_ref, w_ref, qa_ref, ka_ref, va_ref, qb_ref, kc_ref, vc_ref,
                   kst_ref, vs0_ref, vs1_ref, kw_ref, vw0_ref, vw1_ref, gates_ref, stage_ref):
    x = x_ref[...]
    xn = (x * lax.rsqrt(jnp.mean(x * x, axis=-1, keepdims=True) + EPS) * g_ref[...]).astype(BF16)
    seg = lambda a, n: _dot(xn, w_ref[:, a:a + n])
    qa_ref[...] = seg(C_QA, DA).astype(BF16)
    ka_ref[...] = seg(C_KA, DA).astype(BF16)
    va_ref[...] = seg(C_VA, DA).astype(BF16)
    qb_ref[...] = seg(C_QB, DB).astype(BF16)
    kw_ref[...] = seg(C_KW, LANES).astype(BF16)
    tm = x.shape[0]
    for col, out_ref in ((C_KC, kc_ref), (C_VC, vc_ref)):
        stage_ref[...] = seg(col, LANES)
        for j in range(CMP_STRIDE):
            out_ref[:, j * LANES:(j + 1) * LANES] = stage_ref[pl.ds(j, tm // CMP_STRIDE, stride=CMP_STRIDE), :].astype(BF16)
    tok_t = (pl.program_id(0) * tm) % seq_len + lax.broadcasted_iota(jnp.int32, (LANES, tm), 1)
    blk_t = lax.broadcasted_iota(jnp.int32, (LANES, tm), 0)
    stage_ref[...] = seg(C_KS, LANES)
    kst_ref[0, 0:LANES, :] = stage_ref[...].T.astype(BF16)
    kst_ref[0, LANES:2 * LANES, :] = jnp.where(blk_t == tok_t // SEL_BLOCK, 1.0, 0.0).astype(BF16)
    low = lax.broadcasted_iota(jnp.int32, (tm, LANES), 1) < HEAD_DIM
    for col, ref0, ref1 in ((C_VS, vs0_ref, vs1_ref), (C_VW, vw0_ref, vw1_ref)):
        v = seg(col, LANES)
        ref0[...] = jnp.where(low, v, 1.0).astype(BF16)
        ref1[...] = jnp.where(low, 1.0, v).astype(BF16)
    gates_ref[...] = jax.nn.sigmoid(seg(C_GT, LANES))


def _inproj(x2d, gain, w_perm, seq_len):
    n, d = x2d.shape
    row = lambda w: pl.BlockSpec((IN_TM, w), lambda i: (i, 0))
    rows = lambda w: (jax.ShapeDtypeStruct((n, w), BF16), row(w))
    chunks = (jax.ShapeDtypeStruct((n // CMP_STRIDE, CMP_STRIDE * LANES), BF16),
              pl.BlockSpec((IN_TM // CMP_STRIDE, CMP_STRIDE * LANES), lambda i: (i, 0)))
    per_seq = seq_len // IN_TM
    keys_t = (jax.ShapeDtypeStruct((n // seq_len, 2 * LANES, seq_len), BF16),
              pl.BlockSpec((1, 2 * LANES, IN_TM), lambda i: (i // per_seq, 0, i % per_seq)))
    outs = [rows(DA), rows(DA), rows(DA), rows(DB), chunks, chunks, keys_t] + [rows(LANES)] * 5
    outs.append((jax.ShapeDtypeStruct((n, LANES), F32), row(LANES)))
    return pl.pallas_call(
        functools.partial(_inproj_kernel, seq_len),
        grid=(n // IN_TM,),
        in_specs=[row(d), pl.BlockSpec((1, d), lambda i: (0, 0)), pl.BlockSpec((d, N_COLS), lambda i: (0, 0))],
        out_specs=[spec for _, spec in outs],
        out_shape=[shape for shape, _ in outs],
        scratch_shapes=[pltpu.VMEM((IN_TM, LANES), F32)],
        compiler_params=pltpu.CompilerParams(dimension_semantics=("parallel",), vmem_limit_bytes=VMEM_LIMIT),
        name="inproj",
    )(x2d, gain.reshape(1, d), w_perm)


def _embed_pair(w, n_tok):
    c = w.shape[1]
    w4 = w.reshape(n_tok, 1, HEAD_DIM, 1, c) * jnp.eye(NSA_KV_HEADS, dtype=w.dtype).reshape(1, 2, 1, 2, 1)
    return w4.reshape(n_tok * 2 * HEAD_DIM, 2 * c)


def _gelu_tanh(x):
    return 0.5 * x * (1.0 + jnp.tanh(math.sqrt(2.0 / math.pi) * (x + 0.044715 * (x * x * x))))


def _compress_kernel(ck_ref, cv_ref, posk_ref, posv_ref, wk1a, wk1b, wk2, wv1a, wv1b, wv2,
                     kout_ref, vout_ref, shift_ref):
    ncp = ck_ref.shape[1]
    for c_ref, pos_ref, w1a, w1b, w2, out_ref in ((ck_ref, posk_ref, wk1a, wk1b, wk2, kout_ref),
                                                  (cv_ref, posv_ref, wv1a, wv1b, wv2, vout_ref)):
        c = c_ref[0].astype(F32)
        first = _dot((c + pos_ref[0:1, :]).astype(BF16), w1a[...])
        second = _dot((c + pos_ref[1:2, :]).astype(BF16), w1b[...])
        shift_ref[0:ncp, :] = second
        shift_ref[ncp:ncp + 8, :] = jnp.zeros((8, second.shape[1]), F32)
        hidden = _gelu_tanh(first + shift_ref[1:ncp + 1, :])
        out_ref[0] = _dot(hidden.astype(BF16), w2[...]).astype(BF16)


def _compress(kc, vc, pos_k, pos_v, k_w1, k_w2, v_w1, v_w2):
    b, ncp, wide = kc.shape
    half = CMP_STRIDE * HEAD_DIM

    def prep(w1, w2, pos):
        pos_pair = jnp.broadcast_to(pos.reshape(2, CMP_STRIDE, 1, HEAD_DIM), (2, CMP_STRIDE, 2, HEAD_DIM))
        return (_embed_pair(w1[:half], CMP_STRIDE).astype(BF16), _embed_pair(w1[half:], CMP_STRIDE).astype(BF16),
                jnp.kron(jnp.eye(NSA_KV_HEADS, dtype=w2.dtype), w2).astype(BF16),
                pos_pair.reshape(2, wide).astype(F32))

    wk1a, wk1b, wk2, posk = prep(k_w1, k_w2, pos_k)
    wv1a, wv1b, wv2, posv = prep(v_w1, v_w2, pos_v)
    full = lambda a: pl.BlockSpec(a.shape, lambda i: (0,) * a.ndim)
    tok = pl.BlockSpec((1, ncp, wide), lambda i: (i, 0, 0))
    out = pl.BlockSpec((1, ncp, LANES), lambda i: (i, 0, 0))
    return pl.pallas_call(
        _compress_kernel,
        grid=(b,),
        in_specs=[tok, tok, full(posk), full(posv), full(wk1a), full(wk1b), full(wk2), full(wv1a), full(wv1b), full(wv2)],
        out_specs=[out, out],
        out_shape=[jax.ShapeDtypeStruct((b, ncp, LANES), BF16)] * 2,
        scratch_shapes=[pltpu.VMEM((ncp + 8, 2 * CMP_HIDDEN), F32)],
        compiler_params=pltpu.CompilerParams(dimension_semantics=("parallel",), vmem_limit_bytes=VMEM_LIMIT),
        name="compress",
    )(kc, vc, posk, posv, wk1a, wk1b, wk2, wv1a, wv1b, wv2)


def _pair_masks(rows):
    lane = lax.broadcasted_iota(jnp.int32, (rows, LANES), 1)
    return lane < HEAD_DIM


A_SUPER = QB * DIL_PATTERNS[-1][1]


def _mixer_a_kernel(q_ref, kp_ref, kc_ref, vp_ref, vc_ref, bias_ref, out_ref, qf_ref, kf_ref, vf_ref, o_ref, lse_ref):
    first = pl.program_id(1) == 0
    n_pairs = DIL_HEADS // 2
    for p in range(n_pairs):
        cs = slice(p * LANES, (p + 1) * LANES)
        qf_ref[p] = q_ref[0, :, cs].astype(F32)
        kf_ref[p, 0:A_SUPER, :] = kp_ref[0, :, cs].astype(F32)
        kf_ref[p, A_SUPER:2 * A_SUPER, :] = kc_ref[0, :, cs].astype(F32)
        vf_ref[p, 0:A_SUPER, :] = vp_ref[0, :, cs].astype(F32)
        vf_ref[p, A_SUPER:2 * A_SUPER, :] = vc_ref[0, :, cs].astype(F32)
    low = _pair_masks(QB)
    in_prev = lax.broadcasted_iota(jnp.int32, (QB, 2 * QB), 1) < QB
    zero = jnp.zeros((QB, LANES), BF16)

    def chunk(idx, dil, q_base, k_base, q_span, k_span, off, at_start):
        rows = lambda size: pl.ds(off, size) if dil == 1 else pl.ds(off, size, stride=dil)
        q_win = lambda ref: ref.at[pl.ds(pl.multiple_of(q_base, 8), q_span), :]
        k_win = lambda ref: ref.at[pl.ds(pl.multiple_of(k_base, 8), k_span), :]
        q_rows = rows(QB)
        prev_mask = jnp.where(jnp.logical_and(in_prev, jnp.logical_and(first, at_start)), NEG, 0.0)
        for p in range(n_pairs):
            q = q_win(qf_ref.at[p])[q_rows, :].astype(BF16)
            keys = k_win(kf_ref.at[p])[rows(2 * QB), :].astype(BF16)
            vals = k_win(vf_ref.at[p])[rows(2 * QB), :].astype(BF16)
            o_win, lse_win = q_win(o_ref.at[p]), q_win(lse_ref.at[p])
            lhs = jnp.concatenate([jnp.where(low, q, zero), jnp.where(low, zero, q)], axis=0)
            bias = jnp.concatenate([bias_ref[idx, 2 * p] + prev_mask, bias_ref[idx, 2 * p + 1] + prev_mask], axis=0)
            s = _nt_dot(lhs, keys) + bias
            m = jnp.max(s, axis=1, keepdims=True)
            e = jnp.exp(s - m)
            l = jnp.sum(e, axis=1, keepdims=True)
            pv = _dot(e.astype(BF16), vals) * (1.0 / l)
            lse = m + jnp.log(l)
            o_new = jnp.where(low, pv[:QB], pv[QB:])
            l_new = jnp.where(low, jnp.broadcast_to(lse[:QB], (QB, LANES)), jnp.broadcast_to(lse[QB:], (QB, LANES)))
            if idx > 0:
                o_old, l_old = o_win[q_rows, :], lse_win[q_rows, :]
                mx = jnp.maximum(l_old, l_new)
                w_old, w_new = jnp.exp(l_old - mx), jnp.exp(l_new - mx)
                tot = w_old + w_new
                o_new = (w_old * o_old + w_new * o_new) * (1.0 / tot)
                l_new = mx + jnp.log(tot)
            o_win[q_rows, :] = o_new
            lse_win[q_rows, :] = l_new

    def loop(n, body):
        lax.fori_loop(0, n, lambda i, carry: (body(i), carry)[1], 0, unroll=min(n, 4))

    for idx, (_, dil) in enumerate(DIL_PATTERNS):
        span = QB * dil
        n_chunks = A_SUPER // span
        if n_chunks > 1:
            for r in range(dil):
                loop(n_chunks, lambda c, idx=idx, dil=dil, span=span, r=r:
                     chunk(idx, dil, span * c, A_SUPER + span * (c - 1), span, 2 * span, r, c == 0))
        else:
            for r in range(8):
                loop(dil // 8, lambda hi, idx=idx, dil=dil, span=span, r=r:
                     chunk(idx, dil, 8 * hi, 8 * hi, span - 8, 2 * span - 8, r, True))
    for p in range(n_pairs):
        out_ref[0, :, p * LANES:(p + 1) * LANES] = o_ref[p].astype(BF16)


def _mixer_a(qa, ka, va, bias):
    b, t, _ = qa.shape
    cur = pl.BlockSpec((1, A_SUPER, DA), lambda bi, i: (bi, i, 0))
    prev = pl.BlockSpec((1, A_SUPER, DA), lambda bi, i: (bi, jnp.maximum(i - 1, 0), 0))
    return pl.pallas_call(
        _mixer_a_kernel,
        grid=(b, t // A_SUPER),
        in_specs=[cur, prev, cur, prev, cur, pl.BlockSpec(bias.shape, lambda bi, i: (0, 0, 0, 0))],
        out_specs=cur,
        out_shape=jax.ShapeDtypeStruct((b, t, DA), BF16),
        scratch_shapes=[pltpu.VMEM((DIL_HEADS // 2, rows, LANES), F32) for rows in (A_SUPER, 2 * A_SUPER, 2 * A_SUPER, A_SUPER, A_SUPER)],
        compiler_params=pltpu.CompilerParams(dimension_semantics=("parallel", "parallel"), vmem_limit_bytes=VMEM_LIMIT),
        name="mixer_a",
    )(qa, ka, ka, va, va, bias)


def _dil_bias(f_a, dil):
    steps = DIL_PATTERNS[0][0]
    g = f_a[:, 0:dil * steps + 1:dil]
    lo, hi = QB - (2 * QB - 1), QB + QB
    vals = jnp.concatenate([jnp.full((DIL_HEADS, -lo), NEG, F32), g, jnp.full((DIL_HEADS, hi - steps - 1), NEG, F32)], axis=1)
    return _toeplitz_of(vals, lo, QB, 2 * QB)


CMP_TILE_KEYS = LANES
CMP_TILE_SPAN = CMP_TILE_KEYS * CMP_STRIDE // QB
CMP_CONST_DELTA = 28


def _cmp_bias(f_b):
    per = QB // CMP_STRIDE
    n_rows = per * (CMP_CONST_DELTA + 1)
    m_lo, m_hi = -(CMP_TILE_KEYS - 1), n_rows
    f_b = f_b.astype(BF16)
    base = _extend(f_b, CMP_STRIDE * m_lo - (CMP_BLOCK - 1), CMP_STRIDE * m_hi - (CMP_BLOCK - 1))
    g = base.reshape(NSA_HEADS, m_hi - m_lo, CMP_STRIDE).transpose(0, 2, 1)
    t = _toeplitz_of(g, m_lo, n_rows, CMP_TILE_KEYS)
    t = t.reshape(NSA_HEADS, CMP_STRIDE, CMP_CONST_DELTA + 1, per, CMP_TILE_KEYS).transpose(2, 0, 3, 1, 4)
    t = t.reshape(CMP_CONST_DELTA + 1, NSA_HEADS, QB, CMP_TILE_KEYS)
    return jnp.concatenate([jnp.full((1,) + t.shape[1:], NEG, t.dtype), t], axis=0)


def _overlap_matrix_t(ncp, n_sel_pad):
    n = np.arange(ncp)[None, :] * CMP_STRIDE
    s = np.arange(n_sel_pad)[:, None] * SEL_BLOCK
    ov = np.clip(np.minimum(n + CMP_BLOCK, s + SEL_BLOCK) - np.maximum(n, s), 0, None) / CMP_BLOCK
    return jnp.asarray(ov, BF16)


def _gate_tile(gates_ref, branch, g, row0=0):
    c = branch * NSA_HEADS + g * 2
    low = _pair_masks(QB)
    return jnp.where(low, jnp.broadcast_to(gates_ref[0, row0:row0 + QB, c:c + 1], (QB, LANES)),
                     jnp.broadcast_to(gates_ref[0, row0:row0 + QB, c + 1:c + 2], (QB, LANES)))


def _masked_q(qb_ref, g, kv, row0=0):
    q = qb_ref[0, row0:row0 + QB, g * LANES:(g + 1) * LANES]
    low = _pair_masks(QB)
    keep = low if kv == 0 else jnp.logical_not(low)
    return jnp.where(keep, q, jnp.zeros_like(q))


def _cmp_kernel(n_tiles, qb_ref, kcmp_ref, vcmp_ref, gates_ref, ov_ref, *rest):
    tbl_refs, (oc_ref, sel_ref, q_ref, s_ref, p_ref, pv_ref) = rest[:n_tiles], rest[n_tiles:]
    qblk = pl.program_id(1)
    t0 = qblk * QB
    low = _pair_masks(QB)
    for kv in range(NSA_KV_HEADS):
        for g in range(NSA_GROUP):
            r = kv * NSA_GROUP + g
            q_ref[r * QB:(r + 1) * QB, :] = _masked_q(qb_ref, g, kv)

    def attend(n_vis):
        kc = n_vis * CMP_TILE_KEYS
        n_blk = n_vis * CMP_TILE_SPAN * QB // SEL_BLOCK
        s_ref[:, 0:kc] = _nt_dot(q_ref[...], kcmp_ref[0, 0:kc, :])
        blk = lax.broadcasted_iota(jnp.int32, (n_blk, QB), 0)
        cur = (t0 + lax.broadcasted_iota(jnp.int32, (n_blk, QB), 1)) // SEL_BLOCK
        blk_f = blk.astype(F32)
        forced = (blk == cur) | (blk == cur - 1) | (blk == 0)
        causal = blk <= cur
        scores = []
        for kv in range(NSA_KV_HEADS):
            psum = jnp.zeros((QB, kc), F32)
            for g in range(NSA_GROUP):
                r = kv * NSA_GROUP + g
                rows = slice(r * QB, (r + 1) * QB)
                s = s_ref[rows, 0:kc] + jnp.concatenate([tbl_refs[c][0, r].astype(F32) for c in range(n_vis)], axis=1)
                m = jnp.max(s, axis=1, keepdims=True)
                e = jnp.exp2(s - m)
                den = jnp.sum(e, axis=1, keepdims=True)
                p = e * jnp.where(m > 0.5 * NEG, 1.0 / den, 0.0)
                psum = psum + p
                p_ref[rows, 0:kc] = p.astype(BF16)
            hi = psum.astype(BF16)
            lo = (psum - hi.astype(F32)).astype(BF16)
            ov_t = ov_ref[0:n_blk, 0:kc]
            imp_t = _nt_dot(ov_t, hi) + _nt_dot(ov_t, lo)
            scores.append(jnp.where(forced, -jnp.inf, jnp.where(causal, imp_t, -1.0)))
        pv_ref[...] = _dot(p_ref[:, 0:kc], vcmp_ref[0, 0:kc, :])

        def pick(_, carry):
            new = []
            for val, sel in carry:
                mx = jnp.max(val, axis=0, keepdims=True)
                idx = jnp.min(jnp.where(val == mx, blk_f, float(n_blk)), axis=0, keepdims=True)
                hit = blk_f == idx
                new.append((jnp.where(hit, -jnp.inf, val), jnp.where(hit, 1.0, sel)))
            return tuple(new)

        taken = jnp.where(forced, 1.0, 0.0)
        picked = lax.fori_loop(0, SEL_TOPK - N_FORCED, pick, tuple((v, taken) for v in scores))
        eye = jnp.where(lax.broadcasted_iota(jnp.int32, (QB, QB), 0) == lax.broadcasted_iota(jnp.int32, (QB, QB), 1),
                        1.0, 0.0).astype(BF16)
        for kv in range(NSA_KV_HEADS):
            sel_t = jnp.where(causal, picked[kv][1], 0.0).astype(BF16)
            sel_ref[0, kv, :, 0:n_blk] = _nt_dot(eye, sel_t).astype(BF16)
            if n_blk < LANES:
                sel_ref[0, kv, :, n_blk:LANES] = jnp.zeros((QB, LANES - n_blk), BF16)

    n_vis = qblk // CMP_TILE_SPAN + 1
    for w in range(1, n_tiles + 1):
        pl.when(n_vis == w)(functools.partial(attend, w))

    for g in range(NSA_GROUP):
        o0, o1 = pv_ref[g * QB:(g + 1) * QB, :], pv_ref[(NSA_GROUP + g) * QB:(NSA_GROUP + g + 1) * QB, :]
        oc_ref[0, :, g * LANES:(g + 1) * LANES] = (jnp.where(low, o0, o1) * _gate_tile(gates_ref, 0, g)).astype(BF16)


def _compressed_branch(qb, kcmp, vcmp, gates, f_b):
    b, t, _ = qb.shape
    ncp = kcmp.shape[1]
    n_tiles = ncp // CMP_TILE_KEYS
    tbl = _cmp_bias(f_b)
    ov = _overlap_matrix_t(ncp, LANES)

    def tbl_spec(c):
        return pl.BlockSpec((1, NSA_HEADS, QB, CMP_TILE_KEYS),
                            lambda bi, i: (jnp.clip(i - CMP_TILE_SPAN * c, -1, CMP_CONST_DELTA) + 1, 0, 0, 0))

    blockq = lambda w: pl.BlockSpec((1, QB, w), lambda bi, i: (bi, i, 0))
    batch = lambda a: pl.BlockSpec((1,) + a.shape[1:], lambda bi, i: (bi, 0, 0))
    return pl.pallas_call(
        functools.partial(_cmp_kernel, n_tiles),
        grid=(b, t // QB),
        in_specs=[blockq(DB), batch(kcmp), batch(vcmp), blockq(LANES), pl.BlockSpec(ov.shape, lambda bi, i: (0, 0))]
                 + [tbl_spec(c) for c in range(n_tiles)],
        out_specs=[blockq(DB), pl.BlockSpec((1, NSA_KV_HEADS, QB, LANES), lambda bi, i: (bi, 0, i, 0))],
        out_shape=[jax.ShapeDtypeStruct((b, t, DB), BF16), jax.ShapeDtypeStruct((b, NSA_KV_HEADS, t, LANES), BF16)],
        scratch_shapes=[pltpu.VMEM((N_ROWGROUPS * QB, LANES), BF16), pltpu.VMEM((N_ROWGROUPS * QB, ncp), F32),
                        pltpu.VMEM((N_ROWGROUPS * QB, ncp), BF16), pltpu.VMEM((N_ROWGROUPS * QB, LANES), F32)],
        compiler_params=pltpu.CompilerParams(dimension_semantics=("parallel", "parallel"), vmem_limit_bytes=VMEM_LIMIT),
        name="nsa_compressed",
    )(qb, kcmp, vcmp, gates, ov, *([tbl] * n_tiles))


SEL_NEAR = 13
SEL_FAR_BLOCK, SEL_NEAR_BLOCK = 16, 8


def _sel_bias(f_b):
    n_off = SEL_NEAR + 1
    cols = QB * n_off
    rel = f_b - f_b[:, BIAS_LEN - 1:]
    lo = -(QB - 1)
    big = _toeplitz_of(_extend(rel, lo, lo + QB + cols - 1), lo, QB, cols)
    tiles = jnp.flip(big.reshape(NSA_HEADS, QB, n_off, QB).transpose(2, 0, 1, 3), axis=0)
    return jnp.concatenate([jnp.zeros((1,) + tiles.shape[1:], F32), tiles], axis=0)


def _pair_ratio(acc0, acc1):
    low = _pair_masks(acc0.shape[0])
    den = pltpu.roll(jnp.where(low, acc1, acc0), HEAD_DIM, axis=1)
    return jnp.where(low, acc0, acc1) * (1.0 / den)


def _sel_kernel(qb_ref, sel_ref, gates_ref, cfar_ref, ks_ref, vs0_ref, vs1_ref, tbl_ref, out_ref,
                qaug_ref, s_ref, s1_ref, acc_ref, m_ref, alpha_ref):
    qblk = pl.program_id(1)
    for kv in range(NSA_KV_HEADS):
        unchosen = jnp.where(sel_ref[0, kv].astype(F32) > 0.0, 0.0, NEG)
        for g in range(NSA_GROUP):
            r = kv * NSA_GROUP + g
            rows = slice(r * QB, (r + 1) * QB)
            qaug_ref[rows, 0:LANES] = _masked_q(qb_ref, g, kv)
            qaug_ref[rows, LANES:2 * LANES] = (unchosen + cfar_ref[r:r + 1, :]).astype(BF16)
    acc_ref[...] = jnp.zeros_like(acc_ref)
    m_ref[...] = jnp.full_like(m_ref, NEG)

    last_tile = ks_ref.shape[2] // SEL_KT - 1

    def scores(j, dst_ref):
        start = pl.multiple_of(jnp.minimum(j, last_tile) * SEL_KT, SEL_KT)
        dst_ref[...] = _dot(qaug_ref[...], ks_ref[0, :, pl.ds(start, SEL_KT)])

    def consume(j, src_ref, near):
        start = pl.multiple_of(j * SEL_KT, SEL_KT)
        for r in range(N_ROWGROUPS):
            rows = slice(r * QB, (r + 1) * QB)
            s = src_ref[rows, :]
            if near:
                e1 = jnp.clip(qblk - 2 * j + 1, 0, SEL_NEAR + 1)
                e2 = jnp.clip(qblk - 2 * j, 0, SEL_NEAR + 1)
                s = s + jnp.concatenate([tbl_ref[e1, r], tbl_ref[e2, r]], axis=1)
                src_ref[rows, :] = s
            m_old = m_ref[rows, :]
            m_new = jnp.maximum(m_old, jnp.max(s, axis=1, keepdims=True))
            alpha_ref[rows, :] = jnp.exp2(m_old - m_new)
            m_ref[rows, :] = m_new
        for r in range(N_ROWGROUPS):
            rows = slice(r * QB, (r + 1) * QB)
            vals = (vs0_ref if r < NSA_GROUP else vs1_ref)[0, pl.ds(start, SEL_KT), :]
            m_new = m_ref[rows, :]
            p = jnp.exp2(src_ref[rows, :] - jnp.concatenate([m_new, m_new], axis=1))
            acc_ref[rows, :] = alpha_ref[rows, :] * acc_ref[rows, :] + _dot(p.astype(BF16), vals)

    def tile_run(first, count, near):
        bufs = (s_ref, s1_ref)
        for u in range(count):
            scores(first + u + 1, bufs[(u + 1) % 2])
            consume(first + u, bufs[u % 2], near)

    def run_pairs(first, pairs, near, max_block):
        tiles = 2 * pairs
        lax.fori_loop(0, tiles // max_block, lambda i, c: (tile_run(first + max_block * i, max_block, near), c)[1], 0)
        done = tiles // max_block * max_block
        size = max_block // 2
        while size >= 2:
            start = first + done
            pl.when((tiles - done) >= size)(functools.partial(tile_run, start, size, near))
            done = done + jnp.where((tiles - done) >= size, size, 0)
            size //= 2

    n_pairs = ((qblk + 2) // 2 + 1) // 2
    n_far = jnp.maximum((qblk - (SEL_NEAR - 1)) // 2, 0) // 2
    scores(0, s_ref)
    run_pairs(0, n_far, False, SEL_FAR_BLOCK)
    run_pairs(2 * n_far, n_pairs - n_far, True, SEL_NEAR_BLOCK)
    for g in range(NSA_GROUP):
        ratio = _pair_ratio(acc_ref[g * QB:(g + 1) * QB, :], acc_ref[(NSA_GROUP + g) * QB:(NSA_GROUP + g + 1) * QB, :])
        out_ref[0, :, g * LANES:(g + 1) * LANES] = (ratio * _gate_tile(gates_ref, 1, g)).astype(BF16)


def _selected_branch(qb, sel, gates, ks_t, vs0aug, vs1aug, f_b):
    b, t, _ = qb.shape
    tbl = _sel_bias(f_b)
    cfar = jnp.broadcast_to(f_b[:, BIAS_LEN - 1:], (NSA_HEADS, LANES))
    cfar = jnp.pad(cfar, ((0, 16 - NSA_HEADS), (0, 0)))
    blockq = lambda w: pl.BlockSpec((1, QB, w), lambda bi, i: (bi, i, 0))
    batch = lambda a: pl.BlockSpec((1,) + a.shape[1:], lambda bi, i: (bi, 0, 0))
    rows = N_ROWGROUPS * QB
    return pl.pallas_call(
        _sel_kernel,
        grid=(b, t // QB),
        in_specs=[blockq(DB), pl.BlockSpec((1, NSA_KV_HEADS, QB, LANES), lambda bi, i: (bi, 0, i, 0)), blockq(LANES),
                  pl.BlockSpec(cfar.shape, lambda bi, i: (0, 0)), batch(ks_t), batch(vs0aug), batch(vs1aug),
                  pl.BlockSpec(tbl.shape, lambda bi, i: (0, 0, 0, 0))],
        out_specs=blockq(DB),
        out_shape=jax.ShapeDtypeStruct((b, t, DB), BF16),
        scratch_shapes=[pltpu.VMEM((rows, 2 * LANES), BF16), pltpu.VMEM((rows, SEL_KT), F32), pltpu.VMEM((rows, SEL_KT), F32),
                        pltpu.VMEM((rows, LANES), F32), pltpu.VMEM((rows, LANES), F32), pltpu.VMEM((rows, LANES), F32)],
        compiler_params=pltpu.CompilerParams(dimension_semantics=("parallel", "parallel"), vmem_limit_bytes=VMEM_LIMIT),
        name="nsa_selected",
    )(qb, sel, gates, cfar, ks_t, vs0aug, vs1aug, tbl)


WIN_KEYS = WIN + QB


def _win_bias(f_b):
    lo = WIN - (WIN_KEYS - 1)
    vals = _extend(f_b[:, :WIN], lo, WIN)
    vals = jnp.concatenate([vals, jnp.full((NSA_HEADS, lo + QB + WIN_KEYS - 1 - WIN), NEG, F32)], axis=1)
    return _toeplitz_of(vals, lo, QB, WIN_KEYS)


WIN_QBLOCKS = 2


def _win_kernel(qb_ref, gates_ref, kw_ref, vw0_ref, vw1_ref, tbl_ref, out_ref, q_ref, s_ref):
    col = lax.broadcasted_iota(jnp.int32, (1, WIN_KEYS), 1)
    for h in range(WIN_QBLOCKS):
        qblk = pl.program_id(1) * WIN_QBLOCKS + h
        row0 = h * QB
        for kv in range(NSA_KV_HEADS):
            for g in range(NSA_GROUP):
                r = kv * NSA_GROUP + g
                q_ref[h, r * QB:(r + 1) * QB, :] = _masked_q(qb_ref, g, kv, row0)
        start = pl.multiple_of(qblk * QB, QB)
        s_ref[h] = _nt_dot(q_ref[h], kw_ref[0, pl.ds(start, WIN_KEYS), :])
        pad_mask = jnp.where(col + qblk * QB >= WIN, 0.0, NEG)
        outs = []
        for r in range(N_ROWGROUPS):
            vals = (vw0_ref if r < NSA_GROUP else vw1_ref)[0, pl.ds(start, WIN_KEYS), :]
            s = s_ref[h, r * QB:(r + 1) * QB, :] + tbl_ref[r] + pad_mask
            e = jnp.exp2(s - jnp.max(s, axis=1, keepdims=True))
            outs.append(_dot(e.astype(BF16), vals))
        for g in range(NSA_GROUP):
            out_ref[0, row0:row0 + QB, g * LANES:(g + 1) * LANES] = (
                _pair_ratio(outs[g], outs[NSA_GROUP + g]) * _gate_tile(gates_ref, 2, g, row0)).astype(BF16)


def _window_branch(qb, gates, kw, vw0aug, vw1aug, f_b):
    b, t, _ = qb.shape
    tbl = _win_bias(f_b)
    pad_front = lambda a: jnp.pad(a, ((0, 0), (WIN, 0), (0, 0)))
    kw_pad, vw0_pad, vw1_pad = pad_front(kw), pad_front(vw0aug), pad_front(vw1aug)
    blockq = lambda w: pl.BlockSpec((1, WIN_QBLOCKS * QB, w), lambda bi, i: (bi, i, 0))
    batch = lambda a: pl.BlockSpec((1,) + a.shape[1:], lambda bi, i: (bi, 0, 0))
    rows = N_ROWGROUPS * QB
    return pl.pallas_call(
        _win_kernel,
        grid=(b, t // (WIN_QBLOCKS * QB)),
        in_specs=[blockq(DB), blockq(LANES), batch(kw_pad), batch(vw0_pad), batch(vw1_pad),
                  pl.BlockSpec(tbl.shape, lambda bi, i: (0, 0, 0))],
        out_specs=blockq(DB),
        out_shape=jax.ShapeDtypeStruct((b, t, DB), BF16),
        scratch_shapes=[pltpu.VMEM((WIN_QBLOCKS, rows, LANES), BF16), pltpu.VMEM((WIN_QBLOCKS, rows, WIN_KEYS), F32)],
        compiler_params=pltpu.CompilerParams(dimension_semantics=("parallel", "parallel"), vmem_limit_bytes=VMEM_LIMIT),
        name="nsa_window",
    )(qb, gates, kw_pad, vw0_pad, vw1_pad, tbl)


OUT_TM = 256
C_GROUP = N_EXPERTS


def _outproj_kernel(x_ref, oa_ref, oc_ref, os_ref, ow_ref,
                    wout_ref, g_ref, wr_ref, br_ref, h_ref, hn_ref, comb_ref):
    ob = oc_ref[...].astype(F32) + os_ref[...].astype(F32) + ow_ref[...].astype(F32)
    y = _dot(oa_ref[...], wout_ref[0:DA, :]) + _dot(ob.astype(BF16), wout_ref[DA:DA + DB, :])
    h = x_ref[...] + y
    h_ref[...] = h
    hn = h * lax.rsqrt(jnp.mean(h * h, axis=-1, keepdims=True) + EPS) * g_ref[...]
    hn_hi = hn.astype(BF16)
    hn_ref[...] = hn_hi
    hn_lo = (hn - hn_hi.astype(F32)).astype(BF16)
    both = _dot(hn_hi, wr_ref[...])
    logits = both[:, 0:LANES] + both[:, LANES:2 * LANES] + _dot(hn_lo, wr_ref[:, 0:LANES]) + br_ref[...]
    lane = lax.broadcasted_iota(jnp.int32, logits.shape, 1)
    lane_f = lane.astype(F32)
    big = float(LANES)
    gl = jnp.where((lane >= C_GROUP) & (lane < C_GROUP + N_GROUPS), logits, -jnp.inf)
    gmax = jnp.max(gl, axis=1, keepdims=True)
    gidx = jnp.min(jnp.where(gl == gmax, lane_f, big), axis=1, keepdims=True) - C_GROUP
    gprob = 1.0 / jnp.sum(jnp.exp(gl - gmax), axis=1, keepdims=True)
    grp_of_lane = (lane // EXPERTS_PER_GROUP).astype(F32)
    el = jnp.where((lane < N_EXPERTS) & (grp_of_lane == gidx), logits, -jnp.inf)
    v1 = jnp.max(el, axis=1, keepdims=True)
    i1 = jnp.min(jnp.where(el == v1, lane_f, big), axis=1, keepdims=True)
    el2 = jnp.where(lane_f == i1, -jnp.inf, el)
    v2 = jnp.max(el2, axis=1, keepdims=True)
    i2 = jnp.min(jnp.where(el2 == v2, lane_f, big), axis=1, keepdims=True)
    e2 = jnp.exp(v2 - v1)
    p1 = 1.0 / (1.0 + e2)
    comb_ref[...] = (gprob * (jnp.where(lane_f == i1, p1, 0.0) + jnp.where(lane_f == i2, e2 * p1, 0.0))
                     + jnp.where(lane == C_GROUP, gidx, 0.0))


def _outproj(x2d, o_a, b_parts, w_out_perm, gain, w_router, b_router):
    n, d = x2d.shape
    row = lambda w: pl.BlockSpec((OUT_TM, w), lambda i: (i, 0))
    full = lambda a: pl.BlockSpec(a.shape, lambda i: (0, 0))
    return pl.pallas_call(
        _outproj_kernel,
        grid=(n // OUT_TM,),
        in_specs=[row(d), row(DA)] + [row(DB)] * 3 + [full(w_out_perm), pl.BlockSpec((1, d), lambda i: (0, 0)),
                                                      full(w_router), full(b_router)],
        out_specs=[row(d), row(d), row(LANES)],
        out_shape=[jax.ShapeDtypeStruct((n, d), F32), jax.ShapeDtypeStruct((n, d), BF16),
                   jax.ShapeDtypeStruct((n, LANES), F32)],
        compiler_params=pltpu.CompilerParams(dimension_semantics=("parallel",), vmem_limit_bytes=VMEM_LIMIT),
        name="outproj_router",
    )(x2d, o_a, *b_parts, w_out_perm, gain.reshape(1, d), w_router, b_router)


MOE_TM = 1024


MOE_SUB = 128
MOE_FINAL_TM = 512


def _moe_kernel(hn_ref, comb_ref, upper_ref, wg_ref, wu_ref, wd_ref, y_ref,
                perm_ref, hs_ref, cs_ref, ys_ref, start_ref, nsub_ref):
    grp = pl.program_id(1)
    tm, d = hn_ref.shape
    n_pad = perm_ref.shape[0]

    @pl.when(grp == 0)
    def _sort():
        comb = comb_ref[...]
        lane_f = lax.broadcasted_iota(jnp.int32, (tm, LANES), 1).astype(F32)
        onehot = jnp.where(lane_f == comb[:, C_GROUP:C_GROUP + 1], 1.0, 0.0).astype(BF16)
        eye8 = jnp.where(lax.broadcasted_iota(jnp.int32, (8, LANES), 0) == lax.broadcasted_iota(jnp.int32, (8, LANES), 1),
                         1.0, 0.0).astype(BF16)
        onehot_t = _nt_dot(eye8, onehot)
        cum_t = _dot(onehot_t.astype(BF16), upper_ref[...])
        start = jnp.zeros((1, 1), F32)
        pos_t = jnp.zeros((1, tm), F32)
        for k in range(N_GROUPS):
            padded = jnp.ceil(cum_t[k:k + 1, tm - 1:tm] * (1.0 / MOE_SUB)) * MOE_SUB
            pos_t = pos_t + onehot_t[k:k + 1, :] * (start + cum_t[k:k + 1, :] - 1.0)
            start_ref[k] = start[0, 0].astype(jnp.int32)
            nsub_ref[k] = (padded[0, 0] * (1.0 / MOE_SUB)).astype(jnp.int32)
            start = start + padded
        perm = jnp.where(lax.broadcasted_iota(jnp.int32, (n_pad, tm), 0) == pos_t.astype(jnp.int32), 1.0, 0.0).astype(BF16)
        perm_ref[...] = perm
        hs_ref[...] = _dot(perm, hn_ref[...]).astype(BF16)
        c_hi = comb.astype(BF16)
        c_lo = (comb - c_hi.astype(F32)).astype(BF16)
        both = _dot(perm, jnp.concatenate([c_hi, c_lo], axis=1))
        cs_ref[...] = both[:, 0:LANES] + both[:, LANES:2 * LANES]
        ys_ref[...] = jnp.zeros_like(ys_ref)

    lane = lax.broadcasted_iota(jnp.int32, (MOE_SUB, LANES), 1)

    def segment(s, carry):
        rows = pl.ds(pl.multiple_of(start_ref[grp] + s * MOE_SUB, MOE_SUB), MOE_SUB)
        x = hs_ref[rows, :]
        weights = cs_ref[rows, :]
        acc = jnp.zeros((MOE_SUB, d), F32)
        for j in range(EXPERTS_PER_GROUP):
            gate = _dot(x, wg_ref[j])
            up = _dot(x, wu_ref[j])
            w = jnp.sum(jnp.where(lane == grp * EXPERTS_PER_GROUP + j, weights, 0.0), axis=1, keepdims=True)
            acc = acc + _dot((gate * jax.nn.sigmoid(gate) * up * w).astype(BF16), wd_ref[j])
        ys_ref[rows, :] = acc.astype(BF16)
        return carry

    lax.fori_loop(0, nsub_ref[grp], segment, 0)

    @pl.when(grp == pl.num_programs(1) - 1)
    def _unsort():
        y_ref[...] = lax.dot_general(perm_ref[...], ys_ref[...], (((0,), (0,)), ((), ())),
                                     preferred_element_type=F32).astype(BF16)


def _final_kernel(h_ref, y_ref, g_ref, out_ref):
    y = h_ref[...] + y_ref[...].astype(F32)
    out_ref[...] = y * lax.rsqrt(jnp.mean(y * y, axis=-1, keepdims=True) + EPS) * g_ref[...]


def _moe(h, hn, comb, w_gate, w_up, w_down, gain):
    n, d = h.shape
    tm = min(MOE_TM, n)
    n_pad = tm + (N_GROUPS - 1) * MOE_SUB
    row = lambda w: pl.BlockSpec((tm, w), lambda i, g: (i, 0))
    group_w = lambda a: pl.BlockSpec((EXPERTS_PER_GROUP,) + a.shape[1:], lambda i, g: (g, 0, 0))
    upper = jnp.asarray(np.triu(np.ones((tm, tm), np.float32)), BF16)
    y = pl.pallas_call(
        _moe_kernel,
        grid=(n // tm, N_GROUPS),
        in_specs=[row(d), row(LANES), pl.BlockSpec((tm, tm), lambda i, g: (0, 0)),
                  group_w(w_gate), group_w(w_up), group_w(w_down)],
        out_specs=row(d),
        out_shape=jax.ShapeDtypeStruct((n, d), BF16),
        scratch_shapes=[pltpu.VMEM((n_pad, tm), BF16), pltpu.VMEM((n_pad, d), BF16), pltpu.VMEM((n_pad, LANES), F32),
                        pltpu.VMEM((n_pad, d), BF16), pltpu.SMEM((N_GROUPS,), jnp.int32), pltpu.SMEM((N_GROUPS,), jnp.int32)],
        compiler_params=pltpu.CompilerParams(dimension_semantics=("parallel", "arbitrary"), vmem_limit_bytes=VMEM_LIMIT),
        name="moe_experts",
    )(hn, comb, upper, w_gate, w_up, w_down)
    rowf = lambda w: pl.BlockSpec((MOE_FINAL_TM, w), lambda i: (i, 0))
    return pl.pallas_call(
        _final_kernel,
        grid=(n // MOE_FINAL_TM,),
        in_specs=[rowf(d), rowf(d), pl.BlockSpec((1, d), lambda i: (0, 0))],
        out_specs=rowf(d),
        out_shape=jax.ShapeDtypeStruct((n, d), F32),
        compiler_params=pltpu.CompilerParams(dimension_semantics=("parallel",), vmem_limit_bytes=VMEM_LIMIT),
        name="residual_final_norm",
    )(h, y, gain.reshape(1, d))


def _permute_w_out(w_out):
    d = w_out.shape[1]
    wb = w_out[DA:].reshape(NSA_KV_HEADS, NSA_GROUP, HEAD_DIM, d).transpose(1, 0, 2, 3).reshape(DB, d)
    return jnp.concatenate([w_out[:DA], wb], axis=0).astype(BF16)


def _router_weights(w_group, b_group, w_expert, b_expert):
    d = w_group.shape[0]
    w = jnp.concatenate([w_expert.reshape(d, N_EXPERTS), w_group], axis=1)
    b = jnp.concatenate([b_expert.reshape(N_EXPERTS), b_group])
    pad = LANES - w.shape[1]
    w = jnp.pad(w, ((0, 0), (0, pad))).astype(F32)
    w_hi = w.astype(BF16)
    w_lo = (w - w_hi.astype(F32)).astype(BF16)
    return jnp.concatenate([w_hi, w_lo], axis=1), jnp.pad(b, (0, pad)).reshape(1, LANES).astype(F32)


def _layer(h, rel_bias, norm_mix, w_in, w_out, cmp_pos_k, cmp_pos_v, cmp_k_w1, cmp_k_w2, cmp_v_w1, cmp_v_w2,
           norm_ffn, w_rg, b_rg, w_re, b_re, w_gate, w_up, w_down, out_gain):
    b, t, d = h.shape
    n = b * t
    assert t % (QB * DIL_PATTERNS[-1][1]) == 0 and t // SEL_BLOCK <= LANES and n % MOE_TM == 0
    x2d = h.reshape(n, d)
    seq = lambda a: a if a.ndim == 3 else a.reshape(b, a.shape[0] // b, a.shape[-1])
    qa, ka, va, qb, kc, vc, ks_t, vs0aug, vs1aug, kw, vw0aug, vw1aug, gates = map(
        seq, _inproj(x2d, norm_mix, _permute_w_in(w_in), t))
    f_a = _bias_1d(rel_bias[:, :DIL_HEADS])
    f_b = _bias_1d(rel_bias[:, DIL_HEADS:]) * LOG2E
    o_a = _mixer_a(qa, ka, va, jnp.stack([_dil_bias(f_a, dil) for _, dil in DIL_PATTERNS]))
    kcmp, vcmp = _compress(kc, vc, cmp_pos_k, cmp_pos_v, cmp_k_w1, cmp_k_w2, cmp_v_w1, cmp_v_w2)
    o_cmp, sel = _compressed_branch(qb, kcmp, vcmp, gates, f_b)
    o_sel = _selected_branch(qb, sel, gates, ks_t, vs0aug, vs1aug, f_b)
    o_win = _window_branch(qb, gates, kw, vw0aug, vw1aug, f_b)
    b_parts = [o.reshape(n, DB) for o in (o_cmp, o_sel, o_win)]
    w_router, b_router = _router_weights(w_rg, b_rg, w_re, b_re)
    h2, hn, comb = _outproj(x2d, o_a.reshape(n, DA), b_parts, _permute_w_out(w_out), norm_ffn, w_router, b_router)
    return _moe(h2, hn, comb, w_gate.astype(BF16), w_up.astype(BF16), w_down.astype(BF16), out_gain)


def kernel(x, rel_bias, norm_mix, w_in, w_out, cmp_pos_k, cmp_pos_v, cmp_k_w1, cmp_k_w2, cmp_v_w1, cmp_v_w2,
           norm_ffn, w_router_group, b_router_group, w_router_expert, b_router_expert, w_gate, w_up, w_down,
           norm_final):
    depth = norm_mix.shape[0]
    assert depth == 1, "the final RMSNorm is fused into the last layer's expert kernel"
    out = _layer(x, rel_bias, norm_mix[0], w_in[0], w_out[0], cmp_pos_k[0], cmp_pos_v[0], cmp_k_w1[0], cmp_k_w2[0],
                 cmp_v_w1[0], cmp_v_w2[0], norm_ffn[0], w_router_group[0], b_router_group[0], w_router_expert[0],
                 b_router_expert[0], w_gate[0], w_up[0], w_down[0], norm_final)
    return out.reshape(x.shape)
```

```python
import functools
import math

import jax
import jax.numpy as jnp
import numpy as np
from jax import lax
from jax.experimental import pallas as pl
from jax.experimental.pallas import tpu as pltpu

HEAD_DIM = 64
DIL_HEADS = 6
NSA_KV_HEADS = 2
NSA_GROUP = 5
NSA_HEADS = NSA_KV_HEADS * NSA_GROUP
N_HEADS = DIL_HEADS + NSA_HEADS
DIL_PATTERNS = ((128, 1), (512, 4), (2048, 16))
CMP_BLOCK = 32
CMP_STRIDE = 16
CMP_HIDDEN = 256
SEL_BLOCK = 64
SEL_TOPK = 16
WIN = 512
N_FORCED = 3
N_BUCKETS = 32
MAX_DISTANCE = 2048
N_GROUPS = 4
EXPERTS_PER_GROUP = 4
N_EXPERTS = N_GROUPS * EXPERTS_PER_GROUP
D_EXPERT = 512
EPS = 1e-6

LANES = 128
QB = 128
NEG = -1.0e30
LOG2E = math.log2(math.e)
DA = DIL_HEADS * HEAD_DIM
DB = NSA_HEADS * HEAD_DIM
N_ROWGROUPS = NSA_HEADS
SEL_KT = 256
VMEM_LIMIT = 56 * 1024 * 1024

F32 = jnp.float32
BF16 = jnp.bfloat16
NT_DIMS = (((1,), (1,)), ((), ()))


def _nt_dot(a, b):
    return lax.dot_general(a, b, NT_DIMS, preferred_element_type=F32)


def _dot(a, b):
    return jnp.dot(a, b, preferred_element_type=F32)


def _bucket_np(dist):
    dist = np.maximum(np.asarray(dist, np.int64), 0)
    max_exact = N_BUCKETS // 2
    x = np.maximum(dist, 1).astype(np.float32) / np.float32(max_exact)
    large = max_exact + (np.log(x) / np.float32(math.log(MAX_DISTANCE / max_exact))
                         * np.float32(N_BUCKETS - max_exact)).astype(np.int32)
    large = np.minimum(large, N_BUCKETS - 1)
    return np.where(dist < max_exact, dist, large).astype(np.int32)


BIAS_LEN = 4096


def _bias_1d(rel_bias_heads):
    onehot = (_bucket_np(np.arange(BIAS_LEN))[None, :] == np.arange(N_BUCKETS)[:, None]).astype(np.float32)
    return jnp.dot(rel_bias_heads.T.astype(F32), jnp.asarray(onehot), precision=lax.Precision.HIGHEST)


def _extend(f, lo, hi):
    assert hi <= f.shape[-1]
    if lo >= 0:
        return f[..., lo:hi]
    pad = jnp.full(f.shape[:-1] + (-lo,), NEG, f.dtype)
    return jnp.concatenate([pad, f[..., :hi]], axis=-1)


def _toeplitz(w, q, c):
    n = q + c - 1
    assert w.shape[-1] == n
    lead = w.shape[:-1]
    wp = jnp.concatenate([w, jnp.zeros(lead + (1,), w.dtype)], axis=-1)
    flat = jnp.broadcast_to(wp[..., None, :], lead + (q, n + 1)).reshape(lead + (q * (n + 1),))
    return flat[..., :q * n].reshape(lead + (q, n))[..., q - 1:q - 1 + c]


def _toeplitz_of(fn_vals, lo, q, c):
    return _toeplitz(jnp.flip(fn_vals, axis=-1), q, c)


IN_TM = 512
C_QA, C_KA, C_VA = 0, DA, 2 * DA
C_QB = 3 * DA
C_KC = C_QB + DB
C_VC, C_KS, C_VS, C_KW, C_VW, C_GT = (C_KC + LANES * i for i in range(1, 7))
N_COLS = C_GT + LANES


def _permute_w_in(w_in):
    scale = 1.0 / math.sqrt(HEAD_DIM)
    sizes = [DA] * 3 + [DB] + [NSA_KV_HEADS * HEAD_DIM] * 6 + [3 * NSA_HEADS]
    offs = np.concatenate([[0], np.cumsum(sizes)])
    part = lambda i: w_in[:, offs[i]:offs[i + 1]]
    d = w_in.shape[0]
    qb = part(3).reshape(d, NSA_KV_HEADS, NSA_GROUP, HEAD_DIM).transpose(0, 2, 1, 3).reshape(d, DB)
    gt = part(10).reshape(d, NSA_KV_HEADS, NSA_GROUP, 3).transpose(0, 3, 2, 1).reshape(d, 3 * NSA_HEADS)
    gt = jnp.pad(gt, ((0, 0), (0, LANES - 3 * NSA_HEADS)))
    cols = [part(0) * scale, part(1), part(2), qb * (scale * LOG2E)] + [part(i) for i in range(4, 10)] + [gt]
    return jnp.concatenate(cols, axis=1).astype(BF16)


def _inproj_kernel(seq_len, x_ref, g_ref, w_ref, qa_ref, ka_ref, va_ref, qb_ref, kc_ref, vc_ref,
                   kst_ref, vs0_ref, vs1_ref, kw_ref, vw0_ref, vw1_ref, gates_ref, stage_ref):
    x = x_ref[...]
    xn = (x * lax.rsqrt(jnp.mean(x * x, axis=-1, keepdims=True) + EPS) * g_ref[...]).astype(BF16)
    seg = lambda a, n: _dot(xn, w_ref[:, a:a + n])
    qa_ref[...] = seg(C_QA, DA).astype(BF16)
    ka_ref[...] = seg(C_KA, DA).astype(BF16)
    va_ref[...] = seg(C_VA, DA).astype(BF16)
    qb_ref[...] = seg(C_QB, DB).astype(BF16)
    kw_ref[...] = seg(C_KW, LANES).astype(BF16)
    tm = x.shape[0]
    for col, out_ref in ((C_KC, kc_ref), (C_VC, vc_ref)):
        stage_ref[...] = seg(col, LANES)
        for j in range(CMP_STRIDE):
            out_ref[:, j * LANES:(j + 1) * LANES] = stage_ref[pl.ds(j, tm // CMP_STRIDE, stride=CMP_STRIDE), :].astype(BF16)
    tok_t = (pl.program_id(0) * tm) % seq_len + lax.broadcasted_iota(jnp.int32, (LANES, tm), 1)
    blk_t = lax.broadcasted_iota(jnp.int32, (LANES, tm), 0)
    stage_ref[...] = seg(C_KS, LANES)
    kst_ref[0, 0:LANES, :] = stage_ref[...].T.astype(BF16)
    kst_ref[0, LANES:2 * LANES, :] = jnp.where(blk_t == tok_t // SEL_BLOCK, 1.0, 0.0).astype(BF16)
    low = lax.broadcasted_iota(jnp.int32, (tm, LANES), 1) < HEAD_DIM
    for col, ref0, ref1 in ((C_VS, vs0_ref, vs1_ref), (C_VW, vw0_ref, vw1_ref)):
        v = seg(col, LANES)
        ref0[...] = jnp.where(low, v, 1.0).astype(BF16)
        ref1[...] = jnp.where(low, 1.0, v).astype(BF16)
    gates_ref[...] = jax.nn.sigmoid(seg(C_GT, LANES))


def _inproj(x2d, gain, w_perm, seq_len):
    n, d = x2d.shape
    row = lambda w: pl.BlockSpec((IN_TM, w), lambda i: (i, 0))
    rows = lambda w: (jax.ShapeDtypeStruct((n, w), BF16), row(w))
    chunks = (jax.ShapeDtypeStruct((n // CMP_STRIDE, CMP_STRIDE * LANES), BF16),
              pl.BlockSpec((IN_TM // CMP_STRIDE, CMP_STRIDE * LANES), lambda i: (i, 0)))
    per_seq = seq_len // IN_TM
    keys_t = (jax.ShapeDtypeStruct((n // seq_len, 2 * LANES, seq_len), BF16),
              pl.BlockSpec((1, 2 * LANES, IN_TM), lambda i: (i // per_seq, 0, i % per_seq)))
    outs = [rows(DA), rows(DA), rows(DA), rows(DB), chunks, chunks, keys_t] + [rows(LANES)] * 5
    outs.append((jax.ShapeDtypeStruct((n, LANES), F32), row(LANES)))
    return pl.pallas_call(
        functools.partial(_inproj_kernel, seq_len),
        grid=(n // IN_TM,),
        in_specs=[row(d), pl.BlockSpec((1, d), lambda i: (0, 0)), pl.BlockSpec((d, N_COLS), lambda i: (0, 0))],
        out_specs=[spec for _, spec in outs],
        out_shape=[shape for shape, _ in outs],
        scratch_shapes=[pltpu.VMEM((IN_TM, LANES), F32)],
        compiler_params=pltpu.CompilerParams(dimension_semantics=("parallel",), vmem_limit_bytes=VMEM_LIMIT),
        name="inproj",
    )(x2d, gain.reshape(1, d), w_perm)


def _embed_pair(w, n_tok):
    c = w.shape[1]
    w4 = w.reshape(n_tok, 1, HEAD_DIM, 1, c) * jnp.eye(NSA_KV_HEADS, dtype=w.dtype).reshape(1, 2, 1, 2, 1)
    return w4.reshape(n_tok * 2 * HEAD_DIM, 2 * c)


def _gelu_tanh(x):
    return 0.5 * x * (1.0 + jnp.tanh(math.sqrt(2.0 / math.pi) * (x + 0.044715 * (x * x * x))))


def _compress_kernel(ck_ref, cv_ref, posk_ref, posv_ref, wk1a, wk1b, wk2, wv1a, wv1b, wv2,
                     kout_ref, vout_ref, shift_ref):
    ncp = ck_ref.shape[1]
    for c_ref, pos_ref, w1a, w1b, w2, out_ref in ((ck_ref, posk_ref, wk1a, wk1b, wk2, kout_ref),
                                                  (cv_ref, posv_ref, wv1a, wv1b, wv2, vout_ref)):
        c = c_ref[0].astype(F32)
        first = _dot((c + pos_ref[0:1, :]).astype(BF16), w1a[...])
        second = _dot((c + pos_ref[1:2, :]).astype(BF16), w1b[...])
        shift_ref[0:ncp, :] = second
        shift_ref[ncp:ncp + 8, :] = jnp.zeros((8, second.shape[1]), F32)
        hidden = _gelu_tanh(first + shift_ref[1:ncp + 1, :])
        out_ref[0] = _dot(hidden.astype(BF16), w2[...]).astype(BF16)


def _compress(kc, vc, pos_k, pos_v, k_w1, k_w2, v_w1, v_w2):
    b, ncp, wide = kc.shape
    half = CMP_STRIDE * HEAD_DIM

    def prep(w1, w2, pos):
        pos_pair = jnp.broadcast_to(pos.reshape(2, CMP_STRIDE, 1, HEAD_DIM), (2, CMP_STRIDE, 2, HEAD_DIM))
        return (_embed_pair(w1[:half], CMP_STRIDE).astype(BF16), _embed_pair(w1[half:], CMP_STRIDE).astype(BF16),
                jnp.kron(jnp.eye(NSA_KV_HEADS, dtype=w2.dtype), w2).astype(BF16),
                pos_pair.reshape(2, wide).astype(F32))

    wk1a, wk1b, wk2, posk = prep(k_w1, k_w2, pos_k)
    wv1a, wv1b, wv2, posv = prep(v_w1, v_w2, pos_v)
    full = lambda a: pl.BlockSpec(a.shape, lambda i: (0,) * a.ndim)
    tok = pl.BlockSpec((1, ncp, wide), lambda i: (i, 0, 0))
    out = pl.BlockSpec((1, ncp, LANES), lambda i: (i, 0, 0))
    return pl.pallas_call(
        _compress_kernel,
        grid=(b,),
        in_specs=[tok, tok, full(posk), full(posv), full(wk1a), full(wk1b), full(wk2), full(wv1a), full(wv1b), full(wv2)],
        out_specs=[out, out],
        out_shape=[jax.ShapeDtypeStruct((b, ncp, LANES), BF16)] * 2,
        scratch_shapes=[pltpu.VMEM((ncp + 8, 2 * CMP_HIDDEN), F32)],
        compiler_params=pltpu.CompilerParams(dimension_semantics=("parallel",), vmem_limit_bytes=VMEM_LIMIT),
        name="compress",
    )(kc, vc, posk, posv, wk1a, wk1b, wk2, wv1a, wv1b, wv2)


def _pair_masks(rows):
    lane = lax.broadcasted_iota(jnp.int32, (rows, LANES), 1)
    return lane < HEAD_DIM


A_SUPER = QB * DIL_PATTERNS[-1][1]


def _mixer_a_kernel(q_ref, kp_ref, kc_ref, vp_ref, vc_ref, bias_ref, out_ref, qf_ref, kf_ref, vf_ref, o_ref, lse_ref):
    first = pl.program_id(1) == 0
    n_pairs = DIL_HEADS // 2
    for p in range(n_pairs):
        cs = slice(p * LANES, (p + 1) * LANES)
        qf_ref[p] = q_ref[0, :, cs].astype(F32)
        kf_ref[p, 0:A_SUPER, :] = kp_ref[0, :, cs].astype(F32)
        kf_ref[p, A_SUPER:2 * A_SUPER, :] = kc_ref[0, :, cs].astype(F32)
        vf_ref[p, 0:A_SUPER, :] = vp_ref[0, :, cs].astype(F32)
        vf_ref[p, A_SUPER:2 * A_SUPER, :] = vc_ref[0, :, cs].astype(F32)
    low = _pair_masks(QB)
    in_prev = lax.broadcasted_iota(jnp.int32, (QB, 2 * QB), 1) < QB
    zero = jnp.zeros((QB, LANES), BF16)

    def chunk(idx, dil, q_base, k_base, q_span, k_span, off, at_start):
        rows = lambda size: pl.ds(off, size) if dil == 1 else pl.ds(off, size, stride=dil)
        q_win = lambda ref: ref.at[pl.ds(pl.multiple_of(q_base, 8), q_span), :]
        k_win = lambda ref: ref.at[pl.ds(pl.multiple_of(k_base, 8), k_span), :]
        q_rows = rows(QB)
        prev_mask = jnp.where(jnp.logical_and(in_prev, jnp.logical_and(first, at_start)), NEG, 0.0)
        for p in range(n_pairs):
            q = q_win(qf_ref.at[p])[q_rows, :].astype(BF16)
            keys = k_win(kf_ref.at[p])[rows(2 * QB), :].astype(BF16)
            vals = k_win(vf_ref.at[p])[rows(2 * QB), :].astype(BF16)
            o_win, lse_win = q_win(o_ref.at[p]), q_win(lse_ref.at[p])
            lhs = jnp.concatenate([jnp.where(low, q, zero), jnp.where(low, zero, q)], axis=0)
            bias = jnp.concatenate([bias_ref[idx, 2 * p] + prev_mask, bias_ref[idx, 2 * p + 1] + prev_mask], axis=0)
            s = _nt_dot(lhs, keys) + bias
            m = jnp.max(s, axis=1, keepdims=True)
            e = jnp.exp(s - m)
            l = jnp.sum(e, axis=1, keepdims=True)
            pv = _dot(e.astype(BF16), vals) * (1.0 / l)
            lse = m + jnp.log(l)
            o_new = jnp.where(low, pv[:QB], pv[QB:])
            l_new = jnp.where(low, jnp.broadcast_to(lse[:QB], (QB, LANES)), jnp.broadcast_to(lse[QB:], (QB, LANES)))
            if idx > 0:
                o_old, l_old = o_win[q_rows, :], lse_win[q_rows, :]
                mx = jnp.maximum(l_old, l_new)
                w_old, w_new = jnp.exp(l_old - mx), jnp.exp(l_new - mx)
                tot = w_old + w_new
                o_new = (w_old * o_old + w_new * o_new) * (1.0 / tot)
                l_new = mx + jnp.log(tot)
            o_win[q_rows, :] = o_new
            lse_win[q_rows, :] = l_new

    def loop(n, body):
        lax.fori_loop(0, n, lambda i, carry: (body(i), carry)[1], 0, unroll=min(n, 4))

    for idx, (_, dil) in enumerate(DIL_PATTERNS):
        span = QB * dil
        n_chunks = A_SUPER // span
        if n_chunks > 1:
            for r in range(dil):
                loop(n_chunks, lambda c, idx=idx, dil=dil, span=span, r=r:
                     chunk(idx, dil, span * c, A_SUPER + span * (c - 1), span, 2 * span, r, c == 0))
        else:
            for r in range(8):
                loop(dil // 8, lambda hi, idx=idx, dil=dil, span=span, r=r:
                     chunk(idx, dil, 8 * hi, 8 * hi, span - 8, 2 * span - 8, r, True))
    for p in range(n_pairs):
        out_ref[0, :, p * LANES:(p + 1) * LANES] = o_ref[p].astype(BF16)


def _mixer_a(qa, ka, va, bias):
    b, t, _ = qa.shape
    cur = pl.BlockSpec((1, A_SUPER, DA), lambda bi, i: (bi, i, 0))
    prev = pl.BlockSpec((1, A_SUPER, DA), lambda bi, i: (bi, jnp.maximum(i - 1, 0), 0))
    return pl.pallas_call(
        _mixer_a_kernel,
        grid=(b, t // A_SUPER),
        in_specs=[cur, prev, cur, prev, cur, pl.BlockSpec(bias.shape, lambda bi, i: (0, 0, 0, 0))],
        out_specs=cur,
        out_shape=jax.ShapeDtypeStruct((b, t, DA), BF16),
        scratch_shapes=[pltpu.VMEM((DIL_HEADS // 2, rows, LANES), F32) for rows in (A_SUPER, 2 * A_SUPER, 2 * A_SUPER, A_SUPER, A_SUPER)],
        compiler_params=pltpu.CompilerParams(dimension_semantics=("parallel", "parallel"), vmem_limit_bytes=VMEM_LIMIT),
        name="mixer_a",
    )(qa, ka, ka, va, va, bias)


def _dil_bias(f_a, dil):
    steps = DIL_PATTERNS[0][0]
    g = f_a[:, 0:dil * steps + 1:dil]
    lo, hi = QB - (2 * QB - 1), QB + QB
    vals = jnp.concatenate([jnp.full((DIL_HEADS, -lo), NEG, F32), g, jnp.full((DIL_HEADS, hi - steps - 1), NEG, F32)], axis=1)
    return _toeplitz_of(vals, lo, QB, 2 * QB)


CMP_TILE_KEYS = LANES
CMP_TILE_SPAN = CMP_TILE_KEYS * CMP_STRIDE // QB
CMP_CONST_DELTA = 28


def _cmp_bias(f_b):
    per = QB // CMP_STRIDE
    n_rows = per * (CMP_CONST_DELTA + 1)
    m_lo, m_hi = -(CMP_TILE_KEYS - 1), n_rows
    f_b = f_b.astype(BF16)
    base = _extend(f_b, CMP_STRIDE * m_lo - (CMP_BLOCK - 1), CMP_STRIDE * m_hi - (CMP_BLOCK - 1))
    g = base.reshape(NSA_HEADS, m_hi - m_lo, CMP_STRIDE).transpose(0, 2, 1)
    t = _toeplitz_of(g, m_lo, n_rows, CMP_TILE_KEYS)
    t = t.reshape(NSA_HEADS, CMP_STRIDE, CMP_CONST_DELTA + 1, per, CMP_TILE_KEYS).transpose(2, 0, 3, 1, 4)
    t = t.reshape(CMP_CONST_DELTA + 1, NSA_HEADS, QB, CMP_TILE_KEYS)
    return jnp.concatenate([jnp.full((1,) + t.shape[1:], NEG, t.dtype), t], axis=0)


def _overlap_matrix_t(ncp, n_sel_pad):
    n = np.arange(ncp)[None, :] * CMP_STRIDE
    s = np.arange(n_sel_pad)[:, None] * SEL_BLOCK
    ov = np.clip(np.minimum(n + CMP_BLOCK, s + SEL_BLOCK) - np.maximum(n, s), 0, None) / CMP_BLOCK
    return jnp.asarray(ov, BF16)


def _gate_tile(gates_ref, branch, g, row0=0):
    c = branch * NSA_HEADS + g * 2
    low = _pair_masks(QB)
    return jnp.where(low, jnp.broadcast_to(gates_ref[0, row0:row0 + QB, c:c + 1], (QB, LANES)),
                     jnp.broadcast_to(gates_ref[0, row0:row0 + QB, c + 1:c + 2], (QB, LANES)))


def _masked_q(qb_ref, g, kv, row0=0):
    q = qb_ref[0, row0:row0 + QB, g * LANES:(g + 1) * LANES]
    low = _pair_masks(QB)
    keep = low if kv == 0 else jnp.logical_not(low)
    return jnp.where(keep, q, jnp.zeros_like(q))


def _cmp_kernel(n_tiles, qb_ref, kcmp_ref, vcmp_ref, gates_ref, ov_ref, *rest):
    tbl_refs, (oc_ref, sel_ref, q_ref, s_ref, p_ref, pv_ref) = rest[:n_tiles], rest[n_tiles:]
    qblk = pl.program_id(1)
    t0 = qblk * QB
    low = _pair_masks(QB)
    for kv in range(NSA_KV_HEADS):
        for g in range(NSA_GROUP):
            r = kv * NSA_GROUP + g
            q_ref[r * QB:(r + 1) * QB, :] = _masked_q(qb_ref, g, kv)

    def attend(n_vis):
        kc = n_vis * CMP_TILE_KEYS
        n_blk = n_vis * CMP_TILE_SPAN * QB // SEL_BLOCK
        s_ref[:, 0:kc] = _nt_dot(q_ref[...], kcmp_ref[0, 0:kc, :])
        blk = lax.broadcasted_iota(jnp.int32, (n_blk, QB), 0)
        cur = (t0 + lax.broadcasted_iota(jnp.int32, (n_blk, QB), 1)) // SEL_BLOCK
        blk_f = blk.astype(F32)
        forced = (blk == cur) | (blk == cur - 1) | (blk == 0)
        causal = blk <= cur
        scores = []
        for kv in range(NSA_KV_HEADS):
            psum = jnp.zeros((QB, kc), F32)
            for g in range(NSA_GROUP):
                r = kv * NSA_GROUP + g
                rows = slice(r * QB, (r + 1) * QB)
                s = s_ref[rows, 0:kc] + jnp.concatenate([tbl_refs[c][0, r].astype(F32) for c in range(n_vis)], axis=1)
                m = jnp.max(s, axis=1, keepdims=True)
                e = jnp.exp2(s - m)
                den = jnp.sum(e, axis=1, keepdims=True)
                p = e * jnp.where(m > 0.5 * NEG, 1.0 / den, 0.0)
                psum = psum + p
                p_ref[rows, 0:kc] = p.astype(BF16)
            hi = psum.astype(BF16)
            lo = (psum - hi.astype(F32)).astype(BF16)
            ov_t = ov_ref[0:n_blk, 0:kc]
            imp_t = _nt_dot(ov_t, hi) + _nt_dot(ov_t, lo)
            scores.append(jnp.where(forced, -jnp.inf, jnp.where(causal, imp_t, -1.0)))
        pv_ref[...] = _dot(p_ref[:, 0:kc], vcmp_ref[0, 0:kc, :])

        def pick(_, carry):
            new = []
            for val, sel in carry:
                mx = jnp.max(val, axis=0, keepdims=True)
                idx = jnp.min(jnp.where(val == mx, blk_f, float(n_blk)), axis=0, keepdims=True)
                hit = blk_f == idx
                new.append((jnp.where(hit, -jnp.inf, val), jnp.where(hit, 1.0, sel)))
            return tuple(new)

        taken = jnp.where(forced, 1.0, 0.0)
        picked = lax.fori_loop(0, SEL_TOPK - N_FORCED, pick, tuple((v, taken) for v in scores))
        eye = jnp.where(lax.broadcasted_iota(jnp.int32, (QB, QB), 0) == lax.broadcasted_iota(jnp.int32, (QB, QB), 1),
                        1.0, 0.0).astype(BF16)
        for kv in range(NSA_KV_HEADS):
            sel_t = jnp.where(causal, picked[kv][1], 0.0).astype(BF16)
            sel_ref[0, kv, :, 0:n_blk] = _nt_dot(eye, sel_t).astype(BF16)
            if n_blk < LANES:
                sel_ref[0, kv, :, n_blk:LANES] = jnp.zeros((QB, LANES - n_blk), BF16)

    n_vis = qblk // CMP_TILE_SPAN + 1
    for w in range(1, n_tiles + 1):
        pl.when(n_vis == w)(functools.partial(attend, w))

    for g in range(NSA_GROUP):
        o0, o1 = pv_ref[g * QB:(g + 1) * QB, :], pv_ref[(NSA_GROUP + g) * QB:(NSA_GROUP + g + 1) * QB, :]
        oc_ref[0, :, g * LANES:(g + 1) * LANES] = (jnp.where(low, o0, o1) * _gate_tile(gates_ref, 0, g)).astype(BF16)


def _compressed_branch(qb, kcmp, vcmp, gates, f_b):
    b, t, _ = qb.shape
    ncp = kcmp.shape[1]
    n_tiles = ncp // CMP_TILE_KEYS
    tbl = _cmp_bias(f_b)
    ov = _overlap_matrix_t(ncp, LANES)

    def tbl_spec(c):
        return pl.BlockSpec((1, NSA_HEADS, QB, CMP_TILE_KEYS),
                            lambda bi, i: (jnp.clip(i - CMP_TILE_SPAN * c, -1, CMP_CONST_DELTA) + 1, 0, 0, 0))

    blockq = lambda w: pl.BlockSpec((1, QB, w), lambda bi, i: (bi, i, 0))
    batch = lambda a: pl.BlockSpec((1,) + a.shape[1:], lambda bi, i: (bi, 0, 0))
    return pl.pallas_call(
        functools.partial(_cmp_kernel, n_tiles),
        grid=(b, t // QB),
        in_specs=[blockq(DB), batch(kcmp), batch(vcmp), blockq(LANES), pl.BlockSpec(ov.shape, lambda bi, i: (0, 0))]
                 + [tbl_spec(c) for c in range(n_tiles)],
        out_specs=[blockq(DB), pl.BlockSpec((1, NSA_KV_HEADS, QB, LANES), lambda bi, i: (bi, 0, i, 0))],
        out_shape=[jax.ShapeDtypeStruct((b, t, DB), BF16), jax.ShapeDtypeStruct((b, NSA_KV_HEADS, t, LANES), BF16)],
        scratch_shapes=[pltpu.VMEM((N_ROWGROUPS * QB, LANES), BF16), pltpu.VMEM((N_ROWGROUPS * QB, ncp), F32),
                        pltpu.VMEM((N_ROWGROUPS * QB, ncp), BF16), pltpu.VMEM((N_ROWGROUPS * QB, LANES), F32)],
        compiler_params=pltpu.CompilerParams(dimension_semantics=("parallel", "parallel"), vmem_limit_bytes=VMEM_LIMIT),
        name="nsa_compressed",
    )(qb, kcmp, vcmp, gates, ov, *([tbl] * n_tiles))


SEL_NEAR = 13
SEL_FAR_BLOCK, SEL_NEAR_BLOCK = 16, 8


def _sel_bias(f_b):
    n_off = SEL_NEAR + 1
    cols = QB * n_off
    rel = f_b - f_b[:, BIAS_LEN - 1:]
    lo = -(QB - 1)
    big = _toeplitz_of(_extend(rel, lo, lo + QB + cols - 1), lo, QB, cols)
    tiles = jnp.flip(big.reshape(NSA_HEADS, QB, n_off, QB).transpose(2, 0, 1, 3), axis=0)
    return jnp.concatenate([jnp.zeros((1,) + tiles.shape[1:], F32), tiles], axis=0)


def _pair_ratio(acc0, acc1):
    low = _pair_masks(acc0.shape[0])
    den = pltpu.roll(jnp.where(low, acc1, acc0), HEAD_DIM, axis=1)
    return jnp.where(low, acc0, acc1) * (1.0 / den)


def _sel_kernel(qb_ref, sel_ref, gates_ref, cfar_ref, ks_ref, vs0_ref, vs1_ref, tbl_ref, out_ref,
                qaug_ref, s_ref, s1_ref, acc_ref, m_ref, alpha_ref):
    qblk = pl.program_id(1)
    for kv in range(NSA_KV_HEADS):
        unchosen = jnp.where(sel_ref[0, kv].astype(F32) > 0.0, 0.0, NEG)
        for g in range(NSA_GROUP):
            r = kv * NSA_GROUP + g
            rows = slice(r * QB, (r + 1) * QB)
            qaug_ref[rows, 0:LANES] = _masked_q(qb_ref, g, kv)
            qaug_ref[rows, LANES:2 * LANES] = (unchosen + cfar_ref[r:r + 1, :]).astype(BF16)
    acc_ref[...] = jnp.zeros_like(acc_ref)
    m_ref[...] = jnp.full_like(m_ref, NEG)

    last_tile = ks_ref.shape[2] // SEL_KT - 1

    def scores(j, dst_ref):
        start = pl.multiple_of(jnp.minimum(j, last_tile) * SEL_KT, SEL_KT)
        dst_ref[...] = _dot(qaug_ref[...], ks_ref[0, :, pl.ds(start, SEL_KT)])

    def consume(j, src_ref, near):
        start = pl.multiple_of(j * SEL_KT, SEL_KT)
        for r in range(N_ROWGROUPS):
            rows = slice(r * QB, (r + 1) * QB)
            s = src_ref[rows, :]
            if near:
                e1 = jnp.clip(qblk - 2 * j + 1, 0, SEL_NEAR + 1)
                e2 = jnp.clip(qblk - 2 * j, 0, SEL_NEAR + 1)
                s = s + jnp.concatenate([tbl_ref[e1, r], tbl_ref[e2, r]], axis=1)
                src_ref[rows, :] = s
            m_old = m_ref[rows, :]
            m_new = jnp.maximum(m_old, jnp.max(s, axis=1, keepdims=True))
            alpha_ref[rows, :] = jnp.exp2(m_old - m_new)
            m_ref[rows, :] = m_new
        for r in range(N_ROWGROUPS):
            rows = slice(r * QB, (r + 1) * QB)
            vals = (vs0_ref if r < NSA_GROUP else vs1_ref)[0, pl.ds(start, SEL_KT), :]
            m_new = m_ref[rows, :]
            p = jnp.exp2(src_ref[rows, :] - jnp.concatenate([m_new, m_new], axis=1))
            acc_ref[rows, :] = alpha_ref[rows, :] * acc_ref[rows, :] + _dot(p.astype(BF16), vals)

    def tile_run(first, count, near):
        bufs = (s_ref, s1_ref)
        for u in range(count):
            scores(first + u + 1, bufs[(u + 1) % 2])
            consume(first + u, bufs[u % 2], near)

    def run_pairs(first, pairs, near, max_block):
        tiles = 2 * pairs
        lax.fori_loop(0, tiles // max_block, lambda i, c: (tile_run(first + max_block * i, max_block, near), c)[1], 0)
        done = tiles // max_block * max_block
        size = max_block // 2
        while size >= 2:
            start = first + done
            pl.when((tiles - done) >= size)(functools.partial(tile_run, start, size, near))
            done = done + jnp.where((tiles - done) >= size, size, 0)
            size //= 2

    n_pairs = ((qblk + 2) // 2 + 1) // 2
    n_far = jnp.maximum((qblk - (SEL_NEAR - 1)) // 2, 0) // 2
    scores(0, s_ref)
    run_pairs(0, n_far, False, SEL_FAR_BLOCK)
    run_pairs(2 * n_far, n_pairs - n_far, True, SEL_NEAR_BLOCK)
    for g in range(NSA_GROUP):
        ratio = _pair_ratio(acc_ref[g * QB:(g + 1) * QB, :], acc_ref[(NSA_GROUP + g) * QB:(NSA_GROUP + g + 1) * QB, :])
        out_ref[0, :, g * LANES:(g + 1) * LANES] = (ratio * _gate_tile(gates_ref, 1, g)).astype(BF16)


def _selected_branch(qb, sel, gates, ks_t, vs0aug, vs1aug, f_b):
    b, t, _ = qb.shape
    tbl = _sel_bias(f_b)
    cfar = jnp.broadcast_to(f_b[:, BIAS_LEN - 1:], (NSA_HEADS, LANES))
    cfar = jnp.pad(cfar, ((0, 16 - NSA_HEADS), (0, 0)))
    blockq = lambda w: pl.BlockSpec((1, QB, w), lambda bi, i: (bi, i, 0))
    batch = lambda a: pl.BlockSpec((1,) + a.shape[1:], lambda bi, i: (bi, 0, 0))
    rows = N_ROWGROUPS * QB
    return pl.pallas_call(
        _sel_kernel,
        grid=(b, t // QB),
        in_specs=[blockq(DB), pl.BlockSpec((1, NSA_KV_HEADS, QB, LANES), lambda bi, i: (bi, 0, i, 0)), blockq(LANES),
                  pl.BlockSpec(cfar.shape, lambda bi, i: (0, 0)), batch(ks_t), batch(vs0aug), batch(vs1aug),
                  pl.BlockSpec(tbl.shape, lambda bi, i: (0, 0, 0, 0))],
        out_specs=blockq(DB),
        out_shape=jax.ShapeDtypeStruct((b, t, DB), BF16),
        scratch_shapes=[pltpu.VMEM((rows, 2 * LANES), BF16), pltpu.VMEM((rows, SEL_KT), F32), pltpu.VMEM((rows, SEL_KT), F32),
                        pltpu.VMEM((rows, LANES), F32), pltpu.VMEM((rows, LANES), F32), pltpu.VMEM((rows, LANES), F32)],
        compiler_params=pltpu.CompilerParams(dimension_semantics=("parallel", "parallel"), vmem_limit_bytes=VMEM_LIMIT),
        name="nsa_selected",
    )(qb, sel, gates, cfar, ks_t, vs0aug, vs1aug, tbl)


WIN_KEYS = WIN + QB


def _win_bias(f_b):
    lo = WIN - (WIN_KEYS - 1)
    vals = _extend(f_b[:, :WIN], lo, WIN)
    vals = jnp.concatenate([vals, jnp.full((NSA_HEADS, lo + QB + WIN_KEYS - 1 - WIN), NEG, F32)], axis=1)
    return _toeplitz_of(vals, lo, QB, WIN_KEYS)


WIN_QBLOCKS = 2


def _win_kernel(qb_ref, gates_ref, kw_ref, vw0_ref, vw1_ref, tbl_ref, out_ref, q_ref, s_ref):
    for h in range(WIN_QBLOCKS):
        qblk = pl.program_id(1) * WIN_QBLOCKS + h
        row0 = h * QB
        for kv in range(NSA_KV_HEADS):
            for g in range(NSA_GROUP):
                r = kv * NSA_GROUP + g
                q_ref[h, r * QB:(r + 1) * QB, :] = _masked_q(qb_ref, g, kv, row0)
        first_key = jnp.maximum(qblk * QB - WIN, 0)
        start = pl.multiple_of(first_key, QB)
        shift = pl.multiple_of(WIN - (qblk * QB - first_key), QB)
        s_ref[h] = _nt_dot(q_ref[h], kw_ref[0, pl.ds(start, WIN_KEYS), :])
        outs = []
        for r in range(N_ROWGROUPS):
            vals = (vw0_ref if r < NSA_GROUP else vw1_ref)[0, pl.ds(start, WIN_KEYS), :]
            s = s_ref[h, r * QB:(r + 1) * QB, :] + tbl_ref[r, :, pl.ds(shift, WIN_KEYS)]
            e = jnp.exp2(s - jnp.max(s, axis=1, keepdims=True))
            outs.append(_dot(e.astype(BF16), vals))
        for g in range(NSA_GROUP):
            out_ref[0, row0:row0 + QB, g * LANES:(g + 1) * LANES] = (
                _pair_ratio(outs[g], outs[NSA_GROUP + g]) * _gate_tile(gates_ref, 2, g, row0)).astype(BF16)


def _window_branch(qb, gates, kw, vw0aug, vw1aug, f_b):
    b, t, _ = qb.shape
    assert t >= WIN_KEYS
    tbl = _win_bias(f_b)
    tbl = jnp.concatenate([tbl, jnp.full(tbl.shape[:2] + (WIN,), NEG, F32)], axis=2)
    blockq = lambda w: pl.BlockSpec((1, WIN_QBLOCKS * QB, w), lambda bi, i: (bi, i, 0))
    batch = lambda a: pl.BlockSpec((1,) + a.shape[1:], lambda bi, i: (bi, 0, 0))
    rows = N_ROWGROUPS * QB
    return pl.pallas_call(
        _win_kernel,
        grid=(b, t // (WIN_QBLOCKS * QB)),
        in_specs=[blockq(DB), blockq(LANES), batch(kw), batch(vw0aug), batch(vw1aug),
                  pl.BlockSpec(tbl.shape, lambda bi, i: (0, 0, 0))],
        out_specs=blockq(DB),
        out_shape=jax.ShapeDtypeStruct((b, t, DB), BF16),
        scratch_shapes=[pltpu.VMEM((WIN_QBLOCKS, rows, LANES), BF16), pltpu.VMEM((WIN_QBLOCKS, rows, WIN_KEYS), F32)],
        compiler_params=pltpu.CompilerParams(dimension_semantics=("parallel", "parallel"), vmem_limit_bytes=VMEM_LIMIT),
        name="nsa_window",
    )(qb, gates, kw, vw0aug, vw1aug, tbl)


OUT_TM = 256
C_GROUP = N_EXPERTS


def _outproj_kernel(x_ref, oa_ref, oc_ref, os_ref, ow_ref,
                    wout_ref, g_ref, wr_ref, br_ref, h_ref, hn_ref, comb_ref):
    ob = oc_ref[...].astype(F32) + os_ref[...].astype(F32) + ow_ref[...].astype(F32)
    y = _dot(oa_ref[...], wout_ref[0:DA, :]) + _dot(ob.astype(BF16), wout_ref[DA:DA + DB, :])
    h = x_ref[...] + y
    h_ref[...] = h
    hn = h * lax.rsqrt(jnp.mean(h * h, axis=-1, keepdims=True) + EPS) * g_ref[...]
    hn_hi = hn.astype(BF16)
    hn_ref[...] = hn_hi
    hn_lo = (hn - hn_hi.astype(F32)).astype(BF16)
    both = _dot(hn_hi, wr_ref[...])
    logits = both[:, 0:LANES] + both[:, LANES:2 * LANES] + _dot(hn_lo, wr_ref[:, 0:LANES]) + br_ref[...]
    lane = lax.broadcasted_iota(jnp.int32, logits.shape, 1)
    lane_f = lane.astype(F32)
    big = float(LANES)
    gl = jnp.where((lane >= C_GROUP) & (lane < C_GROUP + N_GROUPS), logits, -jnp.inf)
    gmax = jnp.max(gl, axis=1, keepdims=True)
    gidx = jnp.min(jnp.where(gl == gmax, lane_f, big), axis=1, keepdims=True) - C_GROUP
    gprob = 1.0 / jnp.sum(jnp.exp(gl - gmax), axis=1, keepdims=True)
    grp_of_lane = (lane // EXPERTS_PER_GROUP).astype(F32)
    el = jnp.where((lane < N_EXPERTS) & (grp_of_lane == gidx), logits, -jnp.inf)
    v1 = jnp.max(el, axis=1, keepdims=True)
    i1 = jnp.min(jnp.where(el == v1, lane_f, big), axis=1, keepdims=True)
    el2 = jnp.where(lane_f == i1, -jnp.inf, el)
    v2 = jnp.max(el2, axis=1, keepdims=True)
    i2 = jnp.min(jnp.where(el2 == v2, lane_f, big), axis=1, keepdims=True)
    e2 = jnp.exp(v2 - v1)
    p1 = 1.0 / (1.0 + e2)
    comb_ref[...] = (gprob * (jnp.where(lane_f == i1, p1, 0.0) + jnp.where(lane_f == i2, e2 * p1, 0.0))
                     + jnp.where(lane == C_GROUP, gidx, 0.0))


def _outproj(x2d, o_a, b_parts, w_out_perm, gain, w_router, b_router):
    n, d = x2d.shape
    row = lambda w: pl.BlockSpec((OUT_TM, w), lambda i: (i, 0))
    full = lambda a: pl.BlockSpec(a.shape, lambda i: (0, 0))
    return pl.pallas_call(
        _outproj_kernel,
        grid=(n // OUT_TM,),
        in_specs=[row(d), row(DA)] + [row(DB)] * 3 + [full(w_out_perm), pl.BlockSpec((1, d), lambda i: (0, 0)),
                                                      full(w_router), full(b_router)],
        out_specs=[row(d), row(d), row(LANES)],
        out_shape=[jax.ShapeDtypeStruct((n, d), F32), jax.ShapeDtypeStruct((n, d), BF16),
                   jax.ShapeDtypeStruct((n, LANES), F32)],
        compiler_params=pltpu.CompilerParams(dimension_semantics=("parallel",), vmem_limit_bytes=VMEM_LIMIT),
        name="outproj_router",
    )(x2d, o_a, *b_parts, w_out_perm, gain.reshape(1, d), w_router, b_router)


MOE_TM = 1024


MOE_SUB = 128
MOE_FINAL_TM = 512


def _moe_kernel(hn_ref, comb_ref, upper_ref, wg_ref, wu_ref, wd_ref, y_ref,
                perm_ref, hs_ref, cs_ref, ys_ref, start_ref, nsub_ref):
    grp = pl.program_id(1)
    tm, d = hn_ref.shape
    n_pad = perm_ref.shape[0]

    @pl.when(grp == 0)
    def _sort():
        comb = comb_ref[...]
        lane_f = lax.broadcasted_iota(jnp.int32, (tm, LANES), 1).astype(F32)
        onehot = jnp.where(lane_f == comb[:, C_GROUP:C_GROUP + 1], 1.0, 0.0).astype(BF16)
        eye8 = jnp.where(lax.broadcasted_iota(jnp.int32, (8, LANES), 0) == lax.broadcasted_iota(jnp.int32, (8, LANES), 1),
                         1.0, 0.0).astype(BF16)
        onehot_t = _nt_dot(eye8, onehot)
        cum_t = _dot(onehot_t.astype(BF16), upper_ref[...])
        start = jnp.zeros((1, 1), F32)
        pos_t = jnp.zeros((1, tm), F32)
        for k in range(N_GROUPS):
            padded = jnp.ceil(cum_t[k:k + 1, tm - 1:tm] * (1.0 / MOE_SUB)) * MOE_SUB
            pos_t = pos_t + onehot_t[k:k + 1, :] * (start + cum_t[k:k + 1, :] - 1.0)
            start_ref[k] = start[0, 0].astype(jnp.int32)
            nsub_ref[k] = (padded[0, 0] * (1.0 / MOE_SUB)).astype(jnp.int32)
            start = start + padded
        perm = jnp.where(lax.broadcasted_iota(jnp.int32, (n_pad, tm), 0) == pos_t.astype(jnp.int32), 1.0, 0.0).astype(BF16)
        perm_ref[...] = perm
        hs_ref[...] = _dot(perm, hn_ref[...]).astype(BF16)
        c_hi = comb.astype(BF16)
        c_lo = (comb - c_hi.astype(F32)).astype(BF16)
        both = _dot(perm, jnp.concatenate([c_hi, c_lo], axis=1))
        cs_ref[...] = both[:, 0:LANES] + both[:, LANES:2 * LANES]
        ys_ref[...] = jnp.zeros_like(ys_ref)

    lane = lax.broadcasted_iota(jnp.int32, (MOE_SUB, LANES), 1)

    def segment(s, carry):
        rows = pl.ds(pl.multiple_of(start_ref[grp] + s * MOE_SUB, MOE_SUB), MOE_SUB)
        x = hs_ref[rows, :]
        weights = cs_ref[rows, :]
        acc = jnp.zeros((MOE_SUB, d), F32)
        for j in range(EXPERTS_PER_GROUP):
            gate = _dot(x, wg_ref[j])
            up = _dot(x, wu_ref[j])
            w = jnp.sum(jnp.where(lane == grp * EXPERTS_PER_GROUP + j, weights, 0.0), axis=1, keepdims=True)
            acc = acc + _dot((gate * jax.nn.sigmoid(gate) * up * w).astype(BF16), wd_ref[j])
        ys_ref[rows, :] = acc.astype(BF16)
        return carry

    lax.fori_loop(0, nsub_ref[grp], segment, 0)

    @pl.when(grp == pl.num_programs(1) - 1)
    def _unsort():
        y_ref[...] = lax.dot_general(perm_ref[...], ys_ref[...], (((0,), (0,)), ((), ())),
                                     preferred_element_type=F32).astype(BF16)


def _final_kernel(h_ref, y_ref, g_ref, out_ref):
    y = h_ref[...] + y_ref[...].astype(F32)
    out_ref[...] = y * lax.rsqrt(jnp.mean(y * y, axis=-1, keepdims=True) + EPS) * g_ref[...]


def _moe(h, hn, comb, w_gate, w_up, w_down, gain):
    n, d = h.shape
    tm = min(MOE_TM, n)
    n_pad = tm + (N_GROUPS - 1) * MOE_SUB
    row = lambda w: pl.BlockSpec((tm, w), lambda i, g: (i, 0))
    group_w = lambda a: pl.BlockSpec((EXPERTS_PER_GROUP,) + a.shape[1:], lambda i, g: (g, 0, 0))
    upper = jnp.asarray(np.triu(np.ones((tm, tm), np.float32)), BF16)
    y = pl.pallas_call(
        _moe_kernel,
        grid=(n // tm, N_GROUPS),
        in_specs=[row(d), row(LANES), pl.BlockSpec((tm, tm), lambda i, g: (0, 0)),
                  group_w(w_gate), group_w(w_up), group_w(w_down)],
        out_specs=row(d),
        out_shape=jax.ShapeDtypeStruct((n, d), BF16),
        scratch_shapes=[pltpu.VMEM((n_pad, tm), BF16), pltpu.VMEM((n_pad, d), BF16), pltpu.VMEM((n_pad, LANES), F32),
                        pltpu.VMEM((n_pad, d), BF16), pltpu.SMEM((N_GROUPS,), jnp.int32), pltpu.SMEM((N_GROUPS,), jnp.int32)],
        compiler_params=pltpu.CompilerParams(dimension_semantics=("parallel", "arbitrary"), vmem_limit_bytes=VMEM_LIMIT),
        name="moe_experts",
    )(hn, comb, upper, w_gate, w_up, w_down)
    rowf = lambda w: pl.BlockSpec((MOE_FINAL_TM, w), lambda i: (i, 0))
    return pl.pallas_call(
        _final_kernel,
        grid=(n // MOE_FINAL_TM,),
        in_specs=[rowf(d), rowf(d), pl.BlockSpec((1, d), lambda i: (0, 0))],
        out_specs=rowf(d),
        out_shape=jax.ShapeDtypeStruct((n, d), F32),
        compiler_params=pltpu.CompilerParams(dimension_semantics=("parallel",), vmem_limit_bytes=VMEM_LIMIT),
        name="residual_final_norm",
    )(h, y, gain.reshape(1, d))


def _permute_w_out(w_out):
    d = w_out.shape[1]
    wb = w_out[DA:].reshape(NSA_KV_HEADS, NSA_GROUP, HEAD_DIM, d).transpose(1, 0, 2, 3).reshape(DB, d)
    return jnp.concatenate([w_out[:DA], wb], axis=0).astype(BF16)


def _router_weights(w_group, b_group, w_expert, b_expert):
    d = w_group.shape[0]
    w = jnp.concatenate([w_expert.reshape(d, N_EXPERTS), w_group], axis=1)
    b = jnp.concatenate([b_expert.reshape(N_EXPERTS), b_group])
    pad = LANES - w.shape[1]
    w = jnp.pad(w, ((0, 0), (0, pad))).astype(F32)
    w_hi = w.astype(BF16)
    w_lo = (w - w_hi.astype(F32)).astype(BF16)
    return jnp.concatenate([w_hi, w_lo], axis=1), jnp.pad(b, (0, pad)).reshape(1, LANES).astype(F32)


def _layer(h, rel_bias, norm_mix, w_in, w_out, cmp_pos_k, cmp_pos_v, cmp_k_w1, cmp_k_w2, cmp_v_w1, cmp_v_w2,
           norm_ffn, w_rg, b_rg, w_re, b_re, w_gate, w_up, w_down, out_gain):
    b, t, d = h.shape
    n = b * t
    assert t % (QB * DIL_PATTERNS[-1][1]) == 0 and t // SEL_BLOCK <= LANES and n % MOE_TM == 0
    x2d = h.reshape(n, d)
    seq = lambda a: a if a.ndim == 3 else a.reshape(b, a.shape[0] // b, a.shape[-1])
    qa, ka, va, qb, kc, vc, ks_t, vs0aug, vs1aug, kw, vw0aug, vw1aug, gates = map(
        seq, _inproj(x2d, norm_mix, _permute_w_in(w_in), t))
    f_a = _bias_1d(rel_bias[:, :DIL_HEADS])
    f_b = _bias_1d(rel_bias[:, DIL_HEADS:]) * LOG2E
    o_a = _mixer_a(qa, ka, va, jnp.stack([_dil_bias(f_a, dil) for _, dil in DIL_PATTERNS]))
    kcmp, vcmp = _compress(kc, vc, cmp_pos_k, cmp_pos_v, cmp_k_w1, cmp_k_w2, cmp_v_w1, cmp_v_w2)
    o_cmp, sel = _compressed_branch(qb, kcmp, vcmp, gates, f_b)
    o_sel = _selected_branch(qb, sel, gates, ks_t, vs0aug, vs1aug, f_b)
    o_win = _window_branch(qb, gates, kw, vw0aug, vw1aug, f_b)
    b_parts = [o.reshape(n, DB) for o in (o_cmp, o_sel, o_win)]
    w_router, b_router = _router_weights(w_rg, b_rg, w_re, b_re)
    h2, hn, comb = _outproj(x2d, o_a.reshape(n, DA), b_parts, _permute_w_out(w_out), norm_ffn, w_router, b_router)
    return _moe(h2, hn, comb, w_gate.astype(BF16), w_up.astype(BF16), w_down.astype(BF16), out_gain)


def kernel(x, rel_bias, norm_mix, w_in, w_out, cmp_pos_k, cmp_pos_v, cmp_k_w1, cmp_k_w2, cmp_v_w1, cmp_v_w2,
           norm_ffn, w_router_group, b_router_group, w_router_expert, b_router_expert, w_gate, w_up, w_down,
           norm_final):
    depth = norm_mix.shape[0]
    assert depth == 1, "the final RMSNorm is fused into the last layer's expert kernel"
    out = _layer(x, rel_bias, norm_mix[0], w_in[0], w_out[0], cmp_pos_k[0], cmp_pos_v[0], cmp_k_w1[0], cmp_k_w2[0],
                 cmp_v_w1[0], cmp_v_w2[0], norm_ffn[0], w_router_group[0], b_router_group[0], w_router_expert[0],
                 b_router_expert[0], w_gate[0], w_up[0], w_down[0], norm_final)
    return out.reshape(x.shape)
```

```python
import functools
import math

import jax
import jax.numpy as jnp
import numpy as np
from jax import lax
from jax.experimental import pallas as pl
from jax.experimental.pallas import tpu as pltpu

HEAD_DIM = 64
DIL_HEADS = 6
NSA_KV_HEADS = 2
NSA_GROUP = 5
NSA_HEADS = NSA_KV_HEADS * NSA_GROUP
N_HEADS = DIL_HEADS + NSA_HEADS
DIL_PATTERNS = ((128, 1), (512, 4), (2048, 16))
CMP_BLOCK = 32
CMP_STRIDE = 16
CMP_HIDDEN = 256
SEL_BLOCK = 64
SEL_TOPK = 16
WIN = 512
N_FORCED = 3
N_BUCKETS = 32
MAX_DISTANCE = 2048
N_GROUPS = 4
EXPERTS_PER_GROUP = 4
N_EXPERTS = N_GROUPS * EXPERTS_PER_GROUP
D_EXPERT = 512
EPS = 1e-6

LANES = 128
QB = 128
NEG = -1.0e30
LOG2E = math.log2(math.e)
DA = DIL_HEADS * HEAD_DIM
DB = NSA_HEADS * HEAD_DIM
N_ROWGROUPS = NSA_HEADS
SEL_KT = 256
VMEM_LIMIT = 56 * 1024 * 1024

F32 = jnp.float32
BF16 = jnp.bfloat16
NT_DIMS = (((1,), (1,)), ((), ()))


def _nt_dot(a, b):
    return lax.dot_general(a, b, NT_DIMS, preferred_element_type=F32)


def _dot(a, b):
    return jnp.dot(a, b, preferred_element_type=F32)


def _bucket_np(dist):
    dist = np.maximum(np.asarray(dist, np.int64), 0)
    max_exact = N_BUCKETS // 2
    x = np.maximum(dist, 1).astype(np.float32) / np.float32(max_exact)
    large = max_exact + (np.log(x) / np.float32(math.log(MAX_DISTANCE / max_exact))
                         * np.float32(N_BUCKETS - max_exact)).astype(np.int32)
    large = np.minimum(large, N_BUCKETS - 1)
    return np.where(dist < max_exact, dist, large).astype(np.int32)


BIAS_LEN = 4096


def _bias_1d(rel_bias_heads):
    onehot = (_bucket_np(np.arange(BIAS_LEN))[None, :] == np.arange(N_BUCKETS)[:, None]).astype(np.float32)
    return jnp.dot(rel_bias_heads.T.astype(F32), jnp.asarray(onehot), precision=lax.Precision.HIGHEST)


def _extend(f, lo, hi):
    assert hi <= f.shape[-1]
    if lo >= 0:
        return f[..., lo:hi]
    pad = jnp.full(f.shape[:-1] + (-lo,), NEG, f.dtype)
    return jnp.concatenate([pad, f[..., :hi]], axis=-1)


def _toeplitz(w, q, c):
    n = q + c - 1
    assert w.shape[-1] == n
    lead = w.shape[:-1]
    wp = jnp.concatenate([w, jnp.zeros(lead + (1,), w.dtype)], axis=-1)
    flat = jnp.broadcast_to(wp[..., None, :], lead + (q, n + 1)).reshape(lead + (q * (n + 1),))
    return flat[..., :q * n].reshape(lead + (q, n))[..., q - 1:q - 1 + c]


def _toeplitz_of(fn_vals, lo, q, c):
    return _toeplitz(jnp.flip(fn_vals, axis=-1), q, c)


IN_TM = 512
C_QA, C_KA, C_VA = 0, DA, 2 * DA
C_QB = 3 * DA
C_KC = C_QB + DB
C_VC, C_KS, C_VS, C_KW, C_VW, C_GT = (C_KC + LANES * i for i in range(1, 7))
N_COLS = C_GT + LANES


def _permute_w_in(w_in):
    scale = 1.0 / math.sqrt(HEAD_DIM)
    sizes = [DA] * 3 + [DB] + [NSA_KV_HEADS * HEAD_DIM] * 6 + [3 * NSA_HEADS]
    offs = np.concatenate([[0], np.cumsum(sizes)])
    part = lambda i: w_in[:, offs[i]:offs[i + 1]]
    d = w_in.shape[0]
    qb = part(3).reshape(d, NSA_KV_HEADS, NSA_GROUP, HEAD_DIM).transpose(0, 2, 1, 3).reshape(d, DB)
    gt = part(10).reshape(d, NSA_KV_HEADS, NSA_GROUP, 3).transpose(0, 3, 2, 1).reshape(d, 3 * NSA_HEADS)
    gt = jnp.pad(gt, ((0, 0), (0, LANES - 3 * NSA_HEADS)))
    cols = [part(0) * scale, part(1), part(2), qb * (scale * LOG2E)] + [part(i) for i in range(4, 10)] + [gt]
    return jnp.concatenate(cols, axis=1).astype(BF16)


def _inproj_kernel(seq_len, x_ref, g_ref, w_ref, qa_ref, ka_ref, va_ref, qb_ref, kc_ref, vc_ref,
                   kst_ref, vs0_ref, vs1_ref, kw_ref, vw0_ref, vw1_ref, gates_ref, stage_ref):
    x = x_ref[...]
    xn = (x * lax.rsqrt(jnp.mean(x * x, axis=-1, keepdims=True) + EPS) * g_ref[...]).astype(BF16)
    seg = lambda a, n: _dot(xn, w_ref[:, a:a + n])
    qa_ref[...] = seg(C_QA, DA).astype(BF16)
    ka_ref[...] = seg(C_KA, DA).astype(BF16)
    va_ref[...] = seg(C_VA, DA).astype(BF16)
    qb_ref[...] = seg(C_QB, DB).astype(BF16)
    kw_ref[...] = seg(C_KW, LANES).astype(BF16)
    tm = x.shape[0]
    for col, out_ref in ((C_KC, kc_ref), (C_VC, vc_ref)):
        stage_ref[...] = seg(col, LANES)
        for j in range(CMP_STRIDE):
            out_ref[:, j * LANES:(j + 1) * LANES] = stage_ref[pl.ds(j, tm // CMP_STRIDE, stride=CMP_STRIDE), :].astype(BF16)
    tok_t = (pl.program_id(0) * tm) % seq_len + lax.broadcasted_iota(jnp.int32, (LANES, tm), 1)
    blk_t = lax.broadcasted_iota(jnp.int32, (LANES, tm), 0)
    stage_ref[...] = seg(C_KS, LANES)
    kst_ref[0, 0:LANES, :] = stage_ref[...].T.astype(BF16)
    kst_ref[0, LANES:2 * LANES, :] = jnp.where(blk_t == tok_t // SEL_BLOCK, 1.0, 0.0).astype(BF16)
    low = lax.broadcasted_iota(jnp.int32, (tm, LANES), 1) < HEAD_DIM
    for col, ref0, ref1 in ((C_VS, vs0_ref, vs1_ref), (C_VW, vw0_ref, vw1_ref)):
        v = seg(col, LANES)
        ref0[...] = jnp.where(low, v, 1.0).astype(BF16)
        ref1[...] = jnp.where(low, 1.0, v).astype(BF16)
    gates_ref[...] = jax.nn.sigmoid(seg(C_GT, LANES))


def _inproj(x2d, gain, w_perm, seq_len):
    n, d = x2d.shape
    row = lambda w: pl.BlockSpec((IN_TM, w), lambda i: (i, 0))
    rows = lambda w: (jax.ShapeDtypeStruct((n, w), BF16), row(w))
    chunks = (jax.ShapeDtypeStruct((n // CMP_STRIDE, CMP_STRIDE * LANES), BF16),
              pl.BlockSpec((IN_TM // CMP_STRIDE, CMP_STRIDE * LANES), lambda i: (i, 0)))
    per_seq = seq_len // IN_TM
    keys_t = (jax.ShapeDtypeStruct((n // seq_len, 2 * LANES, seq_len), BF16),
              pl.BlockSpec((1, 2 * LANES, IN_TM), lambda i: (i // per_seq, 0, i % per_seq)))
    outs = [rows(DA), rows(DA), rows(DA), rows(DB), chunks, chunks, keys_t] + [rows(LANES)] * 5
    outs.append((jax.ShapeDtypeStruct((n, LANES), F32), row(LANES)))
    return pl.pallas_call(
        functools.partial(_inproj_kernel, seq_len),
        grid=(n // IN_TM,),
        in_specs=[row(d), pl.BlockSpec((1, d), lambda i: (0, 0)), pl.BlockSpec((d, N_COLS), lambda i: (0, 0))],
        out_specs=[spec for _, spec in outs],
        out_shape=[shape for shape, _ in outs],
        scratch_shapes=[pltpu.VMEM((IN_TM, LANES), F32)],
        compiler_params=pltpu.CompilerParams(dimension_semantics=("parallel",), vmem_limit_bytes=VMEM_LIMIT),
        name="inproj",
    )(x2d, gain.reshape(1, d), w_perm)


def _embed_pair(w, n_tok):
    c = w.shape[1]
    w4 = w.reshape(n_tok, 1, HEAD_DIM, 1, c) * jnp.eye(NSA_KV_HEADS, dtype=w.dtype).reshape(1, 2, 1, 2, 1)
    return w4.reshape(n_tok * 2 * HEAD_DIM, 2 * c)


def _gelu_tanh(x):
    return 0.5 * x * (1.0 + jnp.tanh(math.sqrt(2.0 / math.pi) * (x + 0.044715 * (x * x * x))))


def _compress_kernel(ck_ref, cv_ref, posk_ref, posv_ref, wk1a, wk1b, wk2, wv1a, wv1b, wv2,
                     kout_ref, vout_ref, shift_ref):
    ncp = ck_ref.shape[1]
    for c_ref, pos_ref, w1a, w1b, w2, out_ref in ((ck_ref, posk_ref, wk1a, wk1b, wk2, kout_ref),
                                                  (cv_ref, posv_ref, wv1a, wv1b, wv2, vout_ref)):
        c = c_ref[0].astype(F32)
        first = _dot((c + pos_ref[0:1, :]).astype(BF16), w1a[...])
        second = _dot((c + pos_ref[1:2, :]).astype(BF16), w1b[...])
        shift_ref[0:ncp, :] = second
        shift_ref[ncp:ncp + 8, :] = jnp.zeros((8, second.shape[1]), F32)
        hidden = _gelu_tanh(first + shift_ref[1:ncp + 1, :])
        out_ref[0] = _dot(hidden.astype(BF16), w2[...]).astype(BF16)


def _compress(kc, vc, pos_k, pos_v, k_w1, k_w2, v_w1, v_w2):
    b, ncp, wide = kc.shape
    half = CMP_STRIDE * HEAD_DIM

    def prep(w1, w2, pos):
        pos_pair = jnp.broadcast_to(pos.reshape(2, CMP_STRIDE, 1, HEAD_DIM), (2, CMP_STRIDE, 2, HEAD_DIM))
        return (_embed_pair(w1[:half], CMP_STRIDE).astype(BF16), _embed_pair(w1[half:], CMP_STRIDE).astype(BF16),
                jnp.kron(jnp.eye(NSA_KV_HEADS, dtype=w2.dtype), w2).astype(BF16),
                pos_pair.reshape(2, wide).astype(F32))

    wk1a, wk1b, wk2, posk = prep(k_w1, k_w2, pos_k)
    wv1a, wv1b, wv2, posv = prep(v_w1, v_w2, pos_v)
    full = lambda a: pl.BlockSpec(a.shape, lambda i: (0,) * a.ndim)
    tok = pl.BlockSpec((1, ncp, wide), lambda i: (i, 0, 0))
    out = pl.BlockSpec((1, ncp, LANES), lambda i: (i, 0, 0))
    return pl.pallas_call(
        _compress_kernel,
        grid=(b,),
        in_specs=[tok, tok, full(posk), full(posv), full(wk1a), full(wk1b), full(wk2), full(wv1a), full(wv1b), full(wv2)],
        out_specs=[out, out],
        out_shape=[jax.ShapeDtypeStruct((b, ncp, LANES), BF16)] * 2,
        scratch_shapes=[pltpu.VMEM((ncp + 8, 2 * CMP_HIDDEN), F32)],
        compiler_params=pltpu.CompilerParams(dimension_semantics=("parallel",), vmem_limit_bytes=VMEM_LIMIT),
        name="compress",
    )(kc, vc, posk, posv, wk1a, wk1b, wk2, wv1a, wv1b, wv2)


def _pair_masks(rows):
    lane = lax.broadcasted_iota(jnp.int32, (rows, LANES), 1)
    return lane < HEAD_DIM


A_SUPER = QB * DIL_PATTERNS[-1][1]


def _mixer_a_kernel(q_ref, kp_ref, kc_ref, vp_ref, vc_ref, bias_ref, out_ref, qf_ref, kf_ref, vf_ref, o_ref, lse_ref):
    first = pl.program_id(1) == 0
    n_pairs = DIL_HEADS // 2
    for p in range(n_pairs):
        cs = slice(p * LANES, (p + 1) * LANES)
        qf_ref[p] = q_ref[0, :, cs].astype(F32)
        kf_ref[p, 0:A_SUPER, :] = kp_ref[0, :, cs].astype(F32)
        kf_ref[p, A_SUPER:2 * A_SUPER, :] = kc_ref[0, :, cs].astype(F32)
        vf_ref[p, 0:A_SUPER, :] = vp_ref[0, :, cs].astype(F32)
        vf_ref[p, A_SUPER:2 * A_SUPER, :] = vc_ref[0, :, cs].astype(F32)
    low = _pair_masks(QB)
    in_prev = lax.broadcasted_iota(jnp.int32, (QB, 2 * QB), 1) < QB
    zero = jnp.zeros((QB, LANES), BF16)

    def chunk(idx, dil, q_base, k_base, q_span, k_span, off, at_start):
        rows = lambda size: pl.ds(off, size) if dil == 1 else pl.ds(off, size, stride=dil)
        q_win = lambda ref: ref.at[pl.ds(pl.multiple_of(q_base, 8), q_span), :]
        k_win = lambda ref: ref.at[pl.ds(pl.multiple_of(k_base, 8), k_span), :]
        q_rows = rows(QB)
        prev_mask = jnp.where(jnp.logical_and(in_prev, jnp.logical_and(first, at_start)), NEG, 0.0)
        for p in range(n_pairs):
            q = q_win(qf_ref.at[p])[q_rows, :].astype(BF16)
            keys = k_win(kf_ref.at[p])[rows(2 * QB), :].astype(BF16)
            vals = k_win(vf_ref.at[p])[rows(2 * QB), :].astype(BF16)
            o_win, lse_win = q_win(o_ref.at[p]), q_win(lse_ref.at[p])
            lhs = jnp.concatenate([jnp.where(low, q, zero), jnp.where(low, zero, q)], axis=0)
            bias = jnp.concatenate([bias_ref[idx, 2 * p] + prev_mask, bias_ref[idx, 2 * p + 1] + prev_mask], axis=0)
            s = _nt_dot(lhs, keys) + bias
            m = jnp.max(s, axis=1, keepdims=True)
            e = jnp.exp(s - m)
            l = jnp.sum(e, axis=1, keepdims=True)
            pv = _dot(e.astype(BF16), vals) * (1.0 / l)
            lse = m + jnp.log(l)
            o_new = jnp.where(low, pv[:QB], pv[QB:])
            l_new = jnp.where(low, jnp.broadcast_to(lse[:QB], (QB, LANES)), jnp.broadcast_to(lse[QB:], (QB, LANES)))
            if idx > 0:
                o_old, l_old = o_win[q_rows, :], lse_win[q_rows, :]
                mx = jnp.maximum(l_old, l_new)
                w_old, w_new = jnp.exp(l_old - mx), jnp.exp(l_new - mx)
                tot = w_old + w_new
                o_new = (w_old * o_old + w_new * o_new) * (1.0 / tot)
                l_new = mx + jnp.log(tot)
            o_win[q_rows, :] = o_new
            lse_win[q_rows, :] = l_new

    def loop(n, body):
        lax.fori_loop(0, n, lambda i, carry: (body(i), carry)[1], 0, unroll=min(n, 4))

    for idx, (_, dil) in enumerate(DIL_PATTERNS):
        span = QB * dil
        n_chunks = A_SUPER // span
        if n_chunks > 1:
            for r in range(dil):
                loop(n_chunks, lambda c, idx=idx, dil=dil, span=span, r=r:
                     chunk(idx, dil, span * c, A_SUPER + span * (c - 1), span, 2 * span, r, c == 0))
        else:
            for r in range(8):
                loop(dil // 8, lambda hi, idx=idx, dil=dil, span=span, r=r:
                     chunk(idx, dil, 8 * hi, 8 * hi, span - 8, 2 * span - 8, r, True))
    for p in range(n_pairs):
        out_ref[0, :, p * LANES:(p + 1) * LANES] = o_ref[p].astype(BF16)


def _mixer_a(qa, ka, va, bias):
    b, t, _ = qa.shape
    cur = pl.BlockSpec((1, A_SUPER, DA), lambda bi, i: (bi, i, 0))
    prev = pl.BlockSpec((1, A_SUPER, DA), lambda bi, i: (bi, jnp.maximum(i - 1, 0), 0))
    return pl.pallas_call(
        _mixer_a_kernel,
        grid=(b, t // A_SUPER),
        in_specs=[cur, prev, cur, prev, cur, pl.BlockSpec(bias.shape, lambda bi, i: (0, 0, 0, 0))],
        out_specs=cur,
        out_shape=jax.ShapeDtypeStruct((b, t, DA), BF16),
        scratch_shapes=[pltpu.VMEM((DIL_HEADS // 2, rows, LANES), F32) for rows in (A_SUPER, 2 * A_SUPER, 2 * A_SUPER, A_SUPER, A_SUPER)],
        compiler_params=pltpu.CompilerParams(dimension_semantics=("parallel", "parallel"), vmem_limit_bytes=VMEM_LIMIT),
        name="mixer_a",
    )(qa, ka, ka, va, va, bias)


def _dil_bias(f_a, dil):
    steps = DIL_PATTERNS[0][0]
    g = f_a[:, 0:dil * steps + 1:dil]
    lo, hi = QB - (2 * QB - 1), QB + QB
    vals = jnp.concatenate([jnp.full((DIL_HEADS, -lo), NEG, F32), g, jnp.full((DIL_HEADS, hi - steps - 1), NEG, F32)], axis=1)
    return _toeplitz_of(vals, lo, QB, 2 * QB)


CMP_TILE_KEYS = LANES
CMP_TILE_SPAN = CMP_TILE_KEYS * CMP_STRIDE // QB
CMP_CONST_DELTA = 28


def _cmp_bias(f_b):
    per = QB // CMP_STRIDE
    n_rows = per * (CMP_CONST_DELTA + 1)
    m_lo, m_hi = -(CMP_TILE_KEYS - 1), n_rows
    f_b = f_b.astype(BF16)
    base = _extend(f_b, CMP_STRIDE * m_lo - (CMP_BLOCK - 1), CMP_STRIDE * m_hi - (CMP_BLOCK - 1))
    g = base.reshape(NSA_HEADS, m_hi - m_lo, CMP_STRIDE).transpose(0, 2, 1)
    t = _toeplitz_of(g, m_lo, n_rows, CMP_TILE_KEYS)
    t = t.reshape(NSA_HEADS, CMP_STRIDE, CMP_CONST_DELTA + 1, per, CMP_TILE_KEYS).transpose(2, 0, 3, 1, 4)
    t = t.reshape(CMP_CONST_DELTA + 1, NSA_HEADS, QB, CMP_TILE_KEYS)
    return jnp.concatenate([jnp.full((1,) + t.shape[1:], NEG, t.dtype), t], axis=0)


def _overlap_matrix_t(ncp, n_sel_pad):
    n = np.arange(ncp)[None, :] * CMP_STRIDE
    s = np.arange(n_sel_pad)[:, None] * SEL_BLOCK
    ov = np.clip(np.minimum(n + CMP_BLOCK, s + SEL_BLOCK) - np.maximum(n, s), 0, None) / CMP_BLOCK
    return jnp.asarray(ov, BF16)


def _gate_tile(gates_ref, branch, g, row0=0):
    c = branch * NSA_HEADS + g * 2
    low = _pair_masks(QB)
    return jnp.where(low, jnp.broadcast_to(gates_ref[0, row0:row0 + QB, c:c + 1], (QB, LANES)),
                     jnp.broadcast_to(gates_ref[0, row0:row0 + QB, c + 1:c + 2], (QB, LANES)))


def _masked_q(qb_ref, g, kv, row0=0):
    q = qb_ref[0, row0:row0 + QB, g * LANES:(g + 1) * LANES]
    low = _pair_masks(QB)
    keep = low if kv == 0 else jnp.logical_not(low)
    return jnp.where(keep, q, jnp.zeros_like(q))


def _cmp_kernel(n_tiles, qb_ref, kcmp_ref, vcmp_ref, gates_ref, ov_ref, *rest):
    tbl_refs, (oc_ref, sel_ref, q_ref, s_ref, p_ref, pv_ref) = rest[:n_tiles], rest[n_tiles:]
    qblk = pl.program_id(1)
    t0 = qblk * QB
    low = _pair_masks(QB)
    for kv in range(NSA_KV_HEADS):
        for g in range(NSA_GROUP):
            r = kv * NSA_GROUP + g
            q_ref[r * QB:(r + 1) * QB, :] = _masked_q(qb_ref, g, kv)

    def attend(n_vis):
        kc = n_vis * CMP_TILE_KEYS
        n_blk = n_vis * CMP_TILE_SPAN * QB // SEL_BLOCK
        s_ref[:, 0:kc] = _nt_dot(q_ref[...], kcmp_ref[0, 0:kc, :])
        blk = lax.broadcasted_iota(jnp.int32, (n_blk, QB), 0)
        cur = (t0 + lax.broadcasted_iota(jnp.int32, (n_blk, QB), 1)) // SEL_BLOCK
        blk_f = blk.astype(F32)
        forced = (blk == cur) | (blk == cur - 1) | (blk == 0)
        causal = blk <= cur
        scores = []
        for kv in range(NSA_KV_HEADS):
            psum = jnp.zeros((QB, kc), F32)
            for g in range(NSA_GROUP):
                r = kv * NSA_GROUP + g
                rows = slice(r * QB, (r + 1) * QB)
                s = s_ref[rows, 0:kc] + jnp.concatenate([tbl_refs[c][0, r].astype(F32) for c in range(n_vis)], axis=1)
                m = jnp.max(s, axis=1, keepdims=True)
                e = jnp.exp2(s - m)
                den = jnp.sum(e, axis=1, keepdims=True)
                p = e * jnp.where(m > 0.5 * NEG, 1.0 / den, 0.0)
                psum = psum + p
                p_ref[rows, 0:kc] = p.astype(BF16)
            hi = psum.astype(BF16)
            lo = (psum - hi.astype(F32)).astype(BF16)
            ov_t = ov_ref[0:n_blk, 0:kc]
            imp_t = _nt_dot(ov_t, hi) + _nt_dot(ov_t, lo)
            scores.append(jnp.where(forced, -jnp.inf, jnp.where(causal, imp_t, -1.0)))
        pv_ref[...] = _dot(p_ref[:, 0:kc], vcmp_ref[0, 0:kc, :])

        def pick(_, carry):
            new = []
            for val, sel in carry:
                mx = jnp.max(val, axis=0, keepdims=True)
                idx = jnp.min(jnp.where(val == mx, blk_f, float(n_blk)), axis=0, keepdims=True)
                hit = blk_f == idx
                new.append((jnp.where(hit, -jnp.inf, val), jnp.where(hit, 1.0, sel)))
            return tuple(new)

        taken = jnp.where(forced, 1.0, 0.0)
        picked = lax.fori_loop(0, SEL_TOPK - N_FORCED, pick, tuple((v, taken) for v in scores))
        eye = jnp.where(lax.broadcasted_iota(jnp.int32, (QB, QB), 0) == lax.broadcasted_iota(jnp.int32, (QB, QB), 1),
                        1.0, 0.0).astype(BF16)
        for kv in range(NSA_KV_HEADS):
            sel_t = jnp.where(causal, picked[kv][1], 0.0).astype(BF16)
            sel_ref[0, kv, :, 0:n_blk] = _nt_dot(eye, sel_t).astype(BF16)
            if n_blk < LANES:
                sel_ref[0, kv, :, n_blk:LANES] = jnp.zeros((QB, LANES - n_blk), BF16)

    n_vis = qblk // CMP_TILE_SPAN + 1
    for w in range(1, n_tiles + 1):
        pl.when(n_vis == w)(functools.partial(attend, w))

    for g in range(NSA_GROUP):
        o0, o1 = pv_ref[g * QB:(g + 1) * QB, :], pv_ref[(NSA_GROUP + g) * QB:(NSA_GROUP + g + 1) * QB, :]
        oc_ref[0, :, g * LANES:(g + 1) * LANES] = (jnp.where(low, o0, o1) * _gate_tile(gates_ref, 0, g)).astype(BF16)


def _compressed_branch(qb, kcmp, vcmp, gates, f_b):
    b, t, _ = qb.shape
    ncp = kcmp.shape[1]
    n_tiles = ncp // CMP_TILE_KEYS
    tbl = _cmp_bias(f_b)
    ov = _overlap_matrix_t(ncp, LANES)

    def tbl_spec(c):
        return pl.BlockSpec((1, NSA_HEADS, QB, CMP_TILE_KEYS),
                            lambda bi, i: (jnp.clip(i - CMP_TILE_SPAN * c, -1, CMP_CONST_DELTA) + 1, 0, 0, 0))

    blockq = lambda w: pl.BlockSpec((1, QB, w), lambda bi, i: (bi, i, 0))
    batch = lambda a: pl.BlockSpec((1,) + a.shape[1:], lambda bi, i: (bi, 0, 0))
    return pl.pallas_call(
        functools.partial(_cmp_kernel, n_tiles),
        grid=(b, t // QB),
        in_specs=[blockq(DB), batch(kcmp), batch(vcmp), blockq(LANES), pl.BlockSpec(ov.shape, lambda bi, i: (0, 0))]
                 + [tbl_spec(c) for c in range(n_tiles)],
        out_specs=[blockq(DB), pl.BlockSpec((1, NSA_KV_HEADS, QB, LANES), lambda bi, i: (bi, 0, i, 0))],
        out_shape=[jax.ShapeDtypeStruct((b, t, DB), BF16), jax.ShapeDtypeStruct((b, NSA_KV_HEADS, t, LANES), BF16)],
        scratch_shapes=[pltpu.VMEM((N_ROWGROUPS * QB, LANES), BF16), pltpu.VMEM((N_ROWGROUPS * QB, ncp), F32),
                        pltpu.VMEM((N_ROWGROUPS * QB, ncp), BF16), pltpu.VMEM((N_ROWGROUPS * QB, LANES), F32)],
        compiler_params=pltpu.CompilerParams(dimension_semantics=("parallel", "parallel"), vmem_limit_bytes=VMEM_LIMIT),
        name="nsa_compressed",
    )(qb, kcmp, vcmp, gates, ov, *([tbl] * n_tiles))


SEL_NEAR = 13
SEL_FAR_BLOCK, SEL_NEAR_BLOCK = 16, 8


def _sel_bias(f_b):
    n_off = SEL_NEAR + 1
    cols = QB * n_off
    rel = f_b - f_b[:, BIAS_LEN - 1:]
    lo = -(QB - 1)
    big = _toeplitz_of(_extend(rel, lo, lo + QB + cols - 1), lo, QB, cols)
    tiles = jnp.flip(big.reshape(NSA_HEADS, QB, n_off, QB).transpose(2, 0, 1, 3), axis=0)
    return jnp.concatenate([jnp.zeros((1,) + tiles.shape[1:], F32), tiles], axis=0)


def _pair_ratio(acc0, acc1):
    low = _pair_masks(acc0.shape[0])
    den = pltpu.roll(jnp.where(low, acc1, acc0), HEAD_DIM, axis=1)
    return jnp.where(low, acc0, acc1) * (1.0 / den)


def _sel_kernel(qb_ref, sel_ref, gates_ref, cfar_ref, ks_ref, vs0_ref, vs1_ref, tbl_ref, out_ref,
                qaug_ref, s_ref, s1_ref, acc_ref, m_ref, alpha_ref):
    qblk = pl.program_id(1)
    for kv in range(NSA_KV_HEADS):
        unchosen = jnp.where(sel_ref[0, kv].astype(F32) > 0.0, 0.0, NEG)
        for g in range(NSA_GROUP):
            r = kv * NSA_GROUP + g
            rows = slice(r * QB, (r + 1) * QB)
            qaug_ref[rows, 0:LANES] = _masked_q(qb_ref, g, kv)
            qaug_ref[rows, LANES:2 * LANES] = (unchosen + cfar_ref[r:r + 1, :]).astype(BF16)
    acc_ref[...] = jnp.zeros_like(acc_ref)
    m_ref[...] = jnp.full_like(m_ref, NEG)

    last_tile = ks_ref.shape[2] // SEL_KT - 1

    def scores(j, dst_ref):
        start = pl.multiple_of(jnp.minimum(j, last_tile) * SEL_KT, SEL_KT)
        dst_ref[...] = _dot(qaug_ref[...], ks_ref[0, :, pl.ds(start, SEL_KT)])

    def consume(j, src_ref, near):
        start = pl.multiple_of(j * SEL_KT, SEL_KT)
        for r in range(N_ROWGROUPS):
            rows = slice(r * QB, (r + 1) * QB)
            s = src_ref[rows, :]
            if near:
                e1 = jnp.clip(qblk - 2 * j + 1, 0, SEL_NEAR + 1)
                e2 = jnp.clip(qblk - 2 * j, 0, SEL_NEAR + 1)
                s = s + jnp.concatenate([tbl_ref[e1, r], tbl_ref[e2, r]], axis=1)
                src_ref[rows, :] = s
            m_old = m_ref[rows, :]
            m_new = jnp.maximum(m_old, jnp.max(s, axis=1, keepdims=True))
            alpha_ref[rows, :] = jnp.exp2(m_old - m_new)
            m_ref[rows, :] = m_new
        for r in range(N_ROWGROUPS):
            rows = slice(r * QB, (r + 1) * QB)
            vals = (vs0_ref if r < NSA_GROUP else vs1_ref)[0, pl.ds(start, SEL_KT), :]
            m_new = m_ref[rows, :]
            p = jnp.exp2(src_ref[rows, :] - jnp.concatenate([m_new, m_new], axis=1))
            acc_ref[rows, :] = alpha_ref[rows, :] * acc_ref[rows, :] + _dot(p.astype(BF16), vals)

    def tile_run(first, count, near):
        bufs = (s_ref, s1_ref)
        for u in range(count):
            scores(first + u + 1, bufs[(u + 1) % 2])
            consume(first + u, bufs[u % 2], near)

    def run_pairs(first, pairs, near, max_block):
        tiles = 2 * pairs
        lax.fori_loop(0, tiles // max_block, lambda i, c: (tile_run(first + max_block * i, max_block, near), c)[1], 0)
        done = tiles // max_block * max_block
        size = max_block // 2
        while size >= 2:
            start = first + done
            pl.when((tiles - done) >= size)(functools.partial(tile_run, start, size, near))
            done = done + jnp.where((tiles - done) >= size, size, 0)
            size //= 2

    n_pairs = ((qblk + 2) // 2 + 1) // 2
    n_far = jnp.maximum((qblk - (SEL_NEAR - 1)) // 2, 0) // 2
    scores(0, s_ref)
    run_pairs(0, n_far, False, SEL_FAR_BLOCK)
    run_pairs(2 * n_far, n_pairs - n_far, True, SEL_NEAR_BLOCK)
    for g in range(NSA_GROUP):
        ratio = _pair_ratio(acc_ref[g * QB:(g + 1) * QB, :], acc_ref[(NSA_GROUP + g) * QB:(NSA_GROUP + g + 1) * QB, :])
        out_ref[0, :, g * LANES:(g + 1) * LANES] = (ratio * _gate_tile(gates_ref, 1, g)).astype(BF16)


def _selected_branch(qb, sel, gates, ks_t, vs0aug, vs1aug, f_b):
    b, t, _ = qb.shape
    tbl = _sel_bias(f_b)
    cfar = jnp.broadcast_to(f_b[:, BIAS_LEN - 1:], (NSA_HEADS, LANES))
    cfar = jnp.pad(cfar, ((0, 16 - NSA_HEADS), (0, 0)))
    blockq = lambda w: pl.BlockSpec((1, QB, w), lambda bi, i: (bi, i, 0))
    batch = lambda a: pl.BlockSpec((1,) + a.shape[1:], lambda bi, i: (bi, 0, 0))
    rows = N_ROWGROUPS * QB
    return pl.pallas_call(
        _sel_kernel,
        grid=(b, t // QB),
        in_specs=[blockq(DB), pl.BlockSpec((1, NSA_KV_HEADS, QB, LANES), lambda bi, i: (bi, 0, i, 0)), blockq(LANES),
                  pl.BlockSpec(cfar.shape, lambda bi, i: (0, 0)), batch(ks_t), batch(vs0aug), batch(vs1aug),
                  pl.BlockSpec(tbl.shape, lambda bi, i: (0, 0, 0, 0))],
        out_specs=blockq(DB),
        out_shape=jax.ShapeDtypeStruct((b, t, DB), BF16),
        scratch_shapes=[pltpu.VMEM((rows, 2 * LANES), BF16), pltpu.VMEM((rows, SEL_KT), F32), pltpu.VMEM((rows, SEL_KT), F32),
                        pltpu.VMEM((rows, LANES), F32), pltpu.VMEM((rows, LANES), F32), pltpu.VMEM((rows, LANES), F32)],
        compiler_params=pltpu.CompilerParams(dimension_semantics=("parallel", "parallel"), vmem_limit_bytes=VMEM_LIMIT),
        name="nsa_selected",
    )(qb, sel, gates, cfar, ks_t, vs0aug, vs1aug, tbl)


WIN_KEYS = WIN + QB


def _win_bias(f_b):
    lo = WIN - (WIN_KEYS - 1)
    vals = _extend(f_b[:, :WIN], lo, WIN)
    vals = jnp.concatenate([vals, jnp.full((NSA_HEADS, lo + QB + WIN_KEYS - 1 - WIN), NEG, F32)], axis=1)
    return _toeplitz_of(vals, lo, QB, WIN_KEYS)


WIN_QBLOCKS = 4


def _win_kernel(qb_ref, gates_ref, kw_ref, vw0_ref, vw1_ref, tbl_ref, out_ref, q_ref, s_ref):
    for h in range(WIN_QBLOCKS):
        qblk = pl.program_id(1) * WIN_QBLOCKS + h
        row0 = h * QB
        for kv in range(NSA_KV_HEADS):
            for g in range(NSA_GROUP):
                r = kv * NSA_GROUP + g
                q_ref[h, r * QB:(r + 1) * QB, :] = _masked_q(qb_ref, g, kv, row0)
        first_key = jnp.maximum(qblk * QB - WIN, 0)
        start = pl.multiple_of(first_key, QB)
        shift = pl.multiple_of(WIN - (qblk * QB - first_key), QB)
        s_ref[h] = _nt_dot(q_ref[h], kw_ref[0, pl.ds(start, WIN_KEYS), :])
        outs = []
        for r in range(N_ROWGROUPS):
            vals = (vw0_ref if r < NSA_GROUP else vw1_ref)[0, pl.ds(start, WIN_KEYS), :]
            s = s_ref[h, r * QB:(r + 1) * QB, :] + tbl_ref[r, :, pl.ds(shift, WIN_KEYS)]
            e = jnp.exp2(s - jnp.max(s, axis=1, keepdims=True))
            outs.append(_dot(e.astype(BF16), vals))
        for g in range(NSA_GROUP):
            out_ref[0, row0:row0 + QB, g * LANES:(g + 1) * LANES] = (
                _pair_ratio(outs[g], outs[NSA_GROUP + g]) * _gate_tile(gates_ref, 2, g, row0)).astype(BF16)


def _window_branch(qb, gates, kw, vw0aug, vw1aug, f_b):
    b, t, _ = qb.shape
    assert t >= WIN_KEYS
    tbl = _win_bias(f_b)
    tbl = jnp.concatenate([tbl, jnp.full(tbl.shape[:2] + (WIN,), NEG, F32)], axis=2)
    blockq = lambda w: pl.BlockSpec((1, WIN_QBLOCKS * QB, w), lambda bi, i: (bi, i, 0))
    batch = lambda a: pl.BlockSpec((1,) + a.shape[1:], lambda bi, i: (bi, 0, 0))
    rows = N_ROWGROUPS * QB
    return pl.pallas_call(
        _win_kernel,
        grid=(b, t // (WIN_QBLOCKS * QB)),
        in_specs=[blockq(DB), blockq(LANES), batch(kw), batch(vw0aug), batch(vw1aug),
                  pl.BlockSpec(tbl.shape, lambda bi, i: (0, 0, 0))],
        out_specs=blockq(DB),
        out_shape=jax.ShapeDtypeStruct((b, t, DB), BF16),
        scratch_shapes=[pltpu.VMEM((WIN_QBLOCKS, rows, LANES), BF16), pltpu.VMEM((WIN_QBLOCKS, rows, WIN_KEYS), F32)],
        compiler_params=pltpu.CompilerParams(dimension_semantics=("parallel", "parallel"), vmem_limit_bytes=VMEM_LIMIT),
        name="nsa_window",
    )(qb, gates, kw, vw0aug, vw1aug, tbl)


OUT_TM = 256
C_GROUP = N_EXPERTS


def _outproj_kernel(x_ref, oa_ref, oc_ref, os_ref, ow_ref,
                    wout_ref, g_ref, wr_ref, br_ref, h_ref, hn_ref, comb_ref):
    ob = oc_ref[...].astype(F32) + os_ref[...].astype(F32) + ow_ref[...].astype(F32)
    y = _dot(oa_ref[...], wout_ref[0:DA, :]) + _dot(ob.astype(BF16), wout_ref[DA:DA + DB, :])
    h = x_ref[...] + y
    h_ref[...] = h
    hn = h * lax.rsqrt(jnp.mean(h * h, axis=-1, keepdims=True) + EPS) * g_ref[...]
    hn_hi = hn.astype(BF16)
    hn_ref[...] = hn_hi
    hn_lo = (hn - hn_hi.astype(F32)).astype(BF16)
    both = _dot(hn_hi, wr_ref[...])
    logits = both[:, 0:LANES] + both[:, LANES:2 * LANES] + _dot(hn_lo, wr_ref[:, 0:LANES]) + br_ref[...]
    lane = lax.broadcasted_iota(jnp.int32, logits.shape, 1)
    lane_f = lane.astype(F32)
    big = float(LANES)
    gl = jnp.where((lane >= C_GROUP) & (lane < C_GROUP + N_GROUPS), logits, -jnp.inf)
    gmax = jnp.max(gl, axis=1, keepdims=True)
    gidx = jnp.min(jnp.where(gl == gmax, lane_f, big), axis=1, keepdims=True) - C_GROUP
    gprob = 1.0 / jnp.sum(jnp.exp(gl - gmax), axis=1, keepdims=True)
    grp_of_lane = (lane // EXPERTS_PER_GROUP).astype(F32)
    el = jnp.where((lane < N_EXPERTS) & (grp_of_lane == gidx), logits, -jnp.inf)
    v1 = jnp.max(el, axis=1, keepdims=True)
    i1 = jnp.min(jnp.where(el == v1, lane_f, big), axis=1, keepdims=True)
    el2 = jnp.where(lane_f == i1, -jnp.inf, el)
    v2 = jnp.max(el2, axis=1, keepdims=True)
    i2 = jnp.min(jnp.where(el2 == v2, lane_f, big), axis=1, keepdims=True)
    e2 = jnp.exp(v2 - v1)
    p1 = 1.0 / (1.0 + e2)
    comb_ref[...] = (gprob * (jnp.where(lane_f == i1, p1, 0.0) + jnp.where(lane_f == i2, e2 * p1, 0.0))
                     + jnp.where(lane == C_GROUP, gidx, 0.0))


def _outproj(x2d, o_a, b_parts, w_out_perm, gain, w_router, b_router):
    n, d = x2d.shape
    row = lambda w: pl.BlockSpec((OUT_TM, w), lambda i: (i, 0))
    full = lambda a: pl.BlockSpec(a.shape, lambda i: (0, 0))
    return pl.pallas_call(
        _outproj_kernel,
        grid=(n // OUT_TM,),
        in_specs=[row(d), row(DA)] + [row(DB)] * 3 + [full(w_out_perm), pl.BlockSpec((1, d), lambda i: (0, 0)),
                                                      full(w_router), full(b_router)],
        out_specs=[row(d), row(d), row(LANES)],
        out_shape=[jax.ShapeDtypeStruct((n, d), F32), jax.ShapeDtypeStruct((n, d), BF16),
                   jax.ShapeDtypeStruct((n, LANES), F32)],
        compiler_params=pltpu.CompilerParams(dimension_semantics=("parallel",), vmem_limit_bytes=VMEM_LIMIT),
        name="outproj_router",
    )(x2d, o_a, *b_parts, w_out_perm, gain.reshape(1, d), w_router, b_router)


MOE_TM = 1024


MOE_SUB = 128
MOE_FINAL_TM = 512


def _moe_kernel(hn_ref, comb_ref, upper_ref, wg_ref, wu_ref, wd_ref, y_ref,
                perm_ref, hs_ref, cs_ref, ys_ref, start_ref, nsub_ref):
    grp = pl.program_id(1)
    tm, d = hn_ref.shape
    n_pad = perm_ref.shape[0]

    @pl.when(grp == 0)
    def _sort():
        comb = comb_ref[...]
        lane_f = lax.broadcasted_iota(jnp.int32, (tm, LANES), 1).astype(F32)
        onehot = jnp.where(lane_f == comb[:, C_GROUP:C_GROUP + 1], 1.0, 0.0).astype(BF16)
        eye8 = jnp.where(lax.broadcasted_iota(jnp.int32, (8, LANES), 0) == lax.broadcasted_iota(jnp.int32, (8, LANES), 1),
                         1.0, 0.0).astype(BF16)
        onehot_t = _nt_dot(eye8, onehot)
        cum_t = _dot(onehot_t.astype(BF16), upper_ref[...])
        start = jnp.zeros((1, 1), F32)
        pos_t = jnp.zeros((1, tm), F32)
        for k in range(N_GROUPS):
            padded = jnp.ceil(cum_t[k:k + 1, tm - 1:tm] * (1.0 / MOE_SUB)) * MOE_SUB
            pos_t = pos_t + onehot_t[k:k + 1, :] * (start + cum_t[k:k + 1, :] - 1.0)
            start_ref[k] = start[0, 0].astype(jnp.int32)
            nsub_ref[k] = (padded[0, 0] * (1.0 / MOE_SUB)).astype(jnp.int32)
            start = start + padded
        perm = jnp.where(lax.broadcasted_iota(jnp.int32, (n_pad, tm), 0) == pos_t.astype(jnp.int32), 1.0, 0.0).astype(BF16)
        perm_ref[...] = perm
        hs_ref[...] = _dot(perm, hn_ref[...]).astype(BF16)
        c_hi = comb.astype(BF16)
        c_lo = (comb - c_hi.astype(F32)).astype(BF16)
        both = _dot(perm, jnp.concatenate([c_hi, c_lo], axis=1))
        cs_ref[...] = both[:, 0:LANES] + both[:, LANES:2 * LANES]
        ys_ref[...] = jnp.zeros_like(ys_ref)

    lane = lax.broadcasted_iota(jnp.int32, (MOE_SUB, LANES), 1)

    def segment(s, carry):
        rows = pl.ds(pl.multiple_of(start_ref[grp] + s * MOE_SUB, MOE_SUB), MOE_SUB)
        x = hs_ref[rows, :]
        weights = cs_ref[rows, :]
        acc = jnp.zeros((MOE_SUB, d), F32)
        for j in range(EXPERTS_PER_GROUP):
            gate = _dot(x, wg_ref[j])
            up = _dot(x, wu_ref[j])
            w = jnp.sum(jnp.where(lane == grp * EXPERTS_PER_GROUP + j, weights, 0.0), axis=1, keepdims=True)
            acc = acc + _dot((gate * jax.nn.sigmoid(gate) * up * w).astype(BF16), wd_ref[j])
        ys_ref[rows, :] = acc.astype(BF16)
        return carry

    lax.fori_loop(0, nsub_ref[grp], segment, 0)

    @pl.when(grp == pl.num_programs(1) - 1)
    def _unsort():
        y_ref[...] = lax.dot_general(perm_ref[...], ys_ref[...], (((0,), (0,)), ((), ())),
                                     preferred_element_type=F32).astype(BF16)


def _final_kernel(h_ref, y_ref, g_ref, out_ref):
    y = h_ref[...] + y_ref[...].astype(F32)
    out_ref[...] = y * lax.rsqrt(jnp.mean(y * y, axis=-1, keepdims=True) + EPS) * g_ref[...]


def _moe(h, hn, comb, w_gate, w_up, w_down, gain):
    n, d = h.shape
    tm = min(MOE_TM, n)
    n_pad = tm + (N_GROUPS - 1) * MOE_SUB
    row = lambda w: pl.BlockSpec((tm, w), lambda i, g: (i, 0))
    group_w = lambda a: pl.BlockSpec((EXPERTS_PER_GROUP,) + a.shape[1:], lambda i, g: (g, 0, 0))
    upper = jnp.asarray(np.triu(np.ones((tm, tm), np.float32)), BF16)
    y = pl.pallas_call(
        _moe_kernel,
        grid=(n // tm, N_GROUPS),
        in_specs=[row(d), row(LANES), pl.BlockSpec((tm, tm), lambda i, g: (0, 0)),
                  group_w(w_gate), group_w(w_up), group_w(w_down)],
        out_specs=row(d),
        out_shape=jax.ShapeDtypeStruct((n, d), BF16),
        scratch_shapes=[pltpu.VMEM((n_pad, tm), BF16), pltpu.VMEM((n_pad, d), BF16), pltpu.VMEM((n_pad, LANES), F32),
                        pltpu.VMEM((n_pad, d), BF16), pltpu.SMEM((N_GROUPS,), jnp.int32), pltpu.SMEM((N_GROUPS,), jnp.int32)],
        compiler_params=pltpu.CompilerParams(dimension_semantics=("parallel", "arbitrary"), vmem_limit_bytes=VMEM_LIMIT),
        name="moe_experts",
    )(hn, comb, upper, w_gate, w_up, w_down)
    rowf = lambda w: pl.BlockSpec((MOE_FINAL_TM, w), lambda i: (i, 0))
    return pl.pallas_call(
        _final_kernel,
        grid=(n // MOE_FINAL_TM,),
        in_specs=[rowf(d), rowf(d), pl.BlockSpec((1, d), lambda i: (0, 0))],
        out_specs=rowf(d),
        out_shape=jax.ShapeDtypeStruct((n, d), F32),
        compiler_params=pltpu.CompilerParams(dimension_semantics=("parallel",), vmem_limit_bytes=VMEM_LIMIT),
        name="residual_final_norm",
    )(h, y, gain.reshape(1, d))


def _permute_w_out(w_out):
    d = w_out.shape[1]
    wb = w_out[DA:].reshape(NSA_KV_HEADS, NSA_GROUP, HEAD_DIM, d).transpose(1, 0, 2, 3).reshape(DB, d)
    return jnp.concatenate([w_out[:DA], wb], axis=0).astype(BF16)


def _router_weights(w_group, b_group, w_expert, b_expert):
    d = w_group.shape[0]
    w = jnp.concatenate([w_expert.reshape(d, N_EXPERTS), w_group], axis=1)
    b = jnp.concatenate([b_expert.reshape(N_EXPERTS), b_group])
    pad = LANES - w.shape[1]
    w = jnp.pad(w, ((0, 0), (0, pad))).astype(F32)
    w_hi = w.astype(BF16)
    w_lo = (w - w_hi.astype(F32)).astype(BF16)
    return jnp.concatenate([w_hi, w_lo], axis=1), jnp.pad(b, (0, pad)).reshape(1, LANES).astype(F32)


def _layer(h, rel_bias, norm_mix, w_in, w_out, cmp_pos_k, cmp_pos_v, cmp_k_w1, cmp_k_w2, cmp_v_w1, cmp_v_w2,
           norm_ffn, w_rg, b_rg, w_re, b_re, w_gate, w_up, w_down, out_gain):
    b, t, d = h.shape
    n = b * t
    assert t % (QB * DIL_PATTERNS[-1][1]) == 0 and t // SEL_BLOCK <= LANES and n % MOE_TM == 0
    x2d = h.reshape(n, d)
    seq = lambda a: a if a.ndim == 3 else a.reshape(b, a.shape[0] // b, a.shape[-1])
    qa, ka, va, qb, kc, vc, ks_t, vs0aug, vs1aug, kw, vw0aug, vw1aug, gates = map(
        seq, _inproj(x2d, norm_mix, _permute_w_in(w_in), t))
    f_a = _bias_1d(rel_bias[:, :DIL_HEADS])
    f_b = _bias_1d(rel_bias[:, DIL_HEADS:]) * LOG2E
    o_a = _mixer_a(qa, ka, va, jnp.stack([_dil_bias(f_a, dil) for _, dil in DIL_PATTERNS]))
    kcmp, vcmp = _compress(kc, vc, cmp_pos_k, cmp_pos_v, cmp_k_w1, cmp_k_w2, cmp_v_w1, cmp_v_w2)
    o_cmp, sel = _compressed_branch(qb, kcmp, vcmp, gates, f_b)
    o_sel = _selected_branch(qb, sel, gates, ks_t, vs0aug, vs1aug, f_b)
    o_win = _window_branch(qb, gates, kw, vw0aug, vw1aug, f_b)
    b_parts = [o.reshape(n, DB) for o in (o_cmp, o_sel, o_win)]
    w_router, b_router = _router_weights(w_rg, b_rg, w_re, b_re)
    h2, hn, comb = _outproj(x2d, o_a.reshape(n, DA), b_parts, _permute_w_out(w_out), norm_ffn, w_router, b_router)
    return _moe(h2, hn, comb, w_gate.astype(BF16), w_up.astype(BF16), w_down.astype(BF16), out_gain)


def kernel(x, rel_bias, norm_mix, w_in, w_out, cmp_pos_k, cmp_pos_v, cmp_k_w1, cmp_k_w2, cmp_v_w1, cmp_v_w2,
           norm_ffn, w_router_group, b_router_group, w_router_expert, b_router_expert, w_gate, w_up, w_down,
           norm_final):
    depth = norm_mix.shape[0]
    assert depth == 1, "the final RMSNorm is fused into the last layer's expert kernel"
    out = _layer(x, rel_bias, norm_mix[0], w_in[0], w_out[0], cmp_pos_k[0], cmp_pos_v[0], cmp_k_w1[0], cmp_k_w2[0],
                 cmp_v_w1[0], cmp_v_w2[0], norm_ffn[0], w_router_group[0], b_router_group[0], w_router_expert[0],
                 b_router_expert[0], w_gate[0], w_up[0], w_down[0], norm_final)
    return out.reshape(x.shape)
```

```python
import functools
import math

import jax
import jax.numpy as jnp
import numpy as np
from jax import lax
from jax.experimental import pallas as pl
from jax.experimental.pallas import tpu as pltpu

HEAD_DIM = 64
DIL_HEADS = 6
NSA_KV_HEADS = 2
NSA_GROUP = 5
NSA_HEADS = NSA_KV_HEADS * NSA_GROUP
N_HEADS = DIL_HEADS + NSA_HEADS
DIL_PATTERNS = ((128, 1), (512, 4), (2048, 16))
CMP_BLOCK = 32
CMP_STRIDE = 16
CMP_HIDDEN = 256
SEL_BLOCK = 64
SEL_TOPK = 16
WIN = 512
N_FORCED = 3
N_BUCKETS = 32
MAX_DISTANCE = 2048
N_GROUPS = 4
EXPERTS_PER_GROUP = 4
N_EXPERTS = N_GROUPS * EXPERTS_PER_GROUP
D_EXPERT = 512
EPS = 1e-6

LANES = 128
QB = 128
NEG = -1.0e30
LOG2E = math.log2(math.e)
DA = DIL_HEADS * HEAD_DIM
DB = NSA_HEADS * HEAD_DIM
N_ROWGROUPS = NSA_HEADS
SEL_KT = 256
VMEM_LIMIT = 56 * 1024 * 1024

F32 = jnp.float32
BF16 = jnp.bfloat16
NT_DIMS = (((1,), (1,)), ((), ()))


def _nt_dot(a, b):
    return lax.dot_general(a, b, NT_DIMS, preferred_element_type=F32)


def _dot(a, b):
    return jnp.dot(a, b, preferred_element_type=F32)


def _bucket_np(dist):
    dist = np.maximum(np.asarray(dist, np.int64), 0)
    max_exact = N_BUCKETS // 2
    x = np.maximum(dist, 1).astype(np.float32) / np.float32(max_exact)
    large = max_exact + (np.log(x) / np.float32(math.log(MAX_DISTANCE / max_exact))
                         * np.float32(N_BUCKETS - max_exact)).astype(np.int32)
    large = np.minimum(large, N_BUCKETS - 1)
    return np.where(dist < max_exact, dist, large).astype(np.int32)


BIAS_LEN = 4096


def _bias_1d(rel_bias_heads):
    onehot = (_bucket_np(np.arange(BIAS_LEN))[None, :] == np.arange(N_BUCKETS)[:, None]).astype(np.float32)
    return jnp.dot(rel_bias_heads.T.astype(F32), jnp.asarray(onehot), precision=lax.Precision.HIGHEST)


def _extend(f, lo, hi):
    assert hi <= f.shape[-1]
    if lo >= 0:
        return f[..., lo:hi]
    pad = jnp.full(f.shape[:-1] + (-lo,), NEG, f.dtype)
    return jnp.concatenate([pad, f[..., :hi]], axis=-1)


def _toeplitz(w, q, c):
    n = q + c - 1
    assert w.shape[-1] == n
    lead = w.shape[:-1]
    wp = jnp.concatenate([w, jnp.zeros(lead + (1,), w.dtype)], axis=-1)
    flat = jnp.broadcast_to(wp[..., None, :], lead + (q, n + 1)).reshape(lead + (q * (n + 1),))
    return flat[..., :q * n].reshape(lead + (q, n))[..., q - 1:q - 1 + c]


def _toeplitz_of(fn_vals, lo, q, c):
    return _toeplitz(jnp.flip(fn_vals, axis=-1), q, c)


IN_TM = 512
C_QA, C_KA, C_VA = 0, DA, 2 * DA
C_QB = 3 * DA
C_KC = C_QB + DB
C_VC, C_KS, C_VS, C_KW, C_VW, C_GT = (C_KC + LANES * i for i in range(1, 7))
N_COLS = C_GT + LANES


def _permute_w_in(w_in):
    scale = 1.0 / math.sqrt(HEAD_DIM)
    sizes = [DA] * 3 + [DB] + [NSA_KV_HEADS * HEAD_DIM] * 6 + [3 * NSA_HEADS]
    offs = np.concatenate([[0], np.cumsum(sizes)])
    part = lambda i: w_in[:, offs[i]:offs[i + 1]]
    d = w_in.shape[0]
    qb = part(3).reshape(d, NSA_KV_HEADS, NSA_GROUP, HEAD_DIM).transpose(0, 2, 1, 3).reshape(d, DB)
    gt = part(10).reshape(d, NSA_KV_HEADS, NSA_GROUP, 3).transpose(0, 3, 2, 1).reshape(d, 3 * NSA_HEADS)
    gt = jnp.pad(gt, ((0, 0), (0, LANES - 3 * NSA_HEADS)))
    cols = [part(0) * scale, part(1), part(2), qb * (scale * LOG2E)] + [part(i) for i in range(4, 10)] + [gt]
    return jnp.concatenate(cols, axis=1).astype(BF16)


def _inproj_kernel(seq_len, x_ref, g_ref, w_ref, qa_ref, ka_ref, va_ref, qb_ref, kc_ref, vc_ref,
                   kst_ref, vs0_ref, vs1_ref, kw_ref, vw0_ref, vw1_ref, gates_ref, stage_ref):
    x = x_ref[...]
    xn = (x * lax.rsqrt(jnp.mean(x * x, axis=-1, keepdims=True) + EPS) * g_ref[...]).astype(BF16)
    seg = lambda a, n: _dot(xn, w_ref[:, a:a + n])
    qa_ref[...] = seg(C_QA, DA).astype(BF16)
    ka_ref[...] = seg(C_KA, DA).astype(BF16)
    va_ref[...] = seg(C_VA, DA).astype(BF16)
    qb_ref[...] = seg(C_QB, DB).astype(BF16)
    kw_ref[...] = seg(C_KW, LANES).astype(BF16)
    tm = x.shape[0]
    for col, out_ref in ((C_KC, kc_ref), (C_VC, vc_ref)):
        stage_ref[...] = seg(col, LANES)
        for j in range(CMP_STRIDE):
            out_ref[:, j * LANES:(j + 1) * LANES] = stage_ref[pl.ds(j, tm // CMP_STRIDE, stride=CMP_STRIDE), :].astype(BF16)
    tok_t = (pl.program_id(0) * tm) % seq_len + lax.broadcasted_iota(jnp.int32, (LANES, tm), 1)
    blk_t = lax.broadcasted_iota(jnp.int32, (LANES, tm), 0)
    stage_ref[...] = seg(C_KS, LANES)
    kst_ref[0, 0:LANES, :] = stage_ref[...].T.astype(BF16)
    kst_ref[0, LANES:2 * LANES, :] = jnp.where(blk_t == tok_t // SEL_BLOCK, 1.0, 0.0).astype(BF16)
    low = lax.broadcasted_iota(jnp.int32, (tm, LANES), 1) < HEAD_DIM
    for col, ref0, ref1 in ((C_VS, vs0_ref, vs1_ref), (C_VW, vw0_ref, vw1_ref)):
        v = seg(col, LANES)
        ref0[...] = jnp.where(low, v, 1.0).astype(BF16)
        ref1[...] = jnp.where(low, 1.0, v).astype(BF16)
    gates_ref[...] = jax.nn.sigmoid(seg(C_GT, LANES))


def _inproj(x2d, gain, w_perm, seq_len):
    n, d = x2d.shape
    row = lambda w: pl.BlockSpec((IN_TM, w), lambda i: (i, 0))
    rows = lambda w: (jax.ShapeDtypeStruct((n, w), BF16), row(w))
    chunks = (jax.ShapeDtypeStruct((n // CMP_STRIDE, CMP_STRIDE * LANES), BF16),
              pl.BlockSpec((IN_TM // CMP_STRIDE, CMP_STRIDE * LANES), lambda i: (i, 0)))
    per_seq = seq_len // IN_TM
    keys_t = (jax.ShapeDtypeStruct((n // seq_len, 2 * LANES, seq_len), BF16),
              pl.BlockSpec((1, 2 * LANES, IN_TM), lambda i: (i // per_seq, 0, i % per_seq)))
    outs = [rows(DA), rows(DA), rows(DA), rows(DB), chunks, chunks, keys_t] + [rows(LANES)] * 5
    outs.append((jax.ShapeDtypeStruct((n, LANES), F32), row(LANES)))
    return pl.pallas_call(
        functools.partial(_inproj_kernel, seq_len),
        grid=(n // IN_TM,),
        in_specs=[row(d), pl.BlockSpec((1, d), lambda i: (0, 0)), pl.BlockSpec((d, N_COLS), lambda i: (0, 0))],
        out_specs=[spec for _, spec in outs],
        out_shape=[shape for shape, _ in outs],
        scratch_shapes=[pltpu.VMEM((IN_TM, LANES), F32)],
        compiler_params=pltpu.CompilerParams(dimension_semantics=("parallel",), vmem_limit_bytes=VMEM_LIMIT),
        name="inproj",
    )(x2d, gain.reshape(1, d), w_perm)


def _embed_pair(w, n_tok):
    c = w.shape[1]
    w4 = w.reshape(n_tok, 1, HEAD_DIM, 1, c) * jnp.eye(NSA_KV_HEADS, dtype=w.dtype).reshape(1, 2, 1, 2, 1)
    return w4.reshape(n_tok * 2 * HEAD_DIM, 2 * c)


def _gelu_tanh(x):
    return 0.5 * x * (1.0 + jnp.tanh(math.sqrt(2.0 / math.pi) * (x + 0.044715 * (x * x * x))))


def _compress_kernel(ck_ref, cv_ref, posk_ref, posv_ref, wk1a, wk1b, wk2, wv1a, wv1b, wv2,
                     kout_ref, vout_ref, shift_ref):
    ncp = ck_ref.shape[1]
    for c_ref, pos_ref, w1a, w1b, w2, out_ref in ((ck_ref, posk_ref, wk1a, wk1b, wk2, kout_ref),
                                                  (cv_ref, posv_ref, wv1a, wv1b, wv2, vout_ref)):
        c = c_ref[0].astype(F32)
        first = _dot((c + pos_ref[0:1, :]).astype(BF16), w1a[...])
        second = _dot((c + pos_ref[1:2, :]).astype(BF16), w1b[...])
        shift_ref[0:ncp, :] = second
        shift_ref[ncp:ncp + 8, :] = jnp.zeros((8, second.shape[1]), F32)
        hidden = _gelu_tanh(first + shift_ref[1:ncp + 1, :])
        out_ref[0] = _dot(hidden.astype(BF16), w2[...]).astype(BF16)


def _compress(kc, vc, pos_k, pos_v, k_w1, k_w2, v_w1, v_w2):
    b, ncp, wide = kc.shape
    half = CMP_STRIDE * HEAD_DIM

    def prep(w1, w2, pos):
        pos_pair = jnp.broadcast_to(pos.reshape(2, CMP_STRIDE, 1, HEAD_DIM), (2, CMP_STRIDE, 2, HEAD_DIM))
        return (_embed_pair(w1[:half], CMP_STRIDE).astype(BF16), _embed_pair(w1[half:], CMP_STRIDE).astype(BF16),
                jnp.kron(jnp.eye(NSA_KV_HEADS, dtype=w2.dtype), w2).astype(BF16),
                pos_pair.reshape(2, wide).astype(F32))

    wk1a, wk1b, wk2, posk = prep(k_w1, k_w2, pos_k)
    wv1a, wv1b, wv2, posv = prep(v_w1, v_w2, pos_v)
    full = lambda a: pl.BlockSpec(a.shape, lambda i: (0,) * a.ndim)
    tok = pl.BlockSpec((1, ncp, wide), lambda i: (i, 0, 0))
    out = pl.BlockSpec((1, ncp, LANES), lambda i: (i, 0, 0))
    return pl.pallas_call(
        _compress_kernel,
        grid=(b,),
        in_specs=[tok, tok, full(posk), full(posv), full(wk1a), full(wk1b), full(wk2), full(wv1a), full(wv1b), full(wv2)],
        out_specs=[out, out],
        out_shape=[jax.ShapeDtypeStruct((b, ncp, LANES), BF16)] * 2,
        scratch_shapes=[pltpu.VMEM((ncp + 8, 2 * CMP_HIDDEN), F32)],
        compiler_params=pltpu.CompilerParams(dimension_semantics=("parallel",), vmem_limit_bytes=VMEM_LIMIT),
        name="compress",
    )(kc, vc, posk, posv, wk1a, wk1b, wk2, wv1a, wv1b, wv2)


def _pair_masks(rows):
    lane = lax.broadcasted_iota(jnp.int32, (rows, LANES), 1)
    return lane < HEAD_DIM


A_SUPER = QB * DIL_PATTERNS[-1][1]


def _mixer_a_kernel(q_ref, kp_ref, kc_ref, vp_ref, vc_ref, bias_ref, out_ref, qf_ref, kf_ref, vf_ref, o_ref, lse_ref):
    first = pl.program_id(1) == 0
    n_pairs = DIL_HEADS // 2
    for p in range(n_pairs):
        cs = slice(p * LANES, (p + 1) * LANES)
        qf_ref[p] = q_ref[0, :, cs].astype(F32)
        kf_ref[p, 0:A_SUPER, :] = kp_ref[0, :, cs].astype(F32)
        kf_ref[p, A_SUPER:2 * A_SUPER, :] = kc_ref[0, :, cs].astype(F32)
        vf_ref[p, 0:A_SUPER, :] = vp_ref[0, :, cs].astype(F32)
        vf_ref[p, A_SUPER:2 * A_SUPER, :] = vc_ref[0, :, cs].astype(F32)
    low = _pair_masks(QB)
    in_prev = lax.broadcasted_iota(jnp.int32, (QB, 2 * QB), 1) < QB
    zero = jnp.zeros((QB, LANES), BF16)

    def chunk(idx, dil, q_base, k_base, q_span, k_span, off, at_start):
        rows = lambda size: pl.ds(off, size) if dil == 1 else pl.ds(off, size, stride=dil)
        q_win = lambda ref: ref.at[pl.ds(pl.multiple_of(q_base, 8), q_span), :]
        k_win = lambda ref: ref.at[pl.ds(pl.multiple_of(k_base, 8), k_span), :]
        q_rows = rows(QB)
        prev_mask = jnp.where(jnp.logical_and(in_prev, jnp.logical_and(first, at_start)), NEG, 0.0)
        for p in range(n_pairs):
            q = q_win(qf_ref.at[p])[q_rows, :].astype(BF16)
            keys = k_win(kf_ref.at[p])[rows(2 * QB), :].astype(BF16)
            vals = k_win(vf_ref.at[p])[rows(2 * QB), :].astype(BF16)
            o_win, lse_win = q_win(o_ref.at[p]), q_win(lse_ref.at[p])
            lhs = jnp.concatenate([jnp.where(low, q, zero), jnp.where(low, zero, q)], axis=0)
            bias = jnp.concatenate([bias_ref[idx, 2 * p] + prev_mask, bias_ref[idx, 2 * p + 1] + prev_mask], axis=0)
            s = _nt_dot(lhs, keys) + bias
            m = jnp.max(s, axis=1, keepdims=True)
            e = jnp.exp(s - m)
            l = jnp.sum(e, axis=1, keepdims=True)
            pv = _dot(e.astype(BF16), vals) * (1.0 / l)
            lse = m + jnp.log(l)
            o_new = jnp.where(low, pv[:QB], pv[QB:])
            l_new = jnp.where(low, jnp.broadcast_to(lse[:QB], (QB, LANES)), jnp.broadcast_to(lse[QB:], (QB, LANES)))
            if idx > 0:
                o_old, l_old = o_win[q_rows, :], lse_win[q_rows, :]
                mx = jnp.maximum(l_old, l_new)
                w_old, w_new = jnp.exp(l_old - mx), jnp.exp(l_new - mx)
                tot = w_old + w_new
                o_new = (w_old * o_old + w_new * o_new) * (1.0 / tot)
                l_new = mx + jnp.log(tot)
            o_win[q_rows, :] = o_new
            lse_win[q_rows, :] = l_new

    def loop(n, body):
        lax.fori_loop(0, n, lambda i, carry: (body(i), carry)[1], 0, unroll=min(n, 4))

    for idx, (_, dil) in enumerate(DIL_PATTERNS):
        span = QB * dil
        n_chunks = A_SUPER // span
        if n_chunks > 1:
            for r in range(dil):
                loop(n_chunks, lambda c, idx=idx, dil=dil, span=span, r=r:
                     chunk(idx, dil, span * c, A_SUPER + span * (c - 1), span, 2 * span, r, c == 0))
        else:
            for r in range(8):
                loop(dil // 8, lambda hi, idx=idx, dil=dil, span=span, r=r:
                     chunk(idx, dil, 8 * hi, 8 * hi, span - 8, 2 * span - 8, r, True))
    for p in range(n_pairs):
        out_ref[0, :, p * LANES:(p + 1) * LANES] = o_ref[p].astype(BF16)


def _mixer_a(qa, ka, va, bias):
    b, t, _ = qa.shape
    cur = pl.BlockSpec((1, A_SUPER, DA), lambda bi, i: (bi, i, 0))
    prev = pl.BlockSpec((1, A_SUPER, DA), lambda bi, i: (bi, jnp.maximum(i - 1, 0), 0))
    return pl.pallas_call(
        _mixer_a_kernel,
        grid=(b, t // A_SUPER),
        in_specs=[cur, prev, cur, prev, cur, pl.BlockSpec(bias.shape, lambda bi, i: (0, 0, 0, 0))],
        out_specs=cur,
        out_shape=jax.ShapeDtypeStruct((b, t, DA), BF16),
        scratch_shapes=[pltpu.VMEM((DIL_HEADS // 2, rows, LANES), F32) for rows in (A_SUPER, 2 * A_SUPER, 2 * A_SUPER, A_SUPER, A_SUPER)],
        compiler_params=pltpu.CompilerParams(dimension_semantics=("parallel", "parallel"), vmem_limit_bytes=VMEM_LIMIT),
        name="mixer_a",
    )(qa, ka, ka, va, va, bias)


def _dil_bias(f_a, dil):
    steps = DIL_PATTERNS[0][0]
    g = f_a[:, 0:dil * steps + 1:dil]
    lo, hi = QB - (2 * QB - 1), QB + QB
    vals = jnp.concatenate([jnp.full((DIL_HEADS, -lo), NEG, F32), g, jnp.full((DIL_HEADS, hi - steps - 1), NEG, F32)], axis=1)
    return _toeplitz_of(vals, lo, QB, 2 * QB)


CMP_TILE_KEYS = LANES
CMP_TILE_SPAN = CMP_TILE_KEYS * CMP_STRIDE // QB
CMP_CONST_DELTA = 28


def _cmp_bias(f_b):
    per = QB // CMP_STRIDE
    n_rows = per * (CMP_CONST_DELTA + 1)
    m_lo, m_hi = -(CMP_TILE_KEYS - 1), n_rows
    f_b = f_b.astype(BF16)
    base = _extend(f_b, CMP_STRIDE * m_lo - (CMP_BLOCK - 1), CMP_STRIDE * m_hi - (CMP_BLOCK - 1))
    g = base.reshape(NSA_HEADS, m_hi - m_lo, CMP_STRIDE).transpose(0, 2, 1)
    t = _toeplitz_of(g, m_lo, n_rows, CMP_TILE_KEYS)
    t = t.reshape(NSA_HEADS, CMP_STRIDE, CMP_CONST_DELTA + 1, per, CMP_TILE_KEYS).transpose(2, 0, 3, 1, 4)
    t = t.reshape(CMP_CONST_DELTA + 1, NSA_HEADS, QB, CMP_TILE_KEYS)
    return jnp.concatenate([jnp.full((1,) + t.shape[1:], NEG, t.dtype), t], axis=0)


def _overlap_matrix_t(ncp, n_sel_pad):
    n = np.arange(ncp)[None, :] * CMP_STRIDE
    s = np.arange(n_sel_pad)[:, None] * SEL_BLOCK
    ov = np.clip(np.minimum(n + CMP_BLOCK, s + SEL_BLOCK) - np.maximum(n, s), 0, None) / CMP_BLOCK
    return jnp.asarray(ov, BF16)


def _gate_tile(gates_ref, branch, g, row0=0):
    c = branch * NSA_HEADS + g * 2
    low = _pair_masks(QB)
    return jnp.where(low, jnp.broadcast_to(gates_ref[0, row0:row0 + QB, c:c + 1], (QB, LANES)),
                     jnp.broadcast_to(gates_ref[0, row0:row0 + QB, c + 1:c + 2], (QB, LANES)))


def _masked_q(qb_ref, g, kv, row0=0):
    q = qb_ref[0, row0:row0 + QB, g * LANES:(g + 1) * LANES]
    low = _pair_masks(QB)
    keep = low if kv == 0 else jnp.logical_not(low)
    return jnp.where(keep, q, jnp.zeros_like(q))


def _cmp_kernel(n_tiles, qb_ref, kcmp_ref, vcmp_ref, gates_ref, ov_ref, *rest):
    tbl_refs, (oc_ref, sel_ref, q_ref, s_ref, p_ref, pv_ref) = rest[:n_tiles], rest[n_tiles:]
    qblk = pl.program_id(1)
    t0 = qblk * QB
    low = _pair_masks(QB)
    for kv in range(NSA_KV_HEADS):
        for g in range(NSA_GROUP):
            r = kv * NSA_GROUP + g
            q_ref[r * QB:(r + 1) * QB, :] = _masked_q(qb_ref, g, kv)

    def attend(n_vis):
        kc = n_vis * CMP_TILE_KEYS
        n_blk = n_vis * CMP_TILE_SPAN * QB // SEL_BLOCK
        s_ref[:, 0:kc] = _nt_dot(q_ref[...], kcmp_ref[0, 0:kc, :])
        blk = lax.broadcasted_iota(jnp.int32, (n_blk, QB), 0)
        cur = (t0 + lax.broadcasted_iota(jnp.int32, (n_blk, QB), 1)) // SEL_BLOCK
        blk_f = blk.astype(F32)
        forced = (blk == cur) | (blk == cur - 1) | (blk == 0)
        causal = blk <= cur
        scores = []
        for kv in range(NSA_KV_HEADS):
            psum = jnp.zeros((QB, kc), F32)
            for g in range(NSA_GROUP):
                r = kv * NSA_GROUP + g
                rows = slice(r * QB, (r + 1) * QB)
                s = s_ref[rows, 0:kc] + jnp.concatenate([tbl_refs[c][0, r].astype(F32) for c in range(n_vis)], axis=1)
                m = jnp.max(s, axis=1, keepdims=True)
                e = jnp.exp2(s - m)
                den = jnp.sum(e, axis=1, keepdims=True)
                p = e * jnp.where(m > 0.5 * NEG, 1.0 / den, 0.0)
                psum = psum + p
                p_ref[rows, 0:kc] = p.astype(BF16)
            hi = psum.astype(BF16)
            lo = (psum - hi.astype(F32)).astype(BF16)
            ov_t = ov_ref[0:n_blk, 0:kc]
            imp_t = _nt_dot(ov_t, hi) + _nt_dot(ov_t, lo)
            scores.append(jnp.where(forced, -jnp.inf, jnp.where(causal, imp_t, -1.0)))
        pv_ref[...] = _dot(p_ref[:, 0:kc], vcmp_ref[0, 0:kc, :])

        def pick(_, carry):
            new = []
            for val, sel in carry:
                mx = jnp.max(val, axis=0, keepdims=True)
                idx = jnp.min(jnp.where(val == mx, blk_f, float(n_blk)), axis=0, keepdims=True)
                hit = blk_f == idx
                new.append((jnp.where(hit, -jnp.inf, val), jnp.where(hit, 1.0, sel)))
            return tuple(new)

        taken = jnp.where(forced, 1.0, 0.0)
        picked = lax.fori_loop(0, SEL_TOPK - N_FORCED, pick, tuple((v, taken) for v in scores))
        eye = jnp.where(lax.broadcasted_iota(jnp.int32, (QB, QB), 0) == lax.broadcasted_iota(jnp.int32, (QB, QB), 1),
                        1.0, 0.0).astype(BF16)
        for kv in range(NSA_KV_HEADS):
            sel_t = jnp.where(causal, picked[kv][1], 0.0).astype(BF16)
            sel_ref[0, kv, :, 0:n_blk] = _nt_dot(eye, sel_t).astype(BF16)
            if n_blk < LANES:
                sel_ref[0, kv, :, n_blk:LANES] = jnp.zeros((QB, LANES - n_blk), BF16)

    n_vis = qblk // CMP_TILE_SPAN + 1
    for w in range(1, n_tiles + 1):
        pl.when(n_vis == w)(functools.partial(attend, w))

    for g in range(NSA_GROUP):
        o0, o1 = pv_ref[g * QB:(g + 1) * QB, :], pv_ref[(NSA_GROUP + g) * QB:(NSA_GROUP + g + 1) * QB, :]
        oc_ref[0, :, g * LANES:(g + 1) * LANES] = (jnp.where(low, o0, o1) * _gate_tile(gates_ref, 0, g)).astype(BF16)


def _compressed_branch(qb, kcmp, vcmp, gates, f_b):
    b, t, _ = qb.shape
    ncp = kcmp.shape[1]
    n_tiles = ncp // CMP_TILE_KEYS
    tbl = _cmp_bias(f_b)
    ov = _overlap_matrix_t(ncp, LANES)

    def tbl_spec(c):
        return pl.BlockSpec((1, NSA_HEADS, QB, CMP_TILE_KEYS),
                            lambda bi, i: (jnp.clip(i - CMP_TILE_SPAN * c, -1, CMP_CONST_DELTA) + 1, 0, 0, 0))

    blockq = lambda w: pl.BlockSpec((1, QB, w), lambda bi, i: (bi, i, 0))
    batch = lambda a: pl.BlockSpec((1,) + a.shape[1:], lambda bi, i: (bi, 0, 0))
    return pl.pallas_call(
        functools.partial(_cmp_kernel, n_tiles),
        grid=(b, t // QB),
        in_specs=[blockq(DB), batch(kcmp), batch(vcmp), blockq(LANES), pl.BlockSpec(ov.shape, lambda bi, i: (0, 0))]
                 + [tbl_spec(c) for c in range(n_tiles)],
        out_specs=[blockq(DB), pl.BlockSpec((1, NSA_KV_HEADS, QB, LANES), lambda bi, i: (bi, 0, i, 0))],
        out_shape=[jax.ShapeDtypeStruct((b, t, DB), BF16), jax.ShapeDtypeStruct((b, NSA_KV_HEADS, t, LANES), BF16)],
        scratch_shapes=[pltpu.VMEM((N_ROWGROUPS * QB, LANES), BF16), pltpu.VMEM((N_ROWGROUPS * QB, ncp), F32),
                        pltpu.VMEM((N_ROWGROUPS * QB, ncp), BF16), pltpu.VMEM((N_ROWGROUPS * QB, LANES), F32)],
        compiler_params=pltpu.CompilerParams(dimension_semantics=("parallel", "parallel"), vmem_limit_bytes=VMEM_LIMIT),
        name="nsa_compressed",
    )(qb, kcmp, vcmp, gates, ov, *([tbl] * n_tiles))


SEL_NEAR = 13
SEL_FAR_BLOCK, SEL_NEAR_BLOCK = 16, 8


def _sel_bias(f_b):
    n_off = SEL_NEAR + 1
    cols = QB * n_off
    rel = f_b - f_b[:, BIAS_LEN - 1:]
    lo = -(QB - 1)
    big = _toeplitz_of(_extend(rel, lo, lo + QB + cols - 1), lo, QB, cols)
    tiles = jnp.flip(big.reshape(NSA_HEADS, QB, n_off, QB).transpose(2, 0, 1, 3), axis=0)
    return jnp.concatenate([jnp.zeros((1,) + tiles.shape[1:], F32), tiles], axis=0)


def _pair_ratio(acc0, acc1):
    low = _pair_masks(acc0.shape[0])
    den = pltpu.roll(jnp.where(low, acc1, acc0), HEAD_DIM, axis=1)
    return jnp.where(low, acc0, acc1) * (1.0 / den)


def _sel_kernel(qb_ref, sel_ref, gates_ref, cfar_ref, ks_ref, vs0_ref, vs1_ref, tbl_ref, out_ref,
                qaug_ref, s_ref, s1_ref, acc_ref, m_ref, alpha_ref):
    qblk = pl.program_id(1)
    for kv in range(NSA_KV_HEADS):
        unchosen = jnp.where(sel_ref[0, kv].astype(F32) > 0.0, 0.0, NEG)
        for g in range(NSA_GROUP):
            r = kv * NSA_GROUP + g
            rows = slice(r * QB, (r + 1) * QB)
            qaug_ref[rows, 0:LANES] = _masked_q(qb_ref, g, kv)
            qaug_ref[rows, LANES:2 * LANES] = (unchosen + cfar_ref[r:r + 1, :]).astype(BF16)
    acc_ref[...] = jnp.zeros_like(acc_ref)
    m_ref[...] = jnp.full_like(m_ref, NEG)

    last_tile = ks_ref.shape[2] // SEL_KT - 1

    def scores(j, dst_ref):
        start = pl.multiple_of(jnp.minimum(j, last_tile) * SEL_KT, SEL_KT)
        dst_ref[...] = _dot(qaug_ref[...], ks_ref[0, :, pl.ds(start, SEL_KT)])

    def consume(j, src_ref, near):
        start = pl.multiple_of(j * SEL_KT, SEL_KT)
        for r in range(N_ROWGROUPS):
            rows = slice(r * QB, (r + 1) * QB)
            s = src_ref[rows, :]
            if near:
                e1 = jnp.clip(qblk - 2 * j + 1, 0, SEL_NEAR + 1)
                e2 = jnp.clip(qblk - 2 * j, 0, SEL_NEAR + 1)
                s = s + jnp.concatenate([tbl_ref[e1, r], tbl_ref[e2, r]], axis=1)
                src_ref[rows, :] = s
            m_old = m_ref[rows, :]
            m_new = jnp.maximum(m_old, jnp.max(s, axis=1, keepdims=True))
            alpha_ref[rows, :] = jnp.exp2(m_old - m_new)
            m_ref[rows, :] = m_new
        for r in range(N_ROWGROUPS):
            rows = slice(r * QB, (r + 1) * QB)
            vals = (vs0_ref if r < NSA_GROUP else vs1_ref)[0, pl.ds(start, SEL_KT), :]
            m_new = m_ref[rows, :]
            p = jnp.exp2(src_ref[rows, :] - jnp.concatenate([m_new, m_new], axis=1))
            acc_ref[rows, :] = alpha_ref[rows, :] * acc_ref[rows, :] + _dot(p.astype(BF16), vals)

    def tile_run(first, count, near):
        bufs = (s_ref, s1_ref)
        for u in range(count):
            scores(first + u + 1, bufs[(u + 1) % 2])
            consume(first + u, bufs[u % 2], near)

    def run_pairs(first, pairs, near, max_block):
        tiles = 2 * pairs
        lax.fori_loop(0, tiles // max_block, lambda i, c: (tile_run(first + max_block * i, max_block, near), c)[1], 0)
        done = tiles // max_block * max_block
        size = max_block // 2
        while size >= 2:
            start = first + done
            pl.when((tiles - done) >= size)(functools.partial(tile_run, start, size, near))
            done = done + jnp.where((tiles - done) >= size, size, 0)
            size //= 2

    n_pairs = ((qblk + 2) // 2 + 1) // 2
    n_far = jnp.maximum((qblk - (SEL_NEAR - 1)) // 2, 0) // 2
    scores(0, s_ref)
    run_pairs(0, n_far, False, SEL_FAR_BLOCK)
    run_pairs(2 * n_far, n_pairs - n_far, True, SEL_NEAR_BLOCK)
    for g in range(NSA_GROUP):
        ratio = _pair_ratio(acc_ref[g * QB:(g + 1) * QB, :], acc_ref[(NSA_GROUP + g) * QB:(NSA_GROUP + g + 1) * QB, :])
        out_ref[0, :, g * LANES:(g + 1) * LANES] = (ratio * _gate_tile(gates_ref, 1, g)).astype(BF16)


def _selected_branch(qb, sel, gates, ks_t, vs0aug, vs1aug, f_b):
    b, t, _ = qb.shape
    tbl = _sel_bias(f_b)
    cfar = jnp.broadcast_to(f_b[:, BIAS_LEN - 1:], (NSA_HEADS, LANES))
    cfar = jnp.pad(cfar, ((0, 16 - NSA_HEADS), (0, 0)))
    blockq = lambda w: pl.BlockSpec((1, QB, w), lambda bi, i: (bi, i, 0))
    batch = lambda a: pl.BlockSpec((1,) + a.shape[1:], lambda bi, i: (bi, 0, 0))
    rows = N_ROWGROUPS * QB
    return pl.pallas_call(
        _sel_kernel,
        grid=(b, t // QB),
        in_specs=[blockq(DB), pl.BlockSpec((1, NSA_KV_HEADS, QB, LANES), lambda bi, i: (bi, 0, i, 0)), blockq(LANES),
                  pl.BlockSpec(cfar.shape, lambda bi, i: (0, 0)), batch(ks_t), batch(vs0aug), batch(vs1aug),
                  pl.BlockSpec(tbl.shape, lambda bi, i: (0, 0, 0, 0))],
        out_specs=blockq(DB),
        out_shape=jax.ShapeDtypeStruct((b, t, DB), BF16),
        scratch_shapes=[pltpu.VMEM((rows, 2 * LANES), BF16), pltpu.VMEM((rows, SEL_KT), F32), pltpu.VMEM((rows, SEL_KT), F32),
                        pltpu.VMEM((rows, LANES), F32), pltpu.VMEM((rows, LANES), F32), pltpu.VMEM((rows, LANES), F32)],
        compiler_params=pltpu.CompilerParams(dimension_semantics=("parallel", "parallel"), vmem_limit_bytes=VMEM_LIMIT),
        name="nsa_selected",
    )(qb, sel, gates, cfar, ks_t, vs0aug, vs1aug, tbl)


WIN_KEYS = WIN + QB


def _win_bias(f_b):
    lo = WIN - (WIN_KEYS - 1)
    vals = _extend(f_b[:, :WIN], lo, WIN)
    vals = jnp.concatenate([vals, jnp.full((NSA_HEADS, lo + QB + WIN_KEYS - 1 - WIN), NEG, F32)], axis=1)
    return _toeplitz_of(vals, lo, QB, WIN_KEYS)


WIN_QBLOCKS = 2


def _win_kernel(qb_ref, gates_ref, kw_ref, vw0_ref, vw1_ref, tbl_ref, out_ref, q_ref, s_ref):
    for h in range(WIN_QBLOCKS):
        qblk = pl.program_id(1) * WIN_QBLOCKS + h
        row0 = h * QB
        for kv in range(NSA_KV_HEADS):
            for g in range(NSA_GROUP):
                r = kv * NSA_GROUP + g
                q_ref[h, r * QB:(r + 1) * QB, :] = _masked_q(qb_ref, g, kv, row0)
        first_key = jnp.maximum(qblk * QB - WIN, 0)
        start = pl.multiple_of(first_key, QB)
        shift = pl.multiple_of(WIN - (qblk * QB - first_key), QB)
        s_ref[h] = _nt_dot(q_ref[h], kw_ref[0, pl.ds(start, WIN_KEYS), :])
        outs = []
        for r in range(N_ROWGROUPS):
            vals = (vw0_ref if r < NSA_GROUP else vw1_ref)[0, pl.ds(start, WIN_KEYS), :]
            s = s_ref[h, r * QB:(r + 1) * QB, :] + tbl_ref[r, :, pl.ds(shift, WIN_KEYS)]
            e = jnp.exp2(s - jnp.max(s, axis=1, keepdims=True))
            outs.append(_dot(e.astype(BF16), vals))
        for g in range(NSA_GROUP):
            out_ref[0, row0:row0 + QB, g * LANES:(g + 1) * LANES] = (
                _pair_ratio(outs[g], outs[NSA_GROUP + g]) * _gate_tile(gates_ref, 2, g, row0)).astype(BF16)


def _window_branch(qb, gates, kw, vw0aug, vw1aug, f_b):
    b, t, _ = qb.shape
    assert t >= WIN_KEYS
    tbl = _win_bias(f_b)
    tbl = jnp.concatenate([tbl, jnp.full(tbl.shape[:2] + (WIN,), NEG, F32)], axis=2)
    blockq = lambda w: pl.BlockSpec((1, WIN_QBLOCKS * QB, w), lambda bi, i: (bi, i, 0))
    batch = lambda a: pl.BlockSpec((1,) + a.shape[1:], lambda bi, i: (bi, 0, 0))
    rows = N_ROWGROUPS * QB
    return pl.pallas_call(
        _win_kernel,
        grid=(b, t // (WIN_QBLOCKS * QB)),
        in_specs=[blockq(DB), blockq(LANES), batch(kw), batch(vw0aug), batch(vw1aug),
                  pl.BlockSpec(tbl.shape, lambda bi, i: (0, 0, 0))],
        out_specs=blockq(DB),
        out_shape=jax.ShapeDtypeStruct((b, t, DB), BF16),
        scratch_shapes=[pltpu.VMEM((WIN_QBLOCKS, rows, LANES), BF16), pltpu.VMEM((WIN_QBLOCKS, rows, WIN_KEYS), F32)],
        compiler_params=pltpu.CompilerParams(dimension_semantics=("parallel", "parallel"), vmem_limit_bytes=VMEM_LIMIT),
        name="nsa_window",
    )(qb, gates, kw, vw0aug, vw1aug, tbl)


OUT_TM = 512
C_GROUP = N_EXPERTS


def _outproj_kernel(x_ref, oa_ref, oc_ref, os_ref, ow_ref,
                    wout_ref, g_ref, wr_ref, br_ref, h_ref, hn_ref, comb_ref):
    ob = oc_ref[...].astype(F32) + os_ref[...].astype(F32) + ow_ref[...].astype(F32)
    y = _dot(oa_ref[...], wout_ref[0:DA, :]) + _dot(ob.astype(BF16), wout_ref[DA:DA + DB, :])
    h = x_ref[...] + y
    h_ref[...] = h
    hn = h * lax.rsqrt(jnp.mean(h * h, axis=-1, keepdims=True) + EPS) * g_ref[...]
    hn_hi = hn.astype(BF16)
    hn_ref[...] = hn_hi
    hn_lo = (hn - hn_hi.astype(F32)).astype(BF16)
    both = _dot(hn_hi, wr_ref[...])
    logits = both[:, 0:LANES] + both[:, LANES:2 * LANES] + _dot(hn_lo, wr_ref[:, 0:LANES]) + br_ref[...]
    lane = lax.broadcasted_iota(jnp.int32, logits.shape, 1)
    lane_f = lane.astype(F32)
    big = float(LANES)
    gl = jnp.where((lane >= C_GROUP) & (lane < C_GROUP + N_GROUPS), logits, -jnp.inf)
    gmax = jnp.max(gl, axis=1, keepdims=True)
    gidx = jnp.min(jnp.where(gl == gmax, lane_f, big), axis=1, keepdims=True) - C_GROUP
    gprob = 1.0 / jnp.sum(jnp.exp(gl - gmax), axis=1, keepdims=True)
    grp_of_lane = (lane // EXPERTS_PER_GROUP).astype(F32)
    el = jnp.where((lane < N_EXPERTS) & (grp_of_lane == gidx), logits, -jnp.inf)
    v1 = jnp.max(el, axis=1, keepdims=True)
    i1 = jnp.min(jnp.where(el == v1, lane_f, big), axis=1, keepdims=True)
    el2 = jnp.where(lane_f == i1, -jnp.inf, el)
    v2 = jnp.max(el2, axis=1, keepdims=True)
    i2 = jnp.min(jnp.where(el2 == v2, lane_f, big), axis=1, keepdims=True)
    e2 = jnp.exp(v2 - v1)
    p1 = 1.0 / (1.0 + e2)
    comb_ref[...] = (gprob * (jnp.where(lane_f == i1, p1, 0.0) + jnp.where(lane_f == i2, e2 * p1, 0.0))
                     + jnp.where(lane == C_GROUP, gidx, 0.0))


def _outproj(x2d, o_a, b_parts, w_out_perm, gain, w_router, b_router):
    n, d = x2d.shape
    row = lambda w: pl.BlockSpec((OUT_TM, w), lambda i: (i, 0))
    full = lambda a: pl.BlockSpec(a.shape, lambda i: (0, 0))
    return pl.pallas_call(
        _outproj_kernel,
        grid=(n // OUT_TM,),
        in_specs=[row(d), row(DA)] + [row(DB)] * 3 + [full(w_out_perm), pl.BlockSpec((1, d), lambda i: (0, 0)),
                                                      full(w_router), full(b_router)],
        out_specs=[row(d), row(d), row(LANES)],
        out_shape=[jax.ShapeDtypeStruct((n, d), F32), jax.ShapeDtypeStruct((n, d), BF16),
                   jax.ShapeDtypeStruct((n, LANES), F32)],
        compiler_params=pltpu.CompilerParams(dimension_semantics=("parallel",), vmem_limit_bytes=VMEM_LIMIT),
        name="outproj_router",
    )(x2d, o_a, *b_parts, w_out_perm, gain.reshape(1, d), w_router, b_router)


MOE_TM = 1024


MOE_SUB = 128
MOE_FINAL_TM = 512


def _moe_kernel(hn_ref, comb_ref, upper_ref, wg_ref, wu_ref, wd_ref, y_ref,
                perm_ref, hs_ref, cs_ref, ys_ref, start_ref, nsub_ref):
    grp = pl.program_id(1)
    tm, d = hn_ref.shape
    n_pad = perm_ref.shape[0]

    @pl.when(grp == 0)
    def _sort():
        comb = comb_ref[...]
        lane_f = lax.broadcasted_iota(jnp.int32, (tm, LANES), 1).astype(F32)
        onehot = jnp.where(lane_f == comb[:, C_GROUP:C_GROUP + 1], 1.0, 0.0).astype(BF16)
        eye8 = jnp.where(lax.broadcasted_iota(jnp.int32, (8, LANES), 0) == lax.broadcasted_iota(jnp.int32, (8, LANES), 1),
                         1.0, 0.0).astype(BF16)
        onehot_t = _nt_dot(eye8, onehot)
        cum_t = _dot(onehot_t.astype(BF16), upper_ref[...])
        start = jnp.zeros((1, 1), F32)
        pos_t = jnp.zeros((1, tm), F32)
        for k in range(N_GROUPS):
            padded = jnp.ceil(cum_t[k:k + 1, tm - 1:tm] * (1.0 / MOE_SUB)) * MOE_SUB
            pos_t = pos_t + onehot_t[k:k + 1, :] * (start + cum_t[k:k + 1, :] - 1.0)
            start_ref[k] = start[0, 0].astype(jnp.int32)
            nsub_ref[k] = (padded[0, 0] * (1.0 / MOE_SUB)).astype(jnp.int32)
            start = start + padded
        perm = jnp.where(lax.broadcasted_iota(jnp.int32, (n_pad, tm), 0) == pos_t.astype(jnp.int32), 1.0, 0.0).astype(BF16)
        perm_ref[...] = perm
        hs_ref[...] = _dot(perm, hn_ref[...]).astype(BF16)
        c_hi = comb.astype(BF16)
        c_lo = (comb - c_hi.astype(F32)).astype(BF16)
        both = _dot(perm, jnp.concatenate([c_hi, c_lo], axis=1))
        cs_ref[...] = both[:, 0:LANES] + both[:, LANES:2 * LANES]
        ys_ref[...] = jnp.zeros_like(ys_ref)

    lane = lax.broadcasted_iota(jnp.int32, (MOE_SUB, LANES), 1)

    def segment(s, carry):
        rows = pl.ds(pl.multiple_of(start_ref[grp] + s * MOE_SUB, MOE_SUB), MOE_SUB)
        x = hs_ref[rows, :]
        weights = cs_ref[rows, :]
        acc = jnp.zeros((MOE_SUB, d), F32)
        for j in range(EXPERTS_PER_GROUP):
            gate = _dot(x, wg_ref[j])
            up = _dot(x, wu_ref[j])
            w = jnp.sum(jnp.where(lane == grp * EXPERTS_PER_GROUP + j, weights, 0.0), axis=1, keepdims=True)
            acc = acc + _dot((gate * jax.nn.sigmoid(gate) * up * w).astype(BF16), wd_ref[j])
        ys_ref[rows, :] = acc.astype(BF16)
        return carry

    lax.fori_loop(0, nsub_ref[grp], segment, 0)

    @pl.when(grp == pl.num_programs(1) - 1)
    def _unsort():
        y_ref[...] = lax.dot_general(perm_ref[...], ys_ref[...], (((0,), (0,)), ((), ())),
                                     preferred_element_type=F32).astype(BF16)


def _final_kernel(h_ref, y_ref, g_ref, out_ref):
    y = h_ref[...] + y_ref[...].astype(F32)
    out_ref[...] = y * lax.rsqrt(jnp.mean(y * y, axis=-1, keepdims=True) + EPS) * g_ref[...]


def _moe(h, hn, comb, w_gate, w_up, w_down, gain):
    n, d = h.shape
    tm = min(MOE_TM, n)
    n_pad = tm + (N_GROUPS - 1) * MOE_SUB
    row = lambda w: pl.BlockSpec((tm, w), lambda i, g: (i, 0))
    group_w = lambda a: pl.BlockSpec((EXPERTS_PER_GROUP,) + a.shape[1:], lambda i, g: (g, 0, 0))
    upper = jnp.asarray(np.triu(np.ones((tm, tm), np.float32)), BF16)
    y = pl.pallas_call(
        _moe_kernel,
        grid=(n // tm, N_GROUPS),
        in_specs=[row(d), row(LANES), pl.BlockSpec((tm, tm), lambda i, g: (0, 0)),
                  group_w(w_gate), group_w(w_up), group_w(w_down)],
        out_specs=row(d),
        out_shape=jax.ShapeDtypeStruct((n, d), BF16),
        scratch_shapes=[pltpu.VMEM((n_pad, tm), BF16), pltpu.VMEM((n_pad, d), BF16), pltpu.VMEM((n_pad, LANES), F32),
                        pltpu.VMEM((n_pad, d), BF16), pltpu.SMEM((N_GROUPS,), jnp.int32), pltpu.SMEM((N_GROUPS,), jnp.int32)],
        compiler_params=pltpu.CompilerParams(dimension_semantics=("parallel", "arbitrary"), vmem_limit_bytes=VMEM_LIMIT),
        name="moe_experts",
    )(hn, comb, upper, w_gate, w_up, w_down)
    rowf = lambda w: pl.BlockSpec((MOE_FINAL_TM, w), lambda i: (i, 0))
    return pl.pallas_call(
        _final_kernel,
        grid=(n // MOE_FINAL_TM,),
        in_specs=[rowf(d), rowf(d), pl.BlockSpec((1, d), lambda i: (0, 0))],
        out_specs=rowf(d),
        out_shape=jax.ShapeDtypeStruct((n, d), F32),
        compiler_params=pltpu.CompilerParams(dimension_semantics=("parallel",), vmem_limit_bytes=VMEM_LIMIT),
        name="residual_final_norm",
    )(h, y, gain.reshape(1, d))


def _permute_w_out(w_out):
    d = w_out.shape[1]
    wb = w_out[DA:].reshape(NSA_KV_HEADS, NSA_GROUP, HEAD_DIM, d).transpose(1, 0, 2, 3).reshape(DB, d)
    return jnp.concatenate([w_out[:DA], wb], axis=0).astype(BF16)


def _router_weights(w_group, b_group, w_expert, b_expert):
    d = w_group.shape[0]
    w = jnp.concatenate([w_expert.reshape(d, N_EXPERTS), w_group], axis=1)
    b = jnp.concatenate([b_expert.reshape(N_EXPERTS), b_group])
    pad = LANES - w.shape[1]
    w = jnp.pad(w, ((0, 0), (0, pad))).astype(F32)
    w_hi = w.astype(BF16)
    w_lo = (w - w_hi.astype(F32)).astype(BF16)
    return jnp.concatenate([w_hi, w_lo], axis=1), jnp.pad(b, (0, pad)).reshape(1, LANES).astype(F32)


def _layer(h, rel_bias, norm_mix, w_in, w_out, cmp_pos_k, cmp_pos_v, cmp_k_w1, cmp_k_w2, cmp_v_w1, cmp_v_w2,
           norm_ffn, w_rg, b_rg, w_re, b_re, w_gate, w_up, w_down, out_gain):
    b, t, d = h.shape
    n = b * t
    assert t % (QB * DIL_PATTERNS[-1][1]) == 0 and t // SEL_BLOCK <= LANES and n % MOE_TM == 0
    x2d = h.reshape(n, d)
    seq = lambda a: a if a.ndim == 3 else a.reshape(b, a.shape[0] // b, a.shape[-1])
    qa, ka, va, qb, kc, vc, ks_t, vs0aug, vs1aug, kw, vw0aug, vw1aug, gates = map(
        seq, _inproj(x2d, norm_mix, _permute_w_in(w_in), t))
    f_a = _bias_1d(rel_bias[:, :DIL_HEADS])
    f_b = _bias_1d(rel_bias[:, DIL_HEADS:]) * LOG2E
    o_a = _mixer_a(qa, ka, va, jnp.stack([_dil_bias(f_a, dil) for _, dil in DIL_PATTERNS]))
    kcmp, vcmp = _compress(kc, vc, cmp_pos_k, cmp_pos_v, cmp_k_w1, cmp_k_w2, cmp_v_w1, cmp_v_w2)
    o_cmp, sel = _compressed_branch(qb, kcmp, vcmp, gates, f_b)
    o_sel = _selected_branch(qb, sel, gates, ks_t, vs0aug, vs1aug, f_b)
    o_win = _window_branch(qb, gates, kw, vw0aug, vw1aug, f_b)
    b_parts = [o.reshape(n, DB) for o in (o_cmp, o_sel, o_win)]
    w_router, b_router = _router_weights(w_rg, b_rg, w_re, b_re)
    h2, hn, comb = _outproj(x2d, o_a.reshape(n, DA), b_parts, _permute_w_out(w_out), norm_ffn, w_router, b_router)
    return _moe(h2, hn, comb, w_gate.astype(BF16), w_up.astype(BF16), w_down.astype(BF16), out_gain)


def kernel(x, rel_bias, norm_mix, w_in, w_out, cmp_pos_k, cmp_pos_v, cmp_k_w1, cmp_k_w2, cmp_v_w1, cmp_v_w2,
           norm_ffn, w_router_group, b_router_group, w_router_expert, b_router_expert, w_gate, w_up, w_down,
           norm_final):
    depth = norm_mix.shape[0]
    assert depth == 1, "the final RMSNorm is fused into the last layer's expert kernel"
    out = _layer(x, rel_bias, norm_mix[0], w_in[0], w_out[0], cmp_pos_k[0], cmp_pos_v[0], cmp_k_w1[0], cmp_k_w2[0],
                 cmp_v_w1[0], cmp_v_w2[0], norm_ffn[0], w_router_group[0], b_router_group[0], w_router_expert[0],
                 b_router_expert[0], w_gate[0], w_up[0], w_down[0], norm_final)
    return out.reshape(x.shape)
```
